```python
import math
import jax, jax.numpy as jnp
from jax import lax
import numpy as np

D_MODEL = 1024
BATCH = 4
SEQ = 4096
DEPTH = 4

N_MEM = 256
N_HEADS = 4
HEAD_DIM = 64
BRANCH_W = N_HEADS * HEAD_DIM
N_BRANCH = 5
ROPE_THETA = 500000.0
ROT_64 = 16
ROT_32 = 8
Q_BLOCK = 128
IDX_HEADS = 8
IDX_DIM = 32
TOPK_MAX = 256
MOBA_BLOCK = 256
MOBA_TOPK = 3
MOBA_Q_CHUNK = 64
DIFF_DIM = 32
Q_LORA = 256
KV_LORA = 128
MLA_NOPE = 64
MLA_ROPE = 32
MLA_V = 64
DN_ALPHA = (2 * DEPTH) ** 0.25
DN_BETA = (8 * DEPTH) ** -0.25
LN_EPS = 1e-5
RMS_EPS = 1e-6

IN_LAYOUT = (
    ("a_q", BRANCH_W), ("a_k", BRANCH_W), ("a_v", BRANCH_W),
    ("i_q", IDX_HEADS * IDX_DIM), ("i_k", IDX_DIM), ("i_w", IDX_HEADS),
    ("b_q", BRANCH_W), ("b_k", BRANCH_W), ("b_v", BRANCH_W),
    ("c_q", BRANCH_W), ("c_k", BRANCH_W), ("c_v", BRANCH_W),
    ("d_cq", Q_LORA), ("d_ckv", KV_LORA), ("d_kr", MLA_ROPE),
    ("e_q", BRANCH_W),
    ("z", N_BRANCH * BRANCH_W),
    ("g", N_BRANCH * D_MODEL),
)
IN_WIDTH = sum(s for _, s in IN_LAYOUT)

kernel_name = "hybrid_gated_dsa_moba_diff_mla_mem_deepnorm"


def split_projection(h):
    sizes = [s for _, s in IN_LAYOUT]
    offsets = [int(o) for o in np.cumsum(sizes)[:-1]]
    parts = jnp.split(h, offsets, axis=-1)
    return {name: p for (name, _), p in zip(IN_LAYOUT, parts)}


def layer_norm(x, g, b):
    xf = x.astype(jnp.float32)
    mu = jnp.mean(xf, -1, keepdims=True)
    var = jnp.mean(jnp.square(xf - mu), -1, keepdims=True)
    return ((xf - mu) * lax.rsqrt(var + LN_EPS) * g + b).astype(x.dtype)


def rms_norm(x, g):
    xf = x.astype(jnp.float32)
    return (xf * lax.rsqrt(jnp.mean(jnp.square(xf), -1, keepdims=True) + RMS_EPS) * g).astype(x.dtype)


def rope_tables(seq, rot_dim):
    pos = jnp.arange(seq, dtype=jnp.float32)
    inv = ROPE_THETA ** (-jnp.arange(0, rot_dim, 2, dtype=jnp.float32) / rot_dim)
    ang = pos[:, None] * inv[None, :]
    return jnp.cos(ang), jnp.sin(ang)


def apply_rope(x, cos, sin):
    half = x.shape[-1] // 2
    x1, x2 = x[..., :half], x[..., half:]
    c = cos[None, :, None, :].astype(x.dtype)
    s = sin[None, :, None, :].astype(x.dtype)
    return jnp.concatenate([x1 * c - x2 * s, x2 * c + x1 * s], -1)


def partial_rope(x, cos, sin):
    r = 2 * cos.shape[-1]
    return jnp.concatenate([apply_rope(x[..., :r], cos, sin), x[..., r:]], -1)


def to_blocks(x, blk):
    b, t = x.shape[:2]
    return jnp.moveaxis(x.reshape((b, t // blk, blk) + x.shape[2:]), 1, 0)


def from_blocks(y):
    y = jnp.moveaxis(y, 0, 1)
    return y.reshape((y.shape[0], -1) + y.shape[3:])


def sweep(fn, qs, blk):
    t = qs[0].shape[1]
    starts = jnp.arange(t // blk, dtype=jnp.int32) * blk
    out = lax.map(lambda a: fn(a[0], *a[1:]), (starts,) + tuple(to_blocks(q, blk) for q in qs))
    return from_blocks(out)


def dsa_attention(q, k, v, qi, ki, wi):
    b, t = q.shape[:2]
    topk = min(TOPK_MAX, t // 4)
    kpos = jnp.arange(t)
    bidx = jnp.arange(b)[:, None, None]
    scale = HEAD_DIM ** -0.5
    idx_scale = (IDX_HEADS * IDX_DIM) ** -0.5

    def block(start, qb, qib, wib):
        qpos = start + jnp.arange(Q_BLOCK)
        causal = kpos[None, :] <= qpos[:, None]
        logits = jnp.einsum('bqhd,bkd->bqhk', qib, ki)
        score = jnp.einsum('bqhk,bqh->bqk', jax.nn.relu(logits), wib).astype(jnp.float32) * idx_scale
        score = jnp.where(causal[None], score, -jnp.inf)
        _, sel = lax.top_k(score, topk)
        valid = sel <= qpos[None, :, None]
        kg = k[bidx, sel]
        vg = v[bidx, sel]
        s = jnp.einsum('bqhd,bqkhd->bqhk', qb, kg).astype(jnp.float32) * scale
        s = jnp.where(valid[:, :, None, :], s, -jnp.inf)
        p = jax.nn.softmax(s, axis=-1).astype(v.dtype)
        return jnp.einsum('bqhk,bqkhd->bqhd', p, vg)

    return sweep(block, (q, qi, wi), Q_BLOCK)


def moba_attention(q, k, v):
    b, t, h, dh = q.shape
    nb = -(-t // MOBA_BLOCK)
    n_sel = min(MOBA_TOPK, nb - 1)
    pad = nb * MOBA_BLOCK - t
    kp = jnp.pad(k, ((0, 0), (0, pad), (0, 0), (0, 0)))
    vp = jnp.pad(v, ((0, 0), (0, pad), (0, 0), (0, 0)))
    kb = kp.reshape(b, nb, MOBA_BLOCK, h, dh)
    vb = vp.reshape(b, nb, MOBA_BLOCK, h, dh)
    kbar = jnp.mean(kb, axis=2)
    kbh = jnp.moveaxis(kb, 3, 1)
    vbh = jnp.moveaxis(vb, 3, 1)
    bidx = jnp.arange(b)[:, None, None, None]
    hidx = jnp.arange(h)[None, None, :, None]
    blk_ids = jnp.arange(nb)
    own_off = jnp.arange(MOBA_BLOCK)
    scale = dh ** -0.5

    def block(start, qb):
        nq = qb.shape[1]
        qpos = start + jnp.arange(nq)
        own = start // MOBA_BLOCK
        k_own = lax.dynamic_slice_in_dim(kp, own * MOBA_BLOCK, MOBA_BLOCK, axis=1)
        v_own = lax.dynamic_slice_in_dim(vp, own * MOBA_BLOCK, MOBA_BLOCK, axis=1)
        own_pos = own * MOBA_BLOCK + own_off
        s_own = jnp.einsum('bqhd,bjhd->bqhj', qb, k_own).astype(jnp.float32) * scale
        s_own = jnp.where((own_pos[None, :] <= qpos[:, None])[None, :, None, :], s_own, -jnp.inf)
        if n_sel == 0:
            p = jax.nn.softmax(s_own, axis=-1).astype(v.dtype)
            return jnp.einsum('bqhj,bjhd->bqhd', p, v_own)
        gate = jnp.einsum('bqhd,bnhd->bqhn', qb, kbar).astype(jnp.float32)
        gate = jnp.where(blk_ids < own, gate, -jnp.inf)
        _, sel = lax.top_k(gate, n_sel)
        sel_valid = sel < own
        kg = kbh[bidx, hidx, sel]
        vg = vbh[bidx, hidx, sel]
        s_sel = jnp.einsum('bqhd,bqhsjd->bqhsj', qb, kg).astype(jnp.float32) * scale
        s_sel = jnp.where(sel_valid[..., None], s_sel, -jnp.inf).reshape(b, nq, h, n_sel * MOBA_BLOCK)
        p = jax.nn.softmax(jnp.concatenate([s_sel, s_own], -1), axis=-1).astype(v.dtype)
        p_sel = p[..., :n_sel * MOBA_BLOCK].reshape(b, nq, h, n_sel, MOBA_BLOCK)
        p_own = p[..., n_sel * MOBA_BLOCK:]
        return (jnp.einsum('bqhsj,bqhsjd->bqhd', p_sel, vg)
                + jnp.einsum('bqhj,bjhd->bqhd', p_own, v_own))

    return sweep(block, (q,), MOBA_Q_CHUNK)


def diff_attention(q, k, v, lam):
    t = q.shape[1]
    kpos = jnp.arange(t)
    scale = DIFF_DIM ** -0.5

    def block(start, qb):
        qpos = start + jnp.arange(Q_BLOCK)
        causal = kpos[None, :] <= qpos[:, None]
        s = jnp.einsum('bqhcd,bkhcd->bhcqk', qb, k).astype(jnp.float32) * scale
        p = jax.nn.softmax(jnp.where(causal, s, -jnp.inf), axis=-1)
        pd = (p[:, :, 0] - lam * p[:, :, 1]).astype(v.dtype)
        return jnp.einsum('bhqk,bkhd->bqhd', pd, v)

    return sweep(block, (q,), Q_BLOCK)


def mla_attention(qn, qr, kn, kr, v):
    t = qn.shape[1]
    kpos = jnp.arange(t)
    scale = (MLA_NOPE + MLA_ROPE) ** -0.5

    def block(start, qnb, qrb):
        qpos = start + jnp.arange(Q_BLOCK)
        causal = kpos[None, :] <= qpos[:, None]
        s = (jnp.einsum('bqhd,bkhd->bhqk', qnb, kn)
             + jnp.einsum('bqhr,bkr->bhqk', qrb, kr)).astype(jnp.float32) * scale
        p = jax.nn.softmax(jnp.where(causal, s, -jnp.inf), axis=-1).astype(v.dtype)
        return jnp.einsum('bhqk,bkhd->bqhd', p, v)

    return sweep(block, (qn, qr), Q_BLOCK)


def memory_attention(q, mk, mv):
    s = jnp.einsum('bthd,bmhd->bhtm', q, mk).astype(jnp.float32) * HEAD_DIM ** -0.5
    p = jax.nn.softmax(s, axis=-1).astype(mv.dtype)
    return jnp.einsum('bhtm,bmhd->bthd', p, mv)


def hybrid_layer(x, mem, rot64, rot32, rot_mla, layer_idx, w_in, mla_q_norm, w_uq, mla_kv_norm,
                 w_ukv, diff_lam, diff_norm, w_mem_kv, w_branch, w_out, ln_g, ln_b):
    b, t, d = x.shape
    pr = split_projection(x @ w_in)

    def heads(a, n, dd):
        return a.reshape(b, t, n, dd)

    aq = partial_rope(heads(pr['a_q'], N_HEADS, HEAD_DIM), *rot64)
    ak = partial_rope(heads(pr['a_k'], N_HEADS, HEAD_DIM), *rot64)
    av = heads(pr['a_v'], N_HEADS, HEAD_DIM)
    iq = partial_rope(heads(pr['i_q'], IDX_HEADS, IDX_DIM), *rot32)
    ik = partial_rope(pr['i_k'][:, :, None, :], *rot32)[:, :, 0]
    o_a = dsa_attention(aq, ak, av, iq, ik, pr['i_w'])

    bq = partial_rope(heads(pr['b_q'], N_HEADS, HEAD_DIM), *rot64)
    bk = partial_rope(heads(pr['b_k'], N_HEADS, HEAD_DIM), *rot64)
    o_b = moba_attention(bq, bk, heads(pr['b_v'], N_HEADS, HEAD_DIM))

    cq = partial_rope(heads(pr['c_q'], 2 * N_HEADS, DIFF_DIM), *rot32).reshape(b, t, N_HEADS, 2, DIFF_DIM)
    ck = partial_rope(heads(pr['c_k'], 2 * N_HEADS, DIFF_DIM), *rot32).reshape(b, t, N_HEADS, 2, DIFF_DIM)
    cv = heads(pr['c_v'], N_HEADS, 2 * DIFF_DIM)
    lam_init = 0.8 - 0.6 * math.exp(-0.3 * layer_idx)
    dl = diff_lam.astype(jnp.float32)
    lam = jnp.exp(jnp.sum(dl[0] * dl[1])) - jnp.exp(jnp.sum(dl[2] * dl[3])) + lam_init
    o_c = rms_norm(diff_attention(cq, ck, cv, lam), diff_norm) * (1.0 - lam_init)

    q_full = (rms_norm(pr['d_cq'], mla_q_norm) @ w_uq).reshape(b, t, N_HEADS, MLA_NOPE + MLA_ROPE)
    qn, qr = q_full[..., :MLA_NOPE], apply_rope(q_full[..., MLA_NOPE:], *rot_mla)
    kv = (rms_norm(pr['d_ckv'], mla_kv_norm) @ w_ukv).reshape(b, t, N_HEADS, MLA_NOPE + MLA_V)
    kn, dv = kv[..., :MLA_NOPE], kv[..., MLA_NOPE:]
    kr = apply_rope(pr['d_kr'][:, :, None, :], *rot_mla)[:, :, 0]
    o_d = mla_attention(qn, qr, kn, kr, dv)

    mkv = (mem @ w_mem_kv).reshape(b, mem.shape[1], 2, N_HEADS, HEAD_DIM)
    o_e = memory_attention(heads(pr['e_q'], N_HEADS, HEAD_DIM), mkv[:, :, 0], mkv[:, :, 1])

    o = jnp.stack([o_a, o_b, o_c, o_d, o_e], axis=2).reshape(b, t, N_BRANCH, BRANCH_W)
    y = o * jax.nn.silu(pr['z'].reshape(b, t, N_BRANCH, BRANCH_W))
    u = jnp.einsum('btnc,ncd->btnd', y, w_branch)
    g = jax.nn.sigmoid(pr['g'].reshape(b, t, N_BRANCH, d))
    out = jnp.sum(g * u, axis=2) @ w_out
    return layer_norm(DN_ALPHA * x + out, ln_g, ln_b)


def setup_inputs(seed: int = 0) -> dict:
    key = jax.random.key(seed)
    ks = jax.random.split(key, 16)
    f32 = jnp.float32

    def nrm(k, shape, scale):
        return jax.random.normal(k, shape, f32) * scale

    return {
        "x": nrm(ks[0], (BATCH, SEQ, D_MODEL), 1.0),
        "mem": nrm(ks[1], (BATCH, N_MEM, D_MODEL), 1.0),
        "ln0_g": 1.0 + nrm(ks[2], (D_MODEL,), 0.02),
        "ln0_b": nrm(ks[3], (D_MODEL,), 0.02),
        "w_in": nrm(ks[4], (DEPTH, D_MODEL, IN_WIDTH), D_MODEL ** -0.5),
        "mla_q_norm": 1.0 + nrm(ks[5], (DEPTH, Q_LORA), 0.02),
        "w_uq": nrm(ks[6], (DEPTH, Q_LORA, N_HEADS * (MLA_NOPE + MLA_ROPE)), Q_LORA ** -0.5),
        "mla_kv_norm": 1.0 + nrm(ks[7], (DEPTH, KV_LORA), 0.02),
        "w_ukv": nrm(ks[8], (DEPTH, KV_LORA, N_HEADS * (MLA_NOPE + MLA_V)), KV_LORA ** -0.5),
        "diff_lam": nrm(ks[9], (DEPTH, 4, DIFF_DIM), 0.1),
        "diff_norm": 1.0 + nrm(ks[10], (DEPTH, 2 * DIFF_DIM), 0.02),
        "w_mem_kv": nrm(ks[11], (DEPTH, D_MODEL, 2 * BRANCH_W), D_MODEL ** -0.5),
        "w_branch": nrm(ks[12], (DEPTH, N_BRANCH, BRANCH_W, D_MODEL), BRANCH_W ** -0.5 * DN_BETA),
        "w_out": nrm(ks[13], (DEPTH, D_MODEL, D_MODEL), D_MODEL ** -0.5 * DN_BETA),
        "ln_g": 1.0 + nrm(ks[14], (DEPTH, D_MODEL), 0.02),
        "ln_b": nrm(ks[15], (DEPTH, D_MODEL), 0.02),
    }


def reference(x, mem, ln0_g, ln0_b, w_in, mla_q_norm, w_uq, mla_kv_norm, w_ukv, diff_lam,
              diff_norm, w_mem_kv, w_branch, w_out, ln_g, ln_b):
    t = x.shape[1]
    rot64 = rope_tables(t, ROT_64)
    rot32 = rope_tables(t, ROT_32)
    rot_mla = rope_tables(t, MLA_ROPE)
    h = layer_norm(x, ln0_g, ln0_b)
    for l in range(DEPTH):
        h = hybrid_layer(h, mem, rot64, rot32, rot_mla, l, w_in[l], mla_q_norm[l], w_uq[l],
                         mla_kv_norm[l], w_ukv[l], diff_lam[l], diff_norm[l], w_mem_kv[l],
                         w_branch[l], w_out[l], ln_g[l], ln_b[l])
    return h
```

```python
import functools
import math

import numpy as np
import jax
import jax.numpy as jnp
from jax import lax
from jax.experimental import pallas as pl
from jax.experimental.pallas import tpu as pltpu

F32 = jnp.float32
BF16 = jnp.bfloat16
I32 = jnp.int32

N_HEADS = 4
HEAD_DIM = 64
BRANCH_W = N_HEADS * HEAD_DIM
N_BRANCH = 5
ROPE_THETA = 500000.0
ROT_64 = 16
ROT_32 = 8
IDX_HEADS = 8
IDX_DIM = 32
TOPK_MAX = 256
MOBA_BLOCK = 256
MOBA_TOPK = 3
DIFF_DIM = 32
Q_LORA = 256
KV_LORA = 128
MLA_NOPE = 64
MLA_ROPE = 32
MLA_V = 64
LN_EPS = 1e-5
RMS_EPS = 1e-6

IN_LAYOUT = (
    ("a_q", BRANCH_W), ("a_k", BRANCH_W), ("a_v", BRANCH_W),
    ("i_q", IDX_HEADS * IDX_DIM), ("i_k", IDX_DIM), ("i_w", IDX_HEADS),
    ("b_q", BRANCH_W), ("b_k", BRANCH_W), ("b_v", BRANCH_W),
    ("c_q", BRANCH_W), ("c_k", BRANCH_W), ("c_v", BRANCH_W),
    ("d_cq", Q_LORA), ("d_ckv", KV_LORA), ("d_kr", MLA_ROPE),
    ("e_q", BRANCH_W),
    ("z", N_BRANCH * BRANCH_W),
    ("g", N_BRANCH * 1024),
)

LANES = 128
MXU_N = 256
TQ = 256
CK = 256
NEG = -1e30
INT_MIN = np.int32(-2 ** 31)
VMEM_LIMIT = 56 * 1024 * 1024


def _offsets():
    off, out = 0, {}
    for name, size in IN_LAYOUT:
        out[name] = (off, size)
        off += size
    return out


OFF = _offsets()


def _nt_dot(a, b):
    return lax.dot_general(a, b, (((1,), (1,)), ((), ())), preferred_element_type=F32)


def _rep(x, n):
    return x if n == 1 else jnp.concatenate([x] * n, axis=1)


def _fold(w):
    out = w[:, :LANES]
    for j in range(1, w.shape[1] // LANES):
        out = out + w[:, j * LANES:(j + 1) * LANES]
    return out


def _cparams(n_axes):
    return pltpu.CompilerParams(dimension_semantics=("arbitrary",) * n_axes,
                                vmem_limit_bytes=VMEM_LIMIT)


def _softmax_step(s, msk, vc, m_ref, l_ref, acc_ref):
    if msk is not None:
        s = jnp.where(msk, s, NEG)
    m_old = m_ref[...]
    m_new = jnp.maximum(m_old, jnp.max(s, axis=1, keepdims=True))
    alpha = jnp.exp(m_old - m_new)
    p = jnp.exp(s - m_new)
    if msk is not None:
        p = jnp.where(msk, p, 0.0)
    l_ref[...] = alpha * l_ref[...] + jnp.sum(p, axis=1, keepdims=True)
    acc_ref[...] = alpha * acc_ref[...] + jnp.dot(p.astype(BF16), vc, preferred_element_type=F32)
    m_ref[...] = m_new


def _softmax_init(m_ref, l_ref, acc_ref):
    m_ref[...] = jnp.full(m_ref.shape, NEG, F32)
    l_ref[...] = jnp.zeros(l_ref.shape, F32)
    acc_ref[...] = jnp.zeros(acc_ref.shape, F32)


def _ln_kernel(x_ref, g_ref, b_ref, h_ref, hb_ref):
    x = x_ref[...]
    mu = jnp.mean(x, axis=1, keepdims=True)
    xc = x - mu
    var = jnp.mean(xc * xc, axis=1, keepdims=True)
    y = xc * lax.rsqrt(var + LN_EPS) * g_ref[...] + b_ref[...]
    h_ref[...] = y
    hb_ref[...] = y.astype(BF16)


def _layer_norm0(x2, g, b):
    n, d = x2.shape
    tm = 512
    row = pl.BlockSpec((tm, d), lambda i: (i, 0))
    vec = pl.BlockSpec((1, d), lambda i: (0, 0))
    return pl.pallas_call(
        _ln_kernel,
        out_shape=(jax.ShapeDtypeStruct((n, d), F32), jax.ShapeDtypeStruct((n, d), BF16)),
        grid=(n // tm,),
        in_specs=[row, vec, vec],
        out_specs=(row, row),
        compiler_params=_cparams(1),
        name="ln0",
    )(x2, g.reshape(1, d), b.reshape(1, d))


def _proj_plain_kernel(x_ref, w_ref, *out_refs):
    off = 0
    for o_ref in out_refs:
        wd = o_ref.shape[-1]
        for j in range(0, wd, MXU_N):
            acc = jnp.dot(x_ref[...], w_ref[:, off + j:off + j + MXU_N], preferred_element_type=F32)
            o_ref[:, j:j + MXU_N] = acc.astype(o_ref.dtype)
        off += wd


def _proj_plain(hb3, w, widths):
    b, t, d = hb3.shape
    tm = 512
    ncol = w.shape[1]
    return pl.pallas_call(
        _proj_plain_kernel,
        out_shape=tuple(jax.ShapeDtypeStruct((b, t, wd), BF16) for wd in widths),
        grid=(t // tm, b),
        in_specs=[pl.BlockSpec((None, tm, d), lambda i, bb: (bb, i, 0)),
                  pl.BlockSpec((d, ncol), lambda i, bb: (0, 0))],
        out_specs=tuple(pl.BlockSpec((None, tm, wd), lambda i, bb: (bb, i, 0)) for wd in widths),
        compiler_params=_cparams(2),
        name="proj_plain",
    )(hb3, w)


def _proj_rope_kernel(x_ref, w_ref, wr_ref, c_ref, s_ref, *out_refs):
    off = 0
    for o_ref in out_refs:
        wd = o_ref.shape[-1]
        for j in range(0, wd, MXU_N):
            sl = slice(off + j, off + j + MXU_N)
            acc = jnp.dot(x_ref[...], w_ref[:, sl], preferred_element_type=F32)
            rot = jnp.dot(x_ref[...], wr_ref[:, sl], preferred_element_type=F32)
            o_ref[:, j:j + MXU_N] = (acc * c_ref[:, sl] + rot * s_ref[:, sl]).astype(o_ref.dtype)
        off += wd


def _proj_rope(hb3, w, wr, ctab, stab, widths):
    b, t, d = hb3.shape
    tm = 256
    ncol = w.shape[1]
    wspec = pl.BlockSpec((d, ncol), lambda i, bb: (0, 0))
    tspec = pl.BlockSpec((tm, ncol), lambda i, bb: (i, 0))
    return pl.pallas_call(
        _proj_rope_kernel,
        out_shape=tuple(jax.ShapeDtypeStruct((b, t, wd), BF16) for wd in widths),
        grid=(t // tm, b),
        in_specs=[pl.BlockSpec((None, tm, d), lambda i, bb: (bb, i, 0)), wspec, wspec, tspec, tspec],
        out_specs=tuple(pl.BlockSpec((None, tm, wd), lambda i, bb: (bb, i, 0)) for wd in widths),
        compiler_params=_cparams(2),
        name="proj_rope",
    )(hb3, w, wr, ctab, stab)


def _dsa_kernel(aq_ref, ak_ref, av_ref, iq_ref, ik_ref, iw_ref, o_ref,
                keys_ref, iqm_ref, aqm_ref, wb_ref, thr_ref, m_ref, l_ref, acc_ref, os_ref,
                *, topk, idx_scale, pos_bits):
    i = pl.program_id(1)
    nk = i + 1
    rep = CK // LANES
    lane_q = lax.broadcasted_iota(I32, (TQ, BRANCH_W), 1)
    row = lax.broadcasted_iota(I32, (TQ, CK), 0)
    col = lax.broadcasted_iota(I32, (TQ, CK), 1)

    iq = iq_ref[...].astype(F32)
    for hh in range(IDX_HEADS):
        iqm_ref[hh] = jnp.where((lane_q >> 5) == hh, iq, 0.0).astype(BF16)
        wcol = iw_ref[:, KV_LORA + hh:KV_LORA + hh + 1].astype(F32)
        wb_ref[hh] = jnp.broadcast_to(wcol, (TQ, LANES))
    aq = aq_ref[...].astype(F32) * (HEAD_DIM ** -0.5)
    for h in range(N_HEADS):
        aqm_ref[h] = jnp.where((lane_q >> 6) == h, aq, 0.0).astype(BF16)

    def score_body(c, carry):
        kc = ik_ref[pl.ds(pl.multiple_of(c * CK, CK), CK), :]
        acc = jnp.zeros((TQ, CK), F32)
        for hh in range(IDX_HEADS):
            logit = _nt_dot(iqm_ref[hh], kc)
            acc = acc + jnp.maximum(logit, 0.0) * _rep(wb_ref[hh], rep)
        sc = acc * idx_scale
        bits = pltpu.bitcast(sc, I32)
        key = jnp.where(bits < 0, INT_MIN - bits, bits)
        causal = (c * CK + col) <= (i * TQ + row)
        keys_ref[c] = jnp.where(causal, key, INT_MIN)
        return carry

    lax.fori_loop(0, nk, score_body, 0)

    def count(pred):
        def body(c, part):
            return part + _fold(jnp.where(pred(keys_ref[c], c), 1.0, 0.0))
        part = lax.fori_loop(0, nk, body, jnp.zeros((TQ, LANES), F32))
        return jnp.sum(part, axis=1, keepdims=True)

    def bit_body(bi, t_u):
        c_u = t_u | jnp.left_shift(jnp.int32(1), 31 - bi)
        ckey = _rep(jnp.broadcast_to(c_u ^ INT_MIN, (TQ, LANES)), rep)
        cnt = count(lambda k, c: k >= ckey)
        return jnp.where(cnt >= float(topk), c_u, t_u)

    t_u = lax.fori_loop(0, 32, bit_body, jnp.zeros((TQ, 1), I32))
    thr = t_u ^ INT_MIN
    thr_b = _rep(jnp.broadcast_to(thr, (TQ, LANES)), rep)

    n_gt = count(lambda k, c: k > thr_b)
    n_eq = count(lambda k, c: k == thr_b)
    need = float(topk) - n_gt
    amb = jnp.logical_and(n_eq > need, thr > INT_MIN)
    any_amb = jnp.max(jnp.where(amb, 1.0, 0.0)) > 0.5

    @pl.when(any_amb)
    def _():
        def pos_body(bi, r):
            cand = r + jnp.left_shift(jnp.int32(1), pos_bits - 1 - bi)
            cand_b = _rep(jnp.broadcast_to(cand, (TQ, LANES)), rep)
            cnt = count(lambda k, c: jnp.logical_and(k == thr_b, (c * CK + col) < cand_b))
            return jnp.where(cnt < need, cand, r)

        r = lax.fori_loop(0, pos_bits, pos_body, jnp.zeros((TQ, 1), I32))
        r_b = _rep(jnp.broadcast_to(jnp.where(amb, r, jnp.int32(2 ** 30)), (TQ, LANES)), rep)

        def drop_body(c, carry):
            k = keys_ref[c]
            drop = jnp.logical_and(k == thr_b, (c * CK + col) > r_b)
            keys_ref[c] = jnp.where(drop, INT_MIN, k)
            return carry

        lax.fori_loop(0, nk, drop_body, 0)

    thr_ref[...] = jnp.broadcast_to(jnp.maximum(thr, INT_MIN + 1), (TQ, LANES))

    for h in range(N_HEADS):
        _softmax_init(m_ref, l_ref, acc_ref)

        def att_body(c, carry, h=h):
            start = pl.multiple_of(c * CK, CK)
            kc = ak_ref[pl.ds(start, CK), :]
            vc = av_ref[pl.ds(start, CK), :]
            s = _nt_dot(aqm_ref[h], kc)
            msk = keys_ref[c] >= _rep(thr_ref[...], rep)
            _softmax_step(s, msk, vc, m_ref, l_ref, acc_ref)
            return carry

        lax.fori_loop(0, nk, att_body, 0)
        o_h = acc_ref[...] / l_ref[...]
        if h == 0:
            os_ref[...] = o_h
        else:
            os_ref[...] = jnp.where((lane_q >> 6) == h, o_h, os_ref[...])
    o_ref[...] = os_ref[...].astype(o_ref.dtype)


def _dsa(aq, ak, av, iq, ik, iw):
    b, t, _ = aq.shape
    topk = min(TOPK_MAX, t // 4)
    qspec = pl.BlockSpec((None, TQ, BRANCH_W), lambda bb, i: (bb, i, 0))
    kspec = pl.BlockSpec((None, t, BRANCH_W), lambda bb, i: (bb, 0, 0))
    kern = functools.partial(_dsa_kernel, topk=topk, idx_scale=(IDX_HEADS * IDX_DIM) ** -0.5,
                             pos_bits=max(1, (t - 1).bit_length()))
    return pl.pallas_call(
        kern,
        out_shape=jax.ShapeDtypeStruct((b, t, BRANCH_W), BF16),
        grid=(b, t // TQ),
        in_specs=[qspec, kspec, kspec, qspec, kspec, qspec],
        out_specs=qspec,
        scratch_shapes=[
            pltpu.VMEM((t // CK, TQ, CK), I32),
            pltpu.VMEM((IDX_HEADS, TQ, BRANCH_W), BF16),
            pltpu.VMEM((N_HEADS, TQ, BRANCH_W), BF16),
            pltpu.VMEM((IDX_HEADS, TQ, LANES), F32),
            pltpu.VMEM((TQ, LANES), I32),
            pltpu.VMEM((TQ, 1), F32), pltpu.VMEM((TQ, 1), F32),
            pltpu.VMEM((TQ, BRANCH_W), F32), pltpu.VMEM((TQ, BRANCH_W), F32),
        ],
        compiler_params=_cparams(2),
        name="dsa",
    )(aq, ak, av, iq, ik, iw)


def _kbar_kernel(k_ref, o_ref):
    o_ref[...] = jnp.zeros(o_ref.shape, o_ref.dtype)
    nb = k_ref.shape[0] // MOBA_BLOCK
    for n in range(nb):
        blk = k_ref[n * MOBA_BLOCK:(n + 1) * MOBA_BLOCK, :].astype(F32)
        o_ref[n:n + 1, :] = jnp.mean(blk, axis=0, keepdims=True).astype(o_ref.dtype)


def _kbar(bk):
    b, t, w = bk.shape
    return pl.pallas_call(
        _kbar_kernel,
        out_shape=jax.ShapeDtypeStruct((b, LANES, w), BF16),
        grid=(b,),
        in_specs=[pl.BlockSpec((None, t, w), lambda bb: (bb, 0, 0))],
        out_specs=pl.BlockSpec((None, LANES, w), lambda bb: (bb, 0, 0)),
        compiler_params=_cparams(1),
        name="moba_kbar",
    )(bk)


def _moba_kernel(q_ref, k_ref, v_ref, kbar_ref, o_ref, qm_ref, sel_ref, m_ref, l_ref, acc_ref, os_ref):
    i = pl.program_id(1)
    lane_q = lax.broadcasted_iota(I32, (TQ, BRANCH_W), 1)
    lane = lax.broadcasted_iota(I32, (TQ, LANES), 1)
    lane_f = lane.astype(F32)
    row = lax.broadcasted_iota(I32, (TQ, CK), 0)
    col = lax.broadcasted_iota(I32, (TQ, CK), 1)
    q = q_ref[...].astype(F32) * (HEAD_DIM ** -0.5)
    for h in range(N_HEADS):
        qm_ref[h] = jnp.where((lane_q >> 6) == h, q, 0.0).astype(BF16)

    for h in range(N_HEADS):
        gate = _nt_dot(qm_ref[h], kbar_ref[...])
        g = jnp.where(lane < i, gate, NEG)
        sel = jnp.zeros((TQ, LANES), F32)
        for _ in range(MOBA_TOPK):
            mx = jnp.max(g, axis=1, keepdims=True)
            first = jnp.min(jnp.where(g == mx, lane_f, 1e9), axis=1, keepdims=True)
            pick = jnp.logical_and(lane_f == first, mx > 0.5 * NEG)
            sel = jnp.where(pick, 1.0, sel)
            g = jnp.where(pick, NEG, g)
        sel_ref[...] = sel

        _softmax_init(m_ref, l_ref, acc_ref)

        def body(c, carry, h=h):
            start = pl.multiple_of(c * CK, CK)
            s = _nt_dot(qm_ref[h], k_ref[pl.ds(start, CK), :])
            hit = jnp.sum(jnp.where(lane == c, sel_ref[...], 0.0), axis=1, keepdims=True) > 0.5
            msk = jnp.broadcast_to(hit, (TQ, CK))
            _softmax_step(s, msk, v_ref[pl.ds(start, CK), :], m_ref, l_ref, acc_ref)
            return carry

        lax.fori_loop(0, i, body, 0)
        start = pl.multiple_of(i * CK, CK)
        s = _nt_dot(qm_ref[h], k_ref[pl.ds(start, CK), :])
        _softmax_step(s, col <= row, v_ref[pl.ds(start, CK), :], m_ref, l_ref, acc_ref)
        o_h = acc_ref[...] / l_ref[...]
        if h == 0:
            os_ref[...] = o_h
        else:
            os_ref[...] = jnp.where((lane_q >> 6) == h, o_h, os_ref[...])
    o_ref[...] = os_ref[...].astype(o_ref.dtype)


def _moba(bq, bk, bv, kbar):
    b, t, w = bq.shape
    assert TQ == MOBA_BLOCK and CK == MOBA_BLOCK and t % MOBA_BLOCK == 0 and t // MOBA_BLOCK <= LANES
    qspec = pl.BlockSpec((None, TQ, w), lambda bb, i: (bb, i, 0))
    kspec = pl.BlockSpec((None, t, w), lambda bb, i: (bb, 0, 0))
    return pl.pallas_call(
        _moba_kernel,
        out_shape=jax.ShapeDtypeStruct((b, t, w), BF16),
        grid=(b, t // TQ),
        in_specs=[qspec, kspec, kspec, pl.BlockSpec((None, LANES, w), lambda bb, i: (bb, 0, 0))],
        out_specs=qspec,
        scratch_shapes=[
            pltpu.VMEM((N_HEADS, TQ, w), BF16), pltpu.VMEM((TQ, LANES), F32),
            pltpu.VMEM((TQ, 1), F32), pltpu.VMEM((TQ, 1), F32),
            pltpu.VMEM((TQ, w), F32), pltpu.VMEM((TQ, w), F32),
        ],
        compiler_params=_cparams(2),
        name="moba",
    )(bq, bk, bv, kbar)


def _diff_kernel(q_ref, k_ref, v_ref, lam_ref, norm_ref, misc_ref, o_ref,
                 qm_ref, m_ref, l_ref, acc_ref, os_ref):
    i = pl.program_id(1)
    scale = DIFF_DIM ** -0.5
    lane_q = lax.broadcasted_iota(I32, (TQ, BRANCH_W), 1)
    row = lax.broadcasted_iota(I32, (TQ, CK), 0)
    col = lax.broadcasted_iota(I32, (TQ, CK), 1)
    q = q_ref[...].astype(F32)
    for j in range(2 * N_HEADS):
        qm_ref[j] = jnp.where((lane_q >> 5) == j, q, 0.0).astype(BF16)

    dl = lam_ref[...]
    lam_init = misc_ref[0:1, 0:1]
    lam = (jnp.exp(jnp.sum(dl[0:1, :] * dl[1:2, :], axis=1, keepdims=True))
           - jnp.exp(jnp.sum(dl[2:3, :] * dl[3:4, :], axis=1, keepdims=True)) + lam_init)

    for h in range(N_HEADS):
        for c2 in range(2):
            _softmax_init(m_ref.at[c2], l_ref.at[c2], acc_ref.at[c2])

        def step(start, msk, h=h):
            kc = k_ref[pl.ds(start, CK), :]
            vc = v_ref[pl.ds(start, CK), :]
            for c2 in range(2):
                s = _nt_dot(qm_ref[2 * h + c2], kc) * scale
                _softmax_step(s, msk, vc, m_ref.at[c2], l_ref.at[c2], acc_ref.at[c2])

        def body(c, carry):
            step(pl.multiple_of(c * CK, CK), None)
            return carry

        lax.fori_loop(0, i, body, 0)
        step(pl.multiple_of(i * CK, CK), col <= row)
        o_h = acc_ref[0] / l_ref[0] - lam * (acc_ref[1] / l_ref[1])
        if h == 0:
            os_ref[...] = o_h
        else:
            os_ref[...] = jnp.where((lane_q >> 6) == h, o_h, os_ref[...])

    o = os_ref[...]
    o2 = o * o
    ms = jnp.zeros((TQ, BRANCH_W), F32)
    for h in range(N_HEADS):
        in_h = (lane_q >> 6) == h
        ms = jnp.where(in_h, jnp.sum(jnp.where(in_h, o2, 0.0), axis=1, keepdims=True), ms)
    o = o * lax.rsqrt(ms * (1.0 / HEAD_DIM) + RMS_EPS) * norm_ref[...] * (1.0 - lam_init)
    o_ref[...] = o.astype(o_ref.dtype)


def _diff(cq, ck, cv, lam, norm, misc):
    b, t, w = cq.shape
    qspec = pl.BlockSpec((None, TQ, w), lambda bb, i: (bb, i, 0))
    kspec = pl.BlockSpec((None, t, w), lambda bb, i: (bb, 0, 0))
    full = lambda a: pl.BlockSpec(a.shape, lambda bb, i: (0,) * a.ndim)
    return pl.pallas_call(
        _diff_kernel,
        out_shape=jax.ShapeDtypeStruct((b, t, w), BF16),
        grid=(b, t // TQ),
        in_specs=[qspec, kspec, kspec, full(lam), full(norm), full(misc)],
        out_specs=qspec,
        scratch_shapes=[
            pltpu.VMEM((2 * N_HEADS, TQ, w), BF16),
            pltpu.VMEM((2, TQ, 1), F32), pltpu.VMEM((2, TQ, 1), F32),
            pltpu.VMEM((2, TQ, w), F32), pltpu.VMEM((TQ, w), F32),
        ],
        compiler_params=_cparams(2),
        name="diff",
    )(cq, ck, cv, lam, norm, misc)


def _mla_prep_kernel(cq_ref, ckv_ref, kr_ref, qn_ref, kvn_ref, wq_ref, wqr_ref, wk_ref, wv_ref,
                     p_ref, ct_ref, st_ref, q_out, k_out, v_out):
    x = cq_ref[...].astype(F32)
    xn = (x * lax.rsqrt(jnp.mean(x * x, axis=1, keepdims=True) + RMS_EPS) * qn_ref[...]).astype(BF16)
    q = (jnp.dot(xn, wq_ref[...], preferred_element_type=F32) * ct_ref[...]
         + jnp.dot(xn, wqr_ref[...], preferred_element_type=F32) * st_ref[...])
    q_out[...] = q.astype(q_out.dtype)
    c = ckv_ref[:, :KV_LORA].astype(F32)
    cn = (c * lax.rsqrt(jnp.mean(c * c, axis=1, keepdims=True) + RMS_EPS) * kvn_ref[...]).astype(BF16)
    k = (jnp.dot(cn, wk_ref[...], preferred_element_type=F32)
         + jnp.dot(kr_ref[...], p_ref[...], preferred_element_type=F32))
    k_out[...] = k.astype(k_out.dtype)
    v_out[...] = jnp.dot(cn, wv_ref[...], preferred_element_type=F32).astype(v_out.dtype)


def _mla_prep(dcq, ckv, kr, qn, kvn, wq, wqr, wk, wv, pmat, ct, st):
    b, t, _ = dcq.shape
    tm = 512
    hw = N_HEADS * LANES
    row = lambda w: pl.BlockSpec((None, tm, w), lambda i, bb: (bb, i, 0))
    full = lambda a: pl.BlockSpec(a.shape, lambda i, bb: (0,) * a.ndim)
    tab = pl.BlockSpec((tm, hw), lambda i, bb: (i, 0))
    return pl.pallas_call(
        _mla_prep_kernel,
        out_shape=(jax.ShapeDtypeStruct((b, t, hw), BF16), jax.ShapeDtypeStruct((b, t, hw), BF16),
                   jax.ShapeDtypeStruct((b, t, BRANCH_W), BF16)),
        grid=(t // tm, b),
        in_specs=[row(Q_LORA), row(MXU_N), row(MXU_N), full(qn), full(kvn), full(wq), full(wqr),
                  full(wk), full(wv), full(pmat), tab, tab],
        out_specs=(row(hw), row(hw), row(BRANCH_W)),
        compiler_params=_cparams(2),
        name="mla_prep",
    )(dcq, ckv, kr, qn, kvn, wq, wqr, wk, wv, pmat, ct, st)


def _mla_kernel(q_ref, k_ref, v_ref, o_ref, m_ref, l_ref, acc_ref, os_ref):
    i = pl.program_id(1)
    scale = (MLA_NOPE + MLA_ROPE) ** -0.5
    lane_q = lax.broadcasted_iota(I32, (TQ, BRANCH_W), 1)
    row = lax.broadcasted_iota(I32, (TQ, CK), 0)
    col = lax.broadcasted_iota(I32, (TQ, CK), 1)
    for h in range(N_HEADS):
        hs = slice(h * LANES, (h + 1) * LANES)
        _softmax_init(m_ref, l_ref, acc_ref)

        def step(start, msk, hs=hs):
            s = _nt_dot(q_ref[:, hs], k_ref[pl.ds(start, CK), hs]) * scale
            _softmax_step(s, msk, v_ref[pl.ds(start, CK), :], m_ref, l_ref, acc_ref)

        def body(c, carry):
            step(pl.multiple_of(c * CK, CK), None)
            return carry

        lax.fori_loop(0, i, body, 0)
        step(pl.multiple_of(i * CK, CK), col <= row)
        o_h = acc_ref[...] / l_ref[...]
        if h == 0:
            os_ref[...] = o_h
        else:
            os_ref[...] = jnp.where((lane_q >> 6) == h, o_h, os_ref[...])
    o_ref[...] = os_ref[...].astype(o_ref.dtype)


def _mla(qm, km, vm):
    b, t, hw = qm.shape
    w = vm.shape[-1]
    return pl.pallas_call(
        _mla_kernel,
        out_shape=jax.ShapeDtypeStruct((b, t, w), BF16),
        grid=(b, t // TQ),
        in_specs=[pl.BlockSpec((None, TQ, hw), lambda bb, i: (bb, i, 0)),
                  pl.BlockSpec((None, t, hw), lambda bb, i: (bb, 0, 0)),
                  pl.BlockSpec((None, t, w), lambda bb, i: (bb, 0, 0))],
        out_specs=pl.BlockSpec((None, TQ, w), lambda bb, i: (bb, i, 0)),
        scratch_shapes=[pltpu.VMEM((TQ, 1), F32), pltpu.VMEM((TQ, 1), F32),
                        pltpu.VMEM((TQ, w), F32), pltpu.VMEM((TQ, w), F32)],
        compiler_params=_cparams(2),
        name="mla",
    )(qm, km, vm)


def _matmul_kernel(x_ref, w_ref, o_ref):
    o_ref[...] = jnp.dot(x_ref[...].astype(BF16), w_ref[...], preferred_element_type=F32).astype(o_ref.dtype)


def _mem_kv(mem, w):
    b, m, d = mem.shape
    n = w.shape[1]
    return pl.pallas_call(
        _matmul_kernel,
        out_shape=jax.ShapeDtypeStruct((b, m, n), BF16),
        grid=(b,),
        in_specs=[pl.BlockSpec((None, m, d), lambda bb: (bb, 0, 0)), pl.BlockSpec((d, n), lambda bb: (0, 0))],
        out_specs=pl.BlockSpec((None, m, n), lambda bb: (bb, 0, 0)),
        compiler_params=_cparams(1),
        name="mem_kv",
    )(mem, w)


def _mem_kernel(q_ref, kv_ref, o_ref):
    tq = q_ref.shape[0]
    lane_q = lax.broadcasted_iota(I32, (tq, BRANCH_W), 1)
    q = q_ref[...].astype(F32) * (HEAD_DIM ** -0.5)
    mk = kv_ref[:, :BRANCH_W]
    mv = kv_ref[:, BRANCH_W:]
    out = jnp.zeros((tq, BRANCH_W), F32)
    for h in range(N_HEADS):
        in_h = (lane_q >> 6) == h
        s = _nt_dot(jnp.where(in_h, q, 0.0).astype(BF16), mk)
        p = jnp.exp(s - jnp.max(s, axis=1, keepdims=True))
        o_h = jnp.dot(p.astype(BF16), mv, preferred_element_type=F32) / jnp.sum(p, axis=1, keepdims=True)
        out = jnp.where(in_h, o_h, out)
    o_ref[...] = out.astype(o_ref.dtype)


def _mem_attn(eq, mkv):
    b, t, w = eq.shape
    m = mkv.shape[1]
    tq = 512
    return pl.pallas_call(
        _mem_kernel,
        out_shape=jax.ShapeDtypeStruct((b, t, w), BF16),
        grid=(b, t // tq),
        in_specs=[pl.BlockSpec((None, tq, w), lambda bb, i: (bb, i, 0)),
                  pl.BlockSpec((None, m, 2 * w), lambda bb, i: (bb, 0, 0))],
        out_specs=pl.BlockSpec((None, tq, w), lambda bb, i: (bb, i, 0)),
        compiler_params=_cparams(2),
        name="mem_attn",
    )(eq, mkv)


def _final_kernel(h_ref, hb_ref, oa_ref, ob_ref, oc_ref, od_ref, oe_ref, z_ref,
                  wg_ref, wb_ref, wo_ref, g_ref, b_ref, h_out, hb_out, *, alpha):
    d = h_ref.shape[1]
    acc = jnp.zeros(h_ref.shape, F32)
    for n, o_ref in enumerate((oa_ref, ob_ref, oc_ref, od_ref, oe_ref)):
        z = z_ref[:, n * BRANCH_W:(n + 1) * BRANCH_W].astype(F32)
        y = o_ref[...].astype(F32) * (z / (1.0 + jnp.exp(-z)))
        u = jnp.dot(y.astype(BF16), wb_ref[n], preferred_element_type=F32)
        g = jnp.dot(hb_ref[...], wg_ref[:, n * d:(n + 1) * d], preferred_element_type=F32)
        acc = acc + u / (1.0 + jnp.exp(-g))
    out = jnp.dot(acc.astype(BF16), wo_ref[...], preferred_element_type=F32)
    x = alpha * h_ref[...] + out
    mu = jnp.mean(x, axis=1, keepdims=True)
    xc = x - mu
    var = jnp.mean(xc * xc, axis=1, keepdims=True)
    y = xc * lax.rsqrt(var + LN_EPS) * g_ref[...] + b_ref[...]
    h_out[...] = y
    hb_out[...] = y.astype(BF16)


def _final(h, hb, os5, z, wg, wb, wo, ln_g, ln_b, alpha):
    n, d = h.shape
    tm = 256
    row = lambda w: pl.BlockSpec((tm, w), lambda i: (i, 0))
    full = lambda a: pl.BlockSpec(a.shape, lambda i: (0,) * a.ndim)
    return pl.pallas_call(
        functools.partial(_final_kernel, alpha=alpha),
        out_shape=(jax.ShapeDtypeStruct((n, d), F32), jax.ShapeDtypeStruct((n, d), BF16)),
        grid=(n // tm,),
        in_specs=[row(d), row(d)] + [row(BRANCH_W)] * N_BRANCH + [row(N_BRANCH * BRANCH_W),
                  full(wg), full(wb), full(wo), full(ln_g), full(ln_b)],
        out_specs=(row(d), row(d)),
        compiler_params=_cparams(1),
        name="merge_out_ln",
    )(h, hb, *os5, z, wg, wb, wo, ln_g, ln_b)


def _rope_tables(seq, rot_dim):
    pos = jnp.arange(seq, dtype=F32)
    inv = ROPE_THETA ** (-jnp.arange(0, rot_dim, 2, dtype=F32) / rot_dim)
    ang = pos[:, None] * inv[None, :]
    return jnp.cos(ang), jnp.sin(ang)


def _rot_cols(wg, nh, hd, r):
    lead = wg.shape[:-1]
    w4 = wg.reshape(lead + (nh, hd))
    half = r // 2
    parts = [-w4[..., half:r], w4[..., :half]]
    if hd > r:
        parts.append(jnp.zeros(lead + (nh, hd - r), wg.dtype))
    return jnp.concatenate(parts, axis=-1).reshape(lead + (nh * hd,))


def _rope_cs(t, nh, hd, r):
    cos, sin = _rope_tables(t, r)
    c = jnp.concatenate([cos, cos, jnp.ones((t, hd - r), F32)], axis=1)
    s = jnp.concatenate([sin, sin, jnp.zeros((t, hd - r), F32)], axis=1)
    return jnp.tile(c, (1, nh)), jnp.tile(s, (1, nh))


def kernel(x, mem, ln0_g, ln0_b, w_in, mla_q_norm, w_uq, mla_kv_norm, w_ukv, diff_lam, diff_norm,
           w_mem_kv, w_branch, w_out, ln_g, ln_b):
    b, t, d = x.shape
    depth = w_in.shape[0]
    alpha = (2 * depth) ** 0.25
    assert t % 512 == 0 and d == 1024

    def seg(name):
        o, s = OFF[name]
        return w_in[:, :, o:o + s]

    zeros = lambda n: jnp.zeros((depth, d, n), w_in.dtype)

    w_plain = jnp.concatenate(
        [seg("a_v"), seg("b_v"), seg("c_v"), seg("d_cq"),
         seg("d_ckv"), seg("i_w"), zeros(MXU_N - KV_LORA - IDX_HEADS), seg("e_q"), seg("z")], axis=-1).astype(BF16)
    plain_widths = (BRANCH_W,) * 6 + (N_BRANCH * BRANCH_W,)

    rope_groups = [
        (seg("a_q"), N_HEADS, HEAD_DIM, ROT_64), (seg("a_k"), N_HEADS, HEAD_DIM, ROT_64),
        (seg("i_q"), IDX_HEADS, IDX_DIM, ROT_32), (jnp.tile(seg("i_k"), (1, 1, IDX_HEADS)), IDX_HEADS, IDX_DIM, ROT_32),
        (seg("b_q"), N_HEADS, HEAD_DIM, ROT_64), (seg("b_k"), N_HEADS, HEAD_DIM, ROT_64),
        (seg("c_q"), 2 * N_HEADS, DIFF_DIM, ROT_32), (seg("c_k"), 2 * N_HEADS, DIFF_DIM, ROT_32),
        (jnp.concatenate([seg("d_kr"), zeros(MXU_N - MLA_ROPE)], axis=-1), 1, MXU_N, MLA_ROPE),
    ]
    w_rope = jnp.concatenate([g for g, *_ in rope_groups], axis=-1).astype(BF16)
    w_rope_rot = jnp.concatenate([_rot_cols(g, nh, hd, r) for g, nh, hd, r in rope_groups], axis=-1).astype(BF16)
    cs = [_rope_cs(t, nh, hd, r) for _, nh, hd, r in rope_groups]
    ctab = jnp.concatenate([c for c, _ in cs], axis=1)
    stab = jnp.concatenate([s for _, s in cs], axis=1)
    rope_widths = (MXU_N,) * len(rope_groups)

    uq = w_uq.reshape(depth, Q_LORA, N_HEADS, MLA_NOPE + MLA_ROPE)
    qn_w, qr_w = uq[..., :MLA_NOPE], uq[..., MLA_NOPE:]
    pad32 = jnp.zeros((depth, Q_LORA, N_HEADS, LANES - MLA_NOPE - MLA_ROPE), w_uq.dtype)
    hw = N_HEADS * LANES
    wq = jnp.concatenate([qn_w, qr_w, pad32], axis=-1).reshape(depth, Q_LORA, hw).astype(BF16)
    half = MLA_ROPE // 2
    wq_rot = jnp.concatenate([jnp.zeros_like(qn_w), -qr_w[..., half:], qr_w[..., :half], pad32],
                             axis=-1).reshape(depth, Q_LORA, hw).astype(BF16)
    cos_m, sin_m = _rope_tables(t, MLA_ROPE)
    one = lambda n: jnp.ones((t, n), F32)
    zer = lambda n: jnp.zeros((t, n), F32)
    ct_q = jnp.tile(jnp.concatenate([one(MLA_NOPE), cos_m, cos_m, one(LANES - MLA_NOPE - MLA_ROPE)], axis=1), (1, N_HEADS))
    st_q = jnp.tile(jnp.concatenate([zer(MLA_NOPE), sin_m, sin_m, zer(LANES - MLA_NOPE - MLA_ROPE)], axis=1), (1, N_HEADS))
    ukv = w_ukv.reshape(depth, KV_LORA, N_HEADS, MLA_NOPE + MLA_V)
    wk = jnp.concatenate([ukv[..., :MLA_NOPE], jnp.zeros((depth, KV_LORA, N_HEADS, LANES - MLA_NOPE), w_ukv.dtype)],
                         axis=-1).reshape(depth, KV_LORA, hw).astype(BF16)
    wv = ukv[..., MLA_NOPE:].reshape(depth, KV_LORA, N_HEADS * MLA_V).astype(BF16)
    place = np.zeros((MXU_N, hw), np.float32)
    for hh in range(N_HEADS):
        for j in range(MLA_ROPE):
            place[j, hh * LANES + MLA_NOPE + j] = 1.0
    place = jnp.asarray(place, BF16)

    wg = seg("g").astype(BF16)
    wb = w_branch.astype(BF16)
    wo = w_out.astype(BF16)
    wmem = w_mem_kv.astype(BF16)
    norm_t = jnp.tile(diff_norm, (1, N_HEADS)).reshape(depth, 1, BRANCH_W)

    h, hb = _layer_norm0(x.reshape(b * t, d), ln0_g, ln0_b)
    for l in range(depth):
        hb3 = hb.reshape(b, t, d)
        av, bv, cv, dcq, ckv_iw, eq, z = _proj_plain(hb3, w_plain[l], plain_widths)
        aq, ak, iq, ik, bq, bk, cq, ck, kr = _proj_rope(hb3, w_rope[l], w_rope_rot[l], ctab, stab, rope_widths)

        o_a = _dsa(aq, ak, av, iq, ik, ckv_iw)
        o_b = _moba(bq, bk, bv, _kbar(bk))
        lam_init = 0.8 - 0.6 * math.exp(-0.3 * l)
        misc = jnp.full((8, LANES), lam_init, F32)
        o_c = _diff(cq, ck, cv, diff_lam[l].astype(F32), norm_t[l], misc)
        qm, km, vm = _mla_prep(dcq, ckv_iw, kr, mla_q_norm[l].reshape(1, Q_LORA), mla_kv_norm[l].reshape(1, KV_LORA),
                               wq[l], wq_rot[l], wk[l], wv[l], place, ct_q, st_q)
        o_d = _mla(qm, km, vm)
        o_e = _mem_attn(eq, _mem_kv(mem, wmem[l]))

        os5 = [o.reshape(b * t, BRANCH_W) for o in (o_a, o_b, o_c, o_d, o_e)]
        h, hb = _final(h, hb, os5, z.reshape(b * t, N_BRANCH * BRANCH_W), wg[l], wb[l], wo[l],
                       ln_g[l].reshape(1, d), ln_b[l].reshape(1, d), alpha)
    return h.reshape(b, t, d)
```

```python
import functools
import math

import numpy as np
import jax
import jax.numpy as jnp
from jax import lax
from jax.experimental import pallas as pl
from jax.experimental.pallas import tpu as pltpu

F32 = jnp.float32
BF16 = jnp.bfloat16
I32 = jnp.int32

N_HEADS = 4
HEAD_DIM = 64
BRANCH_W = N_HEADS * HEAD_DIM
N_BRANCH = 5
ROPE_THETA = 500000.0
ROT_64 = 16
ROT_32 = 8
IDX_HEADS = 8
IDX_DIM = 32
TOPK_MAX = 256
MOBA_BLOCK = 256
MOBA_TOPK = 3
DIFF_DIM = 32
Q_LORA = 256
KV_LORA = 128
MLA_NOPE = 64
MLA_ROPE = 32
MLA_V = 64
LN_EPS = 1e-5
RMS_EPS = 1e-6

IN_LAYOUT = (
    ("a_q", BRANCH_W), ("a_k", BRANCH_W), ("a_v", BRANCH_W),
    ("i_q", IDX_HEADS * IDX_DIM), ("i_k", IDX_DIM), ("i_w", IDX_HEADS),
    ("b_q", BRANCH_W), ("b_k", BRANCH_W), ("b_v", BRANCH_W),
    ("c_q", BRANCH_W), ("c_k", BRANCH_W), ("c_v", BRANCH_W),
    ("d_cq", Q_LORA), ("d_ckv", KV_LORA), ("d_kr", MLA_ROPE),
    ("e_q", BRANCH_W),
    ("z", N_BRANCH * BRANCH_W),
    ("g", N_BRANCH * 1024),
)

SUBLANES = 8
LANES = 128
MXU_N = 256
TQ = 256
CK = 256
NEG = -1e30
LOG2E = math.log2(math.e)
INT_MIN = np.int32(-2 ** 31)
VMEM_LIMIT = 56 * 1024 * 1024


def _offsets():
    off, out = 0, {}
    for name, size in IN_LAYOUT:
        out[name] = (off, size)
        off += size
    return out


OFF = _offsets()


def _nt_dot(a, b):
    return lax.dot_general(a, b, (((1,), (1,)), ((), ())), preferred_element_type=F32)


def _fold_rows(w):
    xs = [w[r:r + SUBLANES, :] for r in range(0, w.shape[0], SUBLANES)]
    while len(xs) > 1:
        xs = [xs[j] + xs[j + 1] for j in range(0, len(xs) - 1, 2)] + ([xs[-1]] if len(xs) % 2 else [])
    return xs[0]


def _cparams(n_axes):
    return pltpu.CompilerParams(dimension_semantics=("arbitrary",) * n_axes,
                                vmem_limit_bytes=VMEM_LIMIT)


def _softmax_step(s_t, vt_h, m_ref, l_ref, acc_ref):
    m_old = m_ref[...]
    m_new = jnp.maximum(m_old, jnp.max(s_t, axis=0, keepdims=True))
    alpha = jnp.exp2(m_old - m_new)
    p = jnp.exp2(s_t - m_new)
    l_ref[...] = alpha * l_ref[...] + _fold_rows(p)
    acc_ref[...] = alpha * acc_ref[...] + jnp.dot(vt_h, p.astype(BF16), preferred_element_type=F32)
    m_ref[...] = m_new


def _softmax_init(m_ref, l_ref, acc_ref):
    m_ref[...] = jnp.full(m_ref.shape, NEG, F32)
    l_ref[...] = jnp.zeros(l_ref.shape, F32)
    acc_ref[...] = jnp.zeros(acc_ref.shape, F32)


def _softmax_out(l_ref, acc_ref):
    return acc_ref[...] / jnp.sum(l_ref[...], axis=0, keepdims=True)


def _flash_loop(n_prev, qk_all, mask, vt_rows, state):
    s_ref, m_ref, l_ref, acc_ref = state
    n_state = m_ref.shape[0]
    for j in range(n_state):
        _softmax_init(m_ref.at[j], l_ref.at[j], acc_ref.at[j])

    def park(tiles):
        for j, s in enumerate(tiles):
            s_ref[j] = s

    def consume(c, diag):
        for j in range(n_state):
            _softmax_step(mask(c, j, s_ref[j], diag), vt_rows(c, j),
                          m_ref.at[j], l_ref.at[j], acc_ref.at[j])

    park(qk_all(0))

    def body(c, carry):
        nxt = qk_all(c + 1)
        consume(c, False)
        park(nxt)
        return carry

    lax.fori_loop(0, n_prev, body, 0)
    consume(n_prev, True)


def _attn_scratch(n_state):
    return [pltpu.VMEM((n_state, CK, TQ), F32), pltpu.VMEM((n_state, 1, TQ), F32),
            pltpu.VMEM((n_state, SUBLANES, TQ), F32), pltpu.VMEM((n_state, HEAD_DIM, TQ), F32),
            pltpu.VMEM((BRANCH_W, TQ), F32)]


def _kv_specs(t, w):
    kspec = pl.BlockSpec((None, t, w), lambda bb, i: (bb, 0, 0))
    vspec = pl.BlockSpec((None, t // CK, BRANCH_W, CK), lambda bb, i: (bb, 0, 0, 0))
    return kspec, vspec


def _ln_kernel(x_ref, g_ref, b_ref, h_ref, hb_ref):
    x = x_ref[...]
    mu = jnp.mean(x, axis=1, keepdims=True)
    xc = x - mu
    var = jnp.mean(xc * xc, axis=1, keepdims=True)
    y = xc * lax.rsqrt(var + LN_EPS) * g_ref[...] + b_ref[...]
    h_ref[...] = y
    hb_ref[...] = y.astype(BF16)


def _layer_norm0(x2, g, b):
    n, d = x2.shape
    tm = 512
    row = pl.BlockSpec((tm, d), lambda i: (i, 0))
    vec = pl.BlockSpec((1, d), lambda i: (0, 0))
    return pl.pallas_call(
        _ln_kernel,
        out_shape=(jax.ShapeDtypeStruct((n, d), F32), jax.ShapeDtypeStruct((n, d), BF16)),
        grid=(n // tm,),
        in_specs=[row, vec, vec],
        out_specs=(row, row),
        compiler_params=_cparams(1),
        name="ln0",
    )(x2, g.reshape(1, d), b.reshape(1, d))


def _proj_plain_kernel(x_ref, w_ref, wt_ref, *out_refs, n_t):
    for g, o_ref in enumerate(out_refs[:n_t]):
        vt = _nt_dot(wt_ref[g * BRANCH_W:(g + 1) * BRANCH_W, :], x_ref[...])
        for j in range(o_ref.shape[0]):
            o_ref[j] = vt[:, j * CK:(j + 1) * CK].astype(o_ref.dtype)
    off = 0
    for o_ref in out_refs[n_t:]:
        wd = o_ref.shape[-1]
        for j in range(0, wd, MXU_N):
            acc = jnp.dot(x_ref[...], w_ref[:, off + j:off + j + MXU_N], preferred_element_type=F32)
            o_ref[:, j:j + MXU_N] = acc.astype(o_ref.dtype)
        off += wd


def _proj_plain(hb3, w, wt, widths):
    b, t, d = hb3.shape
    tm = 512
    n_t = wt.shape[0] // BRANCH_W
    shapes = [jax.ShapeDtypeStruct((b, t // CK, BRANCH_W, CK), BF16)] * n_t
    specs = [pl.BlockSpec((None, tm // CK, BRANCH_W, CK), lambda i, bb: (bb, i, 0, 0))] * n_t
    shapes += [jax.ShapeDtypeStruct((b, t, wd), BF16) for wd in widths]
    specs += [pl.BlockSpec((None, tm, wd), lambda i, bb: (bb, i, 0)) for wd in widths]
    return pl.pallas_call(
        functools.partial(_proj_plain_kernel, n_t=n_t),
        out_shape=tuple(shapes),
        grid=(t // tm, b),
        in_specs=[pl.BlockSpec((None, tm, d), lambda i, bb: (bb, i, 0)),
                  pl.BlockSpec(w.shape, lambda i, bb: (0, 0)),
                  pl.BlockSpec(wt.shape, lambda i, bb: (0, 0))],
        out_specs=tuple(specs),
        compiler_params=_cparams(2),
        name="proj_plain",
    )(hb3, w, wt)


def _proj_rope_kernel(x_ref, w_ref, wr_ref, c_ref, s_ref, *out_refs):
    off = 0
    for o_ref in out_refs:
        wd = o_ref.shape[-1]
        for j in range(0, wd, MXU_N):
            sl = slice(off + j, off + j + MXU_N)
            acc = jnp.dot(x_ref[...], w_ref[:, sl], preferred_element_type=F32)
            rot = jnp.dot(x_ref[...], wr_ref[:, sl], preferred_element_type=F32)
            o_ref[:, j:j + MXU_N] = (acc * c_ref[:, sl] + rot * s_ref[:, sl]).astype(o_ref.dtype)
        off += wd


def _proj_rope(hb3, w, wr, ctab, stab, widths):
    b, t, d = hb3.shape
    tm = 256
    ncol = w.shape[1]
    wspec = pl.BlockSpec((d, ncol), lambda i, bb: (0, 0))
    tspec = pl.BlockSpec((tm, ncol), lambda i, bb: (i, 0))
    return pl.pallas_call(
        _proj_rope_kernel,
        out_shape=tuple(jax.ShapeDtypeStruct((b, t, wd), BF16) for wd in widths),
        grid=(t // tm, b),
        in_specs=[pl.BlockSpec((None, tm, d), lambda i, bb: (bb, i, 0)), wspec, wspec, tspec, tspec],
        out_specs=tuple(pl.BlockSpec((None, tm, wd), lambda i, bb: (bb, i, 0)) for wd in widths),
        compiler_params=_cparams(2),
        name="proj_rope",
    )(hb3, w, wr, ctab, stab)


def _dsa_kernel(aq_ref, ak_ref, avt_ref, iq_ref, ik_ref, iw_ref, pick_ref, o_ref,
                keys_ref, iqm_ref, aqm_ref, wt_ref, thr_ref, s_ref, m_ref, l_ref, acc_ref, ot_ref,
                *, topk, idx_scale, pos_bits):
    i = pl.program_id(1)
    nk = i + 1
    lane_q = lax.broadcasted_iota(I32, (TQ, BRANCH_W), 1)
    kpos = lax.broadcasted_iota(I32, (CK, TQ), 0)
    qpos = lax.broadcasted_iota(I32, (CK, TQ), 1)

    iq = iq_ref[...].astype(F32)
    for hh in range(IDX_HEADS):
        iqm_ref[hh] = jnp.where((lane_q >> 5) == hh, iq, 0.0).astype(BF16)
    aq = aq_ref[...].astype(F32) * (HEAD_DIM ** -0.5 * LOG2E)
    for h in range(N_HEADS):
        aqm_ref[h] = jnp.where((lane_q >> 6) == h, aq, 0.0).astype(BF16)
    wt_ref[...] = _nt_dot(pick_ref[...], iw_ref[...])

    def score_body(c, carry):
        kc = ik_ref[pl.ds(pl.multiple_of(c * CK, CK), CK), :]
        acc = jnp.zeros((CK, TQ), F32)
        for hh in range(IDX_HEADS):
            logit = _nt_dot(kc, iqm_ref[hh])
            acc = acc + jnp.maximum(logit, 0.0) * wt_ref[hh:hh + 1, :]
        sc = acc * idx_scale
        bits = pltpu.bitcast(sc, I32)
        key = jnp.where(bits < 0, INT_MIN - bits, bits)
        causal = (c * CK + kpos) <= (i * TQ + qpos)
        keys_ref[c] = jnp.where(causal, key, INT_MIN)
        return carry

    lax.fori_loop(0, nk, score_body, 0)

    def count(pred):
        def body(c, part):
            return part + _fold_rows(jnp.where(pred(keys_ref[c], c), 1.0, 0.0))
        part = lax.fori_loop(0, nk, body, jnp.zeros((SUBLANES, TQ), F32))
        return jnp.sum(part, axis=0, keepdims=True)

    def bit_body(bi, t_u):
        c_u = t_u | jnp.left_shift(jnp.int32(1), 31 - bi)
        ckey = c_u ^ INT_MIN
        cnt = count(lambda k, c: k >= ckey)
        return jnp.where(cnt >= float(topk), c_u, t_u)

    t_u = lax.fori_loop(0, 32, bit_body, jnp.zeros((1, TQ), I32))
    thr = t_u ^ INT_MIN

    n_gt = count(lambda k, c: k > thr)
    n_eq = count(lambda k, c: k == thr)
    need = float(topk) - n_gt
    amb = jnp.logical_and(n_eq > need, thr > INT_MIN)
    any_amb = jnp.max(jnp.where(amb, 1.0, 0.0)) > 0.5

    @pl.when(any_amb)
    def _():
        def pos_body(bi, r):
            cand = r + jnp.left_shift(jnp.int32(1), pos_bits - 1 - bi)
            cnt = count(lambda k, c: jnp.logical_and(k == thr, (c * CK + kpos) < cand))
            return jnp.where(cnt < need, cand, r)

        r = lax.fori_loop(0, pos_bits, pos_body, jnp.zeros((1, TQ), I32))
        r = jnp.where(amb, r, jnp.int32(2 ** 30))

        def drop_body(c, carry):
            k = keys_ref[c]
            drop = jnp.logical_and(k == thr, (c * CK + kpos) > r)
            keys_ref[c] = jnp.where(drop, INT_MIN, k)
            return carry

        lax.fori_loop(0, nk, drop_body, 0)

    thr_ref[...] = jnp.maximum(thr, INT_MIN + 1)

    def qk_all(c):
        kc = ak_ref[pl.ds(pl.multiple_of(c * CK, CK), CK), :]
        return [_nt_dot(kc, aqm_ref[h]) for h in range(N_HEADS)]

    _flash_loop(i, qk_all,
                lambda c, h, s, diag: jnp.where(keys_ref[c] >= thr_ref[...], s, NEG),
                lambda c, h: avt_ref[c, h * HEAD_DIM:(h + 1) * HEAD_DIM, :],
                (s_ref, m_ref, l_ref, acc_ref))
    for h in range(N_HEADS):
        ot_ref[h * HEAD_DIM:(h + 1) * HEAD_DIM, :] = _softmax_out(l_ref.at[h], acc_ref.at[h])
    o_ref[...] = ot_ref[...].T.astype(o_ref.dtype)


def _dsa(aq, ak, avt, iq, ik, iw):
    b, t, _ = aq.shape
    topk = min(TOPK_MAX, t // 4)
    qspec = pl.BlockSpec((None, TQ, BRANCH_W), lambda bb, i: (bb, i, 0))
    kspec, vspec = _kv_specs(t, BRANCH_W)
    pick = np.zeros((2 * SUBLANES, MXU_N), np.float32)
    for hh in range(IDX_HEADS):
        pick[hh, KV_LORA + hh] = 1.0
    pick = jnp.asarray(pick, BF16)
    kern = functools.partial(_dsa_kernel, topk=topk, idx_scale=(IDX_HEADS * IDX_DIM) ** -0.5,
                             pos_bits=max(1, (t - 1).bit_length()))
    return pl.pallas_call(
        kern,
        out_shape=jax.ShapeDtypeStruct((b, t, BRANCH_W), BF16),
        grid=(b, t // TQ),
        in_specs=[qspec, kspec, vspec, qspec, kspec, qspec,
                  pl.BlockSpec(pick.shape, lambda bb, i: (0, 0))],
        out_specs=qspec,
        scratch_shapes=[
            pltpu.VMEM((t // CK, CK, TQ), I32),
            pltpu.VMEM((IDX_HEADS, TQ, BRANCH_W), BF16),
            pltpu.VMEM((N_HEADS, TQ, BRANCH_W), BF16),
            pltpu.VMEM((2 * SUBLANES, TQ), F32),
            pltpu.VMEM((1, TQ), I32),
        ] + _attn_scratch(N_HEADS),
        compiler_params=_cparams(2),
        name="dsa",
    )(aq, ak, avt, iq, ik, iw, pick)


def _kbar_kernel(k_ref, o_ref):
    o_ref[...] = jnp.zeros(o_ref.shape, o_ref.dtype)
    nb = k_ref.shape[0] // MOBA_BLOCK
    for n in range(nb):
        blk = k_ref[n * MOBA_BLOCK:(n + 1) * MOBA_BLOCK, :].astype(F32)
        o_ref[n:n + 1, :] = jnp.mean(blk, axis=0, keepdims=True).astype(o_ref.dtype)


def _kbar(bk):
    b, t, w = bk.shape
    nbp = max(2 * SUBLANES, t // MOBA_BLOCK)
    return pl.pallas_call(
        _kbar_kernel,
        out_shape=jax.ShapeDtypeStruct((b, nbp, w), BF16),
        grid=(b,),
        in_specs=[pl.BlockSpec((None, t, w), lambda bb: (bb, 0, 0))],
        out_specs=pl.BlockSpec((None, nbp, w), lambda bb: (bb, 0, 0)),
        compiler_params=_cparams(1),
        name="moba_kbar",
    )(bk)


def _moba_kernel(q_ref, k_ref, vt_ref, kbar_ref, o_ref, qm_ref, bias_ref, s_ref, m_ref, l_ref, acc_ref, ot_ref):
    i = pl.program_id(1)
    nbp = kbar_ref.shape[0]
    lane_q = lax.broadcasted_iota(I32, (TQ, BRANCH_W), 1)
    blk = lax.broadcasted_iota(I32, (nbp, TQ), 0)
    blk_f = blk.astype(F32)
    kpos = lax.broadcasted_iota(I32, (CK, TQ), 0)
    qpos = lax.broadcasted_iota(I32, (CK, TQ), 1)
    q = q_ref[...].astype(F32) * (HEAD_DIM ** -0.5 * LOG2E)
    for h in range(N_HEADS):
        qm_ref[h] = jnp.where((lane_q >> 6) == h, q, 0.0).astype(BF16)

    for h in range(N_HEADS):
        g = jnp.where(blk < i, _nt_dot(kbar_ref[...], qm_ref[h]), NEG)
        bias = jnp.full((nbp, TQ), NEG, F32)
        for _ in range(MOBA_TOPK):
            mx = jnp.max(g, axis=0, keepdims=True)
            first = jnp.min(jnp.where(g == mx, blk_f, 1e9), axis=0, keepdims=True)
            pick = jnp.logical_and(blk_f == first, mx > 0.5 * NEG)
            bias = jnp.where(pick, 0.0, bias)
            g = jnp.where(pick, NEG, g)
        bias_ref[h] = bias

    def qk_all(c):
        kc = k_ref[pl.ds(pl.multiple_of(c * CK, CK), CK), :]
        return [_nt_dot(kc, qm_ref[h]) for h in range(N_HEADS)]

    def mask(c, h, s, diag):
        return jnp.where(kpos <= qpos, s, NEG) if diag else s + bias_ref[h, pl.ds(c, 1), :]

    _flash_loop(i, qk_all, mask, lambda c, h: vt_ref[c, h * HEAD_DIM:(h + 1) * HEAD_DIM, :],
                (s_ref, m_ref, l_ref, acc_ref))
    for h in range(N_HEADS):
        ot_ref[h * HEAD_DIM:(h + 1) * HEAD_DIM, :] = _softmax_out(l_ref.at[h], acc_ref.at[h])
    o_ref[...] = ot_ref[...].T.astype(o_ref.dtype)


def _moba(bq, bk, bvt, kbar):
    b, t, w = bq.shape
    assert TQ == MOBA_BLOCK and CK == MOBA_BLOCK and t % MOBA_BLOCK == 0
    nbp = kbar.shape[1]
    qspec = pl.BlockSpec((None, TQ, w), lambda bb, i: (bb, i, 0))
    kspec, vspec = _kv_specs(t, w)
    return pl.pallas_call(
        _moba_kernel,
        out_shape=jax.ShapeDtypeStruct((b, t, w), BF16),
        grid=(b, t // TQ),
        in_specs=[qspec, kspec, vspec, pl.BlockSpec((None, nbp, w), lambda bb, i: (bb, 0, 0))],
        out_specs=qspec,
        scratch_shapes=[pltpu.VMEM((N_HEADS, TQ, w), BF16), pltpu.VMEM((N_HEADS, nbp, TQ), F32)]
        + _attn_scratch(N_HEADS),
        compiler_params=_cparams(2),
        name="moba",
    )(bq, bk, bvt, kbar)


def _diff_kernel(q_ref, k_ref, vt_ref, lam_ref, norm_ref, misc_ref, o_ref,
                 qm_ref, s_ref, m_ref, l_ref, acc_ref, ot_ref):
    i = pl.program_id(1)
    lane_q = lax.broadcasted_iota(I32, (TQ, BRANCH_W), 1)
    kpos = lax.broadcasted_iota(I32, (CK, TQ), 0)
    qpos = lax.broadcasted_iota(I32, (CK, TQ), 1)
    q = q_ref[...].astype(F32) * (DIFF_DIM ** -0.5 * LOG2E)
    for j in range(2 * N_HEADS):
        qm_ref[j] = jnp.where((lane_q >> 5) == j, q, 0.0).astype(BF16)

    dl = lam_ref[...]
    lam_init = misc_ref[0:1, 0:1]
    lam = (jnp.exp(jnp.sum(dl[0:1, :] * dl[1:2, :], axis=1, keepdims=True))
           - jnp.exp(jnp.sum(dl[2:3, :] * dl[3:4, :], axis=1, keepdims=True)) + lam_init)

    def qk_all(c):
        kc = k_ref[pl.ds(pl.multiple_of(c * CK, CK), CK), :]
        return [_nt_dot(kc, qm_ref[j]) for j in range(2 * N_HEADS)]

    _flash_loop(i, qk_all,
                lambda c, j, s, diag: jnp.where(kpos <= qpos, s, NEG) if diag else s,
                lambda c, j: vt_ref[c, (j // 2) * HEAD_DIM:(j // 2 + 1) * HEAD_DIM, :],
                (s_ref, m_ref, l_ref, acc_ref))

    post = norm_ref[...] * (1.0 - lam_init)
    for h in range(N_HEADS):
        o_h = (_softmax_out(l_ref.at[2 * h], acc_ref.at[2 * h])
               - lam * _softmax_out(l_ref.at[2 * h + 1], acc_ref.at[2 * h + 1]))
        ms = jnp.mean(o_h * o_h, axis=0, keepdims=True)
        ot_ref[h * HEAD_DIM:(h + 1) * HEAD_DIM, :] = o_h * lax.rsqrt(ms + RMS_EPS) * post
    o_ref[...] = ot_ref[...].T.astype(o_ref.dtype)


def _diff(cq, ck, cvt, lam, norm, misc):
    b, t, w = cq.shape
    qspec = pl.BlockSpec((None, TQ, w), lambda bb, i: (bb, i, 0))
    kspec, vspec = _kv_specs(t, w)
    full = lambda a: pl.BlockSpec(a.shape, lambda bb, i: (0,) * a.ndim)
    return pl.pallas_call(
        _diff_kernel,
        out_shape=jax.ShapeDtypeStruct((b, t, w), BF16),
        grid=(b, t // TQ),
        in_specs=[qspec, kspec, vspec, full(lam), full(norm), full(misc)],
        out_specs=qspec,
        scratch_shapes=[pltpu.VMEM((2 * N_HEADS, TQ, w), BF16)] + _attn_scratch(2 * N_HEADS),
        compiler_params=_cparams(2),
        name="diff",
    )(cq, ck, cvt, lam, norm, misc)


def _mla_prep_kernel(cq_ref, ckv_ref, kr_ref, qn_ref, kvn_ref, wq_ref, wqr_ref, wk_ref, wvt_ref,
                     p_ref, ct_ref, st_ref, q_out, k_out, vt_out):
    x = cq_ref[...].astype(F32)
    xn = (x * lax.rsqrt(jnp.mean(x * x, axis=1, keepdims=True) + RMS_EPS) * qn_ref[...]).astype(BF16)
    q = (jnp.dot(xn, wq_ref[...], preferred_element_type=F32) * ct_ref[...]
         + jnp.dot(xn, wqr_ref[...], preferred_element_type=F32) * st_ref[...])
    q_out[...] = q.astype(q_out.dtype)
    c = ckv_ref[:, :KV_LORA].astype(F32)
    cn = (c * lax.rsqrt(jnp.mean(c * c, axis=1, keepdims=True) + RMS_EPS) * kvn_ref[...]).astype(BF16)
    k = (jnp.dot(cn, wk_ref[...], preferred_element_type=F32)
         + jnp.dot(kr_ref[...], p_ref[...], preferred_element_type=F32))
    k_out[...] = k.astype(k_out.dtype)
    vt = _nt_dot(wvt_ref[...], cn)
    for j in range(vt_out.shape[0]):
        vt_out[j] = vt[:, j * CK:(j + 1) * CK].astype(vt_out.dtype)


def _mla_prep(dcq, ckv, kr, qn, kvn, wq, wqr, wk, wvt, pmat, ct, st):
    b, t, _ = dcq.shape
    tm = 512
    hw = N_HEADS * LANES
    row = lambda w: pl.BlockSpec((None, tm, w), lambda i, bb: (bb, i, 0))
    full = lambda a: pl.BlockSpec(a.shape, lambda i, bb: (0,) * a.ndim)
    tab = pl.BlockSpec((tm, hw), lambda i, bb: (i, 0))
    return pl.pallas_call(
        _mla_prep_kernel,
        out_shape=(jax.ShapeDtypeStruct((b, t, hw), BF16), jax.ShapeDtypeStruct((b, t, hw), BF16),
                   jax.ShapeDtypeStruct((b, t // CK, BRANCH_W, CK), BF16)),
        grid=(t // tm, b),
        in_specs=[row(Q_LORA), row(MXU_N), row(MXU_N), full(qn), full(kvn), full(wq), full(wqr),
                  full(wk), full(wvt), full(pmat), tab, tab],
        out_specs=(row(hw), row(hw),
                   pl.BlockSpec((None, tm // CK, BRANCH_W, CK), lambda i, bb: (bb, i, 0, 0))),
        compiler_params=_cparams(2),
        name="mla_prep",
    )(dcq, ckv, kr, qn, kvn, wq, wqr, wk, wvt, pmat, ct, st)


def _mla_kernel(q_ref, k_ref, vt_ref, o_ref, s_ref, m_ref, l_ref, acc_ref, ot_ref):
    i = pl.program_id(1)
    kpos = lax.broadcasted_iota(I32, (CK, TQ), 0)
    qpos = lax.broadcasted_iota(I32, (CK, TQ), 1)

    def qk_all(c):
        start = pl.multiple_of(c * CK, CK)
        hs = [slice(h * LANES, (h + 1) * LANES) for h in range(N_HEADS)]
        return [_nt_dot(k_ref[pl.ds(start, CK), sl], q_ref[:, sl]) for sl in hs]

    _flash_loop(i, qk_all,
                lambda c, h, s, diag: jnp.where(kpos <= qpos, s, NEG) if diag else s,
                lambda c, h: vt_ref[c, h * HEAD_DIM:(h + 1) * HEAD_DIM, :],
                (s_ref, m_ref, l_ref, acc_ref))
    for h in range(N_HEADS):
        ot_ref[h * HEAD_DIM:(h + 1) * HEAD_DIM, :] = _softmax_out(l_ref.at[h], acc_ref.at[h])
    o_ref[...] = ot_ref[...].T.astype(o_ref.dtype)


def _mla(qm, km, vmt):
    b, t, hw = qm.shape
    kspec, vspec = _kv_specs(t, hw)
    return pl.pallas_call(
        _mla_kernel,
        out_shape=jax.ShapeDtypeStruct((b, t, BRANCH_W), BF16),
        grid=(b, t // TQ),
        in_specs=[pl.BlockSpec((None, TQ, hw), lambda bb, i: (bb, i, 0)), kspec, vspec],
        out_specs=pl.BlockSpec((None, TQ, BRANCH_W), lambda bb, i: (bb, i, 0)),
        scratch_shapes=_attn_scratch(N_HEADS),
        compiler_params=_cparams(2),
        name="mla",
    )(qm, km, vmt)


def _matmul_kernel(x_ref, w_ref, o_ref):
    o_ref[...] = jnp.dot(x_ref[...].astype(BF16), w_ref[...], preferred_element_type=F32).astype(o_ref.dtype)


def _mem_kv(mem, w):
    b, m, d = mem.shape
    n = w.shape[1]
    return pl.pallas_call(
        _matmul_kernel,
        out_shape=jax.ShapeDtypeStruct((b, m, n), BF16),
        grid=(b,),
        in_specs=[pl.BlockSpec((None, m, d), lambda bb: (bb, 0, 0)), pl.BlockSpec((d, n), lambda bb: (0, 0))],
        out_specs=pl.BlockSpec((None, m, n), lambda bb: (bb, 0, 0)),
        compiler_params=_cparams(1),
        name="mem_kv",
    )(mem, w)


def _mem_kernel(q_ref, kv_ref, o_ref):
    tq = q_ref.shape[0]
    lane_q = lax.broadcasted_iota(I32, (tq, BRANCH_W), 1)
    q = q_ref[...].astype(F32) * (HEAD_DIM ** -0.5)
    mk = kv_ref[:, :BRANCH_W]
    mv = kv_ref[:, BRANCH_W:]
    out = jnp.zeros((tq, BRANCH_W), F32)
    for h in range(N_HEADS):
        in_h = (lane_q >> 6) == h
        s = _nt_dot(jnp.where(in_h, q, 0.0).astype(BF16), mk)
        p = jnp.exp(s - jnp.max(s, axis=1, keepdims=True))
        o_h = jnp.dot(p.astype(BF16), mv, preferred_element_type=F32) / jnp.sum(p, axis=1, keepdims=True)
        out = jnp.where(in_h, o_h, out)
    o_ref[...] = out.astype(o_ref.dtype)


def _mem_attn(eq, mkv):
    b, t, w = eq.shape
    m = mkv.shape[1]
    tq = 512
    return pl.pallas_call(
        _mem_kernel,
        out_shape=jax.ShapeDtypeStruct((b, t, w), BF16),
        grid=(b, t // tq),
        in_specs=[pl.BlockSpec((None, tq, w), lambda bb, i: (bb, i, 0)),
                  pl.BlockSpec((None, m, 2 * w), lambda bb, i: (bb, 0, 0))],
        out_specs=pl.BlockSpec((None, tq, w), lambda bb, i: (bb, i, 0)),
        compiler_params=_cparams(2),
        name="mem_attn",
    )(eq, mkv)


def _final_kernel(h_ref, hb_ref, oa_ref, ob_ref, oc_ref, od_ref, oe_ref, z_ref,
                  wg_ref, wb_ref, wo_ref, g_ref, b_ref, h_out, hb_out, *, alpha):
    d = h_ref.shape[1]
    acc = jnp.zeros(h_ref.shape, F32)
    for n, o_ref in enumerate((oa_ref, ob_ref, oc_ref, od_ref, oe_ref)):
        z = z_ref[:, n * BRANCH_W:(n + 1) * BRANCH_W].astype(F32)
        y = o_ref[...].astype(F32) * (z / (1.0 + jnp.exp(-z)))
        u = jnp.dot(y.astype(BF16), wb_ref[n], preferred_element_type=F32)
        g = jnp.dot(hb_ref[...], wg_ref[:, n * d:(n + 1) * d], preferred_element_type=F32)
        acc = acc + u / (1.0 + jnp.exp(-g))
    out = jnp.dot(acc.astype(BF16), wo_ref[...], preferred_element_type=F32)
    x = alpha * h_ref[...] + out
    mu = jnp.mean(x, axis=1, keepdims=True)
    xc = x - mu
    var = jnp.mean(xc * xc, axis=1, keepdims=True)
    y = xc * lax.rsqrt(var + LN_EPS) * g_ref[...] + b_ref[...]
    h_out[...] = y
    hb_out[...] = y.astype(BF16)


def _final(h, hb, os5, z, wg, wb, wo, ln_g, ln_b, alpha):
    n, d = h.shape
    tm = 256
    row = lambda w: pl.BlockSpec((tm, w), lambda i: (i, 0))
    full = lambda a: pl.BlockSpec(a.shape, lambda i: (0,) * a.ndim)
    return pl.pallas_call(
        functools.partial(_final_kernel, alpha=alpha),
        out_shape=(jax.ShapeDtypeStruct((n, d), F32), jax.ShapeDtypeStruct((n, d), BF16)),
        grid=(n // tm,),
        in_specs=[row(d), row(d)] + [row(BRANCH_W)] * N_BRANCH + [row(N_BRANCH * BRANCH_W),
                  full(wg), full(wb), full(wo), full(ln_g), full(ln_b)],
        out_specs=(row(d), row(d)),
        compiler_params=_cparams(1),
        name="merge_out_ln",
    )(h, hb, *os5, z, wg, wb, wo, ln_g, ln_b)


def _rope_tables(seq, rot_dim):
    pos = jnp.arange(seq, dtype=F32)
    inv = ROPE_THETA ** (-jnp.arange(0, rot_dim, 2, dtype=F32) / rot_dim)
    ang = pos[:, None] * inv[None, :]
    return jnp.cos(ang), jnp.sin(ang)


def _rot_cols(wg, nh, hd, r):
    lead = wg.shape[:-1]
    w4 = wg.reshape(lead + (nh, hd))
    half = r // 2
    parts = [-w4[..., half:r], w4[..., :half]]
    if hd > r:
        parts.append(jnp.zeros(lead + (nh, hd - r), wg.dtype))
    return jnp.concatenate(parts, axis=-1).reshape(lead + (nh * hd,))


def _rope_cs(t, nh, hd, r):
    cos, sin = _rope_tables(t, r)
    c = jnp.concatenate([cos, cos, jnp.ones((t, hd - r), F32)], axis=1)
    s = jnp.concatenate([sin, sin, jnp.zeros((t, hd - r), F32)], axis=1)
    return jnp.tile(c, (1, nh)), jnp.tile(s, (1, nh))


def kernel(x, mem, ln0_g, ln0_b, w_in, mla_q_norm, w_uq, mla_kv_norm, w_ukv, diff_lam, diff_norm,
           w_mem_kv, w_branch, w_out, ln_g, ln_b):
    b, t, d = x.shape
    depth = w_in.shape[0]
    alpha = (2 * depth) ** 0.25
    assert t % 512 == 0 and d == 1024

    def seg(name):
        o, s = OFF[name]
        return w_in[:, :, o:o + s]

    zeros = lambda n: jnp.zeros((depth, d, n), w_in.dtype)

    w_plain = jnp.concatenate(
        [seg("d_cq"), seg("d_ckv"), seg("i_w"), zeros(MXU_N - KV_LORA - IDX_HEADS), seg("e_q"), seg("z")],
        axis=-1).astype(BF16)
    plain_widths = (BRANCH_W,) * 3 + (N_BRANCH * BRANCH_W,)
    w_vt = jnp.swapaxes(jnp.concatenate([seg("a_v"), seg("b_v"), seg("c_v")], axis=-1), 1, 2).astype(BF16)

    rope_groups = [
        (seg("a_q"), N_HEADS, HEAD_DIM, ROT_64), (seg("a_k"), N_HEADS, HEAD_DIM, ROT_64),
        (seg("i_q"), IDX_HEADS, IDX_DIM, ROT_32), (jnp.tile(seg("i_k"), (1, 1, IDX_HEADS)), IDX_HEADS, IDX_DIM, ROT_32),
        (seg("b_q"), N_HEADS, HEAD_DIM, ROT_64), (seg("b_k"), N_HEADS, HEAD_DIM, ROT_64),
        (seg("c_q"), 2 * N_HEADS, DIFF_DIM, ROT_32), (seg("c_k"), 2 * N_HEADS, DIFF_DIM, ROT_32),
        (jnp.concatenate([seg("d_kr"), zeros(MXU_N - MLA_ROPE)], axis=-1), 1, MXU_N, MLA_ROPE),
    ]
    w_rope = jnp.concatenate([g for g, *_ in rope_groups], axis=-1).astype(BF16)
    w_rope_rot = jnp.concatenate([_rot_cols(g, nh, hd, r) for g, nh, hd, r in rope_groups], axis=-1).astype(BF16)
    cs = [_rope_cs(t, nh, hd, r) for _, nh, hd, r in rope_groups]
    ctab = jnp.concatenate([c for c, _ in cs], axis=1)
    stab = jnp.concatenate([s for _, s in cs], axis=1)
    rope_widths = (MXU_N,) * len(rope_groups)

    uq = w_uq.reshape(depth, Q_LORA, N_HEADS, MLA_NOPE + MLA_ROPE)
    qn_w, qr_w = uq[..., :MLA_NOPE], uq[..., MLA_NOPE:]
    pad32 = jnp.zeros((depth, Q_LORA, N_HEADS, LANES - MLA_NOPE - MLA_ROPE), w_uq.dtype)
    hw = N_HEADS * LANES
    wq = jnp.concatenate([qn_w, qr_w, pad32], axis=-1).reshape(depth, Q_LORA, hw).astype(BF16)
    half = MLA_ROPE // 2
    wq_rot = jnp.concatenate([jnp.zeros_like(qn_w), -qr_w[..., half:], qr_w[..., :half], pad32],
                             axis=-1).reshape(depth, Q_LORA, hw).astype(BF16)
    cos_m, sin_m = _rope_tables(t, MLA_ROPE)
    one = lambda n: jnp.ones((t, n), F32)
    zer = lambda n: jnp.zeros((t, n), F32)
    qs = (MLA_NOPE + MLA_ROPE) ** -0.5 * LOG2E
    ct_q = qs * jnp.tile(jnp.concatenate([one(MLA_NOPE), cos_m, cos_m, one(LANES - MLA_NOPE - MLA_ROPE)], axis=1), (1, N_HEADS))
    st_q = qs * jnp.tile(jnp.concatenate([zer(MLA_NOPE), sin_m, sin_m, zer(LANES - MLA_NOPE - MLA_ROPE)], axis=1), (1, N_HEADS))
    ukv = w_ukv.reshape(depth, KV_LORA, N_HEADS, MLA_NOPE + MLA_V)
    wk = jnp.concatenate([ukv[..., :MLA_NOPE], jnp.zeros((depth, KV_LORA, N_HEADS, LANES - MLA_NOPE), w_ukv.dtype)],
                         axis=-1).reshape(depth, KV_LORA, hw).astype(BF16)
    wvt = jnp.swapaxes(ukv[..., MLA_NOPE:].reshape(depth, KV_LORA, N_HEADS * MLA_V), 1, 2).astype(BF16)
    place = np.zeros((MXU_N, hw), np.float32)
    for hh in range(N_HEADS):
        for j in range(MLA_ROPE):
            place[j, hh * LANES + MLA_NOPE + j] = 1.0
    place = jnp.asarray(place, BF16)

    wg = seg("g").astype(BF16)
    wb = w_branch.astype(BF16)
    wo = w_out.astype(BF16)
    wmem = w_mem_kv.astype(BF16)
    norm_t = jnp.broadcast_to(diff_norm.astype(F32)[:, :, None], (depth, HEAD_DIM, TQ))

    h, hb = _layer_norm0(x.reshape(b * t, d), ln0_g, ln0_b)
    for l in range(depth):
        hb3 = hb.reshape(b, t, d)
        avt, bvt, cvt, dcq, ckv_iw, eq, z = _proj_plain(hb3, w_plain[l], w_vt[l], plain_widths)
        aq, ak, iq, ik, bq, bk, cq, ck, kr = _proj_rope(hb3, w_rope[l], w_rope_rot[l], ctab, stab, rope_widths)

        o_a = _dsa(aq, ak, avt, iq, ik, ckv_iw)
        o_b = _moba(bq, bk, bvt, _kbar(bk))
        lam_init = 0.8 - 0.6 * math.exp(-0.3 * l)
        misc = jnp.full((SUBLANES, LANES), lam_init, F32)
        o_c = _diff(cq, ck, cvt, diff_lam[l].astype(F32), norm_t[l], misc)
        qm, km, vmt = _mla_prep(dcq, ckv_iw, kr, mla_q_norm[l].reshape(1, Q_LORA), mla_kv_norm[l].reshape(1, KV_LORA),
                                wq[l], wq_rot[l], wk[l], wvt[l], place, ct_q, st_q)
        o_d = _mla(qm, km, vmt)
        o_e = _mem_attn(eq, _mem_kv(mem, wmem[l]))

        os5 = [o.reshape(b * t, BRANCH_W) for o in (o_a, o_b, o_c, o_d, o_e)]
        h, hb = _final(h, hb, os5, z.reshape(b * t, N_BRANCH * BRANCH_W), wg[l], wb[l], wo[l],
                       ln_g[l].reshape(1, d), ln_b[l].reshape(1, d), alpha)
    return h.reshape(b, t, d)
```

```python
import functools
import math

import numpy as np
import jax
import jax.numpy as jnp
from jax import lax
from jax.experimental import pallas as pl
from jax.experimental.pallas import tpu as pltpu

F32 = jnp.float32
BF16 = jnp.bfloat16
I32 = jnp.int32
I16 = jnp.int16

N_HEADS = 4
HEAD_DIM = 64
BRANCH_W = N_HEADS * HEAD_DIM
N_BRANCH = 5
ROPE_THETA = 500000.0
ROT_64 = 16
ROT_32 = 8
IDX_HEADS = 8
IDX_DIM = 32
TOPK_MAX = 256
MOBA_BLOCK = 256
MOBA_TOPK = 3
DIFF_DIM = 32
Q_LORA = 256
KV_LORA = 128
MLA_NOPE = 64
MLA_ROPE = 32
MLA_V = 64
LN_EPS = 1e-5
RMS_EPS = 1e-6

IN_LAYOUT = (
    ("a_q", BRANCH_W), ("a_k", BRANCH_W), ("a_v", BRANCH_W),
    ("i_q", IDX_HEADS * IDX_DIM), ("i_k", IDX_DIM), ("i_w", IDX_HEADS),
    ("b_q", BRANCH_W), ("b_k", BRANCH_W), ("b_v", BRANCH_W),
    ("c_q", BRANCH_W), ("c_k", BRANCH_W), ("c_v", BRANCH_W),
    ("d_cq", Q_LORA), ("d_ckv", KV_LORA), ("d_kr", MLA_ROPE),
    ("e_q", BRANCH_W),
    ("z", N_BRANCH * BRANCH_W),
    ("g", N_BRANCH * 1024),
)

SUBLANES = 8
LANES = 128
MXU_N = 256
TQ = 256
CK = 256
NEG = -1e30
LOG2E = math.log2(math.e)
INT_MIN = np.int32(-2 ** 31)
HALF16 = 1 << 15
VMEM_LIMIT = 56 * 1024 * 1024


def _offsets():
    off, out = 0, {}
    for name, size in IN_LAYOUT:
        out[name] = (off, size)
        off += size
    return out


OFF = _offsets()


def _nt_dot(a, b):
    return lax.dot_general(a, b, (((1,), (1,)), ((), ())), preferred_element_type=F32)


def _fold_rows(w, rows=SUBLANES):
    xs = [w[r:r + rows, :] for r in range(0, w.shape[0], rows)]
    while len(xs) > 1:
        xs = [xs[j] + xs[j + 1] for j in range(0, len(xs) - 1, 2)] + ([xs[-1]] if len(xs) % 2 else [])
    return xs[0]


def _masked_qt(q, shift, n, qt_ref):
    qt = q.T
    dim = lax.broadcasted_iota(I32, qt.shape, 0)
    for j in range(n):
        qt_ref[j] = jnp.where((dim >> shift) == j, qt, 0.0).astype(BF16)


def _cparams(n_axes):
    return pltpu.CompilerParams(dimension_semantics=("arbitrary",) * n_axes,
                                vmem_limit_bytes=VMEM_LIMIT)


def _softmax_step(s_t, vt_h, m_ref, l_ref, acc_ref):
    m_old = m_ref[...]
    m_new = jnp.maximum(m_old, jnp.max(s_t, axis=0, keepdims=True))
    alpha = jnp.exp2(m_old - m_new)
    p = jnp.exp2(s_t - m_new)
    l_ref[...] = alpha * l_ref[...] + _fold_rows(p)
    acc_ref[...] = alpha * acc_ref[...] + jnp.dot(vt_h, p.astype(BF16), preferred_element_type=F32)
    m_ref[...] = m_new


def _softmax_init(m_ref, l_ref, acc_ref):
    m_ref[...] = jnp.full(m_ref.shape, NEG, F32)
    l_ref[...] = jnp.zeros(l_ref.shape, F32)
    acc_ref[...] = jnp.zeros(acc_ref.shape, F32)


def _softmax_out(l_ref, acc_ref):
    return acc_ref[...] / jnp.sum(l_ref[...], axis=0, keepdims=True)


def _flash_loop(n_prev, qk_all, mask, vt_rows, state):
    s_ref, m_ref, l_ref, acc_ref = state
    n_state = m_ref.shape[0]
    for j in range(n_state):
        _softmax_init(m_ref.at[j], l_ref.at[j], acc_ref.at[j])

    def park(c, slot):
        for j, s in enumerate(qk_all(c)):
            s_ref[slot, j] = s

    def consume(c, slot, diag):
        for j in range(n_state):
            _softmax_step(mask(c, j, s_ref[slot, j], diag), vt_rows(c, j),
                          m_ref.at[j], l_ref.at[j], acc_ref.at[j])

    park(0, 0)

    def body(p, carry):
        c = 2 * p
        park(c + 1, 1)
        consume(c, 0, False)
        park(c + 2, 0)
        consume(c + 1, 1, False)
        return carry

    n_pair = lax.shift_right_logical(n_prev, 1)
    lax.fori_loop(0, n_pair, body, 0)
    c0 = 2 * n_pair
    odd = (n_prev & 1) == 1

    @pl.when(odd)
    def _():
        park(c0 + 1, 1)
        consume(c0, 0, False)
        consume(c0 + 1, 1, True)

    @pl.when(jnp.logical_not(odd))
    def _():
        consume(c0, 0, True)


def _attn_scratch(n_state):
    return [pltpu.VMEM((2, n_state, CK, TQ), F32), pltpu.VMEM((n_state, 1, TQ), F32),
            pltpu.VMEM((n_state, SUBLANES, TQ), F32), pltpu.VMEM((n_state, HEAD_DIM, TQ), F32),
            pltpu.VMEM((BRANCH_W, TQ), F32)]


def _kv_specs(t, w):
    kspec = pl.BlockSpec((None, t, w), lambda bb, i: (bb, 0, 0))
    vspec = pl.BlockSpec((None, t // CK, BRANCH_W, CK), lambda bb, i: (bb, 0, 0, 0))
    return kspec, vspec


def _ln_kernel(x_ref, g_ref, b_ref, h_ref, hb_ref):
    x = x_ref[...]
    mu = jnp.mean(x, axis=1, keepdims=True)
    xc = x - mu
    var = jnp.mean(xc * xc, axis=1, keepdims=True)
    y = xc * lax.rsqrt(var + LN_EPS) * g_ref[...] + b_ref[...]
    h_ref[...] = y
    hb_ref[...] = y.astype(BF16)


def _layer_norm0(x2, g, b):
    n, d = x2.shape
    tm = 512
    row = pl.BlockSpec((tm, d), lambda i: (i, 0))
    vec = pl.BlockSpec((1, d), lambda i: (0, 0))
    return pl.pallas_call(
        _ln_kernel,
        out_shape=(jax.ShapeDtypeStruct((n, d), F32), jax.ShapeDtypeStruct((n, d), BF16)),
        grid=(n // tm,),
        in_specs=[row, vec, vec],
        out_specs=(row, row),
        compiler_params=_cparams(1),
        name="ln0",
    )(x2, g.reshape(1, d), b.reshape(1, d))


def _proj_plain_kernel(x_ref, w_ref, wt_ref, *out_refs, n_t):
    for g, o_ref in enumerate(out_refs[:n_t]):
        vt = _nt_dot(wt_ref[g * BRANCH_W:(g + 1) * BRANCH_W, :], x_ref[...])
        for j in range(o_ref.shape[0]):
            o_ref[j] = vt[:, j * CK:(j + 1) * CK].astype(o_ref.dtype)
    off = 0
    for o_ref in out_refs[n_t:]:
        wd = o_ref.shape[-1]
        for j in range(0, wd, MXU_N):
            acc = jnp.dot(x_ref[...], w_ref[:, off + j:off + j + MXU_N], preferred_element_type=F32)
            o_ref[:, j:j + MXU_N] = acc.astype(o_ref.dtype)
        off += wd


def _proj_plain(hb3, w, wt, widths):
    b, t, d = hb3.shape
    tm = 512
    n_t = wt.shape[0] // BRANCH_W
    shapes = [jax.ShapeDtypeStruct((b, t // CK, BRANCH_W, CK), BF16)] * n_t
    specs = [pl.BlockSpec((None, tm // CK, BRANCH_W, CK), lambda i, bb: (bb, i, 0, 0))] * n_t
    shapes += [jax.ShapeDtypeStruct((b, t, wd), BF16) for wd in widths]
    specs += [pl.BlockSpec((None, tm, wd), lambda i, bb: (bb, i, 0)) for wd in widths]
    return pl.pallas_call(
        functools.partial(_proj_plain_kernel, n_t=n_t),
        out_shape=tuple(shapes),
        grid=(t // tm, b),
        in_specs=[pl.BlockSpec((None, tm, d), lambda i, bb: (bb, i, 0)),
                  pl.BlockSpec(w.shape, lambda i, bb: (0, 0)),
                  pl.BlockSpec(wt.shape, lambda i, bb: (0, 0))],
        out_specs=tuple(specs),
        compiler_params=_cparams(2),
        name="proj_plain",
    )(hb3, w, wt)


def _proj_rope_kernel(x_ref, w_ref, wr_ref, c_ref, s_ref, *out_refs):
    off = 0
    for o_ref in out_refs:
        wd = o_ref.shape[-1]
        for j in range(0, wd, MXU_N):
            sl = slice(off + j, off + j + MXU_N)
            acc = jnp.dot(x_ref[...], w_ref[:, sl], preferred_element_type=F32)
            rot = jnp.dot(x_ref[...], wr_ref[:, sl], preferred_element_type=F32)
            o_ref[:, j:j + MXU_N] = (acc * c_ref[:, sl] + rot * s_ref[:, sl]).astype(o_ref.dtype)
        off += wd


def _proj_rope(hb3, w, wr, ctab, stab, widths):
    b, t, d = hb3.shape
    tm = 256
    ncol = w.shape[1]
    wspec = pl.BlockSpec((d, ncol), lambda i, bb: (0, 0))
    tspec = pl.BlockSpec((tm, ncol), lambda i, bb: (i, 0))
    return pl.pallas_call(
        _proj_rope_kernel,
        out_shape=tuple(jax.ShapeDtypeStruct((b, t, wd), BF16) for wd in widths),
        grid=(t // tm, b),
        in_specs=[pl.BlockSpec((None, tm, d), lambda i, bb: (bb, i, 0)), wspec, wspec, tspec, tspec],
        out_specs=tuple(pl.BlockSpec((None, tm, wd), lambda i, bb: (bb, i, 0)) for wd in widths),
        compiler_params=_cparams(2),
        name="proj_rope",
    )(hb3, w, wr, ctab, stab)


def _dsa_kernel(aq_ref, ak_ref, avt_ref, iq_ref, ik_ref, iw_ref, pick_ref, o_ref,
                keys_ref, hi_ref, lo_ref, iqt_ref, aqt_ref, wt_ref, thr_ref, s_ref, m_ref, l_ref, acc_ref, ot_ref,
                *, topk, idx_scale, pos_bits):
    i = pl.program_id(1)
    nk = i + 1
    kpos = lax.broadcasted_iota(I32, (CK, TQ), 0)
    qpos = lax.broadcasted_iota(I32, (CK, TQ), 1)

    _masked_qt(iq_ref[...].astype(F32), 5, IDX_HEADS, iqt_ref)
    _masked_qt(aq_ref[...].astype(F32) * (HEAD_DIM ** -0.5 * LOG2E), 6, N_HEADS, aqt_ref)
    wt_ref[...] = _nt_dot(pick_ref[...], iw_ref[...])

    def score_body(c, carry):
        kc = ik_ref[pl.ds(pl.multiple_of(c * CK, CK), CK), :]
        acc = jnp.zeros((CK, TQ), F32)
        for hh in range(IDX_HEADS):
            logit = jnp.dot(kc, iqt_ref[hh], preferred_element_type=F32)
            acc = acc + jnp.maximum(logit, 0.0) * wt_ref[hh:hh + 1, :]
        sc = acc * idx_scale
        bits = pltpu.bitcast(sc, I32)
        key = jnp.where(bits < 0, INT_MIN - bits, bits)
        causal = (c * CK + kpos) <= (i * TQ + qpos)
        key = jnp.where(causal, key, INT_MIN)
        keys_ref[c] = key
        hi_ref[c] = (key >> 16).astype(I16)
        lo_ref[c] = ((key & 0xFFFF) - HALF16).astype(I16)
        return carry

    lax.fori_loop(0, nk, score_body, 0)

    def count16(ref, pred):
        def body(c, part):
            return part + _fold_rows(jnp.where(pred(ref[c]), jnp.int16(1), jnp.int16(0)), 2 * SUBLANES)
        part = lax.fori_loop(0, nk, body, jnp.zeros((2 * SUBLANES, TQ), I16))
        return jnp.sum(part.astype(F32), axis=0, keepdims=True)

    def search16(ref, need):
        def bit_body(bi, t_u):
            c_u = t_u | jnp.left_shift(jnp.int32(1), 15 - bi)
            ck = (c_u - HALF16).astype(I16)
            cnt = count16(ref, lambda v: v >= ck)
            return jnp.where(cnt >= need, c_u, t_u)
        return lax.fori_loop(0, 16, bit_body, jnp.zeros((1, TQ), I32))

    hi_u = search16(hi_ref, float(topk))
    thr_hi = (hi_u - HALF16).astype(I16)
    n_above = count16(hi_ref, lambda v: v > thr_hi)

    def bucket_body(c, carry):
        hi_ref[c] = jnp.where(hi_ref[c] == thr_hi, lo_ref[c], jnp.int16(-HALF16))
        return carry

    lax.fori_loop(0, nk, bucket_body, 0)
    lo_u = search16(hi_ref, float(topk) - n_above)
    thr = ((hi_u - HALF16) << 16) | lo_u

    def count(pred):
        def body(c, part):
            return part + _fold_rows(jnp.where(pred(keys_ref[c], c), 1.0, 0.0))
        part = lax.fori_loop(0, nk, body, jnp.zeros((SUBLANES, TQ), F32))
        return jnp.sum(part, axis=0, keepdims=True)

    n_gt = count(lambda k, c: k > thr)
    n_eq = count(lambda k, c: k == thr)
    need = float(topk) - n_gt
    amb = jnp.logical_and(n_eq > need, thr > INT_MIN)
    any_amb = jnp.max(jnp.where(amb, 1.0, 0.0)) > 0.5

    @pl.when(any_amb)
    def _():
        def pos_body(bi, r):
            cand = r + jnp.left_shift(jnp.int32(1), pos_bits - 1 - bi)
            cnt = count(lambda k, c: jnp.logical_and(k == thr, (c * CK + kpos) < cand))
            return jnp.where(cnt < need, cand, r)

        r = lax.fori_loop(0, pos_bits, pos_body, jnp.zeros((1, TQ), I32))
        r = jnp.where(amb, r, jnp.int32(2 ** 30))

        def drop_body(c, carry):
            k = keys_ref[c]
            drop = jnp.logical_and(k == thr, (c * CK + kpos) > r)
            keys_ref[c] = jnp.where(drop, INT_MIN, k)
            return carry

        lax.fori_loop(0, nk, drop_body, 0)

    thr_ref[...] = jnp.maximum(thr, INT_MIN + 1)

    def qk_all(c):
        kc = ak_ref[pl.ds(pl.multiple_of(c * CK, CK), CK), :]
        return [jnp.dot(kc, aqt_ref[h], preferred_element_type=F32) for h in range(N_HEADS)]

    _flash_loop(i, qk_all,
                lambda c, h, s, diag: jnp.where(keys_ref[c] >= thr_ref[...], s, NEG),
                lambda c, h: avt_ref[c, h * HEAD_DIM:(h + 1) * HEAD_DIM, :],
                (s_ref, m_ref, l_ref, acc_ref))
    for h in range(N_HEADS):
        ot_ref[h * HEAD_DIM:(h + 1) * HEAD_DIM, :] = _softmax_out(l_ref.at[h], acc_ref.at[h])
    o_ref[...] = ot_ref[...].T.astype(o_ref.dtype)


def _dsa(aq, ak, avt, iq, ik, iw):
    b, t, _ = aq.shape
    topk = min(TOPK_MAX, t // 4)
    qspec = pl.BlockSpec((None, TQ, BRANCH_W), lambda bb, i: (bb, i, 0))
    kspec, vspec = _kv_specs(t, BRANCH_W)
    pick = np.zeros((2 * SUBLANES, MXU_N), np.float32)
    for hh in range(IDX_HEADS):
        pick[hh, KV_LORA + hh] = 1.0
    pick = jnp.asarray(pick, BF16)
    kern = functools.partial(_dsa_kernel, topk=topk, idx_scale=(IDX_HEADS * IDX_DIM) ** -0.5,
                             pos_bits=max(1, (t - 1).bit_length()))
    return pl.pallas_call(
        kern,
        out_shape=jax.ShapeDtypeStruct((b, t, BRANCH_W), BF16),
        grid=(b, t // TQ),
        in_specs=[qspec, kspec, vspec, qspec, kspec, qspec,
                  pl.BlockSpec(pick.shape, lambda bb, i: (0, 0))],
        out_specs=qspec,
        scratch_shapes=[
            pltpu.VMEM((t // CK, CK, TQ), I32),
            pltpu.VMEM((t // CK, CK, TQ), I16),
            pltpu.VMEM((t // CK, CK, TQ), I16),
            pltpu.VMEM((IDX_HEADS, BRANCH_W, TQ), BF16),
            pltpu.VMEM((N_HEADS, BRANCH_W, TQ), BF16),
            pltpu.VMEM((2 * SUBLANES, TQ), F32),
            pltpu.VMEM((1, TQ), I32),
        ] + _attn_scratch(N_HEADS),
        compiler_params=_cparams(2),
        name="dsa",
    )(aq, ak, avt, iq, ik, iw, pick)


def _kbar_kernel(k_ref, o_ref):
    o_ref[...] = jnp.zeros(o_ref.shape, o_ref.dtype)
    nb = k_ref.shape[0] // MOBA_BLOCK
    for n in range(nb):
        blk = k_ref[n * MOBA_BLOCK:(n + 1) * MOBA_BLOCK, :].astype(F32)
        o_ref[n:n + 1, :] = jnp.mean(blk, axis=0, keepdims=True).astype(o_ref.dtype)


def _kbar(bk):
    b, t, w = bk.shape
    nbp = max(2 * SUBLANES, t // MOBA_BLOCK)
    return pl.pallas_call(
        _kbar_kernel,
        out_shape=jax.ShapeDtypeStruct((b, nbp, w), BF16),
        grid=(b,),
        in_specs=[pl.BlockSpec((None, t, w), lambda bb: (bb, 0, 0))],
        out_specs=pl.BlockSpec((None, nbp, w), lambda bb: (bb, 0, 0)),
        compiler_params=_cparams(1),
        name="moba_kbar",
    )(bk)


def _moba_kernel(q_ref, k_ref, vt_ref, kbar_ref, o_ref, qt_ref, bias_ref, s_ref, m_ref, l_ref, acc_ref, ot_ref):
    i = pl.program_id(1)
    nbp = kbar_ref.shape[0]
    blk = lax.broadcasted_iota(I32, (nbp, TQ), 0)
    blk_f = blk.astype(F32)
    kpos = lax.broadcasted_iota(I32, (CK, TQ), 0)
    qpos = lax.broadcasted_iota(I32, (CK, TQ), 1)
    _masked_qt(q_ref[...].astype(F32) * (HEAD_DIM ** -0.5 * LOG2E), 6, N_HEADS, qt_ref)

    for h in range(N_HEADS):
        g = jnp.where(blk < i, jnp.dot(kbar_ref[...], qt_ref[h], preferred_element_type=F32), NEG)
        bias = jnp.full((nbp, TQ), NEG, F32)
        for _ in range(MOBA_TOPK):
            mx = jnp.max(g, axis=0, keepdims=True)
            first = jnp.min(jnp.where(g == mx, blk_f, 1e9), axis=0, keepdims=True)
            pick = jnp.logical_and(blk_f == first, mx > 0.5 * NEG)
            bias = jnp.where(pick, 0.0, bias)
            g = jnp.where(pick, NEG, g)
        bias_ref[h] = bias

    def qk_all(c):
        kc = k_ref[pl.ds(pl.multiple_of(c * CK, CK), CK), :]
        return [jnp.dot(kc, qt_ref[h], preferred_element_type=F32) for h in range(N_HEADS)]

    def mask(c, h, s, diag):
        return jnp.where(kpos <= qpos, s, NEG) if diag else s + bias_ref[h, pl.ds(c, 1), :]

    _flash_loop(i, qk_all, mask, lambda c, h: vt_ref[c, h * HEAD_DIM:(h + 1) * HEAD_DIM, :],
                (s_ref, m_ref, l_ref, acc_ref))
    for h in range(N_HEADS):
        ot_ref[h * HEAD_DIM:(h + 1) * HEAD_DIM, :] = _softmax_out(l_ref.at[h], acc_ref.at[h])
    o_ref[...] = ot_ref[...].T.astype(o_ref.dtype)


def _moba(bq, bk, bvt, kbar):
    b, t, w = bq.shape
    assert TQ == MOBA_BLOCK and CK == MOBA_BLOCK and t % MOBA_BLOCK == 0
    nbp = kbar.shape[1]
    qspec = pl.BlockSpec((None, TQ, w), lambda bb, i: (bb, i, 0))
    kspec, vspec = _kv_specs(t, w)
    return pl.pallas_call(
        _moba_kernel,
        out_shape=jax.ShapeDtypeStruct((b, t, w), BF16),
        grid=(b, t // TQ),
        in_specs=[qspec, kspec, vspec, pl.BlockSpec((None, nbp, w), lambda bb, i: (bb, 0, 0))],
        out_specs=qspec,
        scratch_shapes=[pltpu.VMEM((N_HEADS, w, TQ), BF16), pltpu.VMEM((N_HEADS, nbp, TQ), F32)]
        + _attn_scratch(N_HEADS),
        compiler_params=_cparams(2),
        name="moba",
    )(bq, bk, bvt, kbar)


def _diff_kernel(q_ref, k_ref, vt_ref, lam_ref, norm_ref, misc_ref, o_ref,
                 qt_ref, s_ref, m_ref, l_ref, acc_ref, ot_ref):
    i = pl.program_id(1)
    kpos = lax.broadcasted_iota(I32, (CK, TQ), 0)
    qpos = lax.broadcasted_iota(I32, (CK, TQ), 1)
    _masked_qt(q_ref[...].astype(F32) * (DIFF_DIM ** -0.5 * LOG2E), 5, 2 * N_HEADS, qt_ref)

    dl = lam_ref[...]
    lam_init = misc_ref[0:1, 0:1]
    lam = (jnp.exp(jnp.sum(dl[0:1, :] * dl[1:2, :], axis=1, keepdims=True))
           - jnp.exp(jnp.sum(dl[2:3, :] * dl[3:4, :], axis=1, keepdims=True)) + lam_init)

    def qk_all(c):
        kc = k_ref[pl.ds(pl.multiple_of(c * CK, CK), CK), :]
        return [jnp.dot(kc, qt_ref[j], preferred_element_type=F32) for j in range(2 * N_HEADS)]

    _flash_loop(i, qk_all,
                lambda c, j, s, diag: jnp.where(kpos <= qpos, s, NEG) if diag else s,
                lambda c, j: vt_ref[c, (j // 2) * HEAD_DIM:(j // 2 + 1) * HEAD_DIM, :],
                (s_ref, m_ref, l_ref, acc_ref))

    post = norm_ref[...] * (1.0 - lam_init)
    for h in range(N_HEADS):
        o_h = (_softmax_out(l_ref.at[2 * h], acc_ref.at[2 * h])
               - lam * _softmax_out(l_ref.at[2 * h + 1], acc_ref.at[2 * h + 1]))
        ms = jnp.mean(o_h * o_h, axis=0, keepdims=True)
        ot_ref[h * HEAD_DIM:(h + 1) * HEAD_DIM, :] = o_h * lax.rsqrt(ms + RMS_EPS) * post
    o_ref[...] = ot_ref[...].T.astype(o_ref.dtype)


def _diff(cq, ck, cvt, lam, norm, misc):
    b, t, w = cq.shape
    qspec = pl.BlockSpec((None, TQ, w), lambda bb, i: (bb, i, 0))
    kspec, vspec = _kv_specs(t, w)
    full = lambda a: pl.BlockSpec(a.shape, lambda bb, i: (0,) * a.ndim)
    return pl.pallas_call(
        _diff_kernel,
        out_shape=jax.ShapeDtypeStruct((b, t, w), BF16),
        grid=(b, t // TQ),
        in_specs=[qspec, kspec, vspec, full(lam), full(norm), full(misc)],
        out_specs=qspec,
        scratch_shapes=[pltpu.VMEM((2 * N_HEADS, w, TQ), BF16)] + _attn_scratch(2 * N_HEADS),
        compiler_params=_cparams(2),
        name="diff",
    )(cq, ck, cvt, lam, norm, misc)


def _mla_prep_kernel(cq_ref, ckv_ref, kr_ref, qn_ref, kvn_ref, wq_ref, wqr_ref, wk_ref, wvt_ref,
                     p_ref, ct_ref, st_ref, q_out, k_out, vt_out):
    x = cq_ref[...].astype(F32)
    xn = (x * lax.rsqrt(jnp.mean(x * x, axis=1, keepdims=True) + RMS_EPS) * qn_ref[...]).astype(BF16)
    q = (jnp.dot(xn, wq_ref[...], preferred_element_type=F32) * ct_ref[...]
         + jnp.dot(xn, wqr_ref[...], preferred_element_type=F32) * st_ref[...])
    q_out[...] = q.astype(q_out.dtype)
    c = ckv_ref[:, :KV_LORA].astype(F32)
    cn = (c * lax.rsqrt(jnp.mean(c * c, axis=1, keepdims=True) + RMS_EPS) * kvn_ref[...]).astype(BF16)
    k = (jnp.dot(cn, wk_ref[...], preferred_element_type=F32)
         + jnp.dot(kr_ref[...], p_ref[...], preferred_element_type=F32))
    k_out[...] = k.astype(k_out.dtype)
    vt = _nt_dot(wvt_ref[...], cn)
    for j in range(vt_out.shape[0]):
        vt_out[j] = vt[:, j * CK:(j + 1) * CK].astype(vt_out.dtype)


def _mla_prep(dcq, ckv, kr, qn, kvn, wq, wqr, wk, wvt, pmat, ct, st):
    b, t, _ = dcq.shape
    tm = 512
    hw = N_HEADS * LANES
    row = lambda w: pl.BlockSpec((None, tm, w), lambda i, bb: (bb, i, 0))
    full = lambda a: pl.BlockSpec(a.shape, lambda i, bb: (0,) * a.ndim)
    tab = pl.BlockSpec((tm, hw), lambda i, bb: (i, 0))
    return pl.pallas_call(
        _mla_prep_kernel,
        out_shape=(jax.ShapeDtypeStruct((b, t, hw), BF16), jax.ShapeDtypeStruct((b, t, hw), BF16),
                   jax.ShapeDtypeStruct((b, t // CK, BRANCH_W, CK), BF16)),
        grid=(t // tm, b),
        in_specs=[row(Q_LORA), row(MXU_N), row(MXU_N), full(qn), full(kvn), full(wq), full(wqr),
                  full(wk), full(wvt), full(pmat), tab, tab],
        out_specs=(row(hw), row(hw),
                   pl.BlockSpec((None, tm // CK, BRANCH_W, CK), lambda i, bb: (bb, i, 0, 0))),
        compiler_params=_cparams(2),
        name="mla_prep",
    )(dcq, ckv, kr, qn, kvn, wq, wqr, wk, wvt, pmat, ct, st)


def _mla_kernel(q_ref, k_ref, vt_ref, o_ref, qt_ref, s_ref, m_ref, l_ref, acc_ref, ot_ref):
    i = pl.program_id(1)
    kpos = lax.broadcasted_iota(I32, (CK, TQ), 0)
    qpos = lax.broadcasted_iota(I32, (CK, TQ), 1)
    hs = [slice(h * LANES, (h + 1) * LANES) for h in range(N_HEADS)]
    for h in range(N_HEADS):
        qt_ref[h] = q_ref[:, hs[h]].astype(F32).T.astype(BF16)

    def qk_all(c):
        start = pl.multiple_of(c * CK, CK)
        return [jnp.dot(k_ref[pl.ds(start, CK), hs[h]], qt_ref[h], preferred_element_type=F32)
                for h in range(N_HEADS)]

    _flash_loop(i, qk_all,
                lambda c, h, s, diag: jnp.where(kpos <= qpos, s, NEG) if diag else s,
                lambda c, h: vt_ref[c, h * HEAD_DIM:(h + 1) * HEAD_DIM, :],
                (s_ref, m_ref, l_ref, acc_ref))
    for h in range(N_HEADS):
        ot_ref[h * HEAD_DIM:(h + 1) * HEAD_DIM, :] = _softmax_out(l_ref.at[h], acc_ref.at[h])
    o_ref[...] = ot_ref[...].T.astype(o_ref.dtype)


def _mla(qm, km, vmt):
    b, t, hw = qm.shape
    kspec, vspec = _kv_specs(t, hw)
    return pl.pallas_call(
        _mla_kernel,
        out_shape=jax.ShapeDtypeStruct((b, t, BRANCH_W), BF16),
        grid=(b, t // TQ),
        in_specs=[pl.BlockSpec((None, TQ, hw), lambda bb, i: (bb, i, 0)), kspec, vspec],
        out_specs=pl.BlockSpec((None, TQ, BRANCH_W), lambda bb, i: (bb, i, 0)),
        scratch_shapes=[pltpu.VMEM((N_HEADS, LANES, TQ), BF16)] + _attn_scratch(N_HEADS),
        compiler_params=_cparams(2),
        name="mla",
    )(qm, km, vmt)


def _matmul_kernel(x_ref, w_ref, o_ref):
    o_ref[...] = jnp.dot(x_ref[...].astype(BF16), w_ref[...], preferred_element_type=F32).astype(o_ref.dtype)


def _mem_kv(mem, w):
    b, m, d = mem.shape
    n = w.shape[1]
    return pl.pallas_call(
        _matmul_kernel,
        out_shape=jax.ShapeDtypeStruct((b, m, n), BF16),
        grid=(b,),
        in_specs=[pl.BlockSpec((None, m, d), lambda bb: (bb, 0, 0)), pl.BlockSpec((d, n), lambda bb: (0, 0))],
        out_specs=pl.BlockSpec((None, m, n), lambda bb: (bb, 0, 0)),
        compiler_params=_cparams(1),
        name="mem_kv",
    )(mem, w)


def _mem_kernel(q_ref, kv_ref, o_ref):
    tq = q_ref.shape[0]
    lane_q = lax.broadcasted_iota(I32, (tq, BRANCH_W), 1)
    q = q_ref[...].astype(F32) * (HEAD_DIM ** -0.5)
    mk = kv_ref[:, :BRANCH_W]
    mv = kv_ref[:, BRANCH_W:]
    out = jnp.zeros((tq, BRANCH_W), F32)
    for h in range(N_HEADS):
        in_h = (lane_q >> 6) == h
        s = _nt_dot(jnp.where(in_h, q, 0.0).astype(BF16), mk)
        p = jnp.exp(s - jnp.max(s, axis=1, keepdims=True))
        o_h = jnp.dot(p.astype(BF16), mv, preferred_element_type=F32) / jnp.sum(p, axis=1, keepdims=True)
        out = jnp.where(in_h, o_h, out)
    o_ref[...] = out.astype(o_ref.dtype)


def _mem_attn(eq, mkv):
    b, t, w = eq.shape
    m = mkv.shape[1]
    tq = 512
    return pl.pallas_call(
        _mem_kernel,
        out_shape=jax.ShapeDtypeStruct((b, t, w), BF16),
        grid=(b, t // tq),
        in_specs=[pl.BlockSpec((None, tq, w), lambda bb, i: (bb, i, 0)),
                  pl.BlockSpec((None, m, 2 * w), lambda bb, i: (bb, 0, 0))],
        out_specs=pl.BlockSpec((None, tq, w), lambda bb, i: (bb, i, 0)),
        compiler_params=_cparams(2),
        name="mem_attn",
    )(eq, mkv)


def _final_kernel(h_ref, hb_ref, oa_ref, ob_ref, oc_ref, od_ref, oe_ref, z_ref,
                  wg_ref, wb_ref, wo_ref, g_ref, b_ref, h_out, hb_out, *, alpha):
    d = h_ref.shape[1]
    acc = jnp.zeros(h_ref.shape, F32)
    for n, o_ref in enumerate((oa_ref, ob_ref, oc_ref, od_ref, oe_ref)):
        z = z_ref[:, n * BRANCH_W:(n + 1) * BRANCH_W].astype(F32)
        y = o_ref[...].astype(F32) * (z / (1.0 + jnp.exp(-z)))
        u = jnp.dot(y.astype(BF16), wb_ref[n], preferred_element_type=F32)
        g = jnp.dot(hb_ref[...], wg_ref[:, n * d:(n + 1) * d], preferred_element_type=F32)
        acc = acc + u / (1.0 + jnp.exp(-g))
    out = jnp.dot(acc.astype(BF16), wo_ref[...], preferred_element_type=F32)
    x = alpha * h_ref[...] + out
    mu = jnp.mean(x, axis=1, keepdims=True)
    xc = x - mu
    var = jnp.mean(xc * xc, axis=1, keepdims=True)
    y = xc * lax.rsqrt(var + LN_EPS) * g_ref[...] + b_ref[...]
    h_out[...] = y
    hb_out[...] = y.astype(BF16)


def _final(h, hb, os5, z, wg, wb, wo, ln_g, ln_b, alpha):
    n, d = h.shape
    tm = 256
    row = lambda w: pl.BlockSpec((tm, w), lambda i: (i, 0))
    full = lambda a: pl.BlockSpec(a.shape, lambda i: (0,) * a.ndim)
    return pl.pallas_call(
        functools.partial(_final_kernel, alpha=alpha),
        out_shape=(jax.ShapeDtypeStruct((n, d), F32), jax.ShapeDtypeStruct((n, d), BF16)),
        grid=(n // tm,),
        in_specs=[row(d), row(d)] + [row(BRANCH_W)] * N_BRANCH + [row(N_BRANCH * BRANCH_W),
                  full(wg), full(wb), full(wo), full(ln_g), full(ln_b)],
        out_specs=(row(d), row(d)),
        compiler_params=_cparams(1),
        name="merge_out_ln",
    )(h, hb, *os5, z, wg, wb, wo, ln_g, ln_b)


def _rope_tables(seq, rot_dim):
    pos = jnp.arange(seq, dtype=F32)
    inv = ROPE_THETA ** (-jnp.arange(0, rot_dim, 2, dtype=F32) / rot_dim)
    ang = pos[:, None] * inv[None, :]
    return jnp.cos(ang), jnp.sin(ang)


def _rot_cols(wg, nh, hd, r):
    lead = wg.shape[:-1]
    w4 = wg.reshape(lead + (nh, hd))
    half = r // 2
    parts = [-w4[..., half:r], w4[..., :half]]
    if hd > r:
        parts.append(jnp.zeros(lead + (nh, hd - r), wg.dtype))
    return jnp.concatenate(parts, axis=-1).reshape(lead + (nh * hd,))


def _rope_cs(t, nh, hd, r):
    cos, sin = _rope_tables(t, r)
    c = jnp.concatenate([cos, cos, jnp.ones((t, hd - r), F32)], axis=1)
    s = jnp.concatenate([sin, sin, jnp.zeros((t, hd - r), F32)], axis=1)
    return jnp.tile(c, (1, nh)), jnp.tile(s, (1, nh))


def kernel(x, mem, ln0_g, ln0_b, w_in, mla_q_norm, w_uq, mla_kv_norm, w_ukv, diff_lam, diff_norm,
           w_mem_kv, w_branch, w_out, ln_g, ln_b):
    b, t, d = x.shape
    depth = w_in.shape[0]
    alpha = (2 * depth) ** 0.25
    assert t % 512 == 0 and d == 1024

    def seg(name):
        o, s = OFF[name]
        return w_in[:, :, o:o + s]

    zeros = lambda n: jnp.zeros((depth, d, n), w_in.dtype)

    w_plain = jnp.concatenate(
        [seg("d_cq"), seg("d_ckv"), seg("i_w"), zeros(MXU_N - KV_LORA - IDX_HEADS), seg("e_q"), seg("z")],
        axis=-1).astype(BF16)
    plain_widths = (BRANCH_W,) * 3 + (N_BRANCH * BRANCH_W,)
    w_vt = jnp.swapaxes(jnp.concatenate([seg("a_v"), seg("b_v"), seg("c_v")], axis=-1), 1, 2).astype(BF16)

    rope_groups = [
        (seg("a_q"), N_HEADS, HEAD_DIM, ROT_64), (seg("a_k"), N_HEADS, HEAD_DIM, ROT_64),
        (seg("i_q"), IDX_HEADS, IDX_DIM, ROT_32), (jnp.tile(seg("i_k"), (1, 1, IDX_HEADS)), IDX_HEADS, IDX_DIM, ROT_32),
        (seg("b_q"), N_HEADS, HEAD_DIM, ROT_64), (seg("b_k"), N_HEADS, HEAD_DIM, ROT_64),
        (seg("c_q"), 2 * N_HEADS, DIFF_DIM, ROT_32), (seg("c_k"), 2 * N_HEADS, DIFF_DIM, ROT_32),
        (jnp.concatenate([seg("d_kr"), zeros(MXU_N - MLA_ROPE)], axis=-1), 1, MXU_N, MLA_ROPE),
    ]
    w_rope = jnp.concatenate([g for g, *_ in rope_groups], axis=-1).astype(BF16)
    w_rope_rot = jnp.concatenate([_rot_cols(g, nh, hd, r) for g, nh, hd, r in rope_groups], axis=-1).astype(BF16)
    cs = [_rope_cs(t, nh, hd, r) for _, nh, hd, r in rope_groups]
    ctab = jnp.concatenate([c for c, _ in cs], axis=1)
    stab = jnp.concatenate([s for _, s in cs], axis=1)
    rope_widths = (MXU_N,) * len(rope_groups)

    uq = w_uq.reshape(depth, Q_LORA, N_HEADS, MLA_NOPE + MLA_ROPE)
    qn_w, qr_w = uq[..., :MLA_NOPE], uq[..., MLA_NOPE:]
    pad32 = jnp.zeros((depth, Q_LORA, N_HEADS, LANES - MLA_NOPE - MLA_ROPE), w_uq.dtype)
    hw = N_HEADS * LANES
    wq = jnp.concatenate([qn_w, qr_w, pad32], axis=-1).reshape(depth, Q_LORA, hw).astype(BF16)
    half = MLA_ROPE // 2
    wq_rot = jnp.concatenate([jnp.zeros_like(qn_w), -qr_w[..., half:], qr_w[..., :half], pad32],
                             axis=-1).reshape(depth, Q_LORA, hw).astype(BF16)
    cos_m, sin_m = _rope_tables(t, MLA_ROPE)
    one = lambda n: jnp.ones((t, n), F32)
    zer = lambda n: jnp.zeros((t, n), F32)
    qs = (MLA_NOPE + MLA_ROPE) ** -0.5 * LOG2E
    ct_q = qs * jnp.tile(jnp.concatenate([one(MLA_NOPE), cos_m, cos_m, one(LANES - MLA_NOPE - MLA_ROPE)], axis=1), (1, N_HEADS))
    st_q = qs * jnp.tile(jnp.concatenate([zer(MLA_NOPE), sin_m, sin_m, zer(LANES - MLA_NOPE - MLA_ROPE)], axis=1), (1, N_HEADS))
    ukv = w_ukv.reshape(depth, KV_LORA, N_HEADS, MLA_NOPE + MLA_V)
    wk = jnp.concatenate([ukv[..., :MLA_NOPE], jnp.zeros((depth, KV_LORA, N_HEADS, LANES - MLA_NOPE), w_ukv.dtype)],
                         axis=-1).reshape(depth, KV_LORA, hw).astype(BF16)
    wvt = jnp.swapaxes(ukv[..., MLA_NOPE:].reshape(depth, KV_LORA, N_HEADS * MLA_V), 1, 2).astype(BF16)
    place = np.zeros((MXU_N, hw), np.float32)
    for hh in range(N_HEADS):
        for j in range(MLA_ROPE):
            place[j, hh * LANES + MLA_NOPE + j] = 1.0
    place = jnp.asarray(place, BF16)

    wg = seg("g").astype(BF16)
    wb = w_branch.astype(BF16)
    wo = w_out.astype(BF16)
    wmem = w_mem_kv.astype(BF16)
    norm_t = jnp.broadcast_to(diff_norm.astype(F32)[:, :, None], (depth, HEAD_DIM, TQ))

    h, hb = _layer_norm0(x.reshape(b * t, d), ln0_g, ln0_b)
    for l in range(depth):
        hb3 = hb.reshape(b, t, d)
        avt, bvt, cvt, dcq, ckv_iw, eq, z = _proj_plain(hb3, w_plain[l], w_vt[l], plain_widths)
        aq, ak, iq, ik, bq, bk, cq, ck, kr = _proj_rope(hb3, w_rope[l], w_rope_rot[l], ctab, stab, rope_widths)

        o_a = _dsa(aq, ak, avt, iq, ik, ckv_iw)
        o_b = _moba(bq, bk, bvt, _kbar(bk))
        lam_init = 0.8 - 0.6 * math.exp(-0.3 * l)
        misc = jnp.full((SUBLANES, LANES), lam_init, F32)
        o_c = _diff(cq, ck, cvt, diff_lam[l].astype(F32), norm_t[l], misc)
        qm, km, vmt = _mla_prep(dcq, ckv_iw, kr, mla_q_norm[l].reshape(1, Q_LORA), mla_kv_norm[l].reshape(1, KV_LORA),
                                wq[l], wq_rot[l], wk[l], wvt[l], place, ct_q, st_q)
        o_d = _mla(qm, km, vmt)
        o_e = _mem_attn(eq, _mem_kv(mem, wmem[l]))

        os5 = [o.reshape(b * t, BRANCH_W) for o in (o_a, o_b, o_c, o_d, o_e)]
        h, hb = _final(h, hb, os5, z.reshape(b * t, N_BRANCH * BRANCH_W), wg[l], wb[l], wo[l],
                       ln_g[l].reshape(1, d), ln_b[l].reshape(1, d), alpha)
    return h.reshape(b, t, d)
```

```python
import functools
import math

import numpy as np
import jax
import jax.numpy as jnp
from jax import lax
from jax.experimental import pallas as pl
from jax.experimental.pallas import tpu as pltpu

F32 = jnp.float32
BF16 = jnp.bfloat16
I32 = jnp.int32
I16 = jnp.int16

N_HEADS = 4
HEAD_DIM = 64
BRANCH_W = N_HEADS * HEAD_DIM
N_BRANCH = 5
ROPE_THETA = 500000.0
ROT_64 = 16
ROT_32 = 8
IDX_HEADS = 8
IDX_DIM = 32
TOPK_MAX = 256
MOBA_BLOCK = 256
MOBA_TOPK = 3
DIFF_DIM = 32
Q_LORA = 256
KV_LORA = 128
MLA_NOPE = 64
MLA_ROPE = 32
MLA_V = 64
LN_EPS = 1e-5
RMS_EPS = 1e-6

IN_LAYOUT = (
    ("a_q", BRANCH_W), ("a_k", BRANCH_W), ("a_v", BRANCH_W),
    ("i_q", IDX_HEADS * IDX_DIM), ("i_k", IDX_DIM), ("i_w", IDX_HEADS),
    ("b_q", BRANCH_W), ("b_k", BRANCH_W), ("b_v", BRANCH_W),
    ("c_q", BRANCH_W), ("c_k", BRANCH_W), ("c_v", BRANCH_W),
    ("d_cq", Q_LORA), ("d_ckv", KV_LORA), ("d_kr", MLA_ROPE),
    ("e_q", BRANCH_W),
    ("z", N_BRANCH * BRANCH_W),
    ("g", N_BRANCH * 1024),
)

SUBLANES = 8
LANES = 128
MXU_N = 256
TQ = 256
CK = 256
VROWS = HEAD_DIM + 16
NEG = -1e30
LOG2E = math.log2(math.e)
INT_MIN = np.int32(-2 ** 31)
HALF16 = 1 << 15
VMEM_LIMIT = 56 * 1024 * 1024


def _offsets():
    off, out = 0, {}
    for name, size in IN_LAYOUT:
        out[name] = (off, size)
        off += size
    return out


OFF = _offsets()


def _nt_dot(a, b):
    return lax.dot_general(a, b, (((1,), (1,)), ((), ())), preferred_element_type=F32)


def _fold_rows(w, rows=SUBLANES):
    xs = [w[r:r + rows, :] for r in range(0, w.shape[0], rows)]
    while len(xs) > 1:
        xs = [xs[j] + xs[j + 1] for j in range(0, len(xs) - 1, 2)] + ([xs[-1]] if len(xs) % 2 else [])
    return xs[0]


def _masked_qt(q, shift, n, qt_ref):
    qt = q.T
    dim = lax.broadcasted_iota(I32, qt.shape, 0)
    for j in range(n):
        qt_ref[j] = jnp.where((dim >> shift) == j, qt, 0.0).astype(BF16)


def _cparams(n_axes):
    return pltpu.CompilerParams(dimension_semantics=("arbitrary",) * n_axes,
                                vmem_limit_bytes=VMEM_LIMIT)


def _softmax_step(s_t, vt_h, m_ref, acc_ref):
    m_old = m_ref[...]
    m_new = jnp.maximum(m_old, jnp.max(s_t, axis=0, keepdims=True))
    alpha = jnp.exp2(m_old - m_new)
    p = jnp.exp2(s_t - m_new)
    acc_ref[...] = alpha * acc_ref[...] + jnp.dot(vt_h, p.astype(BF16), preferred_element_type=F32)
    m_ref[...] = m_new


def _softmax_init(m_ref, acc_ref):
    m_ref[...] = jnp.full(m_ref.shape, NEG, F32)
    acc_ref[...] = jnp.zeros(acc_ref.shape, F32)


def _softmax_out(acc_ref):
    return acc_ref[:HEAD_DIM, :] / acc_ref[HEAD_DIM:HEAD_DIM + 1, :]


def _store_vt(o_ref, vt):
    ones = jnp.ones((VROWS - HEAD_DIM, CK), o_ref.dtype)
    for j in range(o_ref.shape[0]):
        for h in range(N_HEADS):
            o_ref[j, h * VROWS:h * VROWS + HEAD_DIM, :] = (
                vt[h * HEAD_DIM:(h + 1) * HEAD_DIM, j * CK:(j + 1) * CK].astype(o_ref.dtype))
            o_ref[j, h * VROWS + HEAD_DIM:(h + 1) * VROWS, :] = ones


def _flash_loop(n_prev, qk_all, mask, vt_rows, state, prep=None):
    s_ref, m_ref, acc_ref = state
    n_state = m_ref.shape[0]
    for j in range(n_state):
        _softmax_init(m_ref.at[j], acc_ref.at[j])

    def park(c, slot):
        for j, s in enumerate(qk_all(c)):
            s_ref[slot, j] = s

    def consume(c, slot, diag):
        ctx = (c, diag) if prep is None else prep(c, diag)
        for j in range(n_state):
            _softmax_step(mask(ctx, j, s_ref[slot, j]), vt_rows(c, j),
                          m_ref.at[j], acc_ref.at[j])

    park(0, 0)

    def body(p, carry):
        c = 2 * p
        park(c + 1, 1)
        consume(c, 0, False)
        park(c + 2, 0)
        consume(c + 1, 1, False)
        return carry

    n_pair = lax.shift_right_logical(n_prev, 1)
    lax.fori_loop(0, n_pair, body, 0)
    c0 = 2 * n_pair
    odd = (n_prev & 1) == 1

    @pl.when(odd)
    def _():
        park(c0 + 1, 1)
        consume(c0, 0, False)
        consume(c0 + 1, 1, True)

    @pl.when(jnp.logical_not(odd))
    def _():
        consume(c0, 0, True)


def _attn_scratch(n_state):
    return [pltpu.VMEM((2, n_state, CK, TQ), F32), pltpu.VMEM((n_state, 1, TQ), F32),
            pltpu.VMEM((n_state, VROWS, TQ), F32), pltpu.VMEM((BRANCH_W, TQ), F32)]


def _kv_specs(t, w):
    kspec = pl.BlockSpec((None, t, w), lambda bb, i: (bb, 0, 0))
    vspec = pl.BlockSpec((None, t // CK, N_HEADS * VROWS, CK), lambda bb, i: (bb, 0, 0, 0))
    return kspec, vspec


def _ln_kernel(x_ref, g_ref, b_ref, h_ref, hb_ref):
    x = x_ref[...]
    mu = jnp.mean(x, axis=1, keepdims=True)
    xc = x - mu
    var = jnp.mean(xc * xc, axis=1, keepdims=True)
    y = xc * lax.rsqrt(var + LN_EPS) * g_ref[...] + b_ref[...]
    h_ref[...] = y
    hb_ref[...] = y.astype(BF16)


def _layer_norm0(x2, g, b):
    n, d = x2.shape
    tm = 512
    row = pl.BlockSpec((tm, d), lambda i: (i, 0))
    vec = pl.BlockSpec((1, d), lambda i: (0, 0))
    return pl.pallas_call(
        _ln_kernel,
        out_shape=(jax.ShapeDtypeStruct((n, d), F32), jax.ShapeDtypeStruct((n, d), BF16)),
        grid=(n // tm,),
        in_specs=[row, vec, vec],
        out_specs=(row, row),
        compiler_params=_cparams(1),
        name="ln0",
    )(x2, g.reshape(1, d), b.reshape(1, d))


def _proj_plain_kernel(x_ref, w_ref, wt_ref, *out_refs, n_t):
    for g, o_ref in enumerate(out_refs[:n_t]):
        _store_vt(o_ref, _nt_dot(wt_ref[g * BRANCH_W:(g + 1) * BRANCH_W, :], x_ref[...]))
    off = 0
    for o_ref in out_refs[n_t:]:
        wd = o_ref.shape[-1]
        for j in range(0, wd, MXU_N):
            acc = jnp.dot(x_ref[...], w_ref[:, off + j:off + j + MXU_N], preferred_element_type=F32)
            o_ref[:, j:j + MXU_N] = acc.astype(o_ref.dtype)
        off += wd


def _proj_plain(hb3, w, wt, widths):
    b, t, d = hb3.shape
    tm = 512
    n_t = wt.shape[0] // BRANCH_W
    shapes = [jax.ShapeDtypeStruct((b, t // CK, N_HEADS * VROWS, CK), BF16)] * n_t
    specs = [pl.BlockSpec((None, tm // CK, N_HEADS * VROWS, CK), lambda i, bb: (bb, i, 0, 0))] * n_t
    shapes += [jax.ShapeDtypeStruct((b, t, wd), BF16) for wd in widths]
    specs += [pl.BlockSpec((None, tm, wd), lambda i, bb: (bb, i, 0)) for wd in widths]
    return pl.pallas_call(
        functools.partial(_proj_plain_kernel, n_t=n_t),
        out_shape=tuple(shapes),
        grid=(t // tm, b),
        in_specs=[pl.BlockSpec((None, tm, d), lambda i, bb: (bb, i, 0)),
                  pl.BlockSpec(w.shape, lambda i, bb: (0, 0)),
                  pl.BlockSpec(wt.shape, lambda i, bb: (0, 0))],
        out_specs=tuple(specs),
        compiler_params=_cparams(2),
        name="proj_plain",
    )(hb3, w, wt)


def _proj_rope_kernel(x_ref, w_ref, wr_ref, c_ref, s_ref, *out_refs):
    off = 0
    for o_ref in out_refs:
        wd = o_ref.shape[-1]
        for j in range(0, wd, MXU_N):
            sl = slice(off + j, off + j + MXU_N)
            acc = jnp.dot(x_ref[...], w_ref[:, sl], preferred_element_type=F32)
            rot = jnp.dot(x_ref[...], wr_ref[:, sl], preferred_element_type=F32)
            o_ref[:, j:j + MXU_N] = (acc * c_ref[:, sl] + rot * s_ref[:, sl]).astype(o_ref.dtype)
        off += wd


def _proj_rope(hb3, w, wr, ctab, stab, widths):
    b, t, d = hb3.shape
    tm = 256
    ncol = w.shape[1]
    wspec = pl.BlockSpec((d, ncol), lambda i, bb: (0, 0))
    tspec = pl.BlockSpec((tm, ncol), lambda i, bb: (i, 0))
    return pl.pallas_call(
        _proj_rope_kernel,
        out_shape=tuple(jax.ShapeDtypeStruct((b, t, wd), BF16) for wd in widths),
        grid=(t // tm, b),
        in_specs=[pl.BlockSpec((None, tm, d), lambda i, bb: (bb, i, 0)), wspec, wspec, tspec, tspec],
        out_specs=tuple(pl.BlockSpec((None, tm, wd), lambda i, bb: (bb, i, 0)) for wd in widths),
        compiler_params=_cparams(2),
        name="proj_rope",
    )(hb3, w, wr, ctab, stab)


def _dsa_kernel(aq_ref, ak_ref, avt_ref, iq_ref, ik_ref, iw_ref, pick_ref, tri_ref, o_ref,
                keys_ref, hi_ref, lo_ref, iqt_ref, aqt_ref, wt_ref, thr_ref, s_ref, m_ref, acc_ref, ot_ref,
                *, topk, idx_scale):
    i = pl.program_id(1)
    nk = i + 1
    kpos = lax.broadcasted_iota(I32, (CK, TQ), 0)
    qpos = lax.broadcasted_iota(I32, (CK, TQ), 1)

    _masked_qt(iq_ref[...].astype(F32), 5, IDX_HEADS, iqt_ref)
    _masked_qt(aq_ref[...].astype(F32) * (HEAD_DIM ** -0.5 * LOG2E), 6, N_HEADS, aqt_ref)
    wt_ref[...] = _nt_dot(pick_ref[...], iw_ref[...]) * idx_scale

    def score_chunk(c, diag):
        kc = ik_ref[pl.ds(pl.multiple_of(c * CK, CK), CK), :]
        sc = jnp.zeros((CK, TQ), F32)
        for hh in range(IDX_HEADS):
            logit = jnp.dot(kc, iqt_ref[hh], preferred_element_type=F32)
            sc = sc + jnp.maximum(logit, 0.0) * wt_ref[hh:hh + 1, :]
        bits = pltpu.bitcast(sc, I32)
        key = jnp.where(bits < 0, INT_MIN - bits, bits)
        if diag:
            key = jnp.where(kpos <= qpos, key, INT_MIN)
        keys_ref[c] = key
        hi_ref[c] = (key >> 16).astype(I16)
        lo_ref[c] = ((key & 0xFFFF) - HALF16).astype(I16)

    def score_body(c, carry):
        score_chunk(c, False)
        return carry

    lax.fori_loop(0, i, score_body, 0)
    score_chunk(i, True)

    def count16(ref, pred):
        def body(c, part):
            return part + _fold_rows(jnp.where(pred(ref[c]), jnp.int16(1), jnp.int16(0)), 2 * SUBLANES)
        part = lax.fori_loop(0, nk, body, jnp.zeros((2 * SUBLANES, TQ), I16))
        return jnp.sum(part.astype(F32), axis=0, keepdims=True)

    def search16(ref, need):
        def bit_body(bi, t_u):
            c_u = t_u | jnp.left_shift(jnp.int32(1), 15 - bi)
            ck = (c_u - HALF16).astype(I16)
            cnt = count16(ref, lambda v: v >= ck)
            return jnp.where(cnt >= need, c_u, t_u)
        return lax.fori_loop(0, 16, bit_body, jnp.zeros((1, TQ), I32))

    hi_u = search16(hi_ref, float(topk))
    thr_hi = (hi_u - HALF16).astype(I16)
    n_above = count16(hi_ref, lambda v: v > thr_hi)

    def bucket_body(c, carry):
        hi_ref[c] = jnp.where(hi_ref[c] == thr_hi, lo_ref[c], jnp.int16(-HALF16))
        return carry

    lax.fori_loop(0, nk, bucket_body, 0)
    lo_u = search16(hi_ref, float(topk) - n_above)
    thr = ((hi_u - HALF16) << 16) | lo_u

    def tie_body(c, carry):
        k = keys_ref[c]
        return (carry[0] + _fold_rows(jnp.where(k > thr, 1.0, 0.0)),
                carry[1] + _fold_rows(jnp.where(k == thr, 1.0, 0.0)))

    zero8 = jnp.zeros((SUBLANES, TQ), F32)
    gt8, eq8 = lax.fori_loop(0, nk, tie_body, (zero8, zero8))
    need = float(topk) - jnp.sum(gt8, axis=0, keepdims=True)
    amb = jnp.logical_and(jnp.sum(eq8, axis=0, keepdims=True) > need, thr > INT_MIN)
    any_amb = jnp.max(jnp.where(amb, 1.0, 0.0)) > 0.5

    @pl.when(any_amb)
    def _():
        def drop_body(c, seen):
            k = keys_ref[c]
            eq = k == thr
            eqf = jnp.where(eq, 1.0, 0.0)
            rank = jnp.dot(tri_ref[...], eqf.astype(BF16), preferred_element_type=F32) + seen
            drop = jnp.logical_and(jnp.logical_and(eq, rank > need), amb)
            keys_ref[c] = jnp.where(drop, INT_MIN, k)
            return seen + jnp.sum(eqf, axis=0, keepdims=True)

        lax.fori_loop(0, nk, drop_body, jnp.zeros((1, TQ), F32))

    thr_ref[...] = jnp.maximum(thr, INT_MIN + 1)

    def qk_all(c):
        kc = ak_ref[pl.ds(pl.multiple_of(c * CK, CK), CK), :]
        return [jnp.dot(kc, aqt_ref[h], preferred_element_type=F32) for h in range(N_HEADS)]

    _flash_loop(i, qk_all,
                lambda keep, h, s: jnp.where(keep, s, NEG),
                lambda c, h: avt_ref[c, h * VROWS:(h + 1) * VROWS, :],
                (s_ref, m_ref, acc_ref),
                prep=lambda c, diag: keys_ref[c] >= thr_ref[...])
    for h in range(N_HEADS):
        ot_ref[h * HEAD_DIM:(h + 1) * HEAD_DIM, :] = _softmax_out(acc_ref.at[h])
    o_ref[...] = ot_ref[...].T.astype(o_ref.dtype)


def _dsa(aq, ak, avt, iq, ik, iw):
    b, t, _ = aq.shape
    topk = min(TOPK_MAX, t // 4)
    qspec = pl.BlockSpec((None, TQ, BRANCH_W), lambda bb, i: (bb, i, 0))
    kspec, vspec = _kv_specs(t, BRANCH_W)
    pick = np.zeros((2 * SUBLANES, MXU_N), np.float32)
    for hh in range(IDX_HEADS):
        pick[hh, KV_LORA + hh] = 1.0
    pick = jnp.asarray(pick, BF16)
    tri = jnp.asarray(np.tril(np.ones((CK, CK), np.float32)), BF16)
    kern = functools.partial(_dsa_kernel, topk=topk, idx_scale=(IDX_HEADS * IDX_DIM) ** -0.5)
    return pl.pallas_call(
        kern,
        out_shape=jax.ShapeDtypeStruct((b, t, BRANCH_W), BF16),
        grid=(b, t // TQ),
        in_specs=[qspec, kspec, vspec, qspec, kspec, qspec,
                  pl.BlockSpec(pick.shape, lambda bb, i: (0, 0)), pl.BlockSpec(tri.shape, lambda bb, i: (0, 0))],
        out_specs=qspec,
        scratch_shapes=[
            pltpu.VMEM((t // CK, CK, TQ), I32),
            pltpu.VMEM((t // CK, CK, TQ), I16),
            pltpu.VMEM((t // CK, CK, TQ), I16),
            pltpu.VMEM((IDX_HEADS, BRANCH_W, TQ), BF16),
            pltpu.VMEM((N_HEADS, BRANCH_W, TQ), BF16),
            pltpu.VMEM((2 * SUBLANES, TQ), F32),
            pltpu.VMEM((1, TQ), I32),
        ] + _attn_scratch(N_HEADS),
        compiler_params=_cparams(2),
        name="dsa",
    )(aq, ak, avt, iq, ik, iw, pick, tri)


def _kbar_kernel(k_ref, o_ref):
    o_ref[...] = jnp.zeros(o_ref.shape, o_ref.dtype)
    nb = k_ref.shape[0] // MOBA_BLOCK
    for n in range(nb):
        blk = k_ref[n * MOBA_BLOCK:(n + 1) * MOBA_BLOCK, :].astype(F32)
        o_ref[n:n + 1, :] = jnp.mean(blk, axis=0, keepdims=True).astype(o_ref.dtype)


def _kbar(bk):
    b, t, w = bk.shape
    nbp = max(2 * SUBLANES, t // MOBA_BLOCK)
    return pl.pallas_call(
        _kbar_kernel,
        out_shape=jax.ShapeDtypeStruct((b, nbp, w), BF16),
        grid=(b,),
        in_specs=[pl.BlockSpec((None, t, w), lambda bb: (bb, 0, 0))],
        out_specs=pl.BlockSpec((None, nbp, w), lambda bb: (bb, 0, 0)),
        compiler_params=_cparams(1),
        name="moba_kbar",
    )(bk)


def _moba_kernel(q_ref, k_ref, vt_ref, kbar_ref, o_ref, qt_ref, bias_ref, s_ref, m_ref, acc_ref, ot_ref):
    i = pl.program_id(1)
    nbp = kbar_ref.shape[0]
    blk = lax.broadcasted_iota(I32, (nbp, TQ), 0)
    blk_f = blk.astype(F32)
    kpos = lax.broadcasted_iota(I32, (CK, TQ), 0)
    qpos = lax.broadcasted_iota(I32, (CK, TQ), 1)
    _masked_qt(q_ref[...].astype(F32) * (HEAD_DIM ** -0.5 * LOG2E), 6, N_HEADS, qt_ref)

    for h in range(N_HEADS):
        g = jnp.where(blk < i, jnp.dot(kbar_ref[...], qt_ref[h], preferred_element_type=F32), NEG)
        bias = jnp.full((nbp, TQ), NEG, F32)
        for _ in range(MOBA_TOPK):
            mx = jnp.max(g, axis=0, keepdims=True)
            first = jnp.min(jnp.where(g == mx, blk_f, 1e9), axis=0, keepdims=True)
            pick = jnp.logical_and(blk_f == first, mx > 0.5 * NEG)
            bias = jnp.where(pick, 0.0, bias)
            g = jnp.where(pick, NEG, g)
        bias_ref[h] = bias

    def qk_all(c):
        kc = k_ref[pl.ds(pl.multiple_of(c * CK, CK), CK), :]
        return [jnp.dot(kc, qt_ref[h], preferred_element_type=F32) for h in range(N_HEADS)]

    def mask(ctx, h, s):
        c, diag = ctx
        return jnp.where(kpos <= qpos, s, NEG) if diag else s + bias_ref[h, pl.ds(c, 1), :]

    _flash_loop(i, qk_all, mask, lambda c, h: vt_ref[c, h * VROWS:(h + 1) * VROWS, :],
                (s_ref, m_ref, acc_ref))
    for h in range(N_HEADS):
        ot_ref[h * HEAD_DIM:(h + 1) * HEAD_DIM, :] = _softmax_out(acc_ref.at[h])
    o_ref[...] = ot_ref[...].T.astype(o_ref.dtype)


def _moba(bq, bk, bvt, kbar):
    b, t, w = bq.shape
    assert TQ == MOBA_BLOCK and CK == MOBA_BLOCK and t % MOBA_BLOCK == 0
    nbp = kbar.shape[1]
    qspec = pl.BlockSpec((None, TQ, w), lambda bb, i: (bb, i, 0))
    kspec, vspec = _kv_specs(t, w)
    return pl.pallas_call(
        _moba_kernel,
        out_shape=jax.ShapeDtypeStruct((b, t, w), BF16),
        grid=(b, t // TQ),
        in_specs=[qspec, kspec, vspec, pl.BlockSpec((None, nbp, w), lambda bb, i: (bb, 0, 0))],
        out_specs=qspec,
        scratch_shapes=[pltpu.VMEM((N_HEADS, w, TQ), BF16), pltpu.VMEM((N_HEADS, nbp, TQ), F32)]
        + _attn_scratch(N_HEADS),
        compiler_params=_cparams(2),
        name="moba",
    )(bq, bk, bvt, kbar)


def _diff_kernel(q_ref, k_ref, vt_ref, lam_ref, norm_ref, misc_ref, o_ref,
                 qt_ref, s_ref, m_ref, acc_ref, ot_ref):
    i = pl.program_id(1)
    kpos = lax.broadcasted_iota(I32, (CK, TQ), 0)
    qpos = lax.broadcasted_iota(I32, (CK, TQ), 1)
    _masked_qt(q_ref[...].astype(F32) * (DIFF_DIM ** -0.5 * LOG2E), 5, 2 * N_HEADS, qt_ref)

    dl = lam_ref[...]
    lam_init = misc_ref[0:1, 0:1]
    lam = (jnp.exp(jnp.sum(dl[0:1, :] * dl[1:2, :], axis=1, keepdims=True))
           - jnp.exp(jnp.sum(dl[2:3, :] * dl[3:4, :], axis=1, keepdims=True)) + lam_init)

    def qk_all(c):
        kc = k_ref[pl.ds(pl.multiple_of(c * CK, CK), CK), :]
        return [jnp.dot(kc, qt_ref[j], preferred_element_type=F32) for j in range(2 * N_HEADS)]

    _flash_loop(i, qk_all,
                lambda ctx, j, s: jnp.where(kpos <= qpos, s, NEG) if ctx[1] else s,
                lambda c, j: vt_ref[c, (j // 2) * VROWS:(j // 2 + 1) * VROWS, :],
                (s_ref, m_ref, acc_ref))

    post = norm_ref[...] * (1.0 - lam_init)
    for h in range(N_HEADS):
        o_h = _softmax_out(acc_ref.at[2 * h]) - lam * _softmax_out(acc_ref.at[2 * h + 1])
        ms = jnp.mean(o_h * o_h, axis=0, keepdims=True)
        ot_ref[h * HEAD_DIM:(h + 1) * HEAD_DIM, :] = o_h * lax.rsqrt(ms + RMS_EPS) * post
    o_ref[...] = ot_ref[...].T.astype(o_ref.dtype)


def _diff(cq, ck, cvt, lam, norm, misc):
    b, t, w = cq.shape
    qspec = pl.BlockSpec((None, TQ, w), lambda bb, i: (bb, i, 0))
    kspec, vspec = _kv_specs(t, w)
    full = lambda a: pl.BlockSpec(a.shape, lambda bb, i: (0,) * a.ndim)
    return pl.pallas_call(
        _diff_kernel,
        out_shape=jax.ShapeDtypeStruct((b, t, w), BF16),
        grid=(b, t // TQ),
        in_specs=[qspec, kspec, vspec, full(lam), full(norm), full(misc)],
        out_specs=qspec,
        scratch_shapes=[pltpu.VMEM((2 * N_HEADS, w, TQ), BF16)] + _attn_scratch(2 * N_HEADS),
        compiler_params=_cparams(2),
        name="diff",
    )(cq, ck, cvt, lam, norm, misc)


def _mla_prep_kernel(cq_ref, ckv_ref, kr_ref, qn_ref, kvn_ref, wq_ref, wqr_ref, wk_ref, wvt_ref,
                     p_ref, ct_ref, st_ref, q_out, k_out, vt_out):
    x = cq_ref[...].astype(F32)
    xn = (x * lax.rsqrt(jnp.mean(x * x, axis=1, keepdims=True) + RMS_EPS) * qn_ref[...]).astype(BF16)
    q = (jnp.dot(xn, wq_ref[...], preferred_element_type=F32) * ct_ref[...]
         + jnp.dot(xn, wqr_ref[...], preferred_element_type=F32) * st_ref[...])
    q_out[...] = q.astype(q_out.dtype)
    c = ckv_ref[:, :KV_LORA].astype(F32)
    cn = (c * lax.rsqrt(jnp.mean(c * c, axis=1, keepdims=True) + RMS_EPS) * kvn_ref[...]).astype(BF16)
    k = (jnp.dot(cn, wk_ref[...], preferred_element_type=F32)
         + jnp.dot(kr_ref[...], p_ref[...], preferred_element_type=F32))
    k_out[...] = k.astype(k_out.dtype)
    _store_vt(vt_out, _nt_dot(wvt_ref[...], cn))


def _mla_prep(dcq, ckv, kr, qn, kvn, wq, wqr, wk, wvt, pmat, ct, st):
    b, t, _ = dcq.shape
    tm = 512
    hw = N_HEADS * LANES
    row = lambda w: pl.BlockSpec((None, tm, w), lambda i, bb: (bb, i, 0))
    full = lambda a: pl.BlockSpec(a.shape, lambda i, bb: (0,) * a.ndim)
    tab = pl.BlockSpec((tm, hw), lambda i, bb: (i, 0))
    return pl.pallas_call(
        _mla_prep_kernel,
        out_shape=(jax.ShapeDtypeStruct((b, t, hw), BF16), jax.ShapeDtypeStruct((b, t, hw), BF16),
                   jax.ShapeDtypeStruct((b, t // CK, N_HEADS * VROWS, CK), BF16)),
        grid=(t // tm, b),
        in_specs=[row(Q_LORA), row(MXU_N), row(MXU_N), full(qn), full(kvn), full(wq), full(wqr),
                  full(wk), full(wvt), full(pmat), tab, tab],
        out_specs=(row(hw), row(hw),
                   pl.BlockSpec((None, tm // CK, N_HEADS * VROWS, CK), lambda i, bb: (bb, i, 0, 0))),
        compiler_params=_cparams(2),
        name="mla_prep",
    )(dcq, ckv, kr, qn, kvn, wq, wqr, wk, wvt, pmat, ct, st)


def _mla_kernel(q_ref, k_ref, vt_ref, o_ref, qt_ref, s_ref, m_ref, acc_ref, ot_ref):
    i = pl.program_id(1)
    kpos = lax.broadcasted_iota(I32, (CK, TQ), 0)
    qpos = lax.broadcasted_iota(I32, (CK, TQ), 1)
    hs = [slice(h * LANES, (h + 1) * LANES) for h in range(N_HEADS)]
    for h in range(N_HEADS):
        qt_ref[h] = q_ref[:, hs[h]].astype(F32).T.astype(BF16)

    def qk_all(c):
        start = pl.multiple_of(c * CK, CK)
        return [jnp.dot(k_ref[pl.ds(start, CK), hs[h]], qt_ref[h], preferred_element_type=F32)
                for h in range(N_HEADS)]

    _flash_loop(i, qk_all,
                lambda ctx, h, s: jnp.where(kpos <= qpos, s, NEG) if ctx[1] else s,
                lambda c, h: vt_ref[c, h * VROWS:(h + 1) * VROWS, :],
                (s_ref, m_ref, acc_ref))
    for h in range(N_HEADS):
        ot_ref[h * HEAD_DIM:(h + 1) * HEAD_DIM, :] = _softmax_out(acc_ref.at[h])
    o_ref[...] = ot_ref[...].T.astype(o_ref.dtype)


def _mla(qm, km, vmt):
    b, t, hw = qm.shape
    kspec, vspec = _kv_specs(t, hw)
    return pl.pallas_call(
        _mla_kernel,
        out_shape=jax.ShapeDtypeStruct((b, t, BRANCH_W), BF16),
        grid=(b, t // TQ),
        in_specs=[pl.BlockSpec((None, TQ, hw), lambda bb, i: (bb, i, 0)), kspec, vspec],
        out_specs=pl.BlockSpec((None, TQ, BRANCH_W), lambda bb, i: (bb, i, 0)),
        scratch_shapes=[pltpu.VMEM((N_HEADS, LANES, TQ), BF16)] + _attn_scratch(N_HEADS),
        compiler_params=_cparams(2),
        name="mla",
    )(qm, km, vmt)


def _matmul_kernel(x_ref, w_ref, o_ref):
    o_ref[...] = jnp.dot(x_ref[...].astype(BF16), w_ref[...], preferred_element_type=F32).astype(o_ref.dtype)


def _mem_kv(mem, w):
    b, m, d = mem.shape
    n = w.shape[1]
    return pl.pallas_call(
        _matmul_kernel,
        out_shape=jax.ShapeDtypeStruct((b, m, n), BF16),
        grid=(b,),
        in_specs=[pl.BlockSpec((None, m, d), lambda bb: (bb, 0, 0)), pl.BlockSpec((d, n), lambda bb: (0, 0))],
        out_specs=pl.BlockSpec((None, m, n), lambda bb: (bb, 0, 0)),
        compiler_params=_cparams(1),
        name="mem_kv",
    )(mem, w)


def _mem_kernel(q_ref, kv_ref, o_ref):
    tq = q_ref.shape[0]
    lane_q = lax.broadcasted_iota(I32, (tq, BRANCH_W), 1)
    q = q_ref[...].astype(F32) * (HEAD_DIM ** -0.5)
    mk = kv_ref[:, :BRANCH_W]
    mv = kv_ref[:, BRANCH_W:]
    out = jnp.zeros((tq, BRANCH_W), F32)
    for h in range(N_HEADS):
        in_h = (lane_q >> 6) == h
        s = _nt_dot(jnp.where(in_h, q, 0.0).astype(BF16), mk)
        p = jnp.exp(s - jnp.max(s, axis=1, keepdims=True))
        o_h = jnp.dot(p.astype(BF16), mv, preferred_element_type=F32) / jnp.sum(p, axis=1, keepdims=True)
        out = jnp.where(in_h, o_h, out)
    o_ref[...] = out.astype(o_ref.dtype)


def _mem_attn(eq, mkv):
    b, t, w = eq.shape
    m = mkv.shape[1]
    tq = 512
    return pl.pallas_call(
        _mem_kernel,
        out_shape=jax.ShapeDtypeStruct((b, t, w), BF16),
        grid=(b, t // tq),
        in_specs=[pl.BlockSpec((None, tq, w), lambda bb, i: (bb, i, 0)),
                  pl.BlockSpec((None, m, 2 * w), lambda bb, i: (bb, 0, 0))],
        out_specs=pl.BlockSpec((None, tq, w), lambda bb, i: (bb, i, 0)),
        compiler_params=_cparams(2),
        name="mem_attn",
    )(eq, mkv)


def _final_kernel(h_ref, hb_ref, oa_ref, ob_ref, oc_ref, od_ref, oe_ref, z_ref,
                  wg_ref, wb_ref, wo_ref, g_ref, b_ref, h_out, hb_out, *, alpha):
    d = h_ref.shape[1]
    acc = jnp.zeros(h_ref.shape, F32)
    for n, o_ref in enumerate((oa_ref, ob_ref, oc_ref, od_ref, oe_ref)):
        z = z_ref[:, n * BRANCH_W:(n + 1) * BRANCH_W].astype(F32)
        y = o_ref[...].astype(F32) * (z / (1.0 + jnp.exp(-z)))
        u = jnp.dot(y.astype(BF16), wb_ref[n], preferred_element_type=F32)
        g = jnp.dot(hb_ref[...], wg_ref[:, n * d:(n + 1) * d], preferred_element_type=F32)
        acc = acc + u / (1.0 + jnp.exp(-g))
    out = jnp.dot(acc.astype(BF16), wo_ref[...], preferred_element_type=F32)
    x = alpha * h_ref[...] + out
    mu = jnp.mean(x, axis=1, keepdims=True)
    xc = x - mu
    var = jnp.mean(xc * xc, axis=1, keepdims=True)
    y = xc * lax.rsqrt(var + LN_EPS) * g_ref[...] + b_ref[...]
    h_out[...] = y
    hb_out[...] = y.astype(BF16)


def _final(h, hb, os5, z, wg, wb, wo, ln_g, ln_b, alpha):
    n, d = h.shape
    tm = 256
    row = lambda w: pl.BlockSpec((tm, w), lambda i: (i, 0))
    full = lambda a: pl.BlockSpec(a.shape, lambda i: (0,) * a.ndim)
    return pl.pallas_call(
        functools.partial(_final_kernel, alpha=alpha),
        out_shape=(jax.ShapeDtypeStruct((n, d), F32), jax.ShapeDtypeStruct((n, d), BF16)),
        grid=(n // tm,),
        in_specs=[row(d), row(d)] + [row(BRANCH_W)] * N_BRANCH + [row(N_BRANCH * BRANCH_W),
                  full(wg), full(wb), full(wo), full(ln_g), full(ln_b)],
        out_specs=(row(d), row(d)),
        compiler_params=_cparams(1),
        name="merge_out_ln",
    )(h, hb, *os5, z, wg, wb, wo, ln_g, ln_b)


def _rope_tables(seq, rot_dim):
    pos = jnp.arange(seq, dtype=F32)
    inv = ROPE_THETA ** (-jnp.arange(0, rot_dim, 2, dtype=F32) / rot_dim)
    ang = pos[:, None] * inv[None, :]
    return jnp.cos(ang), jnp.sin(ang)


def _rot_cols(wg, nh, hd, r):
    lead = wg.shape[:-1]
    w4 = wg.reshape(lead + (nh, hd))
    half = r // 2
    parts = [-w4[..., half:r], w4[..., :half]]
    if hd > r:
        parts.append(jnp.zeros(lead + (nh, hd - r), wg.dtype))
    return jnp.concatenate(parts, axis=-1).reshape(lead + (nh * hd,))


def _rope_cs(t, nh, hd, r):
    cos, sin = _rope_tables(t, r)
    c = jnp.concatenate([cos, cos, jnp.ones((t, hd - r), F32)], axis=1)
    s = jnp.concatenate([sin, sin, jnp.zeros((t, hd - r), F32)], axis=1)
    return jnp.tile(c, (1, nh)), jnp.tile(s, (1, nh))


def kernel(x, mem, ln0_g, ln0_b, w_in, mla_q_norm, w_uq, mla_kv_norm, w_ukv, diff_lam, diff_norm,
           w_mem_kv, w_branch, w_out, ln_g, ln_b):
    b, t, d = x.shape
    depth = w_in.shape[0]
    alpha = (2 * depth) ** 0.25
    assert t % 512 == 0 and d == 1024

    def seg(name):
        o, s = OFF[name]
        return w_in[:, :, o:o + s]

    zeros = lambda n: jnp.zeros((depth, d, n), w_in.dtype)

    w_plain = jnp.concatenate(
        [seg("d_cq"), seg("d_ckv"), seg("i_w"), zeros(MXU_N - KV_LORA - IDX_HEADS), seg("e_q"), seg("z")],
        axis=-1).astype(BF16)
    plain_widths = (BRANCH_W,) * 3 + (N_BRANCH * BRANCH_W,)
    w_vt = jnp.swapaxes(jnp.concatenate([seg("a_v"), seg("b_v"), seg("c_v")], axis=-1), 1, 2).astype(BF16)

    rope_groups = [
        (seg("a_q"), N_HEADS, HEAD_DIM, ROT_64), (seg("a_k"), N_HEADS, HEAD_DIM, ROT_64),
        (seg("i_q"), IDX_HEADS, IDX_DIM, ROT_32), (jnp.tile(seg("i_k"), (1, 1, IDX_HEADS)), IDX_HEADS, IDX_DIM, ROT_32),
        (seg("b_q"), N_HEADS, HEAD_DIM, ROT_64), (seg("b_k"), N_HEADS, HEAD_DIM, ROT_64),
        (seg("c_q"), 2 * N_HEADS, DIFF_DIM, ROT_32), (seg("c_k"), 2 * N_HEADS, DIFF_DIM, ROT_32),
        (jnp.concatenate([seg("d_kr"), zeros(MXU_N - MLA_ROPE)], axis=-1), 1, MXU_N, MLA_ROPE),
    ]
    w_rope = jnp.concatenate([g for g, *_ in rope_groups], axis=-1).astype(BF16)
    w_rope_rot = jnp.concatenate([_rot_cols(g, nh, hd, r) for g, nh, hd, r in rope_groups], axis=-1).astype(BF16)
    cs = [_rope_cs(t, nh, hd, r) for _, nh, hd, r in rope_groups]
    ctab = jnp.concatenate([c for c, _ in cs], axis=1)
    stab = jnp.concatenate([s for _, s in cs], axis=1)
    rope_widths = (MXU_N,) * len(rope_groups)

    uq = w_uq.reshape(depth, Q_LORA, N_HEADS, MLA_NOPE + MLA_ROPE)
    qn_w, qr_w = uq[..., :MLA_NOPE], uq[..., MLA_NOPE:]
    pad32 = jnp.zeros((depth, Q_LORA, N_HEADS, LANES - MLA_NOPE - MLA_ROPE), w_uq.dtype)
    hw = N_HEADS * LANES
    wq = jnp.concatenate([qn_w, qr_w, pad32], axis=-1).reshape(depth, Q_LORA, hw).astype(BF16)
    half = MLA_ROPE // 2
    wq_rot = jnp.concatenate([jnp.zeros_like(qn_w), -qr_w[..., half:], qr_w[..., :half], pad32],
                             axis=-1).reshape(depth, Q_LORA, hw).astype(BF16)
    cos_m, sin_m = _rope_tables(t, MLA_ROPE)
    one = lambda n: jnp.ones((t, n), F32)
    zer = lambda n: jnp.zeros((t, n), F32)
    qs = (MLA_NOPE + MLA_ROPE) ** -0.5 * LOG2E
    ct_q = qs * jnp.tile(jnp.concatenate([one(MLA_NOPE), cos_m, cos_m, one(LANES - MLA_NOPE - MLA_ROPE)], axis=1), (1, N_HEADS))
    st_q = qs * jnp.tile(jnp.concatenate([zer(MLA_NOPE), sin_m, sin_m, zer(LANES - MLA_NOPE - MLA_ROPE)], axis=1), (1, N_HEADS))
    ukv = w_ukv.reshape(depth, KV_LORA, N_HEADS, MLA_NOPE + MLA_V)
    wk = jnp.concatenate([ukv[..., :MLA_NOPE], jnp.zeros((depth, KV_LORA, N_HEADS, LANES - MLA_NOPE), w_ukv.dtype)],
                         axis=-1).reshape(depth, KV_LORA, hw).astype(BF16)
    wvt = jnp.swapaxes(ukv[..., MLA_NOPE:].reshape(depth, KV_LORA, N_HEADS * MLA_V), 1, 2).astype(BF16)
    place = np.zeros((MXU_N, hw), np.float32)
    for hh in range(N_HEADS):
        for j in range(MLA_ROPE):
            place[j, hh * LANES + MLA_NOPE + j] = 1.0
    place = jnp.asarray(place, BF16)

    wg = seg("g").astype(BF16)
    wb = w_branch.astype(BF16)
    wo = w_out.astype(BF16)
    wmem = w_mem_kv.astype(BF16)
    norm_t = jnp.broadcast_to(diff_norm.astype(F32)[:, :, None], (depth, HEAD_DIM, TQ))

    h, hb = _layer_norm0(x.reshape(b * t, d), ln0_g, ln0_b)
    for l in range(depth):
        hb3 = hb.reshape(b, t, d)
        avt, bvt, cvt, dcq, ckv_iw, eq, z = _proj_plain(hb3, w_plain[l], w_vt[l], plain_widths)
        aq, ak, iq, ik, bq, bk, cq, ck, kr = _proj_rope(hb3, w_rope[l], w_rope_rot[l], ctab, stab, rope_widths)

        o_a = _dsa(aq, ak, avt, iq, ik, ckv_iw)
        o_b = _moba(bq, bk, bvt, _kbar(bk))
        lam_init = 0.8 - 0.6 * math.exp(-0.3 * l)
        misc = jnp.full((SUBLANES, LANES), lam_init, F32)
        o_c = _diff(cq, ck, cvt, diff_lam[l].astype(F32), norm_t[l], misc)
        qm, km, vmt = _mla_prep(dcq, ckv_iw, kr, mla_q_norm[l].reshape(1, Q_LORA), mla_kv_norm[l].reshape(1, KV_LORA),
                                wq[l], wq_rot[l], wk[l], wvt[l], place, ct_q, st_q)
        o_d = _mla(qm, km, vmt)
        o_e = _mem_attn(eq, _mem_kv(mem, wmem[l]))

        os5 = [o.reshape(b * t, BRANCH_W) for o in (o_a, o_b, o_c, o_d, o_e)]
        h, hb = _final(h, hb, os5, z.reshape(b * t, N_BRANCH * BRANCH_W), wg[l], wb[l], wo[l],
                       ln_g[l].reshape(1, d), ln_b[l].reshape(1, d), alpha)
    return h.reshape(b, t, d)
```

```python
import functools
import math

import numpy as np
import jax
import jax.numpy as jnp
from jax import lax
from jax.experimental import pallas as pl
from jax.experimental.pallas import tpu as pltpu

F32 = jnp.float32
BF16 = jnp.bfloat16
I32 = jnp.int32
I16 = jnp.int16

N_HEADS = 4
HEAD_DIM = 64
BRANCH_W = N_HEADS * HEAD_DIM
N_BRANCH = 5
ROPE_THETA = 500000.0
ROT_64 = 16
ROT_32 = 8
IDX_HEADS = 8
IDX_DIM = 32
TOPK_MAX = 256
MOBA_BLOCK = 256
MOBA_TOPK = 3
DIFF_DIM = 32
Q_LORA = 256
KV_LORA = 128
MLA_NOPE = 64
MLA_ROPE = 32
MLA_V = 64
LN_EPS = 1e-5
RMS_EPS = 1e-6

IN_LAYOUT = (
    ("a_q", BRANCH_W), ("a_k", BRANCH_W), ("a_v", BRANCH_W),
    ("i_q", IDX_HEADS * IDX_DIM), ("i_k", IDX_DIM), ("i_w", IDX_HEADS),
    ("b_q", BRANCH_W), ("b_k", BRANCH_W), ("b_v", BRANCH_W),
    ("c_q", BRANCH_W), ("c_k", BRANCH_W), ("c_v", BRANCH_W),
    ("d_cq", Q_LORA), ("d_ckv", KV_LORA), ("d_kr", MLA_ROPE),
    ("e_q", BRANCH_W),
    ("z", N_BRANCH * BRANCH_W),
    ("g", N_BRANCH * 1024),
)

SUBLANES = 8
LANES = 128
MXU_N = 256
TQ = 256
CK = 256
VROWS = HEAD_DIM + 16
NEG = -1e30
LOG2E = math.log2(math.e)
INT_MIN = np.int32(-2 ** 31)
HALF16 = 1 << 15
VMEM_LIMIT = 56 * 1024 * 1024


def _offsets():
    off, out = 0, {}
    for name, size in IN_LAYOUT:
        out[name] = (off, size)
        off += size
    return out


OFF = _offsets()


def _nt_dot(a, b):
    return lax.dot_general(a, b, (((1,), (1,)), ((), ())), preferred_element_type=F32)


def _tn_dot(w, x):
    return lax.dot_general(w, x, (((0,), (1,)), ((), ())), preferred_element_type=F32)


def _fold_rows(w, rows=SUBLANES):
    xs = [w[r:r + rows, :] for r in range(0, w.shape[0], rows)]
    while len(xs) > 1:
        xs = [xs[j] + xs[j + 1] for j in range(0, len(xs) - 1, 2)] + ([xs[-1]] if len(xs) % 2 else [])
    return xs[0]


def _masked_qt(q, shift, n, qt_ref):
    qt = q.T
    dim = lax.broadcasted_iota(I32, qt.shape, 0)
    for j in range(n):
        qt_ref[j] = jnp.where((dim >> shift) == j, qt, 0.0).astype(BF16)


def _cparams(n_axes):
    return pltpu.CompilerParams(dimension_semantics=("arbitrary",) * n_axes,
                                vmem_limit_bytes=VMEM_LIMIT)


def _softmax_step(s_t, vt_h, m_ref, acc_ref):
    m_old = m_ref[...]
    m_new = jnp.maximum(m_old, jnp.max(s_t, axis=0, keepdims=True))
    alpha = jnp.exp2(m_old - m_new)
    p = jnp.exp2(s_t - m_new)
    acc_ref[...] = alpha * acc_ref[...] + jnp.dot(vt_h, p.astype(BF16), preferred_element_type=F32)
    m_ref[...] = m_new


def _softmax_init(m_ref, acc_ref):
    m_ref[...] = jnp.full(m_ref.shape, NEG, F32)
    acc_ref[...] = jnp.zeros(acc_ref.shape, F32)


def _softmax_out(acc_ref):
    return acc_ref[:HEAD_DIM, :] / acc_ref[HEAD_DIM:HEAD_DIM + 1, :]


def _store_vt(o_ref, vt):
    ones = jnp.ones((VROWS - HEAD_DIM, CK), o_ref.dtype)
    for j in range(o_ref.shape[0]):
        for h in range(N_HEADS):
            o_ref[j, h * VROWS:h * VROWS + HEAD_DIM, :] = (
                vt[h * HEAD_DIM:(h + 1) * HEAD_DIM, j * CK:(j + 1) * CK].astype(o_ref.dtype))
            o_ref[j, h * VROWS + HEAD_DIM:(h + 1) * VROWS, :] = ones


def _flash_loop(n_prev, qk_all, mask, vt_rows, state, prep=None):
    s_ref, m_ref, acc_ref = state
    n_state = m_ref.shape[0]
    for j in range(n_state):
        _softmax_init(m_ref.at[j], acc_ref.at[j])

    def park(c, slot):
        for j, s in enumerate(qk_all(c)):
            s_ref[slot, j] = s

    def consume(c, slot, diag):
        ctx = (c, diag) if prep is None else prep(c, diag)
        for j in range(n_state):
            _softmax_step(mask(ctx, j, s_ref[slot, j]), vt_rows(c, j),
                          m_ref.at[j], acc_ref.at[j])

    park(0, 0)

    def body(p, carry):
        c = 2 * p
        park(c + 1, 1)
        consume(c, 0, False)
        park(c + 2, 0)
        consume(c + 1, 1, False)
        return carry

    n_pair = lax.shift_right_logical(n_prev, 1)
    lax.fori_loop(0, n_pair, body, 0)
    c0 = 2 * n_pair
    odd = (n_prev & 1) == 1

    @pl.when(odd)
    def _():
        park(c0 + 1, 1)
        consume(c0, 0, False)
        consume(c0 + 1, 1, True)

    @pl.when(jnp.logical_not(odd))
    def _():
        consume(c0, 0, True)


def _attn_scratch(n_state):
    return [pltpu.VMEM((2, n_state, CK, TQ), F32), pltpu.VMEM((n_state, 1, TQ), F32),
            pltpu.VMEM((n_state, VROWS, TQ), F32), pltpu.VMEM((BRANCH_W, TQ), F32)]


def _kv_specs(t, w):
    kspec = pl.BlockSpec((None, t, w), lambda bb, i: (bb, 0, 0))
    vspec = pl.BlockSpec((None, t // CK, N_HEADS * VROWS, CK), lambda bb, i: (bb, 0, 0, 0))
    return kspec, vspec


def _ln_kernel(x_ref, g_ref, b_ref, h_ref, hb_ref):
    x = x_ref[...]
    mu = jnp.mean(x, axis=1, keepdims=True)
    xc = x - mu
    var = jnp.mean(xc * xc, axis=1, keepdims=True)
    y = xc * lax.rsqrt(var + LN_EPS) * g_ref[...] + b_ref[...]
    h_ref[...] = y
    hb_ref[...] = y.astype(BF16)


def _layer_norm0(x2, g, b):
    n, d = x2.shape
    tm = 512
    row = pl.BlockSpec((tm, d), lambda i: (i, 0))
    vec = pl.BlockSpec((1, d), lambda i: (0, 0))
    return pl.pallas_call(
        _ln_kernel,
        out_shape=(jax.ShapeDtypeStruct((n, d), F32), jax.ShapeDtypeStruct((n, d), BF16)),
        grid=(n // tm,),
        in_specs=[row, vec, vec],
        out_specs=(row, row),
        compiler_params=_cparams(1),
        name="ln0",
    )(x2, g.reshape(1, d), b.reshape(1, d))


def _proj_plain_kernel(x_ref, w_ref, wt_ref, *out_refs, n_t):
    for g, o_ref in enumerate(out_refs[:n_t]):
        _store_vt(o_ref, _tn_dot(wt_ref[:, g * BRANCH_W:(g + 1) * BRANCH_W], x_ref[...]))
    off = 0
    for o_ref in out_refs[n_t:]:
        wd = o_ref.shape[-1]
        for j in range(0, wd, MXU_N):
            acc = jnp.dot(x_ref[...], w_ref[:, off + j:off + j + MXU_N], preferred_element_type=F32)
            o_ref[:, j:j + MXU_N] = acc.astype(o_ref.dtype)
        off += wd


def _proj_plain(hb3, w, wt, widths):
    b, t, d = hb3.shape
    tm = 512
    n_t = wt.shape[1] // BRANCH_W
    shapes = [jax.ShapeDtypeStruct((b, t // CK, N_HEADS * VROWS, CK), BF16)] * n_t
    specs = [pl.BlockSpec((None, tm // CK, N_HEADS * VROWS, CK), lambda i, bb: (bb, i, 0, 0))] * n_t
    shapes += [jax.ShapeDtypeStruct((b, t, wd), BF16) for wd in widths]
    specs += [pl.BlockSpec((None, tm, wd), lambda i, bb: (bb, i, 0)) for wd in widths]
    return pl.pallas_call(
        functools.partial(_proj_plain_kernel, n_t=n_t),
        out_shape=tuple(shapes),
        grid=(t // tm, b),
        in_specs=[pl.BlockSpec((None, tm, d), lambda i, bb: (bb, i, 0)),
                  pl.BlockSpec(w.shape, lambda i, bb: (0, 0)),
                  pl.BlockSpec(wt.shape, lambda i, bb: (0, 0))],
        out_specs=tuple(specs),
        compiler_params=_cparams(2),
        name="proj_plain",
    )(hb3, w, wt)


def _proj_rope_kernel(x_ref, w_ref, c_ref, s_ref, *out_refs, heads):
    lane = lax.broadcasted_iota(I32, (x_ref.shape[0], MXU_N), 1)
    for g, o_ref in enumerate(out_refs):
        hd, half = heads[g]
        sl = slice(g * MXU_N, (g + 1) * MXU_N)
        acc = jnp.dot(x_ref[...], w_ref[:, sl], preferred_element_type=F32)
        partner = jnp.where((lane & (hd - 1)) < half,
                            pltpu.roll(acc, MXU_N - half, 1), pltpu.roll(acc, half, 1))
        o_ref[...] = (acc * c_ref[:, sl] + partner * s_ref[:, sl]).astype(o_ref.dtype)


def _proj_rope(hb3, w, ctab, stab, heads):
    b, t, d = hb3.shape
    tm = 256
    ncol = w.shape[1]
    assert ncol == MXU_N * len(heads)
    tspec = pl.BlockSpec((tm, ncol), lambda i, bb: (i, 0))
    ospec = pl.BlockSpec((None, tm, MXU_N), lambda i, bb: (bb, i, 0))
    return pl.pallas_call(
        functools.partial(_proj_rope_kernel, heads=heads),
        out_shape=(jax.ShapeDtypeStruct((b, t, MXU_N), BF16),) * len(heads),
        grid=(t // tm, b),
        in_specs=[pl.BlockSpec((None, tm, d), lambda i, bb: (bb, i, 0)),
                  pl.BlockSpec((d, ncol), lambda i, bb: (0, 0)), tspec, tspec],
        out_specs=(ospec,) * len(heads),
        compiler_params=_cparams(2),
        name="proj_rope",
    )(hb3, w, ctab, stab)


def _dsa_kernel(aq_ref, ak_ref, avt_ref, iq_ref, ik_ref, iw_ref, pick_ref, tri_ref, o_ref,
                keys_ref, hi_ref, lo_ref, iqt_ref, aqt_ref, wt_ref, thr_ref, s_ref, m_ref, acc_ref, ot_ref,
                *, topk, idx_scale):
    i = pl.program_id(1)
    nk = i + 1
    kpos = lax.broadcasted_iota(I32, (CK, TQ), 0)
    qpos = lax.broadcasted_iota(I32, (CK, TQ), 1)

    _masked_qt(iq_ref[...].astype(F32), 5, IDX_HEADS, iqt_ref)
    _masked_qt(aq_ref[...].astype(F32) * (HEAD_DIM ** -0.5 * LOG2E), 6, N_HEADS, aqt_ref)
    wt_ref[...] = _nt_dot(pick_ref[...], iw_ref[...]) * idx_scale

    def logits(c):
        kc = ik_ref[pl.ds(pl.multiple_of(c * CK, CK), CK), :]
        return [jnp.dot(kc, iqt_ref[hh], preferred_element_type=F32) for hh in range(IDX_HEADS)]

    def put_keys(c, key):
        keys_ref[c] = key
        hi_ref[c] = (key >> 16).astype(I16)
        lo_ref[c] = ((key & 0xFFFF) - HALF16).astype(I16)

    def score_chunk(c, lg, diag):
        sc = jnp.zeros((CK, TQ), F32)
        for hh in range(IDX_HEADS):
            sc = sc + jnp.maximum(lg[hh], 0.0) * wt_ref[hh:hh + 1, :]
        bits = pltpu.bitcast(sc, I32)
        key = jnp.where(bits < 0, INT_MIN - bits, bits)
        put_keys(c, jnp.where(kpos <= qpos, key, INT_MIN) if diag else key)

    def score_pair(p, carry):
        lg0, lg1 = logits(2 * p), logits(2 * p + 1)
        score_chunk(2 * p, lg0, False)
        score_chunk(2 * p + 1, lg1, False)
        return carry

    lax.fori_loop(0, lax.shift_right_logical(i, 1), score_pair, 0)

    @pl.when((i & 1) == 1)
    def _():
        score_chunk(i - 1, logits(i - 1), False)

    score_chunk(i, logits(i), True)

    @pl.when(jnp.logical_and((nk & 1) == 1, nk < keys_ref.shape[0]))
    def _():
        put_keys(nk, jnp.full((CK, TQ), INT_MIN, I32))

    def pair_loop(body, init):
        def pair(p, carry):
            return body(2 * p + 1, body(2 * p, carry))
        return lax.fori_loop(0, lax.shift_right_logical(nk + 1, 1), pair, init)

    def count16(ref, pred):
        def body(c, part):
            return part + _fold_rows(jnp.where(pred(ref[c]), jnp.int16(1), jnp.int16(0)), 2 * SUBLANES)
        part = pair_loop(body, jnp.zeros((2 * SUBLANES, TQ), I16))
        return jnp.sum(part.astype(F32), axis=0, keepdims=True)

    def search16(ref, need):
        def bit_body(bi, t_u):
            c_u = t_u | jnp.left_shift(jnp.int32(1), 15 - bi)
            ck = (c_u - HALF16).astype(I16)
            cnt = count16(ref, lambda v: v >= ck)
            return jnp.where(cnt >= need, c_u, t_u)
        return lax.fori_loop(0, 16, bit_body, jnp.zeros((1, TQ), I32))

    hi_u = search16(hi_ref, float(topk))
    thr_hi = (hi_u - HALF16).astype(I16)
    n_above = count16(hi_ref, lambda v: v > thr_hi)

    def bucket_body(c, carry):
        hi_ref[c] = jnp.where(hi_ref[c] == thr_hi, lo_ref[c], jnp.int16(-HALF16))
        return carry

    pair_loop(bucket_body, 0)
    lo_u = search16(hi_ref, float(topk) - n_above)
    thr = ((hi_u - HALF16) << 16) | lo_u

    def tie_body(c, carry):
        k = keys_ref[c]
        return (carry[0] + _fold_rows(jnp.where(k > thr, 1.0, 0.0)),
                carry[1] + _fold_rows(jnp.where(k == thr, 1.0, 0.0)))

    zero8 = jnp.zeros((SUBLANES, TQ), F32)
    gt8, eq8 = pair_loop(tie_body, (zero8, zero8))
    need = float(topk) - jnp.sum(gt8, axis=0, keepdims=True)
    amb = jnp.logical_and(jnp.sum(eq8, axis=0, keepdims=True) > need, thr > INT_MIN)
    any_amb = jnp.max(jnp.where(amb, 1.0, 0.0)) > 0.5

    @pl.when(any_amb)
    def _():
        def drop_body(c, seen):
            k = keys_ref[c]
            eq = k == thr
            eqf = jnp.where(eq, 1.0, 0.0)
            rank = jnp.dot(tri_ref[...], eqf.astype(BF16), preferred_element_type=F32) + seen
            drop = jnp.logical_and(jnp.logical_and(eq, rank > need), amb)
            keys_ref[c] = jnp.where(drop, INT_MIN, k)
            return seen + jnp.sum(eqf, axis=0, keepdims=True)

        lax.fori_loop(0, nk, drop_body, jnp.zeros((1, TQ), F32))

    thr_ref[...] = jnp.maximum(thr, INT_MIN + 1)

    def qk_all(c):
        kc = ak_ref[pl.ds(pl.multiple_of(c * CK, CK), CK), :]
        return [jnp.dot(kc, aqt_ref[h], preferred_element_type=F32) for h in range(N_HEADS)]

    _flash_loop(i, qk_all,
                lambda keep, h, s: jnp.where(keep, s, NEG),
                lambda c, h: avt_ref[c, h * VROWS:(h + 1) * VROWS, :],
                (s_ref, m_ref, acc_ref),
                prep=lambda c, diag: keys_ref[c] >= thr_ref[...])
    for h in range(N_HEADS):
        ot_ref[h * HEAD_DIM:(h + 1) * HEAD_DIM, :] = _softmax_out(acc_ref.at[h])
    o_ref[...] = ot_ref[...].T.astype(o_ref.dtype)


def _dsa(aq, ak, avt, iq, ik, iw):
    b, t, _ = aq.shape
    topk = min(TOPK_MAX, t // 4)
    qspec = pl.BlockSpec((None, TQ, BRANCH_W), lambda bb, i: (bb, i, 0))
    kspec, vspec = _kv_specs(t, BRANCH_W)
    pick = np.zeros((2 * SUBLANES, MXU_N), np.float32)
    for hh in range(IDX_HEADS):
        pick[hh, KV_LORA + hh] = 1.0
    pick = jnp.asarray(pick, BF16)
    tri = jnp.asarray(np.tril(np.ones((CK, CK), np.float32)), BF16)
    kern = functools.partial(_dsa_kernel, topk=topk, idx_scale=(IDX_HEADS * IDX_DIM) ** -0.5)
    return pl.pallas_call(
        kern,
        out_shape=jax.ShapeDtypeStruct((b, t, BRANCH_W), BF16),
        grid=(b, t // TQ),
        in_specs=[qspec, kspec, vspec, qspec, kspec, qspec,
                  pl.BlockSpec(pick.shape, lambda bb, i: (0, 0)), pl.BlockSpec(tri.shape, lambda bb, i: (0, 0))],
        out_specs=qspec,
        scratch_shapes=[
            pltpu.VMEM((t // CK, CK, TQ), I32),
            pltpu.VMEM((t // CK, CK, TQ), I16),
            pltpu.VMEM((t // CK, CK, TQ), I16),
            pltpu.VMEM((IDX_HEADS, BRANCH_W, TQ), BF16),
            pltpu.VMEM((N_HEADS, BRANCH_W, TQ), BF16),
            pltpu.VMEM((2 * SUBLANES, TQ), F32),
            pltpu.VMEM((1, TQ), I32),
        ] + _attn_scratch(N_HEADS),
        compiler_params=_cparams(2),
        name="dsa",
    )(aq, ak, avt, iq, ik, iw, pick, tri)


def _kbar_kernel(k_ref, o_ref):
    o_ref[...] = jnp.zeros(o_ref.shape, o_ref.dtype)
    nb = k_ref.shape[0] // MOBA_BLOCK
    for n in range(nb):
        blk = k_ref[n * MOBA_BLOCK:(n + 1) * MOBA_BLOCK, :].astype(F32)
        o_ref[n:n + 1, :] = jnp.mean(blk, axis=0, keepdims=True).astype(o_ref.dtype)


def _kbar(bk):
    b, t, w = bk.shape
    nbp = max(2 * SUBLANES, t // MOBA_BLOCK)
    return pl.pallas_call(
        _kbar_kernel,
        out_shape=jax.ShapeDtypeStruct((b, nbp, w), BF16),
        grid=(b,),
        in_specs=[pl.BlockSpec((None, t, w), lambda bb: (bb, 0, 0))],
        out_specs=pl.BlockSpec((None, nbp, w), lambda bb: (bb, 0, 0)),
        compiler_params=_cparams(1),
        name="moba_kbar",
    )(bk)


def _moba_kernel(q_ref, k_ref, vt_ref, kbar_ref, o_ref, qt_ref, bias_ref, s_ref, m_ref, acc_ref, ot_ref):
    i = pl.program_id(1)
    nbp = kbar_ref.shape[0]
    blk = lax.broadcasted_iota(I32, (nbp, TQ), 0)
    blk_f = blk.astype(F32)
    kpos = lax.broadcasted_iota(I32, (CK, TQ), 0)
    qpos = lax.broadcasted_iota(I32, (CK, TQ), 1)
    _masked_qt(q_ref[...].astype(F32) * (HEAD_DIM ** -0.5 * LOG2E), 6, N_HEADS, qt_ref)

    for h in range(N_HEADS):
        g = jnp.where(blk < i, jnp.dot(kbar_ref[...], qt_ref[h], preferred_element_type=F32), NEG)
        bias = jnp.full((nbp, TQ), NEG, F32)
        for _ in range(MOBA_TOPK):
            mx = jnp.max(g, axis=0, keepdims=True)
            first = jnp.min(jnp.where(g == mx, blk_f, 1e9), axis=0, keepdims=True)
            pick = jnp.logical_and(blk_f == first, mx > 0.5 * NEG)
            bias = jnp.where(pick, 0.0, bias)
            g = jnp.where(pick, NEG, g)
        bias_ref[h] = bias

    def qk_all(c):
        kc = k_ref[pl.ds(pl.multiple_of(c * CK, CK), CK), :]
        return [jnp.dot(kc, qt_ref[h], preferred_element_type=F32) for h in range(N_HEADS)]

    def mask(ctx, h, s):
        c, diag = ctx
        return jnp.where(kpos <= qpos, s, NEG) if diag else s + bias_ref[h, pl.ds(c, 1), :]

    _flash_loop(i, qk_all, mask, lambda c, h: vt_ref[c, h * VROWS:(h + 1) * VROWS, :],
                (s_ref, m_ref, acc_ref))
    for h in range(N_HEADS):
        ot_ref[h * HEAD_DIM:(h + 1) * HEAD_DIM, :] = _softmax_out(acc_ref.at[h])
    o_ref[...] = ot_ref[...].T.astype(o_ref.dtype)


def _moba(bq, bk, bvt, kbar):
    b, t, w = bq.shape
    assert TQ == MOBA_BLOCK and CK == MOBA_BLOCK and t % MOBA_BLOCK == 0
    nbp = kbar.shape[1]
    qspec = pl.BlockSpec((None, TQ, w), lambda bb, i: (bb, i, 0))
    kspec, vspec = _kv_specs(t, w)
    return pl.pallas_call(
        _moba_kernel,
        out_shape=jax.ShapeDtypeStruct((b, t, w), BF16),
        grid=(b, t // TQ),
        in_specs=[qspec, kspec, vspec, pl.BlockSpec((None, nbp, w), lambda bb, i: (bb, 0, 0))],
        out_specs=qspec,
        scratch_shapes=[pltpu.VMEM((N_HEADS, w, TQ), BF16), pltpu.VMEM((N_HEADS, nbp, TQ), F32)]
        + _attn_scratch(N_HEADS),
        compiler_params=_cparams(2),
        name="moba",
    )(bq, bk, bvt, kbar)


def _diff_kernel(q_ref, k_ref, vt_ref, lam_ref, norm_ref, misc_ref, o_ref,
                 qt_ref, s_ref, m_ref, acc_ref, ot_ref):
    i = pl.program_id(1)
    kpos = lax.broadcasted_iota(I32, (CK, TQ), 0)
    qpos = lax.broadcasted_iota(I32, (CK, TQ), 1)
    _masked_qt(q_ref[...].astype(F32) * (DIFF_DIM ** -0.5 * LOG2E), 5, 2 * N_HEADS, qt_ref)

    dl = lam_ref[...]
    lam_init = misc_ref[0:1, 0:1]
    lam = (jnp.exp(jnp.sum(dl[0:1, :] * dl[1:2, :], axis=1, keepdims=True))
           - jnp.exp(jnp.sum(dl[2:3, :] * dl[3:4, :], axis=1, keepdims=True)) + lam_init)

    def qk_all(c):
        kc = k_ref[pl.ds(pl.multiple_of(c * CK, CK), CK), :]
        return [jnp.dot(kc, qt_ref[j], preferred_element_type=F32) for j in range(2 * N_HEADS)]

    _flash_loop(i, qk_all,
                lambda ctx, j, s: jnp.where(kpos <= qpos, s, NEG) if ctx[1] else s,
                lambda c, j: vt_ref[c, (j // 2) * VROWS:(j // 2 + 1) * VROWS, :],
                (s_ref, m_ref, acc_ref))

    post = norm_ref[...] * (1.0 - lam_init)
    for h in range(N_HEADS):
        o_h = _softmax_out(acc_ref.at[2 * h]) - lam * _softmax_out(acc_ref.at[2 * h + 1])
        ms = jnp.mean(o_h * o_h, axis=0, keepdims=True)
        ot_ref[h * HEAD_DIM:(h + 1) * HEAD_DIM, :] = o_h * lax.rsqrt(ms + RMS_EPS) * post
    o_ref[...] = ot_ref[...].T.astype(o_ref.dtype)


def _diff(cq, ck, cvt, lam, norm, misc):
    b, t, w = cq.shape
    qspec = pl.BlockSpec((None, TQ, w), lambda bb, i: (bb, i, 0))
    kspec, vspec = _kv_specs(t, w)
    full = lambda a: pl.BlockSpec(a.shape, lambda bb, i: (0,) * a.ndim)
    return pl.pallas_call(
        _diff_kernel,
        out_shape=jax.ShapeDtypeStruct((b, t, w), BF16),
        grid=(b, t // TQ),
        in_specs=[qspec, kspec, vspec, full(lam), full(norm), full(misc)],
        out_specs=qspec,
        scratch_shapes=[pltpu.VMEM((2 * N_HEADS, w, TQ), BF16)] + _attn_scratch(2 * N_HEADS),
        compiler_params=_cparams(2),
        name="diff",
    )(cq, ck, cvt, lam, norm, misc)


def _mla_prep_kernel(cq_ref, ckv_ref, kr_ref, qn_ref, kvn_ref, wq_ref, wqr_ref, wk_ref, wvt_ref,
                     p_ref, ct_ref, st_ref, q_out, k_out, vt_out):
    x = cq_ref[...].astype(F32)
    xn = (x * lax.rsqrt(jnp.mean(x * x, axis=1, keepdims=True) + RMS_EPS) * qn_ref[...]).astype(BF16)
    q = (jnp.dot(xn, wq_ref[...], preferred_element_type=F32) * ct_ref[...]
         + jnp.dot(xn, wqr_ref[...], preferred_element_type=F32) * st_ref[...])
    q_out[...] = q.astype(q_out.dtype)
    c = ckv_ref[:, :KV_LORA].astype(F32)
    cn = (c * lax.rsqrt(jnp.mean(c * c, axis=1, keepdims=True) + RMS_EPS) * kvn_ref[...]).astype(BF16)
    k = (jnp.dot(cn, wk_ref[...], preferred_element_type=F32)
         + jnp.dot(kr_ref[...], p_ref[...], preferred_element_type=F32))
    k_out[...] = k.astype(k_out.dtype)
    _store_vt(vt_out, _tn_dot(wvt_ref[...], cn))


def _mla_prep(dcq, ckv, kr, qn, kvn, wq, wqr, wk, wvt, pmat, ct, st):
    b, t, _ = dcq.shape
    tm = 512
    hw = N_HEADS * LANES
    row = lambda w: pl.BlockSpec((None, tm, w), lambda i, bb: (bb, i, 0))
    full = lambda a: pl.BlockSpec(a.shape, lambda i, bb: (0,) * a.ndim)
    tab = pl.BlockSpec((tm, hw), lambda i, bb: (i, 0))
    return pl.pallas_call(
        _mla_prep_kernel,
        out_shape=(jax.ShapeDtypeStruct((b, t, hw), BF16), jax.ShapeDtypeStruct((b, t, hw), BF16),
                   jax.ShapeDtypeStruct((b, t // CK, N_HEADS * VROWS, CK), BF16)),
        grid=(t // tm, b),
        in_specs=[row(Q_LORA), row(MXU_N), row(MXU_N), full(qn), full(kvn), full(wq), full(wqr),
                  full(wk), full(wvt), full(pmat), tab, tab],
        out_specs=(row(hw), row(hw),
                   pl.BlockSpec((None, tm // CK, N_HEADS * VROWS, CK), lambda i, bb: (bb, i, 0, 0))),
        compiler_params=_cparams(2),
        name="mla_prep",
    )(dcq, ckv, kr, qn, kvn, wq, wqr, wk, wvt, pmat, ct, st)


def _mla_kernel(q_ref, k_ref, vt_ref, o_ref, qt_ref, s_ref, m_ref, acc_ref, ot_ref):
    i = pl.program_id(1)
    kpos = lax.broadcasted_iota(I32, (CK, TQ), 0)
    qpos = lax.broadcasted_iota(I32, (CK, TQ), 1)
    hs = [slice(h * LANES, (h + 1) * LANES) for h in range(N_HEADS)]
    for h in range(N_HEADS):
        qt_ref[h] = q_ref[:, hs[h]].astype(F32).T.astype(BF16)

    def qk_all(c):
        start = pl.multiple_of(c * CK, CK)
        return [jnp.dot(k_ref[pl.ds(start, CK), hs[h]], qt_ref[h], preferred_element_type=F32)
                for h in range(N_HEADS)]

    _flash_loop(i, qk_all,
                lambda ctx, h, s: jnp.where(kpos <= qpos, s, NEG) if ctx[1] else s,
                lambda c, h: vt_ref[c, h * VROWS:(h + 1) * VROWS, :],
                (s_ref, m_ref, acc_ref))
    for h in range(N_HEADS):
        ot_ref[h * HEAD_DIM:(h + 1) * HEAD_DIM, :] = _softmax_out(acc_ref.at[h])
    o_ref[...] = ot_ref[...].T.astype(o_ref.dtype)


def _mla(qm, km, vmt):
    b, t, hw = qm.shape
    kspec, vspec = _kv_specs(t, hw)
    return pl.pallas_call(
        _mla_kernel,
        out_shape=jax.ShapeDtypeStruct((b, t, BRANCH_W), BF16),
        grid=(b, t // TQ),
        in_specs=[pl.BlockSpec((None, TQ, hw), lambda bb, i: (bb, i, 0)), kspec, vspec],
        out_specs=pl.BlockSpec((None, TQ, BRANCH_W), lambda bb, i: (bb, i, 0)),
        scratch_shapes=[pltpu.VMEM((N_HEADS, LANES, TQ), BF16)] + _attn_scratch(N_HEADS),
        compiler_params=_cparams(2),
        name="mla",
    )(qm, km, vmt)


def _matmul_kernel(x_ref, w_ref, o_ref):
    o_ref[...] = jnp.dot(x_ref[...].astype(BF16), w_ref[...], preferred_element_type=F32).astype(o_ref.dtype)


def _mem_kv(mem, w):
    b, m, d = mem.shape
    n = w.shape[1]
    return pl.pallas_call(
        _matmul_kernel,
        out_shape=jax.ShapeDtypeStruct((b, m, n), BF16),
        grid=(b,),
        in_specs=[pl.BlockSpec((None, m, d), lambda bb: (bb, 0, 0)), pl.BlockSpec((d, n), lambda bb: (0, 0))],
        out_specs=pl.BlockSpec((None, m, n), lambda bb: (bb, 0, 0)),
        compiler_params=_cparams(1),
        name="mem_kv",
    )(mem, w)


def _mem_kernel(q_ref, kv_ref, o_ref):
    tq = q_ref.shape[0]
    lane_q = lax.broadcasted_iota(I32, (tq, BRANCH_W), 1)
    q = q_ref[...].astype(F32) * (HEAD_DIM ** -0.5)
    mk = kv_ref[:, :BRANCH_W]
    mv = kv_ref[:, BRANCH_W:]
    out = jnp.zeros((tq, BRANCH_W), F32)
    for h in range(N_HEADS):
        in_h = (lane_q >> 6) == h
        s = _nt_dot(jnp.where(in_h, q, 0.0).astype(BF16), mk)
        p = jnp.exp(s - jnp.max(s, axis=1, keepdims=True))
        o_h = jnp.dot(p.astype(BF16), mv, preferred_element_type=F32) / jnp.sum(p, axis=1, keepdims=True)
        out = jnp.where(in_h, o_h, out)
    o_ref[...] = out.astype(o_ref.dtype)


def _mem_attn(eq, mkv):
    b, t, w = eq.shape
    m = mkv.shape[1]
    tq = 512
    return pl.pallas_call(
        _mem_kernel,
        out_shape=jax.ShapeDtypeStruct((b, t, w), BF16),
        grid=(b, t // tq),
        in_specs=[pl.BlockSpec((None, tq, w), lambda bb, i: (bb, i, 0)),
                  pl.BlockSpec((None, m, 2 * w), lambda bb, i: (bb, 0, 0))],
        out_specs=pl.BlockSpec((None, tq, w), lambda bb, i: (bb, i, 0)),
        compiler_params=_cparams(2),
        name="mem_attn",
    )(eq, mkv)


def _final_kernel(h_ref, hb_ref, oa_ref, ob_ref, oc_ref, od_ref, oe_ref, z_ref,
                  wg_ref, wb_ref, wo_ref, g_ref, b_ref, h_out, hb_out, *, alpha):
    d = h_ref.shape[1]
    acc = jnp.zeros(h_ref.shape, F32)
    for n, o_ref in enumerate((oa_ref, ob_ref, oc_ref, od_ref, oe_ref)):
        z = z_ref[:, n * BRANCH_W:(n + 1) * BRANCH_W].astype(F32)
        y = o_ref[...].astype(F32) * (z / (1.0 + jnp.exp(-z)))
        u = jnp.dot(y.astype(BF16), wb_ref[n], preferred_element_type=F32)
        g = jnp.dot(hb_ref[...], wg_ref[:, n * d:(n + 1) * d], preferred_element_type=F32)
        acc = acc + u / (1.0 + jnp.exp(-g))
    out = jnp.dot(acc.astype(BF16), wo_ref[...], preferred_element_type=F32)
    x = alpha * h_ref[...] + out
    mu = jnp.mean(x, axis=1, keepdims=True)
    xc = x - mu
    var = jnp.mean(xc * xc, axis=1, keepdims=True)
    y = xc * lax.rsqrt(var + LN_EPS) * g_ref[...] + b_ref[...]
    h_out[...] = y
    hb_out[...] = y.astype(BF16)


def _final(h, hb, os5, z, wg, wb, wo, ln_g, ln_b, alpha):
    n, d = h.shape
    tm = 256
    row = lambda w: pl.BlockSpec((tm, w), lambda i: (i, 0))
    full = lambda a: pl.BlockSpec(a.shape, lambda i: (0,) * a.ndim)
    return pl.pallas_call(
        functools.partial(_final_kernel, alpha=alpha),
        out_shape=(jax.ShapeDtypeStruct((n, d), F32), jax.ShapeDtypeStruct((n, d), BF16)),
        grid=(n // tm,),
        in_specs=[row(d), row(d)] + [row(BRANCH_W)] * N_BRANCH + [row(N_BRANCH * BRANCH_W),
                  full(wg), full(wb), full(wo), full(ln_g), full(ln_b)],
        out_specs=(row(d), row(d)),
        compiler_params=_cparams(1),
        name="merge_out_ln",
    )(h, hb, *os5, z, wg, wb, wo, ln_g, ln_b)


def _rope_tables(seq, rot_dim):
    pos = jnp.arange(seq, dtype=F32)
    inv = ROPE_THETA ** (-jnp.arange(0, rot_dim, 2, dtype=F32) / rot_dim)
    ang = pos[:, None] * inv[None, :]
    return jnp.cos(ang), jnp.sin(ang)


def _rope_cs(t, nh, hd, r):
    cos, sin = _rope_tables(t, r)
    c = jnp.concatenate([cos, cos, jnp.ones((t, hd - r), F32)], axis=1)
    s = jnp.concatenate([-sin, sin, jnp.zeros((t, hd - r), F32)], axis=1)
    return jnp.tile(c, (1, nh)), jnp.tile(s, (1, nh))


def kernel(x, mem, ln0_g, ln0_b, w_in, mla_q_norm, w_uq, mla_kv_norm, w_ukv, diff_lam, diff_norm,
           w_mem_kv, w_branch, w_out, ln_g, ln_b):
    b, t, d = x.shape
    depth = w_in.shape[0]
    alpha = (2 * depth) ** 0.25
    assert t % 512 == 0 and d == 1024

    def seg(name):
        o, s = OFF[name]
        return w_in[:, :, o:o + s]

    zeros = lambda n: jnp.zeros((depth, d, n), w_in.dtype)

    w_plain = jnp.concatenate(
        [seg("d_cq"), seg("d_ckv"), seg("i_w"), zeros(MXU_N - KV_LORA - IDX_HEADS), seg("e_q"), seg("z")],
        axis=-1).astype(BF16)
    plain_widths = (BRANCH_W,) * 3 + (N_BRANCH * BRANCH_W,)
    w_vt = jnp.concatenate([seg("a_v"), seg("b_v"), seg("c_v")], axis=-1).astype(BF16)

    rope_groups = [
        (seg("a_q"), N_HEADS, HEAD_DIM, ROT_64), (seg("a_k"), N_HEADS, HEAD_DIM, ROT_64),
        (seg("i_q"), IDX_HEADS, IDX_DIM, ROT_32), (jnp.tile(seg("i_k"), (1, 1, IDX_HEADS)), IDX_HEADS, IDX_DIM, ROT_32),
        (seg("b_q"), N_HEADS, HEAD_DIM, ROT_64), (seg("b_k"), N_HEADS, HEAD_DIM, ROT_64),
        (seg("c_q"), 2 * N_HEADS, DIFF_DIM, ROT_32), (seg("c_k"), 2 * N_HEADS, DIFF_DIM, ROT_32),
        (jnp.concatenate([seg("d_kr"), zeros(MXU_N - MLA_ROPE)], axis=-1), 1, MXU_N, MLA_ROPE),
    ]
    w_rope = jnp.concatenate([g for g, *_ in rope_groups], axis=-1).astype(BF16)
    rope_heads = tuple((hd, r // 2) for _, _, hd, r in rope_groups)
    cs = [_rope_cs(t, nh, hd, r) for _, nh, hd, r in rope_groups]
    ctab = jnp.concatenate([c for c, _ in cs], axis=1)
    stab = jnp.concatenate([s for _, s in cs], axis=1)

    uq = w_uq.reshape(depth, Q_LORA, N_HEADS, MLA_NOPE + MLA_ROPE)
    qn_w, qr_w = uq[..., :MLA_NOPE], uq[..., MLA_NOPE:]
    pad32 = jnp.zeros((depth, Q_LORA, N_HEADS, LANES - MLA_NOPE - MLA_ROPE), w_uq.dtype)
    hw = N_HEADS * LANES
    wq = jnp.concatenate([qn_w, qr_w, pad32], axis=-1).reshape(depth, Q_LORA, hw).astype(BF16)
    half = MLA_ROPE // 2
    wq_rot = jnp.concatenate([jnp.zeros_like(qn_w), -qr_w[..., half:], qr_w[..., :half], pad32],
                             axis=-1).reshape(depth, Q_LORA, hw).astype(BF16)
    cos_m, sin_m = _rope_tables(t, MLA_ROPE)
    one = lambda n: jnp.ones((t, n), F32)
    zer = lambda n: jnp.zeros((t, n), F32)
    qs = (MLA_NOPE + MLA_ROPE) ** -0.5 * LOG2E
    ct_q = qs * jnp.tile(jnp.concatenate([one(MLA_NOPE), cos_m, cos_m, one(LANES - MLA_NOPE - MLA_ROPE)], axis=1), (1, N_HEADS))
    st_q = qs * jnp.tile(jnp.concatenate([zer(MLA_NOPE), sin_m, sin_m, zer(LANES - MLA_NOPE - MLA_ROPE)], axis=1), (1, N_HEADS))
    ukv = w_ukv.reshape(depth, KV_LORA, N_HEADS, MLA_NOPE + MLA_V)
    wk = jnp.concatenate([ukv[..., :MLA_NOPE], jnp.zeros((depth, KV_LORA, N_HEADS, LANES - MLA_NOPE), w_ukv.dtype)],
                         axis=-1).reshape(depth, KV_LORA, hw).astype(BF16)
    wvt = ukv[..., MLA_NOPE:].reshape(depth, KV_LORA, N_HEADS * MLA_V).astype(BF16)
    place = np.zeros((MXU_N, hw), np.float32)
    for hh in range(N_HEADS):
        for j in range(MLA_ROPE):
            place[j, hh * LANES + MLA_NOPE + j] = 1.0
    place = jnp.asarray(place, BF16)

    wg = seg("g").astype(BF16)
    wb = w_branch.astype(BF16)
    wo = w_out.astype(BF16)
    wmem = w_mem_kv.astype(BF16)
    norm_t = jnp.broadcast_to(diff_norm.astype(F32)[:, :, None], (depth, HEAD_DIM, TQ))

    h, hb = _layer_norm0(x.reshape(b * t, d), ln0_g, ln0_b)
    for l in range(depth):
        hb3 = hb.reshape(b, t, d)
        avt, bvt, cvt, dcq, ckv_iw, eq, z = _proj_plain(hb3, w_plain[l], w_vt[l], plain_widths)
        aq, ak, iq, ik, bq, bk, cq, ck, kr = _proj_rope(hb3, w_rope[l], ctab, stab, rope_heads)

        o_a = _dsa(aq, ak, avt, iq, ik, ckv_iw)
        o_b = _moba(bq, bk, bvt, _kbar(bk))
        lam_init = 0.8 - 0.6 * math.exp(-0.3 * l)
        misc = jnp.full((SUBLANES, LANES), lam_init, F32)
        o_c = _diff(cq, ck, cvt, diff_lam[l].astype(F32), norm_t[l], misc)
        qm, km, vmt = _mla_prep(dcq, ckv_iw, kr, mla_q_norm[l].reshape(1, Q_LORA), mla_kv_norm[l].reshape(1, KV_LORA),
                                wq[l], wq_rot[l], wk[l], wvt[l], place, ct_q, st_q)
        o_d = _mla(qm, km, vmt)
        o_e = _mem_attn(eq, _mem_kv(mem, wmem[l]))

        os5 = [o.reshape(b * t, BRANCH_W) for o in (o_a, o_b, o_c, o_d, o_e)]
        h, hb = _final(h, hb, os5, z.reshape(b * t, N_BRANCH * BRANCH_W), wg[l], wb[l], wo[l],
                       ln_g[l].reshape(1, d), ln_b[l].reshape(1, d), alpha)
    return h.reshape(b, t, d)
```

```python
import functools
import math

import numpy as np
import jax
import jax.numpy as jnp
from jax import lax
from jax.experimental import pallas as pl
from jax.experimental.pallas import tpu as pltpu

F32 = jnp.float32
BF16 = jnp.bfloat16
I32 = jnp.int32
I16 = jnp.int16

N_HEADS = 4
HEAD_DIM = 64
BRANCH_W = N_HEADS * HEAD_DIM
N_BRANCH = 5
ROPE_THETA = 500000.0
ROT_64 = 16
ROT_32 = 8
IDX_HEADS = 8
IDX_DIM = 32
TOPK_MAX = 256
MOBA_BLOCK = 256
MOBA_TOPK = 3
DIFF_DIM = 32
Q_LORA = 256
KV_LORA = 128
MLA_NOPE = 64
MLA_ROPE = 32
MLA_V = 64
LN_EPS = 1e-5
RMS_EPS = 1e-6

IN_LAYOUT = (
    ("a_q", BRANCH_W), ("a_k", BRANCH_W), ("a_v", BRANCH_W),
    ("i_q", IDX_HEADS * IDX_DIM), ("i_k", IDX_DIM), ("i_w", IDX_HEADS),
    ("b_q", BRANCH_W), ("b_k", BRANCH_W), ("b_v", BRANCH_W),
    ("c_q", BRANCH_W), ("c_k", BRANCH_W), ("c_v", BRANCH_W),
    ("d_cq", Q_LORA), ("d_ckv", KV_LORA), ("d_kr", MLA_ROPE),
    ("e_q", BRANCH_W),
    ("z", N_BRANCH * BRANCH_W),
    ("g", N_BRANCH * 1024),
)

SUBLANES = 8
LANES = 128
MXU_N = 256
TQ = 256
CK = 256
VROWS = HEAD_DIM + 16
FLASH_UNROLL = 4
NEG = -1e30
LOG2E = math.log2(math.e)
INT_MIN = np.int32(-2 ** 31)
HALF16 = 1 << 15
VMEM_LIMIT = 56 * 1024 * 1024


def _offsets():
    off, out = 0, {}
    for name, size in IN_LAYOUT:
        out[name] = (off, size)
        off += size
    return out


OFF = _offsets()


def _nt_dot(a, b):
    return lax.dot_general(a, b, (((1,), (1,)), ((), ())), preferred_element_type=F32)


def _tn_dot(w, x):
    return lax.dot_general(w, x, (((0,), (1,)), ((), ())), preferred_element_type=F32)


def _fold_rows(w, rows=SUBLANES):
    xs = [w[r:r + rows, :] for r in range(0, w.shape[0], rows)]
    while len(xs) > 1:
        xs = [xs[j] + xs[j + 1] for j in range(0, len(xs) - 1, 2)] + ([xs[-1]] if len(xs) % 2 else [])
    return xs[0]


def _masked_qt(q, shift, n, qt_ref):
    qt = q.T
    dim = lax.broadcasted_iota(I32, qt.shape, 0)
    for j in range(n):
        qt_ref[j] = jnp.where((dim >> shift) == j, qt, 0.0).astype(BF16)


def _cparams(n_axes):
    return pltpu.CompilerParams(dimension_semantics=("arbitrary",) * n_axes,
                                vmem_limit_bytes=VMEM_LIMIT)


def _softmax_step(s_t, vt_h, m_ref, acc_ref):
    m_old = m_ref[...]
    m_new = jnp.maximum(m_old, jnp.max(s_t, axis=0, keepdims=True))
    alpha = jnp.exp2(m_old - m_new)
    p = jnp.exp2(s_t - m_new)
    acc_ref[...] = alpha * acc_ref[...] + jnp.dot(vt_h, p.astype(BF16), preferred_element_type=F32)
    m_ref[...] = m_new


def _softmax_init(m_ref, acc_ref):
    m_ref[...] = jnp.full(m_ref.shape, NEG, F32)
    acc_ref[...] = jnp.zeros(acc_ref.shape, F32)


def _softmax_out(acc_ref):
    return acc_ref[:HEAD_DIM, :] / acc_ref[HEAD_DIM:HEAD_DIM + 1, :]


def _store_vt(o_ref, vt):
    ones = jnp.ones((VROWS - HEAD_DIM, CK), o_ref.dtype)
    for j in range(o_ref.shape[0]):
        for h in range(N_HEADS):
            o_ref[j, h * VROWS:h * VROWS + HEAD_DIM, :] = (
                vt[h * HEAD_DIM:(h + 1) * HEAD_DIM, j * CK:(j + 1) * CK].astype(o_ref.dtype))
            o_ref[j, h * VROWS + HEAD_DIM:(h + 1) * VROWS, :] = ones


def _flash_loop(n_prev, qk_all, mask, vt_rows, state, prep=None):
    s_ref, m_ref, acc_ref = state
    n_state = m_ref.shape[0]
    for j in range(n_state):
        _softmax_init(m_ref.at[j], acc_ref.at[j])

    def park(c, slot):
        for j, s in enumerate(qk_all(c)):
            s_ref[slot, j] = s

    def consume(c, slot, diag):
        ctx = (c, diag) if prep is None else prep(c, diag)
        for j in range(n_state):
            _softmax_step(mask(ctx, j, s_ref[slot, j]), vt_rows(c, j),
                          m_ref.at[j], acc_ref.at[j])

    park(0, 0)

    def pair(c):
        park(c + 1, 1)
        consume(c, 0, False)
        park(c + 2, 0)
        consume(c + 1, 1, False)

    def body(g, carry):
        for u in range(0, FLASH_UNROLL, 2):
            pair(FLASH_UNROLL * g + u)
        return carry

    n_group = lax.shift_right_logical(n_prev, FLASH_UNROLL.bit_length() - 1)
    lax.fori_loop(0, n_group, body, 0)
    c0 = FLASH_UNROLL * n_group
    for u in range(FLASH_UNROLL // 2 - 1):
        @pl.when(n_prev - c0 >= 2 * (u + 1))
        def _(u=u):
            pair(c0 + 2 * u)
    c0 = c0 + 2 * lax.shift_right_logical(n_prev - c0, 1)
    odd = (n_prev & 1) == 1

    @pl.when(odd)
    def _():
        park(c0 + 1, 1)
        consume(c0, 0, False)
        consume(c0 + 1, 1, True)

    @pl.when(jnp.logical_not(odd))
    def _():
        consume(c0, 0, True)


def _attn_scratch(n_state):
    return [pltpu.VMEM((2, n_state, CK, TQ), F32), pltpu.VMEM((n_state, 1, TQ), F32),
            pltpu.VMEM((n_state, VROWS, TQ), F32), pltpu.VMEM((BRANCH_W, TQ), F32)]


def _kv_specs(t, w):
    kspec = pl.BlockSpec((None, t, w), lambda bb, i: (bb, 0, 0))
    vspec = pl.BlockSpec((None, t // CK, N_HEADS * VROWS, CK), lambda bb, i: (bb, 0, 0, 0))
    return kspec, vspec


def _ln_kernel(x_ref, g_ref, b_ref, h_ref, hb_ref):
    x = x_ref[...]
    mu = jnp.mean(x, axis=1, keepdims=True)
    xc = x - mu
    var = jnp.mean(xc * xc, axis=1, keepdims=True)
    y = xc * lax.rsqrt(var + LN_EPS) * g_ref[...] + b_ref[...]
    h_ref[...] = y
    hb_ref[...] = y.astype(BF16)


def _layer_norm0(x2, g, b):
    n, d = x2.shape
    tm = 512
    row = pl.BlockSpec((tm, d), lambda i: (i, 0))
    vec = pl.BlockSpec((1, d), lambda i: (0, 0))
    return pl.pallas_call(
        _ln_kernel,
        out_shape=(jax.ShapeDtypeStruct((n, d), F32), jax.ShapeDtypeStruct((n, d), BF16)),
        grid=(n // tm,),
        in_specs=[row, vec, vec],
        out_specs=(row, row),
        compiler_params=_cparams(1),
        name="ln0",
    )(x2, g.reshape(1, d), b.reshape(1, d))


def _proj_plain_kernel(x_ref, w_ref, wt_ref, *out_refs, n_t):
    for g, o_ref in enumerate(out_refs[:n_t]):
        _store_vt(o_ref, _tn_dot(wt_ref[:, g * BRANCH_W:(g + 1) * BRANCH_W], x_ref[...]))
    off = 0
    for o_ref in out_refs[n_t:]:
        wd = o_ref.shape[-1]
        for j in range(0, wd, MXU_N):
            acc = jnp.dot(x_ref[...], w_ref[:, off + j:off + j + MXU_N], preferred_element_type=F32)
            o_ref[:, j:j + MXU_N] = acc.astype(o_ref.dtype)
        off += wd


def _proj_plain(hb3, w, wt, widths):
    b, t, d = hb3.shape
    tm = 512
    n_t = wt.shape[1] // BRANCH_W
    shapes = [jax.ShapeDtypeStruct((b, t // CK, N_HEADS * VROWS, CK), BF16)] * n_t
    specs = [pl.BlockSpec((None, tm // CK, N_HEADS * VROWS, CK), lambda i, bb: (bb, i, 0, 0))] * n_t
    shapes += [jax.ShapeDtypeStruct((b, t, wd), BF16) for wd in widths]
    specs += [pl.BlockSpec((None, tm, wd), lambda i, bb: (bb, i, 0)) for wd in widths]
    return pl.pallas_call(
        functools.partial(_proj_plain_kernel, n_t=n_t),
        out_shape=tuple(shapes),
        grid=(t // tm, b),
        in_specs=[pl.BlockSpec((None, tm, d), lambda i, bb: (bb, i, 0)),
                  pl.BlockSpec(w.shape, lambda i, bb: (0, 0)),
                  pl.BlockSpec(wt.shape, lambda i, bb: (0, 0))],
        out_specs=tuple(specs),
        compiler_params=_cparams(2),
        name="proj_plain",
    )(hb3, w, wt)


def _proj_rope_kernel(x_ref, w_ref, c_ref, s_ref, *out_refs, heads):
    lane = lax.broadcasted_iota(I32, (x_ref.shape[0], MXU_N), 1)
    for g, o_ref in enumerate(out_refs):
        hd, half = heads[g]
        sl = slice(g * MXU_N, (g + 1) * MXU_N)
        acc = jnp.dot(x_ref[...], w_ref[:, sl], preferred_element_type=F32)
        partner = jnp.where((lane & (hd - 1)) < half,
                            pltpu.roll(acc, MXU_N - half, 1), pltpu.roll(acc, half, 1))
        o_ref[...] = (acc * c_ref[:, sl] + partner * s_ref[:, sl]).astype(o_ref.dtype)


def _proj_rope(hb3, w, ctab, stab, heads):
    b, t, d = hb3.shape
    tm = 512
    ncol = w.shape[1]
    assert ncol == MXU_N * len(heads)
    tspec = pl.BlockSpec((tm, ncol), lambda i, bb: (i, 0))
    ospec = pl.BlockSpec((None, tm, MXU_N), lambda i, bb: (bb, i, 0))
    return pl.pallas_call(
        functools.partial(_proj_rope_kernel, heads=heads),
        out_shape=(jax.ShapeDtypeStruct((b, t, MXU_N), BF16),) * len(heads),
        grid=(t // tm, b),
        in_specs=[pl.BlockSpec((None, tm, d), lambda i, bb: (bb, i, 0)),
                  pl.BlockSpec((d, ncol), lambda i, bb: (0, 0)), tspec, tspec],
        out_specs=(ospec,) * len(heads),
        compiler_params=_cparams(2),
        name="proj_rope",
    )(hb3, w, ctab, stab)


def _dsa_kernel(aq_ref, ak_ref, avt_ref, iq_ref, ik_ref, iw_ref, pick_ref, tri_ref, o_ref,
                keys_ref, hi_ref, lo_ref, iqt_ref, aqt_ref, wt_ref, thr_ref, s_ref, m_ref, acc_ref, ot_ref,
                *, topk, idx_scale):
    i = pl.program_id(1)
    nk = i + 1
    kpos = lax.broadcasted_iota(I32, (CK, TQ), 0)
    qpos = lax.broadcasted_iota(I32, (CK, TQ), 1)

    _masked_qt(iq_ref[...].astype(F32), 5, IDX_HEADS, iqt_ref)
    _masked_qt(aq_ref[...].astype(F32) * (HEAD_DIM ** -0.5 * LOG2E), 6, N_HEADS, aqt_ref)
    wt_ref[...] = _nt_dot(pick_ref[...], iw_ref[...]) * idx_scale

    def logits(c):
        kc = ik_ref[pl.ds(pl.multiple_of(c * CK, CK), CK), :]
        return [jnp.dot(kc, iqt_ref[hh], preferred_element_type=F32) for hh in range(IDX_HEADS)]

    def put_keys(c, key):
        keys_ref[c] = key
        hi_ref[c] = (key >> 16).astype(I16)
        lo_ref[c] = ((key & 0xFFFF) - HALF16).astype(I16)

    def score_chunk(c, lg, diag):
        sc = jnp.zeros((CK, TQ), F32)
        for hh in range(IDX_HEADS):
            sc = sc + jnp.maximum(lg[hh], 0.0) * wt_ref[hh:hh + 1, :]
        bits = pltpu.bitcast(sc, I32)
        key = jnp.where(bits < 0, INT_MIN - bits, bits)
        put_keys(c, jnp.where(kpos <= qpos, key, INT_MIN) if diag else key)

    def score_pair(p, carry):
        lg0, lg1 = logits(2 * p), logits(2 * p + 1)
        score_chunk(2 * p, lg0, False)
        score_chunk(2 * p + 1, lg1, False)
        return carry

    lax.fori_loop(0, lax.shift_right_logical(i, 1), score_pair, 0)

    @pl.when((i & 1) == 1)
    def _():
        score_chunk(i - 1, logits(i - 1), False)

    score_chunk(i, logits(i), True)

    @pl.when(jnp.logical_and((nk & 1) == 1, nk < keys_ref.shape[0]))
    def _():
        put_keys(nk, jnp.full((CK, TQ), INT_MIN, I32))

    def pair_loop(body, init):
        def pair(p, carry):
            return body(2 * p + 1, body(2 * p, carry))
        return lax.fori_loop(0, lax.shift_right_logical(nk + 1, 1), pair, init)

    def count16(ref, pred):
        def body(c, part):
            return part + _fold_rows(jnp.where(pred(ref[c]), jnp.int16(1), jnp.int16(0)), 2 * SUBLANES)
        part = pair_loop(body, jnp.zeros((2 * SUBLANES, TQ), I16))
        return jnp.sum(part.astype(F32), axis=0, keepdims=True)

    def search16(ref, need):
        def bit_body(bi, t_u):
            c_u = t_u | jnp.left_shift(jnp.int32(1), 15 - bi)
            ck = (c_u - HALF16).astype(I16)
            cnt = count16(ref, lambda v: v >= ck)
            return jnp.where(cnt >= need, c_u, t_u)
        return lax.fori_loop(0, 16, bit_body, jnp.zeros((1, TQ), I32))

    hi_u = search16(hi_ref, float(topk))
    thr_hi = (hi_u - HALF16).astype(I16)
    n_above = count16(hi_ref, lambda v: v > thr_hi)

    def bucket_body(c, carry):
        hi_ref[c] = jnp.where(hi_ref[c] == thr_hi, lo_ref[c], jnp.int16(-HALF16))
        return carry

    pair_loop(bucket_body, 0)
    lo_u = search16(hi_ref, float(topk) - n_above)
    thr = ((hi_u - HALF16) << 16) | lo_u

    def tie_body(c, carry):
        k = keys_ref[c]
        return (carry[0] + _fold_rows(jnp.where(k > thr, 1.0, 0.0)),
                carry[1] + _fold_rows(jnp.where(k == thr, 1.0, 0.0)))

    zero8 = jnp.zeros((SUBLANES, TQ), F32)
    gt8, eq8 = pair_loop(tie_body, (zero8, zero8))
    need = float(topk) - jnp.sum(gt8, axis=0, keepdims=True)
    amb = jnp.logical_and(jnp.sum(eq8, axis=0, keepdims=True) > need, thr > INT_MIN)
    any_amb = jnp.max(jnp.where(amb, 1.0, 0.0)) > 0.5

    @pl.when(any_amb)
    def _():
        def drop_body(c, seen):
            k = keys_ref[c]
            eq = k == thr
            eqf = jnp.where(eq, 1.0, 0.0)
            rank = jnp.dot(tri_ref[...], eqf.astype(BF16), preferred_element_type=F32) + seen
            drop = jnp.logical_and(jnp.logical_and(eq, rank > need), amb)
            keys_ref[c] = jnp.where(drop, INT_MIN, k)
            return seen + jnp.sum(eqf, axis=0, keepdims=True)

        lax.fori_loop(0, nk, drop_body, jnp.zeros((1, TQ), F32))

    thr_ref[...] = jnp.maximum(thr, INT_MIN + 1)

    def qk_all(c):
        kc = ak_ref[pl.ds(pl.multiple_of(c * CK, CK), CK), :]
        return [jnp.dot(kc, aqt_ref[h], preferred_element_type=F32) for h in range(N_HEADS)]

    _flash_loop(i, qk_all,
                lambda keep, h, s: jnp.where(keep, s, NEG),
                lambda c, h: avt_ref[c, h * VROWS:(h + 1) * VROWS, :],
                (s_ref, m_ref, acc_ref),
                prep=lambda c, diag: keys_ref[c] >= thr_ref[...])
    for h in range(N_HEADS):
        ot_ref[h * HEAD_DIM:(h + 1) * HEAD_DIM, :] = _softmax_out(acc_ref.at[h])
    o_ref[...] = ot_ref[...].T.astype(o_ref.dtype)


def _dsa(aq, ak, avt, iq, ik, iw):
    b, t, _ = aq.shape
    topk = min(TOPK_MAX, t // 4)
    qspec = pl.BlockSpec((None, TQ, BRANCH_W), lambda bb, i: (bb, i, 0))
    kspec, vspec = _kv_specs(t, BRANCH_W)
    pick = np.zeros((2 * SUBLANES, MXU_N), np.float32)
    for hh in range(IDX_HEADS):
        pick[hh, KV_LORA + hh] = 1.0
    pick = jnp.asarray(pick, BF16)
    tri = jnp.asarray(np.tril(np.ones((CK, CK), np.float32)), BF16)
    kern = functools.partial(_dsa_kernel, topk=topk, idx_scale=(IDX_HEADS * IDX_DIM) ** -0.5)
    return pl.pallas_call(
        kern,
        out_shape=jax.ShapeDtypeStruct((b, t, BRANCH_W), BF16),
        grid=(b, t // TQ),
        in_specs=[qspec, kspec, vspec, qspec, kspec, qspec,
                  pl.BlockSpec(pick.shape, lambda bb, i: (0, 0)), pl.BlockSpec(tri.shape, lambda bb, i: (0, 0))],
        out_specs=qspec,
        scratch_shapes=[
            pltpu.VMEM((t // CK, CK, TQ), I32),
            pltpu.VMEM((t // CK, CK, TQ), I16),
            pltpu.VMEM((t // CK, CK, TQ), I16),
            pltpu.VMEM((IDX_HEADS, BRANCH_W, TQ), BF16),
            pltpu.VMEM((N_HEADS, BRANCH_W, TQ), BF16),
            pltpu.VMEM((2 * SUBLANES, TQ), F32),
            pltpu.VMEM((1, TQ), I32),
        ] + _attn_scratch(N_HEADS),
        compiler_params=_cparams(2),
        name="dsa",
    )(aq, ak, avt, iq, ik, iw, pick, tri)


def _kbar_kernel(k_ref, o_ref):
    o_ref[...] = jnp.zeros(o_ref.shape, o_ref.dtype)
    nb = k_ref.shape[0] // MOBA_BLOCK
    for n in range(nb):
        blk = k_ref[n * MOBA_BLOCK:(n + 1) * MOBA_BLOCK, :].astype(F32)
        o_ref[n:n + 1, :] = jnp.mean(blk, axis=0, keepdims=True).astype(o_ref.dtype)


def _kbar(bk):
    b, t, w = bk.shape
    nbp = max(2 * SUBLANES, t // MOBA_BLOCK)
    return pl.pallas_call(
        _kbar_kernel,
        out_shape=jax.ShapeDtypeStruct((b, nbp, w), BF16),
        grid=(b,),
        in_specs=[pl.BlockSpec((None, t, w), lambda bb: (bb, 0, 0))],
        out_specs=pl.BlockSpec((None, nbp, w), lambda bb: (bb, 0, 0)),
        compiler_params=_cparams(1),
        name="moba_kbar",
    )(bk)


def _moba_kernel(q_ref, k_ref, vt_ref, kbar_ref, o_ref, qt_ref, bias_ref, s_ref, m_ref, acc_ref, ot_ref):
    i = pl.program_id(1)
    nbp = kbar_ref.shape[0]
    blk = lax.broadcasted_iota(I32, (nbp, TQ), 0)
    blk_f = blk.astype(F32)
    kpos = lax.broadcasted_iota(I32, (CK, TQ), 0)
    qpos = lax.broadcasted_iota(I32, (CK, TQ), 1)
    _masked_qt(q_ref[...].astype(F32) * (HEAD_DIM ** -0.5 * LOG2E), 6, N_HEADS, qt_ref)

    for h in range(N_HEADS):
        g = jnp.where(blk < i, jnp.dot(kbar_ref[...], qt_ref[h], preferred_element_type=F32), NEG)
        bias = jnp.full((nbp, TQ), NEG, F32)
        for _ in range(MOBA_TOPK):
            mx = jnp.max(g, axis=0, keepdims=True)
            first = jnp.min(jnp.where(g == mx, blk_f, 1e9), axis=0, keepdims=True)
            pick = jnp.logical_and(blk_f == first, mx > 0.5 * NEG)
            bias = jnp.where(pick, 0.0, bias)
            g = jnp.where(pick, NEG, g)
        bias_ref[h] = bias

    def qk_all(c):
        kc = k_ref[pl.ds(pl.multiple_of(c * CK, CK), CK), :]
        return [jnp.dot(kc, qt_ref[h], preferred_element_type=F32) for h in range(N_HEADS)]

    def mask(ctx, h, s):
        c, diag = ctx
        return jnp.where(kpos <= qpos, s, NEG) if diag else s + bias_ref[h, pl.ds(c, 1), :]

    _flash_loop(i, qk_all, mask, lambda c, h: vt_ref[c, h * VROWS:(h + 1) * VROWS, :],
                (s_ref, m_ref, acc_ref))
    for h in range(N_HEADS):
        ot_ref[h * HEAD_DIM:(h + 1) * HEAD_DIM, :] = _softmax_out(acc_ref.at[h])
    o_ref[...] = ot_ref[...].T.astype(o_ref.dtype)


def _moba(bq, bk, bvt, kbar):
    b, t, w = bq.shape
    assert TQ == MOBA_BLOCK and CK == MOBA_BLOCK and t % MOBA_BLOCK == 0
    nbp = kbar.shape[1]
    qspec = pl.BlockSpec((None, TQ, w), lambda bb, i: (bb, i, 0))
    kspec, vspec = _kv_specs(t, w)
    return pl.pallas_call(
        _moba_kernel,
        out_shape=jax.ShapeDtypeStruct((b, t, w), BF16),
        grid=(b, t // TQ),
        in_specs=[qspec, kspec, vspec, pl.BlockSpec((None, nbp, w), lambda bb, i: (bb, 0, 0))],
        out_specs=qspec,
        scratch_shapes=[pltpu.VMEM((N_HEADS, w, TQ), BF16), pltpu.VMEM((N_HEADS, nbp, TQ), F32)]
        + _attn_scratch(N_HEADS),
        compiler_params=_cparams(2),
        name="moba",
    )(bq, bk, bvt, kbar)


def _diff_kernel(q_ref, k_ref, vt_ref, lam_ref, norm_ref, misc_ref, o_ref,
                 qt_ref, s_ref, m_ref, acc_ref, ot_ref):
    i = pl.program_id(1)
    kpos = lax.broadcasted_iota(I32, (CK, TQ), 0)
    qpos = lax.broadcasted_iota(I32, (CK, TQ), 1)
    _masked_qt(q_ref[...].astype(F32) * (DIFF_DIM ** -0.5 * LOG2E), 5, 2 * N_HEADS, qt_ref)

    dl = lam_ref[...]
    lam_init = misc_ref[0:1, 0:1]
    lam = (jnp.exp(jnp.sum(dl[0:1, :] * dl[1:2, :], axis=1, keepdims=True))
           - jnp.exp(jnp.sum(dl[2:3, :] * dl[3:4, :], axis=1, keepdims=True)) + lam_init)

    def qk_all(c):
        kc = k_ref[pl.ds(pl.multiple_of(c * CK, CK), CK), :]
        return [jnp.dot(kc, qt_ref[j], preferred_element_type=F32) for j in range(2 * N_HEADS)]

    _flash_loop(i, qk_all,
                lambda ctx, j, s: jnp.where(kpos <= qpos, s, NEG) if ctx[1] else s,
                lambda c, j: vt_ref[c, (j // 2) * VROWS:(j // 2 + 1) * VROWS, :],
                (s_ref, m_ref, acc_ref))

    post = norm_ref[...] * (1.0 - lam_init)
    for h in range(N_HEADS):
        o_h = _softmax_out(acc_ref.at[2 * h]) - lam * _softmax_out(acc_ref.at[2 * h + 1])
        ms = jnp.mean(o_h * o_h, axis=0, keepdims=True)
        ot_ref[h * HEAD_DIM:(h + 1) * HEAD_DIM, :] = o_h * lax.rsqrt(ms + RMS_EPS) * post
    o_ref[...] = ot_ref[...].T.astype(o_ref.dtype)


def _diff(cq, ck, cvt, lam, norm, misc):
    b, t, w = cq.shape
    qspec = pl.BlockSpec((None, TQ, w), lambda bb, i: (bb, i, 0))
    kspec, vspec = _kv_specs(t, w)
    full = lambda a: pl.BlockSpec(a.shape, lambda bb, i: (0,) * a.ndim)
    return pl.pallas_call(
        _diff_kernel,
        out_shape=jax.ShapeDtypeStruct((b, t, w), BF16),
        grid=(b, t // TQ),
        in_specs=[qspec, kspec, vspec, full(lam), full(norm), full(misc)],
        out_specs=qspec,
        scratch_shapes=[pltpu.VMEM((2 * N_HEADS, w, TQ), BF16)] + _attn_scratch(2 * N_HEADS),
        compiler_params=_cparams(2),
        name="diff",
    )(cq, ck, cvt, lam, norm, misc)


def _mla_prep_kernel(cq_ref, ckv_ref, kr_ref, qn_ref, kvn_ref, wq_ref, wqr_ref, wk_ref, wvt_ref,
                     p_ref, ct_ref, st_ref, q_out, k_out, vt_out):
    x = cq_ref[...].astype(F32)
    xn = (x * lax.rsqrt(jnp.mean(x * x, axis=1, keepdims=True) + RMS_EPS) * qn_ref[...]).astype(BF16)
    q = (jnp.dot(xn, wq_ref[...], preferred_element_type=F32) * ct_ref[...]
         + jnp.dot(xn, wqr_ref[...], preferred_element_type=F32) * st_ref[...])
    q_out[...] = q.astype(q_out.dtype)
    c = ckv_ref[:, :KV_LORA].astype(F32)
    cn = (c * lax.rsqrt(jnp.mean(c * c, axis=1, keepdims=True) + RMS_EPS) * kvn_ref[...]).astype(BF16)
    k = (jnp.dot(cn, wk_ref[...], preferred_element_type=F32)
         + jnp.dot(kr_ref[...], p_ref[...], preferred_element_type=F32))
    k_out[...] = k.astype(k_out.dtype)
    _store_vt(vt_out, _tn_dot(wvt_ref[...], cn))


def _mla_prep(dcq, ckv, kr, qn, kvn, wq, wqr, wk, wvt, pmat, ct, st):
    b, t, _ = dcq.shape
    tm = 512
    hw = N_HEADS * LANES
    row = lambda w: pl.BlockSpec((None, tm, w), lambda i, bb: (bb, i, 0))
    full = lambda a: pl.BlockSpec(a.shape, lambda i, bb: (0,) * a.ndim)
    tab = pl.BlockSpec((tm, hw), lambda i, bb: (i, 0))
    return pl.pallas_call(
        _mla_prep_kernel,
        out_shape=(jax.ShapeDtypeStruct((b, t, hw), BF16), jax.ShapeDtypeStruct((b, t, hw), BF16),
                   jax.ShapeDtypeStruct((b, t // CK, N_HEADS * VROWS, CK), BF16)),
        grid=(t // tm, b),
        in_specs=[row(Q_LORA), row(MXU_N), row(MXU_N), full(qn), full(kvn), full(wq), full(wqr),
                  full(wk), full(wvt), full(pmat), tab, tab],
        out_specs=(row(hw), row(hw),
                   pl.BlockSpec((None, tm // CK, N_HEADS * VROWS, CK), lambda i, bb: (bb, i, 0, 0))),
        compiler_params=_cparams(2),
        name="mla_prep",
    )(dcq, ckv, kr, qn, kvn, wq, wqr, wk, wvt, pmat, ct, st)


def _mla_kernel(q_ref, k_ref, vt_ref, o_ref, qt_ref, s_ref, m_ref, acc_ref, ot_ref):
    i = pl.program_id(1)
    kpos = lax.broadcasted_iota(I32, (CK, TQ), 0)
    qpos = lax.broadcasted_iota(I32, (CK, TQ), 1)
    hs = [slice(h * LANES, (h + 1) * LANES) for h in range(N_HEADS)]
    for h in range(N_HEADS):
        qt_ref[h] = q_ref[:, hs[h]].astype(F32).T.astype(BF16)

    def qk_all(c):
        start = pl.multiple_of(c * CK, CK)
        return [jnp.dot(k_ref[pl.ds(start, CK), hs[h]], qt_ref[h], preferred_element_type=F32)
                for h in range(N_HEADS)]

    _flash_loop(i, qk_all,
                lambda ctx, h, s: jnp.where(kpos <= qpos, s, NEG) if ctx[1] else s,
                lambda c, h: vt_ref[c, h * VROWS:(h + 1) * VROWS, :],
                (s_ref, m_ref, acc_ref))
    for h in range(N_HEADS):
        ot_ref[h * HEAD_DIM:(h + 1) * HEAD_DIM, :] = _softmax_out(acc_ref.at[h])
    o_ref[...] = ot_ref[...].T.astype(o_ref.dtype)


def _mla(qm, km, vmt):
    b, t, hw = qm.shape
    kspec, vspec = _kv_specs(t, hw)
    return pl.pallas_call(
        _mla_kernel,
        out_shape=jax.ShapeDtypeStruct((b, t, BRANCH_W), BF16),
        grid=(b, t // TQ),
        in_specs=[pl.BlockSpec((None, TQ, hw), lambda bb, i: (bb, i, 0)), kspec, vspec],
        out_specs=pl.BlockSpec((None, TQ, BRANCH_W), lambda bb, i: (bb, i, 0)),
        scratch_shapes=[pltpu.VMEM((N_HEADS, LANES, TQ), BF16)] + _attn_scratch(N_HEADS),
        compiler_params=_cparams(2),
        name="mla",
    )(qm, km, vmt)


def _matmul_kernel(x_ref, w_ref, o_ref):
    o_ref[...] = jnp.dot(x_ref[...].astype(BF16), w_ref[...], preferred_element_type=F32).astype(o_ref.dtype)


def _mem_kv(mem, w):
    b, m, d = mem.shape
    n = w.shape[1]
    return pl.pallas_call(
        _matmul_kernel,
        out_shape=jax.ShapeDtypeStruct((b, m, n), BF16),
        grid=(b,),
        in_specs=[pl.BlockSpec((None, m, d), lambda bb: (bb, 0, 0)), pl.BlockSpec((d, n), lambda bb: (0, 0))],
        out_specs=pl.BlockSpec((None, m, n), lambda bb: (bb, 0, 0)),
        compiler_params=_cparams(1),
        name="mem_kv",
    )(mem, w)


def _mem_kernel(q_ref, kv_ref, o_ref):
    tq = q_ref.shape[0]
    lane_q = lax.broadcasted_iota(I32, (tq, BRANCH_W), 1)
    q = q_ref[...].astype(F32) * (HEAD_DIM ** -0.5)
    mk = kv_ref[:, :BRANCH_W]
    mv = kv_ref[:, BRANCH_W:]
    out = jnp.zeros((tq, BRANCH_W), F32)
    for h in range(N_HEADS):
        in_h = (lane_q >> 6) == h
        s = _nt_dot(jnp.where(in_h, q, 0.0).astype(BF16), mk)
        p = jnp.exp(s - jnp.max(s, axis=1, keepdims=True))
        o_h = jnp.dot(p.astype(BF16), mv, preferred_element_type=F32) / jnp.sum(p, axis=1, keepdims=True)
        out = jnp.where(in_h, o_h, out)
    o_ref[...] = out.astype(o_ref.dtype)


def _mem_attn(eq, mkv):
    b, t, w = eq.shape
    m = mkv.shape[1]
    tq = 512
    return pl.pallas_call(
        _mem_kernel,
        out_shape=jax.ShapeDtypeStruct((b, t, w), BF16),
        grid=(b, t // tq),
        in_specs=[pl.BlockSpec((None, tq, w), lambda bb, i: (bb, i, 0)),
                  pl.BlockSpec((None, m, 2 * w), lambda bb, i: (bb, 0, 0))],
        out_specs=pl.BlockSpec((None, tq, w), lambda bb, i: (bb, i, 0)),
        compiler_params=_cparams(2),
        name="mem_attn",
    )(eq, mkv)


def _final_kernel(h_ref, hb_ref, oa_ref, ob_ref, oc_ref, od_ref, oe_ref, z_ref,
                  wg_ref, wb_ref, wo_ref, g_ref, b_ref, h_out, hb_out, *, alpha):
    d = h_ref.shape[1]
    acc = jnp.zeros(h_ref.shape, F32)
    for n, o_ref in enumerate((oa_ref, ob_ref, oc_ref, od_ref, oe_ref)):
        z = z_ref[:, n * BRANCH_W:(n + 1) * BRANCH_W].astype(F32)
        y = o_ref[...].astype(F32) * (z / (1.0 + jnp.exp(-z)))
        u = jnp.dot(y.astype(BF16), wb_ref[n], preferred_element_type=F32)
        g = jnp.dot(hb_ref[...], wg_ref[:, n * d:(n + 1) * d], preferred_element_type=F32)
        acc = acc + u / (1.0 + jnp.exp(-g))
    out = jnp.dot(acc.astype(BF16), wo_ref[...], preferred_element_type=F32)
    x = alpha * h_ref[...] + out
    mu = jnp.mean(x, axis=1, keepdims=True)
    xc = x - mu
    var = jnp.mean(xc * xc, axis=1, keepdims=True)
    y = xc * lax.rsqrt(var + LN_EPS) * g_ref[...] + b_ref[...]
    h_out[...] = y
    hb_out[...] = y.astype(BF16)


def _final(h, hb, os5, z, wg, wb, wo, ln_g, ln_b, alpha):
    n, d = h.shape
    tm = 256
    row = lambda w: pl.BlockSpec((tm, w), lambda i: (i, 0))
    full = lambda a: pl.BlockSpec(a.shape, lambda i: (0,) * a.ndim)
    return pl.pallas_call(
        functools.partial(_final_kernel, alpha=alpha),
        out_shape=(jax.ShapeDtypeStruct((n, d), F32), jax.ShapeDtypeStruct((n, d), BF16)),
        grid=(n // tm,),
        in_specs=[row(d), row(d)] + [row(BRANCH_W)] * N_BRANCH + [row(N_BRANCH * BRANCH_W),
                  full(wg), full(wb), full(wo), full(ln_g), full(ln_b)],
        out_specs=(row(d), row(d)),
        compiler_params=_cparams(1),
        name="merge_out_ln",
    )(h, hb, *os5, z, wg, wb, wo, ln_g, ln_b)


def _rope_tables(seq, rot_dim):
    pos = jnp.arange(seq, dtype=F32)
    inv = ROPE_THETA ** (-jnp.arange(0, rot_dim, 2, dtype=F32) / rot_dim)
    ang = pos[:, None] * inv[None, :]
    return jnp.cos(ang), jnp.sin(ang)


def _rope_cs(t, nh, hd, r):
    cos, sin = _rope_tables(t, r)
    c = jnp.concatenate([cos, cos, jnp.ones((t, hd - r), F32)], axis=1)
    s = jnp.concatenate([-sin, sin, jnp.zeros((t, hd - r), F32)], axis=1)
    return jnp.tile(c, (1, nh)), jnp.tile(s, (1, nh))


def kernel(x, mem, ln0_g, ln0_b, w_in, mla_q_norm, w_uq, mla_kv_norm, w_ukv, diff_lam, diff_norm,
           w_mem_kv, w_branch, w_out, ln_g, ln_b):
    b, t, d = x.shape
    depth = w_in.shape[0]
    alpha = (2 * depth) ** 0.25
    assert t % 512 == 0 and d == 1024

    def seg(name):
        o, s = OFF[name]
        return w_in[:, :, o:o + s]

    zeros = lambda n: jnp.zeros((depth, d, n), w_in.dtype)

    w_plain = jnp.concatenate(
        [seg("d_cq"), seg("d_ckv"), seg("i_w"), zeros(MXU_N - KV_LORA - IDX_HEADS), seg("e_q"), seg("z")],
        axis=-1).astype(BF16)
    plain_widths = (BRANCH_W,) * 3 + (N_BRANCH * BRANCH_W,)
    w_vt = jnp.concatenate([seg("a_v"), seg("b_v"), seg("c_v")], axis=-1).astype(BF16)

    rope_groups = [
        (seg("a_q"), N_HEADS, HEAD_DIM, ROT_64), (seg("a_k"), N_HEADS, HEAD_DIM, ROT_64),
        (seg("i_q"), IDX_HEADS, IDX_DIM, ROT_32), (jnp.tile(seg("i_k"), (1, 1, IDX_HEADS)), IDX_HEADS, IDX_DIM, ROT_32),
        (seg("b_q"), N_HEADS, HEAD_DIM, ROT_64), (seg("b_k"), N_HEADS, HEAD_DIM, ROT_64),
        (seg("c_q"), 2 * N_HEADS, DIFF_DIM, ROT_32), (seg("c_k"), 2 * N_HEADS, DIFF_DIM, ROT_32),
        (jnp.concatenate([seg("d_kr"), zeros(MXU_N - MLA_ROPE)], axis=-1), 1, MXU_N, MLA_ROPE),
    ]
    w_rope = jnp.concatenate([g for g, *_ in rope_groups], axis=-1).astype(BF16)
    rope_heads = tuple((hd, r // 2) for _, _, hd, r in rope_groups)
    cs = [_rope_cs(t, nh, hd, r) for _, nh, hd, r in rope_groups]
    ctab = jnp.concatenate([c for c, _ in cs], axis=1)
    stab = jnp.concatenate([s for _, s in cs], axis=1)

    uq = w_uq.reshape(depth, Q_LORA, N_HEADS, MLA_NOPE + MLA_ROPE)
    qn_w, qr_w = uq[..., :MLA_NOPE], uq[..., MLA_NOPE:]
    pad32 = jnp.zeros((depth, Q_LORA, N_HEADS, LANES - MLA_NOPE - MLA_ROPE), w_uq.dtype)
    hw = N_HEADS * LANES
    wq = jnp.concatenate([qn_w, qr_w, pad32], axis=-1).reshape(depth, Q_LORA, hw).astype(BF16)
    half = MLA_ROPE // 2
    wq_rot = jnp.concatenate([jnp.zeros_like(qn_w), -qr_w[..., half:], qr_w[..., :half], pad32],
                             axis=-1).reshape(depth, Q_LORA, hw).astype(BF16)
    cos_m, sin_m = _rope_tables(t, MLA_ROPE)
    one = lambda n: jnp.ones((t, n), F32)
    zer = lambda n: jnp.zeros((t, n), F32)
    qs = (MLA_NOPE + MLA_ROPE) ** -0.5 * LOG2E
    ct_q = qs * jnp.tile(jnp.concatenate([one(MLA_NOPE), cos_m, cos_m, one(LANES - MLA_NOPE - MLA_ROPE)], axis=1), (1, N_HEADS))
    st_q = qs * jnp.tile(jnp.concatenate([zer(MLA_NOPE), sin_m, sin_m, zer(LANES - MLA_NOPE - MLA_ROPE)], axis=1), (1, N_HEADS))
    ukv = w_ukv.reshape(depth, KV_LORA, N_HEADS, MLA_NOPE + MLA_V)
    wk = jnp.concatenate([ukv[..., :MLA_NOPE], jnp.zeros((depth, KV_LORA, N_HEADS, LANES - MLA_NOPE), w_ukv.dtype)],
                         axis=-1).reshape(depth, KV_LORA, hw).astype(BF16)
    wvt = ukv[..., MLA_NOPE:].reshape(depth, KV_LORA, N_HEADS * MLA_V).astype(BF16)
    place = np.zeros((MXU_N, hw), np.float32)
    for hh in range(N_HEADS):
        for j in range(MLA_ROPE):
            place[j, hh * LANES + MLA_NOPE + j] = 1.0
    place = jnp.asarray(place, BF16)

    wg = seg("g").astype(BF16)
    wb = w_branch.astype(BF16)
    wo = w_out.astype(BF16)
    wmem = w_mem_kv.astype(BF16)
    norm_t = jnp.broadcast_to(diff_norm.astype(F32)[:, :, None], (depth, HEAD_DIM, TQ))

    h, hb = _layer_norm0(x.reshape(b * t, d), ln0_g, ln0_b)
    for l in range(depth):
        hb3 = hb.reshape(b, t, d)
        avt, bvt, cvt, dcq, ckv_iw, eq, z = _proj_plain(hb3, w_plain[l], w_vt[l], plain_widths)
        aq, ak, iq, ik, bq, bk, cq, ck, kr = _proj_rope(hb3, w_rope[l], ctab, stab, rope_heads)

        o_a = _dsa(aq, ak, avt, iq, ik, ckv_iw)
        o_b = _moba(bq, bk, bvt, _kbar(bk))
        lam_init = 0.8 - 0.6 * math.exp(-0.3 * l)
        misc = jnp.full((SUBLANES, LANES), lam_init, F32)
        o_c = _diff(cq, ck, cvt, diff_lam[l].astype(F32), norm_t[l], misc)
        qm, km, vmt = _mla_prep(dcq, ckv_iw, kr, mla_q_norm[l].reshape(1, Q_LORA), mla_kv_norm[l].reshape(1, KV_LORA),
                                wq[l], wq_rot[l], wk[l], wvt[l], place, ct_q, st_q)
        o_d = _mla(qm, km, vmt)
        o_e = _mem_attn(eq, _mem_kv(mem, wmem[l]))

        os5 = [o.reshape(b * t, BRANCH_W) for o in (o_a, o_b, o_c, o_d, o_e)]
        h, hb = _final(h, hb, os5, z.reshape(b * t, N_BRANCH * BRANCH_W), wg[l], wb[l], wo[l],
                       ln_g[l].reshape(1, d), ln_b[l].reshape(1, d), alpha)
    return h.reshape(b, t, d)
```

```python
import functools
import math

import numpy as np
import jax
import jax.numpy as jnp
from jax import lax
from jax.experimental import pallas as pl
from jax.experimental.pallas import tpu as pltpu

F32 = jnp.float32
BF16 = jnp.bfloat16
I32 = jnp.int32
I16 = jnp.int16

N_HEADS = 4
HEAD_DIM = 64
BRANCH_W = N_HEADS * HEAD_DIM
N_BRANCH = 5
ROPE_THETA = 500000.0
ROT_64 = 16
ROT_32 = 8
IDX_HEADS = 8
IDX_DIM = 32
TOPK_MAX = 256
MOBA_BLOCK = 256
MOBA_TOPK = 3
DIFF_DIM = 32
Q_LORA = 256
KV_LORA = 128
MLA_NOPE = 64
MLA_ROPE = 32
MLA_V = 64
LN_EPS = 1e-5
RMS_EPS = 1e-6

IN_LAYOUT = (
    ("a_q", BRANCH_W), ("a_k", BRANCH_W), ("a_v", BRANCH_W),
    ("i_q", IDX_HEADS * IDX_DIM), ("i_k", IDX_DIM), ("i_w", IDX_HEADS),
    ("b_q", BRANCH_W), ("b_k", BRANCH_W), ("b_v", BRANCH_W),
    ("c_q", BRANCH_W), ("c_k", BRANCH_W), ("c_v", BRANCH_W),
    ("d_cq", Q_LORA), ("d_ckv", KV_LORA), ("d_kr", MLA_ROPE),
    ("e_q", BRANCH_W),
    ("z", N_BRANCH * BRANCH_W),
    ("g", N_BRANCH * 1024),
)

SUBLANES = 8
LANES = 128
MXU_N = 256
TQ = 256
CK = 256
VROWS = HEAD_DIM + 16
FLASH_UNROLL = 4
NEG = -1e30
LOG2E = math.log2(math.e)
INT_MIN = np.int32(-2 ** 31)
HALF16 = 1 << 15
VMEM_LIMIT = 56 * 1024 * 1024


def _offsets():
    off, out = 0, {}
    for name, size in IN_LAYOUT:
        out[name] = (off, size)
        off += size
    return out


OFF = _offsets()


def _nt_dot(a, b):
    return lax.dot_general(a, b, (((1,), (1,)), ((), ())), preferred_element_type=F32)


def _tn_dot(w, x):
    return lax.dot_general(w, x, (((0,), (1,)), ((), ())), preferred_element_type=F32)


def _fold_rows(w, rows=SUBLANES):
    xs = [w[r:r + rows, :] for r in range(0, w.shape[0], rows)]
    while len(xs) > 1:
        xs = [xs[j] + xs[j + 1] for j in range(0, len(xs) - 1, 2)] + ([xs[-1]] if len(xs) % 2 else [])
    return xs[0]


def _masked_qt(q, shift, n, qt_ref):
    qt = q.T
    dim = lax.broadcasted_iota(I32, (LANES, qt.shape[1]), 0)
    for j in range(n):
        half = (j << shift) // LANES
        rows = qt[half * LANES:(half + 1) * LANES, :]
        qt_ref[j] = jnp.where(((dim + half * LANES) >> shift) == j, rows, 0.0).astype(BF16)


def _half(kc, j, shift):
    half = (j << shift) // LANES
    return kc[:, half * LANES:(half + 1) * LANES]


def _cparams(n_axes):
    return pltpu.CompilerParams(dimension_semantics=("arbitrary",) * n_axes,
                                vmem_limit_bytes=VMEM_LIMIT)


def _softmax_step(s_t, vt_h, m_ref, acc_ref):
    m_old = m_ref[...]
    m_new = jnp.maximum(m_old, jnp.max(s_t, axis=0, keepdims=True))
    alpha = jnp.exp2(m_old - m_new)
    p = jnp.exp2(s_t - m_new)
    acc_ref[...] = alpha * acc_ref[...] + jnp.dot(vt_h, p.astype(BF16), preferred_element_type=F32)
    m_ref[...] = m_new


def _softmax_init(m_ref, acc_ref):
    m_ref[...] = jnp.full(m_ref.shape, NEG, F32)
    acc_ref[...] = jnp.zeros(acc_ref.shape, F32)


def _softmax_out(acc_ref):
    return acc_ref[:HEAD_DIM, :] / acc_ref[HEAD_DIM:HEAD_DIM + 1, :]


def _store_vt(o_ref, vt):
    ones = jnp.ones((VROWS - HEAD_DIM, CK), o_ref.dtype)
    for j in range(o_ref.shape[0]):
        for h in range(N_HEADS):
            o_ref[j, h * VROWS:h * VROWS + HEAD_DIM, :] = (
                vt[h * HEAD_DIM:(h + 1) * HEAD_DIM, j * CK:(j + 1) * CK].astype(o_ref.dtype))
            o_ref[j, h * VROWS + HEAD_DIM:(h + 1) * VROWS, :] = ones


def _flash_loop(n_prev, qk_all, mask, vt_rows, state, prep=None):
    s_ref, m_ref, acc_ref = state
    n_state = m_ref.shape[0]
    for j in range(n_state):
        _softmax_init(m_ref.at[j], acc_ref.at[j])

    def park(c, slot):
        for j, s in enumerate(qk_all(c)):
            s_ref[slot, j] = s

    def consume(c, slot, diag):
        ctx = (c, diag) if prep is None else prep(c, diag)
        for j in range(n_state):
            _softmax_step(mask(ctx, j, s_ref[slot, j]), vt_rows(c, j),
                          m_ref.at[j], acc_ref.at[j])

    park(0, 0)

    def pair(c):
        park(c + 1, 1)
        consume(c, 0, False)
        park(c + 2, 0)
        consume(c + 1, 1, False)

    def body(g, carry):
        for u in range(0, FLASH_UNROLL, 2):
            pair(FLASH_UNROLL * g + u)
        return carry

    n_group = lax.shift_right_logical(n_prev, FLASH_UNROLL.bit_length() - 1)
    lax.fori_loop(0, n_group, body, 0)
    c0 = FLASH_UNROLL * n_group
    for u in range(FLASH_UNROLL // 2 - 1):
        @pl.when(n_prev - c0 >= 2 * (u + 1))
        def _(u=u):
            pair(c0 + 2 * u)
    c0 = c0 + 2 * lax.shift_right_logical(n_prev - c0, 1)
    odd = (n_prev & 1) == 1

    @pl.when(odd)
    def _():
        park(c0 + 1, 1)
        consume(c0, 0, False)
        consume(c0 + 1, 1, True)

    @pl.when(jnp.logical_not(odd))
    def _():
        consume(c0, 0, True)


def _attn_scratch(n_state):
    return [pltpu.VMEM((2, n_state, CK, TQ), F32), pltpu.VMEM((n_state, 1, TQ), F32),
            pltpu.VMEM((n_state, VROWS, TQ), F32), pltpu.VMEM((BRANCH_W, TQ), F32)]


def _kv_specs(t, w):
    kspec = pl.BlockSpec((None, t, w), lambda bb, i: (bb, 0, 0))
    vspec = pl.BlockSpec((None, t // CK, N_HEADS * VROWS, CK), lambda bb, i: (bb, 0, 0, 0))
    return kspec, vspec


def _ln_kernel(x_ref, g_ref, b_ref, h_ref, hb_ref):
    x = x_ref[...]
    mu = jnp.mean(x, axis=1, keepdims=True)
    xc = x - mu
    var = jnp.mean(xc * xc, axis=1, keepdims=True)
    y = xc * lax.rsqrt(var + LN_EPS) * g_ref[...] + b_ref[...]
    h_ref[...] = y
    hb_ref[...] = y.astype(BF16)


def _layer_norm0(x2, g, b):
    n, d = x2.shape
    tm = 512
    row = pl.BlockSpec((tm, d), lambda i: (i, 0))
    vec = pl.BlockSpec((1, d), lambda i: (0, 0))
    return pl.pallas_call(
        _ln_kernel,
        out_shape=(jax.ShapeDtypeStruct((n, d), F32), jax.ShapeDtypeStruct((n, d), BF16)),
        grid=(n // tm,),
        in_specs=[row, vec, vec],
        out_specs=(row, row),
        compiler_params=_cparams(1),
        name="ln0",
    )(x2, g.reshape(1, d), b.reshape(1, d))


def _proj_plain_kernel(x_ref, w_ref, wt_ref, *out_refs, n_t):
    for g, o_ref in enumerate(out_refs[:n_t]):
        _store_vt(o_ref, _tn_dot(wt_ref[:, g * BRANCH_W:(g + 1) * BRANCH_W], x_ref[...]))
    off = 0
    for o_ref in out_refs[n_t:]:
        wd = o_ref.shape[-1]
        for j in range(0, wd, MXU_N):
            acc = jnp.dot(x_ref[...], w_ref[:, off + j:off + j + MXU_N], preferred_element_type=F32)
            o_ref[:, j:j + MXU_N] = acc.astype(o_ref.dtype)
        off += wd


def _proj_plain(hb3, w, wt, widths):
    b, t, d = hb3.shape
    tm = 512
    n_t = wt.shape[1] // BRANCH_W
    shapes = [jax.ShapeDtypeStruct((b, t // CK, N_HEADS * VROWS, CK), BF16)] * n_t
    specs = [pl.BlockSpec((None, tm // CK, N_HEADS * VROWS, CK), lambda i, bb: (bb, i, 0, 0))] * n_t
    shapes += [jax.ShapeDtypeStruct((b, t, wd), BF16) for wd in widths]
    specs += [pl.BlockSpec((None, tm, wd), lambda i, bb: (bb, i, 0)) for wd in widths]
    return pl.pallas_call(
        functools.partial(_proj_plain_kernel, n_t=n_t),
        out_shape=tuple(shapes),
        grid=(t // tm, b),
        in_specs=[pl.BlockSpec((None, tm, d), lambda i, bb: (bb, i, 0)),
                  pl.BlockSpec(w.shape, lambda i, bb: (0, 0)),
                  pl.BlockSpec(wt.shape, lambda i, bb: (0, 0))],
        out_specs=tuple(specs),
        compiler_params=_cparams(2),
        name="proj_plain",
    )(hb3, w, wt)


def _proj_rope_kernel(x_ref, w_ref, c_ref, s_ref, *out_refs, heads):
    lane = lax.broadcasted_iota(I32, (x_ref.shape[0], MXU_N), 1)
    for g, o_ref in enumerate(out_refs):
        hd, half = heads[g]
        sl = slice(g * MXU_N, (g + 1) * MXU_N)
        acc = jnp.dot(x_ref[...], w_ref[:, sl], preferred_element_type=F32)
        partner = jnp.where((lane & (hd - 1)) < half,
                            pltpu.roll(acc, MXU_N - half, 1), pltpu.roll(acc, half, 1))
        o_ref[...] = (acc * c_ref[:, sl] + partner * s_ref[:, sl]).astype(o_ref.dtype)


def _proj_rope(hb3, w, ctab, stab, heads):
    b, t, d = hb3.shape
    tm = 512
    ncol = w.shape[1]
    assert ncol == MXU_N * len(heads)
    tspec = pl.BlockSpec((tm, ncol), lambda i, bb: (i, 0))
    ospec = pl.BlockSpec((None, tm, MXU_N), lambda i, bb: (bb, i, 0))
    return pl.pallas_call(
        functools.partial(_proj_rope_kernel, heads=heads),
        out_shape=(jax.ShapeDtypeStruct((b, t, MXU_N), BF16),) * len(heads),
        grid=(t // tm, b),
        in_specs=[pl.BlockSpec((None, tm, d), lambda i, bb: (bb, i, 0)),
                  pl.BlockSpec((d, ncol), lambda i, bb: (0, 0)), tspec, tspec],
        out_specs=(ospec,) * len(heads),
        compiler_params=_cparams(2),
        name="proj_rope",
    )(hb3, w, ctab, stab)


def _dsa_kernel(aq_ref, ak_ref, avt_ref, iq_ref, ik_ref, iw_ref, pick_ref, tri_ref, o_ref,
                keys_ref, hi_ref, lo_ref, iqt_ref, aqt_ref, wt_ref, thr_ref, s_ref, m_ref, acc_ref, ot_ref,
                *, topk, idx_scale):
    i = pl.program_id(1)
    nk = i + 1
    kpos = lax.broadcasted_iota(I32, (CK, TQ), 0)
    qpos = lax.broadcasted_iota(I32, (CK, TQ), 1)

    _masked_qt(iq_ref[...].astype(F32), 5, IDX_HEADS, iqt_ref)
    _masked_qt(aq_ref[...].astype(F32) * (HEAD_DIM ** -0.5 * LOG2E), 6, N_HEADS, aqt_ref)
    wt_ref[...] = _nt_dot(pick_ref[...], iw_ref[...]) * idx_scale

    def logits(c):
        kc = ik_ref[pl.ds(pl.multiple_of(c * CK, CK), CK), :]
        return [jnp.dot(_half(kc, hh, 5), iqt_ref[hh], preferred_element_type=F32) for hh in range(IDX_HEADS)]

    def put_keys(c, key):
        keys_ref[c] = key
        hi_ref[c] = (key >> 16).astype(I16)
        lo_ref[c] = ((key & 0xFFFF) - HALF16).astype(I16)

    def score_chunk(c, lg, diag):
        sc = jnp.zeros((CK, TQ), F32)
        for hh in range(IDX_HEADS):
            sc = sc + jnp.maximum(lg[hh], 0.0) * wt_ref[hh:hh + 1, :]
        bits = pltpu.bitcast(sc, I32)
        key = jnp.where(bits < 0, INT_MIN - bits, bits)
        put_keys(c, jnp.where(kpos <= qpos, key, INT_MIN) if diag else key)

    def score_pair(p, carry):
        lg0, lg1 = logits(2 * p), logits(2 * p + 1)
        score_chunk(2 * p, lg0, False)
        score_chunk(2 * p + 1, lg1, False)
        return carry

    lax.fori_loop(0, lax.shift_right_logical(i, 1), score_pair, 0)

    @pl.when((i & 1) == 1)
    def _():
        score_chunk(i - 1, logits(i - 1), False)

    score_chunk(i, logits(i), True)

    @pl.when(jnp.logical_and((nk & 1) == 1, nk < keys_ref.shape[0]))
    def _():
        put_keys(nk, jnp.full((CK, TQ), INT_MIN, I32))

    def pair_loop(body, init):
        def pair(p, carry):
            return body(2 * p + 1, body(2 * p, carry))
        return lax.fori_loop(0, lax.shift_right_logical(nk + 1, 1), pair, init)

    def count16(ref, pred):
        def body(c, part):
            return part + _fold_rows(jnp.where(pred(ref[c]), jnp.int16(1), jnp.int16(0)), 2 * SUBLANES)
        part = pair_loop(body, jnp.zeros((2 * SUBLANES, TQ), I16))
        return jnp.sum(part.astype(F32), axis=0, keepdims=True)

    def search16(ref, need):
        def bit_body(bi, t_u):
            c_u = t_u | jnp.left_shift(jnp.int32(1), 15 - bi)
            ck = (c_u - HALF16).astype(I16)
            cnt = count16(ref, lambda v: v >= ck)
            return jnp.where(cnt >= need, c_u, t_u)
        return lax.fori_loop(0, 16, bit_body, jnp.zeros((1, TQ), I32))

    hi_u = search16(hi_ref, float(topk))
    thr_hi = (hi_u - HALF16).astype(I16)
    n_above = count16(hi_ref, lambda v: v > thr_hi)

    def bucket_body(c, carry):
        hi_ref[c] = jnp.where(hi_ref[c] == thr_hi, lo_ref[c], jnp.int16(-HALF16))
        return carry

    pair_loop(bucket_body, 0)
    lo_u = search16(hi_ref, float(topk) - n_above)
    thr = ((hi_u - HALF16) << 16) | lo_u

    def tie_body(c, carry):
        k = keys_ref[c]
        return (carry[0] + _fold_rows(jnp.where(k > thr, 1.0, 0.0)),
                carry[1] + _fold_rows(jnp.where(k == thr, 1.0, 0.0)))

    zero8 = jnp.zeros((SUBLANES, TQ), F32)
    gt8, eq8 = pair_loop(tie_body, (zero8, zero8))
    need = float(topk) - jnp.sum(gt8, axis=0, keepdims=True)
    amb = jnp.logical_and(jnp.sum(eq8, axis=0, keepdims=True) > need, thr > INT_MIN)
    any_amb = jnp.max(jnp.where(amb, 1.0, 0.0)) > 0.5

    @pl.when(any_amb)
    def _():
        def drop_body(c, seen):
            k = keys_ref[c]
            eq = k == thr
            eqf = jnp.where(eq, 1.0, 0.0)
            rank = jnp.dot(tri_ref[...], eqf.astype(BF16), preferred_element_type=F32) + seen
            drop = jnp.logical_and(jnp.logical_and(eq, rank > need), amb)
            keys_ref[c] = jnp.where(drop, INT_MIN, k)
            return seen + jnp.sum(eqf, axis=0, keepdims=True)

        lax.fori_loop(0, nk, drop_body, jnp.zeros((1, TQ), F32))

    thr_ref[...] = jnp.maximum(thr, INT_MIN + 1)

    def qk_all(c):
        kc = ak_ref[pl.ds(pl.multiple_of(c * CK, CK), CK), :]
        return [jnp.dot(_half(kc, h, 6), aqt_ref[h], preferred_element_type=F32) for h in range(N_HEADS)]

    _flash_loop(i, qk_all,
                lambda keep, h, s: jnp.where(keep, s, NEG),
                lambda c, h: avt_ref[c, h * VROWS:(h + 1) * VROWS, :],
                (s_ref, m_ref, acc_ref),
                prep=lambda c, diag: keys_ref[c] >= thr_ref[...])
    for h in range(N_HEADS):
        ot_ref[h * HEAD_DIM:(h + 1) * HEAD_DIM, :] = _softmax_out(acc_ref.at[h])
    o_ref[...] = ot_ref[...].T.astype(o_ref.dtype)


def _dsa(aq, ak, avt, iq, ik, iw):
    b, t, _ = aq.shape
    topk = min(TOPK_MAX, t // 4)
    qspec = pl.BlockSpec((None, TQ, BRANCH_W), lambda bb, i: (bb, i, 0))
    kspec, vspec = _kv_specs(t, BRANCH_W)
    pick = np.zeros((2 * SUBLANES, MXU_N), np.float32)
    for hh in range(IDX_HEADS):
        pick[hh, KV_LORA + hh] = 1.0
    pick = jnp.asarray(pick, BF16)
    tri = jnp.asarray(np.tril(np.ones((CK, CK), np.float32)), BF16)
    kern = functools.partial(_dsa_kernel, topk=topk, idx_scale=(IDX_HEADS * IDX_DIM) ** -0.5)
    return pl.pallas_call(
        kern,
        out_shape=jax.ShapeDtypeStruct((b, t, BRANCH_W), BF16),
        grid=(b, t // TQ),
        in_specs=[qspec, kspec, vspec, qspec, kspec, qspec,
                  pl.BlockSpec(pick.shape, lambda bb, i: (0, 0)), pl.BlockSpec(tri.shape, lambda bb, i: (0, 0))],
        out_specs=qspec,
        scratch_shapes=[
            pltpu.VMEM((t // CK, CK, TQ), I32),
            pltpu.VMEM((t // CK, CK, TQ), I16),
            pltpu.VMEM((t // CK, CK, TQ), I16),
            pltpu.VMEM((IDX_HEADS, LANES, TQ), BF16),
            pltpu.VMEM((N_HEADS, LANES, TQ), BF16),
            pltpu.VMEM((2 * SUBLANES, TQ), F32),
            pltpu.VMEM((1, TQ), I32),
        ] + _attn_scratch(N_HEADS),
        compiler_params=_cparams(2),
        name="dsa",
    )(aq, ak, avt, iq, ik, iw, pick, tri)


def _kbar_kernel(k_ref, o_ref):
    o_ref[...] = jnp.zeros(o_ref.shape, o_ref.dtype)
    nb = k_ref.shape[0] // MOBA_BLOCK
    for n in range(nb):
        blk = k_ref[n * MOBA_BLOCK:(n + 1) * MOBA_BLOCK, :].astype(F32)
        o_ref[n:n + 1, :] = jnp.mean(blk, axis=0, keepdims=True).astype(o_ref.dtype)


def _kbar(bk):
    b, t, w = bk.shape
    nbp = max(2 * SUBLANES, t // MOBA_BLOCK)
    return pl.pallas_call(
        _kbar_kernel,
        out_shape=jax.ShapeDtypeStruct((b, nbp, w), BF16),
        grid=(b,),
        in_specs=[pl.BlockSpec((None, t, w), lambda bb: (bb, 0, 0))],
        out_specs=pl.BlockSpec((None, nbp, w), lambda bb: (bb, 0, 0)),
        compiler_params=_cparams(1),
        name="moba_kbar",
    )(bk)


def _moba_kernel(q_ref, k_ref, vt_ref, kbar_ref, o_ref, qt_ref, bias_ref, s_ref, m_ref, acc_ref, ot_ref):
    i = pl.program_id(1)
    nbp = kbar_ref.shape[0]
    blk = lax.broadcasted_iota(I32, (nbp, TQ), 0)
    blk_f = blk.astype(F32)
    kpos = lax.broadcasted_iota(I32, (CK, TQ), 0)
    qpos = lax.broadcasted_iota(I32, (CK, TQ), 1)
    _masked_qt(q_ref[...].astype(F32) * (HEAD_DIM ** -0.5 * LOG2E), 6, N_HEADS, qt_ref)

    for h in range(N_HEADS):
        g = jnp.where(blk < i, jnp.dot(_half(kbar_ref[...], h, 6), qt_ref[h], preferred_element_type=F32), NEG)
        bias = jnp.full((nbp, TQ), NEG, F32)
        for _ in range(MOBA_TOPK):
            mx = jnp.max(g, axis=0, keepdims=True)
            first = jnp.min(jnp.where(g == mx, blk_f, 1e9), axis=0, keepdims=True)
            pick = jnp.logical_and(blk_f == first, mx > 0.5 * NEG)
            bias = jnp.where(pick, 0.0, bias)
            g = jnp.where(pick, NEG, g)
        bias_ref[h] = bias

    def qk_all(c):
        kc = k_ref[pl.ds(pl.multiple_of(c * CK, CK), CK), :]
        return [jnp.dot(_half(kc, h, 6), qt_ref[h], preferred_element_type=F32) for h in range(N_HEADS)]

    def mask(ctx, h, s):
        c, diag = ctx
        return jnp.where(kpos <= qpos, s, NEG) if diag else s + bias_ref[h, pl.ds(c, 1), :]

    _flash_loop(i, qk_all, mask, lambda c, h: vt_ref[c, h * VROWS:(h + 1) * VROWS, :],
                (s_ref, m_ref, acc_ref))
    for h in range(N_HEADS):
        ot_ref[h * HEAD_DIM:(h + 1) * HEAD_DIM, :] = _softmax_out(acc_ref.at[h])
    o_ref[...] = ot_ref[...].T.astype(o_ref.dtype)


def _moba(bq, bk, bvt, kbar):
    b, t, w = bq.shape
    assert TQ == MOBA_BLOCK and CK == MOBA_BLOCK and t % MOBA_BLOCK == 0
    nbp = kbar.shape[1]
    qspec = pl.BlockSpec((None, TQ, w), lambda bb, i: (bb, i, 0))
    kspec, vspec = _kv_specs(t, w)
    return pl.pallas_call(
        _moba_kernel,
        out_shape=jax.ShapeDtypeStruct((b, t, w), BF16),
        grid=(b, t // TQ),
        in_specs=[qspec, kspec, vspec, pl.BlockSpec((None, nbp, w), lambda bb, i: (bb, 0, 0))],
        out_specs=qspec,
        scratch_shapes=[pltpu.VMEM((N_HEADS, LANES, TQ), BF16), pltpu.VMEM((N_HEADS, nbp, TQ), F32)]
        + _attn_scratch(N_HEADS),
        compiler_params=_cparams(2),
        name="moba",
    )(bq, bk, bvt, kbar)


def _diff_kernel(q_ref, k_ref, vt_ref, lam_ref, norm_ref, misc_ref, o_ref,
                 qt_ref, s_ref, m_ref, acc_ref, ot_ref):
    i = pl.program_id(1)
    kpos = lax.broadcasted_iota(I32, (CK, TQ), 0)
    qpos = lax.broadcasted_iota(I32, (CK, TQ), 1)
    _masked_qt(q_ref[...].astype(F32) * (DIFF_DIM ** -0.5 * LOG2E), 5, 2 * N_HEADS, qt_ref)

    dl = lam_ref[...]
    lam_init = misc_ref[0:1, 0:1]
    lam = (jnp.exp(jnp.sum(dl[0:1, :] * dl[1:2, :], axis=1, keepdims=True))
           - jnp.exp(jnp.sum(dl[2:3, :] * dl[3:4, :], axis=1, keepdims=True)) + lam_init)

    def qk_all(c):
        kc = k_ref[pl.ds(pl.multiple_of(c * CK, CK), CK), :]
        return [jnp.dot(_half(kc, j, 5), qt_ref[j], preferred_element_type=F32) for j in range(2 * N_HEADS)]

    _flash_loop(i, qk_all,
                lambda ctx, j, s: jnp.where(kpos <= qpos, s, NEG) if ctx[1] else s,
                lambda c, j: vt_ref[c, (j // 2) * VROWS:(j // 2 + 1) * VROWS, :],
                (s_ref, m_ref, acc_ref))

    post = norm_ref[...] * (1.0 - lam_init)
    for h in range(N_HEADS):
        o_h = _softmax_out(acc_ref.at[2 * h]) - lam * _softmax_out(acc_ref.at[2 * h + 1])
        ms = jnp.mean(o_h * o_h, axis=0, keepdims=True)
        ot_ref[h * HEAD_DIM:(h + 1) * HEAD_DIM, :] = o_h * lax.rsqrt(ms + RMS_EPS) * post
    o_ref[...] = ot_ref[...].T.astype(o_ref.dtype)


def _diff(cq, ck, cvt, lam, norm, misc):
    b, t, w = cq.shape
    qspec = pl.BlockSpec((None, TQ, w), lambda bb, i: (bb, i, 0))
    kspec, vspec = _kv_specs(t, w)
    full = lambda a: pl.BlockSpec(a.shape, lambda bb, i: (0,) * a.ndim)
    return pl.pallas_call(
        _diff_kernel,
        out_shape=jax.ShapeDtypeStruct((b, t, w), BF16),
        grid=(b, t // TQ),
        in_specs=[qspec, kspec, vspec, full(lam), full(norm), full(misc)],
        out_specs=qspec,
        scratch_shapes=[pltpu.VMEM((2 * N_HEADS, LANES, TQ), BF16)] + _attn_scratch(2 * N_HEADS),
        compiler_params=_cparams(2),
        name="diff",
    )(cq, ck, cvt, lam, norm, misc)


def _mla_prep_kernel(cq_ref, ckv_ref, kr_ref, qn_ref, kvn_ref, wq_ref, wqr_ref, wk_ref, wvt_ref,
                     p_ref, ct_ref, st_ref, q_out, k_out, vt_out):
    x = cq_ref[...].astype(F32)
    xn = (x * lax.rsqrt(jnp.mean(x * x, axis=1, keepdims=True) + RMS_EPS) * qn_ref[...]).astype(BF16)
    q = (jnp.dot(xn, wq_ref[...], preferred_element_type=F32) * ct_ref[...]
         + jnp.dot(xn, wqr_ref[...], preferred_element_type=F32) * st_ref[...])
    q_out[...] = q.astype(q_out.dtype)
    c = ckv_ref[:, :KV_LORA].astype(F32)
    cn = (c * lax.rsqrt(jnp.mean(c * c, axis=1, keepdims=True) + RMS_EPS) * kvn_ref[...]).astype(BF16)
    k = (jnp.dot(cn, wk_ref[...], preferred_element_type=F32)
         + jnp.dot(kr_ref[...], p_ref[...], preferred_element_type=F32))
    k_out[...] = k.astype(k_out.dtype)
    _store_vt(vt_out, _tn_dot(wvt_ref[...], cn))


def _mla_prep(dcq, ckv, kr, qn, kvn, wq, wqr, wk, wvt, pmat, ct, st):
    b, t, _ = dcq.shape
    tm = 512
    hw = N_HEADS * LANES
    row = lambda w: pl.BlockSpec((None, tm, w), lambda i, bb: (bb, i, 0))
    full = lambda a: pl.BlockSpec(a.shape, lambda i, bb: (0,) * a.ndim)
    tab = pl.BlockSpec((tm, hw), lambda i, bb: (i, 0))
    return pl.pallas_call(
        _mla_prep_kernel,
        out_shape=(jax.ShapeDtypeStruct((b, t, hw), BF16), jax.ShapeDtypeStruct((b, t, hw), BF16),
                   jax.ShapeDtypeStruct((b, t // CK, N_HEADS * VROWS, CK), BF16)),
        grid=(t // tm, b),
        in_specs=[row(Q_LORA), row(MXU_N), row(MXU_N), full(qn), full(kvn), full(wq), full(wqr),
                  full(wk), full(wvt), full(pmat), tab, tab],
        out_specs=(row(hw), row(hw),
                   pl.BlockSpec((None, tm // CK, N_HEADS * VROWS, CK), lambda i, bb: (bb, i, 0, 0))),
        compiler_params=_cparams(2),
        name="mla_prep",
    )(dcq, ckv, kr, qn, kvn, wq, wqr, wk, wvt, pmat, ct, st)


def _mla_kernel(q_ref, k_ref, vt_ref, o_ref, qt_ref, s_ref, m_ref, acc_ref, ot_ref):
    i = pl.program_id(1)
    kpos = lax.broadcasted_iota(I32, (CK, TQ), 0)
    qpos = lax.broadcasted_iota(I32, (CK, TQ), 1)
    hs = [slice(h * LANES, (h + 1) * LANES) for h in range(N_HEADS)]
    for h in range(N_HEADS):
        qt_ref[h] = q_ref[:, hs[h]].astype(F32).T.astype(BF16)

    def qk_all(c):
        start = pl.multiple_of(c * CK, CK)
        return [jnp.dot(k_ref[pl.ds(start, CK), hs[h]], qt_ref[h], preferred_element_type=F32)
                for h in range(N_HEADS)]

    _flash_loop(i, qk_all,
                lambda ctx, h, s: jnp.where(kpos <= qpos, s, NEG) if ctx[1] else s,
                lambda c, h: vt_ref[c, h * VROWS:(h + 1) * VROWS, :],
                (s_ref, m_ref, acc_ref))
    for h in range(N_HEADS):
        ot_ref[h * HEAD_DIM:(h + 1) * HEAD_DIM, :] = _softmax_out(acc_ref.at[h])
    o_ref[...] = ot_ref[...].T.astype(o_ref.dtype)


def _mla(qm, km, vmt):
    b, t, hw = qm.shape
    kspec, vspec = _kv_specs(t, hw)
    return pl.pallas_call(
        _mla_kernel,
        out_shape=jax.ShapeDtypeStruct((b, t, BRANCH_W), BF16),
        grid=(b, t // TQ),
        in_specs=[pl.BlockSpec((None, TQ, hw), lambda bb, i: (bb, i, 0)), kspec, vspec],
        out_specs=pl.BlockSpec((None, TQ, BRANCH_W), lambda bb, i: (bb, i, 0)),
        scratch_shapes=[pltpu.VMEM((N_HEADS, LANES, TQ), BF16)] + _attn_scratch(N_HEADS),
        compiler_params=_cparams(2),
        name="mla",
    )(qm, km, vmt)


def _matmul_kernel(x_ref, w_ref, o_ref):
    o_ref[...] = jnp.dot(x_ref[...].astype(BF16), w_ref[...], preferred_element_type=F32).astype(o_ref.dtype)


def _mem_kv(mem, w):
    b, m, d = mem.shape
    n = w.shape[1]
    return pl.pallas_call(
        _matmul_kernel,
        out_shape=jax.ShapeDtypeStruct((b, m, n), BF16),
        grid=(b,),
        in_specs=[pl.BlockSpec((None, m, d), lambda bb: (bb, 0, 0)), pl.BlockSpec((d, n), lambda bb: (0, 0))],
        out_specs=pl.BlockSpec((None, m, n), lambda bb: (bb, 0, 0)),
        compiler_params=_cparams(1),
        name="mem_kv",
    )(mem, w)


def _mem_kernel(q_ref, kv_ref, o_ref):
    tq = q_ref.shape[0]
    lane_q = lax.broadcasted_iota(I32, (tq, BRANCH_W), 1)
    q = q_ref[...].astype(F32) * (HEAD_DIM ** -0.5)
    mk = kv_ref[:, :BRANCH_W]
    mv = kv_ref[:, BRANCH_W:]
    out = jnp.zeros((tq, BRANCH_W), F32)
    for h in range(N_HEADS):
        in_h = (lane_q >> 6) == h
        s = _nt_dot(jnp.where(in_h, q, 0.0).astype(BF16), mk)
        p = jnp.exp(s - jnp.max(s, axis=1, keepdims=True))
        o_h = jnp.dot(p.astype(BF16), mv, preferred_element_type=F32) / jnp.sum(p, axis=1, keepdims=True)
        out = jnp.where(in_h, o_h, out)
    o_ref[...] = out.astype(o_ref.dtype)


def _mem_attn(eq, mkv):
    b, t, w = eq.shape
    m = mkv.shape[1]
    tq = 512
    return pl.pallas_call(
        _mem_kernel,
        out_shape=jax.ShapeDtypeStruct((b, t, w), BF16),
        grid=(b, t // tq),
        in_specs=[pl.BlockSpec((None, tq, w), lambda bb, i: (bb, i, 0)),
                  pl.BlockSpec((None, m, 2 * w), lambda bb, i: (bb, 0, 0))],
        out_specs=pl.BlockSpec((None, tq, w), lambda bb, i: (bb, i, 0)),
        compiler_params=_cparams(2),
        name="mem_attn",
    )(eq, mkv)


def _final_kernel(h_ref, hb_ref, oa_ref, ob_ref, oc_ref, od_ref, oe_ref, z_ref,
                  wg_ref, wb_ref, wo_ref, g_ref, b_ref, h_out, hb_out, *, alpha):
    d = h_ref.shape[1]
    acc = jnp.zeros(h_ref.shape, F32)
    for n, o_ref in enumerate((oa_ref, ob_ref, oc_ref, od_ref, oe_ref)):
        z = z_ref[:, n * BRANCH_W:(n + 1) * BRANCH_W].astype(F32)
        y = o_ref[...].astype(F32) * (z / (1.0 + jnp.exp(-z)))
        u = jnp.dot(y.astype(BF16), wb_ref[n], preferred_element_type=F32)
        g = jnp.dot(hb_ref[...], wg_ref[:, n * d:(n + 1) * d], preferred_element_type=F32)
        acc = acc + u / (1.0 + jnp.exp(-g))
    out = jnp.dot(acc.astype(BF16), wo_ref[...], preferred_element_type=F32)
    x = alpha * h_ref[...] + out
    mu = jnp.mean(x, axis=1, keepdims=True)
    xc = x - mu
    var = jnp.mean(xc * xc, axis=1, keepdims=True)
    y = xc * lax.rsqrt(var + LN_EPS) * g_ref[...] + b_ref[...]
    h_out[...] = y
    hb_out[...] = y.astype(BF16)


def _final(h, hb, os5, z, wg, wb, wo, ln_g, ln_b, alpha):
    n, d = h.shape
    tm = 256
    row = lambda w: pl.BlockSpec((tm, w), lambda i: (i, 0))
    full = lambda a: pl.BlockSpec(a.shape, lambda i: (0,) * a.ndim)
    return pl.pallas_call(
        functools.partial(_final_kernel, alpha=alpha),
        out_shape=(jax.ShapeDtypeStruct((n, d), F32), jax.ShapeDtypeStruct((n, d), BF16)),
        grid=(n // tm,),
        in_specs=[row(d), row(d)] + [row(BRANCH_W)] * N_BRANCH + [row(N_BRANCH * BRANCH_W),
                  full(wg), full(wb), full(wo), full(ln_g), full(ln_b)],
        out_specs=(row(d), row(d)),
        compiler_params=_cparams(1),
        name="merge_out_ln",
    )(h, hb, *os5, z, wg, wb, wo, ln_g, ln_b)


def _rope_tables(seq, rot_dim):
    pos = jnp.arange(seq, dtype=F32)
    inv = ROPE_THETA ** (-jnp.arange(0, rot_dim, 2, dtype=F32) / rot_dim)
    ang = pos[:, None] * inv[None, :]
    return jnp.cos(ang), jnp.sin(ang)


def _rope_cs(t, nh, hd, r):
    cos, sin = _rope_tables(t, r)
    c = jnp.concatenate([cos, cos, jnp.ones((t, hd - r), F32)], axis=1)
    s = jnp.concatenate([-sin, sin, jnp.zeros((t, hd - r), F32)], axis=1)
    return jnp.tile(c, (1, nh)), jnp.tile(s, (1, nh))


def kernel(x, mem, ln0_g, ln0_b, w_in, mla_q_norm, w_uq, mla_kv_norm, w_ukv, diff_lam, diff_norm,
           w_mem_kv, w_branch, w_out, ln_g, ln_b):
    b, t, d = x.shape
    depth = w_in.shape[0]
    alpha = (2 * depth) ** 0.25
    assert t % 512 == 0 and d == 1024

    def seg(name):
        o, s = OFF[name]
        return w_in[:, :, o:o + s]

    zeros = lambda n: jnp.zeros((depth, d, n), w_in.dtype)

    w_plain = jnp.concatenate(
        [seg("d_cq"), seg("d_ckv"), seg("i_w"), zeros(MXU_N - KV_LORA - IDX_HEADS), seg("e_q"), seg("z")],
        axis=-1).astype(BF16)
    plain_widths = (BRANCH_W,) * 3 + (N_BRANCH * BRANCH_W,)
    w_vt = jnp.concatenate([seg("a_v"), seg("b_v"), seg("c_v")], axis=-1).astype(BF16)

    rope_groups = [
        (seg("a_q"), N_HEADS, HEAD_DIM, ROT_64), (seg("a_k"), N_HEADS, HEAD_DIM, ROT_64),
        (seg("i_q"), IDX_HEADS, IDX_DIM, ROT_32), (jnp.tile(seg("i_k"), (1, 1, IDX_HEADS)), IDX_HEADS, IDX_DIM, ROT_32),
        (seg("b_q"), N_HEADS, HEAD_DIM, ROT_64), (seg("b_k"), N_HEADS, HEAD_DIM, ROT_64),
        (seg("c_q"), 2 * N_HEADS, DIFF_DIM, ROT_32), (seg("c_k"), 2 * N_HEADS, DIFF_DIM, ROT_32),
        (jnp.concatenate([seg("d_kr"), zeros(MXU_N - MLA_ROPE)], axis=-1), 1, MXU_N, MLA_ROPE),
    ]
    w_rope = jnp.concatenate([g for g, *_ in rope_groups], axis=-1).astype(BF16)
    rope_heads = tuple((hd, r // 2) for _, _, hd, r in rope_groups)
    cs = [_rope_cs(t, nh, hd, r) for _, nh, hd, r in rope_groups]
    ctab = jnp.concatenate([c for c, _ in cs], axis=1)
    stab = jnp.concatenate([s for _, s in cs], axis=1)

    uq = w_uq.reshape(depth, Q_LORA, N_HEADS, MLA_NOPE + MLA_ROPE)
    qn_w, qr_w = uq[..., :MLA_NOPE], uq[..., MLA_NOPE:]
    pad32 = jnp.zeros((depth, Q_LORA, N_HEADS, LANES - MLA_NOPE - MLA_ROPE), w_uq.dtype)
    hw = N_HEADS * LANES
    wq = jnp.concatenate([qn_w, qr_w, pad32], axis=-1).reshape(depth, Q_LORA, hw).astype(BF16)
    half = MLA_ROPE // 2
    wq_rot = jnp.concatenate([jnp.zeros_like(qn_w), -qr_w[..., half:], qr_w[..., :half], pad32],
                             axis=-1).reshape(depth, Q_LORA, hw).astype(BF16)
    cos_m, sin_m = _rope_tables(t, MLA_ROPE)
    one = lambda n: jnp.ones((t, n), F32)
    zer = lambda n: jnp.zeros((t, n), F32)
    qs = (MLA_NOPE + MLA_ROPE) ** -0.5 * LOG2E
    ct_q = qs * jnp.tile(jnp.concatenate([one(MLA_NOPE), cos_m, cos_m, one(LANES - MLA_NOPE - MLA_ROPE)], axis=1), (1, N_HEADS))
    st_q = qs * jnp.tile(jnp.concatenate([zer(MLA_NOPE), sin_m, sin_m, zer(LANES - MLA_NOPE - MLA_ROPE)], axis=1), (1, N_HEADS))
    ukv = w_ukv.reshape(depth, KV_LORA, N_HEADS, MLA_NOPE + MLA_V)
    wk = jnp.concatenate([ukv[..., :MLA_NOPE], jnp.zeros((depth, KV_LORA, N_HEADS, LANES - MLA_NOPE), w_ukv.dtype)],
                         axis=-1).reshape(depth, KV_LORA, hw).astype(BF16)
    wvt = ukv[..., MLA_NOPE:].reshape(depth, KV_LORA, N_HEADS * MLA_V).astype(BF16)
    place = np.zeros((MXU_N, hw), np.float32)
    for hh in range(N_HEADS):
        for j in range(MLA_ROPE):
            place[j, hh * LANES + MLA_NOPE + j] = 1.0
    place = jnp.asarray(place, BF16)

    wg = seg("g").astype(BF16)
    wb = w_branch.astype(BF16)
    wo = w_out.astype(BF16)
    wmem = w_mem_kv.astype(BF16)
    norm_t = jnp.broadcast_to(diff_norm.astype(F32)[:, :, None], (depth, HEAD_DIM, TQ))

    h, hb = _layer_norm0(x.reshape(b * t, d), ln0_g, ln0_b)
    for l in range(depth):
        hb3 = hb.reshape(b, t, d)
        avt, bvt, cvt, dcq, ckv_iw, eq, z = _proj_plain(hb3, w_plain[l], w_vt[l], plain_widths)
        aq, ak, iq, ik, bq, bk, cq, ck, kr = _proj_rope(hb3, w_rope[l], ctab, stab, rope_heads)

        o_a = _dsa(aq, ak, avt, iq, ik, ckv_iw)
        o_b = _moba(bq, bk, bvt, _kbar(bk))
        lam_init = 0.8 - 0.6 * math.exp(-0.3 * l)
        misc = jnp.full((SUBLANES, LANES), lam_init, F32)
        o_c = _diff(cq, ck, cvt, diff_lam[l].astype(F32), norm_t[l], misc)
        qm, km, vmt = _mla_prep(dcq, ckv_iw, kr, mla_q_norm[l].reshape(1, Q_LORA), mla_kv_norm[l].reshape(1, KV_LORA),
                                wq[l], wq_rot[l], wk[l], wvt[l], place, ct_q, st_q)
        o_d = _mla(qm, km, vmt)
        o_e = _mem_attn(eq, _mem_kv(mem, wmem[l]))

        os5 = [o.reshape(b * t, BRANCH_W) for o in (o_a, o_b, o_c, o_d, o_e)]
        h, hb = _final(h, hb, os5, z.reshape(b * t, N_BRANCH * BRANCH_W), wg[l], wb[l], wo[l],
                       ln_g[l].reshape(1, d), ln_b[l].reshape(1, d), alpha)
    return h.reshape(b, t, d)
```

```python
import functools
import math

import numpy as np
import jax
import jax.numpy as jnp
from jax import lax
from jax.experimental import pallas as pl
from jax.experimental.pallas import tpu as pltpu

F32 = jnp.float32
BF16 = jnp.bfloat16
I32 = jnp.int32
I16 = jnp.int16

N_HEADS = 4
HEAD_DIM = 64
BRANCH_W = N_HEADS * HEAD_DIM
N_BRANCH = 5
ROPE_THETA = 500000.0
ROT_64 = 16
ROT_32 = 8
IDX_HEADS = 8
IDX_DIM = 32
TOPK_MAX = 256
MOBA_BLOCK = 256
MOBA_TOPK = 3
DIFF_DIM = 32
Q_LORA = 256
KV_LORA = 128
MLA_NOPE = 64
MLA_ROPE = 32
MLA_V = 64
LN_EPS = 1e-5
RMS_EPS = 1e-6

IN_LAYOUT = (
    ("a_q", BRANCH_W), ("a_k", BRANCH_W), ("a_v", BRANCH_W),
    ("i_q", IDX_HEADS * IDX_DIM), ("i_k", IDX_DIM), ("i_w", IDX_HEADS),
    ("b_q", BRANCH_W), ("b_k", BRANCH_W), ("b_v", BRANCH_W),
    ("c_q", BRANCH_W), ("c_k", BRANCH_W), ("c_v", BRANCH_W),
    ("d_cq", Q_LORA), ("d_ckv", KV_LORA), ("d_kr", MLA_ROPE),
    ("e_q", BRANCH_W),
    ("z", N_BRANCH * BRANCH_W),
    ("g", N_BRANCH * 1024),
)

SUBLANES = 8
LANES = 128
MXU_N = 256
TQ = 256
CK = 256
VROWS = HEAD_DIM + 16
FLASH_UNROLL = 4
NEG = -1e30
LOG2E = math.log2(math.e)
INT_MIN = np.int32(-2 ** 31)
HALF16 = 1 << 15
VMEM_LIMIT = 56 * 1024 * 1024


def _offsets():
    off, out = 0, {}
    for name, size in IN_LAYOUT:
        out[name] = (off, size)
        off += size
    return out


OFF = _offsets()


def _nt_dot(a, b):
    return lax.dot_general(a, b, (((1,), (1,)), ((), ())), preferred_element_type=F32)


def _tn_dot(w, x):
    return lax.dot_general(w, x, (((0,), (1,)), ((), ())), preferred_element_type=F32)


def _fold_rows(w, rows=SUBLANES):
    xs = [w[r:r + rows, :] for r in range(0, w.shape[0], rows)]
    while len(xs) > 1:
        xs = [xs[j] + xs[j + 1] for j in range(0, len(xs) - 1, 2)] + ([xs[-1]] if len(xs) % 2 else [])
    return xs[0]


def _masked_qt(q, shift, n, qt_ref):
    qt = q.T
    dim = lax.broadcasted_iota(I32, (LANES, qt.shape[1]), 0)
    for j in range(n):
        half = (j << shift) // LANES
        rows = qt[half * LANES:(half + 1) * LANES, :]
        qt_ref[j] = jnp.where(((dim + half * LANES) >> shift) == j, rows, 0.0).astype(BF16)


def _half(kc, j, shift):
    half = (j << shift) // LANES
    return kc[:, half * LANES:(half + 1) * LANES]


def _cparams(n_axes):
    return pltpu.CompilerParams(dimension_semantics=("arbitrary",) * n_axes,
                                vmem_limit_bytes=VMEM_LIMIT)


def _softmax_step(s_t, vt_h, m_ref, acc_ref):
    m_old = m_ref[...]
    m_new = jnp.maximum(m_old, jnp.max(s_t, axis=0, keepdims=True))
    alpha = jnp.exp2(m_old - m_new)
    p = jnp.exp2(s_t - m_new)
    acc_ref[...] = alpha * acc_ref[...] + jnp.dot(vt_h, p.astype(BF16), preferred_element_type=F32)
    m_ref[...] = m_new


def _softmax_init(m_ref, acc_ref):
    m_ref[...] = jnp.full(m_ref.shape, NEG, F32)
    acc_ref[...] = jnp.zeros(acc_ref.shape, F32)


def _softmax_out(acc_ref):
    return acc_ref[:HEAD_DIM, :] / acc_ref[HEAD_DIM:HEAD_DIM + 1, :]


def _store_vt(o_ref, vt):
    ones = jnp.ones((VROWS - HEAD_DIM, CK), o_ref.dtype)
    for j in range(o_ref.shape[0]):
        for h in range(N_HEADS):
            o_ref[j, h * VROWS:h * VROWS + HEAD_DIM, :] = (
                vt[h * HEAD_DIM:(h + 1) * HEAD_DIM, j * CK:(j + 1) * CK].astype(o_ref.dtype))
            o_ref[j, h * VROWS + HEAD_DIM:(h + 1) * VROWS, :] = ones


def _flash_loop(n_prev, qk_all, mask, vt_rows, state, prep=None):
    s_ref, m_ref, acc_ref = state
    n_state = m_ref.shape[0]
    for j in range(n_state):
        _softmax_init(m_ref.at[j], acc_ref.at[j])

    def park(c, slot):
        for j, s in enumerate(qk_all(c)):
            s_ref[slot, j] = s

    def consume(c, slot, diag):
        ctx = (c, diag) if prep is None else prep(c, diag)
        for j in range(n_state):
            _softmax_step(mask(ctx, j, s_ref[slot, j]), vt_rows(c, j),
                          m_ref.at[j], acc_ref.at[j])

    park(0, 0)

    def pair(c):
        park(c + 1, 1)
        consume(c, 0, False)
        park(c + 2, 0)
        consume(c + 1, 1, False)

    def body(g, carry):
        for u in range(0, FLASH_UNROLL, 2):
            pair(FLASH_UNROLL * g + u)
        return carry

    n_group = lax.shift_right_logical(n_prev, FLASH_UNROLL.bit_length() - 1)
    lax.fori_loop(0, n_group, body, 0)
    c0 = FLASH_UNROLL * n_group
    for u in range(FLASH_UNROLL // 2 - 1):
        @pl.when(n_prev - c0 >= 2 * (u + 1))
        def _(u=u):
            pair(c0 + 2 * u)
    c0 = c0 + 2 * lax.shift_right_logical(n_prev - c0, 1)
    odd = (n_prev & 1) == 1

    @pl.when(odd)
    def _():
        park(c0 + 1, 1)
        consume(c0, 0, False)
        consume(c0 + 1, 1, True)

    @pl.when(jnp.logical_not(odd))
    def _():
        consume(c0, 0, True)


def _attn_scratch(n_state):
    return [pltpu.VMEM((2, n_state, CK, TQ), F32), pltpu.VMEM((n_state, 1, TQ), F32),
            pltpu.VMEM((n_state, VROWS, TQ), F32), pltpu.VMEM((BRANCH_W, TQ), F32)]


def _kv_specs(t, w):
    kspec = pl.BlockSpec((None, t, w), lambda bb, i: (bb, 0, 0))
    vspec = pl.BlockSpec((None, t // CK, N_HEADS * VROWS, CK), lambda bb, i: (bb, 0, 0, 0))
    return kspec, vspec


def _ln_kernel(x_ref, g_ref, b_ref, h_ref, hb_ref):
    x = x_ref[...]
    mu = jnp.mean(x, axis=1, keepdims=True)
    xc = x - mu
    var = jnp.mean(xc * xc, axis=1, keepdims=True)
    y = xc * lax.rsqrt(var + LN_EPS) * g_ref[...] + b_ref[...]
    h_ref[...] = y
    hb_ref[...] = y.astype(BF16)


def _layer_norm0(x2, g, b):
    n, d = x2.shape
    tm = 512
    row = pl.BlockSpec((tm, d), lambda i: (i, 0))
    vec = pl.BlockSpec((1, d), lambda i: (0, 0))
    return pl.pallas_call(
        _ln_kernel,
        out_shape=(jax.ShapeDtypeStruct((n, d), F32), jax.ShapeDtypeStruct((n, d), BF16)),
        grid=(n // tm,),
        in_specs=[row, vec, vec],
        out_specs=(row, row),
        compiler_params=_cparams(1),
        name="ln0",
    )(x2, g.reshape(1, d), b.reshape(1, d))


def _proj_plain_kernel(x_ref, w_ref, wt_ref, *out_refs, n_t):
    for g, o_ref in enumerate(out_refs[:n_t]):
        _store_vt(o_ref, _tn_dot(wt_ref[:, g * BRANCH_W:(g + 1) * BRANCH_W], x_ref[...]))
    off = 0
    for o_ref in out_refs[n_t:]:
        wd = o_ref.shape[-1]
        for j in range(0, wd, MXU_N):
            acc = jnp.dot(x_ref[...], w_ref[:, off + j:off + j + MXU_N], preferred_element_type=F32)
            o_ref[:, j:j + MXU_N] = acc.astype(o_ref.dtype)
        off += wd


def _proj_plain(hb3, w, wt, widths):
    b, t, d = hb3.shape
    tm = 512
    n_t = wt.shape[1] // BRANCH_W
    shapes = [jax.ShapeDtypeStruct((b, t // CK, N_HEADS * VROWS, CK), BF16)] * n_t
    specs = [pl.BlockSpec((None, tm // CK, N_HEADS * VROWS, CK), lambda i, bb: (bb, i, 0, 0))] * n_t
    shapes += [jax.ShapeDtypeStruct((b, t, wd), BF16) for wd in widths]
    specs += [pl.BlockSpec((None, tm, wd), lambda i, bb: (bb, i, 0)) for wd in widths]
    return pl.pallas_call(
        functools.partial(_proj_plain_kernel, n_t=n_t),
        out_shape=tuple(shapes),
        grid=(t // tm, b),
        in_specs=[pl.BlockSpec((None, tm, d), lambda i, bb: (bb, i, 0)),
                  pl.BlockSpec(w.shape, lambda i, bb: (0, 0)),
                  pl.BlockSpec(wt.shape, lambda i, bb: (0, 0))],
        out_specs=tuple(specs),
        compiler_params=_cparams(2),
        name="proj_plain",
    )(hb3, w, wt)


def _proj_rope_kernel(x_ref, w_ref, c_ref, s_ref, *out_refs, heads):
    lane = lax.broadcasted_iota(I32, (x_ref.shape[0], MXU_N), 1)
    for g, o_ref in enumerate(out_refs):
        hd, half = heads[g]
        sl = slice(g * MXU_N, (g + 1) * MXU_N)
        acc = jnp.dot(x_ref[...], w_ref[:, sl], preferred_element_type=F32)
        partner = jnp.where((lane & (hd - 1)) < half,
                            pltpu.roll(acc, MXU_N - half, 1), pltpu.roll(acc, half, 1))
        o_ref[...] = (acc * c_ref[:, sl] + partner * s_ref[:, sl]).astype(o_ref.dtype)


def _proj_rope(hb3, w, ctab, stab, heads):
    b, t, d = hb3.shape
    tm = 512
    ncol = w.shape[1]
    assert ncol == MXU_N * len(heads)
    tspec = pl.BlockSpec((tm, ncol), lambda i, bb: (i, 0))
    ospec = pl.BlockSpec((None, tm, MXU_N), lambda i, bb: (bb, i, 0))
    return pl.pallas_call(
        functools.partial(_proj_rope_kernel, heads=heads),
        out_shape=(jax.ShapeDtypeStruct((b, t, MXU_N), BF16),) * len(heads),
        grid=(t // tm, b),
        in_specs=[pl.BlockSpec((None, tm, d), lambda i, bb: (bb, i, 0)),
                  pl.BlockSpec((d, ncol), lambda i, bb: (0, 0)), tspec, tspec],
        out_specs=(ospec,) * len(heads),
        compiler_params=_cparams(2),
        name="proj_rope",
    )(hb3, w, ctab, stab)


def _dsa_kernel(aq_ref, ak_ref, avt_ref, iq_ref, ik_ref, iw_ref, pick_ref, tri_ref, o_ref,
                keys_ref, hi_ref, lo_ref, iqt_ref, aqt_ref, wt_ref, thr_ref, s_ref, m_ref, acc_ref, ot_ref,
                *, topk, idx_scale):
    i = pl.program_id(1)
    nk = i + 1
    kpos = lax.broadcasted_iota(I32, (CK, TQ), 0)
    qpos = lax.broadcasted_iota(I32, (CK, TQ), 1)

    _masked_qt(iq_ref[...].astype(F32), 5, IDX_HEADS, iqt_ref)
    _masked_qt(aq_ref[...].astype(F32) * (HEAD_DIM ** -0.5 * LOG2E), 6, N_HEADS, aqt_ref)
    wt_ref[...] = _nt_dot(pick_ref[...], iw_ref[...]) * idx_scale

    def logits(c):
        kc = ik_ref[pl.ds(pl.multiple_of(c * CK, CK), CK), :]
        return [jnp.dot(_half(kc, hh, 5), iqt_ref[hh], preferred_element_type=F32) for hh in range(IDX_HEADS)]

    def put_keys(c, key):
        keys_ref[c] = key
        hi_ref[c] = (key >> 16).astype(I16)
        lo_ref[c] = ((key & 0xFFFF) - HALF16).astype(I16)

    def score_chunk(c, lg, diag):
        sc = jnp.zeros((CK, TQ), F32)
        for hh in range(IDX_HEADS):
            sc = sc + jnp.maximum(lg[hh], 0.0) * wt_ref[hh:hh + 1, :]
        bits = pltpu.bitcast(sc, I32)
        key = jnp.where(bits < 0, INT_MIN - bits, bits)
        put_keys(c, jnp.where(kpos <= qpos, key, INT_MIN) if diag else key)

    def score_pair(p, carry):
        lg0, lg1 = logits(2 * p), logits(2 * p + 1)
        score_chunk(2 * p, lg0, False)
        score_chunk(2 * p + 1, lg1, False)
        return carry

    lax.fori_loop(0, lax.shift_right_logical(i, 1), score_pair, 0)

    @pl.when((i & 1) == 1)
    def _():
        score_chunk(i - 1, logits(i - 1), False)

    score_chunk(i, logits(i), True)

    @pl.when(jnp.logical_and((nk & 1) == 1, nk < keys_ref.shape[0]))
    def _():
        put_keys(nk, jnp.full((CK, TQ), INT_MIN, I32))

    def pair_loop(body, init):
        def pair(p, carry):
            return body(2 * p + 1, body(2 * p, carry))
        return lax.fori_loop(0, lax.shift_right_logical(nk + 1, 1), pair, init)

    def count16(ref, pred):
        def body(c, part):
            return part + _fold_rows(jnp.where(pred(ref[c]), jnp.int16(1), jnp.int16(0)), 2 * SUBLANES)
        part = pair_loop(body, jnp.zeros((2 * SUBLANES, TQ), I16))
        return jnp.sum(part.astype(F32), axis=0, keepdims=True)

    def search16(ref, need):
        def bit_body(bi, t_u):
            c_u = t_u | jnp.left_shift(jnp.int32(1), 15 - bi)
            ck = (c_u - HALF16).astype(I16)
            cnt = count16(ref, lambda v: v >= ck)
            return jnp.where(cnt >= need, c_u, t_u)
        return lax.fori_loop(0, 16, bit_body, jnp.zeros((1, TQ), I32))

    hi_u = search16(hi_ref, float(topk))
    thr_hi = (hi_u - HALF16).astype(I16)
    n_above = count16(hi_ref, lambda v: v > thr_hi)

    def bucket_body(c, carry):
        hi_ref[c] = jnp.where(hi_ref[c] == thr_hi, lo_ref[c], jnp.int16(-HALF16))
        return carry

    pair_loop(bucket_body, 0)
    lo_u = search16(hi_ref, float(topk) - n_above)
    thr = ((hi_u - HALF16) << 16) | lo_u

    def tie_body(c, carry):
        k = keys_ref[c]
        return (carry[0] + _fold_rows(jnp.where(k > thr, 1.0, 0.0)),
                carry[1] + _fold_rows(jnp.where(k == thr, 1.0, 0.0)))

    zero8 = jnp.zeros((SUBLANES, TQ), F32)
    gt8, eq8 = pair_loop(tie_body, (zero8, zero8))
    need = float(topk) - jnp.sum(gt8, axis=0, keepdims=True)
    amb = jnp.logical_and(jnp.sum(eq8, axis=0, keepdims=True) > need, thr > INT_MIN)
    any_amb = jnp.max(jnp.where(amb, 1.0, 0.0)) > 0.5

    @pl.when(any_amb)
    def _():
        def drop_body(c, seen):
            k = keys_ref[c]
            eq = k == thr
            eqf = jnp.where(eq, 1.0, 0.0)
            rank = jnp.dot(tri_ref[...], eqf.astype(BF16), preferred_element_type=F32) + seen
            drop = jnp.logical_and(jnp.logical_and(eq, rank > need), amb)
            keys_ref[c] = jnp.where(drop, INT_MIN, k)
            return seen + jnp.sum(eqf, axis=0, keepdims=True)

        lax.fori_loop(0, nk, drop_body, jnp.zeros((1, TQ), F32))

    thr_ref[...] = jnp.maximum(thr, INT_MIN + 1)

    def qk_all(c):
        kc = ak_ref[pl.ds(pl.multiple_of(c * CK, CK), CK), :]
        return [jnp.dot(_half(kc, h, 6), aqt_ref[h], preferred_element_type=F32) for h in range(N_HEADS)]

    _flash_loop(i, qk_all,
                lambda keep, h, s: jnp.where(keep, s, NEG),
                lambda c, h: avt_ref[c, h * VROWS:(h + 1) * VROWS, :],
                (s_ref, m_ref, acc_ref),
                prep=lambda c, diag: keys_ref[c] >= thr_ref[...])
    for h in range(N_HEADS):
        ot_ref[h * HEAD_DIM:(h + 1) * HEAD_DIM, :] = _softmax_out(acc_ref.at[h])
    o_ref[...] = ot_ref[...].T.astype(o_ref.dtype)


def _dsa(aq, ak, avt, iq, ik, iw):
    b, t, _ = aq.shape
    topk = min(TOPK_MAX, t // 4)
    qspec = pl.BlockSpec((None, TQ, BRANCH_W), lambda bb, i: (bb, i, 0))
    kspec, vspec = _kv_specs(t, BRANCH_W)
    pick = np.zeros((2 * SUBLANES, MXU_N), np.float32)
    for hh in range(IDX_HEADS):
        pick[hh, KV_LORA + hh] = 1.0
    pick = jnp.asarray(pick, BF16)
    tri = jnp.asarray(np.tril(np.ones((CK, CK), np.float32)), BF16)
    kern = functools.partial(_dsa_kernel, topk=topk, idx_scale=(IDX_HEADS * IDX_DIM) ** -0.5)
    return pl.pallas_call(
        kern,
        out_shape=jax.ShapeDtypeStruct((b, t, BRANCH_W), BF16),
        grid=(b, t // TQ),
        in_specs=[qspec, kspec, vspec, qspec, kspec, qspec,
                  pl.BlockSpec(pick.shape, lambda bb, i: (0, 0)), pl.BlockSpec(tri.shape, lambda bb, i: (0, 0))],
        out_specs=qspec,
        scratch_shapes=[
            pltpu.VMEM((t // CK, CK, TQ), I32),
            pltpu.VMEM((t // CK, CK, TQ), I16),
            pltpu.VMEM((t // CK, CK, TQ), I16),
            pltpu.VMEM((IDX_HEADS, LANES, TQ), BF16),
            pltpu.VMEM((N_HEADS, LANES, TQ), BF16),
            pltpu.VMEM((2 * SUBLANES, TQ), F32),
            pltpu.VMEM((1, TQ), I32),
        ] + _attn_scratch(N_HEADS),
        compiler_params=_cparams(2),
        name="dsa",
    )(aq, ak, avt, iq, ik, iw, pick, tri)


def _kbar_kernel(k_ref, o_ref):
    o_ref[...] = jnp.zeros(o_ref.shape, o_ref.dtype)
    nb = k_ref.shape[0] // MOBA_BLOCK
    for n in range(nb):
        blk = k_ref[n * MOBA_BLOCK:(n + 1) * MOBA_BLOCK, :].astype(F32)
        o_ref[n:n + 1, :] = jnp.mean(blk, axis=0, keepdims=True).astype(o_ref.dtype)


def _kbar(bk):
    b, t, w = bk.shape
    nbp = max(2 * SUBLANES, t // MOBA_BLOCK)
    return pl.pallas_call(
        _kbar_kernel,
        out_shape=jax.ShapeDtypeStruct((b, nbp, w), BF16),
        grid=(b,),
        in_specs=[pl.BlockSpec((None, t, w), lambda bb: (bb, 0, 0))],
        out_specs=pl.BlockSpec((None, nbp, w), lambda bb: (bb, 0, 0)),
        compiler_params=_cparams(1),
        name="moba_kbar",
    )(bk)


def _moba_kernel(q_ref, k_ref, vt_ref, kbar_ref, o_ref, qt_ref, bias_ref, s_ref, m_ref, acc_ref, ot_ref):
    i = pl.program_id(1)
    nbp = kbar_ref.shape[0]
    blk = lax.broadcasted_iota(I32, (nbp, TQ), 0)
    blk_f = blk.astype(F32)
    kpos = lax.broadcasted_iota(I32, (CK, TQ), 0)
    qpos = lax.broadcasted_iota(I32, (CK, TQ), 1)
    _masked_qt(q_ref[...].astype(F32) * (HEAD_DIM ** -0.5 * LOG2E), 6, N_HEADS, qt_ref)

    for h in range(N_HEADS):
        g = jnp.where(blk < i, jnp.dot(_half(kbar_ref[...], h, 6), qt_ref[h], preferred_element_type=F32), NEG)
        bias = jnp.full((nbp, TQ), NEG, F32)
        for _ in range(MOBA_TOPK):
            mx = jnp.max(g, axis=0, keepdims=True)
            first = jnp.min(jnp.where(g == mx, blk_f, 1e9), axis=0, keepdims=True)
            pick = jnp.logical_and(blk_f == first, mx > 0.5 * NEG)
            bias = jnp.where(pick, 0.0, bias)
            g = jnp.where(pick, NEG, g)
        bias_ref[h] = bias

    def qk_all(c):
        kc = k_ref[pl.ds(pl.multiple_of(c * CK, CK), CK), :]
        return [jnp.dot(_half(kc, h, 6), qt_ref[h], preferred_element_type=F32) for h in range(N_HEADS)]

    def mask(ctx, h, s):
        c, diag = ctx
        return jnp.where(kpos <= qpos, s, NEG) if diag else s + bias_ref[h, pl.ds(c, 1), :]

    _flash_loop(i, qk_all, mask, lambda c, h: vt_ref[c, h * VROWS:(h + 1) * VROWS, :],
                (s_ref, m_ref, acc_ref))
    for h in range(N_HEADS):
        ot_ref[h * HEAD_DIM:(h + 1) * HEAD_DIM, :] = _softmax_out(acc_ref.at[h])
    o_ref[...] = ot_ref[...].T.astype(o_ref.dtype)


def _moba(bq, bk, bvt, kbar):
    b, t, w = bq.shape
    assert TQ == MOBA_BLOCK and CK == MOBA_BLOCK and t % MOBA_BLOCK == 0
    nbp = kbar.shape[1]
    qspec = pl.BlockSpec((None, TQ, w), lambda bb, i: (bb, i, 0))
    kspec, vspec = _kv_specs(t, w)
    return pl.pallas_call(
        _moba_kernel,
        out_shape=jax.ShapeDtypeStruct((b, t, w), BF16),
        grid=(b, t // TQ),
        in_specs=[qspec, kspec, vspec, pl.BlockSpec((None, nbp, w), lambda bb, i: (bb, 0, 0))],
        out_specs=qspec,
        scratch_shapes=[pltpu.VMEM((N_HEADS, LANES, TQ), BF16), pltpu.VMEM((N_HEADS, nbp, TQ), F32)]
        + _attn_scratch(N_HEADS),
        compiler_params=_cparams(2),
        name="moba",
    )(bq, bk, bvt, kbar)


def _diff_kernel(q_ref, k_ref, vt_ref, lam_ref, norm_ref, misc_ref, o_ref,
                 qt_ref, s_ref, m_ref, acc_ref, ot_ref):
    i = pl.program_id(1)
    kpos = lax.broadcasted_iota(I32, (CK, TQ), 0)
    qpos = lax.broadcasted_iota(I32, (CK, TQ), 1)
    _masked_qt(q_ref[...].astype(F32) * (DIFF_DIM ** -0.5 * LOG2E), 5, 2 * N_HEADS, qt_ref)

    dl = lam_ref[...]
    lam_init = misc_ref[0:1, 0:1]
    lam = (jnp.exp(jnp.sum(dl[0:1, :] * dl[1:2, :], axis=1, keepdims=True))
           - jnp.exp(jnp.sum(dl[2:3, :] * dl[3:4, :], axis=1, keepdims=True)) + lam_init)

    def qk_all(c):
        kc = k_ref[pl.ds(pl.multiple_of(c * CK, CK), CK), :]
        return [jnp.dot(_half(kc, j, 5), qt_ref[j], preferred_element_type=F32) for j in range(2 * N_HEADS)]

    _flash_loop(i, qk_all,
                lambda ctx, j, s: jnp.where(kpos <= qpos, s, NEG) if ctx[1] else s,
                lambda c, j: vt_ref[c, (j // 2) * VROWS:(j // 2 + 1) * VROWS, :],
                (s_ref, m_ref, acc_ref))

    post = norm_ref[...] * (1.0 - lam_init)
    for h in range(N_HEADS):
        o_h = _softmax_out(acc_ref.at[2 * h]) - lam * _softmax_out(acc_ref.at[2 * h + 1])
        ms = jnp.mean(o_h * o_h, axis=0, keepdims=True)
        ot_ref[h * HEAD_DIM:(h + 1) * HEAD_DIM, :] = o_h * lax.rsqrt(ms + RMS_EPS) * post
    o_ref[...] = ot_ref[...].T.astype(o_ref.dtype)


def _diff(cq, ck, cvt, lam, norm, misc):
    b, t, w = cq.shape
    qspec = pl.BlockSpec((None, TQ, w), lambda bb, i: (bb, i, 0))
    kspec, vspec = _kv_specs(t, w)
    full = lambda a: pl.BlockSpec(a.shape, lambda bb, i: (0,) * a.ndim)
    return pl.pallas_call(
        _diff_kernel,
        out_shape=jax.ShapeDtypeStruct((b, t, w), BF16),
        grid=(b, t // TQ),
        in_specs=[qspec, kspec, vspec, full(lam), full(norm), full(misc)],
        out_specs=qspec,
        scratch_shapes=[pltpu.VMEM((2 * N_HEADS, LANES, TQ), BF16)] + _attn_scratch(2 * N_HEADS),
        compiler_params=_cparams(2),
        name="diff",
    )(cq, ck, cvt, lam, norm, misc)


def _mla_prep_kernel(cq_ref, ckv_ref, kr_ref, qn_ref, kvn_ref, wq_ref, wqr_ref, wk_ref, wvt_ref,
                     p_ref, ct_ref, st_ref, q_out, k_out, vt_out):
    x = cq_ref[...].astype(F32)
    xn = (x * lax.rsqrt(jnp.mean(x * x, axis=1, keepdims=True) + RMS_EPS) * qn_ref[...]).astype(BF16)
    q = (jnp.dot(xn, wq_ref[...], preferred_element_type=F32) * ct_ref[...]
         + jnp.dot(xn, wqr_ref[...], preferred_element_type=F32) * st_ref[...])
    q_out[...] = q.astype(q_out.dtype)
    c = ckv_ref[:, :KV_LORA].astype(F32)
    cn = (c * lax.rsqrt(jnp.mean(c * c, axis=1, keepdims=True) + RMS_EPS) * kvn_ref[...]).astype(BF16)
    k = (jnp.dot(cn, wk_ref[...], preferred_element_type=F32)
         + jnp.dot(kr_ref[...], p_ref[...], preferred_element_type=F32))
    k_out[...] = k.astype(k_out.dtype)
    _store_vt(vt_out, _tn_dot(wvt_ref[...], cn))


def _mla_prep(dcq, ckv, kr, qn, kvn, wq, wqr, wk, wvt, pmat, ct, st):
    b, t, _ = dcq.shape
    tm = 512
    hw = N_HEADS * LANES
    row = lambda w: pl.BlockSpec((None, tm, w), lambda i, bb: (bb, i, 0))
    full = lambda a: pl.BlockSpec(a.shape, lambda i, bb: (0,) * a.ndim)
    tab = pl.BlockSpec((tm, hw), lambda i, bb: (i, 0))
    return pl.pallas_call(
        _mla_prep_kernel,
        out_shape=(jax.ShapeDtypeStruct((b, t, hw), BF16), jax.ShapeDtypeStruct((b, t, hw), BF16),
                   jax.ShapeDtypeStruct((b, t // CK, N_HEADS * VROWS, CK), BF16)),
        grid=(t // tm, b),
        in_specs=[row(Q_LORA), row(MXU_N), row(MXU_N), full(qn), full(kvn), full(wq), full(wqr),
                  full(wk), full(wvt), full(pmat), tab, tab],
        out_specs=(row(hw), row(hw),
                   pl.BlockSpec((None, tm // CK, N_HEADS * VROWS, CK), lambda i, bb: (bb, i, 0, 0))),
        compiler_params=_cparams(2),
        name="mla_prep",
    )(dcq, ckv, kr, qn, kvn, wq, wqr, wk, wvt, pmat, ct, st)


def _mla_kernel(q_ref, k_ref, vt_ref, o_ref, qt_ref, s_ref, m_ref, acc_ref, ot_ref):
    i = pl.program_id(1)
    kpos = lax.broadcasted_iota(I32, (CK, TQ), 0)
    qpos = lax.broadcasted_iota(I32, (CK, TQ), 1)
    hs = [slice(h * LANES, (h + 1) * LANES) for h in range(N_HEADS)]
    for h in range(N_HEADS):
        qt_ref[h] = q_ref[:, hs[h]].astype(F32).T.astype(BF16)

    def qk_all(c):
        start = pl.multiple_of(c * CK, CK)
        return [jnp.dot(k_ref[pl.ds(start, CK), hs[h]], qt_ref[h], preferred_element_type=F32)
                for h in range(N_HEADS)]

    _flash_loop(i, qk_all,
                lambda ctx, h, s: jnp.where(kpos <= qpos, s, NEG) if ctx[1] else s,
                lambda c, h: vt_ref[c, h * VROWS:(h + 1) * VROWS, :],
                (s_ref, m_ref, acc_ref))
    for h in range(N_HEADS):
        ot_ref[h * HEAD_DIM:(h + 1) * HEAD_DIM, :] = _softmax_out(acc_ref.at[h])
    o_ref[...] = ot_ref[...].T.astype(o_ref.dtype)


def _mla(qm, km, vmt):
    b, t, hw = qm.shape
    kspec, vspec = _kv_specs(t, hw)
    return pl.pallas_call(
        _mla_kernel,
        out_shape=jax.ShapeDtypeStruct((b, t, BRANCH_W), BF16),
        grid=(b, t // TQ),
        in_specs=[pl.BlockSpec((None, TQ, hw), lambda bb, i: (bb, i, 0)), kspec, vspec],
        out_specs=pl.BlockSpec((None, TQ, BRANCH_W), lambda bb, i: (bb, i, 0)),
        scratch_shapes=[pltpu.VMEM((N_HEADS, LANES, TQ), BF16)] + _attn_scratch(N_HEADS),
        compiler_params=_cparams(2),
        name="mla",
    )(qm, km, vmt)


def _matmul_kernel(x_ref, w_ref, o_ref):
    o_ref[...] = jnp.dot(x_ref[...].astype(BF16), w_ref[...], preferred_element_type=F32).astype(o_ref.dtype)


def _mem_kv(mem, w):
    b, m, d = mem.shape
    n = w.shape[1]
    return pl.pallas_call(
        _matmul_kernel,
        out_shape=jax.ShapeDtypeStruct((b, m, n), BF16),
        grid=(b,),
        in_specs=[pl.BlockSpec((None, m, d), lambda bb: (bb, 0, 0)), pl.BlockSpec((d, n), lambda bb: (0, 0))],
        out_specs=pl.BlockSpec((None, m, n), lambda bb: (bb, 0, 0)),
        compiler_params=_cparams(1),
        name="mem_kv",
    )(mem, w)


def _mem_kernel(q_ref, kv_ref, o_ref):
    tq = q_ref.shape[0]
    lane_q = lax.broadcasted_iota(I32, (tq, BRANCH_W), 1)
    q = q_ref[...].astype(F32) * (HEAD_DIM ** -0.5)
    mk = kv_ref[:, :BRANCH_W]
    mv = kv_ref[:, BRANCH_W:]
    out = jnp.zeros((tq, BRANCH_W), F32)
    for h in range(N_HEADS):
        in_h = (lane_q >> 6) == h
        s = _nt_dot(jnp.where(in_h, q, 0.0).astype(BF16), mk)
        p = jnp.exp(s - jnp.max(s, axis=1, keepdims=True))
        o_h = jnp.dot(p.astype(BF16), mv, preferred_element_type=F32) / jnp.sum(p, axis=1, keepdims=True)
        out = jnp.where(in_h, o_h, out)
    o_ref[...] = out.astype(o_ref.dtype)


def _mem_attn(eq, mkv):
    b, t, w = eq.shape
    m = mkv.shape[1]
    tq = 512
    return pl.pallas_call(
        _mem_kernel,
        out_shape=jax.ShapeDtypeStruct((b, t, w), BF16),
        grid=(b, t // tq),
        in_specs=[pl.BlockSpec((None, tq, w), lambda bb, i: (bb, i, 0)),
                  pl.BlockSpec((None, m, 2 * w), lambda bb, i: (bb, 0, 0))],
        out_specs=pl.BlockSpec((None, tq, w), lambda bb, i: (bb, i, 0)),
        compiler_params=_cparams(2),
        name="mem_attn",
    )(eq, mkv)


def _final_kernel(h_ref, hb_ref, oa_ref, ob_ref, oc_ref, od_ref, oe_ref, z_ref,
                  wg_ref, wb_ref, wo_ref, g_ref, b_ref, h_out, hb_out, *, alpha):
    d = h_ref.shape[1]
    acc = jnp.zeros(h_ref.shape, F32)
    for n, o_ref in enumerate((oa_ref, ob_ref, oc_ref, od_ref, oe_ref)):
        z = z_ref[:, n * BRANCH_W:(n + 1) * BRANCH_W].astype(F32)
        y = o_ref[...].astype(F32) * (z / (1.0 + jnp.exp(-z)))
        u = jnp.dot(y.astype(BF16), wb_ref[n], preferred_element_type=F32)
        g = jnp.dot(hb_ref[...], wg_ref[:, n * d:(n + 1) * d], preferred_element_type=F32)
        acc = acc + u / (1.0 + jnp.exp(-g))
    out = jnp.dot(acc.astype(BF16), wo_ref[...], preferred_element_type=F32)
    x = alpha * h_ref[...] + out
    mu = jnp.mean(x, axis=1, keepdims=True)
    xc = x - mu
    var = jnp.mean(xc * xc, axis=1, keepdims=True)
    y = xc * lax.rsqrt(var + LN_EPS) * g_ref[...] + b_ref[...]
    h_out[...] = y
    hb_out[...] = y.astype(BF16)


def _final(h, hb, os5, z, wg, wb, wo, ln_g, ln_b, alpha):
    n, d = h.shape
    tm = 256
    row = lambda w: pl.BlockSpec((tm, w), lambda i: (i, 0))
    full = lambda a: pl.BlockSpec(a.shape, lambda i: (0,) * a.ndim)
    return pl.pallas_call(
        functools.partial(_final_kernel, alpha=alpha),
        out_shape=(jax.ShapeDtypeStruct((n, d), F32), jax.ShapeDtypeStruct((n, d), BF16)),
        grid=(n // tm,),
        in_specs=[row(d), row(d)] + [row(BRANCH_W)] * N_BRANCH + [row(N_BRANCH * BRANCH_W),
                  full(wg), full(wb), full(wo), full(ln_g), full(ln_b)],
        out_specs=(row(d), row(d)),
        compiler_params=_cparams(1),
        name="merge_out_ln",
    )(h, hb, *os5, z, wg, wb, wo, ln_g, ln_b)


ROPE_GROUPS = (("a_q", N_HEADS, HEAD_DIM, ROT_64), ("a_k", N_HEADS, HEAD_DIM, ROT_64),
               ("i_q", IDX_HEADS, IDX_DIM, ROT_32), ("i_k", IDX_HEADS, IDX_DIM, ROT_32),
               ("b_q", N_HEADS, HEAD_DIM, ROT_64), ("b_k", N_HEADS, HEAD_DIM, ROT_64),
               ("c_q", 2 * N_HEADS, DIFF_DIM, ROT_32), ("c_k", 2 * N_HEADS, DIFF_DIM, ROT_32),
               ("d_kr", 1, MXU_N, MLA_ROPE))
PLAIN_COLS = ("d_cq", "d_ckv", "i_w", None, "e_q", "z")
VALUE_COLS = ("a_v", "b_v", "c_v")


def _weight_prep_kernel(w_ref, plain_ref, vt_ref, rope_ref, g_ref):
    rows = w_ref.shape[0]

    def put(ref, names):
        off = 0
        for name in names:
            if name is None:
                part = jnp.zeros((rows, -off % MXU_N), ref.dtype)
            else:
                o, s = OFF[name]
                part = w_ref[:, o:o + s].astype(ref.dtype)
            ref[:, off:off + part.shape[1]] = part
            off += part.shape[1]
        assert off == ref.shape[1]

    put(plain_ref, PLAIN_COLS)
    put(vt_ref, VALUE_COLS)
    rope_cols = []
    for name, nh, hd, _ in ROPE_GROUPS:
        copies = IDX_HEADS if name == "i_k" else 1
        rope_cols += [name] * copies + ([None] if copies * OFF[name][1] < MXU_N else [])
    put(rope_ref, rope_cols)
    put(g_ref, ("g",))


def _weight_prep(w_in):
    depth, d, n = w_in.shape
    tm = 64
    widths = (sum(OFF[c][1] for c in PLAIN_COLS if c) + MXU_N - KV_LORA - IDX_HEADS,
              len(VALUE_COLS) * BRANCH_W, len(ROPE_GROUPS) * MXU_N, OFF["g"][1])
    return pl.pallas_call(
        _weight_prep_kernel,
        out_shape=tuple(jax.ShapeDtypeStruct((depth, d, w), BF16) for w in widths),
        grid=(depth, d // tm),
        in_specs=[pl.BlockSpec((None, tm, n), lambda l, i: (l, i, 0))],
        out_specs=tuple(pl.BlockSpec((None, tm, w), lambda l, i: (l, i, 0)) for w in widths),
        compiler_params=_cparams(2),
        name="weight_prep",
    )(w_in)


def _rope_tables(seq, rot_dim):
    pos = jnp.arange(seq, dtype=F32)
    inv = ROPE_THETA ** (-jnp.arange(0, rot_dim, 2, dtype=F32) / rot_dim)
    ang = pos[:, None] * inv[None, :]
    return jnp.cos(ang), jnp.sin(ang)


def _rope_cs(t, nh, hd, r):
    cos, sin = _rope_tables(t, r)
    c = jnp.concatenate([cos, cos, jnp.ones((t, hd - r), F32)], axis=1)
    s = jnp.concatenate([-sin, sin, jnp.zeros((t, hd - r), F32)], axis=1)
    return jnp.tile(c, (1, nh)), jnp.tile(s, (1, nh))


def kernel(x, mem, ln0_g, ln0_b, w_in, mla_q_norm, w_uq, mla_kv_norm, w_ukv, diff_lam, diff_norm,
           w_mem_kv, w_branch, w_out, ln_g, ln_b):
    b, t, d = x.shape
    depth = w_in.shape[0]
    alpha = (2 * depth) ** 0.25
    assert t % 512 == 0 and d == 1024

    w_plain, w_vt, w_rope, wg = _weight_prep(w_in)
    plain_widths = (BRANCH_W,) * 3 + (N_BRANCH * BRANCH_W,)
    rope_groups = [(nh, hd, r) for _, nh, hd, r in ROPE_GROUPS]
    rope_heads = tuple((hd, r // 2) for _, hd, r in rope_groups)
    cs = [_rope_cs(t, nh, hd, r) for nh, hd, r in rope_groups]
    ctab = jnp.concatenate([c for c, _ in cs], axis=1)
    stab = jnp.concatenate([s for _, s in cs], axis=1)

    uq = w_uq.reshape(depth, Q_LORA, N_HEADS, MLA_NOPE + MLA_ROPE)
    qn_w, qr_w = uq[..., :MLA_NOPE], uq[..., MLA_NOPE:]
    pad32 = jnp.zeros((depth, Q_LORA, N_HEADS, LANES - MLA_NOPE - MLA_ROPE), w_uq.dtype)
    hw = N_HEADS * LANES
    wq = jnp.concatenate([qn_w, qr_w, pad32], axis=-1).reshape(depth, Q_LORA, hw).astype(BF16)
    half = MLA_ROPE // 2
    wq_rot = jnp.concatenate([jnp.zeros_like(qn_w), -qr_w[..., half:], qr_w[..., :half], pad32],
                             axis=-1).reshape(depth, Q_LORA, hw).astype(BF16)
    cos_m, sin_m = _rope_tables(t, MLA_ROPE)
    one = lambda n: jnp.ones((t, n), F32)
    zer = lambda n: jnp.zeros((t, n), F32)
    qs = (MLA_NOPE + MLA_ROPE) ** -0.5 * LOG2E
    ct_q = qs * jnp.tile(jnp.concatenate([one(MLA_NOPE), cos_m, cos_m, one(LANES - MLA_NOPE - MLA_ROPE)], axis=1), (1, N_HEADS))
    st_q = qs * jnp.tile(jnp.concatenate([zer(MLA_NOPE), sin_m, sin_m, zer(LANES - MLA_NOPE - MLA_ROPE)], axis=1), (1, N_HEADS))
    ukv = w_ukv.reshape(depth, KV_LORA, N_HEADS, MLA_NOPE + MLA_V)
    wk = jnp.concatenate([ukv[..., :MLA_NOPE], jnp.zeros((depth, KV_LORA, N_HEADS, LANES - MLA_NOPE), w_ukv.dtype)],
                         axis=-1).reshape(depth, KV_LORA, hw).astype(BF16)
    wvt = ukv[..., MLA_NOPE:].reshape(depth, KV_LORA, N_HEADS * MLA_V).astype(BF16)
    place = np.zeros((MXU_N, hw), np.float32)
    for hh in range(N_HEADS):
        for j in range(MLA_ROPE):
            place[j, hh * LANES + MLA_NOPE + j] = 1.0
    place = jnp.asarray(place, BF16)

    wb = w_branch.astype(BF16)
    wo = w_out.astype(BF16)
    wmem = w_mem_kv.astype(BF16)
    norm_t = jnp.broadcast_to(diff_norm.astype(F32)[:, :, None], (depth, HEAD_DIM, TQ))

    h, hb = _layer_norm0(x.reshape(b * t, d), ln0_g, ln0_b)
    for l in range(depth):
        hb3 = hb.reshape(b, t, d)
        avt, bvt, cvt, dcq, ckv_iw, eq, z = _proj_plain(hb3, w_plain[l], w_vt[l], plain_widths)
        aq, ak, iq, ik, bq, bk, cq, ck, kr = _proj_rope(hb3, w_rope[l], ctab, stab, rope_heads)

        o_a = _dsa(aq, ak, avt, iq, ik, ckv_iw)
        o_b = _moba(bq, bk, bvt, _kbar(bk))
        lam_init = 0.8 - 0.6 * math.exp(-0.3 * l)
        misc = jnp.full((SUBLANES, LANES), lam_init, F32)
        o_c = _diff(cq, ck, cvt, diff_lam[l].astype(F32), norm_t[l], misc)
        qm, km, vmt = _mla_prep(dcq, ckv_iw, kr, mla_q_norm[l].reshape(1, Q_LORA), mla_kv_norm[l].reshape(1, KV_LORA),
                                wq[l], wq_rot[l], wk[l], wvt[l], place, ct_q, st_q)
        o_d = _mla(qm, km, vmt)
        o_e = _mem_attn(eq, _mem_kv(mem, wmem[l]))

        os5 = [o.reshape(b * t, BRANCH_W) for o in (o_a, o_b, o_c, o_d, o_e)]
        h, hb = _final(h, hb, os5, z.reshape(b * t, N_BRANCH * BRANCH_W), wg[l], wb[l], wo[l],
                       ln_g[l].reshape(1, d), ln_b[l].reshape(1, d), alpha)
    return h.reshape(b, t, d)
```

```python
import functools
import math

import numpy as np
import jax
import jax.numpy as jnp
from jax import lax
from jax.experimental import pallas as pl
from jax.experimental.pallas import tpu as pltpu

F32 = jnp.float32
BF16 = jnp.bfloat16
I32 = jnp.int32
I16 = jnp.int16

N_HEADS = 4
HEAD_DIM = 64
BRANCH_W = N_HEADS * HEAD_DIM
N_BRANCH = 5
ROPE_THETA = 500000.0
ROT_64 = 16
ROT_32 = 8
IDX_HEADS = 8
IDX_DIM = 32
TOPK_MAX = 256
MOBA_BLOCK = 256
MOBA_TOPK = 3
DIFF_DIM = 32
Q_LORA = 256
KV_LORA = 128
MLA_NOPE = 64
MLA_ROPE = 32
MLA_V = 64
LN_EPS = 1e-5
RMS_EPS = 1e-6

IN_LAYOUT = (
    ("a_q", BRANCH_W), ("a_k", BRANCH_W), ("a_v", BRANCH_W),
    ("i_q", IDX_HEADS * IDX_DIM), ("i_k", IDX_DIM), ("i_w", IDX_HEADS),
    ("b_q", BRANCH_W), ("b_k", BRANCH_W), ("b_v", BRANCH_W),
    ("c_q", BRANCH_W), ("c_k", BRANCH_W), ("c_v", BRANCH_W),
    ("d_cq", Q_LORA), ("d_ckv", KV_LORA), ("d_kr", MLA_ROPE),
    ("e_q", BRANCH_W),
    ("z", N_BRANCH * BRANCH_W),
    ("g", N_BRANCH * 1024),
)

SUBLANES = 8
LANES = 128
MXU_N = 256
TQ = 256
CK = 256
VROWS = HEAD_DIM + 16
FLASH_UNROLL = 4
NEG = -1e30
LOG2E = math.log2(math.e)
INT_MIN = np.int32(-2 ** 31)
HALF16 = 1 << 15
VMEM_LIMIT = 56 * 1024 * 1024


def _offsets():
    off, out = 0, {}
    for name, size in IN_LAYOUT:
        out[name] = (off, size)
        off += size
    return out


OFF = _offsets()


def _nt_dot(a, b):
    return lax.dot_general(a, b, (((1,), (1,)), ((), ())), preferred_element_type=F32)


def _tn_dot(w, x):
    return lax.dot_general(w, x, (((0,), (1,)), ((), ())), preferred_element_type=F32)


def _fold_rows(w, rows=SUBLANES):
    xs = [w[r:r + rows, :] for r in range(0, w.shape[0], rows)]
    while len(xs) > 1:
        xs = [xs[j] + xs[j + 1] for j in range(0, len(xs) - 1, 2)] + ([xs[-1]] if len(xs) % 2 else [])
    return xs[0]


def _masked_qt(q, shift, n, qt_ref):
    qt = q.T
    dim = lax.broadcasted_iota(I32, (LANES, qt.shape[1]), 0)
    for j in range(n):
        half = (j << shift) // LANES
        rows = qt[half * LANES:(half + 1) * LANES, :]
        qt_ref[j] = jnp.where(((dim + half * LANES) >> shift) == j, rows, 0.0).astype(BF16)


def _half(kc, j, shift):
    half = (j << shift) // LANES
    return kc[:, half * LANES:(half + 1) * LANES]


def _cparams(n_axes):
    return pltpu.CompilerParams(dimension_semantics=("arbitrary",) * n_axes,
                                vmem_limit_bytes=VMEM_LIMIT)


def _layer_spec(a, l):
    return pl.BlockSpec((None,) + a.shape[1:], lambda *_: (l,) + (0,) * (a.ndim - 1))


def _softmax_step(s_t, vt_h, m_ref, acc_ref):
    m_old = m_ref[...]
    m_new = jnp.maximum(m_old, jnp.max(s_t, axis=0, keepdims=True))
    alpha = jnp.exp2(m_old - m_new)
    p = jnp.exp2(s_t - m_new)
    acc_ref[...] = alpha * acc_ref[...] + jnp.dot(vt_h, p.astype(BF16), preferred_element_type=F32)
    m_ref[...] = m_new


def _softmax_init(m_ref, acc_ref):
    m_ref[...] = jnp.full(m_ref.shape, NEG, F32)
    acc_ref[...] = jnp.zeros(acc_ref.shape, F32)


def _softmax_out(acc_ref):
    return acc_ref[:HEAD_DIM, :] / acc_ref[HEAD_DIM:HEAD_DIM + 1, :]


def _store_vt(o_ref, vt):
    ones = jnp.ones((VROWS - HEAD_DIM, CK), o_ref.dtype)
    for j in range(o_ref.shape[0]):
        for h in range(N_HEADS):
            o_ref[j, h * VROWS:h * VROWS + HEAD_DIM, :] = (
                vt[h * HEAD_DIM:(h + 1) * HEAD_DIM, j * CK:(j + 1) * CK].astype(o_ref.dtype))
            o_ref[j, h * VROWS + HEAD_DIM:(h + 1) * VROWS, :] = ones


def _flash_loop(n_prev, qk_all, mask, vt_rows, state, prep=None):
    s_ref, m_ref, acc_ref = state
    n_state = m_ref.shape[0]
    for j in range(n_state):
        _softmax_init(m_ref.at[j], acc_ref.at[j])

    def park(c, slot):
        for j, s in enumerate(qk_all(c)):
            s_ref[slot, j] = s

    def consume(c, slot, diag):
        ctx = (c, diag) if prep is None else prep(c, diag)
        for j in range(n_state):
            _softmax_step(mask(ctx, j, s_ref[slot, j]), vt_rows(c, j),
                          m_ref.at[j], acc_ref.at[j])

    park(0, 0)

    def pair(c):
        park(c + 1, 1)
        consume(c, 0, False)
        park(c + 2, 0)
        consume(c + 1, 1, False)

    def body(g, carry):
        for u in range(0, FLASH_UNROLL, 2):
            pair(FLASH_UNROLL * g + u)
        return carry

    n_group = lax.shift_right_logical(n_prev, FLASH_UNROLL.bit_length() - 1)
    lax.fori_loop(0, n_group, body, 0)
    c0 = FLASH_UNROLL * n_group
    for u in range(FLASH_UNROLL // 2 - 1):
        @pl.when(n_prev - c0 >= 2 * (u + 1))
        def _(u=u):
            pair(c0 + 2 * u)
    c0 = c0 + 2 * lax.shift_right_logical(n_prev - c0, 1)
    odd = (n_prev & 1) == 1

    @pl.when(odd)
    def _():
        park(c0 + 1, 1)
        consume(c0, 0, False)
        consume(c0 + 1, 1, True)

    @pl.when(jnp.logical_not(odd))
    def _():
        consume(c0, 0, True)


def _attn_scratch(n_state):
    return [pltpu.VMEM((2, n_state, CK, TQ), F32), pltpu.VMEM((n_state, 1, TQ), F32),
            pltpu.VMEM((n_state, VROWS, TQ), F32), pltpu.VMEM((BRANCH_W, TQ), F32)]


def _kv_specs(t, w):
    kspec = pl.BlockSpec((None, t, w), lambda bb, i: (bb, 0, 0))
    vspec = pl.BlockSpec((None, t // CK, N_HEADS * VROWS, CK), lambda bb, i: (bb, 0, 0, 0))
    return kspec, vspec


def _ln_kernel(x_ref, g_ref, b_ref, h_ref, hb_ref):
    x = x_ref[...]
    mu = jnp.mean(x, axis=1, keepdims=True)
    xc = x - mu
    var = jnp.mean(xc * xc, axis=1, keepdims=True)
    y = xc * lax.rsqrt(var + LN_EPS) * g_ref[...] + b_ref[...]
    h_ref[...] = y
    hb_ref[...] = y.astype(BF16)


def _layer_norm0(x2, g, b):
    n, d = x2.shape
    tm = 512
    row = pl.BlockSpec((tm, d), lambda i: (i, 0))
    vec = pl.BlockSpec((1, d), lambda i: (0, 0))
    return pl.pallas_call(
        _ln_kernel,
        out_shape=(jax.ShapeDtypeStruct((n, d), F32), jax.ShapeDtypeStruct((n, d), BF16)),
        grid=(n // tm,),
        in_specs=[row, vec, vec],
        out_specs=(row, row),
        compiler_params=_cparams(1),
        name="ln0",
    )(x2, g.reshape(1, d), b.reshape(1, d))


def _proj_plain_kernel(x_ref, w_ref, wt_ref, *out_refs, n_t):
    for g, o_ref in enumerate(out_refs[:n_t]):
        _store_vt(o_ref, _tn_dot(wt_ref[:, g * BRANCH_W:(g + 1) * BRANCH_W], x_ref[...]))
    off = 0
    for o_ref in out_refs[n_t:]:
        wd = o_ref.shape[-1]
        for j in range(0, wd, MXU_N):
            acc = jnp.dot(x_ref[...], w_ref[:, off + j:off + j + MXU_N], preferred_element_type=F32)
            o_ref[:, j:j + MXU_N] = acc.astype(o_ref.dtype)
        off += wd


def _proj_plain(hb3, w, wt, widths, l):
    b, t, d = hb3.shape
    tm = 512
    n_t = wt.shape[-1] // BRANCH_W
    shapes = [jax.ShapeDtypeStruct((b, t // CK, N_HEADS * VROWS, CK), BF16)] * n_t
    specs = [pl.BlockSpec((None, tm // CK, N_HEADS * VROWS, CK), lambda i, bb: (bb, i, 0, 0))] * n_t
    shapes += [jax.ShapeDtypeStruct((b, t, wd), BF16) for wd in widths]
    specs += [pl.BlockSpec((None, tm, wd), lambda i, bb: (bb, i, 0)) for wd in widths]
    return pl.pallas_call(
        functools.partial(_proj_plain_kernel, n_t=n_t),
        out_shape=tuple(shapes),
        grid=(t // tm, b),
        in_specs=[pl.BlockSpec((None, tm, d), lambda i, bb: (bb, i, 0)),
                  _layer_spec(w, l), _layer_spec(wt, l)],
        out_specs=tuple(specs),
        compiler_params=_cparams(2),
        name="proj_plain",
    )(hb3, w, wt)


def _proj_rope_kernel(x_ref, w_ref, c_ref, s_ref, *out_refs, heads, tables):
    lane = lax.broadcasted_iota(I32, (x_ref.shape[0], MXU_N), 1)
    for g, o_ref in enumerate(out_refs):
        hd, half = heads[g]
        sl = slice(g * MXU_N, (g + 1) * MXU_N)
        acc = jnp.dot(x_ref[...], w_ref[:, sl], preferred_element_type=F32)
        partner = jnp.where((lane & (hd - 1)) < half,
                            pltpu.roll(acc, MXU_N - half, 1), pltpu.roll(acc, half, 1))
        o_ref[...] = (acc * c_ref[tables[g]] + partner * s_ref[tables[g]]).astype(o_ref.dtype)


def _proj_rope(hb3, w, ctab, stab, heads, tables, l):
    b, t, d = hb3.shape
    tm = 512
    assert w.shape[-1] == MXU_N * len(heads)
    tspec = pl.BlockSpec((ctab.shape[0], tm, MXU_N), lambda i, bb: (0, i, 0))
    ospec = pl.BlockSpec((None, tm, MXU_N), lambda i, bb: (bb, i, 0))
    return pl.pallas_call(
        functools.partial(_proj_rope_kernel, heads=heads, tables=tables),
        out_shape=(jax.ShapeDtypeStruct((b, t, MXU_N), BF16),) * len(heads),
        grid=(t // tm, b),
        in_specs=[pl.BlockSpec((None, tm, d), lambda i, bb: (bb, i, 0)),
                  _layer_spec(w, l), tspec, tspec],
        out_specs=(ospec,) * len(heads),
        compiler_params=_cparams(2),
        name="proj_rope",
    )(hb3, w, ctab, stab)


def _dsa_kernel(aq_ref, ak_ref, avt_ref, iq_ref, ik_ref, iw_ref, pick_ref, tri_ref, o_ref,
                keys_ref, hi_ref, lo_ref, iqt_ref, aqt_ref, wt_ref, thr_ref, s_ref, m_ref, acc_ref, ot_ref,
                *, topk, idx_scale):
    i = pl.program_id(1)
    nk = i + 1
    kpos = lax.broadcasted_iota(I32, (CK, TQ), 0)
    qpos = lax.broadcasted_iota(I32, (CK, TQ), 1)

    _masked_qt(iq_ref[...].astype(F32), 5, IDX_HEADS, iqt_ref)
    _masked_qt(aq_ref[...].astype(F32) * (HEAD_DIM ** -0.5 * LOG2E), 6, N_HEADS, aqt_ref)
    wt_ref[...] = _nt_dot(pick_ref[...], iw_ref[...]) * idx_scale

    def logits(c):
        kc = ik_ref[pl.ds(pl.multiple_of(c * CK, CK), CK), :]
        return [jnp.dot(_half(kc, hh, 5), iqt_ref[hh], preferred_element_type=F32) for hh in range(IDX_HEADS)]

    def put_keys(c, key):
        keys_ref[c] = key
        hi_ref[c] = (key >> 16).astype(I16)
        lo_ref[c] = ((key & 0xFFFF) - HALF16).astype(I16)

    def score_chunk(c, lg, diag):
        sc = jnp.zeros((CK, TQ), F32)
        for hh in range(IDX_HEADS):
            sc = sc + jnp.maximum(lg[hh], 0.0) * wt_ref[hh:hh + 1, :]
        bits = pltpu.bitcast(sc, I32)
        key = jnp.where(bits < 0, INT_MIN - bits, bits)
        put_keys(c, jnp.where(kpos <= qpos, key, INT_MIN) if diag else key)

    def score_pair(p, carry):
        lg0, lg1 = logits(2 * p), logits(2 * p + 1)
        score_chunk(2 * p, lg0, False)
        score_chunk(2 * p + 1, lg1, False)
        return carry

    lax.fori_loop(0, lax.shift_right_logical(i, 1), score_pair, 0)

    @pl.when((i & 1) == 1)
    def _():
        score_chunk(i - 1, logits(i - 1), False)

    score_chunk(i, logits(i), True)

    @pl.when(jnp.logical_and((nk & 1) == 1, nk < keys_ref.shape[0]))
    def _():
        put_keys(nk, jnp.full((CK, TQ), INT_MIN, I32))

    def pair_loop(body, init):
        def pair(p, carry):
            return body(2 * p + 1, body(2 * p, carry))
        return lax.fori_loop(0, lax.shift_right_logical(nk + 1, 1), pair, init)

    def count16(ref, pred):
        def body(c, part):
            return part + _fold_rows(jnp.where(pred(ref[c]), jnp.int16(1), jnp.int16(0)), 2 * SUBLANES)
        part = pair_loop(body, jnp.zeros((2 * SUBLANES, TQ), I16))
        return jnp.sum(part.astype(F32), axis=0, keepdims=True)

    def search16(ref, need):
        def bit_body(bi, t_u):
            c_u = t_u | jnp.left_shift(jnp.int32(1), 15 - bi)
            ck = (c_u - HALF16).astype(I16)
            cnt = count16(ref, lambda v: v >= ck)
            return jnp.where(cnt >= need, c_u, t_u)
        return lax.fori_loop(0, 16, bit_body, jnp.zeros((1, TQ), I32))

    hi_u = search16(hi_ref, float(topk))
    thr_hi = (hi_u - HALF16).astype(I16)
    n_above = count16(hi_ref, lambda v: v > thr_hi)

    def bucket_body(c, carry):
        hi_ref[c] = jnp.where(hi_ref[c] == thr_hi, lo_ref[c], jnp.int16(-HALF16))
        return carry

    pair_loop(bucket_body, 0)
    lo_u = search16(hi_ref, float(topk) - n_above)
    thr = ((hi_u - HALF16) << 16) | lo_u

    def tie_body(c, carry):
        k = keys_ref[c]
        return (carry[0] + _fold_rows(jnp.where(k > thr, 1.0, 0.0)),
                carry[1] + _fold_rows(jnp.where(k == thr, 1.0, 0.0)))

    zero8 = jnp.zeros((SUBLANES, TQ), F32)
    gt8, eq8 = pair_loop(tie_body, (zero8, zero8))
    need = float(topk) - jnp.sum(gt8, axis=0, keepdims=True)
    amb = jnp.logical_and(jnp.sum(eq8, axis=0, keepdims=True) > need, thr > INT_MIN)
    any_amb = jnp.max(jnp.where(amb, 1.0, 0.0)) > 0.5

    @pl.when(any_amb)
    def _():
        def drop_body(c, seen):
            k = keys_ref[c]
            eq = k == thr
            eqf = jnp.where(eq, 1.0, 0.0)
            rank = jnp.dot(tri_ref[...], eqf.astype(BF16), preferred_element_type=F32) + seen
            drop = jnp.logical_and(jnp.logical_and(eq, rank > need), amb)
            keys_ref[c] = jnp.where(drop, INT_MIN, k)
            return seen + jnp.sum(eqf, axis=0, keepdims=True)

        lax.fori_loop(0, nk, drop_body, jnp.zeros((1, TQ), F32))

    thr_ref[...] = jnp.maximum(thr, INT_MIN + 1)

    def qk_all(c):
        kc = ak_ref[pl.ds(pl.multiple_of(c * CK, CK), CK), :]
        return [jnp.dot(_half(kc, h, 6), aqt_ref[h], preferred_element_type=F32) for h in range(N_HEADS)]

    _flash_loop(i, qk_all,
                lambda keep, h, s: jnp.where(keep, s, NEG),
                lambda c, h: avt_ref[c, h * VROWS:(h + 1) * VROWS, :],
                (s_ref, m_ref, acc_ref),
                prep=lambda c, diag: keys_ref[c] >= thr_ref[...])
    for h in range(N_HEADS):
        ot_ref[h * HEAD_DIM:(h + 1) * HEAD_DIM, :] = _softmax_out(acc_ref.at[h])
    o_ref[...] = ot_ref[...].T.astype(o_ref.dtype)


def _dsa(aq, ak, avt, iq, ik, iw):
    b, t, _ = aq.shape
    topk = min(TOPK_MAX, t // 4)
    qspec = pl.BlockSpec((None, TQ, BRANCH_W), lambda bb, i: (bb, i, 0))
    kspec, vspec = _kv_specs(t, BRANCH_W)
    pick = np.zeros((2 * SUBLANES, MXU_N), np.float32)
    for hh in range(IDX_HEADS):
        pick[hh, KV_LORA + hh] = 1.0
    pick = jnp.asarray(pick, BF16)
    tri = jnp.asarray(np.tril(np.ones((CK, CK), np.float32)), BF16)
    kern = functools.partial(_dsa_kernel, topk=topk, idx_scale=(IDX_HEADS * IDX_DIM) ** -0.5)
    return pl.pallas_call(
        kern,
        out_shape=jax.ShapeDtypeStruct((b, t, BRANCH_W), BF16),
        grid=(b, t // TQ),
        in_specs=[qspec, kspec, vspec, qspec, kspec, qspec,
                  pl.BlockSpec(pick.shape, lambda bb, i: (0, 0)), pl.BlockSpec(tri.shape, lambda bb, i: (0, 0))],
        out_specs=qspec,
        scratch_shapes=[
            pltpu.VMEM((t // CK, CK, TQ), I32),
            pltpu.VMEM((t // CK, CK, TQ), I16),
            pltpu.VMEM((t // CK, CK, TQ), I16),
            pltpu.VMEM((IDX_HEADS, LANES, TQ), BF16),
            pltpu.VMEM((N_HEADS, LANES, TQ), BF16),
            pltpu.VMEM((2 * SUBLANES, TQ), F32),
            pltpu.VMEM((1, TQ), I32),
        ] + _attn_scratch(N_HEADS),
        compiler_params=_cparams(2),
        name="dsa",
    )(aq, ak, avt, iq, ik, iw, pick, tri)


def _kbar_kernel(k_ref, o_ref):
    o_ref[...] = jnp.zeros(o_ref.shape, o_ref.dtype)
    nb = k_ref.shape[0] // MOBA_BLOCK
    for n in range(nb):
        blk = k_ref[n * MOBA_BLOCK:(n + 1) * MOBA_BLOCK, :].astype(F32)
        o_ref[n:n + 1, :] = jnp.mean(blk, axis=0, keepdims=True).astype(o_ref.dtype)


def _kbar(bk):
    b, t, w = bk.shape
    nbp = max(2 * SUBLANES, t // MOBA_BLOCK)
    return pl.pallas_call(
        _kbar_kernel,
        out_shape=jax.ShapeDtypeStruct((b, nbp, w), BF16),
        grid=(b,),
        in_specs=[pl.BlockSpec((None, t, w), lambda bb: (bb, 0, 0))],
        out_specs=pl.BlockSpec((None, nbp, w), lambda bb: (bb, 0, 0)),
        compiler_params=_cparams(1),
        name="moba_kbar",
    )(bk)


def _moba_kernel(q_ref, k_ref, vt_ref, kbar_ref, o_ref, qt_ref, bias_ref, s_ref, m_ref, acc_ref, ot_ref):
    i = pl.program_id(1)
    nbp = kbar_ref.shape[0]
    blk = lax.broadcasted_iota(I32, (nbp, TQ), 0)
    blk_f = blk.astype(F32)
    kpos = lax.broadcasted_iota(I32, (CK, TQ), 0)
    qpos = lax.broadcasted_iota(I32, (CK, TQ), 1)
    _masked_qt(q_ref[...].astype(F32) * (HEAD_DIM ** -0.5 * LOG2E), 6, N_HEADS, qt_ref)

    for h in range(N_HEADS):
        g = jnp.where(blk < i, jnp.dot(_half(kbar_ref[...], h, 6), qt_ref[h], preferred_element_type=F32), NEG)
        bias = jnp.full((nbp, TQ), NEG, F32)
        for _ in range(MOBA_TOPK):
            mx = jnp.max(g, axis=0, keepdims=True)
            first = jnp.min(jnp.where(g == mx, blk_f, 1e9), axis=0, keepdims=True)
            pick = jnp.logical_and(blk_f == first, mx > 0.5 * NEG)
            bias = jnp.where(pick, 0.0, bias)
            g = jnp.where(pick, NEG, g)
        bias_ref[h] = bias

    def qk_all(c):
        kc = k_ref[pl.ds(pl.multiple_of(c * CK, CK), CK), :]
        return [jnp.dot(_half(kc, h, 6), qt_ref[h], preferred_element_type=F32) for h in range(N_HEADS)]

    def mask(ctx, h, s):
        c, diag = ctx
        return jnp.where(kpos <= qpos, s, NEG) if diag else s + bias_ref[h, pl.ds(c, 1), :]

    _flash_loop(i, qk_all, mask, lambda c, h: vt_ref[c, h * VROWS:(h + 1) * VROWS, :],
                (s_ref, m_ref, acc_ref))
    for h in range(N_HEADS):
        ot_ref[h * HEAD_DIM:(h + 1) * HEAD_DIM, :] = _softmax_out(acc_ref.at[h])
    o_ref[...] = ot_ref[...].T.astype(o_ref.dtype)


def _moba(bq, bk, bvt, kbar):
    b, t, w = bq.shape
    assert TQ == MOBA_BLOCK and CK == MOBA_BLOCK and t % MOBA_BLOCK == 0
    nbp = kbar.shape[1]
    qspec = pl.BlockSpec((None, TQ, w), lambda bb, i: (bb, i, 0))
    kspec, vspec = _kv_specs(t, w)
    return pl.pallas_call(
        _moba_kernel,
        out_shape=jax.ShapeDtypeStruct((b, t, w), BF16),
        grid=(b, t // TQ),
        in_specs=[qspec, kspec, vspec, pl.BlockSpec((None, nbp, w), lambda bb, i: (bb, 0, 0))],
        out_specs=qspec,
        scratch_shapes=[pltpu.VMEM((N_HEADS, LANES, TQ), BF16), pltpu.VMEM((N_HEADS, nbp, TQ), F32)]
        + _attn_scratch(N_HEADS),
        compiler_params=_cparams(2),
        name="moba",
    )(bq, bk, bvt, kbar)


def _diff_kernel(q_ref, k_ref, vt_ref, lam_ref, norm_ref, misc_ref, o_ref,
                 qt_ref, s_ref, m_ref, acc_ref, ot_ref):
    i = pl.program_id(1)
    kpos = lax.broadcasted_iota(I32, (CK, TQ), 0)
    qpos = lax.broadcasted_iota(I32, (CK, TQ), 1)
    _masked_qt(q_ref[...].astype(F32) * (DIFF_DIM ** -0.5 * LOG2E), 5, 2 * N_HEADS, qt_ref)

    dl = lam_ref[...]
    lam_init = misc_ref[0:1, 0:1]
    lam = (jnp.exp(jnp.sum(dl[0:1, :] * dl[1:2, :], axis=1, keepdims=True))
           - jnp.exp(jnp.sum(dl[2:3, :] * dl[3:4, :], axis=1, keepdims=True)) + lam_init)

    def qk_all(c):
        kc = k_ref[pl.ds(pl.multiple_of(c * CK, CK), CK), :]
        return [jnp.dot(_half(kc, j, 5), qt_ref[j], preferred_element_type=F32) for j in range(2 * N_HEADS)]

    _flash_loop(i, qk_all,
                lambda ctx, j, s: jnp.where(kpos <= qpos, s, NEG) if ctx[1] else s,
                lambda c, j: vt_ref[c, (j // 2) * VROWS:(j // 2 + 1) * VROWS, :],
                (s_ref, m_ref, acc_ref))

    post = norm_ref[...] * (1.0 - lam_init)
    for h in range(N_HEADS):
        o_h = _softmax_out(acc_ref.at[2 * h]) - lam * _softmax_out(acc_ref.at[2 * h + 1])
        ms = jnp.mean(o_h * o_h, axis=0, keepdims=True)
        ot_ref[h * HEAD_DIM:(h + 1) * HEAD_DIM, :] = o_h * lax.rsqrt(ms + RMS_EPS) * post
    o_ref[...] = ot_ref[...].T.astype(o_ref.dtype)


def _diff(cq, ck, cvt, lam, norm, misc):
    b, t, w = cq.shape
    qspec = pl.BlockSpec((None, TQ, w), lambda bb, i: (bb, i, 0))
    kspec, vspec = _kv_specs(t, w)
    full = lambda a: pl.BlockSpec(a.shape, lambda bb, i: (0,) * a.ndim)
    return pl.pallas_call(
        _diff_kernel,
        out_shape=jax.ShapeDtypeStruct((b, t, w), BF16),
        grid=(b, t // TQ),
        in_specs=[qspec, kspec, vspec, full(lam), full(norm), full(misc)],
        out_specs=qspec,
        scratch_shapes=[pltpu.VMEM((2 * N_HEADS, LANES, TQ), BF16)] + _attn_scratch(2 * N_HEADS),
        compiler_params=_cparams(2),
        name="diff",
    )(cq, ck, cvt, lam, norm, misc)


def _mla_prep_kernel(cq_ref, ckv_ref, kr_ref, qn_ref, kvn_ref, wq_ref, wqr_ref, wk_ref, wvt_ref,
                     p_ref, ct_ref, st_ref, q_out, k_out, vt_out):
    x = cq_ref[...].astype(F32)
    xn = (x * lax.rsqrt(jnp.mean(x * x, axis=1, keepdims=True) + RMS_EPS) * qn_ref[...]).astype(BF16)
    q = (jnp.dot(xn, wq_ref[...], preferred_element_type=F32) * ct_ref[...]
         + jnp.dot(xn, wqr_ref[...], preferred_element_type=F32) * st_ref[...])
    q_out[...] = q.astype(q_out.dtype)
    c = ckv_ref[:, :KV_LORA].astype(F32)
    cn = (c * lax.rsqrt(jnp.mean(c * c, axis=1, keepdims=True) + RMS_EPS) * kvn_ref[...]).astype(BF16)
    k = (jnp.dot(cn, wk_ref[...], preferred_element_type=F32)
         + jnp.dot(kr_ref[...], p_ref[...], preferred_element_type=F32))
    k_out[...] = k.astype(k_out.dtype)
    _store_vt(vt_out, _tn_dot(wvt_ref[...], cn))


def _mla_prep(dcq, ckv, kr, qn, kvn, wq, wqr, wk, wvt, pmat, ct, st, l):
    b, t, _ = dcq.shape
    tm = 512
    hw = N_HEADS * LANES
    row = lambda w: pl.BlockSpec((None, tm, w), lambda i, bb: (bb, i, 0))
    full = lambda a: pl.BlockSpec(a.shape, lambda i, bb: (0,) * a.ndim)
    tab = pl.BlockSpec((tm, hw), lambda i, bb: (i, 0))
    return pl.pallas_call(
        _mla_prep_kernel,
        out_shape=(jax.ShapeDtypeStruct((b, t, hw), BF16), jax.ShapeDtypeStruct((b, t, hw), BF16),
                   jax.ShapeDtypeStruct((b, t // CK, N_HEADS * VROWS, CK), BF16)),
        grid=(t // tm, b),
        in_specs=[row(Q_LORA), row(MXU_N), row(MXU_N), full(qn), full(kvn), _layer_spec(wq, l), _layer_spec(wqr, l),
                  _layer_spec(wk, l), _layer_spec(wvt, l), full(pmat), tab, tab],
        out_specs=(row(hw), row(hw),
                   pl.BlockSpec((None, tm // CK, N_HEADS * VROWS, CK), lambda i, bb: (bb, i, 0, 0))),
        compiler_params=_cparams(2),
        name="mla_prep",
    )(dcq, ckv, kr, qn, kvn, wq, wqr, wk, wvt, pmat, ct, st)


def _mla_kernel(q_ref, k_ref, vt_ref, o_ref, qt_ref, s_ref, m_ref, acc_ref, ot_ref):
    i = pl.program_id(1)
    kpos = lax.broadcasted_iota(I32, (CK, TQ), 0)
    qpos = lax.broadcasted_iota(I32, (CK, TQ), 1)
    hs = [slice(h * LANES, (h + 1) * LANES) for h in range(N_HEADS)]
    for h in range(N_HEADS):
        qt_ref[h] = q_ref[:, hs[h]].astype(F32).T.astype(BF16)

    def qk_all(c):
        start = pl.multiple_of(c * CK, CK)
        return [jnp.dot(k_ref[pl.ds(start, CK), hs[h]], qt_ref[h], preferred_element_type=F32)
                for h in range(N_HEADS)]

    _flash_loop(i, qk_all,
                lambda ctx, h, s: jnp.where(kpos <= qpos, s, NEG) if ctx[1] else s,
                lambda c, h: vt_ref[c, h * VROWS:(h + 1) * VROWS, :],
                (s_ref, m_ref, acc_ref))
    for h in range(N_HEADS):
        ot_ref[h * HEAD_DIM:(h + 1) * HEAD_DIM, :] = _softmax_out(acc_ref.at[h])
    o_ref[...] = ot_ref[...].T.astype(o_ref.dtype)


def _mla(qm, km, vmt):
    b, t, hw = qm.shape
    kspec, vspec = _kv_specs(t, hw)
    return pl.pallas_call(
        _mla_kernel,
        out_shape=jax.ShapeDtypeStruct((b, t, BRANCH_W), BF16),
        grid=(b, t // TQ),
        in_specs=[pl.BlockSpec((None, TQ, hw), lambda bb, i: (bb, i, 0)), kspec, vspec],
        out_specs=pl.BlockSpec((None, TQ, BRANCH_W), lambda bb, i: (bb, i, 0)),
        scratch_shapes=[pltpu.VMEM((N_HEADS, LANES, TQ), BF16)] + _attn_scratch(N_HEADS),
        compiler_params=_cparams(2),
        name="mla",
    )(qm, km, vmt)


def _matmul_kernel(x_ref, w_ref, o_ref):
    o_ref[...] = jnp.dot(x_ref[...].astype(BF16), w_ref[...], preferred_element_type=F32).astype(o_ref.dtype)


def _mem_kv(mem, w, l):
    b, m, d = mem.shape
    n = w.shape[-1]
    return pl.pallas_call(
        _matmul_kernel,
        out_shape=jax.ShapeDtypeStruct((b, m, n), BF16),
        grid=(b,),
        in_specs=[pl.BlockSpec((None, m, d), lambda bb: (bb, 0, 0)), _layer_spec(w, l)],
        out_specs=pl.BlockSpec((None, m, n), lambda bb: (bb, 0, 0)),
        compiler_params=_cparams(1),
        name="mem_kv",
    )(mem, w)


def _mem_kernel(q_ref, kv_ref, o_ref):
    tq = q_ref.shape[0]
    lane_q = lax.broadcasted_iota(I32, (tq, BRANCH_W), 1)
    q = q_ref[...].astype(F32) * (HEAD_DIM ** -0.5)
    mk = kv_ref[:, :BRANCH_W]
    mv = kv_ref[:, BRANCH_W:]
    out = jnp.zeros((tq, BRANCH_W), F32)
    for h in range(N_HEADS):
        in_h = (lane_q >> 6) == h
        s = _nt_dot(jnp.where(in_h, q, 0.0).astype(BF16), mk)
        p = jnp.exp(s - jnp.max(s, axis=1, keepdims=True))
        o_h = jnp.dot(p.astype(BF16), mv, preferred_element_type=F32) / jnp.sum(p, axis=1, keepdims=True)
        out = jnp.where(in_h, o_h, out)
    o_ref[...] = out.astype(o_ref.dtype)


def _mem_attn(eq, mkv):
    b, t, w = eq.shape
    m = mkv.shape[1]
    tq = 512
    return pl.pallas_call(
        _mem_kernel,
        out_shape=jax.ShapeDtypeStruct((b, t, w), BF16),
        grid=(b, t // tq),
        in_specs=[pl.BlockSpec((None, tq, w), lambda bb, i: (bb, i, 0)),
                  pl.BlockSpec((None, m, 2 * w), lambda bb, i: (bb, 0, 0))],
        out_specs=pl.BlockSpec((None, tq, w), lambda bb, i: (bb, i, 0)),
        compiler_params=_cparams(2),
        name="mem_attn",
    )(eq, mkv)


def _final_kernel(h_ref, hb_ref, oa_ref, ob_ref, oc_ref, od_ref, oe_ref, z_ref,
                  wg_ref, wb_ref, wo_ref, g_ref, b_ref, h_out, hb_out, *, alpha):
    d = h_ref.shape[1]
    acc = jnp.zeros(h_ref.shape, F32)
    for n, o_ref in enumerate((oa_ref, ob_ref, oc_ref, od_ref, oe_ref)):
        z = z_ref[:, n * BRANCH_W:(n + 1) * BRANCH_W].astype(F32)
        y = o_ref[...].astype(F32) * (z / (1.0 + jnp.exp(-z)))
        u = jnp.dot(y.astype(BF16), wb_ref[n], preferred_element_type=F32)
        g = jnp.dot(hb_ref[...], wg_ref[:, n * d:(n + 1) * d], preferred_element_type=F32)
        acc = acc + u / (1.0 + jnp.exp(-g))
    out = jnp.dot(acc.astype(BF16), wo_ref[...], preferred_element_type=F32)
    x = alpha * h_ref[...] + out
    mu = jnp.mean(x, axis=1, keepdims=True)
    xc = x - mu
    var = jnp.mean(xc * xc, axis=1, keepdims=True)
    y = xc * lax.rsqrt(var + LN_EPS) * g_ref[...] + b_ref[...]
    h_out[...] = y
    hb_out[...] = y.astype(BF16)


def _final(h, hb, os5, z, wg, wb, wo, ln_g, ln_b, alpha, l):
    n, d = h.shape
    tm = 256
    row = lambda w: pl.BlockSpec((tm, w), lambda i: (i, 0))
    full = lambda a: pl.BlockSpec(a.shape, lambda i: (0,) * a.ndim)
    return pl.pallas_call(
        functools.partial(_final_kernel, alpha=alpha),
        out_shape=(jax.ShapeDtypeStruct((n, d), F32), jax.ShapeDtypeStruct((n, d), BF16)),
        grid=(n // tm,),
        in_specs=[row(d), row(d)] + [row(BRANCH_W)] * N_BRANCH + [row(N_BRANCH * BRANCH_W),
                  _layer_spec(wg, l), _layer_spec(wb, l), _layer_spec(wo, l), full(ln_g), full(ln_b)],
        out_specs=(row(d), row(d)),
        compiler_params=_cparams(1),
        name="merge_out_ln",
    )(h, hb, *os5, z, wg, wb, wo, ln_g, ln_b)


ROPE_GROUPS = (("a_q", N_HEADS, HEAD_DIM, ROT_64), ("a_k", N_HEADS, HEAD_DIM, ROT_64),
               ("i_q", IDX_HEADS, IDX_DIM, ROT_32), ("i_k", IDX_HEADS, IDX_DIM, ROT_32),
               ("b_q", N_HEADS, HEAD_DIM, ROT_64), ("b_k", N_HEADS, HEAD_DIM, ROT_64),
               ("c_q", 2 * N_HEADS, DIFF_DIM, ROT_32), ("c_k", 2 * N_HEADS, DIFF_DIM, ROT_32),
               ("d_kr", 1, MXU_N, MLA_ROPE))
PLAIN_COLS = ("d_cq", "d_ckv", "i_w", None, "e_q", "z")
VALUE_COLS = ("a_v", "b_v", "c_v")


def _weight_prep_kernel(w_ref, plain_ref, vt_ref, rope_ref, g_ref):
    rows = w_ref.shape[0]

    def put(ref, names):
        off = 0
        for name in names:
            if name is None:
                part = jnp.zeros((rows, -off % MXU_N), ref.dtype)
            else:
                o, s = OFF[name]
                part = w_ref[:, o:o + s].astype(ref.dtype)
            ref[:, off:off + part.shape[1]] = part
            off += part.shape[1]
        assert off == ref.shape[1]

    put(plain_ref, PLAIN_COLS)
    put(vt_ref, VALUE_COLS)
    rope_cols = []
    for name, nh, hd, _ in ROPE_GROUPS:
        copies = IDX_HEADS if name == "i_k" else 1
        rope_cols += [name] * copies + ([None] if copies * OFF[name][1] < MXU_N else [])
    put(rope_ref, rope_cols)
    put(g_ref, ("g",))


def _weight_prep(w_in):
    depth, d, n = w_in.shape
    tm = 64
    widths = (sum(OFF[c][1] for c in PLAIN_COLS if c) + MXU_N - KV_LORA - IDX_HEADS,
              len(VALUE_COLS) * BRANCH_W, len(ROPE_GROUPS) * MXU_N, OFF["g"][1])
    return pl.pallas_call(
        _weight_prep_kernel,
        out_shape=tuple(jax.ShapeDtypeStruct((depth, d, w), BF16) for w in widths),
        grid=(depth, d // tm),
        in_specs=[pl.BlockSpec((None, tm, n), lambda l, i: (l, i, 0))],
        out_specs=tuple(pl.BlockSpec((None, tm, w), lambda l, i: (l, i, 0)) for w in widths),
        compiler_params=_cparams(2),
        name="weight_prep",
    )(w_in)


def _rope_tables(seq, rot_dim):
    pos = jnp.arange(seq, dtype=F32)
    inv = ROPE_THETA ** (-jnp.arange(0, rot_dim, 2, dtype=F32) / rot_dim)
    ang = pos[:, None] * inv[None, :]
    return jnp.cos(ang), jnp.sin(ang)


def _rope_cs(t, nh, hd, r):
    cos, sin = _rope_tables(t, r)
    c = jnp.concatenate([cos, cos, jnp.ones((t, hd - r), F32)], axis=1)
    s = jnp.concatenate([-sin, sin, jnp.zeros((t, hd - r), F32)], axis=1)
    return jnp.tile(c, (1, nh)), jnp.tile(s, (1, nh))


def kernel(x, mem, ln0_g, ln0_b, w_in, mla_q_norm, w_uq, mla_kv_norm, w_ukv, diff_lam, diff_norm,
           w_mem_kv, w_branch, w_out, ln_g, ln_b):
    b, t, d = x.shape
    depth = w_in.shape[0]
    alpha = (2 * depth) ** 0.25
    assert t % 512 == 0 and d == 1024

    w_plain, w_vt, w_rope, wg = _weight_prep(w_in)
    plain_widths = (BRANCH_W,) * 3 + (N_BRANCH * BRANCH_W,)
    rope_heads = tuple((hd, r // 2) for _, _, hd, r in ROPE_GROUPS)
    patterns = sorted(set((nh, hd, r) for _, nh, hd, r in ROPE_GROUPS))
    rope_tables = tuple(patterns.index((nh, hd, r)) for _, nh, hd, r in ROPE_GROUPS)
    cs = [_rope_cs(t, nh, hd, r) for nh, hd, r in patterns]
    ctab = jnp.stack([c for c, _ in cs])
    stab = jnp.stack([s for _, s in cs])

    uq = w_uq.reshape(depth, Q_LORA, N_HEADS, MLA_NOPE + MLA_ROPE)
    qn_w, qr_w = uq[..., :MLA_NOPE], uq[..., MLA_NOPE:]
    pad32 = jnp.zeros((depth, Q_LORA, N_HEADS, LANES - MLA_NOPE - MLA_ROPE), w_uq.dtype)
    hw = N_HEADS * LANES
    wq = jnp.concatenate([qn_w, qr_w, pad32], axis=-1).reshape(depth, Q_LORA, hw).astype(BF16)
    half = MLA_ROPE // 2
    wq_rot = jnp.concatenate([jnp.zeros_like(qn_w), -qr_w[..., half:], qr_w[..., :half], pad32],
                             axis=-1).reshape(depth, Q_LORA, hw).astype(BF16)
    cos_m, sin_m = _rope_tables(t, MLA_ROPE)
    one = lambda n: jnp.ones((t, n), F32)
    zer = lambda n: jnp.zeros((t, n), F32)
    qs = (MLA_NOPE + MLA_ROPE) ** -0.5 * LOG2E
    ct_q = qs * jnp.tile(jnp.concatenate([one(MLA_NOPE), cos_m, cos_m, one(LANES - MLA_NOPE - MLA_ROPE)], axis=1), (1, N_HEADS))
    st_q = qs * jnp.tile(jnp.concatenate([zer(MLA_NOPE), sin_m, sin_m, zer(LANES - MLA_NOPE - MLA_ROPE)], axis=1), (1, N_HEADS))
    ukv = w_ukv.reshape(depth, KV_LORA, N_HEADS, MLA_NOPE + MLA_V)
    wk = jnp.concatenate([ukv[..., :MLA_NOPE], jnp.zeros((depth, KV_LORA, N_HEADS, LANES - MLA_NOPE), w_ukv.dtype)],
                         axis=-1).reshape(depth, KV_LORA, hw).astype(BF16)
    wvt = ukv[..., MLA_NOPE:].reshape(depth, KV_LORA, N_HEADS * MLA_V).astype(BF16)
    place = np.zeros((MXU_N, hw), np.float32)
    for hh in range(N_HEADS):
        for j in range(MLA_ROPE):
            place[j, hh * LANES + MLA_NOPE + j] = 1.0
    place = jnp.asarray(place, BF16)

    wb = w_branch.astype(BF16)
    wo = w_out.astype(BF16)
    wmem = w_mem_kv.astype(BF16)
    norm_t = jnp.broadcast_to(diff_norm.astype(F32)[:, :, None], (depth, HEAD_DIM, TQ))

    h, hb = _layer_norm0(x.reshape(b * t, d), ln0_g, ln0_b)
    for l in range(depth):
        hb3 = hb.reshape(b, t, d)
        avt, bvt, cvt, dcq, ckv_iw, eq, z = _proj_plain(hb3, w_plain, w_vt, plain_widths, l)
        aq, ak, iq, ik, bq, bk, cq, ck, kr = _proj_rope(hb3, w_rope, ctab, stab, rope_heads, rope_tables, l)

        o_a = _dsa(aq, ak, avt, iq, ik, ckv_iw)
        o_b = _moba(bq, bk, bvt, _kbar(bk))
        lam_init = 0.8 - 0.6 * math.exp(-0.3 * l)
        misc = jnp.full((SUBLANES, LANES), lam_init, F32)
        o_c = _diff(cq, ck, cvt, diff_lam[l].astype(F32), norm_t[l], misc)
        qm, km, vmt = _mla_prep(dcq, ckv_iw, kr, mla_q_norm[l].reshape(1, Q_LORA), mla_kv_norm[l].reshape(1, KV_LORA),
                                wq, wq_rot, wk, wvt, place, ct_q, st_q, l)
        o_d = _mla(qm, km, vmt)
        o_e = _mem_attn(eq, _mem_kv(mem, wmem, l))

        os5 = [o.reshape(b * t, BRANCH_W) for o in (o_a, o_b, o_c, o_d, o_e)]
        h, hb = _final(h, hb, os5, z.reshape(b * t, N_BRANCH * BRANCH_W), wg, wb, wo,
                       ln_g[l].reshape(1, d), ln_b[l].reshape(1, d), alpha, l)
    return h.reshape(b, t, d)
```

```python
import functools
import math

import numpy as np
import jax
import jax.numpy as jnp
from jax import lax
from jax.experimental import pallas as pl
from jax.experimental.pallas import tpu as pltpu

F32 = jnp.float32
BF16 = jnp.bfloat16
I32 = jnp.int32
I16 = jnp.int16

N_HEADS = 4
HEAD_DIM = 64
BRANCH_W = N_HEADS * HEAD_DIM
N_BRANCH = 5
ROPE_THETA = 500000.0
ROT_64 = 16
ROT_32 = 8
IDX_HEADS = 8
IDX_DIM = 32
TOPK_MAX = 256
MOBA_BLOCK = 256
MOBA_TOPK = 3
DIFF_DIM = 32
Q_LORA = 256
KV_LORA = 128
MLA_NOPE = 64
MLA_ROPE = 32
MLA_V = 64
LN_EPS = 1e-5
RMS_EPS = 1e-6

IN_LAYOUT = (
    ("a_q", BRANCH_W), ("a_k", BRANCH_W), ("a_v", BRANCH_W),
    ("i_q", IDX_HEADS * IDX_DIM), ("i_k", IDX_DIM), ("i_w", IDX_HEADS),
    ("b_q", BRANCH_W), ("b_k", BRANCH_W), ("b_v", BRANCH_W),
    ("c_q", BRANCH_W), ("c_k", BRANCH_W), ("c_v", BRANCH_W),
    ("d_cq", Q_LORA), ("d_ckv", KV_LORA), ("d_kr", MLA_ROPE),
    ("e_q", BRANCH_W),
    ("z", N_BRANCH * BRANCH_W),
    ("g", N_BRANCH * 1024),
)

SUBLANES = 8
LANES = 128
MXU_N = 256
TQ = 512
CK = 256
VROWS = HEAD_DIM + 16
FLASH_UNROLL = 4
NEG = -1e30
LOG2E = math.log2(math.e)
INT_MIN = np.int32(-2 ** 31)
HALF16 = 1 << 15
VMEM_LIMIT = 56 * 1024 * 1024


def _offsets():
    off, out = 0, {}
    for name, size in IN_LAYOUT:
        out[name] = (off, size)
        off += size
    return out


OFF = _offsets()


def _nt_dot(a, b):
    return lax.dot_general(a, b, (((1,), (1,)), ((), ())), preferred_element_type=F32)


def _tn_dot(w, x):
    return lax.dot_general(w, x, (((0,), (1,)), ((), ())), preferred_element_type=F32)


def _fold_rows(w, rows=SUBLANES):
    xs = [w[r:r + rows, :] for r in range(0, w.shape[0], rows)]
    while len(xs) > 1:
        xs = [xs[j] + xs[j + 1] for j in range(0, len(xs) - 1, 2)] + ([xs[-1]] if len(xs) % 2 else [])
    return xs[0]


def _masked_qt(q, shift, n, qt_ref):
    qt = q.T
    dim = lax.broadcasted_iota(I32, (LANES, qt.shape[1]), 0)
    for j in range(n):
        half = (j << shift) // LANES
        rows = qt[half * LANES:(half + 1) * LANES, :]
        qt_ref[j] = jnp.where(((dim + half * LANES) >> shift) == j, rows, 0.0).astype(BF16)


def _half(kc, j, shift):
    half = (j << shift) // LANES
    return kc[:, half * LANES:(half + 1) * LANES]


def _cparams(n_axes):
    return pltpu.CompilerParams(dimension_semantics=("arbitrary",) * n_axes,
                                vmem_limit_bytes=VMEM_LIMIT)


def _layer_spec(a, l):
    return pl.BlockSpec((None,) + a.shape[1:], lambda *_: (l,) + (0,) * (a.ndim - 1))


def _softmax_step(s_t, vt_h, m_ref, acc_ref):
    m_old = m_ref[...]
    m_new = jnp.maximum(m_old, jnp.max(s_t, axis=0, keepdims=True))
    alpha = jnp.exp2(m_old - m_new)
    p = jnp.exp2(s_t - m_new)
    acc_ref[...] = alpha * acc_ref[...] + jnp.dot(vt_h, p.astype(BF16), preferred_element_type=F32)
    m_ref[...] = m_new


def _softmax_init(m_ref, acc_ref):
    m_ref[...] = jnp.full(m_ref.shape, NEG, F32)
    acc_ref[...] = jnp.zeros(acc_ref.shape, F32)


def _softmax_out(acc_ref):
    return acc_ref[:HEAD_DIM, :] / acc_ref[HEAD_DIM:HEAD_DIM + 1, :]


def _store_vt(o_ref, vt):
    ones = jnp.ones((VROWS - HEAD_DIM, CK), o_ref.dtype)
    for j in range(o_ref.shape[0]):
        for h in range(N_HEADS):
            o_ref[j, h * VROWS:h * VROWS + HEAD_DIM, :] = (
                vt[h * HEAD_DIM:(h + 1) * HEAD_DIM, j * CK:(j + 1) * CK].astype(o_ref.dtype))
            o_ref[j, h * VROWS + HEAD_DIM:(h + 1) * VROWS, :] = ones


def _flash_loop(n_full, qk_all, mask, vt_rows, state, prep=None):
    s_ref, m_ref, acc_ref = state
    n_state = m_ref.shape[0]
    for j in range(n_state):
        _softmax_init(m_ref.at[j], acc_ref.at[j])

    def park(c, slot):
        for j, s in enumerate(qk_all(c)):
            s_ref[slot, j] = s

    def consume(c, slot, d):
        ctx = (c, d) if prep is None else prep(c, d)
        for j in range(n_state):
            _softmax_step(mask(ctx, j, s_ref[slot, j]), vt_rows(c, j),
                          m_ref.at[j], acc_ref.at[j])

    park(0, 0)

    def pair(c):
        park(c + 1, 1)
        consume(c, 0, None)
        park(c + 2, 0)
        consume(c + 1, 1, None)

    def body(g, carry):
        for u in range(0, FLASH_UNROLL, 2):
            pair(FLASH_UNROLL * g + u)
        return carry

    n_group = lax.shift_right_logical(n_full, FLASH_UNROLL.bit_length() - 1)
    lax.fori_loop(0, n_group, body, 0)
    c0 = FLASH_UNROLL * n_group
    for u in range(FLASH_UNROLL // 2 - 1):
        @pl.when(n_full - c0 >= 2 * (u + 1))
        def _(u=u):
            pair(c0 + 2 * u)
    park(n_full + 1, 1)
    consume(n_full, 0, 0)
    consume(n_full + 1, 1, 1)


def _causal(d):
    kpos = lax.broadcasted_iota(I32, (CK, TQ), 0) + d * CK
    return kpos <= lax.broadcasted_iota(I32, (CK, TQ), 1)


def _attn_scratch(n_state):
    return [pltpu.VMEM((2, n_state, CK, TQ), F32), pltpu.VMEM((n_state, 1, TQ), F32),
            pltpu.VMEM((n_state, VROWS, TQ), F32), pltpu.VMEM((BRANCH_W, TQ), F32)]


def _kv_specs(t, w):
    kspec = pl.BlockSpec((None, t, w), lambda bb, i: (bb, 0, 0))
    vspec = pl.BlockSpec((None, t // CK, N_HEADS * VROWS, CK), lambda bb, i: (bb, 0, 0, 0))
    return kspec, vspec


def _ln_kernel(x_ref, g_ref, b_ref, h_ref, hb_ref):
    x = x_ref[...]
    mu = jnp.mean(x, axis=1, keepdims=True)
    xc = x - mu
    var = jnp.mean(xc * xc, axis=1, keepdims=True)
    y = xc * lax.rsqrt(var + LN_EPS) * g_ref[...] + b_ref[...]
    h_ref[...] = y
    hb_ref[...] = y.astype(BF16)


def _layer_norm0(x2, g, b):
    n, d = x2.shape
    tm = 512
    row = pl.BlockSpec((tm, d), lambda i: (i, 0))
    vec = pl.BlockSpec((1, d), lambda i: (0, 0))
    return pl.pallas_call(
        _ln_kernel,
        out_shape=(jax.ShapeDtypeStruct((n, d), F32), jax.ShapeDtypeStruct((n, d), BF16)),
        grid=(n // tm,),
        in_specs=[row, vec, vec],
        out_specs=(row, row),
        compiler_params=_cparams(1),
        name="ln0",
    )(x2, g.reshape(1, d), b.reshape(1, d))


def _proj_plain_kernel(x_ref, w_ref, wt_ref, *out_refs, n_t):
    for g, o_ref in enumerate(out_refs[:n_t]):
        _store_vt(o_ref, _tn_dot(wt_ref[:, g * BRANCH_W:(g + 1) * BRANCH_W], x_ref[...]))
    off = 0
    for o_ref in out_refs[n_t:]:
        wd = o_ref.shape[-1]
        for j in range(0, wd, MXU_N):
            acc = jnp.dot(x_ref[...], w_ref[:, off + j:off + j + MXU_N], preferred_element_type=F32)
            o_ref[:, j:j + MXU_N] = acc.astype(o_ref.dtype)
        off += wd


def _proj_plain(hb3, w, wt, widths, l):
    b, t, d = hb3.shape
    tm = 512
    n_t = wt.shape[-1] // BRANCH_W
    shapes = [jax.ShapeDtypeStruct((b, t // CK, N_HEADS * VROWS, CK), BF16)] * n_t
    specs = [pl.BlockSpec((None, tm // CK, N_HEADS * VROWS, CK), lambda i, bb: (bb, i, 0, 0))] * n_t
    shapes += [jax.ShapeDtypeStruct((b, t, wd), BF16) for wd in widths]
    specs += [pl.BlockSpec((None, tm, wd), lambda i, bb: (bb, i, 0)) for wd in widths]
    return pl.pallas_call(
        functools.partial(_proj_plain_kernel, n_t=n_t),
        out_shape=tuple(shapes),
        grid=(t // tm, b),
        in_specs=[pl.BlockSpec((None, tm, d), lambda i, bb: (bb, i, 0)),
                  _layer_spec(w, l), _layer_spec(wt, l)],
        out_specs=tuple(specs),
        compiler_params=_cparams(2),
        name="proj_plain",
    )(hb3, w, wt)


def _proj_rope_kernel(x_ref, w_ref, c_ref, s_ref, *out_refs, heads, tables):
    lane = lax.broadcasted_iota(I32, (x_ref.shape[0], MXU_N), 1)
    for g, o_ref in enumerate(out_refs):
        hd, half = heads[g]
        sl = slice(g * MXU_N, (g + 1) * MXU_N)
        acc = jnp.dot(x_ref[...], w_ref[:, sl], preferred_element_type=F32)
        partner = jnp.where((lane & (hd - 1)) < half,
                            pltpu.roll(acc, MXU_N - half, 1), pltpu.roll(acc, half, 1))
        o_ref[...] = (acc * c_ref[tables[g]] + partner * s_ref[tables[g]]).astype(o_ref.dtype)


def _proj_rope(hb3, w, ctab, stab, heads, tables, l):
    b, t, d = hb3.shape
    tm = 512
    assert w.shape[-1] == MXU_N * len(heads)
    tspec = pl.BlockSpec((ctab.shape[0], tm, MXU_N), lambda i, bb: (0, i, 0))
    ospec = pl.BlockSpec((None, tm, MXU_N), lambda i, bb: (bb, i, 0))
    return pl.pallas_call(
        functools.partial(_proj_rope_kernel, heads=heads, tables=tables),
        out_shape=(jax.ShapeDtypeStruct((b, t, MXU_N), BF16),) * len(heads),
        grid=(t // tm, b),
        in_specs=[pl.BlockSpec((None, tm, d), lambda i, bb: (bb, i, 0)),
                  _layer_spec(w, l), tspec, tspec],
        out_specs=(ospec,) * len(heads),
        compiler_params=_cparams(2),
        name="proj_rope",
    )(hb3, w, ctab, stab)


def _dsa_kernel(aq_ref, ak_ref, avt_ref, iq_ref, ik_ref, iw_ref, pick_ref, tri_ref, o_ref,
                keys_ref, hi_ref, lo_ref, iqt_ref, aqt_ref, wt_ref, thr_ref, s_ref, m_ref, acc_ref, ot_ref,
                *, topk, idx_scale):
    i = pl.program_id(1)
    n_full = 2 * i
    n_pair = i + 1

    _masked_qt(iq_ref[...].astype(F32), 5, IDX_HEADS, iqt_ref)
    _masked_qt(aq_ref[...].astype(F32) * (HEAD_DIM ** -0.5 * LOG2E), 6, N_HEADS, aqt_ref)
    wt_ref[...] = _nt_dot(pick_ref[...], iw_ref[...]) * idx_scale

    def logits(c):
        kc = ik_ref[pl.ds(pl.multiple_of(c * CK, CK), CK), :]
        return [jnp.dot(_half(kc, hh, 5), iqt_ref[hh], preferred_element_type=F32) for hh in range(IDX_HEADS)]

    def put_keys(c, key):
        keys_ref[c] = key
        hi_ref[c] = (key >> 16).astype(I16)
        lo_ref[c] = ((key & 0xFFFF) - HALF16).astype(I16)

    def score_chunk(c, lg, d):
        sc = jnp.zeros((CK, TQ), F32)
        for hh in range(IDX_HEADS):
            sc = sc + jnp.maximum(lg[hh], 0.0) * wt_ref[hh:hh + 1, :]
        bits = pltpu.bitcast(sc, I32)
        key = jnp.where(bits < 0, INT_MIN - bits, bits)
        put_keys(c, key if d is None else jnp.where(_causal(d), key, INT_MIN))

    def score_pair(c, d0, d1):
        lg0, lg1 = logits(c), logits(c + 1)
        score_chunk(c, lg0, d0)
        score_chunk(c + 1, lg1, d1)

    def score_body(p, carry):
        score_pair(2 * p, None, None)
        return carry

    lax.fori_loop(0, i, score_body, 0)
    score_pair(n_full, 0, 1)

    def pair_loop(body, init):
        def pair(p, carry):
            return body(2 * p + 1, body(2 * p, carry))
        return lax.fori_loop(0, n_pair, pair, init)

    def count16(ref, pred):
        def body(c, part):
            return part + _fold_rows(jnp.where(pred(ref[c]), jnp.int16(1), jnp.int16(0)), 2 * SUBLANES)
        part = pair_loop(body, jnp.zeros((2 * SUBLANES, TQ), I16))
        return jnp.sum(part.astype(F32), axis=0, keepdims=True)

    def search16(ref, need):
        def bit_body(bi, t_u):
            c_u = t_u | jnp.left_shift(jnp.int32(1), 15 - bi)
            ck = (c_u - HALF16).astype(I16)
            cnt = count16(ref, lambda v: v >= ck)
            return jnp.where(cnt >= need, c_u, t_u)
        return lax.fori_loop(0, 16, bit_body, jnp.zeros((1, TQ), I32))

    hi_u = search16(hi_ref, float(topk))
    thr_hi = (hi_u - HALF16).astype(I16)
    n_above = count16(hi_ref, lambda v: v > thr_hi)

    def bucket_body(c, carry):
        hi_ref[c] = jnp.where(hi_ref[c] == thr_hi, lo_ref[c], jnp.int16(-HALF16))
        return carry

    pair_loop(bucket_body, 0)
    lo_u = search16(hi_ref, float(topk) - n_above)
    thr = ((hi_u - HALF16) << 16) | lo_u

    def tie_body(c, carry):
        k = keys_ref[c]
        return (carry[0] + _fold_rows(jnp.where(k > thr, 1.0, 0.0)),
                carry[1] + _fold_rows(jnp.where(k == thr, 1.0, 0.0)))

    zero8 = jnp.zeros((SUBLANES, TQ), F32)
    gt8, eq8 = pair_loop(tie_body, (zero8, zero8))
    need = float(topk) - jnp.sum(gt8, axis=0, keepdims=True)
    amb = jnp.logical_and(jnp.sum(eq8, axis=0, keepdims=True) > need, thr > INT_MIN)
    any_amb = jnp.max(jnp.where(amb, 1.0, 0.0)) > 0.5

    @pl.when(any_amb)
    def _():
        def drop_body(c, seen):
            k = keys_ref[c]
            eq = k == thr
            eqf = jnp.where(eq, 1.0, 0.0)
            rank = jnp.dot(tri_ref[...], eqf.astype(BF16), preferred_element_type=F32) + seen
            drop = jnp.logical_and(jnp.logical_and(eq, rank > need), amb)
            keys_ref[c] = jnp.where(drop, INT_MIN, k)
            return seen + jnp.sum(eqf, axis=0, keepdims=True)

        pair_loop(drop_body, jnp.zeros((1, TQ), F32))

    thr_ref[...] = jnp.maximum(thr, INT_MIN + 1)

    def qk_all(c):
        kc = ak_ref[pl.ds(pl.multiple_of(c * CK, CK), CK), :]
        return [jnp.dot(_half(kc, h, 6), aqt_ref[h], preferred_element_type=F32) for h in range(N_HEADS)]

    _flash_loop(n_full, qk_all,
                lambda keep, h, s: jnp.where(keep, s, NEG),
                lambda c, h: avt_ref[c, h * VROWS:(h + 1) * VROWS, :],
                (s_ref, m_ref, acc_ref),
                prep=lambda c, d: keys_ref[c] >= thr_ref[...])
    for h in range(N_HEADS):
        ot_ref[h * HEAD_DIM:(h + 1) * HEAD_DIM, :] = _softmax_out(acc_ref.at[h])
    o_ref[...] = ot_ref[...].T.astype(o_ref.dtype)


def _dsa(aq, ak, avt, iq, ik, iw):
    b, t, _ = aq.shape
    topk = min(TOPK_MAX, t // 4)
    qspec = pl.BlockSpec((None, TQ, BRANCH_W), lambda bb, i: (bb, i, 0))
    kspec, vspec = _kv_specs(t, BRANCH_W)
    pick = np.zeros((2 * SUBLANES, MXU_N), np.float32)
    for hh in range(IDX_HEADS):
        pick[hh, KV_LORA + hh] = 1.0
    pick = jnp.asarray(pick, BF16)
    tri = jnp.asarray(np.tril(np.ones((CK, CK), np.float32)), BF16)
    kern = functools.partial(_dsa_kernel, topk=topk, idx_scale=(IDX_HEADS * IDX_DIM) ** -0.5)
    return pl.pallas_call(
        kern,
        out_shape=jax.ShapeDtypeStruct((b, t, BRANCH_W), BF16),
        grid=(b, t // TQ),
        in_specs=[qspec, kspec, vspec, qspec, kspec, qspec,
                  pl.BlockSpec(pick.shape, lambda bb, i: (0, 0)), pl.BlockSpec(tri.shape, lambda bb, i: (0, 0))],
        out_specs=qspec,
        scratch_shapes=[
            pltpu.VMEM((t // CK, CK, TQ), I32),
            pltpu.VMEM((t // CK, CK, TQ), I16),
            pltpu.VMEM((t // CK, CK, TQ), I16),
            pltpu.VMEM((IDX_HEADS, LANES, TQ), BF16),
            pltpu.VMEM((N_HEADS, LANES, TQ), BF16),
            pltpu.VMEM((2 * SUBLANES, TQ), F32),
            pltpu.VMEM((1, TQ), I32),
        ] + _attn_scratch(N_HEADS),
        compiler_params=_cparams(2),
        name="dsa",
    )(aq, ak, avt, iq, ik, iw, pick, tri)


def _kbar_kernel(k_ref, o_ref):
    o_ref[...] = jnp.zeros(o_ref.shape, o_ref.dtype)
    nb = k_ref.shape[0] // MOBA_BLOCK
    for n in range(nb):
        blk = k_ref[n * MOBA_BLOCK:(n + 1) * MOBA_BLOCK, :].astype(F32)
        o_ref[n:n + 1, :] = jnp.mean(blk, axis=0, keepdims=True).astype(o_ref.dtype)


def _kbar(bk):
    b, t, w = bk.shape
    nbp = max(2 * SUBLANES, t // MOBA_BLOCK)
    return pl.pallas_call(
        _kbar_kernel,
        out_shape=jax.ShapeDtypeStruct((b, nbp, w), BF16),
        grid=(b,),
        in_specs=[pl.BlockSpec((None, t, w), lambda bb: (bb, 0, 0))],
        out_specs=pl.BlockSpec((None, nbp, w), lambda bb: (bb, 0, 0)),
        compiler_params=_cparams(1),
        name="moba_kbar",
    )(bk)


def _moba_kernel(q_ref, k_ref, vt_ref, kbar_ref, o_ref, qt_ref, bias_ref, s_ref, m_ref, acc_ref, ot_ref):
    i = pl.program_id(1)
    nbp = kbar_ref.shape[0]
    blk = lax.broadcasted_iota(I32, (nbp, TQ), 0)
    blk_f = blk.astype(F32)
    own = 2 * i + (lax.broadcasted_iota(I32, (nbp, TQ), 1) >> (MOBA_BLOCK.bit_length() - 1))
    upper = lax.broadcasted_iota(I32, (1, TQ), 1) >= MOBA_BLOCK
    _masked_qt(q_ref[...].astype(F32) * (HEAD_DIM ** -0.5 * LOG2E), 6, N_HEADS, qt_ref)

    for h in range(N_HEADS):
        g = jnp.where(blk < own, jnp.dot(_half(kbar_ref[...], h, 6), qt_ref[h], preferred_element_type=F32), NEG)
        bias = jnp.full((nbp, TQ), NEG, F32)
        for _ in range(MOBA_TOPK):
            mx = jnp.max(g, axis=0, keepdims=True)
            first = jnp.min(jnp.where(g == mx, blk_f, 1e9), axis=0, keepdims=True)
            pick = jnp.logical_and(blk_f == first, mx > 0.5 * NEG)
            bias = jnp.where(pick, 0.0, bias)
            g = jnp.where(pick, NEG, g)
        bias_ref[h] = bias

    def qk_all(c):
        kc = k_ref[pl.ds(pl.multiple_of(c * CK, CK), CK), :]
        return [jnp.dot(_half(kc, h, 6), qt_ref[h], preferred_element_type=F32) for h in range(N_HEADS)]

    def mask(ctx, h, s):
        c, d = ctx
        if d is None:
            return s + bias_ref[h, pl.ds(c, 1), :]
        own_part = jnp.where(_causal(d), s, NEG)
        if d == 1:
            return own_part
        return jnp.where(upper, s + bias_ref[h, pl.ds(c, 1), :], own_part)

    _flash_loop(2 * i, qk_all, mask, lambda c, h: vt_ref[c, h * VROWS:(h + 1) * VROWS, :],
                (s_ref, m_ref, acc_ref))
    for h in range(N_HEADS):
        ot_ref[h * HEAD_DIM:(h + 1) * HEAD_DIM, :] = _softmax_out(acc_ref.at[h])
    o_ref[...] = ot_ref[...].T.astype(o_ref.dtype)


def _moba(bq, bk, bvt, kbar):
    b, t, w = bq.shape
    assert TQ == 2 * MOBA_BLOCK and CK == MOBA_BLOCK and t % TQ == 0
    nbp = kbar.shape[1]
    qspec = pl.BlockSpec((None, TQ, w), lambda bb, i: (bb, i, 0))
    kspec, vspec = _kv_specs(t, w)
    return pl.pallas_call(
        _moba_kernel,
        out_shape=jax.ShapeDtypeStruct((b, t, w), BF16),
        grid=(b, t // TQ),
        in_specs=[qspec, kspec, vspec, pl.BlockSpec((None, nbp, w), lambda bb, i: (bb, 0, 0))],
        out_specs=qspec,
        scratch_shapes=[pltpu.VMEM((N_HEADS, LANES, TQ), BF16), pltpu.VMEM((N_HEADS, nbp, TQ), F32)]
        + _attn_scratch(N_HEADS),
        compiler_params=_cparams(2),
        name="moba",
    )(bq, bk, bvt, kbar)


def _diff_kernel(q_ref, k_ref, vt_ref, lam_ref, norm_ref, misc_ref, o_ref,
                 qt_ref, s_ref, m_ref, acc_ref, ot_ref):
    i = pl.program_id(1)
    _masked_qt(q_ref[...].astype(F32) * (DIFF_DIM ** -0.5 * LOG2E), 5, 2 * N_HEADS, qt_ref)

    dl = lam_ref[...]
    lam_init = misc_ref[0:1, 0:1]
    lam = (jnp.exp(jnp.sum(dl[0:1, :] * dl[1:2, :], axis=1, keepdims=True))
           - jnp.exp(jnp.sum(dl[2:3, :] * dl[3:4, :], axis=1, keepdims=True)) + lam_init)

    def qk_all(c):
        kc = k_ref[pl.ds(pl.multiple_of(c * CK, CK), CK), :]
        return [jnp.dot(_half(kc, j, 5), qt_ref[j], preferred_element_type=F32) for j in range(2 * N_HEADS)]

    _flash_loop(2 * i, qk_all,
                lambda ctx, j, s: s if ctx[1] is None else jnp.where(_causal(ctx[1]), s, NEG),
                lambda c, j: vt_ref[c, (j // 2) * VROWS:(j // 2 + 1) * VROWS, :],
                (s_ref, m_ref, acc_ref))

    post = norm_ref[...] * (1.0 - lam_init)
    for h in range(N_HEADS):
        o_h = _softmax_out(acc_ref.at[2 * h]) - lam * _softmax_out(acc_ref.at[2 * h + 1])
        ms = jnp.mean(o_h * o_h, axis=0, keepdims=True)
        ot_ref[h * HEAD_DIM:(h + 1) * HEAD_DIM, :] = o_h * lax.rsqrt(ms + RMS_EPS) * post
    o_ref[...] = ot_ref[...].T.astype(o_ref.dtype)


def _diff(cq, ck, cvt, lam, norm, misc):
    b, t, w = cq.shape
    qspec = pl.BlockSpec((None, TQ, w), lambda bb, i: (bb, i, 0))
    kspec, vspec = _kv_specs(t, w)
    full = lambda a: pl.BlockSpec(a.shape, lambda bb, i: (0,) * a.ndim)
    return pl.pallas_call(
        _diff_kernel,
        out_shape=jax.ShapeDtypeStruct((b, t, w), BF16),
        grid=(b, t // TQ),
        in_specs=[qspec, kspec, vspec, full(lam), full(norm), full(misc)],
        out_specs=qspec,
        scratch_shapes=[pltpu.VMEM((2 * N_HEADS, LANES, TQ), BF16)] + _attn_scratch(2 * N_HEADS),
        compiler_params=_cparams(2),
        name="diff",
    )(cq, ck, cvt, lam, norm, misc)


def _mla_prep_kernel(cq_ref, ckv_ref, kr_ref, qn_ref, kvn_ref, wq_ref, wqr_ref, wk_ref, wvt_ref,
                     p_ref, ct_ref, st_ref, q_out, k_out, vt_out):
    x = cq_ref[...].astype(F32)
    xn = (x * lax.rsqrt(jnp.mean(x * x, axis=1, keepdims=True) + RMS_EPS) * qn_ref[...]).astype(BF16)
    q = (jnp.dot(xn, wq_ref[...], preferred_element_type=F32) * ct_ref[...]
         + jnp.dot(xn, wqr_ref[...], preferred_element_type=F32) * st_ref[...])
    q_out[...] = q.astype(q_out.dtype)
    c = ckv_ref[:, :KV_LORA].astype(F32)
    cn = (c * lax.rsqrt(jnp.mean(c * c, axis=1, keepdims=True) + RMS_EPS) * kvn_ref[...]).astype(BF16)
    k = (jnp.dot(cn, wk_ref[...], preferred_element_type=F32)
         + jnp.dot(kr_ref[...], p_ref[...], preferred_element_type=F32))
    k_out[...] = k.astype(k_out.dtype)
    _store_vt(vt_out, _tn_dot(wvt_ref[...], cn))


def _mla_prep(dcq, ckv, kr, qn, kvn, wq, wqr, wk, wvt, pmat, ct, st, l):
    b, t, _ = dcq.shape
    tm = 512
    hw = N_HEADS * LANES
    row = lambda w: pl.BlockSpec((None, tm, w), lambda i, bb: (bb, i, 0))
    full = lambda a: pl.BlockSpec(a.shape, lambda i, bb: (0,) * a.ndim)
    tab = pl.BlockSpec((tm, hw), lambda i, bb: (i, 0))
    return pl.pallas_call(
        _mla_prep_kernel,
        out_shape=(jax.ShapeDtypeStruct((b, t, hw), BF16), jax.ShapeDtypeStruct((b, t, hw), BF16),
                   jax.ShapeDtypeStruct((b, t // CK, N_HEADS * VROWS, CK), BF16)),
        grid=(t // tm, b),
        in_specs=[row(Q_LORA), row(MXU_N), row(MXU_N), full(qn), full(kvn), _layer_spec(wq, l), _layer_spec(wqr, l),
                  _layer_spec(wk, l), _layer_spec(wvt, l), full(pmat), tab, tab],
        out_specs=(row(hw), row(hw),
                   pl.BlockSpec((None, tm // CK, N_HEADS * VROWS, CK), lambda i, bb: (bb, i, 0, 0))),
        compiler_params=_cparams(2),
        name="mla_prep",
    )(dcq, ckv, kr, qn, kvn, wq, wqr, wk, wvt, pmat, ct, st)


def _mla_kernel(q_ref, k_ref, vt_ref, o_ref, qt_ref, s_ref, m_ref, acc_ref, ot_ref):
    i = pl.program_id(1)
    hs = [slice(h * LANES, (h + 1) * LANES) for h in range(N_HEADS)]
    for h in range(N_HEADS):
        qt_ref[h] = q_ref[:, hs[h]].astype(F32).T.astype(BF16)

    def qk_all(c):
        start = pl.multiple_of(c * CK, CK)
        return [jnp.dot(k_ref[pl.ds(start, CK), hs[h]], qt_ref[h], preferred_element_type=F32)
                for h in range(N_HEADS)]

    _flash_loop(2 * i, qk_all,
                lambda ctx, h, s: s if ctx[1] is None else jnp.where(_causal(ctx[1]), s, NEG),
                lambda c, h: vt_ref[c, h * VROWS:(h + 1) * VROWS, :],
                (s_ref, m_ref, acc_ref))
    for h in range(N_HEADS):
        ot_ref[h * HEAD_DIM:(h + 1) * HEAD_DIM, :] = _softmax_out(acc_ref.at[h])
    o_ref[...] = ot_ref[...].T.astype(o_ref.dtype)


def _mla(qm, km, vmt):
    b, t, hw = qm.shape
    kspec, vspec = _kv_specs(t, hw)
    return pl.pallas_call(
        _mla_kernel,
        out_shape=jax.ShapeDtypeStruct((b, t, BRANCH_W), BF16),
        grid=(b, t // TQ),
        in_specs=[pl.BlockSpec((None, TQ, hw), lambda bb, i: (bb, i, 0)), kspec, vspec],
        out_specs=pl.BlockSpec((None, TQ, BRANCH_W), lambda bb, i: (bb, i, 0)),
        scratch_shapes=[pltpu.VMEM((N_HEADS, LANES, TQ), BF16)] + _attn_scratch(N_HEADS),
        compiler_params=_cparams(2),
        name="mla",
    )(qm, km, vmt)


def _matmul_kernel(x_ref, w_ref, o_ref):
    o_ref[...] = jnp.dot(x_ref[...].astype(BF16), w_ref[...], preferred_element_type=F32).astype(o_ref.dtype)


def _mem_kv(mem, w, l):
    b, m, d = mem.shape
    n = w.shape[-1]
    return pl.pallas_call(
        _matmul_kernel,
        out_shape=jax.ShapeDtypeStruct((b, m, n), BF16),
        grid=(b,),
        in_specs=[pl.BlockSpec((None, m, d), lambda bb: (bb, 0, 0)), _layer_spec(w, l)],
        out_specs=pl.BlockSpec((None, m, n), lambda bb: (bb, 0, 0)),
        compiler_params=_cparams(1),
        name="mem_kv",
    )(mem, w)


def _mem_kernel(q_ref, kv_ref, o_ref):
    tq = q_ref.shape[0]
    lane_q = lax.broadcasted_iota(I32, (tq, BRANCH_W), 1)
    q = q_ref[...].astype(F32) * (HEAD_DIM ** -0.5)
    mk = kv_ref[:, :BRANCH_W]
    mv = kv_ref[:, BRANCH_W:]
    out = jnp.zeros((tq, BRANCH_W), F32)
    for h in range(N_HEADS):
        in_h = (lane_q >> 6) == h
        s = _nt_dot(jnp.where(in_h, q, 0.0).astype(BF16), mk)
        p = jnp.exp(s - jnp.max(s, axis=1, keepdims=True))
        o_h = jnp.dot(p.astype(BF16), mv, preferred_element_type=F32) / jnp.sum(p, axis=1, keepdims=True)
        out = jnp.where(in_h, o_h, out)
    o_ref[...] = out.astype(o_ref.dtype)


def _mem_attn(eq, mkv):
    b, t, w = eq.shape
    m = mkv.shape[1]
    tq = 512
    return pl.pallas_call(
        _mem_kernel,
        out_shape=jax.ShapeDtypeStruct((b, t, w), BF16),
        grid=(b, t // tq),
        in_specs=[pl.BlockSpec((None, tq, w), lambda bb, i: (bb, i, 0)),
                  pl.BlockSpec((None, m, 2 * w), lambda bb, i: (bb, 0, 0))],
        out_specs=pl.BlockSpec((None, tq, w), lambda bb, i: (bb, i, 0)),
        compiler_params=_cparams(2),
        name="mem_attn",
    )(eq, mkv)


def _final_kernel(h_ref, hb_ref, oa_ref, ob_ref, oc_ref, od_ref, oe_ref, z_ref,
                  wg_ref, wb_ref, wo_ref, g_ref, b_ref, h_out, hb_out, *, alpha):
    d = h_ref.shape[1]
    acc = jnp.zeros(h_ref.shape, F32)
    for n, o_ref in enumerate((oa_ref, ob_ref, oc_ref, od_ref, oe_ref)):
        z = z_ref[:, n * BRANCH_W:(n + 1) * BRANCH_W].astype(F32)
        y = o_ref[...].astype(F32) * (z / (1.0 + jnp.exp(-z)))
        u = jnp.dot(y.astype(BF16), wb_ref[n], preferred_element_type=F32)
        g = jnp.dot(hb_ref[...], wg_ref[:, n * d:(n + 1) * d], preferred_element_type=F32)
        acc = acc + u / (1.0 + jnp.exp(-g))
    out = jnp.dot(acc.astype(BF16), wo_ref[...], preferred_element_type=F32)
    x = alpha * h_ref[...] + out
    mu = jnp.mean(x, axis=1, keepdims=True)
    xc = x - mu
    var = jnp.mean(xc * xc, axis=1, keepdims=True)
    y = xc * lax.rsqrt(var + LN_EPS) * g_ref[...] + b_ref[...]
    h_out[...] = y
    hb_out[...] = y.astype(BF16)


def _final(h, hb, os5, z, wg, wb, wo, ln_g, ln_b, alpha, l):
    n, d = h.shape
    tm = 256
    row = lambda w: pl.BlockSpec((tm, w), lambda i: (i, 0))
    full = lambda a: pl.BlockSpec(a.shape, lambda i: (0,) * a.ndim)
    return pl.pallas_call(
        functools.partial(_final_kernel, alpha=alpha),
        out_shape=(jax.ShapeDtypeStruct((n, d), F32), jax.ShapeDtypeStruct((n, d), BF16)),
        grid=(n // tm,),
        in_specs=[row(d), row(d)] + [row(BRANCH_W)] * N_BRANCH + [row(N_BRANCH * BRANCH_W),
                  _layer_spec(wg, l), _layer_spec(wb, l), _layer_spec(wo, l), full(ln_g), full(ln_b)],
        out_specs=(row(d), row(d)),
        compiler_params=_cparams(1),
        name="merge_out_ln",
    )(h, hb, *os5, z, wg, wb, wo, ln_g, ln_b)


ROPE_GROUPS = (("a_q", N_HEADS, HEAD_DIM, ROT_64), ("a_k", N_HEADS, HEAD_DIM, ROT_64),
               ("i_q", IDX_HEADS, IDX_DIM, ROT_32), ("i_k", IDX_HEADS, IDX_DIM, ROT_32),
               ("b_q", N_HEADS, HEAD_DIM, ROT_64), ("b_k", N_HEADS, HEAD_DIM, ROT_64),
               ("c_q", 2 * N_HEADS, DIFF_DIM, ROT_32), ("c_k", 2 * N_HEADS, DIFF_DIM, ROT_32),
               ("d_kr", 1, MXU_N, MLA_ROPE))
PLAIN_COLS = ("d_cq", "d_ckv", "i_w", None, "e_q", "z")
VALUE_COLS = ("a_v", "b_v", "c_v")


def _weight_prep_kernel(w_ref, plain_ref, vt_ref, rope_ref, g_ref):
    rows = w_ref.shape[0]

    def put(ref, names):
        off = 0
        for name in names:
            if name is None:
                part = jnp.zeros((rows, -off % MXU_N), ref.dtype)
            else:
                o, s = OFF[name]
                part = w_ref[:, o:o + s].astype(ref.dtype)
            ref[:, off:off + part.shape[1]] = part
            off += part.shape[1]
        assert off == ref.shape[1]

    put(plain_ref, PLAIN_COLS)
    put(vt_ref, VALUE_COLS)
    rope_cols = []
    for name, nh, hd, _ in ROPE_GROUPS:
        copies = IDX_HEADS if name == "i_k" else 1
        rope_cols += [name] * copies + ([None] if copies * OFF[name][1] < MXU_N else [])
    put(rope_ref, rope_cols)
    put(g_ref, ("g",))


def _weight_prep(w_in):
    depth, d, n = w_in.shape
    tm = 64
    widths = (sum(OFF[c][1] for c in PLAIN_COLS if c) + MXU_N - KV_LORA - IDX_HEADS,
              len(VALUE_COLS) * BRANCH_W, len(ROPE_GROUPS) * MXU_N, OFF["g"][1])
    return pl.pallas_call(
        _weight_prep_kernel,
        out_shape=tuple(jax.ShapeDtypeStruct((depth, d, w), BF16) for w in widths),
        grid=(depth, d // tm),
        in_specs=[pl.BlockSpec((None, tm, n), lambda l, i: (l, i, 0))],
        out_specs=tuple(pl.BlockSpec((None, tm, w), lambda l, i: (l, i, 0)) for w in widths),
        compiler_params=_cparams(2),
        name="weight_prep",
    )(w_in)


def _rope_tables(seq, rot_dim):
    pos = jnp.arange(seq, dtype=F32)
    inv = ROPE_THETA ** (-jnp.arange(0, rot_dim, 2, dtype=F32) / rot_dim)
    ang = pos[:, None] * inv[None, :]
    return jnp.cos(ang), jnp.sin(ang)


def _rope_cs(t, nh, hd, r):
    cos, sin = _rope_tables(t, r)
    c = jnp.concatenate([cos, cos, jnp.ones((t, hd - r), F32)], axis=1)
    s = jnp.concatenate([-sin, sin, jnp.zeros((t, hd - r), F32)], axis=1)
    return jnp.tile(c, (1, nh)), jnp.tile(s, (1, nh))


def kernel(x, mem, ln0_g, ln0_b, w_in, mla_q_norm, w_uq, mla_kv_norm, w_ukv, diff_lam, diff_norm,
           w_mem_kv, w_branch, w_out, ln_g, ln_b):
    b, t, d = x.shape
    depth = w_in.shape[0]
    alpha = (2 * depth) ** 0.25
    assert t % 512 == 0 and d == 1024

    w_plain, w_vt, w_rope, wg = _weight_prep(w_in)
    plain_widths = (BRANCH_W,) * 3 + (N_BRANCH * BRANCH_W,)
    rope_heads = tuple((hd, r // 2) for _, _, hd, r in ROPE_GROUPS)
    patterns = sorted(set((nh, hd, r) for _, nh, hd, r in ROPE_GROUPS))
    rope_tables = tuple(patterns.index((nh, hd, r)) for _, nh, hd, r in ROPE_GROUPS)
    cs = [_rope_cs(t, nh, hd, r) for nh, hd, r in patterns]
    ctab = jnp.stack([c for c, _ in cs])
    stab = jnp.stack([s for _, s in cs])

    uq = w_uq.reshape(depth, Q_LORA, N_HEADS, MLA_NOPE + MLA_ROPE)
    qn_w, qr_w = uq[..., :MLA_NOPE], uq[..., MLA_NOPE:]
    pad32 = jnp.zeros((depth, Q_LORA, N_HEADS, LANES - MLA_NOPE - MLA_ROPE), w_uq.dtype)
    hw = N_HEADS * LANES
    wq = jnp.concatenate([qn_w, qr_w, pad32], axis=-1).reshape(depth, Q_LORA, hw).astype(BF16)
    half = MLA_ROPE // 2
    wq_rot = jnp.concatenate([jnp.zeros_like(qn_w), -qr_w[..., half:], qr_w[..., :half], pad32],
                             axis=-1).reshape(depth, Q_LORA, hw).astype(BF16)
    cos_m, sin_m = _rope_tables(t, MLA_ROPE)
    one = lambda n: jnp.ones((t, n), F32)
    zer = lambda n: jnp.zeros((t, n), F32)
    qs = (MLA_NOPE + MLA_ROPE) ** -0.5 * LOG2E
    ct_q = qs * jnp.tile(jnp.concatenate([one(MLA_NOPE), cos_m, cos_m, one(LANES - MLA_NOPE - MLA_ROPE)], axis=1), (1, N_HEADS))
    st_q = qs * jnp.tile(jnp.concatenate([zer(MLA_NOPE), sin_m, sin_m, zer(LANES - MLA_NOPE - MLA_ROPE)], axis=1), (1, N_HEADS))
    ukv = w_ukv.reshape(depth, KV_LORA, N_HEADS, MLA_NOPE + MLA_V)
    wk = jnp.concatenate([ukv[..., :MLA_NOPE], jnp.zeros((depth, KV_LORA, N_HEADS, LANES - MLA_NOPE), w_ukv.dtype)],
                         axis=-1).reshape(depth, KV_LORA, hw).astype(BF16)
    wvt = ukv[..., MLA_NOPE:].reshape(depth, KV_LORA, N_HEADS * MLA_V).astype(BF16)
    place = np.zeros((MXU_N, hw), np.float32)
    for hh in range(N_HEADS):
        for j in range(MLA_ROPE):
            place[j, hh * LANES + MLA_NOPE + j] = 1.0
    place = jnp.asarray(place, BF16)

    wb = w_branch.astype(BF16)
    wo = w_out.astype(BF16)
    wmem = w_mem_kv.astype(BF16)
    norm_t = jnp.broadcast_to(diff_norm.astype(F32)[:, :, None], (depth, HEAD_DIM, TQ))

    h, hb = _layer_norm0(x.reshape(b * t, d), ln0_g, ln0_b)
    for l in range(depth):
        hb3 = hb.reshape(b, t, d)
        avt, bvt, cvt, dcq, ckv_iw, eq, z = _proj_plain(hb3, w_plain, w_vt, plain_widths, l)
        aq, ak, iq, ik, bq, bk, cq, ck, kr = _proj_rope(hb3, w_rope, ctab, stab, rope_heads, rope_tables, l)

        o_a = _dsa(aq, ak, avt, iq, ik, ckv_iw)
        o_b = _moba(bq, bk, bvt, _kbar(bk))
        lam_init = 0.8 - 0.6 * math.exp(-0.3 * l)
        misc = jnp.full((SUBLANES, LANES), lam_init, F32)
        o_c = _diff(cq, ck, cvt, diff_lam[l].astype(F32), norm_t[l], misc)
        qm, km, vmt = _mla_prep(dcq, ckv_iw, kr, mla_q_norm[l].reshape(1, Q_LORA), mla_kv_norm[l].reshape(1, KV_LORA),
                                wq, wq_rot, wk, wvt, place, ct_q, st_q, l)
        o_d = _mla(qm, km, vmt)
        o_e = _mem_attn(eq, _mem_kv(mem, wmem, l))

        os5 = [o.reshape(b * t, BRANCH_W) for o in (o_a, o_b, o_c, o_d, o_e)]
        h, hb = _final(h, hb, os5, z.reshape(b * t, N_BRANCH * BRANCH_W), wg, wb, wo,
                       ln_g[l].reshape(1, d), ln_b[l].reshape(1, d), alpha, l)
    return h.reshape(b, t, d)
```

```python
import functools
import math

import numpy as np
import jax
import jax.numpy as jnp
from jax import lax
from jax.experimental import pallas as pl
from jax.experimental.pallas import tpu as pltpu

F32 = jnp.float32
BF16 = jnp.bfloat16
I32 = jnp.int32
I16 = jnp.int16

N_HEADS = 4
HEAD_DIM = 64
BRANCH_W = N_HEADS * HEAD_DIM
N_BRANCH = 5
ROPE_THETA = 500000.0
ROT_64 = 16
ROT_32 = 8
IDX_HEADS = 8
IDX_DIM = 32
TOPK_MAX = 256
MOBA_BLOCK = 256
MOBA_TOPK = 3
DIFF_DIM = 32
Q_LORA = 256
KV_LORA = 128
MLA_NOPE = 64
MLA_ROPE = 32
MLA_V = 64
LN_EPS = 1e-5
RMS_EPS = 1e-6

IN_LAYOUT = (
    ("a_q", BRANCH_W), ("a_k", BRANCH_W), ("a_v", BRANCH_W),
    ("i_q", IDX_HEADS * IDX_DIM), ("i_k", IDX_DIM), ("i_w", IDX_HEADS),
    ("b_q", BRANCH_W), ("b_k", BRANCH_W), ("b_v", BRANCH_W),
    ("c_q", BRANCH_W), ("c_k", BRANCH_W), ("c_v", BRANCH_W),
    ("d_cq", Q_LORA), ("d_ckv", KV_LORA), ("d_kr", MLA_ROPE),
    ("e_q", BRANCH_W),
    ("z", N_BRANCH * BRANCH_W),
    ("g", N_BRANCH * 1024),
)

SUBLANES = 8
LANES = 128
MXU_N = 256
TQ = 512
CK = 256
VROWS = HEAD_DIM + 16
FLASH_UNROLL = 4
NEG = -1e30
LOG2E = math.log2(math.e)
INT_MIN = np.int32(-2 ** 31)
HALF16 = 1 << 15
VMEM_LIMIT = 56 * 1024 * 1024


def _offsets():
    off, out = 0, {}
    for name, size in IN_LAYOUT:
        out[name] = (off, size)
        off += size
    return out


OFF = _offsets()


def _nt_dot(a, b):
    return lax.dot_general(a, b, (((1,), (1,)), ((), ())), preferred_element_type=F32)


def _tn_dot(w, x):
    return lax.dot_general(w, x, (((0,), (1,)), ((), ())), preferred_element_type=F32)


def _fold_rows(w, rows=SUBLANES):
    xs = [w[r:r + rows, :] for r in range(0, w.shape[0], rows)]
    while len(xs) > 1:
        xs = [xs[j] + xs[j + 1] for j in range(0, len(xs) - 1, 2)] + ([xs[-1]] if len(xs) % 2 else [])
    return xs[0]


def _masked_qt(q, shift, n, qt_ref):
    qt = q.T
    dim = lax.broadcasted_iota(I32, (LANES, qt.shape[1]), 0)
    for j in range(n):
        half = (j << shift) // LANES
        rows = qt[half * LANES:(half + 1) * LANES, :]
        qt_ref[j] = jnp.where(((dim + half * LANES) >> shift) == j, rows, 0.0).astype(BF16)


def _half(kc, j, shift):
    half = (j << shift) // LANES
    return kc[:, half * LANES:(half + 1) * LANES]


def _cparams(n_axes):
    return pltpu.CompilerParams(dimension_semantics=("arbitrary",) * n_axes,
                                vmem_limit_bytes=VMEM_LIMIT)


def _layer_spec(a, l):
    return pl.BlockSpec((None,) + a.shape[1:], lambda *_: (l,) + (0,) * (a.ndim - 1))


def _softmax_step(s_t, vt_h, m_ref, acc_ref):
    m_old = m_ref[...]
    m_new = jnp.maximum(m_old, jnp.max(s_t, axis=0, keepdims=True))
    alpha = jnp.exp2(m_old - m_new)
    p = jnp.exp2(s_t - m_new)
    acc_ref[...] = alpha * acc_ref[...] + jnp.dot(vt_h, p.astype(BF16), preferred_element_type=F32)
    m_ref[...] = m_new


def _softmax_init(m_ref, acc_ref):
    m_ref[...] = jnp.full(m_ref.shape, NEG, F32)
    acc_ref[...] = jnp.zeros(acc_ref.shape, F32)


def _softmax_out(acc_ref):
    return acc_ref[:HEAD_DIM, :] / acc_ref[HEAD_DIM:HEAD_DIM + 1, :]


def _store_vt(o_ref, vt):
    ones = jnp.ones((VROWS - HEAD_DIM, CK), o_ref.dtype)
    for j in range(o_ref.shape[0]):
        for h in range(N_HEADS):
            o_ref[j, h * VROWS:h * VROWS + HEAD_DIM, :] = (
                vt[h * HEAD_DIM:(h + 1) * HEAD_DIM, j * CK:(j + 1) * CK].astype(o_ref.dtype))
            o_ref[j, h * VROWS + HEAD_DIM:(h + 1) * VROWS, :] = ones


def _flash_loop(n_full, qk_all, mask, vt_rows, state, prep=None):
    s_ref, m_ref, acc_ref = state
    n_state = m_ref.shape[0]
    for j in range(n_state):
        _softmax_init(m_ref.at[j], acc_ref.at[j])

    def park(c, slot):
        for j, s in enumerate(qk_all(c)):
            s_ref[slot, j] = s

    def consume(c, slot, d):
        ctx = (c, d) if prep is None else prep(c, d)
        for j in range(n_state):
            _softmax_step(mask(ctx, j, s_ref[slot, j]), vt_rows(c, j),
                          m_ref.at[j], acc_ref.at[j])

    park(0, 0)

    def pair(c):
        park(c + 1, 1)
        consume(c, 0, None)
        park(c + 2, 0)
        consume(c + 1, 1, None)

    def body(g, carry):
        for u in range(0, FLASH_UNROLL, 2):
            pair(FLASH_UNROLL * g + u)
        return carry

    n_group = lax.shift_right_logical(n_full, FLASH_UNROLL.bit_length() - 1)
    lax.fori_loop(0, n_group, body, 0)
    c0 = FLASH_UNROLL * n_group
    for u in range(FLASH_UNROLL // 2 - 1):
        @pl.when(n_full - c0 >= 2 * (u + 1))
        def _(u=u):
            pair(c0 + 2 * u)
    park(n_full + 1, 1)
    consume(n_full, 0, 0)
    consume(n_full + 1, 1, 1)


def _causal(d):
    kpos = lax.broadcasted_iota(I32, (CK, TQ), 0) + d * CK
    return kpos <= lax.broadcasted_iota(I32, (CK, TQ), 1)


def _attn_scratch(n_state):
    return [pltpu.VMEM((2, n_state, CK, TQ), F32), pltpu.VMEM((n_state, 1, TQ), F32),
            pltpu.VMEM((n_state, VROWS, TQ), F32), pltpu.VMEM((BRANCH_W, TQ), F32)]


def _kv_specs(t, w):
    kspec = pl.BlockSpec((None, t, w), lambda bb, i: (bb, 0, 0))
    vspec = pl.BlockSpec((None, t // CK, N_HEADS * VROWS, CK), lambda bb, i: (bb, 0, 0, 0))
    return kspec, vspec


def _ln_kernel(x_ref, g_ref, b_ref, h_ref, hb_ref):
    x = x_ref[...]
    mu = jnp.mean(x, axis=1, keepdims=True)
    xc = x - mu
    var = jnp.mean(xc * xc, axis=1, keepdims=True)
    y = xc * lax.rsqrt(var + LN_EPS) * g_ref[...] + b_ref[...]
    h_ref[...] = y
    hb_ref[...] = y.astype(BF16)


def _layer_norm0(x2, g, b):
    n, d = x2.shape
    tm = 512
    row = pl.BlockSpec((tm, d), lambda i: (i, 0))
    vec = pl.BlockSpec((1, d), lambda i: (0, 0))
    return pl.pallas_call(
        _ln_kernel,
        out_shape=(jax.ShapeDtypeStruct((n, d), F32), jax.ShapeDtypeStruct((n, d), BF16)),
        grid=(n // tm,),
        in_specs=[row, vec, vec],
        out_specs=(row, row),
        compiler_params=_cparams(1),
        name="ln0",
    )(x2, g.reshape(1, d), b.reshape(1, d))


def _proj_plain_kernel(x_ref, w_ref, wt_ref, *out_refs, n_t):
    for g, o_ref in enumerate(out_refs[:n_t]):
        _store_vt(o_ref, _tn_dot(wt_ref[:, g * BRANCH_W:(g + 1) * BRANCH_W], x_ref[...]))
    off = 0
    for o_ref in out_refs[n_t:]:
        wd = o_ref.shape[-1]
        for j in range(0, wd, MXU_N):
            acc = jnp.dot(x_ref[...], w_ref[:, off + j:off + j + MXU_N], preferred_element_type=F32)
            o_ref[:, j:j + MXU_N] = acc.astype(o_ref.dtype)
        off += wd


def _proj_plain(hb3, w, wt, widths, l):
    b, t, d = hb3.shape
    tm = 512
    n_t = wt.shape[-1] // BRANCH_W
    shapes = [jax.ShapeDtypeStruct((b, t // CK, N_HEADS * VROWS, CK), BF16)] * n_t
    specs = [pl.BlockSpec((None, tm // CK, N_HEADS * VROWS, CK), lambda i, bb: (bb, i, 0, 0))] * n_t
    shapes += [jax.ShapeDtypeStruct((b, t, wd), BF16) for wd in widths]
    specs += [pl.BlockSpec((None, tm, wd), lambda i, bb: (bb, i, 0)) for wd in widths]
    return pl.pallas_call(
        functools.partial(_proj_plain_kernel, n_t=n_t),
        out_shape=tuple(shapes),
        grid=(t // tm, b),
        in_specs=[pl.BlockSpec((None, tm, d), lambda i, bb: (bb, i, 0)),
                  _layer_spec(w, l), _layer_spec(wt, l)],
        out_specs=tuple(specs),
        compiler_params=_cparams(2),
        name="proj_plain",
    )(hb3, w, wt)


def _proj_rope_kernel(x_ref, w_ref, c_ref, s_ref, *out_refs, heads, tables):
    lane = lax.broadcasted_iota(I32, (x_ref.shape[0], MXU_N), 1)
    for g, o_ref in enumerate(out_refs):
        hd, half = heads[g]
        sl = slice(g * MXU_N, (g + 1) * MXU_N)
        acc = jnp.dot(x_ref[...], w_ref[:, sl], preferred_element_type=F32)
        partner = jnp.where((lane & (hd - 1)) < half,
                            pltpu.roll(acc, MXU_N - half, 1), pltpu.roll(acc, half, 1))
        o_ref[...] = (acc * c_ref[tables[g]] + partner * s_ref[tables[g]]).astype(o_ref.dtype)


def _proj_rope(hb3, w, ctab, stab, heads, tables, l):
    b, t, d = hb3.shape
    tm = 512
    assert w.shape[-1] == MXU_N * len(heads)
    tspec = pl.BlockSpec((ctab.shape[0], tm, MXU_N), lambda i, bb: (0, i, 0))
    ospec = pl.BlockSpec((None, tm, MXU_N), lambda i, bb: (bb, i, 0))
    return pl.pallas_call(
        functools.partial(_proj_rope_kernel, heads=heads, tables=tables),
        out_shape=(jax.ShapeDtypeStruct((b, t, MXU_N), BF16),) * len(heads),
        grid=(t // tm, b),
        in_specs=[pl.BlockSpec((None, tm, d), lambda i, bb: (bb, i, 0)),
                  _layer_spec(w, l), tspec, tspec],
        out_specs=(ospec,) * len(heads),
        compiler_params=_cparams(2),
        name="proj_rope",
    )(hb3, w, ctab, stab)


def _dsa_kernel(aq_ref, ak_ref, avt_ref, iq_ref, ik_ref, iw_ref, pick_ref, tri_ref, o_ref,
                keys_ref, hi_ref, lo_ref, bk_ref, iqt_ref, aqt_ref, wt_ref, thr_ref, s_ref, m_ref, acc_ref, ot_ref,
                *, topk, idx_scale):
    i = pl.program_id(1)
    n_full = 2 * i
    n_pair = i + 1

    _masked_qt(iq_ref[...].astype(F32), 5, IDX_HEADS, iqt_ref)
    _masked_qt(aq_ref[...].astype(F32) * (HEAD_DIM ** -0.5 * LOG2E), 6, N_HEADS, aqt_ref)
    wt_ref[...] = _nt_dot(pick_ref[...], iw_ref[...]) * idx_scale

    def logits(c):
        kc = ik_ref[pl.ds(pl.multiple_of(c * CK, CK), CK), :]
        return [jnp.dot(_half(kc, hh, 5), iqt_ref[hh], preferred_element_type=F32) for hh in range(IDX_HEADS)]

    def put_keys(c, key):
        keys_ref[c] = key
        hi_ref[c] = (key >> 16).astype(I16)
        lo_ref[c] = ((key & 0xFFFF) - HALF16).astype(I16)

    def score_chunk(c, lg, d):
        sc = jnp.zeros((CK, TQ), F32)
        for hh in range(IDX_HEADS):
            sc = sc + jnp.maximum(lg[hh], 0.0) * wt_ref[hh:hh + 1, :]
        bits = pltpu.bitcast(sc, I32)
        key = jnp.where(bits < 0, INT_MIN - bits, bits)
        put_keys(c, key if d is None else jnp.where(_causal(d), key, INT_MIN))

    def score_pair(c, d0, d1):
        lg0, lg1 = logits(c), logits(c + 1)
        score_chunk(c, lg0, d0)
        score_chunk(c + 1, lg1, d1)

    def score_body(p, carry):
        score_pair(2 * p, None, None)
        return carry

    lax.fori_loop(0, i, score_body, 0)
    score_pair(n_full, 0, 1)

    def pair_loop(body, init):
        def pair(p, carry):
            return body(2 * p + 1, body(2 * p, carry))
        return lax.fori_loop(0, n_pair, pair, init)

    def count16(pred, also=None):
        def body(c, part):
            hit = jnp.where(pred(c), jnp.int16(1), jnp.int16(0))
            if also is not None:
                hit = jnp.where(also(c), hit, jnp.int16(0))
            return part + _fold_rows(hit, 2 * SUBLANES)
        part = pair_loop(body, jnp.zeros((2 * SUBLANES, TQ), I16))
        return jnp.sum(part.astype(F32), axis=0, keepdims=True)

    def search16(ref, need):
        def bit_body(bi, t_u):
            c_u = t_u | jnp.left_shift(jnp.int32(1), 15 - bi)
            ck = (c_u - HALF16).astype(I16)
            cnt = count16(lambda c: ref[c] >= ck)
            return jnp.where(cnt >= need, c_u, t_u)
        return lax.fori_loop(0, 16, bit_body, jnp.zeros((1, TQ), I32))

    hi_u = search16(hi_ref, float(topk))
    thr_hi = (hi_u - HALF16).astype(I16)
    n_above = count16(lambda c: hi_ref[c] > thr_hi)

    def bucket_body(c, carry):
        bk_ref[c] = jnp.where(hi_ref[c] == thr_hi, lo_ref[c], jnp.int16(-HALF16))
        return carry

    pair_loop(bucket_body, 0)
    lo_u = search16(bk_ref, float(topk) - n_above)
    thr_lo = (lo_u - HALF16).astype(I16)
    thr = ((hi_u - HALF16) << 16) | lo_u

    n_gt = n_above + count16(lambda c: bk_ref[c] > thr_lo)
    n_eq = count16(lambda c: lo_ref[c] == thr_lo, also=lambda c: hi_ref[c] == thr_hi)
    need = float(topk) - n_gt
    amb = jnp.logical_and(n_eq > need, thr > INT_MIN)
    any_amb = jnp.max(jnp.where(amb, 1.0, 0.0)) > 0.5

    @pl.when(any_amb)
    def _():
        def drop_body(c, seen):
            k = keys_ref[c]
            eq = k == thr
            eqf = jnp.where(eq, 1.0, 0.0)
            rank = jnp.dot(tri_ref[...], eqf.astype(BF16), preferred_element_type=F32) + seen
            drop = jnp.logical_and(jnp.logical_and(eq, rank > need), amb)
            keys_ref[c] = jnp.where(drop, INT_MIN, k)
            return seen + jnp.sum(eqf, axis=0, keepdims=True)

        pair_loop(drop_body, jnp.zeros((1, TQ), F32))

    thr_ref[...] = jnp.maximum(thr, INT_MIN + 1)

    def qk_all(c):
        kc = ak_ref[pl.ds(pl.multiple_of(c * CK, CK), CK), :]
        return [jnp.dot(_half(kc, h, 6), aqt_ref[h], preferred_element_type=F32) for h in range(N_HEADS)]

    _flash_loop(n_full, qk_all,
                lambda keep, h, s: jnp.where(keep, s, NEG),
                lambda c, h: avt_ref[c, h * VROWS:(h + 1) * VROWS, :],
                (s_ref, m_ref, acc_ref),
                prep=lambda c, d: keys_ref[c] >= thr_ref[...])
    for h in range(N_HEADS):
        ot_ref[h * HEAD_DIM:(h + 1) * HEAD_DIM, :] = _softmax_out(acc_ref.at[h])
    o_ref[...] = ot_ref[...].T.astype(o_ref.dtype)


def _dsa(aq, ak, avt, iq, ik, iw):
    b, t, _ = aq.shape
    topk = min(TOPK_MAX, t // 4)
    qspec = pl.BlockSpec((None, TQ, BRANCH_W), lambda bb, i: (bb, i, 0))
    kspec, vspec = _kv_specs(t, BRANCH_W)
    pick = np.zeros((2 * SUBLANES, MXU_N), np.float32)
    for hh in range(IDX_HEADS):
        pick[hh, KV_LORA + hh] = 1.0
    pick = jnp.asarray(pick, BF16)
    tri = jnp.asarray(np.tril(np.ones((CK, CK), np.float32)), BF16)
    kern = functools.partial(_dsa_kernel, topk=topk, idx_scale=(IDX_HEADS * IDX_DIM) ** -0.5)
    return pl.pallas_call(
        kern,
        out_shape=jax.ShapeDtypeStruct((b, t, BRANCH_W), BF16),
        grid=(b, t // TQ),
        in_specs=[qspec, kspec, vspec, qspec, kspec, qspec,
                  pl.BlockSpec(pick.shape, lambda bb, i: (0, 0)), pl.BlockSpec(tri.shape, lambda bb, i: (0, 0))],
        out_specs=qspec,
        scratch_shapes=[
            pltpu.VMEM((t // CK, CK, TQ), I32),
            pltpu.VMEM((t // CK, CK, TQ), I16),
            pltpu.VMEM((t // CK, CK, TQ), I16),
            pltpu.VMEM((t // CK, CK, TQ), I16),
            pltpu.VMEM((IDX_HEADS, LANES, TQ), BF16),
            pltpu.VMEM((N_HEADS, LANES, TQ), BF16),
            pltpu.VMEM((2 * SUBLANES, TQ), F32),
            pltpu.VMEM((1, TQ), I32),
        ] + _attn_scratch(N_HEADS),
        compiler_params=_cparams(2),
        name="dsa",
    )(aq, ak, avt, iq, ik, iw, pick, tri)


def _kbar_kernel(k_ref, o_ref):
    o_ref[...] = jnp.zeros(o_ref.shape, o_ref.dtype)
    nb = k_ref.shape[0] // MOBA_BLOCK
    for n in range(nb):
        blk = k_ref[n * MOBA_BLOCK:(n + 1) * MOBA_BLOCK, :].astype(F32)
        o_ref[n:n + 1, :] = jnp.mean(blk, axis=0, keepdims=True).astype(o_ref.dtype)


def _kbar(bk):
    b, t, w = bk.shape
    nbp = max(2 * SUBLANES, t // MOBA_BLOCK)
    return pl.pallas_call(
        _kbar_kernel,
        out_shape=jax.ShapeDtypeStruct((b, nbp, w), BF16),
        grid=(b,),
        in_specs=[pl.BlockSpec((None, t, w), lambda bb: (bb, 0, 0))],
        out_specs=pl.BlockSpec((None, nbp, w), lambda bb: (bb, 0, 0)),
        compiler_params=_cparams(1),
        name="moba_kbar",
    )(bk)


def _moba_kernel(q_ref, k_ref, vt_ref, kbar_ref, o_ref, qt_ref, bias_ref, s_ref, m_ref, acc_ref, ot_ref):
    i = pl.program_id(1)
    nbp = kbar_ref.shape[0]
    blk = lax.broadcasted_iota(I32, (nbp, TQ), 0)
    blk_f = blk.astype(F32)
    own = 2 * i + (lax.broadcasted_iota(I32, (nbp, TQ), 1) >> (MOBA_BLOCK.bit_length() - 1))
    upper = lax.broadcasted_iota(I32, (1, TQ), 1) >= MOBA_BLOCK
    _masked_qt(q_ref[...].astype(F32) * (HEAD_DIM ** -0.5 * LOG2E), 6, N_HEADS, qt_ref)

    for h in range(N_HEADS):
        g = jnp.where(blk < own, jnp.dot(_half(kbar_ref[...], h, 6), qt_ref[h], preferred_element_type=F32), NEG)
        bias = jnp.full((nbp, TQ), NEG, F32)
        for _ in range(MOBA_TOPK):
            mx = jnp.max(g, axis=0, keepdims=True)
            first = jnp.min(jnp.where(g == mx, blk_f, 1e9), axis=0, keepdims=True)
            pick = jnp.logical_and(blk_f == first, mx > 0.5 * NEG)
            bias = jnp.where(pick, 0.0, bias)
            g = jnp.where(pick, NEG, g)
        bias_ref[h] = bias

    def qk_all(c):
        kc = k_ref[pl.ds(pl.multiple_of(c * CK, CK), CK), :]
        return [jnp.dot(_half(kc, h, 6), qt_ref[h], preferred_element_type=F32) for h in range(N_HEADS)]

    def mask(ctx, h, s):
        c, d = ctx
        if d is None:
            return s + bias_ref[h, pl.ds(c, 1), :]
        own_part = jnp.where(_causal(d), s, NEG)
        if d == 1:
            return own_part
        return jnp.where(upper, s + bias_ref[h, pl.ds(c, 1), :], own_part)

    _flash_loop(2 * i, qk_all, mask, lambda c, h: vt_ref[c, h * VROWS:(h + 1) * VROWS, :],
                (s_ref, m_ref, acc_ref))
    for h in range(N_HEADS):
        ot_ref[h * HEAD_DIM:(h + 1) * HEAD_DIM, :] = _softmax_out(acc_ref.at[h])
    o_ref[...] = ot_ref[...].T.astype(o_ref.dtype)


def _moba(bq, bk, bvt, kbar):
    b, t, w = bq.shape
    assert TQ == 2 * MOBA_BLOCK and CK == MOBA_BLOCK and t % TQ == 0
    nbp = kbar.shape[1]
    qspec = pl.BlockSpec((None, TQ, w), lambda bb, i: (bb, i, 0))
    kspec, vspec = _kv_specs(t, w)
    return pl.pallas_call(
        _moba_kernel,
        out_shape=jax.ShapeDtypeStruct((b, t, w), BF16),
        grid=(b, t // TQ),
        in_specs=[qspec, kspec, vspec, pl.BlockSpec((None, nbp, w), lambda bb, i: (bb, 0, 0))],
        out_specs=qspec,
        scratch_shapes=[pltpu.VMEM((N_HEADS, LANES, TQ), BF16), pltpu.VMEM((N_HEADS, nbp, TQ), F32)]
        + _attn_scratch(N_HEADS),
        compiler_params=_cparams(2),
        name="moba",
    )(bq, bk, bvt, kbar)


def _diff_kernel(q_ref, k_ref, vt_ref, lam_ref, norm_ref, misc_ref, o_ref,
                 qt_ref, s_ref, m_ref, acc_ref, ot_ref):
    i = pl.program_id(1)
    _masked_qt(q_ref[...].astype(F32) * (DIFF_DIM ** -0.5 * LOG2E), 5, 2 * N_HEADS, qt_ref)

    dl = lam_ref[...]
    lam_init = misc_ref[0:1, 0:1]
    lam = (jnp.exp(jnp.sum(dl[0:1, :] * dl[1:2, :], axis=1, keepdims=True))
           - jnp.exp(jnp.sum(dl[2:3, :] * dl[3:4, :], axis=1, keepdims=True)) + lam_init)

    def qk_all(c):
        kc = k_ref[pl.ds(pl.multiple_of(c * CK, CK), CK), :]
        return [jnp.dot(_half(kc, j, 5), qt_ref[j], preferred_element_type=F32) for j in range(2 * N_HEADS)]

    _flash_loop(2 * i, qk_all,
                lambda ctx, j, s: s if ctx[1] is None else jnp.where(_causal(ctx[1]), s, NEG),
                lambda c, j: vt_ref[c, (j // 2) * VROWS:(j // 2 + 1) * VROWS, :],
                (s_ref, m_ref, acc_ref))

    post = norm_ref[...] * (1.0 - lam_init)
    for h in range(N_HEADS):
        o_h = _softmax_out(acc_ref.at[2 * h]) - lam * _softmax_out(acc_ref.at[2 * h + 1])
        ms = jnp.mean(o_h * o_h, axis=0, keepdims=True)
        ot_ref[h * HEAD_DIM:(h + 1) * HEAD_DIM, :] = o_h * lax.rsqrt(ms + RMS_EPS) * post
    o_ref[...] = ot_ref[...].T.astype(o_ref.dtype)


def _diff(cq, ck, cvt, lam, norm, misc):
    b, t, w = cq.shape
    qspec = pl.BlockSpec((None, TQ, w), lambda bb, i: (bb, i, 0))
    kspec, vspec = _kv_specs(t, w)
    full = lambda a: pl.BlockSpec(a.shape, lambda bb, i: (0,) * a.ndim)
    return pl.pallas_call(
        _diff_kernel,
        out_shape=jax.ShapeDtypeStruct((b, t, w), BF16),
        grid=(b, t // TQ),
        in_specs=[qspec, kspec, vspec, full(lam), full(norm), full(misc)],
        out_specs=qspec,
        scratch_shapes=[pltpu.VMEM((2 * N_HEADS, LANES, TQ), BF16)] + _attn_scratch(2 * N_HEADS),
        compiler_params=_cparams(2),
        name="diff",
    )(cq, ck, cvt, lam, norm, misc)


def _mla_prep_kernel(cq_ref, ckv_ref, kr_ref, qn_ref, kvn_ref, wq_ref, wqr_ref, wk_ref, wvt_ref,
                     p_ref, ct_ref, st_ref, q_out, k_out, vt_out):
    x = cq_ref[...].astype(F32)
    xn = (x * lax.rsqrt(jnp.mean(x * x, axis=1, keepdims=True) + RMS_EPS) * qn_ref[...]).astype(BF16)
    q = (jnp.dot(xn, wq_ref[...], preferred_element_type=F32) * ct_ref[...]
         + jnp.dot(xn, wqr_ref[...], preferred_element_type=F32) * st_ref[...])
    q_out[...] = q.astype(q_out.dtype)
    c = ckv_ref[:, :KV_LORA].astype(F32)
    cn = (c * lax.rsqrt(jnp.mean(c * c, axis=1, keepdims=True) + RMS_EPS) * kvn_ref[...]).astype(BF16)
    k = (jnp.dot(cn, wk_ref[...], preferred_element_type=F32)
         + jnp.dot(kr_ref[...], p_ref[...], preferred_element_type=F32))
    k_out[...] = k.astype(k_out.dtype)
    _store_vt(vt_out, _tn_dot(wvt_ref[...], cn))


def _mla_prep(dcq, ckv, kr, qn, kvn, wq, wqr, wk, wvt, pmat, ct, st, l):
    b, t, _ = dcq.shape
    tm = 512
    hw = N_HEADS * LANES
    row = lambda w: pl.BlockSpec((None, tm, w), lambda i, bb: (bb, i, 0))
    full = lambda a: pl.BlockSpec(a.shape, lambda i, bb: (0,) * a.ndim)
    tab = pl.BlockSpec((tm, hw), lambda i, bb: (i, 0))
    return pl.pallas_call(
        _mla_prep_kernel,
        out_shape=(jax.ShapeDtypeStruct((b, t, hw), BF16), jax.ShapeDtypeStruct((b, t, hw), BF16),
                   jax.ShapeDtypeStruct((b, t // CK, N_HEADS * VROWS, CK), BF16)),
        grid=(t // tm, b),
        in_specs=[row(Q_LORA), row(MXU_N), row(MXU_N), full(qn), full(kvn), _layer_spec(wq, l), _layer_spec(wqr, l),
                  _layer_spec(wk, l), _layer_spec(wvt, l), full(pmat), tab, tab],
        out_specs=(row(hw), row(hw),
                   pl.BlockSpec((None, tm // CK, N_HEADS * VROWS, CK), lambda i, bb: (bb, i, 0, 0))),
        compiler_params=_cparams(2),
        name="mla_prep",
    )(dcq, ckv, kr, qn, kvn, wq, wqr, wk, wvt, pmat, ct, st)


def _mla_kernel(q_ref, k_ref, vt_ref, o_ref, qt_ref, s_ref, m_ref, acc_ref, ot_ref):
    i = pl.program_id(1)
    hs = [slice(h * LANES, (h + 1) * LANES) for h in range(N_HEADS)]
    for h in range(N_HEADS):
        qt_ref[h] = q_ref[:, hs[h]].astype(F32).T.astype(BF16)

    def qk_all(c):
        start = pl.multiple_of(c * CK, CK)
        return [jnp.dot(k_ref[pl.ds(start, CK), hs[h]], qt_ref[h], preferred_element_type=F32)
                for h in range(N_HEADS)]

    _flash_loop(2 * i, qk_all,
                lambda ctx, h, s: s if ctx[1] is None else jnp.where(_causal(ctx[1]), s, NEG),
                lambda c, h: vt_ref[c, h * VROWS:(h + 1) * VROWS, :],
                (s_ref, m_ref, acc_ref))
    for h in range(N_HEADS):
        ot_ref[h * HEAD_DIM:(h + 1) * HEAD_DIM, :] = _softmax_out(acc_ref.at[h])
    o_ref[...] = ot_ref[...].T.astype(o_ref.dtype)


def _mla(qm, km, vmt):
    b, t, hw = qm.shape
    kspec, vspec = _kv_specs(t, hw)
    return pl.pallas_call(
        _mla_kernel,
        out_shape=jax.ShapeDtypeStruct((b, t, BRANCH_W), BF16),
        grid=(b, t // TQ),
        in_specs=[pl.BlockSpec((None, TQ, hw), lambda bb, i: (bb, i, 0)), kspec, vspec],
        out_specs=pl.BlockSpec((None, TQ, BRANCH_W), lambda bb, i: (bb, i, 0)),
        scratch_shapes=[pltpu.VMEM((N_HEADS, LANES, TQ), BF16)] + _attn_scratch(N_HEADS),
        compiler_params=_cparams(2),
        name="mla",
    )(qm, km, vmt)


def _matmul_kernel(x_ref, w_ref, o_ref):
    o_ref[...] = jnp.dot(x_ref[...].astype(BF16), w_ref[...], preferred_element_type=F32).astype(o_ref.dtype)


def _mem_kv(mem, w, l):
    b, m, d = mem.shape
    n = w.shape[-1]
    return pl.pallas_call(
        _matmul_kernel,
        out_shape=jax.ShapeDtypeStruct((b, m, n), BF16),
        grid=(b,),
        in_specs=[pl.BlockSpec((None, m, d), lambda bb: (bb, 0, 0)), _layer_spec(w, l)],
        out_specs=pl.BlockSpec((None, m, n), lambda bb: (bb, 0, 0)),
        compiler_params=_cparams(1),
        name="mem_kv",
    )(mem, w)


def _mem_kernel(q_ref, kv_ref, o_ref):
    tq = q_ref.shape[0]
    lane_q = lax.broadcasted_iota(I32, (tq, BRANCH_W), 1)
    q = q_ref[...].astype(F32) * (HEAD_DIM ** -0.5)
    mk = kv_ref[:, :BRANCH_W]
    mv = kv_ref[:, BRANCH_W:]
    out = jnp.zeros((tq, BRANCH_W), F32)
    for h in range(N_HEADS):
        in_h = (lane_q >> 6) == h
        s = _nt_dot(jnp.where(in_h, q, 0.0).astype(BF16), mk)
        p = jnp.exp(s - jnp.max(s, axis=1, keepdims=True))
        o_h = jnp.dot(p.astype(BF16), mv, preferred_element_type=F32) / jnp.sum(p, axis=1, keepdims=True)
        out = jnp.where(in_h, o_h, out)
    o_ref[...] = out.astype(o_ref.dtype)


def _mem_attn(eq, mkv):
    b, t, w = eq.shape
    m = mkv.shape[1]
    tq = 512
    return pl.pallas_call(
        _mem_kernel,
        out_shape=jax.ShapeDtypeStruct((b, t, w), BF16),
        grid=(b, t // tq),
        in_specs=[pl.BlockSpec((None, tq, w), lambda bb, i: (bb, i, 0)),
                  pl.BlockSpec((None, m, 2 * w), lambda bb, i: (bb, 0, 0))],
        out_specs=pl.BlockSpec((None, tq, w), lambda bb, i: (bb, i, 0)),
        compiler_params=_cparams(2),
        name="mem_attn",
    )(eq, mkv)


def _final_kernel(h_ref, hb_ref, oa_ref, ob_ref, oc_ref, od_ref, oe_ref, z_ref,
                  wg_ref, wb_ref, wo_ref, g_ref, b_ref, h_out, hb_out, *, alpha):
    d = h_ref.shape[1]
    acc = jnp.zeros(h_ref.shape, F32)
    for n, o_ref in enumerate((oa_ref, ob_ref, oc_ref, od_ref, oe_ref)):
        z = z_ref[:, n * BRANCH_W:(n + 1) * BRANCH_W].astype(F32)
        y = o_ref[...].astype(F32) * (z / (1.0 + jnp.exp(-z)))
        u = jnp.dot(y.astype(BF16), wb_ref[n], preferred_element_type=F32)
        g = jnp.dot(hb_ref[...], wg_ref[:, n * d:(n + 1) * d], preferred_element_type=F32)
        acc = acc + u / (1.0 + jnp.exp(-g))
    out = jnp.dot(acc.astype(BF16), wo_ref[...], preferred_element_type=F32)
    x = alpha * h_ref[...] + out
    mu = jnp.mean(x, axis=1, keepdims=True)
    xc = x - mu
    var = jnp.mean(xc * xc, axis=1, keepdims=True)
    y = xc * lax.rsqrt(var + LN_EPS) * g_ref[...] + b_ref[...]
    h_out[...] = y
    hb_out[...] = y.astype(BF16)


def _final(h, hb, os5, z, wg, wb, wo, ln_g, ln_b, alpha, l):
    n, d = h.shape
    tm = 256
    row = lambda w: pl.BlockSpec((tm, w), lambda i: (i, 0))
    full = lambda a: pl.BlockSpec(a.shape, lambda i: (0,) * a.ndim)
    return pl.pallas_call(
        functools.partial(_final_kernel, alpha=alpha),
        out_shape=(jax.ShapeDtypeStruct((n, d), F32), jax.ShapeDtypeStruct((n, d), BF16)),
        grid=(n // tm,),
        in_specs=[row(d), row(d)] + [row(BRANCH_W)] * N_BRANCH + [row(N_BRANCH * BRANCH_W),
                  _layer_spec(wg, l), _layer_spec(wb, l), _layer_spec(wo, l), full(ln_g), full(ln_b)],
        out_specs=(row(d), row(d)),
        compiler_params=_cparams(1),
        name="merge_out_ln",
    )(h, hb, *os5, z, wg, wb, wo, ln_g, ln_b)


ROPE_GROUPS = (("a_q", N_HEADS, HEAD_DIM, ROT_64), ("a_k", N_HEADS, HEAD_DIM, ROT_64),
               ("i_q", IDX_HEADS, IDX_DIM, ROT_32), ("i_k", IDX_HEADS, IDX_DIM, ROT_32),
               ("b_q", N_HEADS, HEAD_DIM, ROT_64), ("b_k", N_HEADS, HEAD_DIM, ROT_64),
               ("c_q", 2 * N_HEADS, DIFF_DIM, ROT_32), ("c_k", 2 * N_HEADS, DIFF_DIM, ROT_32),
               ("d_kr", 1, MXU_N, MLA_ROPE))
PLAIN_COLS = ("d_cq", "d_ckv", "i_w", None, "e_q", "z")
VALUE_COLS = ("a_v", "b_v", "c_v")


def _weight_prep_kernel(w_ref, plain_ref, vt_ref, rope_ref, g_ref):
    rows = w_ref.shape[0]

    def put(ref, names):
        off = 0
        for name in names:
            if name is None:
                part = jnp.zeros((rows, -off % MXU_N), ref.dtype)
            else:
                o, s = OFF[name]
                part = w_ref[:, o:o + s].astype(ref.dtype)
            ref[:, off:off + part.shape[1]] = part
            off += part.shape[1]
        assert off == ref.shape[1]

    put(plain_ref, PLAIN_COLS)
    put(vt_ref, VALUE_COLS)
    rope_cols = []
    for name, nh, hd, _ in ROPE_GROUPS:
        copies = IDX_HEADS if name == "i_k" else 1
        rope_cols += [name] * copies + ([None] if copies * OFF[name][1] < MXU_N else [])
    put(rope_ref, rope_cols)
    put(g_ref, ("g",))


def _weight_prep(w_in):
    depth, d, n = w_in.shape
    tm = 64
    widths = (sum(OFF[c][1] for c in PLAIN_COLS if c) + MXU_N - KV_LORA - IDX_HEADS,
              len(VALUE_COLS) * BRANCH_W, len(ROPE_GROUPS) * MXU_N, OFF["g"][1])
    return pl.pallas_call(
        _weight_prep_kernel,
        out_shape=tuple(jax.ShapeDtypeStruct((depth, d, w), BF16) for w in widths),
        grid=(depth, d // tm),
        in_specs=[pl.BlockSpec((None, tm, n), lambda l, i: (l, i, 0))],
        out_specs=tuple(pl.BlockSpec((None, tm, w), lambda l, i: (l, i, 0)) for w in widths),
        compiler_params=_cparams(2),
        name="weight_prep",
    )(w_in)


def _rope_tables(seq, rot_dim):
    pos = jnp.arange(seq, dtype=F32)
    inv = ROPE_THETA ** (-jnp.arange(0, rot_dim, 2, dtype=F32) / rot_dim)
    ang = pos[:, None] * inv[None, :]
    return jnp.cos(ang), jnp.sin(ang)


def _rope_cs(t, nh, hd, r):
    cos, sin = _rope_tables(t, r)
    c = jnp.concatenate([cos, cos, jnp.ones((t, hd - r), F32)], axis=1)
    s = jnp.concatenate([-sin, sin, jnp.zeros((t, hd - r), F32)], axis=1)
    return jnp.tile(c, (1, nh)), jnp.tile(s, (1, nh))


def kernel(x, mem, ln0_g, ln0_b, w_in, mla_q_norm, w_uq, mla_kv_norm, w_ukv, diff_lam, diff_norm,
           w_mem_kv, w_branch, w_out, ln_g, ln_b):
    b, t, d = x.shape
    depth = w_in.shape[0]
    alpha = (2 * depth) ** 0.25
    assert t % 512 == 0 and d == 1024

    w_plain, w_vt, w_rope, wg = _weight_prep(w_in)
    plain_widths = (BRANCH_W,) * 3 + (N_BRANCH * BRANCH_W,)
    rope_heads = tuple((hd, r // 2) for _, _, hd, r in ROPE_GROUPS)
    patterns = sorted(set((nh, hd, r) for _, nh, hd, r in ROPE_GROUPS))
    rope_tables = tuple(patterns.index((nh, hd, r)) for _, nh, hd, r in ROPE_GROUPS)
    cs = [_rope_cs(t, nh, hd, r) for nh, hd, r in patterns]
    ctab = jnp.stack([c for c, _ in cs])
    stab = jnp.stack([s for _, s in cs])

    uq = w_uq.reshape(depth, Q_LORA, N_HEADS, MLA_NOPE + MLA_ROPE)
    qn_w, qr_w = uq[..., :MLA_NOPE], uq[..., MLA_NOPE:]
    pad32 = jnp.zeros((depth, Q_LORA, N_HEADS, LANES - MLA_NOPE - MLA_ROPE), w_uq.dtype)
    hw = N_HEADS * LANES
    wq = jnp.concatenate([qn_w, qr_w, pad32], axis=-1).reshape(depth, Q_LORA, hw).astype(BF16)
    half = MLA_ROPE // 2
    wq_rot = jnp.concatenate([jnp.zeros_like(qn_w), -qr_w[..., half:], qr_w[..., :half], pad32],
                             axis=-1).reshape(depth, Q_LORA, hw).astype(BF16)
    cos_m, sin_m = _rope_tables(t, MLA_ROPE)
    one = lambda n: jnp.ones((t, n), F32)
    zer = lambda n: jnp.zeros((t, n), F32)
    qs = (MLA_NOPE + MLA_ROPE) ** -0.5 * LOG2E
    ct_q = qs * jnp.tile(jnp.concatenate([one(MLA_NOPE), cos_m, cos_m, one(LANES - MLA_NOPE - MLA_ROPE)], axis=1), (1, N_HEADS))
    st_q = qs * jnp.tile(jnp.concatenate([zer(MLA_NOPE), sin_m, sin_m, zer(LANES - MLA_NOPE - MLA_ROPE)], axis=1), (1, N_HEADS))
    ukv = w_ukv.reshape(depth, KV_LORA, N_HEADS, MLA_NOPE + MLA_V)
    wk = jnp.concatenate([ukv[..., :MLA_NOPE], jnp.zeros((depth, KV_LORA, N_HEADS, LANES - MLA_NOPE), w_ukv.dtype)],
                         axis=-1).reshape(depth, KV_LORA, hw).astype(BF16)
    wvt = ukv[..., MLA_NOPE:].reshape(depth, KV_LORA, N_HEADS * MLA_V).astype(BF16)
    place = np.zeros((MXU_N, hw), np.float32)
    for hh in range(N_HEADS):
        for j in range(MLA_ROPE):
            place[j, hh * LANES + MLA_NOPE + j] = 1.0
    place = jnp.asarray(place, BF16)

    wb = w_branch.astype(BF16)
    wo = w_out.astype(BF16)
    wmem = w_mem_kv.astype(BF16)
    norm_t = jnp.broadcast_to(diff_norm.astype(F32)[:, :, None], (depth, HEAD_DIM, TQ))

    h, hb = _layer_norm0(x.reshape(b * t, d), ln0_g, ln0_b)
    for l in range(depth):
        hb3 = hb.reshape(b, t, d)
        avt, bvt, cvt, dcq, ckv_iw, eq, z = _proj_plain(hb3, w_plain, w_vt, plain_widths, l)
        aq, ak, iq, ik, bq, bk, cq, ck, kr = _proj_rope(hb3, w_rope, ctab, stab, rope_heads, rope_tables, l)

        o_a = _dsa(aq, ak, avt, iq, ik, ckv_iw)
        o_b = _moba(bq, bk, bvt, _kbar(bk))
        lam_init = 0.8 - 0.6 * math.exp(-0.3 * l)
        misc = jnp.full((SUBLANES, LANES), lam_init, F32)
        o_c = _diff(cq, ck, cvt, diff_lam[l].astype(F32), norm_t[l], misc)
        qm, km, vmt = _mla_prep(dcq, ckv_iw, kr, mla_q_norm[l].reshape(1, Q_LORA), mla_kv_norm[l].reshape(1, KV_LORA),
                                wq, wq_rot, wk, wvt, place, ct_q, st_q, l)
        o_d = _mla(qm, km, vmt)
        o_e = _mem_attn(eq, _mem_kv(mem, wmem, l))

        os5 = [o.reshape(b * t, BRANCH_W) for o in (o_a, o_b, o_c, o_d, o_e)]
        h, hb = _final(h, hb, os5, z.reshape(b * t, N_BRANCH * BRANCH_W), wg, wb, wo,
                       ln_g[l].reshape(1, d), ln_b[l].reshape(1, d), alpha, l)
    return h.reshape(b, t, d)
```

```python
import functools
import math

import numpy as np
import jax
import jax.numpy as jnp
from jax import lax
from jax.experimental import pallas as pl
from jax.experimental.pallas import tpu as pltpu

F32 = jnp.float32
BF16 = jnp.bfloat16
I32 = jnp.int32
I16 = jnp.int16

N_HEADS = 4
HEAD_DIM = 64
BRANCH_W = N_HEADS * HEAD_DIM
N_BRANCH = 5
ROPE_THETA = 500000.0
ROT_64 = 16
ROT_32 = 8
IDX_HEADS = 8
IDX_DIM = 32
TOPK_MAX = 256
MOBA_BLOCK = 256
MOBA_TOPK = 3
DIFF_DIM = 32
Q_LORA = 256
KV_LORA = 128
MLA_NOPE = 64
MLA_ROPE = 32
MLA_V = 64
LN_EPS = 1e-5
RMS_EPS = 1e-6

IN_LAYOUT = (
    ("a_q", BRANCH_W), ("a_k", BRANCH_W), ("a_v", BRANCH_W),
    ("i_q", IDX_HEADS * IDX_DIM), ("i_k", IDX_DIM), ("i_w", IDX_HEADS),
    ("b_q", BRANCH_W), ("b_k", BRANCH_W), ("b_v", BRANCH_W),
    ("c_q", BRANCH_W), ("c_k", BRANCH_W), ("c_v", BRANCH_W),
    ("d_cq", Q_LORA), ("d_ckv", KV_LORA), ("d_kr", MLA_ROPE),
    ("e_q", BRANCH_W),
    ("z", N_BRANCH * BRANCH_W),
    ("g", N_BRANCH * 1024),
)

SUBLANES = 8
LANES = 128
MXU_N = 256
TQ = 512
CK = 256
VROWS = HEAD_DIM + 16
FLASH_UNROLL = 4
NEG = -1e30
LOG2E = math.log2(math.e)
INT_MIN = np.int32(-2 ** 31)
HALF16 = 1 << 15
VMEM_LIMIT = 56 * 1024 * 1024


def _offsets():
    off, out = 0, {}
    for name, size in IN_LAYOUT:
        out[name] = (off, size)
        off += size
    return out


OFF = _offsets()


def _nt_dot(a, b):
    return lax.dot_general(a, b, (((1,), (1,)), ((), ())), preferred_element_type=F32)


def _tn_dot(w, x):
    return lax.dot_general(w, x, (((0,), (1,)), ((), ())), preferred_element_type=F32)


def _fold_rows(w, rows=SUBLANES):
    xs = [w[r:r + rows, :] for r in range(0, w.shape[0], rows)]
    while len(xs) > 1:
        xs = [xs[j] + xs[j + 1] for j in range(0, len(xs) - 1, 2)] + ([xs[-1]] if len(xs) % 2 else [])
    return xs[0]


def _masked_qt(q, shift, n, qt_ref):
    qt = q.T
    dim = lax.broadcasted_iota(I32, (LANES, qt.shape[1]), 0)
    for j in range(n):
        half = (j << shift) // LANES
        rows = qt[half * LANES:(half + 1) * LANES, :]
        qt_ref[j] = jnp.where(((dim + half * LANES) >> shift) == j, rows, 0.0).astype(BF16)


def _half(kc, j, shift):
    half = (j << shift) // LANES
    return kc[:, half * LANES:(half + 1) * LANES]


def _cparams(n_axes):
    return pltpu.CompilerParams(dimension_semantics=("arbitrary",) * n_axes,
                                vmem_limit_bytes=VMEM_LIMIT)


def _layer_spec(a, l):
    return pl.BlockSpec((None,) + a.shape[1:], lambda *_: (l,) + (0,) * (a.ndim - 1))


def _softmax_step(s_t, m_tile, vt_h, m_ref, acc_ref):
    m_old = m_ref[...]
    m_new = jnp.maximum(m_old, m_tile)
    alpha = jnp.exp2(m_old - m_new)
    p = jnp.exp2(s_t - m_new)
    acc_ref[...] = alpha * acc_ref[...] + jnp.dot(vt_h, p.astype(BF16), preferred_element_type=F32)
    m_ref[...] = m_new


def _softmax_init(m_ref, acc_ref):
    m_ref[...] = jnp.full(m_ref.shape, NEG, F32)
    acc_ref[...] = jnp.zeros(acc_ref.shape, F32)


def _softmax_out(acc_ref):
    return acc_ref[:HEAD_DIM, :] / acc_ref[HEAD_DIM:HEAD_DIM + 1, :]


def _store_vt(o_ref, vt):
    ones = jnp.ones((VROWS - HEAD_DIM, CK), o_ref.dtype)
    for j in range(o_ref.shape[0]):
        for h in range(N_HEADS):
            o_ref[j, h * VROWS:h * VROWS + HEAD_DIM, :] = (
                vt[h * HEAD_DIM:(h + 1) * HEAD_DIM, j * CK:(j + 1) * CK].astype(o_ref.dtype))
            o_ref[j, h * VROWS + HEAD_DIM:(h + 1) * VROWS, :] = ones


def _flash_loop(n_full, qk_all, mask, vt_rows, state, prep=None, causal_tail=True):
    s_ref, mx_ref, m_ref, acc_ref = state
    n_state = m_ref.shape[0]
    for j in range(n_state):
        _softmax_init(m_ref.at[j], acc_ref.at[j])

    def park(c, slot):
        ctx = c if prep is None else prep(c)
        for j, s in enumerate(qk_all(c)):
            if mask is not None:
                s = mask(ctx, j, s)
            s_ref[slot, j] = s
            mx_ref[slot, j] = jnp.max(s, axis=0, keepdims=True)

    def consume(c, slot, d):
        for j in range(n_state):
            if d is None or not causal_tail:
                s, m_tile = s_ref[slot, j], mx_ref[slot, j]
            else:
                s = jnp.where(_causal(d), s_ref[slot, j], NEG)
                m_tile = jnp.max(s, axis=0, keepdims=True)
            _softmax_step(s, m_tile, vt_rows(c, j), m_ref.at[j], acc_ref.at[j])

    park(0, 0)

    def pair(c):
        park(c + 1, 1)
        consume(c, 0, None)
        park(c + 2, 0)
        consume(c + 1, 1, None)

    def body(g, carry):
        for u in range(0, FLASH_UNROLL, 2):
            pair(FLASH_UNROLL * g + u)
        return carry

    n_group = lax.shift_right_logical(n_full, FLASH_UNROLL.bit_length() - 1)
    lax.fori_loop(0, n_group, body, 0)
    c0 = FLASH_UNROLL * n_group
    for u in range(FLASH_UNROLL // 2 - 1):
        @pl.when(n_full - c0 >= 2 * (u + 1))
        def _(u=u):
            pair(c0 + 2 * u)
    park(n_full + 1, 1)
    consume(n_full, 0, 0)
    consume(n_full + 1, 1, 1)


def _causal(d):
    kpos = lax.broadcasted_iota(I32, (CK, TQ), 0) + d * CK
    return kpos <= lax.broadcasted_iota(I32, (CK, TQ), 1)


def _attn_scratch(n_state):
    return [pltpu.VMEM((2, n_state, CK, TQ), F32), pltpu.VMEM((2, n_state, 1, TQ), F32),
            pltpu.VMEM((n_state, 1, TQ), F32), pltpu.VMEM((n_state, VROWS, TQ), F32),
            pltpu.VMEM((BRANCH_W, TQ), F32)]


def _kv_specs(t, w):
    kspec = pl.BlockSpec((None, t, w), lambda bb, i: (bb, 0, 0))
    vspec = pl.BlockSpec((None, t // CK, N_HEADS * VROWS, CK), lambda bb, i: (bb, 0, 0, 0))
    return kspec, vspec


def _ln_kernel(x_ref, g_ref, b_ref, h_ref, hb_ref):
    x = x_ref[...]
    mu = jnp.mean(x, axis=1, keepdims=True)
    xc = x - mu
    var = jnp.mean(xc * xc, axis=1, keepdims=True)
    y = xc * lax.rsqrt(var + LN_EPS) * g_ref[...] + b_ref[...]
    h_ref[...] = y
    hb_ref[...] = y.astype(BF16)


def _layer_norm0(x2, g, b):
    n, d = x2.shape
    tm = 512
    row = pl.BlockSpec((tm, d), lambda i: (i, 0))
    vec = pl.BlockSpec((1, d), lambda i: (0, 0))
    return pl.pallas_call(
        _ln_kernel,
        out_shape=(jax.ShapeDtypeStruct((n, d), F32), jax.ShapeDtypeStruct((n, d), BF16)),
        grid=(n // tm,),
        in_specs=[row, vec, vec],
        out_specs=(row, row),
        compiler_params=_cparams(1),
        name="ln0",
    )(x2, g.reshape(1, d), b.reshape(1, d))


def _proj_plain_kernel(x_ref, w_ref, wt_ref, *out_refs, n_t):
    for g, o_ref in enumerate(out_refs[:n_t]):
        _store_vt(o_ref, _tn_dot(wt_ref[:, g * BRANCH_W:(g + 1) * BRANCH_W], x_ref[...]))
    off = 0
    for o_ref in out_refs[n_t:]:
        wd = o_ref.shape[-1]
        for j in range(0, wd, MXU_N):
            acc = jnp.dot(x_ref[...], w_ref[:, off + j:off + j + MXU_N], preferred_element_type=F32)
            o_ref[:, j:j + MXU_N] = acc.astype(o_ref.dtype)
        off += wd


def _proj_plain(hb3, w, wt, widths, l):
    b, t, d = hb3.shape
    tm = 512
    n_t = wt.shape[-1] // BRANCH_W
    shapes = [jax.ShapeDtypeStruct((b, t // CK, N_HEADS * VROWS, CK), BF16)] * n_t
    specs = [pl.BlockSpec((None, tm // CK, N_HEADS * VROWS, CK), lambda i, bb: (bb, i, 0, 0))] * n_t
    shapes += [jax.ShapeDtypeStruct((b, t, wd), BF16) for wd in widths]
    specs += [pl.BlockSpec((None, tm, wd), lambda i, bb: (bb, i, 0)) for wd in widths]
    return pl.pallas_call(
        functools.partial(_proj_plain_kernel, n_t=n_t),
        out_shape=tuple(shapes),
        grid=(t // tm, b),
        in_specs=[pl.BlockSpec((None, tm, d), lambda i, bb: (bb, i, 0)),
                  _layer_spec(w, l), _layer_spec(wt, l)],
        out_specs=tuple(specs),
        compiler_params=_cparams(2),
        name="proj_plain",
    )(hb3, w, wt)


def _proj_rope_kernel(x_ref, w_ref, c_ref, s_ref, *out_refs, heads, tables):
    lane = lax.broadcasted_iota(I32, (x_ref.shape[0], MXU_N), 1)
    for g, o_ref in enumerate(out_refs):
        hd, half = heads[g]
        sl = slice(g * MXU_N, (g + 1) * MXU_N)
        acc = jnp.dot(x_ref[...], w_ref[:, sl], preferred_element_type=F32)
        partner = jnp.where((lane & (hd - 1)) < half,
                            pltpu.roll(acc, MXU_N - half, 1), pltpu.roll(acc, half, 1))
        o_ref[...] = (acc * c_ref[tables[g]] + partner * s_ref[tables[g]]).astype(o_ref.dtype)


def _proj_rope(hb3, w, ctab, stab, heads, tables, l):
    b, t, d = hb3.shape
    tm = 512
    assert w.shape[-1] == MXU_N * len(heads)
    tspec = pl.BlockSpec((ctab.shape[0], tm, MXU_N), lambda i, bb: (0, i, 0))
    ospec = pl.BlockSpec((None, tm, MXU_N), lambda i, bb: (bb, i, 0))
    return pl.pallas_call(
        functools.partial(_proj_rope_kernel, heads=heads, tables=tables),
        out_shape=(jax.ShapeDtypeStruct((b, t, MXU_N), BF16),) * len(heads),
        grid=(t // tm, b),
        in_specs=[pl.BlockSpec((None, tm, d), lambda i, bb: (bb, i, 0)),
                  _layer_spec(w, l), tspec, tspec],
        out_specs=(ospec,) * len(heads),
        compiler_params=_cparams(2),
        name="proj_rope",
    )(hb3, w, ctab, stab)


def _dsa_kernel(aq_ref, ak_ref, avt_ref, iq_ref, ik_ref, iw_ref, pick_ref, tri_ref, o_ref,
                keys_ref, hi_ref, lo_ref, bk_ref, iqt_ref, aqt_ref, wt_ref, thr_ref, s_ref, mx_ref, m_ref, acc_ref, ot_ref,
                *, topk, idx_scale):
    i = pl.program_id(1)
    n_full = 2 * i
    n_pair = i + 1

    _masked_qt(iq_ref[...].astype(F32), 5, IDX_HEADS, iqt_ref)
    _masked_qt(aq_ref[...].astype(F32) * (HEAD_DIM ** -0.5 * LOG2E), 6, N_HEADS, aqt_ref)
    wt_ref[...] = _nt_dot(pick_ref[...], iw_ref[...]) * idx_scale

    def logits(c):
        kc = ik_ref[pl.ds(pl.multiple_of(c * CK, CK), CK), :]
        return [jnp.dot(_half(kc, hh, 5), iqt_ref[hh], preferred_element_type=F32) for hh in range(IDX_HEADS)]

    def put_keys(c, key):
        keys_ref[c] = key
        hi_ref[c] = (key >> 16).astype(I16)
        lo_ref[c] = ((key & 0xFFFF) - HALF16).astype(I16)

    def score_chunk(c, lg, d):
        sc = jnp.zeros((CK, TQ), F32)
        for hh in range(IDX_HEADS):
            sc = sc + jnp.maximum(lg[hh], 0.0) * wt_ref[hh:hh + 1, :]
        bits = pltpu.bitcast(sc, I32)
        key = jnp.where(bits < 0, INT_MIN - bits, bits)
        put_keys(c, key if d is None else jnp.where(_causal(d), key, INT_MIN))

    def score_pair(c, d0, d1):
        lg0, lg1 = logits(c), logits(c + 1)
        score_chunk(c, lg0, d0)
        score_chunk(c + 1, lg1, d1)

    def score_body(p, carry):
        score_pair(2 * p, None, None)
        return carry

    lax.fori_loop(0, i, score_body, 0)
    score_pair(n_full, 0, 1)

    def pair_loop(body, init):
        def pair(p, carry):
            return body(2 * p + 1, body(2 * p, carry))
        return lax.fori_loop(0, n_pair, pair, init)

    def count16(pred, also=None):
        def body(c, part):
            hit = jnp.where(pred(c), jnp.int16(1), jnp.int16(0))
            if also is not None:
                hit = jnp.where(also(c), hit, jnp.int16(0))
            return part + _fold_rows(hit, 2 * SUBLANES)
        part = pair_loop(body, jnp.zeros((2 * SUBLANES, TQ), I16))
        return jnp.sum(part.astype(F32), axis=0, keepdims=True)

    def search16(ref, need):
        def bit_body(bi, t_u):
            c_u = t_u | jnp.left_shift(jnp.int32(1), 15 - bi)
            ck = (c_u - HALF16).astype(I16)
            cnt = count16(lambda c: ref[c] >= ck)
            return jnp.where(cnt >= need, c_u, t_u)
        return lax.fori_loop(0, 16, bit_body, jnp.zeros((1, TQ), I32))

    hi_u = search16(hi_ref, float(topk))
    thr_hi = (hi_u - HALF16).astype(I16)
    n_above = count16(lambda c: hi_ref[c] > thr_hi)

    def bucket_body(c, carry):
        bk_ref[c] = jnp.where(hi_ref[c] == thr_hi, lo_ref[c], jnp.int16(-HALF16))
        return carry

    pair_loop(bucket_body, 0)
    lo_u = search16(bk_ref, float(topk) - n_above)
    thr_lo = (lo_u - HALF16).astype(I16)
    thr = ((hi_u - HALF16) << 16) | lo_u

    n_gt = n_above + count16(lambda c: bk_ref[c] > thr_lo)
    n_eq = count16(lambda c: lo_ref[c] == thr_lo, also=lambda c: hi_ref[c] == thr_hi)
    need = float(topk) - n_gt
    amb = jnp.logical_and(n_eq > need, thr > INT_MIN)
    any_amb = jnp.max(jnp.where(amb, 1.0, 0.0)) > 0.5

    @pl.when(any_amb)
    def _():
        def drop_body(c, seen):
            k = keys_ref[c]
            eq = k == thr
            eqf = jnp.where(eq, 1.0, 0.0)
            rank = jnp.dot(tri_ref[...], eqf.astype(BF16), preferred_element_type=F32) + seen
            drop = jnp.logical_and(jnp.logical_and(eq, rank > need), amb)
            keys_ref[c] = jnp.where(drop, INT_MIN, k)
            return seen + jnp.sum(eqf, axis=0, keepdims=True)

        pair_loop(drop_body, jnp.zeros((1, TQ), F32))

    thr_ref[...] = jnp.maximum(thr, INT_MIN + 1)

    def qk_all(c):
        kc = ak_ref[pl.ds(pl.multiple_of(c * CK, CK), CK), :]
        return [jnp.dot(_half(kc, h, 6), aqt_ref[h], preferred_element_type=F32) for h in range(N_HEADS)]

    _flash_loop(n_full, qk_all,
                lambda keep, h, s: jnp.where(keep, s, NEG),
                lambda c, h: avt_ref[c, h * VROWS:(h + 1) * VROWS, :],
                (s_ref, mx_ref, m_ref, acc_ref),
                prep=lambda c: keys_ref[c] >= thr_ref[...], causal_tail=False)
    for h in range(N_HEADS):
        ot_ref[h * HEAD_DIM:(h + 1) * HEAD_DIM, :] = _softmax_out(acc_ref.at[h])
    o_ref[...] = ot_ref[...].T.astype(o_ref.dtype)


def _dsa(aq, ak, avt, iq, ik, iw):
    b, t, _ = aq.shape
    topk = min(TOPK_MAX, t // 4)
    qspec = pl.BlockSpec((None, TQ, BRANCH_W), lambda bb, i: (bb, i, 0))
    kspec, vspec = _kv_specs(t, BRANCH_W)
    pick = np.zeros((2 * SUBLANES, MXU_N), np.float32)
    for hh in range(IDX_HEADS):
        pick[hh, KV_LORA + hh] = 1.0
    pick = jnp.asarray(pick, BF16)
    tri = jnp.asarray(np.tril(np.ones((CK, CK), np.float32)), BF16)
    kern = functools.partial(_dsa_kernel, topk=topk, idx_scale=(IDX_HEADS * IDX_DIM) ** -0.5)
    return pl.pallas_call(
        kern,
        out_shape=jax.ShapeDtypeStruct((b, t, BRANCH_W), BF16),
        grid=(b, t // TQ),
        in_specs=[qspec, kspec, vspec, qspec, kspec, qspec,
                  pl.BlockSpec(pick.shape, lambda bb, i: (0, 0)), pl.BlockSpec(tri.shape, lambda bb, i: (0, 0))],
        out_specs=qspec,
        scratch_shapes=[
            pltpu.VMEM((t // CK, CK, TQ), I32),
            pltpu.VMEM((t // CK, CK, TQ), I16),
            pltpu.VMEM((t // CK, CK, TQ), I16),
            pltpu.VMEM((t // CK, CK, TQ), I16),
            pltpu.VMEM((IDX_HEADS, LANES, TQ), BF16),
            pltpu.VMEM((N_HEADS, LANES, TQ), BF16),
            pltpu.VMEM((2 * SUBLANES, TQ), F32),
            pltpu.VMEM((1, TQ), I32),
        ] + _attn_scratch(N_HEADS),
        compiler_params=_cparams(2),
        name="dsa",
    )(aq, ak, avt, iq, ik, iw, pick, tri)


def _kbar_kernel(k_ref, o_ref):
    o_ref[...] = jnp.zeros(o_ref.shape, o_ref.dtype)
    nb = k_ref.shape[0] // MOBA_BLOCK
    for n in range(nb):
        blk = k_ref[n * MOBA_BLOCK:(n + 1) * MOBA_BLOCK, :].astype(F32)
        o_ref[n:n + 1, :] = jnp.mean(blk, axis=0, keepdims=True).astype(o_ref.dtype)


def _kbar(bk):
    b, t, w = bk.shape
    nbp = max(2 * SUBLANES, t // MOBA_BLOCK)
    return pl.pallas_call(
        _kbar_kernel,
        out_shape=jax.ShapeDtypeStruct((b, nbp, w), BF16),
        grid=(b,),
        in_specs=[pl.BlockSpec((None, t, w), lambda bb: (bb, 0, 0))],
        out_specs=pl.BlockSpec((None, nbp, w), lambda bb: (bb, 0, 0)),
        compiler_params=_cparams(1),
        name="moba_kbar",
    )(bk)


def _moba_kernel(q_ref, k_ref, vt_ref, kbar_ref, o_ref, qt_ref, bias_ref, s_ref, mx_ref, m_ref, acc_ref, ot_ref):
    i = pl.program_id(1)
    nbp = kbar_ref.shape[0]
    blk = lax.broadcasted_iota(I32, (nbp, TQ), 0)
    blk_f = blk.astype(F32)
    own = 2 * i + (lax.broadcasted_iota(I32, (nbp, TQ), 1) >> (MOBA_BLOCK.bit_length() - 1))
    _masked_qt(q_ref[...].astype(F32) * (HEAD_DIM ** -0.5 * LOG2E), 6, N_HEADS, qt_ref)

    for h in range(N_HEADS):
        g = jnp.where(blk < own, jnp.dot(_half(kbar_ref[...], h, 6), qt_ref[h], preferred_element_type=F32), NEG)
        bias = jnp.full((nbp, TQ), NEG, F32)
        for _ in range(MOBA_TOPK):
            mx = jnp.max(g, axis=0, keepdims=True)
            first = jnp.min(jnp.where(g == mx, blk_f, 1e9), axis=0, keepdims=True)
            pick = jnp.logical_and(blk_f == first, mx > 0.5 * NEG)
            bias = jnp.where(pick, 0.0, bias)
            g = jnp.where(pick, NEG, g)
        bias_ref[h] = jnp.where(blk == own, 0.0, bias)

    def qk_all(c):
        kc = k_ref[pl.ds(pl.multiple_of(c * CK, CK), CK), :]
        return [jnp.dot(_half(kc, h, 6), qt_ref[h], preferred_element_type=F32) for h in range(N_HEADS)]

    _flash_loop(2 * i, qk_all, lambda c, h, s: s + bias_ref[h, pl.ds(c, 1), :],
                lambda c, h: vt_ref[c, h * VROWS:(h + 1) * VROWS, :], (s_ref, mx_ref, m_ref, acc_ref))
    for h in range(N_HEADS):
        ot_ref[h * HEAD_DIM:(h + 1) * HEAD_DIM, :] = _softmax_out(acc_ref.at[h])
    o_ref[...] = ot_ref[...].T.astype(o_ref.dtype)


def _moba(bq, bk, bvt, kbar):
    b, t, w = bq.shape
    assert TQ == 2 * MOBA_BLOCK and CK == MOBA_BLOCK and t % TQ == 0
    nbp = kbar.shape[1]
    qspec = pl.BlockSpec((None, TQ, w), lambda bb, i: (bb, i, 0))
    kspec, vspec = _kv_specs(t, w)
    return pl.pallas_call(
        _moba_kernel,
        out_shape=jax.ShapeDtypeStruct((b, t, w), BF16),
        grid=(b, t // TQ),
        in_specs=[qspec, kspec, vspec, pl.BlockSpec((None, nbp, w), lambda bb, i: (bb, 0, 0))],
        out_specs=qspec,
        scratch_shapes=[pltpu.VMEM((N_HEADS, LANES, TQ), BF16), pltpu.VMEM((N_HEADS, nbp, TQ), F32)]
        + _attn_scratch(N_HEADS),
        compiler_params=_cparams(2),
        name="moba",
    )(bq, bk, bvt, kbar)


def _diff_kernel(q_ref, k_ref, vt_ref, lam_ref, norm_ref, misc_ref, o_ref,
                 qt_ref, s_ref, mx_ref, m_ref, acc_ref, ot_ref):
    i = pl.program_id(1)
    _masked_qt(q_ref[...].astype(F32) * (DIFF_DIM ** -0.5 * LOG2E), 5, 2 * N_HEADS, qt_ref)

    dl = lam_ref[...]
    lam_init = misc_ref[0:1, 0:1]
    lam = (jnp.exp(jnp.sum(dl[0:1, :] * dl[1:2, :], axis=1, keepdims=True))
           - jnp.exp(jnp.sum(dl[2:3, :] * dl[3:4, :], axis=1, keepdims=True)) + lam_init)

    def qk_all(c):
        kc = k_ref[pl.ds(pl.multiple_of(c * CK, CK), CK), :]
        return [jnp.dot(_half(kc, j, 5), qt_ref[j], preferred_element_type=F32) for j in range(2 * N_HEADS)]

    _flash_loop(2 * i, qk_all, None,
                lambda c, j: vt_ref[c, (j // 2) * VROWS:(j // 2 + 1) * VROWS, :],
                (s_ref, mx_ref, m_ref, acc_ref))

    post = norm_ref[...] * (1.0 - lam_init)
    for h in range(N_HEADS):
        o_h = _softmax_out(acc_ref.at[2 * h]) - lam * _softmax_out(acc_ref.at[2 * h + 1])
        ms = jnp.mean(o_h * o_h, axis=0, keepdims=True)
        ot_ref[h * HEAD_DIM:(h + 1) * HEAD_DIM, :] = o_h * lax.rsqrt(ms + RMS_EPS) * post
    o_ref[...] = ot_ref[...].T.astype(o_ref.dtype)


def _diff(cq, ck, cvt, lam, norm, misc):
    b, t, w = cq.shape
    qspec = pl.BlockSpec((None, TQ, w), lambda bb, i: (bb, i, 0))
    kspec, vspec = _kv_specs(t, w)
    full = lambda a: pl.BlockSpec(a.shape, lambda bb, i: (0,) * a.ndim)
    return pl.pallas_call(
        _diff_kernel,
        out_shape=jax.ShapeDtypeStruct((b, t, w), BF16),
        grid=(b, t // TQ),
        in_specs=[qspec, kspec, vspec, full(lam), full(norm), full(misc)],
        out_specs=qspec,
        scratch_shapes=[pltpu.VMEM((2 * N_HEADS, LANES, TQ), BF16)] + _attn_scratch(2 * N_HEADS),
        compiler_params=_cparams(2),
        name="diff",
    )(cq, ck, cvt, lam, norm, misc)


def _mla_prep_kernel(cq_ref, ckv_ref, kr_ref, qn_ref, kvn_ref, wq_ref, wqr_ref, wk_ref, wvt_ref,
                     p_ref, ct_ref, st_ref, q_out, k_out, vt_out):
    x = cq_ref[...].astype(F32)
    xn = (x * lax.rsqrt(jnp.mean(x * x, axis=1, keepdims=True) + RMS_EPS) * qn_ref[...]).astype(BF16)
    q = (jnp.dot(xn, wq_ref[...], preferred_element_type=F32) * ct_ref[...]
         + jnp.dot(xn, wqr_ref[...], preferred_element_type=F32) * st_ref[...])
    q_out[...] = q.astype(q_out.dtype)
    c = ckv_ref[:, :KV_LORA].astype(F32)
    cn = (c * lax.rsqrt(jnp.mean(c * c, axis=1, keepdims=True) + RMS_EPS) * kvn_ref[...]).astype(BF16)
    k = (jnp.dot(cn, wk_ref[...], preferred_element_type=F32)
         + jnp.dot(kr_ref[...], p_ref[...], preferred_element_type=F32))
    k_out[...] = k.astype(k_out.dtype)
    _store_vt(vt_out, _tn_dot(wvt_ref[...], cn))


def _mla_prep(dcq, ckv, kr, qn, kvn, wq, wqr, wk, wvt, pmat, ct, st, l):
    b, t, _ = dcq.shape
    tm = 512
    hw = N_HEADS * LANES
    row = lambda w: pl.BlockSpec((None, tm, w), lambda i, bb: (bb, i, 0))
    full = lambda a: pl.BlockSpec(a.shape, lambda i, bb: (0,) * a.ndim)
    tab = pl.BlockSpec((tm, hw), lambda i, bb: (i, 0))
    return pl.pallas_call(
        _mla_prep_kernel,
        out_shape=(jax.ShapeDtypeStruct((b, t, hw), BF16), jax.ShapeDtypeStruct((b, t, hw), BF16),
                   jax.ShapeDtypeStruct((b, t // CK, N_HEADS * VROWS, CK), BF16)),
        grid=(t // tm, b),
        in_specs=[row(Q_LORA), row(MXU_N), row(MXU_N), full(qn), full(kvn), _layer_spec(wq, l), _layer_spec(wqr, l),
                  _layer_spec(wk, l), _layer_spec(wvt, l), full(pmat), tab, tab],
        out_specs=(row(hw), row(hw),
                   pl.BlockSpec((None, tm // CK, N_HEADS * VROWS, CK), lambda i, bb: (bb, i, 0, 0))),
        compiler_params=_cparams(2),
        name="mla_prep",
    )(dcq, ckv, kr, qn, kvn, wq, wqr, wk, wvt, pmat, ct, st)


def _mla_kernel(q_ref, k_ref, vt_ref, o_ref, qt_ref, s_ref, mx_ref, m_ref, acc_ref, ot_ref):
    i = pl.program_id(1)
    hs = [slice(h * LANES, (h + 1) * LANES) for h in range(N_HEADS)]
    for h in range(N_HEADS):
        qt_ref[h] = q_ref[:, hs[h]].astype(F32).T.astype(BF16)

    def qk_all(c):
        start = pl.multiple_of(c * CK, CK)
        return [jnp.dot(k_ref[pl.ds(start, CK), hs[h]], qt_ref[h], preferred_element_type=F32)
                for h in range(N_HEADS)]

    _flash_loop(2 * i, qk_all, None,
                lambda c, h: vt_ref[c, h * VROWS:(h + 1) * VROWS, :],
                (s_ref, mx_ref, m_ref, acc_ref))
    for h in range(N_HEADS):
        ot_ref[h * HEAD_DIM:(h + 1) * HEAD_DIM, :] = _softmax_out(acc_ref.at[h])
    o_ref[...] = ot_ref[...].T.astype(o_ref.dtype)


def _mla(qm, km, vmt):
    b, t, hw = qm.shape
    kspec, vspec = _kv_specs(t, hw)
    return pl.pallas_call(
        _mla_kernel,
        out_shape=jax.ShapeDtypeStruct((b, t, BRANCH_W), BF16),
        grid=(b, t // TQ),
        in_specs=[pl.BlockSpec((None, TQ, hw), lambda bb, i: (bb, i, 0)), kspec, vspec],
        out_specs=pl.BlockSpec((None, TQ, BRANCH_W), lambda bb, i: (bb, i, 0)),
        scratch_shapes=[pltpu.VMEM((N_HEADS, LANES, TQ), BF16)] + _attn_scratch(N_HEADS),
        compiler_params=_cparams(2),
        name="mla",
    )(qm, km, vmt)


def _matmul_kernel(x_ref, w_ref, o_ref):
    o_ref[...] = jnp.dot(x_ref[...].astype(BF16), w_ref[...], preferred_element_type=F32).astype(o_ref.dtype)


def _mem_kv(mem, w, l):
    b, m, d = mem.shape
    n = w.shape[-1]
    return pl.pallas_call(
        _matmul_kernel,
        out_shape=jax.ShapeDtypeStruct((b, m, n), BF16),
        grid=(b,),
        in_specs=[pl.BlockSpec((None, m, d), lambda bb: (bb, 0, 0)), _layer_spec(w, l)],
        out_specs=pl.BlockSpec((None, m, n), lambda bb: (bb, 0, 0)),
        compiler_params=_cparams(1),
        name="mem_kv",
    )(mem, w)


def _mem_kernel(q_ref, kv_ref, o_ref):
    tq = q_ref.shape[0]
    lane_q = lax.broadcasted_iota(I32, (tq, BRANCH_W), 1)
    q = q_ref[...].astype(F32) * (HEAD_DIM ** -0.5)
    mk = kv_ref[:, :BRANCH_W]
    mv = kv_ref[:, BRANCH_W:]
    out = jnp.zeros((tq, BRANCH_W), F32)
    for h in range(N_HEADS):
        in_h = (lane_q >> 6) == h
        s = _nt_dot(jnp.where(in_h, q, 0.0).astype(BF16), mk)
        p = jnp.exp(s - jnp.max(s, axis=1, keepdims=True))
        o_h = jnp.dot(p.astype(BF16), mv, preferred_element_type=F32) / jnp.sum(p, axis=1, keepdims=True)
        out = jnp.where(in_h, o_h, out)
    o_ref[...] = out.astype(o_ref.dtype)


def _mem_attn(eq, mkv):
    b, t, w = eq.shape
    m = mkv.shape[1]
    tq = 512
    return pl.pallas_call(
        _mem_kernel,
        out_shape=jax.ShapeDtypeStruct((b, t, w), BF16),
        grid=(b, t // tq),
        in_specs=[pl.BlockSpec((None, tq, w), lambda bb, i: (bb, i, 0)),
                  pl.BlockSpec((None, m, 2 * w), lambda bb, i: (bb, 0, 0))],
        out_specs=pl.BlockSpec((None, tq, w), lambda bb, i: (bb, i, 0)),
        compiler_params=_cparams(2),
        name="mem_attn",
    )(eq, mkv)


def _final_kernel(h_ref, hb_ref, oa_ref, ob_ref, oc_ref, od_ref, oe_ref, z_ref,
                  wg_ref, wb_ref, wo_ref, g_ref, b_ref, h_out, hb_out, *, alpha):
    d = h_ref.shape[1]
    acc = jnp.zeros(h_ref.shape, F32)
    for n, o_ref in enumerate((oa_ref, ob_ref, oc_ref, od_ref, oe_ref)):
        z = z_ref[:, n * BRANCH_W:(n + 1) * BRANCH_W].astype(F32)
        y = o_ref[...].astype(F32) * (z / (1.0 + jnp.exp(-z)))
        u = jnp.dot(y.astype(BF16), wb_ref[n], preferred_element_type=F32)
        g = jnp.dot(hb_ref[...], wg_ref[:, n * d:(n + 1) * d], preferred_element_type=F32)
        acc = acc + u / (1.0 + jnp.exp(-g))
    out = jnp.dot(acc.astype(BF16), wo_ref[...], preferred_element_type=F32)
    x = alpha * h_ref[...] + out
    mu = jnp.mean(x, axis=1, keepdims=True)
    xc = x - mu
    var = jnp.mean(xc * xc, axis=1, keepdims=True)
    y = xc * lax.rsqrt(var + LN_EPS) * g_ref[...] + b_ref[...]
    h_out[...] = y
    hb_out[...] = y.astype(BF16)


def _final(h, hb, os5, z, wg, wb, wo, ln_g, ln_b, alpha, l):
    n, d = h.shape
    tm = 256
    row = lambda w: pl.BlockSpec((tm, w), lambda i: (i, 0))
    full = lambda a: pl.BlockSpec(a.shape, lambda i: (0,) * a.ndim)
    return pl.pallas_call(
        functools.partial(_final_kernel, alpha=alpha),
        out_shape=(jax.ShapeDtypeStruct((n, d), F32), jax.ShapeDtypeStruct((n, d), BF16)),
        grid=(n // tm,),
        in_specs=[row(d), row(d)] + [row(BRANCH_W)] * N_BRANCH + [row(N_BRANCH * BRANCH_W),
                  _layer_spec(wg, l), _layer_spec(wb, l), _layer_spec(wo, l), full(ln_g), full(ln_b)],
        out_specs=(row(d), row(d)),
        compiler_params=_cparams(1),
        name="merge_out_ln",
    )(h, hb, *os5, z, wg, wb, wo, ln_g, ln_b)


ROPE_GROUPS = (("a_q", N_HEADS, HEAD_DIM, ROT_64), ("a_k", N_HEADS, HEAD_DIM, ROT_64),
               ("i_q", IDX_HEADS, IDX_DIM, ROT_32), ("i_k", IDX_HEADS, IDX_DIM, ROT_32),
               ("b_q", N_HEADS, HEAD_DIM, ROT_64), ("b_k", N_HEADS, HEAD_DIM, ROT_64),
               ("c_q", 2 * N_HEADS, DIFF_DIM, ROT_32), ("c_k", 2 * N_HEADS, DIFF_DIM, ROT_32),
               ("d_kr", 1, MXU_N, MLA_ROPE))
PLAIN_COLS = ("d_cq", "d_ckv", "i_w", None, "e_q", "z")
VALUE_COLS = ("a_v", "b_v", "c_v")


def _weight_prep_kernel(w_ref, plain_ref, vt_ref, rope_ref, g_ref):
    rows = w_ref.shape[0]

    def put(ref, names):
        off = 0
        for name in names:
            if name is None:
                part = jnp.zeros((rows, -off % MXU_N), ref.dtype)
            else:
                o, s = OFF[name]
                part = w_ref[:, o:o + s].astype(ref.dtype)
            ref[:, off:off + part.shape[1]] = part
            off += part.shape[1]
        assert off == ref.shape[1]

    put(plain_ref, PLAIN_COLS)
    put(vt_ref, VALUE_COLS)
    rope_cols = []
    for name, nh, hd, _ in ROPE_GROUPS:
        copies = IDX_HEADS if name == "i_k" else 1
        rope_cols += [name] * copies + ([None] if copies * OFF[name][1] < MXU_N else [])
    put(rope_ref, rope_cols)
    put(g_ref, ("g",))


def _weight_prep(w_in):
    depth, d, n = w_in.shape
    tm = 64
    widths = (sum(OFF[c][1] for c in PLAIN_COLS if c) + MXU_N - KV_LORA - IDX_HEADS,
              len(VALUE_COLS) * BRANCH_W, len(ROPE_GROUPS) * MXU_N, OFF["g"][1])
    return pl.pallas_call(
        _weight_prep_kernel,
        out_shape=tuple(jax.ShapeDtypeStruct((depth, d, w), BF16) for w in widths),
        grid=(depth, d // tm),
        in_specs=[pl.BlockSpec((None, tm, n), lambda l, i: (l, i, 0))],
        out_specs=tuple(pl.BlockSpec((None, tm, w), lambda l, i: (l, i, 0)) for w in widths),
        compiler_params=_cparams(2),
        name="weight_prep",
    )(w_in)


def _rope_tables(seq, rot_dim):
    pos = jnp.arange(seq, dtype=F32)
    inv = ROPE_THETA ** (-jnp.arange(0, rot_dim, 2, dtype=F32) / rot_dim)
    ang = pos[:, None] * inv[None, :]
    return jnp.cos(ang), jnp.sin(ang)


def _rope_cs(t, nh, hd, r):
    cos, sin = _rope_tables(t, r)
    c = jnp.concatenate([cos, cos, jnp.ones((t, hd - r), F32)], axis=1)
    s = jnp.concatenate([-sin, sin, jnp.zeros((t, hd - r), F32)], axis=1)
    return jnp.tile(c, (1, nh)), jnp.tile(s, (1, nh))


def kernel(x, mem, ln0_g, ln0_b, w_in, mla_q_norm, w_uq, mla_kv_norm, w_ukv, diff_lam, diff_norm,
           w_mem_kv, w_branch, w_out, ln_g, ln_b):
    b, t, d = x.shape
    depth = w_in.shape[0]
    alpha = (2 * depth) ** 0.25
    assert t % 512 == 0 and d == 1024

    w_plain, w_vt, w_rope, wg = _weight_prep(w_in)
    plain_widths = (BRANCH_W,) * 3 + (N_BRANCH * BRANCH_W,)
    rope_heads = tuple((hd, r // 2) for _, _, hd, r in ROPE_GROUPS)
    patterns = sorted(set((nh, hd, r) for _, nh, hd, r in ROPE_GROUPS))
    rope_tables = tuple(patterns.index((nh, hd, r)) for _, nh, hd, r in ROPE_GROUPS)
    cs = [_rope_cs(t, nh, hd, r) for nh, hd, r in patterns]
    ctab = jnp.stack([c for c, _ in cs])
    stab = jnp.stack([s for _, s in cs])

    uq = w_uq.reshape(depth, Q_LORA, N_HEADS, MLA_NOPE + MLA_ROPE)
    qn_w, qr_w = uq[..., :MLA_NOPE], uq[..., MLA_NOPE:]
    pad32 = jnp.zeros((depth, Q_LORA, N_HEADS, LANES - MLA_NOPE - MLA_ROPE), w_uq.dtype)
    hw = N_HEADS * LANES
    wq = jnp.concatenate([qn_w, qr_w, pad32], axis=-1).reshape(depth, Q_LORA, hw).astype(BF16)
    half = MLA_ROPE // 2
    wq_rot = jnp.concatenate([jnp.zeros_like(qn_w), -qr_w[..., half:], qr_w[..., :half], pad32],
                             axis=-1).reshape(depth, Q_LORA, hw).astype(BF16)
    cos_m, sin_m = _rope_tables(t, MLA_ROPE)
    one = lambda n: jnp.ones((t, n), F32)
    zer = lambda n: jnp.zeros((t, n), F32)
    qs = (MLA_NOPE + MLA_ROPE) ** -0.5 * LOG2E
    ct_q = qs * jnp.tile(jnp.concatenate([one(MLA_NOPE), cos_m, cos_m, one(LANES - MLA_NOPE - MLA_ROPE)], axis=1), (1, N_HEADS))
    st_q = qs * jnp.tile(jnp.concatenate([zer(MLA_NOPE), sin_m, sin_m, zer(LANES - MLA_NOPE - MLA_ROPE)], axis=1), (1, N_HEADS))
    ukv = w_ukv.reshape(depth, KV_LORA, N_HEADS, MLA_NOPE + MLA_V)
    wk = jnp.concatenate([ukv[..., :MLA_NOPE], jnp.zeros((depth, KV_LORA, N_HEADS, LANES - MLA_NOPE), w_ukv.dtype)],
                         axis=-1).reshape(depth, KV_LORA, hw).astype(BF16)
    wvt = ukv[..., MLA_NOPE:].reshape(depth, KV_LORA, N_HEADS * MLA_V).astype(BF16)
    place = np.zeros((MXU_N, hw), np.float32)
    for hh in range(N_HEADS):
        for j in range(MLA_ROPE):
            place[j, hh * LANES + MLA_NOPE + j] = 1.0
    place = jnp.asarray(place, BF16)

    wb = w_branch.astype(BF16)
    wo = w_out.astype(BF16)
    wmem = w_mem_kv.astype(BF16)
    norm_t = jnp.broadcast_to(diff_norm.astype(F32)[:, :, None], (depth, HEAD_DIM, TQ))

    h, hb = _layer_norm0(x.reshape(b * t, d), ln0_g, ln0_b)
    for l in range(depth):
        hb3 = hb.reshape(b, t, d)
        avt, bvt, cvt, dcq, ckv_iw, eq, z = _proj_plain(hb3, w_plain, w_vt, plain_widths, l)
        aq, ak, iq, ik, bq, bk, cq, ck, kr = _proj_rope(hb3, w_rope, ctab, stab, rope_heads, rope_tables, l)

        o_a = _dsa(aq, ak, avt, iq, ik, ckv_iw)
        o_b = _moba(bq, bk, bvt, _kbar(bk))
        lam_init = 0.8 - 0.6 * math.exp(-0.3 * l)
        misc = jnp.full((SUBLANES, LANES), lam_init, F32)
        o_c = _diff(cq, ck, cvt, diff_lam[l].astype(F32), norm_t[l], misc)
        qm, km, vmt = _mla_prep(dcq, ckv_iw, kr, mla_q_norm[l].reshape(1, Q_LORA), mla_kv_norm[l].reshape(1, KV_LORA),
                                wq, wq_rot, wk, wvt, place, ct_q, st_q, l)
        o_d = _mla(qm, km, vmt)
        o_e = _mem_attn(eq, _mem_kv(mem, wmem, l))

        os5 = [o.reshape(b * t, BRANCH_W) for o in (o_a, o_b, o_c, o_d, o_e)]
        h, hb = _final(h, hb, os5, z.reshape(b * t, N_BRANCH * BRANCH_W), wg, wb, wo,
                       ln_g[l].reshape(1, d), ln_b[l].reshape(1, d), alpha, l)
    return h.reshape(b, t, d)
```

```python
import functools
import math

import numpy as np
import jax
import jax.numpy as jnp
from jax import lax
from jax.experimental import pallas as pl
from jax.experimental.pallas import tpu as pltpu

F32 = jnp.float32
BF16 = jnp.bfloat16
I32 = jnp.int32
I16 = jnp.int16

N_HEADS = 4
HEAD_DIM = 64
BRANCH_W = N_HEADS * HEAD_DIM
N_BRANCH = 5
ROPE_THETA = 500000.0
ROT_64 = 16
ROT_32 = 8
IDX_HEADS = 8
IDX_DIM = 32
TOPK_MAX = 256
MOBA_BLOCK = 256
MOBA_TOPK = 3
DIFF_DIM = 32
Q_LORA = 256
KV_LORA = 128
MLA_NOPE = 64
MLA_ROPE = 32
MLA_V = 64
LN_EPS = 1e-5
RMS_EPS = 1e-6

IN_LAYOUT = (
    ("a_q", BRANCH_W), ("a_k", BRANCH_W), ("a_v", BRANCH_W),
    ("i_q", IDX_HEADS * IDX_DIM), ("i_k", IDX_DIM), ("i_w", IDX_HEADS),
    ("b_q", BRANCH_W), ("b_k", BRANCH_W), ("b_v", BRANCH_W),
    ("c_q", BRANCH_W), ("c_k", BRANCH_W), ("c_v", BRANCH_W),
    ("d_cq", Q_LORA), ("d_ckv", KV_LORA), ("d_kr", MLA_ROPE),
    ("e_q", BRANCH_W),
    ("z", N_BRANCH * BRANCH_W),
    ("g", N_BRANCH * 1024),
)

SUBLANES = 8
LANES = 128
MXU_N = 256
TQ = 512
CK = 256
VROWS = HEAD_DIM + 16
FLASH_UNROLL = 4
NEG = -1e30
LOG2E = math.log2(math.e)
INT_MIN = np.int32(-2 ** 31)
HALF16 = 1 << 15
VMEM_LIMIT = 56 * 1024 * 1024


def _offsets():
    off, out = 0, {}
    for name, size in IN_LAYOUT:
        out[name] = (off, size)
        off += size
    return out


OFF = _offsets()


def _nt_dot(a, b):
    return lax.dot_general(a, b, (((1,), (1,)), ((), ())), preferred_element_type=F32)


def _tn_dot(w, x):
    return lax.dot_general(w, x, (((0,), (1,)), ((), ())), preferred_element_type=F32)


def _fold_rows(w, rows=SUBLANES):
    xs = [w[r:r + rows, :] for r in range(0, w.shape[0], rows)]
    while len(xs) > 1:
        xs = [xs[j] + xs[j + 1] for j in range(0, len(xs) - 1, 2)] + ([xs[-1]] if len(xs) % 2 else [])
    return xs[0]


def _masked_qt(q, shift, n, qt_ref):
    qt = q.T
    dim = lax.broadcasted_iota(I32, (LANES, qt.shape[1]), 0)
    for j in range(n):
        half = (j << shift) // LANES
        rows = qt[half * LANES:(half + 1) * LANES, :]
        qt_ref[j] = jnp.where(((dim + half * LANES) >> shift) == j, rows, 0.0).astype(BF16)


def _half(kc, j, shift):
    half = (j << shift) // LANES
    return kc[:, half * LANES:(half + 1) * LANES]


def _cparams(n_axes):
    return pltpu.CompilerParams(dimension_semantics=("arbitrary",) * n_axes,
                                vmem_limit_bytes=VMEM_LIMIT)


def _layer_spec(a, l):
    return pl.BlockSpec((None,) + a.shape[1:], lambda *_: (l,) + (0,) * (a.ndim - 1))


def _softmax_step(s_t, m_tile, vt_h, m_ref, acc_ref):
    m_old = m_ref[...]
    m_new = jnp.maximum(m_old, m_tile)
    alpha = jnp.exp2(m_old - m_new)
    p = jnp.exp2(s_t - m_new)
    acc_ref[...] = alpha * acc_ref[...] + jnp.dot(vt_h, p.astype(BF16), preferred_element_type=F32)
    m_ref[...] = m_new


def _softmax_init(m_ref, acc_ref):
    m_ref[...] = jnp.full(m_ref.shape, NEG, F32)
    acc_ref[...] = jnp.zeros(acc_ref.shape, F32)


def _softmax_out(acc_ref):
    return acc_ref[:HEAD_DIM, :] / acc_ref[HEAD_DIM:HEAD_DIM + 1, :]


def _store_vt(o_ref, vt):
    ones = jnp.ones((VROWS - HEAD_DIM, CK), o_ref.dtype)
    for j in range(o_ref.shape[0]):
        for h in range(N_HEADS):
            o_ref[j, h * VROWS:h * VROWS + HEAD_DIM, :] = (
                vt[h * HEAD_DIM:(h + 1) * HEAD_DIM, j * CK:(j + 1) * CK].astype(o_ref.dtype))
            o_ref[j, h * VROWS + HEAD_DIM:(h + 1) * VROWS, :] = ones


def _flash_loop(n_full, qk_all, mask, vt_rows, state, prep=None, causal_tail=True):
    s_ref, mx_ref, m_ref, acc_ref = state
    n_state = m_ref.shape[0]
    for j in range(n_state):
        _softmax_init(m_ref.at[j], acc_ref.at[j])

    def park(c, slot):
        ctx = c if prep is None else prep(c)
        for j, s in enumerate(qk_all(c)):
            if mask is not None:
                s = mask(ctx, j, s)
            s_ref[slot, j] = s
            mx_ref[slot, j] = jnp.max(s, axis=0, keepdims=True)

    def consume(c, slot, d):
        for j in range(n_state):
            if d is None or not causal_tail:
                s, m_tile = s_ref[slot, j], mx_ref[slot, j]
            else:
                s = jnp.where(_causal(d), s_ref[slot, j], NEG)
                m_tile = jnp.max(s, axis=0, keepdims=True)
            _softmax_step(s, m_tile, vt_rows(c, j), m_ref.at[j], acc_ref.at[j])

    park(0, 0)

    def pair(c):
        park(c + 1, 1)
        consume(c, 0, None)
        park(c + 2, 0)
        consume(c + 1, 1, None)

    def body(g, carry):
        for u in range(0, FLASH_UNROLL, 2):
            pair(FLASH_UNROLL * g + u)
        return carry

    n_group = lax.shift_right_logical(n_full, FLASH_UNROLL.bit_length() - 1)
    lax.fori_loop(0, n_group, body, 0)
    c0 = FLASH_UNROLL * n_group
    for u in range(FLASH_UNROLL // 2 - 1):
        @pl.when(n_full - c0 >= 2 * (u + 1))
        def _(u=u):
            pair(c0 + 2 * u)
    park(n_full + 1, 1)
    consume(n_full, 0, 0)
    consume(n_full + 1, 1, 1)


def _causal(d):
    kpos = lax.broadcasted_iota(I32, (CK, TQ), 0) + d * CK
    return kpos <= lax.broadcasted_iota(I32, (CK, TQ), 1)


def _attn_scratch(n_state):
    return [pltpu.VMEM((2, n_state, CK, TQ), F32), pltpu.VMEM((2, n_state, 1, TQ), F32),
            pltpu.VMEM((n_state, 1, TQ), F32), pltpu.VMEM((n_state, VROWS, TQ), F32),
            pltpu.VMEM((BRANCH_W, TQ), F32)]


def _kv_specs(t, w):
    kspec = pl.BlockSpec((None, t, w), lambda bb, i: (bb, 0, 0))
    vspec = pl.BlockSpec((None, t // CK, N_HEADS * VROWS, CK), lambda bb, i: (bb, 0, 0, 0))
    return kspec, vspec


def _ln_kernel(x_ref, g_ref, b_ref, h_ref, hb_ref):
    x = x_ref[...]
    mu = jnp.mean(x, axis=1, keepdims=True)
    xc = x - mu
    var = jnp.mean(xc * xc, axis=1, keepdims=True)
    y = xc * lax.rsqrt(var + LN_EPS) * g_ref[...] + b_ref[...]
    h_ref[...] = y
    hb_ref[...] = y.astype(BF16)


def _layer_norm0(x2, g, b):
    n, d = x2.shape
    tm = 512
    row = pl.BlockSpec((tm, d), lambda i: (i, 0))
    vec = pl.BlockSpec((1, d), lambda i: (0, 0))
    return pl.pallas_call(
        _ln_kernel,
        out_shape=(jax.ShapeDtypeStruct((n, d), F32), jax.ShapeDtypeStruct((n, d), BF16)),
        grid=(n // tm,),
        in_specs=[row, vec, vec],
        out_specs=(row, row),
        compiler_params=_cparams(1),
        name="ln0",
    )(x2, g.reshape(1, d), b.reshape(1, d))


def _proj_plain_kernel(x_ref, w_ref, wt_ref, *out_refs, n_t):
    for g, o_ref in enumerate(out_refs[:n_t]):
        _store_vt(o_ref, _tn_dot(wt_ref[:, g * BRANCH_W:(g + 1) * BRANCH_W], x_ref[...]))
    off = 0
    for o_ref in out_refs[n_t:]:
        wd = o_ref.shape[-1]
        for j in range(0, wd, MXU_N):
            acc = jnp.dot(x_ref[...], w_ref[:, off + j:off + j + MXU_N], preferred_element_type=F32)
            o_ref[:, j:j + MXU_N] = acc.astype(o_ref.dtype)
        off += wd


def _proj_plain(hb3, w, wt, widths, l):
    b, t, d = hb3.shape
    tm = 512
    n_t = wt.shape[-1] // BRANCH_W
    shapes = [jax.ShapeDtypeStruct((b, t // CK, N_HEADS * VROWS, CK), BF16)] * n_t
    specs = [pl.BlockSpec((None, tm // CK, N_HEADS * VROWS, CK), lambda i, bb: (bb, i, 0, 0))] * n_t
    shapes += [jax.ShapeDtypeStruct((b, t, wd), BF16) for wd in widths]
    specs += [pl.BlockSpec((None, tm, wd), lambda i, bb: (bb, i, 0)) for wd in widths]
    return pl.pallas_call(
        functools.partial(_proj_plain_kernel, n_t=n_t),
        out_shape=tuple(shapes),
        grid=(t // tm, b),
        in_specs=[pl.BlockSpec((None, tm, d), lambda i, bb: (bb, i, 0)),
                  _layer_spec(w, l), _layer_spec(wt, l)],
        out_specs=tuple(specs),
        compiler_params=_cparams(2),
        name="proj_plain",
    )(hb3, w, wt)


def _proj_rope_kernel(x_ref, w_ref, c_ref, s_ref, *out_refs, heads, tables):
    lane = lax.broadcasted_iota(I32, (x_ref.shape[0], MXU_N), 1)
    for g, o_ref in enumerate(out_refs):
        hd, half = heads[g]
        sl = slice(g * MXU_N, (g + 1) * MXU_N)
        acc = jnp.dot(x_ref[...], w_ref[:, sl], preferred_element_type=F32)
        partner = jnp.where((lane & (hd - 1)) < half,
                            pltpu.roll(acc, MXU_N - half, 1), pltpu.roll(acc, half, 1))
        o_ref[...] = (acc * c_ref[tables[g]] + partner * s_ref[tables[g]]).astype(o_ref.dtype)


def _proj_rope(hb3, w, ctab, stab, heads, tables, l):
    b, t, d = hb3.shape
    tm = 512
    assert w.shape[-1] == MXU_N * len(heads)
    tspec = pl.BlockSpec((ctab.shape[0], tm, MXU_N), lambda i, bb: (0, i, 0))
    ospec = pl.BlockSpec((None, tm, MXU_N), lambda i, bb: (bb, i, 0))
    return pl.pallas_call(
        functools.partial(_proj_rope_kernel, heads=heads, tables=tables),
        out_shape=(jax.ShapeDtypeStruct((b, t, MXU_N), BF16),) * len(heads),
        grid=(t // tm, b),
        in_specs=[pl.BlockSpec((None, tm, d), lambda i, bb: (bb, i, 0)),
                  _layer_spec(w, l), tspec, tspec],
        out_specs=(ospec,) * len(heads),
        compiler_params=_cparams(2),
        name="proj_rope",
    )(hb3, w, ctab, stab)


def _dsa_kernel(aq_ref, ak_ref, avt_ref, iq_ref, ik_ref, iw_ref, pick_ref, tri_ref, o_ref,
                keys_ref, hi_ref, lo_ref, bk_ref, iqt_ref, aqt_ref, wt_ref, thr_ref, s_ref, mx_ref, m_ref, acc_ref, ot_ref,
                *, topk, idx_scale):
    i = pl.program_id(1)
    n_full = 2 * i
    n_pair = i + 1

    _masked_qt(iq_ref[...].astype(F32), 5, IDX_HEADS, iqt_ref)
    _masked_qt(aq_ref[...].astype(F32) * (HEAD_DIM ** -0.5 * LOG2E), 6, N_HEADS, aqt_ref)
    wt_ref[...] = _nt_dot(pick_ref[...], iw_ref[...]) * idx_scale

    def logits(c):
        kc = ik_ref[pl.ds(pl.multiple_of(c * CK, CK), CK), :]
        return [jnp.dot(_half(kc, hh, 5), iqt_ref[hh], preferred_element_type=F32) for hh in range(IDX_HEADS)]

    def put_keys(c, key):
        keys_ref[c] = key
        hi_ref[c] = (key >> 16).astype(I16)
        lo_ref[c] = ((key & 0xFFFF) - HALF16).astype(I16)

    def score_chunk(c, lg, d):
        sc = jnp.zeros((CK, TQ), F32)
        for hh in range(IDX_HEADS):
            sc = sc + jnp.maximum(lg[hh], 0.0) * wt_ref[hh:hh + 1, :]
        bits = pltpu.bitcast(sc, I32)
        key = jnp.where(bits < 0, INT_MIN - bits, bits)
        put_keys(c, key if d is None else jnp.where(_causal(d), key, INT_MIN))

    def score_pair(c, d0, d1):
        lg0, lg1 = logits(c), logits(c + 1)
        score_chunk(c, lg0, d0)
        score_chunk(c + 1, lg1, d1)

    def score_body(p, carry):
        score_pair(2 * p, None, None)
        return carry

    lax.fori_loop(0, i, score_body, 0)
    score_pair(n_full, 0, 1)

    def pair_loop(body, init):
        def pair(p, carry):
            return body(2 * p + 1, body(2 * p, carry))
        return lax.fori_loop(0, n_pair, pair, init)

    def count16(pred, also=None):
        def body(c, part):
            hit = jnp.where(pred(c), jnp.int16(1), jnp.int16(0))
            if also is not None:
                hit = jnp.where(also(c), hit, jnp.int16(0))
            return part + _fold_rows(hit, 2 * SUBLANES)
        part = pair_loop(body, jnp.zeros((2 * SUBLANES, TQ), I16))
        return jnp.sum(part.astype(F32), axis=0, keepdims=True)

    def search16(ref, need):
        def bit_body(bi, t_u):
            c_u = t_u | jnp.left_shift(jnp.int32(1), 15 - bi)
            ck = (c_u - HALF16).astype(I16)
            cnt = count16(lambda c: ref[c] >= ck)
            return jnp.where(cnt >= need, c_u, t_u)
        return lax.fori_loop(0, 16, bit_body, jnp.zeros((1, TQ), I32))

    hi_u = search16(hi_ref, float(topk))
    thr_hi = (hi_u - HALF16).astype(I16)
    n_above = count16(lambda c: hi_ref[c] > thr_hi)

    def bucket_body(c, carry):
        bk_ref[c] = jnp.where(hi_ref[c] == thr_hi, lo_ref[c], jnp.int16(-HALF16))
        return carry

    pair_loop(bucket_body, 0)
    lo_u = search16(bk_ref, float(topk) - n_above)
    thr_lo = (lo_u - HALF16).astype(I16)
    thr = ((hi_u - HALF16) << 16) | lo_u

    n_gt = n_above + count16(lambda c: bk_ref[c] > thr_lo)
    n_eq = count16(lambda c: lo_ref[c] == thr_lo, also=lambda c: hi_ref[c] == thr_hi)
    need = float(topk) - n_gt
    amb = jnp.logical_and(n_eq > need, thr > INT_MIN)
    any_amb = jnp.max(jnp.where(amb, 1.0, 0.0)) > 0.5

    @pl.when(any_amb)
    def _():
        def drop_body(c, seen):
            k = keys_ref[c]
            eq = k == thr
            eqf = jnp.where(eq, 1.0, 0.0)
            rank = jnp.dot(tri_ref[...], eqf.astype(BF16), preferred_element_type=F32) + seen
            drop = jnp.logical_and(jnp.logical_and(eq, rank > need), amb)
            keys_ref[c] = jnp.where(drop, INT_MIN, k)
            return seen + jnp.sum(eqf, axis=0, keepdims=True)

        pair_loop(drop_body, jnp.zeros((1, TQ), F32))

    thr_ref[...] = jnp.maximum(thr, INT_MIN + 1)

    def qk_all(c):
        kc = ak_ref[pl.ds(pl.multiple_of(c * CK, CK), CK), :]
        return [jnp.dot(_half(kc, h, 6), aqt_ref[h], preferred_element_type=F32) for h in range(N_HEADS)]

    _flash_loop(n_full, qk_all,
                lambda keep, h, s: jnp.where(keep, s, NEG),
                lambda c, h: avt_ref[c, h * VROWS:(h + 1) * VROWS, :],
                (s_ref, mx_ref, m_ref, acc_ref),
                prep=lambda c: keys_ref[c] >= thr_ref[...], causal_tail=False)
    for h in range(N_HEADS):
        ot_ref[h * HEAD_DIM:(h + 1) * HEAD_DIM, :] = _softmax_out(acc_ref.at[h])
    o_ref[...] = ot_ref[...].T.astype(o_ref.dtype)


def _dsa(aq, ak, avt, iq, ik, iw):
    b, t, _ = aq.shape
    topk = min(TOPK_MAX, t // 4)
    qspec = pl.BlockSpec((None, TQ, BRANCH_W), lambda bb, i: (bb, i, 0))
    kspec, vspec = _kv_specs(t, BRANCH_W)
    pick = np.zeros((2 * SUBLANES, MXU_N), np.float32)
    for hh in range(IDX_HEADS):
        pick[hh, KV_LORA + hh] = 1.0
    pick = jnp.asarray(pick, BF16)
    tri = jnp.asarray(np.tril(np.ones((CK, CK), np.float32)), BF16)
    kern = functools.partial(_dsa_kernel, topk=topk, idx_scale=(IDX_HEADS * IDX_DIM) ** -0.5)
    return pl.pallas_call(
        kern,
        out_shape=jax.ShapeDtypeStruct((b, t, BRANCH_W), BF16),
        grid=(b, t // TQ),
        in_specs=[qspec, kspec, vspec, qspec, kspec, qspec,
                  pl.BlockSpec(pick.shape, lambda bb, i: (0, 0)), pl.BlockSpec(tri.shape, lambda bb, i: (0, 0))],
        out_specs=qspec,
        scratch_shapes=[
            pltpu.VMEM((t // CK, CK, TQ), I32),
            pltpu.VMEM((t // CK, CK, TQ), I16),
            pltpu.VMEM((t // CK, CK, TQ), I16),
            pltpu.VMEM((t // CK, CK, TQ), I16),
            pltpu.VMEM((IDX_HEADS, LANES, TQ), BF16),
            pltpu.VMEM((N_HEADS, LANES, TQ), BF16),
            pltpu.VMEM((2 * SUBLANES, TQ), F32),
            pltpu.VMEM((1, TQ), I32),
        ] + _attn_scratch(N_HEADS),
        compiler_params=_cparams(2),
        name="dsa",
    )(aq, ak, avt, iq, ik, iw, pick, tri)


def _kbar_kernel(k_ref, o_ref):
    o_ref[...] = jnp.zeros(o_ref.shape, o_ref.dtype)
    nb = k_ref.shape[0] // MOBA_BLOCK
    for n in range(nb):
        blk = k_ref[n * MOBA_BLOCK:(n + 1) * MOBA_BLOCK, :].astype(F32)
        o_ref[n:n + 1, :] = jnp.mean(blk, axis=0, keepdims=True).astype(o_ref.dtype)


def _kbar(bk):
    b, t, w = bk.shape
    nbp = max(2 * SUBLANES, t // MOBA_BLOCK)
    return pl.pallas_call(
        _kbar_kernel,
        out_shape=jax.ShapeDtypeStruct((b, nbp, w), BF16),
        grid=(b,),
        in_specs=[pl.BlockSpec((None, t, w), lambda bb: (bb, 0, 0))],
        out_specs=pl.BlockSpec((None, nbp, w), lambda bb: (bb, 0, 0)),
        compiler_params=_cparams(1),
        name="moba_kbar",
    )(bk)


def _moba_kernel(q_ref, k_ref, vt_ref, kbar_ref, o_ref, qt_ref, bias_ref, s_ref, mx_ref, m_ref, acc_ref, ot_ref):
    i = pl.program_id(1)
    nbp = kbar_ref.shape[0]
    blk = lax.broadcasted_iota(I32, (nbp, TQ), 0)
    blk_f = blk.astype(F32)
    own = 2 * i + (lax.broadcasted_iota(I32, (nbp, TQ), 1) >> (MOBA_BLOCK.bit_length() - 1))
    _masked_qt(q_ref[...].astype(F32) * (HEAD_DIM ** -0.5 * LOG2E), 6, N_HEADS, qt_ref)

    for h in range(N_HEADS):
        g = jnp.where(blk < own, jnp.dot(_half(kbar_ref[...], h, 6), qt_ref[h], preferred_element_type=F32), NEG)
        bias = jnp.full((nbp, TQ), NEG, F32)
        for _ in range(MOBA_TOPK):
            mx = jnp.max(g, axis=0, keepdims=True)
            first = jnp.min(jnp.where(g == mx, blk_f, 1e9), axis=0, keepdims=True)
            pick = jnp.logical_and(blk_f == first, mx > 0.5 * NEG)
            bias = jnp.where(pick, 0.0, bias)
            g = jnp.where(pick, NEG, g)
        bias_ref[h] = jnp.where(blk == own, 0.0, bias)

    def qk_all(c):
        kc = k_ref[pl.ds(pl.multiple_of(c * CK, CK), CK), :]
        return [jnp.dot(_half(kc, h, 6), qt_ref[h], preferred_element_type=F32) for h in range(N_HEADS)]

    _flash_loop(2 * i, qk_all, lambda c, h, s: s + bias_ref[h, pl.ds(c, 1), :],
                lambda c, h: vt_ref[c, h * VROWS:(h + 1) * VROWS, :], (s_ref, mx_ref, m_ref, acc_ref))
    for h in range(N_HEADS):
        ot_ref[h * HEAD_DIM:(h + 1) * HEAD_DIM, :] = _softmax_out(acc_ref.at[h])
    o_ref[...] = ot_ref[...].T.astype(o_ref.dtype)


def _moba(bq, bk, bvt, kbar):
    b, t, w = bq.shape
    assert TQ == 2 * MOBA_BLOCK and CK == MOBA_BLOCK and t % TQ == 0
    nbp = kbar.shape[1]
    qspec = pl.BlockSpec((None, TQ, w), lambda bb, i: (bb, i, 0))
    kspec, vspec = _kv_specs(t, w)
    return pl.pallas_call(
        _moba_kernel,
        out_shape=jax.ShapeDtypeStruct((b, t, w), BF16),
        grid=(b, t // TQ),
        in_specs=[qspec, kspec, vspec, pl.BlockSpec((None, nbp, w), lambda bb, i: (bb, 0, 0))],
        out_specs=qspec,
        scratch_shapes=[pltpu.VMEM((N_HEADS, LANES, TQ), BF16), pltpu.VMEM((N_HEADS, nbp, TQ), F32)]
        + _attn_scratch(N_HEADS),
        compiler_params=_cparams(2),
        name="moba",
    )(bq, bk, bvt, kbar)


def _diff_kernel(q_ref, k_ref, vt_ref, lam_ref, norm_ref, misc_ref, o_ref,
                 qt_ref, s_ref, mx_ref, m_ref, acc_ref, ot_ref):
    i = pl.program_id(1)
    _masked_qt(q_ref[...].astype(F32) * (DIFF_DIM ** -0.5 * LOG2E), 5, 2 * N_HEADS, qt_ref)

    dl = lam_ref[...]
    lam_init = misc_ref[0:1, 0:1]
    lam = (jnp.exp(jnp.sum(dl[0:1, :] * dl[1:2, :], axis=1, keepdims=True))
           - jnp.exp(jnp.sum(dl[2:3, :] * dl[3:4, :], axis=1, keepdims=True)) + lam_init)

    def qk_all(c):
        kc = k_ref[pl.ds(pl.multiple_of(c * CK, CK), CK), :]
        return [jnp.dot(_half(kc, j, 5), qt_ref[j], preferred_element_type=F32) for j in range(2 * N_HEADS)]

    _flash_loop(2 * i, qk_all, None,
                lambda c, j: vt_ref[c, (j // 2) * VROWS:(j // 2 + 1) * VROWS, :],
                (s_ref, mx_ref, m_ref, acc_ref))

    post = norm_ref[...] * (1.0 - lam_init)
    for h in range(N_HEADS):
        o_h = _softmax_out(acc_ref.at[2 * h]) - lam * _softmax_out(acc_ref.at[2 * h + 1])
        ms = jnp.mean(o_h * o_h, axis=0, keepdims=True)
        ot_ref[h * HEAD_DIM:(h + 1) * HEAD_DIM, :] = o_h * lax.rsqrt(ms + RMS_EPS) * post
    o_ref[...] = ot_ref[...].T.astype(o_ref.dtype)


def _diff(cq, ck, cvt, lam, norm, misc):
    b, t, w = cq.shape
    qspec = pl.BlockSpec((None, TQ, w), lambda bb, i: (bb, i, 0))
    kspec, vspec = _kv_specs(t, w)
    full = lambda a: pl.BlockSpec(a.shape, lambda bb, i: (0,) * a.ndim)
    return pl.pallas_call(
        _diff_kernel,
        out_shape=jax.ShapeDtypeStruct((b, t, w), BF16),
        grid=(b, t // TQ),
        in_specs=[qspec, kspec, vspec, full(lam), full(norm), full(misc)],
        out_specs=qspec,
        scratch_shapes=[pltpu.VMEM((2 * N_HEADS, LANES, TQ), BF16)] + _attn_scratch(2 * N_HEADS),
        compiler_params=_cparams(2),
        name="diff",
    )(cq, ck, cvt, lam, norm, misc)


def _mla_prep_kernel(cq_ref, ckv_ref, kr_ref, qn_ref, kvn_ref, wq_ref, wqr_ref, wk_ref, wvt_ref,
                     p_ref, ct_ref, st_ref, q_out, k_out, vt_out):
    x = cq_ref[...].astype(F32)
    xn = (x * lax.rsqrt(jnp.mean(x * x, axis=1, keepdims=True) + RMS_EPS) * qn_ref[...]).astype(BF16)
    q = (jnp.dot(xn, wq_ref[...], preferred_element_type=F32) * ct_ref[...]
         + jnp.dot(xn, wqr_ref[...], preferred_element_type=F32) * st_ref[...])
    q_out[...] = q.astype(q_out.dtype)
    c = ckv_ref[:, :KV_LORA].astype(F32)
    cn = (c * lax.rsqrt(jnp.mean(c * c, axis=1, keepdims=True) + RMS_EPS) * kvn_ref[...]).astype(BF16)
    k = (jnp.dot(cn, wk_ref[...], preferred_element_type=F32)
         + jnp.dot(kr_ref[...], p_ref[...], preferred_element_type=F32))
    k_out[...] = k.astype(k_out.dtype)
    _store_vt(vt_out, _tn_dot(wvt_ref[...], cn))


def _mla_prep(dcq, ckv, kr, qn, kvn, wq, wqr, wk, wvt, pmat, ct, st, l):
    b, t, _ = dcq.shape
    tm = 512
    hw = N_HEADS * LANES
    row = lambda w: pl.BlockSpec((None, tm, w), lambda i, bb: (bb, i, 0))
    full = lambda a: pl.BlockSpec(a.shape, lambda i, bb: (0,) * a.ndim)
    tab = pl.BlockSpec((tm, hw), lambda i, bb: (i, 0))
    return pl.pallas_call(
        _mla_prep_kernel,
        out_shape=(jax.ShapeDtypeStruct((b, t, hw), BF16), jax.ShapeDtypeStruct((b, t, hw), BF16),
                   jax.ShapeDtypeStruct((b, t // CK, N_HEADS * VROWS, CK), BF16)),
        grid=(t // tm, b),
        in_specs=[row(Q_LORA), row(MXU_N), row(MXU_N), full(qn), full(kvn), _layer_spec(wq, l), _layer_spec(wqr, l),
                  _layer_spec(wk, l), _layer_spec(wvt, l), full(pmat), tab, tab],
        out_specs=(row(hw), row(hw),
                   pl.BlockSpec((None, tm // CK, N_HEADS * VROWS, CK), lambda i, bb: (bb, i, 0, 0))),
        compiler_params=_cparams(2),
        name="mla_prep",
    )(dcq, ckv, kr, qn, kvn, wq, wqr, wk, wvt, pmat, ct, st)


def _mla_kernel(q_ref, k_ref, vt_ref, o_ref, qt_ref, s_ref, mx_ref, m_ref, acc_ref, ot_ref):
    i = pl.program_id(1)
    hs = [slice(h * LANES, (h + 1) * LANES) for h in range(N_HEADS)]
    for h in range(N_HEADS):
        qt_ref[h] = q_ref[:, hs[h]].astype(F32).T.astype(BF16)

    def qk_all(c):
        start = pl.multiple_of(c * CK, CK)
        return [jnp.dot(k_ref[pl.ds(start, CK), hs[h]], qt_ref[h], preferred_element_type=F32)
                for h in range(N_HEADS)]

    _flash_loop(2 * i, qk_all, None,
                lambda c, h: vt_ref[c, h * VROWS:(h + 1) * VROWS, :],
                (s_ref, mx_ref, m_ref, acc_ref))
    for h in range(N_HEADS):
        ot_ref[h * HEAD_DIM:(h + 1) * HEAD_DIM, :] = _softmax_out(acc_ref.at[h])
    o_ref[...] = ot_ref[...].T.astype(o_ref.dtype)


def _mla(qm, km, vmt):
    b, t, hw = qm.shape
    kspec, vspec = _kv_specs(t, hw)
    return pl.pallas_call(
        _mla_kernel,
        out_shape=jax.ShapeDtypeStruct((b, t, BRANCH_W), BF16),
        grid=(b, t // TQ),
        in_specs=[pl.BlockSpec((None, TQ, hw), lambda bb, i: (bb, i, 0)), kspec, vspec],
        out_specs=pl.BlockSpec((None, TQ, BRANCH_W), lambda bb, i: (bb, i, 0)),
        scratch_shapes=[pltpu.VMEM((N_HEADS, LANES, TQ), BF16)] + _attn_scratch(N_HEADS),
        compiler_params=_cparams(2),
        name="mla",
    )(qm, km, vmt)


def _matmul_kernel(x_ref, w_ref, o_ref):
    o_ref[...] = jnp.dot(x_ref[...].astype(BF16), w_ref[...], preferred_element_type=F32).astype(o_ref.dtype)


def _mem_kv(mem, w, l):
    b, m, d = mem.shape
    n = w.shape[-1]
    return pl.pallas_call(
        _matmul_kernel,
        out_shape=jax.ShapeDtypeStruct((b, m, n), BF16),
        grid=(b,),
        in_specs=[pl.BlockSpec((None, m, d), lambda bb: (bb, 0, 0)), _layer_spec(w, l)],
        out_specs=pl.BlockSpec((None, m, n), lambda bb: (bb, 0, 0)),
        compiler_params=_cparams(1),
        name="mem_kv",
    )(mem, w)


def _mem_kernel(q_ref, kv_ref, o_ref):
    tq = q_ref.shape[0]
    lane_q = lax.broadcasted_iota(I32, (tq, BRANCH_W), 1)
    q = q_ref[...].astype(F32) * (HEAD_DIM ** -0.5)
    mk = kv_ref[:, :BRANCH_W]
    mv = kv_ref[:, BRANCH_W:]
    out = jnp.zeros((tq, BRANCH_W), F32)
    for h in range(N_HEADS):
        in_h = (lane_q >> 6) == h
        s = _nt_dot(jnp.where(in_h, q, 0.0).astype(BF16), mk)
        p = jnp.exp(s - jnp.max(s, axis=1, keepdims=True))
        o_h = jnp.dot(p.astype(BF16), mv, preferred_element_type=F32) / jnp.sum(p, axis=1, keepdims=True)
        out = jnp.where(in_h, o_h, out)
    o_ref[...] = out.astype(o_ref.dtype)


def _mem_attn(eq, mkv):
    b, t, w = eq.shape
    m = mkv.shape[1]
    tq = 512
    return pl.pallas_call(
        _mem_kernel,
        out_shape=jax.ShapeDtypeStruct((b, t, w), BF16),
        grid=(b, t // tq),
        in_specs=[pl.BlockSpec((None, tq, w), lambda bb, i: (bb, i, 0)),
                  pl.BlockSpec((None, m, 2 * w), lambda bb, i: (bb, 0, 0))],
        out_specs=pl.BlockSpec((None, tq, w), lambda bb, i: (bb, i, 0)),
        compiler_params=_cparams(2),
        name="mem_attn",
    )(eq, mkv)


def _final_kernel(h_ref, hb_ref, oa_ref, ob_ref, oc_ref, od_ref, oe_ref, z_ref,
                  wg_ref, wb_ref, wo_ref, g_ref, b_ref, h_out, hb_out, *, alpha):
    d = h_ref.shape[1]
    acc = jnp.zeros(h_ref.shape, F32)
    for n, o_ref in enumerate((oa_ref, ob_ref, oc_ref, od_ref, oe_ref)):
        z = z_ref[:, n * BRANCH_W:(n + 1) * BRANCH_W].astype(F32)
        y = o_ref[...].astype(F32) * (z / (1.0 + jnp.exp(-z)))
        u = jnp.dot(y.astype(BF16), wb_ref[n], preferred_element_type=F32)
        g = jnp.dot(hb_ref[...], wg_ref[:, n * d:(n + 1) * d], preferred_element_type=F32)
        acc = acc + u / (1.0 + jnp.exp(-g))
    out = jnp.dot(acc.astype(BF16), wo_ref[...], preferred_element_type=F32)
    x = alpha * h_ref[...] + out
    mu = jnp.mean(x, axis=1, keepdims=True)
    xc = x - mu
    var = jnp.mean(xc * xc, axis=1, keepdims=True)
    y = xc * lax.rsqrt(var + LN_EPS) * g_ref[...] + b_ref[...]
    h_out[...] = y
    hb_out[...] = y.astype(BF16)


def _final(h, hb, os5, z, wg, wb, wo, ln_g, ln_b, alpha, l):
    n, d = h.shape
    tm = 256
    row = lambda w: pl.BlockSpec((tm, w), lambda i: (i, 0))
    full = lambda a: pl.BlockSpec(a.shape, lambda i: (0,) * a.ndim)
    return pl.pallas_call(
        functools.partial(_final_kernel, alpha=alpha),
        out_shape=(jax.ShapeDtypeStruct((n, d), F32), jax.ShapeDtypeStruct((n, d), BF16)),
        grid=(n // tm,),
        in_specs=[row(d), row(d)] + [row(BRANCH_W)] * N_BRANCH + [row(N_BRANCH * BRANCH_W),
                  _layer_spec(wg, l), _layer_spec(wb, l), _layer_spec(wo, l), full(ln_g), full(ln_b)],
        out_specs=(row(d), row(d)),
        compiler_params=_cparams(1),
        name="merge_out_ln",
    )(h, hb, *os5, z, wg, wb, wo, ln_g, ln_b)


ROPE_GROUPS = (("a_q", N_HEADS, HEAD_DIM, ROT_64), ("a_k", N_HEADS, HEAD_DIM, ROT_64),
               ("i_q", IDX_HEADS, IDX_DIM, ROT_32), ("i_k", IDX_HEADS, IDX_DIM, ROT_32),
               ("b_q", N_HEADS, HEAD_DIM, ROT_64), ("b_k", N_HEADS, HEAD_DIM, ROT_64),
               ("c_q", 2 * N_HEADS, DIFF_DIM, ROT_32), ("c_k", 2 * N_HEADS, DIFF_DIM, ROT_32),
               ("d_kr", 1, MXU_N, MLA_ROPE))
PLAIN_COLS = ("d_cq", "d_ckv", "i_w", None, "e_q", "z")
VALUE_COLS = ("a_v", "b_v", "c_v")


def _weight_prep_kernel(w_ref, plain_ref, vt_ref, rope_ref, g_ref):
    rows = w_ref.shape[0]

    def put(ref, names):
        off = 0
        for name in names:
            if name is None:
                part = jnp.zeros((rows, -off % MXU_N), ref.dtype)
            else:
                o, s = OFF[name]
                part = w_ref[:, o:o + s].astype(ref.dtype)
            ref[:, off:off + part.shape[1]] = part
            off += part.shape[1]
        assert off == ref.shape[1]

    put(plain_ref, PLAIN_COLS)
    put(vt_ref, VALUE_COLS)
    rope_cols = []
    for name, nh, hd, _ in ROPE_GROUPS:
        copies = IDX_HEADS if name == "i_k" else 1
        rope_cols += [name] * copies + ([None] if copies * OFF[name][1] < MXU_N else [])
    put(rope_ref, rope_cols)
    put(g_ref, ("g",))


def _weight_prep(w_in):
    depth, d, n = w_in.shape
    tm = 64
    widths = (sum(OFF[c][1] for c in PLAIN_COLS if c) + MXU_N - KV_LORA - IDX_HEADS,
              len(VALUE_COLS) * BRANCH_W, len(ROPE_GROUPS) * MXU_N, OFF["g"][1])
    return pl.pallas_call(
        _weight_prep_kernel,
        out_shape=tuple(jax.ShapeDtypeStruct((depth, d, w), BF16) for w in widths),
        grid=(depth, d // tm),
        in_specs=[pl.BlockSpec((None, tm, n), lambda l, i: (l, i, 0))],
        out_specs=tuple(pl.BlockSpec((None, tm, w), lambda l, i: (l, i, 0)) for w in widths),
        compiler_params=_cparams(2),
        name="weight_prep",
    )(w_in)


def _rope_tables(seq, rot_dim):
    pos = jnp.arange(seq, dtype=F32)
    inv = ROPE_THETA ** (-jnp.arange(0, rot_dim, 2, dtype=F32) / rot_dim)
    ang = pos[:, None] * inv[None, :]
    return jnp.cos(ang), jnp.sin(ang)


def _rope_cs(t, nh, hd, r):
    cos, sin = _rope_tables(t, r)
    c = jnp.concatenate([cos, cos, jnp.ones((t, hd - r), F32)], axis=1)
    s = jnp.concatenate([-sin, sin, jnp.zeros((t, hd - r), F32)], axis=1)
    return jnp.tile(c, (1, nh)), jnp.tile(s, (1, nh))


def kernel(x, mem, ln0_g, ln0_b, w_in, mla_q_norm, w_uq, mla_kv_norm, w_ukv, diff_lam, diff_norm,
           w_mem_kv, w_branch, w_out, ln_g, ln_b):
    b, t, d = x.shape
    depth = w_in.shape[0]
    alpha = (2 * depth) ** 0.25
    assert t % 512 == 0 and d == 1024

    w_plain, w_vt, w_rope, wg = _weight_prep(w_in.astype(BF16))
    plain_widths = (BRANCH_W,) * 3 + (N_BRANCH * BRANCH_W,)
    rope_heads = tuple((hd, r // 2) for _, _, hd, r in ROPE_GROUPS)
    patterns = sorted(set((nh, hd, r) for _, nh, hd, r in ROPE_GROUPS))
    rope_tables = tuple(patterns.index((nh, hd, r)) for _, nh, hd, r in ROPE_GROUPS)
    cs = [_rope_cs(t, nh, hd, r) for nh, hd, r in patterns]
    ctab = jnp.stack([c for c, _ in cs])
    stab = jnp.stack([s for _, s in cs])

    uq = w_uq.reshape(depth, Q_LORA, N_HEADS, MLA_NOPE + MLA_ROPE)
    qn_w, qr_w = uq[..., :MLA_NOPE], uq[..., MLA_NOPE:]
    pad32 = jnp.zeros((depth, Q_LORA, N_HEADS, LANES - MLA_NOPE - MLA_ROPE), w_uq.dtype)
    hw = N_HEADS * LANES
    wq = jnp.concatenate([qn_w, qr_w, pad32], axis=-1).reshape(depth, Q_LORA, hw).astype(BF16)
    half = MLA_ROPE // 2
    wq_rot = jnp.concatenate([jnp.zeros_like(qn_w), -qr_w[..., half:], qr_w[..., :half], pad32],
                             axis=-1).reshape(depth, Q_LORA, hw).astype(BF16)
    cos_m, sin_m = _rope_tables(t, MLA_ROPE)
    one = lambda n: jnp.ones((t, n), F32)
    zer = lambda n: jnp.zeros((t, n), F32)
    qs = (MLA_NOPE + MLA_ROPE) ** -0.5 * LOG2E
    ct_q = qs * jnp.tile(jnp.concatenate([one(MLA_NOPE), cos_m, cos_m, one(LANES - MLA_NOPE - MLA_ROPE)], axis=1), (1, N_HEADS))
    st_q = qs * jnp.tile(jnp.concatenate([zer(MLA_NOPE), sin_m, sin_m, zer(LANES - MLA_NOPE - MLA_ROPE)], axis=1), (1, N_HEADS))
    ukv = w_ukv.reshape(depth, KV_LORA, N_HEADS, MLA_NOPE + MLA_V)
    wk = jnp.concatenate([ukv[..., :MLA_NOPE], jnp.zeros((depth, KV_LORA, N_HEADS, LANES - MLA_NOPE), w_ukv.dtype)],
                         axis=-1).reshape(depth, KV_LORA, hw).astype(BF16)
    wvt = ukv[..., MLA_NOPE:].reshape(depth, KV_LORA, N_HEADS * MLA_V).astype(BF16)
    place = np.zeros((MXU_N, hw), np.float32)
    for hh in range(N_HEADS):
        for j in range(MLA_ROPE):
            place[j, hh * LANES + MLA_NOPE + j] = 1.0
    place = jnp.asarray(place, BF16)

    wb = w_branch.astype(BF16)
    wo = w_out.astype(BF16)
    wmem = w_mem_kv.astype(BF16)
    norm_t = jnp.broadcast_to(diff_norm.astype(F32)[:, :, None], (depth, HEAD_DIM, TQ))

    h, hb = _layer_norm0(x.reshape(b * t, d), ln0_g, ln0_b)
    for l in range(depth):
        hb3 = hb.reshape(b, t, d)
        avt, bvt, cvt, dcq, ckv_iw, eq, z = _proj_plain(hb3, w_plain, w_vt, plain_widths, l)
        aq, ak, iq, ik, bq, bk, cq, ck, kr = _proj_rope(hb3, w_rope, ctab, stab, rope_heads, rope_tables, l)

        o_a = _dsa(aq, ak, avt, iq, ik, ckv_iw)
        o_b = _moba(bq, bk, bvt, _kbar(bk))
        lam_init = 0.8 - 0.6 * math.exp(-0.3 * l)
        misc = jnp.full((SUBLANES, LANES), lam_init, F32)
        o_c = _diff(cq, ck, cvt, diff_lam[l].astype(F32), norm_t[l], misc)
        qm, km, vmt = _mla_prep(dcq, ckv_iw, kr, mla_q_norm[l].reshape(1, Q_LORA), mla_kv_norm[l].reshape(1, KV_LORA),
                                wq, wq_rot, wk, wvt, place, ct_q, st_q, l)
        o_d = _mla(qm, km, vmt)
        o_e = _mem_attn(eq, _mem_kv(mem, wmem, l))

        os5 = [o.reshape(b * t, BRANCH_W) for o in (o_a, o_b, o_c, o_d, o_e)]
        h, hb = _final(h, hb, os5, z.reshape(b * t, N_BRANCH * BRANCH_W), wg, wb, wo,
                       ln_g[l].reshape(1, d), ln_b[l].reshape(1, d), alpha, l)
    return h.reshape(b, t, d)
```

```python
import functools
import math

import numpy as np
import jax
import jax.numpy as jnp
from jax import lax
from jax.experimental import pallas as pl
from jax.experimental.pallas import tpu as pltpu

F32 = jnp.float32
BF16 = jnp.bfloat16
I32 = jnp.int32
I16 = jnp.int16

N_HEADS = 4
HEAD_DIM = 64
BRANCH_W = N_HEADS * HEAD_DIM
N_BRANCH = 5
ROPE_THETA = 500000.0
ROT_64 = 16
ROT_32 = 8
IDX_HEADS = 8
IDX_DIM = 32
TOPK_MAX = 256
MOBA_BLOCK = 256
MOBA_TOPK = 3
DIFF_DIM = 32
Q_LORA = 256
KV_LORA = 128
MLA_NOPE = 64
MLA_ROPE = 32
MLA_V = 64
LN_EPS = 1e-5
RMS_EPS = 1e-6

IN_LAYOUT = (
    ("a_q", BRANCH_W), ("a_k", BRANCH_W), ("a_v", BRANCH_W),
    ("i_q", IDX_HEADS * IDX_DIM), ("i_k", IDX_DIM), ("i_w", IDX_HEADS),
    ("b_q", BRANCH_W), ("b_k", BRANCH_W), ("b_v", BRANCH_W),
    ("c_q", BRANCH_W), ("c_k", BRANCH_W), ("c_v", BRANCH_W),
    ("d_cq", Q_LORA), ("d_ckv", KV_LORA), ("d_kr", MLA_ROPE),
    ("e_q", BRANCH_W),
    ("z", N_BRANCH * BRANCH_W),
    ("g", N_BRANCH * 1024),
)

SUBLANES = 8
LANES = 128
MXU_N = 256
TQ = 512
CK = 256
VROWS = HEAD_DIM + 16
FLASH_UNROLL = 4
NEG = -1e30
LOG2E = math.log2(math.e)
INT_MIN = np.int32(-2 ** 31)
HALF16 = 1 << 15
VMEM_LIMIT = 56 * 1024 * 1024


def _offsets():
    off, out = 0, {}
    for name, size in IN_LAYOUT:
        out[name] = (off, size)
        off += size
    return out


OFF = _offsets()


def _nt_dot(a, b):
    return lax.dot_general(a, b, (((1,), (1,)), ((), ())), preferred_element_type=F32)


def _tn_dot(w, x):
    return lax.dot_general(w, x, (((0,), (1,)), ((), ())), preferred_element_type=F32)


def _fold_rows(w, rows=SUBLANES):
    xs = [w[r:r + rows, :] for r in range(0, w.shape[0], rows)]
    while len(xs) > 1:
        xs = [xs[j] + xs[j + 1] for j in range(0, len(xs) - 1, 2)] + ([xs[-1]] if len(xs) % 2 else [])
    return xs[0]


def _masked_qt(q, shift, n, qt_ref):
    qt = q.T
    dim = lax.broadcasted_iota(I32, (LANES, qt.shape[1]), 0)
    for j in range(n):
        half = (j << shift) // LANES
        rows = qt[half * LANES:(half + 1) * LANES, :]
        qt_ref[j] = jnp.where(((dim + half * LANES) >> shift) == j, rows, 0.0).astype(BF16)


def _half(kc, j, shift):
    half = (j << shift) // LANES
    return kc[:, half * LANES:(half + 1) * LANES]


def _cparams(n_axes):
    return pltpu.CompilerParams(dimension_semantics=("arbitrary",) * n_axes,
                                vmem_limit_bytes=VMEM_LIMIT)


def _layer_spec(a, l):
    return pl.BlockSpec((None,) + a.shape[1:], lambda *_: (l,) + (0,) * (a.ndim - 1))


def _softmax_step(s_t, m_tile, vt_h, m_ref, acc_ref):
    m_old = m_ref[...]
    m_new = jnp.maximum(m_old, m_tile)
    alpha = jnp.exp2(m_old - m_new)
    p = jnp.exp2(s_t - m_new)
    acc_ref[...] = alpha * acc_ref[...] + jnp.dot(vt_h, p.astype(BF16), preferred_element_type=F32)
    m_ref[...] = m_new


def _softmax_init(m_ref, acc_ref):
    m_ref[...] = jnp.full(m_ref.shape, NEG, F32)
    acc_ref[...] = jnp.zeros(acc_ref.shape, F32)


def _softmax_out(acc_ref):
    return acc_ref[:HEAD_DIM, :] / acc_ref[HEAD_DIM:HEAD_DIM + 1, :]


def _store_vt(o_ref, vt):
    ones = jnp.ones((VROWS - HEAD_DIM, CK), o_ref.dtype)
    for j in range(o_ref.shape[0]):
        for h in range(N_HEADS):
            o_ref[j, h * VROWS:h * VROWS + HEAD_DIM, :] = (
                vt[h * HEAD_DIM:(h + 1) * HEAD_DIM, j * CK:(j + 1) * CK].astype(o_ref.dtype))
            o_ref[j, h * VROWS + HEAD_DIM:(h + 1) * VROWS, :] = ones


def _flash_loop(n_full, qk_all, mask, vt_rows, state, prep=None, causal_tail=True):
    s_ref, mx_ref, m_ref, acc_ref = state
    n_state = m_ref.shape[0]
    for j in range(n_state):
        _softmax_init(m_ref.at[j], acc_ref.at[j])

    def park(c, slot):
        ctx = c if prep is None else prep(c)
        for j, s in enumerate(qk_all(c)):
            if mask is not None:
                s = mask(ctx, j, s)
            s_ref[slot, j] = s
            mx_ref[slot, j] = jnp.max(s, axis=0, keepdims=True)

    def consume(c, slot, d):
        for j in range(n_state):
            if d is None or not causal_tail:
                s, m_tile = s_ref[slot, j], mx_ref[slot, j]
            else:
                s = jnp.where(_causal(d), s_ref[slot, j], NEG)
                m_tile = jnp.max(s, axis=0, keepdims=True)
            _softmax_step(s, m_tile, vt_rows(c, j), m_ref.at[j], acc_ref.at[j])

    park(0, 0)

    def pair(c):
        park(c + 1, 1)
        consume(c, 0, None)
        park(c + 2, 0)
        consume(c + 1, 1, None)

    def body(g, carry):
        for u in range(0, FLASH_UNROLL, 2):
            pair(FLASH_UNROLL * g + u)
        return carry

    n_group = lax.shift_right_logical(n_full, FLASH_UNROLL.bit_length() - 1)
    lax.fori_loop(0, n_group, body, 0)
    c0 = FLASH_UNROLL * n_group
    for u in range(FLASH_UNROLL // 2 - 1):
        @pl.when(n_full - c0 >= 2 * (u + 1))
        def _(u=u):
            pair(c0 + 2 * u)
    park(n_full + 1, 1)
    consume(n_full, 0, 0)
    consume(n_full + 1, 1, 1)


def _causal(d):
    kpos = lax.broadcasted_iota(I32, (CK, TQ), 0) + d * CK
    return kpos <= lax.broadcasted_iota(I32, (CK, TQ), 1)


def _attn_scratch(n_state):
    return [pltpu.VMEM((2, n_state, CK, TQ), F32), pltpu.VMEM((2, n_state, 1, TQ), F32),
            pltpu.VMEM((n_state, 1, TQ), F32), pltpu.VMEM((n_state, VROWS, TQ), F32),
            pltpu.VMEM((BRANCH_W, TQ), F32)]


def _kv_specs(t, w):
    kspec = pl.BlockSpec((None, t, w), lambda bb, i: (bb, 0, 0))
    vspec = pl.BlockSpec((None, t // CK, N_HEADS * VROWS, CK), lambda bb, i: (bb, 0, 0, 0))
    return kspec, vspec


def _ln_kernel(x_ref, g_ref, b_ref, h_ref, hb_ref):
    x = x_ref[...]
    mu = jnp.mean(x, axis=1, keepdims=True)
    xc = x - mu
    var = jnp.mean(xc * xc, axis=1, keepdims=True)
    y = xc * lax.rsqrt(var + LN_EPS) * g_ref[...] + b_ref[...]
    h_ref[...] = y
    hb_ref[...] = y.astype(BF16)


def _layer_norm0(x2, g, b):
    n, d = x2.shape
    tm = 512
    row = pl.BlockSpec((tm, d), lambda i: (i, 0))
    vec = pl.BlockSpec((1, d), lambda i: (0, 0))
    return pl.pallas_call(
        _ln_kernel,
        out_shape=(jax.ShapeDtypeStruct((n, d), F32), jax.ShapeDtypeStruct((n, d), BF16)),
        grid=(n // tm,),
        in_specs=[row, vec, vec],
        out_specs=(row, row),
        compiler_params=_cparams(1),
        name="ln0",
    )(x2, g.reshape(1, d), b.reshape(1, d))


def _proj_plain_kernel(x_ref, w_ref, wt_ref, *out_refs, n_t):
    for g, o_ref in enumerate(out_refs[:n_t]):
        _store_vt(o_ref, _tn_dot(wt_ref[:, g * BRANCH_W:(g + 1) * BRANCH_W], x_ref[...]))
    off = 0
    for o_ref in out_refs[n_t:]:
        wd = o_ref.shape[-1]
        for j in range(0, wd, MXU_N):
            acc = jnp.dot(x_ref[...], w_ref[:, off + j:off + j + MXU_N], preferred_element_type=F32)
            o_ref[:, j:j + MXU_N] = acc.astype(o_ref.dtype)
        off += wd


def _proj_plain(hb3, w, wt, widths, l):
    b, t, d = hb3.shape
    tm = 512
    n_t = wt.shape[-1] // BRANCH_W
    shapes = [jax.ShapeDtypeStruct((b, t // CK, N_HEADS * VROWS, CK), BF16)] * n_t
    specs = [pl.BlockSpec((None, tm // CK, N_HEADS * VROWS, CK), lambda i, bb: (bb, i, 0, 0))] * n_t
    shapes += [jax.ShapeDtypeStruct((b, t, wd), BF16) for wd in widths]
    specs += [pl.BlockSpec((None, tm, wd), lambda i, bb: (bb, i, 0)) for wd in widths]
    return pl.pallas_call(
        functools.partial(_proj_plain_kernel, n_t=n_t),
        out_shape=tuple(shapes),
        grid=(t // tm, b),
        in_specs=[pl.BlockSpec((None, tm, d), lambda i, bb: (bb, i, 0)),
                  _layer_spec(w, l), _layer_spec(wt, l)],
        out_specs=tuple(specs),
        compiler_params=_cparams(2),
        name="proj_plain",
    )(hb3, w, wt)


def _proj_rope_kernel(x_ref, w_ref, c_ref, s_ref, *out_refs, heads, tables):
    lane = lax.broadcasted_iota(I32, (x_ref.shape[0], MXU_N), 1)
    for g, o_ref in enumerate(out_refs):
        hd, half = heads[g]
        sl = slice(g * MXU_N, (g + 1) * MXU_N)
        acc = jnp.dot(x_ref[...], w_ref[:, sl], preferred_element_type=F32)
        partner = jnp.where((lane & (hd - 1)) < half,
                            pltpu.roll(acc, MXU_N - half, 1), pltpu.roll(acc, half, 1))
        o_ref[...] = (acc * c_ref[tables[g]] + partner * s_ref[tables[g]]).astype(o_ref.dtype)


def _proj_rope(hb3, w, ctab, stab, heads, tables, l):
    b, t, d = hb3.shape
    tm = 512
    assert w.shape[-1] == MXU_N * len(heads)
    tspec = pl.BlockSpec((ctab.shape[0], tm, MXU_N), lambda i, bb: (0, i, 0))
    ospec = pl.BlockSpec((None, tm, MXU_N), lambda i, bb: (bb, i, 0))
    return pl.pallas_call(
        functools.partial(_proj_rope_kernel, heads=heads, tables=tables),
        out_shape=(jax.ShapeDtypeStruct((b, t, MXU_N), BF16),) * len(heads),
        grid=(t // tm, b),
        in_specs=[pl.BlockSpec((None, tm, d), lambda i, bb: (bb, i, 0)),
                  _layer_spec(w, l), tspec, tspec],
        out_specs=(ospec,) * len(heads),
        compiler_params=_cparams(2),
        name="proj_rope",
    )(hb3, w, ctab, stab)


def _dsa_kernel(aq_ref, ak_ref, avt_ref, iq_ref, ik_ref, iw_ref, pick_ref, tri_ref, o_ref,
                keys_ref, hi_ref, lo_ref, bk_ref, iqt_ref, aqt_ref, wt_ref, thr_ref, s_ref, mx_ref, m_ref, acc_ref, ot_ref,
                *, topk, idx_scale):
    i = pl.program_id(1)
    n_full = 2 * i
    n_pair = i + 1

    iqt = iq_ref[...].astype(F32).T
    for hh in range(IDX_HEADS):
        iqt_ref[hh] = jnp.concatenate([iqt[hh * IDX_DIM:(hh + 1) * IDX_DIM, :],
                                       jnp.zeros((LANES - IDX_DIM, TQ), F32)], axis=0).astype(BF16)
    _masked_qt(aq_ref[...].astype(F32) * (HEAD_DIM ** -0.5 * LOG2E), 6, N_HEADS, aqt_ref)
    wt_ref[...] = _nt_dot(pick_ref[...], iw_ref[...]) * idx_scale

    def logits(c):
        kc = ik_ref[pl.ds(pl.multiple_of(c * CK, CK), CK), :]
        return [jnp.dot(kc[:, :LANES], iqt_ref[hh], preferred_element_type=F32) for hh in range(IDX_HEADS)]

    def put_keys(c, key):
        keys_ref[c] = key
        hi_ref[c] = (key >> 16).astype(I16)
        lo_ref[c] = ((key & 0xFFFF) - HALF16).astype(I16)

    def score_chunk(c, lg, d):
        sc = jnp.zeros((CK, TQ), F32)
        for hh in range(IDX_HEADS):
            sc = sc + jnp.maximum(lg[hh], 0.0) * wt_ref[hh:hh + 1, :]
        bits = pltpu.bitcast(sc, I32)
        key = jnp.where(bits < 0, INT_MIN - bits, bits)
        put_keys(c, key if d is None else jnp.where(_causal(d), key, INT_MIN))

    def score_pair(c, d0, d1):
        lg0, lg1 = logits(c), logits(c + 1)
        score_chunk(c, lg0, d0)
        score_chunk(c + 1, lg1, d1)

    def score_body(p, carry):
        score_pair(2 * p, None, None)
        return carry

    lax.fori_loop(0, i, score_body, 0)
    score_pair(n_full, 0, 1)

    def pair_loop(body, init):
        def pair(p, carry):
            return body(2 * p + 1, body(2 * p, carry))
        return lax.fori_loop(0, n_pair, pair, init)

    def count16(pred, also=None):
        def body(c, part):
            hit = jnp.where(pred(c), jnp.int16(1), jnp.int16(0))
            if also is not None:
                hit = jnp.where(also(c), hit, jnp.int16(0))
            return part + _fold_rows(hit, 2 * SUBLANES)
        part = pair_loop(body, jnp.zeros((2 * SUBLANES, TQ), I16))
        return jnp.sum(part.astype(F32), axis=0, keepdims=True)

    def search16(ref, need):
        def bit_body(bi, t_u):
            c_u = t_u | jnp.left_shift(jnp.int32(1), 15 - bi)
            ck = (c_u - HALF16).astype(I16)
            cnt = count16(lambda c: ref[c] >= ck)
            return jnp.where(cnt >= need, c_u, t_u)
        return lax.fori_loop(0, 16, bit_body, jnp.zeros((1, TQ), I32))

    hi_u = search16(hi_ref, float(topk))
    thr_hi = (hi_u - HALF16).astype(I16)
    n_above = count16(lambda c: hi_ref[c] > thr_hi)

    def bucket_body(c, carry):
        bk_ref[c] = jnp.where(hi_ref[c] == thr_hi, lo_ref[c], jnp.int16(-HALF16))
        return carry

    pair_loop(bucket_body, 0)
    lo_u = search16(bk_ref, float(topk) - n_above)
    thr_lo = (lo_u - HALF16).astype(I16)
    thr = ((hi_u - HALF16) << 16) | lo_u

    n_gt = n_above + count16(lambda c: bk_ref[c] > thr_lo)
    n_eq = count16(lambda c: lo_ref[c] == thr_lo, also=lambda c: hi_ref[c] == thr_hi)
    need = float(topk) - n_gt
    amb = jnp.logical_and(n_eq > need, thr > INT_MIN)
    any_amb = jnp.max(jnp.where(amb, 1.0, 0.0)) > 0.5

    @pl.when(any_amb)
    def _():
        def drop_body(c, seen):
            k = keys_ref[c]
            eq = k == thr
            eqf = jnp.where(eq, 1.0, 0.0)
            rank = jnp.dot(tri_ref[...], eqf.astype(BF16), preferred_element_type=F32) + seen
            drop = jnp.logical_and(jnp.logical_and(eq, rank > need), amb)
            keys_ref[c] = jnp.where(drop, INT_MIN, k)
            return seen + jnp.sum(eqf, axis=0, keepdims=True)

        pair_loop(drop_body, jnp.zeros((1, TQ), F32))

    thr_ref[...] = jnp.maximum(thr, INT_MIN + 1)

    def qk_all(c):
        kc = ak_ref[pl.ds(pl.multiple_of(c * CK, CK), CK), :]
        return [jnp.dot(_half(kc, h, 6), aqt_ref[h], preferred_element_type=F32) for h in range(N_HEADS)]

    _flash_loop(n_full, qk_all,
                lambda keep, h, s: jnp.where(keep, s, NEG),
                lambda c, h: avt_ref[c, h * VROWS:(h + 1) * VROWS, :],
                (s_ref, mx_ref, m_ref, acc_ref),
                prep=lambda c: keys_ref[c] >= thr_ref[...], causal_tail=False)
    for h in range(N_HEADS):
        ot_ref[h * HEAD_DIM:(h + 1) * HEAD_DIM, :] = _softmax_out(acc_ref.at[h])
    o_ref[...] = ot_ref[...].T.astype(o_ref.dtype)


def _dsa(aq, ak, avt, iq, ik, iw):
    b, t, _ = aq.shape
    topk = min(TOPK_MAX, t // 4)
    qspec = pl.BlockSpec((None, TQ, BRANCH_W), lambda bb, i: (bb, i, 0))
    kspec, vspec = _kv_specs(t, BRANCH_W)
    pick = np.zeros((2 * SUBLANES, MXU_N), np.float32)
    for hh in range(IDX_HEADS):
        pick[hh, IDX_DIM + hh] = 1.0
    pick = jnp.asarray(pick, BF16)
    tri = jnp.asarray(np.tril(np.ones((CK, CK), np.float32)), BF16)
    kern = functools.partial(_dsa_kernel, topk=topk, idx_scale=(IDX_HEADS * IDX_DIM) ** -0.5)
    return pl.pallas_call(
        kern,
        out_shape=jax.ShapeDtypeStruct((b, t, BRANCH_W), BF16),
        grid=(b, t // TQ),
        in_specs=[qspec, kspec, vspec, qspec, kspec, qspec,
                  pl.BlockSpec(pick.shape, lambda bb, i: (0, 0)), pl.BlockSpec(tri.shape, lambda bb, i: (0, 0))],
        out_specs=qspec,
        scratch_shapes=[
            pltpu.VMEM((t // CK, CK, TQ), I32),
            pltpu.VMEM((t // CK, CK, TQ), I16),
            pltpu.VMEM((t // CK, CK, TQ), I16),
            pltpu.VMEM((t // CK, CK, TQ), I16),
            pltpu.VMEM((IDX_HEADS, LANES, TQ), BF16),
            pltpu.VMEM((N_HEADS, LANES, TQ), BF16),
            pltpu.VMEM((2 * SUBLANES, TQ), F32),
            pltpu.VMEM((1, TQ), I32),
        ] + _attn_scratch(N_HEADS),
        compiler_params=_cparams(2),
        name="dsa",
    )(aq, ak, avt, iq, ik, iw, pick, tri)


def _kbar_kernel(k_ref, o_ref):
    o_ref[...] = jnp.zeros(o_ref.shape, o_ref.dtype)
    nb = k_ref.shape[0] // MOBA_BLOCK
    for n in range(nb):
        blk = k_ref[n * MOBA_BLOCK:(n + 1) * MOBA_BLOCK, :].astype(F32)
        o_ref[n:n + 1, :] = jnp.mean(blk, axis=0, keepdims=True).astype(o_ref.dtype)


def _kbar(bk):
    b, t, w = bk.shape
    nbp = max(2 * SUBLANES, t // MOBA_BLOCK)
    return pl.pallas_call(
        _kbar_kernel,
        out_shape=jax.ShapeDtypeStruct((b, nbp, w), BF16),
        grid=(b,),
        in_specs=[pl.BlockSpec((None, t, w), lambda bb: (bb, 0, 0))],
        out_specs=pl.BlockSpec((None, nbp, w), lambda bb: (bb, 0, 0)),
        compiler_params=_cparams(1),
        name="moba_kbar",
    )(bk)


def _moba_kernel(q_ref, k_ref, vt_ref, kbar_ref, o_ref, qt_ref, bias_ref, s_ref, mx_ref, m_ref, acc_ref, ot_ref):
    i = pl.program_id(1)
    nbp = kbar_ref.shape[0]
    blk = lax.broadcasted_iota(I32, (nbp, TQ), 0)
    blk_f = blk.astype(F32)
    own = 2 * i + (lax.broadcasted_iota(I32, (nbp, TQ), 1) >> (MOBA_BLOCK.bit_length() - 1))
    _masked_qt(q_ref[...].astype(F32) * (HEAD_DIM ** -0.5 * LOG2E), 6, N_HEADS, qt_ref)

    for h in range(N_HEADS):
        g = jnp.where(blk < own, jnp.dot(_half(kbar_ref[...], h, 6), qt_ref[h], preferred_element_type=F32), NEG)
        bias = jnp.full((nbp, TQ), NEG, F32)
        for _ in range(MOBA_TOPK):
            mx = jnp.max(g, axis=0, keepdims=True)
            first = jnp.min(jnp.where(g == mx, blk_f, 1e9), axis=0, keepdims=True)
            pick = jnp.logical_and(blk_f == first, mx > 0.5 * NEG)
            bias = jnp.where(pick, 0.0, bias)
            g = jnp.where(pick, NEG, g)
        bias_ref[h] = jnp.where(blk == own, 0.0, bias)

    def qk_all(c):
        kc = k_ref[pl.ds(pl.multiple_of(c * CK, CK), CK), :]
        return [jnp.dot(_half(kc, h, 6), qt_ref[h], preferred_element_type=F32) for h in range(N_HEADS)]

    _flash_loop(2 * i, qk_all, lambda c, h, s: s + bias_ref[h, pl.ds(c, 1), :],
                lambda c, h: vt_ref[c, h * VROWS:(h + 1) * VROWS, :], (s_ref, mx_ref, m_ref, acc_ref))
    for h in range(N_HEADS):
        ot_ref[h * HEAD_DIM:(h + 1) * HEAD_DIM, :] = _softmax_out(acc_ref.at[h])
    o_ref[...] = ot_ref[...].T.astype(o_ref.dtype)


def _moba(bq, bk, bvt, kbar):
    b, t, w = bq.shape
    assert TQ == 2 * MOBA_BLOCK and CK == MOBA_BLOCK and t % TQ == 0
    nbp = kbar.shape[1]
    qspec = pl.BlockSpec((None, TQ, w), lambda bb, i: (bb, i, 0))
    kspec, vspec = _kv_specs(t, w)
    return pl.pallas_call(
        _moba_kernel,
        out_shape=jax.ShapeDtypeStruct((b, t, w), BF16),
        grid=(b, t // TQ),
        in_specs=[qspec, kspec, vspec, pl.BlockSpec((None, nbp, w), lambda bb, i: (bb, 0, 0))],
        out_specs=qspec,
        scratch_shapes=[pltpu.VMEM((N_HEADS, LANES, TQ), BF16), pltpu.VMEM((N_HEADS, nbp, TQ), F32)]
        + _attn_scratch(N_HEADS),
        compiler_params=_cparams(2),
        name="moba",
    )(bq, bk, bvt, kbar)


def _diff_kernel(q_ref, k_ref, vt_ref, lam_ref, norm_ref, misc_ref, o_ref,
                 qt_ref, s_ref, mx_ref, m_ref, acc_ref, ot_ref):
    i = pl.program_id(1)
    _masked_qt(q_ref[...].astype(F32) * (DIFF_DIM ** -0.5 * LOG2E), 5, 2 * N_HEADS, qt_ref)

    dl = lam_ref[...]
    lam_init = misc_ref[0:1, 0:1]
    lam = (jnp.exp(jnp.sum(dl[0:1, :] * dl[1:2, :], axis=1, keepdims=True))
           - jnp.exp(jnp.sum(dl[2:3, :] * dl[3:4, :], axis=1, keepdims=True)) + lam_init)

    def qk_all(c):
        kc = k_ref[pl.ds(pl.multiple_of(c * CK, CK), CK), :]
        return [jnp.dot(_half(kc, j, 5), qt_ref[j], preferred_element_type=F32) for j in range(2 * N_HEADS)]

    _flash_loop(2 * i, qk_all, None,
                lambda c, j: vt_ref[c, (j // 2) * VROWS:(j // 2 + 1) * VROWS, :],
                (s_ref, mx_ref, m_ref, acc_ref))

    post = norm_ref[...] * (1.0 - lam_init)
    for h in range(N_HEADS):
        o_h = _softmax_out(acc_ref.at[2 * h]) - lam * _softmax_out(acc_ref.at[2 * h + 1])
        ms = jnp.mean(o_h * o_h, axis=0, keepdims=True)
        ot_ref[h * HEAD_DIM:(h + 1) * HEAD_DIM, :] = o_h * lax.rsqrt(ms + RMS_EPS) * post
    o_ref[...] = ot_ref[...].T.astype(o_ref.dtype)


def _diff(cq, ck, cvt, lam, norm, misc):
    b, t, w = cq.shape
    qspec = pl.BlockSpec((None, TQ, w), lambda bb, i: (bb, i, 0))
    kspec, vspec = _kv_specs(t, w)
    full = lambda a: pl.BlockSpec(a.shape, lambda bb, i: (0,) * a.ndim)
    return pl.pallas_call(
        _diff_kernel,
        out_shape=jax.ShapeDtypeStruct((b, t, w), BF16),
        grid=(b, t // TQ),
        in_specs=[qspec, kspec, vspec, full(lam), full(norm), full(misc)],
        out_specs=qspec,
        scratch_shapes=[pltpu.VMEM((2 * N_HEADS, LANES, TQ), BF16)] + _attn_scratch(2 * N_HEADS),
        compiler_params=_cparams(2),
        name="diff",
    )(cq, ck, cvt, lam, norm, misc)


def _mla_prep_kernel(cq_ref, ckv_ref, kr_ref, qn_ref, kvn_ref, wq_ref, wqr_ref, wk_ref, wvt_ref,
                     p_ref, ct_ref, st_ref, q_out, k_out, vt_out):
    x = cq_ref[...].astype(F32)
    xn = (x * lax.rsqrt(jnp.mean(x * x, axis=1, keepdims=True) + RMS_EPS) * qn_ref[...]).astype(BF16)
    q = (jnp.dot(xn, wq_ref[...], preferred_element_type=F32) * ct_ref[...]
         + jnp.dot(xn, wqr_ref[...], preferred_element_type=F32) * st_ref[...])
    q_out[...] = q.astype(q_out.dtype)
    c = ckv_ref[:, :KV_LORA].astype(F32)
    cn = (c * lax.rsqrt(jnp.mean(c * c, axis=1, keepdims=True) + RMS_EPS) * kvn_ref[...]).astype(BF16)
    k = (jnp.dot(cn, wk_ref[...], preferred_element_type=F32)
         + jnp.dot(kr_ref[...], p_ref[...], preferred_element_type=F32))
    k_out[...] = k.astype(k_out.dtype)
    _store_vt(vt_out, _tn_dot(wvt_ref[...], cn))


def _mla_prep(dcq, ckv, kr, qn, kvn, wq, wqr, wk, wvt, pmat, ct, st, l):
    b, t, _ = dcq.shape
    tm = 512
    hw = N_HEADS * LANES
    row = lambda w: pl.BlockSpec((None, tm, w), lambda i, bb: (bb, i, 0))
    full = lambda a: pl.BlockSpec(a.shape, lambda i, bb: (0,) * a.ndim)
    tab = pl.BlockSpec((tm, hw), lambda i, bb: (i, 0))
    return pl.pallas_call(
        _mla_prep_kernel,
        out_shape=(jax.ShapeDtypeStruct((b, t, hw), BF16), jax.ShapeDtypeStruct((b, t, hw), BF16),
                   jax.ShapeDtypeStruct((b, t // CK, N_HEADS * VROWS, CK), BF16)),
        grid=(t // tm, b),
        in_specs=[row(Q_LORA), row(MXU_N), row(MXU_N), full(qn), full(kvn), _layer_spec(wq, l), _layer_spec(wqr, l),
                  _layer_spec(wk, l), _layer_spec(wvt, l), full(pmat), tab, tab],
        out_specs=(row(hw), row(hw),
                   pl.BlockSpec((None, tm // CK, N_HEADS * VROWS, CK), lambda i, bb: (bb, i, 0, 0))),
        compiler_params=_cparams(2),
        name="mla_prep",
    )(dcq, ckv, kr, qn, kvn, wq, wqr, wk, wvt, pmat, ct, st)


def _mla_kernel(q_ref, k_ref, vt_ref, o_ref, qt_ref, s_ref, mx_ref, m_ref, acc_ref, ot_ref):
    i = pl.program_id(1)
    hs = [slice(h * LANES, (h + 1) * LANES) for h in range(N_HEADS)]
    for h in range(N_HEADS):
        qt_ref[h] = q_ref[:, hs[h]].astype(F32).T.astype(BF16)

    def qk_all(c):
        start = pl.multiple_of(c * CK, CK)
        return [jnp.dot(k_ref[pl.ds(start, CK), hs[h]], qt_ref[h], preferred_element_type=F32)
                for h in range(N_HEADS)]

    _flash_loop(2 * i, qk_all, None,
                lambda c, h: vt_ref[c, h * VROWS:(h + 1) * VROWS, :],
                (s_ref, mx_ref, m_ref, acc_ref))
    for h in range(N_HEADS):
        ot_ref[h * HEAD_DIM:(h + 1) * HEAD_DIM, :] = _softmax_out(acc_ref.at[h])
    o_ref[...] = ot_ref[...].T.astype(o_ref.dtype)


def _mla(qm, km, vmt):
    b, t, hw = qm.shape
    kspec, vspec = _kv_specs(t, hw)
    return pl.pallas_call(
        _mla_kernel,
        out_shape=jax.ShapeDtypeStruct((b, t, BRANCH_W), BF16),
        grid=(b, t // TQ),
        in_specs=[pl.BlockSpec((None, TQ, hw), lambda bb, i: (bb, i, 0)), kspec, vspec],
        out_specs=pl.BlockSpec((None, TQ, BRANCH_W), lambda bb, i: (bb, i, 0)),
        scratch_shapes=[pltpu.VMEM((N_HEADS, LANES, TQ), BF16)] + _attn_scratch(N_HEADS),
        compiler_params=_cparams(2),
        name="mla",
    )(qm, km, vmt)


def _matmul_kernel(x_ref, w_ref, o_ref):
    o_ref[...] = jnp.dot(x_ref[...].astype(BF16), w_ref[...], preferred_element_type=F32).astype(o_ref.dtype)


def _mem_kv(mem, w, l):
    b, m, d = mem.shape
    n = w.shape[-1]
    return pl.pallas_call(
        _matmul_kernel,
        out_shape=jax.ShapeDtypeStruct((b, m, n), BF16),
        grid=(b,),
        in_specs=[pl.BlockSpec((None, m, d), lambda bb: (bb, 0, 0)), _layer_spec(w, l)],
        out_specs=pl.BlockSpec((None, m, n), lambda bb: (bb, 0, 0)),
        compiler_params=_cparams(1),
        name="mem_kv",
    )(mem, w)


def _mem_kernel(q_ref, kv_ref, o_ref):
    tq = q_ref.shape[0]
    lane_q = lax.broadcasted_iota(I32, (tq, BRANCH_W), 1)
    q = q_ref[...].astype(F32) * (HEAD_DIM ** -0.5)
    mk = kv_ref[:, :BRANCH_W]
    mv = kv_ref[:, BRANCH_W:]
    out = jnp.zeros((tq, BRANCH_W), F32)
    for h in range(N_HEADS):
        in_h = (lane_q >> 6) == h
        s = _nt_dot(jnp.where(in_h, q, 0.0).astype(BF16), mk)
        p = jnp.exp(s - jnp.max(s, axis=1, keepdims=True))
        o_h = jnp.dot(p.astype(BF16), mv, preferred_element_type=F32) / jnp.sum(p, axis=1, keepdims=True)
        out = jnp.where(in_h, o_h, out)
    o_ref[...] = out.astype(o_ref.dtype)


def _mem_attn(eq, mkv):
    b, t, w = eq.shape
    m = mkv.shape[1]
    tq = 512
    return pl.pallas_call(
        _mem_kernel,
        out_shape=jax.ShapeDtypeStruct((b, t, w), BF16),
        grid=(b, t // tq),
        in_specs=[pl.BlockSpec((None, tq, w), lambda bb, i: (bb, i, 0)),
                  pl.BlockSpec((None, m, 2 * w), lambda bb, i: (bb, 0, 0))],
        out_specs=pl.BlockSpec((None, tq, w), lambda bb, i: (bb, i, 0)),
        compiler_params=_cparams(2),
        name="mem_attn",
    )(eq, mkv)


def _final_kernel(h_ref, hb_ref, oa_ref, ob_ref, oc_ref, od_ref, oe_ref, z_ref,
                  wg_ref, wb_ref, wo_ref, g_ref, b_ref, h_out, hb_out, *, alpha):
    d = h_ref.shape[1]
    acc = jnp.zeros(h_ref.shape, F32)
    for n, o_ref in enumerate((oa_ref, ob_ref, oc_ref, od_ref, oe_ref)):
        z = z_ref[:, n * BRANCH_W:(n + 1) * BRANCH_W].astype(F32)
        y = o_ref[...].astype(F32) * (z / (1.0 + jnp.exp(-z)))
        u = jnp.dot(y.astype(BF16), wb_ref[n], preferred_element_type=F32)
        g = jnp.dot(hb_ref[...], wg_ref[:, n * d:(n + 1) * d], preferred_element_type=F32)
        acc = acc + u / (1.0 + jnp.exp(-g))
    out = jnp.dot(acc.astype(BF16), wo_ref[...], preferred_element_type=F32)
    x = alpha * h_ref[...] + out
    mu = jnp.mean(x, axis=1, keepdims=True)
    xc = x - mu
    var = jnp.mean(xc * xc, axis=1, keepdims=True)
    y = xc * lax.rsqrt(var + LN_EPS) * g_ref[...] + b_ref[...]
    h_out[...] = y
    hb_out[...] = y.astype(BF16)


def _final(h, hb, os5, z, wg, wb, wo, ln_g, ln_b, alpha, l):
    n, d = h.shape
    tm = 256
    row = lambda w: pl.BlockSpec((tm, w), lambda i: (i, 0))
    full = lambda a: pl.BlockSpec(a.shape, lambda i: (0,) * a.ndim)
    return pl.pallas_call(
        functools.partial(_final_kernel, alpha=alpha),
        out_shape=(jax.ShapeDtypeStruct((n, d), F32), jax.ShapeDtypeStruct((n, d), BF16)),
        grid=(n // tm,),
        in_specs=[row(d), row(d)] + [row(BRANCH_W)] * N_BRANCH + [row(N_BRANCH * BRANCH_W),
                  _layer_spec(wg, l), _layer_spec(wb, l), _layer_spec(wo, l), full(ln_g), full(ln_b)],
        out_specs=(row(d), row(d)),
        compiler_params=_cparams(1),
        name="merge_out_ln",
    )(h, hb, *os5, z, wg, wb, wo, ln_g, ln_b)


ROPE_GROUPS = (("a_q", N_HEADS, HEAD_DIM, ROT_64), ("a_k", N_HEADS, HEAD_DIM, ROT_64),
               ("i_q", IDX_HEADS, IDX_DIM, ROT_32), ("i_k", 1, MXU_N, ROT_32),
               ("b_q", N_HEADS, HEAD_DIM, ROT_64), ("b_k", N_HEADS, HEAD_DIM, ROT_64),
               ("c_q", 2 * N_HEADS, DIFF_DIM, ROT_32), ("c_k", 2 * N_HEADS, DIFF_DIM, ROT_32),
               ("d_kr", 1, MXU_N, MLA_ROPE))
PLAIN_COLS = ("d_cq", "d_ckv", "e_q") + tuple(("z", j) for j in range(N_BRANCH))
VALUE_COLS = ("a_v", "b_v", "c_v")
GATE_COLS = tuple(("g", j) for j in range(OFF["g"][1] // MXU_N))


def _window_start(col):
    name, j = col if isinstance(col, tuple) else (col, 0)
    return OFF[name][0] + j * MXU_N


def _weight_prep_kernel(offs_ref, wt_ref, o_ref):
    o_ref[...] = wt_ref[...].T.astype(o_ref.dtype)


def _weight_windows(wt, cols, name):
    depth, n, d = wt.shape
    starts = [_window_start(c) for c in cols]
    assert all(st % SUBLANES == 0 and st + MXU_N <= n for st in starts)
    grid_spec = pltpu.PrefetchScalarGridSpec(
        num_scalar_prefetch=1,
        grid=(depth, len(cols)),
        in_specs=[pl.BlockSpec((None, pl.Element(MXU_N), pl.Element(d)),
                               lambda l, j, offs: (l, pl.multiple_of(offs[j], SUBLANES), 0))],
        out_specs=pl.BlockSpec((None, d, MXU_N), lambda l, j, offs: (l, 0, j)),
    )
    return pl.pallas_call(
        _weight_prep_kernel,
        out_shape=jax.ShapeDtypeStruct((depth, d, MXU_N * len(cols)), BF16),
        grid_spec=grid_spec,
        compiler_params=_cparams(2),
        name=name,
    )(jnp.asarray(np.asarray(starts, np.int32)), wt)


def _weight_prep(w_in):
    wt = jnp.swapaxes(w_in, 1, 2)
    return (_weight_windows(wt, PLAIN_COLS, "wprep_plain"), _weight_windows(wt, VALUE_COLS, "wprep_value"),
            _weight_windows(wt, [name for name, *_ in ROPE_GROUPS], "wprep_rope"),
            _weight_windows(wt, GATE_COLS, "wprep_gate"))


def _rope_tables(seq, rot_dim):
    pos = jnp.arange(seq, dtype=F32)
    inv = ROPE_THETA ** (-jnp.arange(0, rot_dim, 2, dtype=F32) / rot_dim)
    ang = pos[:, None] * inv[None, :]
    return jnp.cos(ang), jnp.sin(ang)


def _rope_cs(t, nh, hd, r):
    cos, sin = _rope_tables(t, r)
    c = jnp.concatenate([cos, cos, jnp.ones((t, hd - r), F32)], axis=1)
    s = jnp.concatenate([-sin, sin, jnp.zeros((t, hd - r), F32)], axis=1)
    return jnp.tile(c, (1, nh)), jnp.tile(s, (1, nh))


def kernel(x, mem, ln0_g, ln0_b, w_in, mla_q_norm, w_uq, mla_kv_norm, w_ukv, diff_lam, diff_norm,
           w_mem_kv, w_branch, w_out, ln_g, ln_b):
    b, t, d = x.shape
    depth = w_in.shape[0]
    alpha = (2 * depth) ** 0.25
    assert t % 512 == 0 and d == 1024

    w_plain, w_vt, w_rope, wg = _weight_prep(w_in)
    plain_widths = (BRANCH_W,) * 3 + (N_BRANCH * BRANCH_W,)
    rope_heads = tuple((hd, r // 2) for _, _, hd, r in ROPE_GROUPS)
    patterns = sorted(set((nh, hd, r) for _, nh, hd, r in ROPE_GROUPS))
    rope_tables = tuple(patterns.index((nh, hd, r)) for _, nh, hd, r in ROPE_GROUPS)
    cs = [_rope_cs(t, nh, hd, r) for nh, hd, r in patterns]
    ctab = jnp.stack([c for c, _ in cs])
    stab = jnp.stack([s for _, s in cs])

    uq = w_uq.reshape(depth, Q_LORA, N_HEADS, MLA_NOPE + MLA_ROPE)
    qn_w, qr_w = uq[..., :MLA_NOPE], uq[..., MLA_NOPE:]
    pad32 = jnp.zeros((depth, Q_LORA, N_HEADS, LANES - MLA_NOPE - MLA_ROPE), w_uq.dtype)
    hw = N_HEADS * LANES
    wq = jnp.concatenate([qn_w, qr_w, pad32], axis=-1).reshape(depth, Q_LORA, hw).astype(BF16)
    half = MLA_ROPE // 2
    wq_rot = jnp.concatenate([jnp.zeros_like(qn_w), -qr_w[..., half:], qr_w[..., :half], pad32],
                             axis=-1).reshape(depth, Q_LORA, hw).astype(BF16)
    cos_m, sin_m = _rope_tables(t, MLA_ROPE)
    one = lambda n: jnp.ones((t, n), F32)
    zer = lambda n: jnp.zeros((t, n), F32)
    qs = (MLA_NOPE + MLA_ROPE) ** -0.5 * LOG2E
    ct_q = qs * jnp.tile(jnp.concatenate([one(MLA_NOPE), cos_m, cos_m, one(LANES - MLA_NOPE - MLA_ROPE)], axis=1), (1, N_HEADS))
    st_q = qs * jnp.tile(jnp.concatenate([zer(MLA_NOPE), sin_m, sin_m, zer(LANES - MLA_NOPE - MLA_ROPE)], axis=1), (1, N_HEADS))
    ukv = w_ukv.reshape(depth, KV_LORA, N_HEADS, MLA_NOPE + MLA_V)
    wk = jnp.concatenate([ukv[..., :MLA_NOPE], jnp.zeros((depth, KV_LORA, N_HEADS, LANES - MLA_NOPE), w_ukv.dtype)],
                         axis=-1).reshape(depth, KV_LORA, hw).astype(BF16)
    wvt = ukv[..., MLA_NOPE:].reshape(depth, KV_LORA, N_HEADS * MLA_V).astype(BF16)
    place = np.zeros((MXU_N, hw), np.float32)
    for hh in range(N_HEADS):
        for j in range(MLA_ROPE):
            place[j, hh * LANES + MLA_NOPE + j] = 1.0
    place = jnp.asarray(place, BF16)

    wb = w_branch.astype(BF16)
    wo = w_out.astype(BF16)
    wmem = w_mem_kv.astype(BF16)
    norm_t = jnp.broadcast_to(diff_norm.astype(F32)[:, :, None], (depth, HEAD_DIM, TQ))

    h, hb = _layer_norm0(x.reshape(b * t, d), ln0_g, ln0_b)
    for l in range(depth):
        hb3 = hb.reshape(b, t, d)
        avt, bvt, cvt, dcq, ckv_iw, eq, z = _proj_plain(hb3, w_plain, w_vt, plain_widths, l)
        aq, ak, iq, ik, bq, bk, cq, ck, kr = _proj_rope(hb3, w_rope, ctab, stab, rope_heads, rope_tables, l)

        o_a = _dsa(aq, ak, avt, iq, ik, ik)
        o_b = _moba(bq, bk, bvt, _kbar(bk))
        lam_init = 0.8 - 0.6 * math.exp(-0.3 * l)
        misc = jnp.full((SUBLANES, LANES), lam_init, F32)
        o_c = _diff(cq, ck, cvt, diff_lam[l].astype(F32), norm_t[l], misc)
        qm, km, vmt = _mla_prep(dcq, ckv_iw, kr, mla_q_norm[l].reshape(1, Q_LORA), mla_kv_norm[l].reshape(1, KV_LORA),
                                wq, wq_rot, wk, wvt, place, ct_q, st_q, l)
        o_d = _mla(qm, km, vmt)
        o_e = _mem_attn(eq, _mem_kv(mem, wmem, l))

        os5 = [o.reshape(b * t, BRANCH_W) for o in (o_a, o_b, o_c, o_d, o_e)]
        h, hb = _final(h, hb, os5, z.reshape(b * t, N_BRANCH * BRANCH_W), wg, wb, wo,
                       ln_g[l].reshape(1, d), ln_b[l].reshape(1, d), alpha, l)
    return h.reshape(b, t, d)
```

```python
import functools
import math

import numpy as np
import jax
import jax.numpy as jnp
from jax import lax
from jax.experimental import pallas as pl
from jax.experimental.pallas import tpu as pltpu

F32 = jnp.float32
BF16 = jnp.bfloat16
I32 = jnp.int32
I16 = jnp.int16

N_HEADS = 4
HEAD_DIM = 64
BRANCH_W = N_HEADS * HEAD_DIM
N_BRANCH = 5
ROPE_THETA = 500000.0
ROT_64 = 16
ROT_32 = 8
IDX_HEADS = 8
IDX_DIM = 32
TOPK_MAX = 256
MOBA_BLOCK = 256
MOBA_TOPK = 3
DIFF_DIM = 32
Q_LORA = 256
KV_LORA = 128
MLA_NOPE = 64
MLA_ROPE = 32
MLA_V = 64
LN_EPS = 1e-5
RMS_EPS = 1e-6

IN_LAYOUT = (
    ("a_q", BRANCH_W), ("a_k", BRANCH_W), ("a_v", BRANCH_W),
    ("i_q", IDX_HEADS * IDX_DIM), ("i_k", IDX_DIM), ("i_w", IDX_HEADS),
    ("b_q", BRANCH_W), ("b_k", BRANCH_W), ("b_v", BRANCH_W),
    ("c_q", BRANCH_W), ("c_k", BRANCH_W), ("c_v", BRANCH_W),
    ("d_cq", Q_LORA), ("d_ckv", KV_LORA), ("d_kr", MLA_ROPE),
    ("e_q", BRANCH_W),
    ("z", N_BRANCH * BRANCH_W),
    ("g", N_BRANCH * 1024),
)

SUBLANES = 8
LANES = 128
MXU_N = 256
TQ = 512
CK = 256
VROWS = HEAD_DIM + 16
FLASH_UNROLL = 4
NEG = -1e30
LOG2E = math.log2(math.e)
INT_MIN = np.int32(-2 ** 31)
HALF16 = 1 << 15
VMEM_LIMIT = 56 * 1024 * 1024


def _offsets():
    off, out = 0, {}
    for name, size in IN_LAYOUT:
        out[name] = (off, size)
        off += size
    return out


OFF = _offsets()


def _nt_dot(a, b):
    return lax.dot_general(a, b, (((1,), (1,)), ((), ())), preferred_element_type=F32)


def _tn_dot(w, x):
    return lax.dot_general(w, x, (((0,), (1,)), ((), ())), preferred_element_type=F32)


def _fold_rows(w, rows=SUBLANES):
    xs = [w[r:r + rows, :] for r in range(0, w.shape[0], rows)]
    while len(xs) > 1:
        xs = [xs[j] + xs[j + 1] for j in range(0, len(xs) - 1, 2)] + ([xs[-1]] if len(xs) % 2 else [])
    return xs[0]


def _masked_qt(q, shift, n, qt_ref):
    qt = q.T
    dim = lax.broadcasted_iota(I32, (LANES, qt.shape[1]), 0)
    for j in range(n):
        half = (j << shift) // LANES
        rows = qt[half * LANES:(half + 1) * LANES, :]
        qt_ref[j] = jnp.where(((dim + half * LANES) >> shift) == j, rows, 0.0).astype(BF16)


def _half(kc, j, shift):
    half = (j << shift) // LANES
    return kc[:, half * LANES:(half + 1) * LANES]


def _cparams(n_axes):
    return pltpu.CompilerParams(dimension_semantics=("arbitrary",) * n_axes,
                                vmem_limit_bytes=VMEM_LIMIT)


def _layer_spec(a, l):
    return pl.BlockSpec((None,) + a.shape[1:], lambda *_: (l,) + (0,) * (a.ndim - 1))


def _softmax_step(s_t, m_tile, vt_h, m_ref, acc_ref):
    m_old = m_ref[...]
    m_new = jnp.maximum(m_old, m_tile)
    alpha = jnp.exp2(m_old - m_new)
    p = jnp.exp2(s_t - m_new)
    acc_ref[...] = alpha * acc_ref[...] + jnp.dot(vt_h, p.astype(BF16), preferred_element_type=F32)
    m_ref[...] = m_new


def _softmax_init(m_ref, acc_ref):
    m_ref[...] = jnp.full(m_ref.shape, NEG, F32)
    acc_ref[...] = jnp.zeros(acc_ref.shape, F32)


def _softmax_out(acc_ref):
    return acc_ref[:HEAD_DIM, :] / acc_ref[HEAD_DIM:HEAD_DIM + 1, :]


def _store_vt(o_ref, vt):
    ones = jnp.ones((VROWS - HEAD_DIM, CK), o_ref.dtype)
    for j in range(o_ref.shape[0]):
        for h in range(N_HEADS):
            o_ref[j, h * VROWS:h * VROWS + HEAD_DIM, :] = (
                vt[h * HEAD_DIM:(h + 1) * HEAD_DIM, j * CK:(j + 1) * CK].astype(o_ref.dtype))
            o_ref[j, h * VROWS + HEAD_DIM:(h + 1) * VROWS, :] = ones


def _flash_loop(n_full, qk_all, mask, vt_rows, state, prep=None, causal_tail=True):
    s_ref, mx_ref, m_ref, acc_ref = state
    n_state = m_ref.shape[0]
    for j in range(n_state):
        _softmax_init(m_ref.at[j], acc_ref.at[j])

    def park(c, slot):
        ctx = c if prep is None else prep(c)
        for j, s in enumerate(qk_all(c)):
            if mask is not None:
                s = mask(ctx, j, s)
            s_ref[slot, j] = s
            mx_ref[slot, j] = jnp.max(s, axis=0, keepdims=True)

    def consume(c, slot, d):
        for j in range(n_state):
            if d is None or not causal_tail:
                s, m_tile = s_ref[slot, j], mx_ref[slot, j]
            else:
                s = jnp.where(_causal(d), s_ref[slot, j], NEG)
                m_tile = jnp.max(s, axis=0, keepdims=True)
            _softmax_step(s, m_tile, vt_rows(c, j), m_ref.at[j], acc_ref.at[j])

    park(0, 0)

    def pair(c):
        park(c + 1, 1)
        consume(c, 0, None)
        park(c + 2, 0)
        consume(c + 1, 1, None)

    def body(g, carry):
        for u in range(0, FLASH_UNROLL, 2):
            pair(FLASH_UNROLL * g + u)
        return carry

    n_group = lax.shift_right_logical(n_full, FLASH_UNROLL.bit_length() - 1)
    lax.fori_loop(0, n_group, body, 0)
    c0 = FLASH_UNROLL * n_group
    for u in range(FLASH_UNROLL // 2 - 1):
        @pl.when(n_full - c0 >= 2 * (u + 1))
        def _(u=u):
            pair(c0 + 2 * u)
    park(n_full + 1, 1)
    consume(n_full, 0, 0)
    consume(n_full + 1, 1, 1)


def _causal(d):
    kpos = lax.broadcasted_iota(I32, (CK, TQ), 0) + d * CK
    return kpos <= lax.broadcasted_iota(I32, (CK, TQ), 1)


def _attn_scratch(n_state):
    return [pltpu.VMEM((2, n_state, CK, TQ), F32), pltpu.VMEM((2, n_state, 1, TQ), F32),
            pltpu.VMEM((n_state, 1, TQ), F32), pltpu.VMEM((n_state, VROWS, TQ), F32),
            pltpu.VMEM((BRANCH_W, TQ), F32)]


def _kv_specs(t, w):
    kspec = pl.BlockSpec((None, t, w), lambda bb, i: (bb, 0, 0))
    vspec = pl.BlockSpec((None, t // CK, N_HEADS * VROWS, CK), lambda bb, i: (bb, 0, 0, 0))
    return kspec, vspec


def _ln_kernel(x_ref, g_ref, b_ref, h_ref, hb_ref):
    x = x_ref[...]
    mu = jnp.mean(x, axis=1, keepdims=True)
    xc = x - mu
    var = jnp.mean(xc * xc, axis=1, keepdims=True)
    y = xc * lax.rsqrt(var + LN_EPS) * g_ref[...] + b_ref[...]
    h_ref[...] = y
    hb_ref[...] = y.astype(BF16)


def _layer_norm0(x2, g, b):
    n, d = x2.shape
    tm = 512
    row = pl.BlockSpec((tm, d), lambda i: (i, 0))
    vec = pl.BlockSpec((1, d), lambda i: (0, 0))
    return pl.pallas_call(
        _ln_kernel,
        out_shape=(jax.ShapeDtypeStruct((n, d), F32), jax.ShapeDtypeStruct((n, d), BF16)),
        grid=(n // tm,),
        in_specs=[row, vec, vec],
        out_specs=(row, row),
        compiler_params=_cparams(1),
        name="ln0",
    )(x2, g.reshape(1, d), b.reshape(1, d))


def _proj_plain_kernel(x_ref, w_ref, wt_ref, *out_refs, n_t):
    for g, o_ref in enumerate(out_refs[:n_t]):
        _store_vt(o_ref, _tn_dot(wt_ref[:, g * BRANCH_W:(g + 1) * BRANCH_W], x_ref[...]))
    off = 0
    for o_ref in out_refs[n_t:]:
        wd = o_ref.shape[-1]
        for j in range(0, wd, MXU_N):
            acc = jnp.dot(x_ref[...], w_ref[:, off + j:off + j + MXU_N], preferred_element_type=F32)
            o_ref[:, j:j + MXU_N] = acc.astype(o_ref.dtype)
        off += wd


def _proj_plain(hb3, w, wt, widths, l):
    b, t, d = hb3.shape
    tm = 512
    n_t = wt.shape[-1] // BRANCH_W
    shapes = [jax.ShapeDtypeStruct((b, t // CK, N_HEADS * VROWS, CK), BF16)] * n_t
    specs = [pl.BlockSpec((None, tm // CK, N_HEADS * VROWS, CK), lambda i, bb: (bb, i, 0, 0))] * n_t
    shapes += [jax.ShapeDtypeStruct((b, t, wd), BF16) for wd in widths]
    specs += [pl.BlockSpec((None, tm, wd), lambda i, bb: (bb, i, 0)) for wd in widths]
    return pl.pallas_call(
        functools.partial(_proj_plain_kernel, n_t=n_t),
        out_shape=tuple(shapes),
        grid=(t // tm, b),
        in_specs=[pl.BlockSpec((None, tm, d), lambda i, bb: (bb, i, 0)),
                  _layer_spec(w, l), _layer_spec(wt, l)],
        out_specs=tuple(specs),
        compiler_params=_cparams(2),
        name="proj_plain",
    )(hb3, w, wt)


def _proj_rope_kernel(x_ref, w_ref, c_ref, s_ref, *out_refs, heads, tables):
    lane = lax.broadcasted_iota(I32, (x_ref.shape[0], MXU_N), 1)
    for g, o_ref in enumerate(out_refs):
        hd, half = heads[g]
        sl = slice(g * MXU_N, (g + 1) * MXU_N)
        acc = jnp.dot(x_ref[...], w_ref[:, sl], preferred_element_type=F32)
        partner = jnp.where((lane & (hd - 1)) < half,
                            pltpu.roll(acc, MXU_N - half, 1), pltpu.roll(acc, half, 1))
        o_ref[...] = (acc * c_ref[tables[g]] + partner * s_ref[tables[g]]).astype(o_ref.dtype)


def _proj_rope(hb3, w, ctab, stab, heads, tables, l):
    b, t, d = hb3.shape
    tm = 512
    assert w.shape[-1] == MXU_N * len(heads)
    tspec = pl.BlockSpec((ctab.shape[0], tm, MXU_N), lambda i, bb: (0, i, 0))
    ospec = pl.BlockSpec((None, tm, MXU_N), lambda i, bb: (bb, i, 0))
    return pl.pallas_call(
        functools.partial(_proj_rope_kernel, heads=heads, tables=tables),
        out_shape=(jax.ShapeDtypeStruct((b, t, MXU_N), BF16),) * len(heads),
        grid=(t // tm, b),
        in_specs=[pl.BlockSpec((None, tm, d), lambda i, bb: (bb, i, 0)),
                  _layer_spec(w, l), tspec, tspec],
        out_specs=(ospec,) * len(heads),
        compiler_params=_cparams(2),
        name="proj_rope",
    )(hb3, w, ctab, stab)


def _dsa_kernel(aq_ref, ak_ref, avt_ref, iq_ref, ik_ref, iw_ref, pick_ref, tri_ref, o_ref,
                keys_ref, hi_ref, lo_ref, bk_ref, iqt_ref, aqt_ref, wt_ref, thr_ref, s_ref, mx_ref, m_ref, acc_ref, ot_ref,
                *, topk, idx_scale):
    i = pl.program_id(1)
    n_full = 2 * i
    n_pair = i + 1

    iqt = iq_ref[...].astype(F32).T
    for hh in range(IDX_HEADS):
        iqt_ref[hh] = iqt[hh * IDX_DIM:(hh + 1) * IDX_DIM, :].astype(BF16)
    _masked_qt(aq_ref[...].astype(F32) * (HEAD_DIM ** -0.5 * LOG2E), 6, N_HEADS, aqt_ref)
    wt_ref[...] = _nt_dot(pick_ref[...], iw_ref[...]) * idx_scale

    def logits(c):
        kc = ik_ref[pl.ds(pl.multiple_of(c * CK, CK), CK), :]
        return [jnp.dot(kc[:, :IDX_DIM], iqt_ref[hh], preferred_element_type=F32) for hh in range(IDX_HEADS)]

    def put_keys(c, key):
        keys_ref[c] = key
        hi_ref[c] = (key >> 16).astype(I16)
        lo_ref[c] = ((key & 0xFFFF) - HALF16).astype(I16)

    def score_chunk(c, lg, d):
        sc = jnp.zeros((CK, TQ), F32)
        for hh in range(IDX_HEADS):
            sc = sc + jnp.maximum(lg[hh], 0.0) * wt_ref[hh:hh + 1, :]
        bits = pltpu.bitcast(sc, I32)
        key = jnp.where(bits < 0, INT_MIN - bits, bits)
        put_keys(c, key if d is None else jnp.where(_causal(d), key, INT_MIN))

    def score_pair(c, d0, d1):
        lg0, lg1 = logits(c), logits(c + 1)
        score_chunk(c, lg0, d0)
        score_chunk(c + 1, lg1, d1)

    def score_body(p, carry):
        score_pair(2 * p, None, None)
        return carry

    lax.fori_loop(0, i, score_body, 0)
    score_pair(n_full, 0, 1)

    def pair_loop(body, init):
        def pair(p, carry):
            return body(2 * p + 1, body(2 * p, carry))
        return lax.fori_loop(0, n_pair, pair, init)

    def count16(pred, also=None):
        def body(c, part):
            hit = jnp.where(pred(c), jnp.int16(1), jnp.int16(0))
            if also is not None:
                hit = jnp.where(also(c), hit, jnp.int16(0))
            return part + _fold_rows(hit, 2 * SUBLANES)
        part = pair_loop(body, jnp.zeros((2 * SUBLANES, TQ), I16))
        return jnp.sum(part.astype(F32), axis=0, keepdims=True)

    def search16(ref, need):
        def bit_body(bi, t_u):
            c_u = t_u | jnp.left_shift(jnp.int32(1), 15 - bi)
            ck = (c_u - HALF16).astype(I16)
            cnt = count16(lambda c: ref[c] >= ck)
            return jnp.where(cnt >= need, c_u, t_u)
        return lax.fori_loop(0, 16, bit_body, jnp.zeros((1, TQ), I32))

    hi_u = search16(hi_ref, float(topk))
    thr_hi = (hi_u - HALF16).astype(I16)
    n_above = count16(lambda c: hi_ref[c] > thr_hi)

    def bucket_body(c, carry):
        bk_ref[c] = jnp.where(hi_ref[c] == thr_hi, lo_ref[c], jnp.int16(-HALF16))
        return carry

    pair_loop(bucket_body, 0)
    lo_u = search16(bk_ref, float(topk) - n_above)
    thr_lo = (lo_u - HALF16).astype(I16)
    thr = ((hi_u - HALF16) << 16) | lo_u

    n_gt = n_above + count16(lambda c: bk_ref[c] > thr_lo)
    n_eq = count16(lambda c: lo_ref[c] == thr_lo, also=lambda c: hi_ref[c] == thr_hi)
    need = float(topk) - n_gt
    amb = jnp.logical_and(n_eq > need, thr > INT_MIN)
    any_amb = jnp.max(jnp.where(amb, 1.0, 0.0)) > 0.5

    @pl.when(any_amb)
    def _():
        def drop_body(c, seen):
            k = keys_ref[c]
            eq = k == thr
            eqf = jnp.where(eq, 1.0, 0.0)
            rank = jnp.dot(tri_ref[...], eqf.astype(BF16), preferred_element_type=F32) + seen
            drop = jnp.logical_and(jnp.logical_and(eq, rank > need), amb)
            keys_ref[c] = jnp.where(drop, INT_MIN, k)
            return seen + jnp.sum(eqf, axis=0, keepdims=True)

        pair_loop(drop_body, jnp.zeros((1, TQ), F32))

    thr_ref[...] = jnp.maximum(thr, INT_MIN + 1)

    def qk_all(c):
        kc = ak_ref[pl.ds(pl.multiple_of(c * CK, CK), CK), :]
        return [jnp.dot(_half(kc, h, 6), aqt_ref[h], preferred_element_type=F32) for h in range(N_HEADS)]

    _flash_loop(n_full, qk_all,
                lambda keep, h, s: jnp.where(keep, s, NEG),
                lambda c, h: avt_ref[c, h * VROWS:(h + 1) * VROWS, :],
                (s_ref, mx_ref, m_ref, acc_ref),
                prep=lambda c: keys_ref[c] >= thr_ref[...], causal_tail=False)
    for h in range(N_HEADS):
        ot_ref[h * HEAD_DIM:(h + 1) * HEAD_DIM, :] = _softmax_out(acc_ref.at[h])
    o_ref[...] = ot_ref[...].T.astype(o_ref.dtype)


def _dsa(aq, ak, avt, iq, ik, iw):
    b, t, _ = aq.shape
    topk = min(TOPK_MAX, t // 4)
    qspec = pl.BlockSpec((None, TQ, BRANCH_W), lambda bb, i: (bb, i, 0))
    kspec, vspec = _kv_specs(t, BRANCH_W)
    pick = np.zeros((2 * SUBLANES, MXU_N), np.float32)
    for hh in range(IDX_HEADS):
        pick[hh, IDX_DIM + hh] = 1.0
    pick = jnp.asarray(pick, BF16)
    tri = jnp.asarray(np.tril(np.ones((CK, CK), np.float32)), BF16)
    kern = functools.partial(_dsa_kernel, topk=topk, idx_scale=(IDX_HEADS * IDX_DIM) ** -0.5)
    return pl.pallas_call(
        kern,
        out_shape=jax.ShapeDtypeStruct((b, t, BRANCH_W), BF16),
        grid=(b, t // TQ),
        in_specs=[qspec, kspec, vspec, qspec, kspec, qspec,
                  pl.BlockSpec(pick.shape, lambda bb, i: (0, 0)), pl.BlockSpec(tri.shape, lambda bb, i: (0, 0))],
        out_specs=qspec,
        scratch_shapes=[
            pltpu.VMEM((t // CK, CK, TQ), I32),
            pltpu.VMEM((t // CK, CK, TQ), I16),
            pltpu.VMEM((t // CK, CK, TQ), I16),
            pltpu.VMEM((t // CK, CK, TQ), I16),
            pltpu.VMEM((IDX_HEADS, IDX_DIM, TQ), BF16),
            pltpu.VMEM((N_HEADS, LANES, TQ), BF16),
            pltpu.VMEM((2 * SUBLANES, TQ), F32),
            pltpu.VMEM((1, TQ), I32),
        ] + _attn_scratch(N_HEADS),
        compiler_params=_cparams(2),
        name="dsa",
    )(aq, ak, avt, iq, ik, iw, pick, tri)


def _kbar_kernel(k_ref, o_ref):
    o_ref[...] = jnp.zeros(o_ref.shape, o_ref.dtype)
    nb = k_ref.shape[0] // MOBA_BLOCK
    for n in range(nb):
        blk = k_ref[n * MOBA_BLOCK:(n + 1) * MOBA_BLOCK, :].astype(F32)
        o_ref[n:n + 1, :] = jnp.mean(blk, axis=0, keepdims=True).astype(o_ref.dtype)


def _kbar(bk):
    b, t, w = bk.shape
    nbp = max(2 * SUBLANES, t // MOBA_BLOCK)
    return pl.pallas_call(
        _kbar_kernel,
        out_shape=jax.ShapeDtypeStruct((b, nbp, w), BF16),
        grid=(b,),
        in_specs=[pl.BlockSpec((None, t, w), lambda bb: (bb, 0, 0))],
        out_specs=pl.BlockSpec((None, nbp, w), lambda bb: (bb, 0, 0)),
        compiler_params=_cparams(1),
        name="moba_kbar",
    )(bk)


def _moba_kernel(q_ref, k_ref, vt_ref, kbar_ref, o_ref, qt_ref, bias_ref, s_ref, mx_ref, m_ref, acc_ref, ot_ref):
    i = pl.program_id(1)
    nbp = kbar_ref.shape[0]
    blk = lax.broadcasted_iota(I32, (nbp, TQ), 0)
    blk_f = blk.astype(F32)
    own = 2 * i + (lax.broadcasted_iota(I32, (nbp, TQ), 1) >> (MOBA_BLOCK.bit_length() - 1))
    _masked_qt(q_ref[...].astype(F32) * (HEAD_DIM ** -0.5 * LOG2E), 6, N_HEADS, qt_ref)

    for h in range(N_HEADS):
        g = jnp.where(blk < own, jnp.dot(_half(kbar_ref[...], h, 6), qt_ref[h], preferred_element_type=F32), NEG)
        bias = jnp.full((nbp, TQ), NEG, F32)
        for _ in range(MOBA_TOPK):
            mx = jnp.max(g, axis=0, keepdims=True)
            first = jnp.min(jnp.where(g == mx, blk_f, 1e9), axis=0, keepdims=True)
            pick = jnp.logical_and(blk_f == first, mx > 0.5 * NEG)
            bias = jnp.where(pick, 0.0, bias)
            g = jnp.where(pick, NEG, g)
        bias_ref[h] = jnp.where(blk == own, 0.0, bias)

    def qk_all(c):
        kc = k_ref[pl.ds(pl.multiple_of(c * CK, CK), CK), :]
        return [jnp.dot(_half(kc, h, 6), qt_ref[h], preferred_element_type=F32) for h in range(N_HEADS)]

    _flash_loop(2 * i, qk_all, lambda c, h, s: s + bias_ref[h, pl.ds(c, 1), :],
                lambda c, h: vt_ref[c, h * VROWS:(h + 1) * VROWS, :], (s_ref, mx_ref, m_ref, acc_ref))
    for h in range(N_HEADS):
        ot_ref[h * HEAD_DIM:(h + 1) * HEAD_DIM, :] = _softmax_out(acc_ref.at[h])
    o_ref[...] = ot_ref[...].T.astype(o_ref.dtype)


def _moba(bq, bk, bvt, kbar):
    b, t, w = bq.shape
    assert TQ == 2 * MOBA_BLOCK and CK == MOBA_BLOCK and t % TQ == 0
    nbp = kbar.shape[1]
    qspec = pl.BlockSpec((None, TQ, w), lambda bb, i: (bb, i, 0))
    kspec, vspec = _kv_specs(t, w)
    return pl.pallas_call(
        _moba_kernel,
        out_shape=jax.ShapeDtypeStruct((b, t, w), BF16),
        grid=(b, t // TQ),
        in_specs=[qspec, kspec, vspec, pl.BlockSpec((None, nbp, w), lambda bb, i: (bb, 0, 0))],
        out_specs=qspec,
        scratch_shapes=[pltpu.VMEM((N_HEADS, LANES, TQ), BF16), pltpu.VMEM((N_HEADS, nbp, TQ), F32)]
        + _attn_scratch(N_HEADS),
        compiler_params=_cparams(2),
        name="moba",
    )(bq, bk, bvt, kbar)


def _diff_kernel(q_ref, k_ref, vt_ref, lam_ref, norm_ref, misc_ref, o_ref,
                 qt_ref, s_ref, mx_ref, m_ref, acc_ref, ot_ref):
    i = pl.program_id(1)
    _masked_qt(q_ref[...].astype(F32) * (DIFF_DIM ** -0.5 * LOG2E), 5, 2 * N_HEADS, qt_ref)

    dl = lam_ref[...]
    lam_init = misc_ref[0:1, 0:1]
    lam = (jnp.exp(jnp.sum(dl[0:1, :] * dl[1:2, :], axis=1, keepdims=True))
           - jnp.exp(jnp.sum(dl[2:3, :] * dl[3:4, :], axis=1, keepdims=True)) + lam_init)

    def qk_all(c):
        kc = k_ref[pl.ds(pl.multiple_of(c * CK, CK), CK), :]
        return [jnp.dot(_half(kc, j, 5), qt_ref[j], preferred_element_type=F32) for j in range(2 * N_HEADS)]

    _flash_loop(2 * i, qk_all, None,
                lambda c, j: vt_ref[c, (j // 2) * VROWS:(j // 2 + 1) * VROWS, :],
                (s_ref, mx_ref, m_ref, acc_ref))

    post = norm_ref[...] * (1.0 - lam_init)
    for h in range(N_HEADS):
        o_h = _softmax_out(acc_ref.at[2 * h]) - lam * _softmax_out(acc_ref.at[2 * h + 1])
        ms = jnp.mean(o_h * o_h, axis=0, keepdims=True)
        ot_ref[h * HEAD_DIM:(h + 1) * HEAD_DIM, :] = o_h * lax.rsqrt(ms + RMS_EPS) * post
    o_ref[...] = ot_ref[...].T.astype(o_ref.dtype)


def _diff(cq, ck, cvt, lam, norm, misc):
    b, t, w = cq.shape
    qspec = pl.BlockSpec((None, TQ, w), lambda bb, i: (bb, i, 0))
    kspec, vspec = _kv_specs(t, w)
    full = lambda a: pl.BlockSpec(a.shape, lambda bb, i: (0,) * a.ndim)
    return pl.pallas_call(
        _diff_kernel,
        out_shape=jax.ShapeDtypeStruct((b, t, w), BF16),
        grid=(b, t // TQ),
        in_specs=[qspec, kspec, vspec, full(lam), full(norm), full(misc)],
        out_specs=qspec,
        scratch_shapes=[pltpu.VMEM((2 * N_HEADS, LANES, TQ), BF16)] + _attn_scratch(2 * N_HEADS),
        compiler_params=_cparams(2),
        name="diff",
    )(cq, ck, cvt, lam, norm, misc)


def _mla_prep_kernel(cq_ref, ckv_ref, kr_ref, qn_ref, kvn_ref, wq_ref, wqr_ref, wk_ref, wvt_ref,
                     p_ref, ct_ref, st_ref, q_out, k_out, vt_out):
    x = cq_ref[...].astype(F32)
    xn = (x * lax.rsqrt(jnp.mean(x * x, axis=1, keepdims=True) + RMS_EPS) * qn_ref[...]).astype(BF16)
    q = (jnp.dot(xn, wq_ref[...], preferred_element_type=F32) * ct_ref[...]
         + jnp.dot(xn, wqr_ref[...], preferred_element_type=F32) * st_ref[...])
    q_out[...] = q.astype(q_out.dtype)
    c = ckv_ref[:, :KV_LORA].astype(F32)
    cn = (c * lax.rsqrt(jnp.mean(c * c, axis=1, keepdims=True) + RMS_EPS) * kvn_ref[...]).astype(BF16)
    k = (jnp.dot(cn, wk_ref[...], preferred_element_type=F32)
         + jnp.dot(kr_ref[...], p_ref[...], preferred_element_type=F32))
    k_out[...] = k.astype(k_out.dtype)
    _store_vt(vt_out, _tn_dot(wvt_ref[...], cn))


def _mla_prep(dcq, ckv, kr, qn, kvn, wq, wqr, wk, wvt, pmat, ct, st, l):
    b, t, _ = dcq.shape
    tm = 512
    hw = N_HEADS * LANES
    row = lambda w: pl.BlockSpec((None, tm, w), lambda i, bb: (bb, i, 0))
    full = lambda a: pl.BlockSpec(a.shape, lambda i, bb: (0,) * a.ndim)
    tab = pl.BlockSpec((tm, hw), lambda i, bb: (i, 0))
    return pl.pallas_call(
        _mla_prep_kernel,
        out_shape=(jax.ShapeDtypeStruct((b, t, hw), BF16), jax.ShapeDtypeStruct((b, t, hw), BF16),
                   jax.ShapeDtypeStruct((b, t // CK, N_HEADS * VROWS, CK), BF16)),
        grid=(t // tm, b),
        in_specs=[row(Q_LORA), row(MXU_N), row(MXU_N), full(qn), full(kvn), _layer_spec(wq, l), _layer_spec(wqr, l),
                  _layer_spec(wk, l), _layer_spec(wvt, l), full(pmat), tab, tab],
        out_specs=(row(hw), row(hw),
                   pl.BlockSpec((None, tm // CK, N_HEADS * VROWS, CK), lambda i, bb: (bb, i, 0, 0))),
        compiler_params=_cparams(2),
        name="mla_prep",
    )(dcq, ckv, kr, qn, kvn, wq, wqr, wk, wvt, pmat, ct, st)


def _mla_kernel(q_ref, k_ref, vt_ref, o_ref, qt_ref, s_ref, mx_ref, m_ref, acc_ref, ot_ref):
    i = pl.program_id(1)
    hs = [slice(h * LANES, (h + 1) * LANES) for h in range(N_HEADS)]
    for h in range(N_HEADS):
        qt_ref[h] = q_ref[:, hs[h]].astype(F32).T.astype(BF16)

    def qk_all(c):
        start = pl.multiple_of(c * CK, CK)
        return [jnp.dot(k_ref[pl.ds(start, CK), hs[h]], qt_ref[h], preferred_element_type=F32)
                for h in range(N_HEADS)]

    _flash_loop(2 * i, qk_all, None,
                lambda c, h: vt_ref[c, h * VROWS:(h + 1) * VROWS, :],
                (s_ref, mx_ref, m_ref, acc_ref))
    for h in range(N_HEADS):
        ot_ref[h * HEAD_DIM:(h + 1) * HEAD_DIM, :] = _softmax_out(acc_ref.at[h])
    o_ref[...] = ot_ref[...].T.astype(o_ref.dtype)


def _mla(qm, km, vmt):
    b, t, hw = qm.shape
    kspec, vspec = _kv_specs(t, hw)
    return pl.pallas_call(
        _mla_kernel,
        out_shape=jax.ShapeDtypeStruct((b, t, BRANCH_W), BF16),
        grid=(b, t // TQ),
        in_specs=[pl.BlockSpec((None, TQ, hw), lambda bb, i: (bb, i, 0)), kspec, vspec],
        out_specs=pl.BlockSpec((None, TQ, BRANCH_W), lambda bb, i: (bb, i, 0)),
        scratch_shapes=[pltpu.VMEM((N_HEADS, LANES, TQ), BF16)] + _attn_scratch(N_HEADS),
        compiler_params=_cparams(2),
        name="mla",
    )(qm, km, vmt)


def _mem_kv_kernel(x_ref, w_ref, k_ref, vt_ref):
    x = x_ref[...].astype(BF16)
    k_ref[...] = jnp.dot(x, w_ref[:, :BRANCH_W], preferred_element_type=F32).astype(k_ref.dtype)
    _store_vt(vt_ref, _tn_dot(w_ref[:, BRANCH_W:], x))


def _mem_kv(mem, w, l):
    b, m, d = mem.shape
    assert m % CK == 0
    return pl.pallas_call(
        _mem_kv_kernel,
        out_shape=(jax.ShapeDtypeStruct((b, m, BRANCH_W), BF16),
                   jax.ShapeDtypeStruct((b, m // CK, N_HEADS * VROWS, CK), BF16)),
        grid=(b,),
        in_specs=[pl.BlockSpec((None, m, d), lambda bb: (bb, 0, 0)), _layer_spec(w, l)],
        out_specs=(pl.BlockSpec((None, m, BRANCH_W), lambda bb: (bb, 0, 0)),
                   pl.BlockSpec((None, m // CK, N_HEADS * VROWS, CK), lambda bb: (bb, 0, 0, 0))),
        compiler_params=_cparams(1),
        name="mem_kv",
    )(mem, w)


def _mem_kernel(q_ref, k_ref, vt_ref, o_ref, qt_ref, ot_ref):
    _masked_qt(q_ref[...].astype(F32) * (HEAD_DIM ** -0.5 * LOG2E), 6, N_HEADS, qt_ref)
    s_all = [jnp.dot(_half(k_ref[...], h, 6), qt_ref[h], preferred_element_type=F32) for h in range(N_HEADS)]
    for h in range(N_HEADS):
        s_t = s_all[h]
        p = jnp.exp2(s_t - jnp.max(s_t, axis=0, keepdims=True)).astype(BF16)
        acc = jnp.dot(vt_ref[0, h * VROWS:(h + 1) * VROWS, :], p, preferred_element_type=F32)
        ot_ref[h * HEAD_DIM:(h + 1) * HEAD_DIM, :] = acc[:HEAD_DIM, :] / acc[HEAD_DIM:HEAD_DIM + 1, :]
    o_ref[...] = ot_ref[...].T.astype(o_ref.dtype)


def _mem_attn(eq, mk, mvt):
    b, t, w = eq.shape
    m = mk.shape[1]
    assert m == CK
    return pl.pallas_call(
        _mem_kernel,
        out_shape=jax.ShapeDtypeStruct((b, t, w), BF16),
        grid=(b, t // TQ),
        in_specs=[pl.BlockSpec((None, TQ, w), lambda bb, i: (bb, i, 0)),
                  pl.BlockSpec((None, m, w), lambda bb, i: (bb, 0, 0)),
                  pl.BlockSpec((None,) + mvt.shape[1:], lambda bb, i: (bb, 0, 0, 0))],
        out_specs=pl.BlockSpec((None, TQ, w), lambda bb, i: (bb, i, 0)),
        scratch_shapes=[pltpu.VMEM((N_HEADS, LANES, TQ), BF16), pltpu.VMEM((BRANCH_W, TQ), F32)],
        compiler_params=_cparams(2),
        name="mem_attn",
    )(eq, mk, mvt)


def _final_kernel(h_ref, hb_ref, oa_ref, ob_ref, oc_ref, od_ref, oe_ref, z_ref,
                  wg_ref, wb_ref, wo_ref, g_ref, b_ref, h_out, hb_out, *, alpha):
    d = h_ref.shape[1]
    acc = jnp.zeros(h_ref.shape, F32)
    for n, o_ref in enumerate((oa_ref, ob_ref, oc_ref, od_ref, oe_ref)):
        z = z_ref[:, n * BRANCH_W:(n + 1) * BRANCH_W].astype(F32)
        y = o_ref[...].astype(F32) * (z / (1.0 + jnp.exp(-z)))
        u = jnp.dot(y.astype(BF16), wb_ref[n], preferred_element_type=F32)
        g = jnp.dot(hb_ref[...], wg_ref[:, n * d:(n + 1) * d], preferred_element_type=F32)
        acc = acc + u / (1.0 + jnp.exp(-g))
    out = jnp.dot(acc.astype(BF16), wo_ref[...], preferred_element_type=F32)
    x = alpha * h_ref[...] + out
    mu = jnp.mean(x, axis=1, keepdims=True)
    xc = x - mu
    var = jnp.mean(xc * xc, axis=1, keepdims=True)
    y = xc * lax.rsqrt(var + LN_EPS) * g_ref[...] + b_ref[...]
    h_out[...] = y
    hb_out[...] = y.astype(BF16)


def _final(h, hb, os5, z, wg, wb, wo, ln_g, ln_b, alpha, l):
    n, d = h.shape
    tm = 256
    row = lambda w: pl.BlockSpec((tm, w), lambda i: (i, 0))
    full = lambda a: pl.BlockSpec(a.shape, lambda i: (0,) * a.ndim)
    return pl.pallas_call(
        functools.partial(_final_kernel, alpha=alpha),
        out_shape=(jax.ShapeDtypeStruct((n, d), F32), jax.ShapeDtypeStruct((n, d), BF16)),
        grid=(n // tm,),
        in_specs=[row(d), row(d)] + [row(BRANCH_W)] * N_BRANCH + [row(N_BRANCH * BRANCH_W),
                  _layer_spec(wg, l), _layer_spec(wb, l), _layer_spec(wo, l), full(ln_g), full(ln_b)],
        out_specs=(row(d), row(d)),
        compiler_params=_cparams(1),
        name="merge_out_ln",
    )(h, hb, *os5, z, wg, wb, wo, ln_g, ln_b)


ROPE_GROUPS = (("a_q", N_HEADS, HEAD_DIM, ROT_64), ("a_k", N_HEADS, HEAD_DIM, ROT_64),
               ("i_q", IDX_HEADS, IDX_DIM, ROT_32), ("i_k", 1, MXU_N, ROT_32),
               ("b_q", N_HEADS, HEAD_DIM, ROT_64), ("b_k", N_HEADS, HEAD_DIM, ROT_64),
               ("c_q", 2 * N_HEADS, DIFF_DIM, ROT_32), ("c_k", 2 * N_HEADS, DIFF_DIM, ROT_32),
               ("d_kr", 1, MXU_N, MLA_ROPE))
PLAIN_COLS = ("d_cq", "d_ckv", "e_q") + tuple(("z", j) for j in range(N_BRANCH))
VALUE_COLS = ("a_v", "b_v", "c_v")
GATE_COLS = tuple(("g", j) for j in range(OFF["g"][1] // MXU_N))


def _window_start(col):
    name, j = col if isinstance(col, tuple) else (col, 0)
    return OFF[name][0] + j * MXU_N


def _weight_prep_kernel(offs_ref, wt_ref, o_ref):
    o_ref[...] = wt_ref[...].T.astype(o_ref.dtype)


def _weight_windows(wt, cols, name):
    depth, n, d = wt.shape
    starts = [_window_start(c) for c in cols]
    assert all(st % SUBLANES == 0 and st + MXU_N <= n for st in starts)
    grid_spec = pltpu.PrefetchScalarGridSpec(
        num_scalar_prefetch=1,
        grid=(depth, len(cols)),
        in_specs=[pl.BlockSpec((None, pl.Element(MXU_N), pl.Element(d)),
                               lambda l, j, offs: (l, pl.multiple_of(offs[j], SUBLANES), 0))],
        out_specs=pl.BlockSpec((None, d, MXU_N), lambda l, j, offs: (l, 0, j)),
    )
    return pl.pallas_call(
        _weight_prep_kernel,
        out_shape=jax.ShapeDtypeStruct((depth, d, MXU_N * len(cols)), BF16),
        grid_spec=grid_spec,
        compiler_params=_cparams(2),
        name=name,
    )(jnp.asarray(np.asarray(starts, np.int32)), wt)


def _weight_prep(w_in):
    wt = jnp.swapaxes(w_in, 1, 2)
    return (_weight_windows(wt, PLAIN_COLS, "wprep_plain"), _weight_windows(wt, VALUE_COLS, "wprep_value"),
            _weight_windows(wt, [name for name, *_ in ROPE_GROUPS], "wprep_rope"),
            _weight_windows(wt, GATE_COLS, "wprep_gate"))


def _rope_tables(seq, rot_dim):
    pos = jnp.arange(seq, dtype=F32)
    inv = ROPE_THETA ** (-jnp.arange(0, rot_dim, 2, dtype=F32) / rot_dim)
    ang = pos[:, None] * inv[None, :]
    return jnp.cos(ang), jnp.sin(ang)


def _rope_cs(t, nh, hd, r):
    cos, sin = _rope_tables(t, r)
    c = jnp.concatenate([cos, cos, jnp.ones((t, hd - r), F32)], axis=1)
    s = jnp.concatenate([-sin, sin, jnp.zeros((t, hd - r), F32)], axis=1)
    return jnp.tile(c, (1, nh)), jnp.tile(s, (1, nh))


def kernel(x, mem, ln0_g, ln0_b, w_in, mla_q_norm, w_uq, mla_kv_norm, w_ukv, diff_lam, diff_norm,
           w_mem_kv, w_branch, w_out, ln_g, ln_b):
    b, t, d = x.shape
    depth = w_in.shape[0]
    alpha = (2 * depth) ** 0.25
    assert t % 512 == 0 and d == 1024

    w_plain, w_vt, w_rope, wg = _weight_prep(w_in)
    plain_widths = (BRANCH_W,) * 3 + (N_BRANCH * BRANCH_W,)
    rope_heads = tuple((hd, r // 2) for _, _, hd, r in ROPE_GROUPS)
    patterns = sorted(set((nh, hd, r) for _, nh, hd, r in ROPE_GROUPS))
    rope_tables = tuple(patterns.index((nh, hd, r)) for _, nh, hd, r in ROPE_GROUPS)
    cs = [_rope_cs(t, nh, hd, r) for nh, hd, r in patterns]
    ctab = jnp.stack([c for c, _ in cs])
    stab = jnp.stack([s for _, s in cs])

    uq = w_uq.reshape(depth, Q_LORA, N_HEADS, MLA_NOPE + MLA_ROPE)
    qn_w, qr_w = uq[..., :MLA_NOPE], uq[..., MLA_NOPE:]
    pad32 = jnp.zeros((depth, Q_LORA, N_HEADS, LANES - MLA_NOPE - MLA_ROPE), w_uq.dtype)
    hw = N_HEADS * LANES
    wq = jnp.concatenate([qn_w, qr_w, pad32], axis=-1).reshape(depth, Q_LORA, hw).astype(BF16)
    half = MLA_ROPE // 2
    wq_rot = jnp.concatenate([jnp.zeros_like(qn_w), -qr_w[..., half:], qr_w[..., :half], pad32],
                             axis=-1).reshape(depth, Q_LORA, hw).astype(BF16)
    cos_m, sin_m = _rope_tables(t, MLA_ROPE)
    one = lambda n: jnp.ones((t, n), F32)
    zer = lambda n: jnp.zeros((t, n), F32)
    qs = (MLA_NOPE + MLA_ROPE) ** -0.5 * LOG2E
    ct_q = qs * jnp.tile(jnp.concatenate([one(MLA_NOPE), cos_m, cos_m, one(LANES - MLA_NOPE - MLA_ROPE)], axis=1), (1, N_HEADS))
    st_q = qs * jnp.tile(jnp.concatenate([zer(MLA_NOPE), sin_m, sin_m, zer(LANES - MLA_NOPE - MLA_ROPE)], axis=1), (1, N_HEADS))
    ukv = w_ukv.reshape(depth, KV_LORA, N_HEADS, MLA_NOPE + MLA_V)
    wk = jnp.concatenate([ukv[..., :MLA_NOPE], jnp.zeros((depth, KV_LORA, N_HEADS, LANES - MLA_NOPE), w_ukv.dtype)],
                         axis=-1).reshape(depth, KV_LORA, hw).astype(BF16)
    wvt = ukv[..., MLA_NOPE:].reshape(depth, KV_LORA, N_HEADS * MLA_V).astype(BF16)
    place = np.zeros((MXU_N, hw), np.float32)
    for hh in range(N_HEADS):
        for j in range(MLA_ROPE):
            place[j, hh * LANES + MLA_NOPE + j] = 1.0
    place = jnp.asarray(place, BF16)

    wb = w_branch.astype(BF16)
    wo = w_out.astype(BF16)
    wmem = w_mem_kv.astype(BF16)
    norm_t = jnp.broadcast_to(diff_norm.astype(F32)[:, :, None], (depth, HEAD_DIM, TQ))

    h, hb = _layer_norm0(x.reshape(b * t, d), ln0_g, ln0_b)
    for l in range(depth):
        hb3 = hb.reshape(b, t, d)
        avt, bvt, cvt, dcq, ckv_iw, eq, z = _proj_plain(hb3, w_plain, w_vt, plain_widths, l)
        aq, ak, iq, ik, bq, bk, cq, ck, kr = _proj_rope(hb3, w_rope, ctab, stab, rope_heads, rope_tables, l)

        o_a = _dsa(aq, ak, avt, iq, ik, ik)
        o_b = _moba(bq, bk, bvt, _kbar(bk))
        lam_init = 0.8 - 0.6 * math.exp(-0.3 * l)
        misc = jnp.full((SUBLANES, LANES), lam_init, F32)
        o_c = _diff(cq, ck, cvt, diff_lam[l].astype(F32), norm_t[l], misc)
        qm, km, vmt = _mla_prep(dcq, ckv_iw, kr, mla_q_norm[l].reshape(1, Q_LORA), mla_kv_norm[l].reshape(1, KV_LORA),
                                wq, wq_rot, wk, wvt, place, ct_q, st_q, l)
        o_d = _mla(qm, km, vmt)
        o_e = _mem_attn(eq, *_mem_kv(mem, wmem, l))

        os5 = [o.reshape(b * t, BRANCH_W) for o in (o_a, o_b, o_c, o_d, o_e)]
        h, hb = _final(h, hb, os5, z.reshape(b * t, N_BRANCH * BRANCH_W), wg, wb, wo,
                       ln_g[l].reshape(1, d), ln_b[l].reshape(1, d), alpha, l)
    return h.reshape(b, t, d)
```

```python
import functools
import math

import numpy as np
import jax
import jax.numpy as jnp
from jax import lax
from jax.experimental import pallas as pl
from jax.experimental.pallas import tpu as pltpu

F32 = jnp.float32
BF16 = jnp.bfloat16
I32 = jnp.int32
I16 = jnp.int16

N_HEADS = 4
HEAD_DIM = 64
BRANCH_W = N_HEADS * HEAD_DIM
N_BRANCH = 5
ROPE_THETA = 500000.0
ROT_64 = 16
ROT_32 = 8
IDX_HEADS = 8
IDX_DIM = 32
TOPK_MAX = 256
MOBA_BLOCK = 256
MOBA_TOPK = 3
DIFF_DIM = 32
Q_LORA = 256
KV_LORA = 128
MLA_NOPE = 64
MLA_ROPE = 32
MLA_V = 64
LN_EPS = 1e-5
RMS_EPS = 1e-6

IN_LAYOUT = (
    ("a_q", BRANCH_W), ("a_k", BRANCH_W), ("a_v", BRANCH_W),
    ("i_q", IDX_HEADS * IDX_DIM), ("i_k", IDX_DIM), ("i_w", IDX_HEADS),
    ("b_q", BRANCH_W), ("b_k", BRANCH_W), ("b_v", BRANCH_W),
    ("c_q", BRANCH_W), ("c_k", BRANCH_W), ("c_v", BRANCH_W),
    ("d_cq", Q_LORA), ("d_ckv", KV_LORA), ("d_kr", MLA_ROPE),
    ("e_q", BRANCH_W),
    ("z", N_BRANCH * BRANCH_W),
    ("g", N_BRANCH * 1024),
)

SUBLANES = 8
LANES = 128
MXU_N = 256
TQ = 512
CK = 256
VROWS = HEAD_DIM + 16
FLASH_UNROLL = 4
NEG = -1e30
LOG2E = math.log2(math.e)
INT_MIN = np.int32(-2 ** 31)
HALF16 = 1 << 15
VMEM_LIMIT = 56 * 1024 * 1024


def _offsets():
    off, out = 0, {}
    for name, size in IN_LAYOUT:
        out[name] = (off, size)
        off += size
    return out


OFF = _offsets()


def _nt_dot(a, b):
    return lax.dot_general(a, b, (((1,), (1,)), ((), ())), preferred_element_type=F32)


def _tn_dot(w, x):
    return lax.dot_general(w, x, (((0,), (1,)), ((), ())), preferred_element_type=F32)


def _fold_rows(w, rows=SUBLANES):
    xs = [w[r:r + rows, :] for r in range(0, w.shape[0], rows)]
    while len(xs) > 1:
        xs = [xs[j] + xs[j + 1] for j in range(0, len(xs) - 1, 2)] + ([xs[-1]] if len(xs) % 2 else [])
    return xs[0]


def _masked_qt(q, shift, n, qt_ref):
    qt = q.T
    dim = lax.broadcasted_iota(I32, (LANES, qt.shape[1]), 0)
    for j in range(n):
        half = (j << shift) // LANES
        rows = qt[half * LANES:(half + 1) * LANES, :]
        qt_ref[j] = jnp.where(((dim + half * LANES) >> shift) == j, rows, 0.0).astype(BF16)


def _half(kc, j, shift):
    half = (j << shift) // LANES
    return kc[:, half * LANES:(half + 1) * LANES]


def _cparams(n_axes):
    return pltpu.CompilerParams(dimension_semantics=("arbitrary",) * n_axes,
                                vmem_limit_bytes=VMEM_LIMIT)


def _layer_spec(a, l):
    return pl.BlockSpec((None,) + a.shape[1:], lambda *_: (l,) + (0,) * (a.ndim - 1))


def _softmax_step(s_t, m_tile, vt_h, m_ref, acc_ref):
    m_old = m_ref[...]
    m_new = jnp.maximum(m_old, m_tile)
    alpha = jnp.exp2(m_old - m_new)
    p = jnp.exp2(s_t - m_new)
    acc_ref[...] = alpha * acc_ref[...] + jnp.dot(vt_h, p.astype(BF16), preferred_element_type=F32)
    m_ref[...] = m_new


def _softmax_init(m_ref, acc_ref):
    m_ref[...] = jnp.full(m_ref.shape, NEG, F32)
    acc_ref[...] = jnp.zeros(acc_ref.shape, F32)


def _softmax_out(acc_ref):
    return acc_ref[:HEAD_DIM, :] / acc_ref[HEAD_DIM:HEAD_DIM + 1, :]


def _store_vt(o_ref, vt):
    ones = jnp.ones((VROWS - HEAD_DIM, CK), o_ref.dtype)
    for j in range(o_ref.shape[0]):
        for h in range(N_HEADS):
            o_ref[j, h * VROWS:h * VROWS + HEAD_DIM, :] = (
                vt[h * HEAD_DIM:(h + 1) * HEAD_DIM, j * CK:(j + 1) * CK].astype(o_ref.dtype))
            o_ref[j, h * VROWS + HEAD_DIM:(h + 1) * VROWS, :] = ones


def _flash_loop(n_full, qk_all, mask, vt_rows, state, prep=None, causal_tail=True):
    s_ref, mx_ref, m_ref, acc_ref = state
    n_state = m_ref.shape[0]
    for j in range(n_state):
        _softmax_init(m_ref.at[j], acc_ref.at[j])

    def park(c, slot):
        ctx = c if prep is None else prep(c)
        for j, s in enumerate(qk_all(c)):
            if mask is not None:
                s = mask(ctx, j, s)
            s_ref[slot, j] = s
            mx_ref[slot, j] = jnp.max(s, axis=0, keepdims=True)

    def consume(c, slot, d):
        for j in range(n_state):
            if d is None or not causal_tail:
                s, m_tile = s_ref[slot, j], mx_ref[slot, j]
            else:
                s = jnp.where(_causal(d), s_ref[slot, j], NEG)
                m_tile = jnp.max(s, axis=0, keepdims=True)
            _softmax_step(s, m_tile, vt_rows(c, j), m_ref.at[j], acc_ref.at[j])

    park(0, 0)

    def pair(c):
        park(c + 1, 1)
        consume(c, 0, None)
        park(c + 2, 0)
        consume(c + 1, 1, None)

    def body(g, carry):
        for u in range(0, FLASH_UNROLL, 2):
            pair(FLASH_UNROLL * g + u)
        return carry

    n_group = lax.shift_right_logical(n_full, FLASH_UNROLL.bit_length() - 1)
    lax.fori_loop(0, n_group, body, 0)
    c0 = FLASH_UNROLL * n_group
    for u in range(FLASH_UNROLL // 2 - 1):
        @pl.when(n_full - c0 >= 2 * (u + 1))
        def _(u=u):
            pair(c0 + 2 * u)
    park(n_full + 1, 1)
    consume(n_full, 0, 0)
    consume(n_full + 1, 1, 1)


def _causal(d):
    kpos = lax.broadcasted_iota(I32, (CK, TQ), 0) + d * CK
    return kpos <= lax.broadcasted_iota(I32, (CK, TQ), 1)


def _attn_scratch(n_state):
    return [pltpu.VMEM((2, n_state, CK, TQ), F32), pltpu.VMEM((2, n_state, 1, TQ), F32),
            pltpu.VMEM((n_state, 1, TQ), F32), pltpu.VMEM((n_state, VROWS, TQ), F32),
            pltpu.VMEM((BRANCH_W, TQ), F32)]


def _kv_specs(t, w):
    kspec = pl.BlockSpec((None, t, w), lambda bb, i: (bb, 0, 0))
    vspec = pl.BlockSpec((None, t // CK, N_HEADS * VROWS, CK), lambda bb, i: (bb, 0, 0, 0))
    return kspec, vspec


def _ln_kernel(x_ref, g_ref, b_ref, h_ref, hb_ref):
    x = x_ref[...]
    mu = jnp.mean(x, axis=1, keepdims=True)
    xc = x - mu
    var = jnp.mean(xc * xc, axis=1, keepdims=True)
    y = xc * lax.rsqrt(var + LN_EPS) * g_ref[...] + b_ref[...]
    h_ref[...] = y
    hb_ref[...] = y.astype(BF16)


def _layer_norm0(x2, g, b):
    n, d = x2.shape
    tm = 512
    row = pl.BlockSpec((tm, d), lambda i: (i, 0))
    vec = pl.BlockSpec((1, d), lambda i: (0, 0))
    return pl.pallas_call(
        _ln_kernel,
        out_shape=(jax.ShapeDtypeStruct((n, d), F32), jax.ShapeDtypeStruct((n, d), BF16)),
        grid=(n // tm,),
        in_specs=[row, vec, vec],
        out_specs=(row, row),
        compiler_params=_cparams(1),
        name="ln0",
    )(x2, g.reshape(1, d), b.reshape(1, d))


def _proj_plain_kernel(x_ref, w_ref, wt_ref, *out_refs, n_t):
    for g, o_ref in enumerate(out_refs[:n_t]):
        _store_vt(o_ref, _tn_dot(wt_ref[:, g * BRANCH_W:(g + 1) * BRANCH_W], x_ref[...]))
    off = 0
    for o_ref in out_refs[n_t:]:
        wd = o_ref.shape[-1]
        for j in range(0, wd, MXU_N):
            acc = jnp.dot(x_ref[...], w_ref[:, off + j:off + j + MXU_N], preferred_element_type=F32)
            o_ref[:, j:j + MXU_N] = acc.astype(o_ref.dtype)
        off += wd


def _proj_plain(hb3, w, wt, widths, l):
    b, t, d = hb3.shape
    tm = 512
    n_t = wt.shape[-1] // BRANCH_W
    shapes = [jax.ShapeDtypeStruct((b, t // CK, N_HEADS * VROWS, CK), BF16)] * n_t
    specs = [pl.BlockSpec((None, tm // CK, N_HEADS * VROWS, CK), lambda i, bb: (bb, i, 0, 0))] * n_t
    shapes += [jax.ShapeDtypeStruct((b, t, wd), BF16) for wd in widths]
    specs += [pl.BlockSpec((None, tm, wd), lambda i, bb: (bb, i, 0)) for wd in widths]
    return pl.pallas_call(
        functools.partial(_proj_plain_kernel, n_t=n_t),
        out_shape=tuple(shapes),
        grid=(t // tm, b),
        in_specs=[pl.BlockSpec((None, tm, d), lambda i, bb: (bb, i, 0)),
                  _layer_spec(w, l), _layer_spec(wt, l)],
        out_specs=tuple(specs),
        compiler_params=_cparams(2),
        name="proj_plain",
    )(hb3, w, wt)


def _proj_rope_kernel(x_ref, w_ref, c_ref, s_ref, *out_refs, heads, tables):
    lane = lax.broadcasted_iota(I32, (x_ref.shape[0], MXU_N), 1)
    for g, o_ref in enumerate(out_refs):
        hd, half = heads[g]
        sl = slice(g * MXU_N, (g + 1) * MXU_N)
        acc = jnp.dot(x_ref[...], w_ref[:, sl], preferred_element_type=F32)
        partner = jnp.where((lane & (hd - 1)) < half,
                            pltpu.roll(acc, MXU_N - half, 1), pltpu.roll(acc, half, 1))
        o_ref[...] = (acc * c_ref[tables[g]] + partner * s_ref[tables[g]]).astype(o_ref.dtype)


def _proj_rope(hb3, w, ctab, stab, heads, tables, l):
    b, t, d = hb3.shape
    tm = 512
    assert w.shape[-1] == MXU_N * len(heads)
    tspec = pl.BlockSpec((ctab.shape[0], tm, MXU_N), lambda i, bb: (0, i, 0))
    ospec = pl.BlockSpec((None, tm, MXU_N), lambda i, bb: (bb, i, 0))
    return pl.pallas_call(
        functools.partial(_proj_rope_kernel, heads=heads, tables=tables),
        out_shape=(jax.ShapeDtypeStruct((b, t, MXU_N), BF16),) * len(heads),
        grid=(t // tm, b),
        in_specs=[pl.BlockSpec((None, tm, d), lambda i, bb: (bb, i, 0)),
                  _layer_spec(w, l), tspec, tspec],
        out_specs=(ospec,) * len(heads),
        compiler_params=_cparams(2),
        name="proj_rope",
    )(hb3, w, ctab, stab)


def _dsa_kernel(aq_ref, ak_ref, avt_ref, iq_ref, ik_ref, iw_ref, pick_ref, tri_ref, o_ref,
                keys_ref, hi_ref, lo_ref, bk_ref, iqt_ref, aqt_ref, wt_ref, thr_ref, s_ref, mx_ref, m_ref, acc_ref, ot_ref,
                *, topk, idx_scale):
    i = pl.program_id(1)
    n_full = 2 * i
    n_pair = i + 1

    iqt = iq_ref[...].astype(F32).T
    for hh in range(IDX_HEADS):
        iqt_ref[hh] = iqt[hh * IDX_DIM:(hh + 1) * IDX_DIM, :].astype(BF16)
    _masked_qt(aq_ref[...].astype(F32) * (HEAD_DIM ** -0.5 * LOG2E), 6, N_HEADS, aqt_ref)
    wt_ref[...] = _nt_dot(pick_ref[...], iw_ref[...]) * idx_scale

    def logits(c):
        kc = ik_ref[pl.ds(pl.multiple_of(c * CK, CK), CK), :]
        return [jnp.dot(kc[:, :IDX_DIM], iqt_ref[hh], preferred_element_type=F32) for hh in range(IDX_HEADS)]

    def put_keys(c, key):
        keys_ref[c] = key
        hi_ref[c] = (key >> 16).astype(I16)
        lo_ref[c] = ((key & 0xFFFF) - HALF16).astype(I16)

    def score_chunk(c, lg, d):
        sc = jnp.zeros((CK, TQ), F32)
        for hh in range(IDX_HEADS):
            sc = sc + jnp.maximum(lg[hh], 0.0) * wt_ref[hh:hh + 1, :]
        bits = pltpu.bitcast(sc, I32)
        key = jnp.where(bits < 0, INT_MIN - bits, bits)
        put_keys(c, key if d is None else jnp.where(_causal(d), key, INT_MIN))

    def score_pair(c, d0, d1):
        lg0, lg1 = logits(c), logits(c + 1)
        score_chunk(c, lg0, d0)
        score_chunk(c + 1, lg1, d1)

    def score_body(p, carry):
        score_pair(2 * p, None, None)
        return carry

    lax.fori_loop(0, i, score_body, 0)
    score_pair(n_full, 0, 1)

    def pair_loop(body, init):
        def pair(p, carry):
            return body(2 * p + 1, body(2 * p, carry))
        return lax.fori_loop(0, n_pair, pair, init)

    def count16(pred, also=None):
        def body(c, part):
            hit = jnp.where(pred(c), jnp.int16(1), jnp.int16(0))
            if also is not None:
                hit = jnp.where(also(c), hit, jnp.int16(0))
            return part + _fold_rows(hit, 2 * SUBLANES)
        part = pair_loop(body, jnp.zeros((2 * SUBLANES, TQ), I16))
        return jnp.sum(part.astype(F32), axis=0, keepdims=True)

    def search16(ref, need):
        def bit_body(bi, t_u):
            c_u = t_u | jnp.left_shift(jnp.int32(1), 15 - bi)
            ck = (c_u - HALF16).astype(I16)
            cnt = count16(lambda c: ref[c] >= ck)
            return jnp.where(cnt >= need, c_u, t_u)
        return lax.fori_loop(0, 16, bit_body, jnp.zeros((1, TQ), I32))

    hi_u = search16(hi_ref, float(topk))
    thr_hi = (hi_u - HALF16).astype(I16)
    n_above = count16(lambda c: hi_ref[c] > thr_hi)

    def bucket_body(c, carry):
        bk_ref[c] = jnp.where(hi_ref[c] == thr_hi, lo_ref[c], jnp.int16(-HALF16))
        return carry

    pair_loop(bucket_body, 0)
    lo_u = search16(bk_ref, float(topk) - n_above)
    thr_lo = (lo_u - HALF16).astype(I16)
    thr = ((hi_u - HALF16) << 16) | lo_u

    n_gt = n_above + count16(lambda c: bk_ref[c] > thr_lo)
    n_eq = count16(lambda c: lo_ref[c] == thr_lo, also=lambda c: hi_ref[c] == thr_hi)
    need = float(topk) - n_gt
    amb = jnp.logical_and(n_eq > need, thr > INT_MIN)
    any_amb = jnp.max(jnp.where(amb, 1.0, 0.0)) > 0.5

    @pl.when(any_amb)
    def _():
        def drop_body(c, seen):
            k = keys_ref[c]
            eq = k == thr
            eqf = jnp.where(eq, 1.0, 0.0)
            rank = jnp.dot(tri_ref[...], eqf.astype(BF16), preferred_element_type=F32) + seen
            drop = jnp.logical_and(jnp.logical_and(eq, rank > need), amb)
            keys_ref[c] = jnp.where(drop, INT_MIN, k)
            return seen + jnp.sum(eqf, axis=0, keepdims=True)

        pair_loop(drop_body, jnp.zeros((1, TQ), F32))

    thr_ref[...] = jnp.maximum(thr, INT_MIN + 1)

    def qk_all(c):
        kc = ak_ref[pl.ds(pl.multiple_of(c * CK, CK), CK), :]
        return [jnp.dot(_half(kc, h, 6), aqt_ref[h], preferred_element_type=F32) for h in range(N_HEADS)]

    _flash_loop(n_full, qk_all,
                lambda keep, h, s: jnp.where(keep, s, NEG),
                lambda c, h: avt_ref[c, h * VROWS:(h + 1) * VROWS, :],
                (s_ref, mx_ref, m_ref, acc_ref),
                prep=lambda c: keys_ref[c] >= thr_ref[...], causal_tail=False)
    for h in range(N_HEADS):
        ot_ref[h * HEAD_DIM:(h + 1) * HEAD_DIM, :] = _softmax_out(acc_ref.at[h])
    o_ref[...] = ot_ref[...].T.astype(o_ref.dtype)


def _dsa(aq, ak, avt, iq, ik, iw):
    b, t, _ = aq.shape
    topk = min(TOPK_MAX, t // 4)
    qspec = pl.BlockSpec((None, TQ, BRANCH_W), lambda bb, i: (bb, i, 0))
    kspec, vspec = _kv_specs(t, BRANCH_W)
    pick = np.zeros((2 * SUBLANES, MXU_N), np.float32)
    for hh in range(IDX_HEADS):
        pick[hh, IDX_DIM + hh] = 1.0
    pick = jnp.asarray(pick, BF16)
    tri = jnp.asarray(np.tril(np.ones((CK, CK), np.float32)), BF16)
    kern = functools.partial(_dsa_kernel, topk=topk, idx_scale=(IDX_HEADS * IDX_DIM) ** -0.5)
    return pl.pallas_call(
        kern,
        out_shape=jax.ShapeDtypeStruct((b, t, BRANCH_W), BF16),
        grid=(b, t // TQ),
        in_specs=[qspec, kspec, vspec, qspec, kspec, qspec,
                  pl.BlockSpec(pick.shape, lambda bb, i: (0, 0)), pl.BlockSpec(tri.shape, lambda bb, i: (0, 0))],
        out_specs=qspec,
        scratch_shapes=[
            pltpu.VMEM((t // CK, CK, TQ), I32),
            pltpu.VMEM((t // CK, CK, TQ), I16),
            pltpu.VMEM((t // CK, CK, TQ), I16),
            pltpu.VMEM((t // CK, CK, TQ), I16),
            pltpu.VMEM((IDX_HEADS, IDX_DIM, TQ), BF16),
            pltpu.VMEM((N_HEADS, LANES, TQ), BF16),
            pltpu.VMEM((2 * SUBLANES, TQ), F32),
            pltpu.VMEM((1, TQ), I32),
        ] + _attn_scratch(N_HEADS),
        compiler_params=_cparams(2),
        name="dsa",
    )(aq, ak, avt, iq, ik, iw, pick, tri)


def _kbar_kernel(k_ref, o_ref):
    o_ref[...] = jnp.zeros(o_ref.shape, o_ref.dtype)
    nb = k_ref.shape[0] // MOBA_BLOCK
    for n in range(nb):
        blk = k_ref[n * MOBA_BLOCK:(n + 1) * MOBA_BLOCK, :].astype(F32)
        o_ref[n:n + 1, :] = jnp.mean(blk, axis=0, keepdims=True).astype(o_ref.dtype)


def _kbar(bk):
    b, t, w = bk.shape
    nbp = max(2 * SUBLANES, t // MOBA_BLOCK)
    return pl.pallas_call(
        _kbar_kernel,
        out_shape=jax.ShapeDtypeStruct((b, nbp, w), BF16),
        grid=(b,),
        in_specs=[pl.BlockSpec((None, t, w), lambda bb: (bb, 0, 0))],
        out_specs=pl.BlockSpec((None, nbp, w), lambda bb: (bb, 0, 0)),
        compiler_params=_cparams(1),
        name="moba_kbar",
    )(bk)


def _moba_kernel(q_ref, k_ref, vt_ref, kbar_ref, o_ref, qt_ref, bias_ref, s_ref, mx_ref, m_ref, acc_ref, ot_ref):
    i = pl.program_id(1)
    nbp = kbar_ref.shape[0]
    blk = lax.broadcasted_iota(I32, (nbp, TQ), 0)
    blk_f = blk.astype(F32)
    own = 2 * i + (lax.broadcasted_iota(I32, (nbp, TQ), 1) >> (MOBA_BLOCK.bit_length() - 1))
    _masked_qt(q_ref[...].astype(F32) * (HEAD_DIM ** -0.5 * LOG2E), 6, N_HEADS, qt_ref)

    for h in range(N_HEADS):
        g = jnp.where(blk < own, jnp.dot(_half(kbar_ref[...], h, 6), qt_ref[h], preferred_element_type=F32), NEG)
        bias = jnp.full((nbp, TQ), NEG, F32)
        for _ in range(MOBA_TOPK):
            mx = jnp.max(g, axis=0, keepdims=True)
            first = jnp.min(jnp.where(g == mx, blk_f, 1e9), axis=0, keepdims=True)
            pick = jnp.logical_and(blk_f == first, mx > 0.5 * NEG)
            bias = jnp.where(pick, 0.0, bias)
            g = jnp.where(pick, NEG, g)
        bias_ref[h] = jnp.where(blk == own, 0.0, bias)

    def qk_all(c):
        kc = k_ref[pl.ds(pl.multiple_of(c * CK, CK), CK), :]
        return [jnp.dot(_half(kc, h, 6), qt_ref[h], preferred_element_type=F32) for h in range(N_HEADS)]

    _flash_loop(2 * i, qk_all, lambda c, h, s: s + bias_ref[h, pl.ds(c, 1), :],
                lambda c, h: vt_ref[c, h * VROWS:(h + 1) * VROWS, :], (s_ref, mx_ref, m_ref, acc_ref))
    for h in range(N_HEADS):
        ot_ref[h * HEAD_DIM:(h + 1) * HEAD_DIM, :] = _softmax_out(acc_ref.at[h])
    o_ref[...] = ot_ref[...].T.astype(o_ref.dtype)


def _moba(bq, bk, bvt, kbar):
    b, t, w = bq.shape
    assert TQ == 2 * MOBA_BLOCK and CK == MOBA_BLOCK and t % TQ == 0
    nbp = kbar.shape[1]
    qspec = pl.BlockSpec((None, TQ, w), lambda bb, i: (bb, i, 0))
    kspec, vspec = _kv_specs(t, w)
    return pl.pallas_call(
        _moba_kernel,
        out_shape=jax.ShapeDtypeStruct((b, t, w), BF16),
        grid=(b, t // TQ),
        in_specs=[qspec, kspec, vspec, pl.BlockSpec((None, nbp, w), lambda bb, i: (bb, 0, 0))],
        out_specs=qspec,
        scratch_shapes=[pltpu.VMEM((N_HEADS, LANES, TQ), BF16), pltpu.VMEM((N_HEADS, nbp, TQ), F32)]
        + _attn_scratch(N_HEADS),
        compiler_params=_cparams(2),
        name="moba",
    )(bq, bk, bvt, kbar)


def _diff_kernel(q_ref, k_ref, vt_ref, lam_ref, norm_ref, misc_ref, o_ref,
                 qt_ref, s_ref, mx_ref, m_ref, acc_ref, ot_ref):
    i = pl.program_id(1)
    _masked_qt(q_ref[...].astype(F32) * (DIFF_DIM ** -0.5 * LOG2E), 5, 2 * N_HEADS, qt_ref)

    dl = lam_ref[...]
    lam_init = misc_ref[0:1, 0:1]
    lam = (jnp.exp(jnp.sum(dl[0:1, :] * dl[1:2, :], axis=1, keepdims=True))
           - jnp.exp(jnp.sum(dl[2:3, :] * dl[3:4, :], axis=1, keepdims=True)) + lam_init)

    def qk_all(c):
        kc = k_ref[pl.ds(pl.multiple_of(c * CK, CK), CK), :]
        return [jnp.dot(_half(kc, j, 5), qt_ref[j], preferred_element_type=F32) for j in range(2 * N_HEADS)]

    _flash_loop(2 * i, qk_all, None,
                lambda c, j: vt_ref[c, (j // 2) * VROWS:(j // 2 + 1) * VROWS, :],
                (s_ref, mx_ref, m_ref, acc_ref))

    post = norm_ref[...] * (1.0 - lam_init)
    for h in range(N_HEADS):
        o_h = _softmax_out(acc_ref.at[2 * h]) - lam * _softmax_out(acc_ref.at[2 * h + 1])
        ms = jnp.mean(o_h * o_h, axis=0, keepdims=True)
        ot_ref[h * HEAD_DIM:(h + 1) * HEAD_DIM, :] = o_h * lax.rsqrt(ms + RMS_EPS) * post
    o_ref[...] = ot_ref[...].T.astype(o_ref.dtype)


def _diff(cq, ck, cvt, lam, norm, misc):
    b, t, w = cq.shape
    qspec = pl.BlockSpec((None, TQ, w), lambda bb, i: (bb, i, 0))
    kspec, vspec = _kv_specs(t, w)
    full = lambda a: pl.BlockSpec(a.shape, lambda bb, i: (0,) * a.ndim)
    return pl.pallas_call(
        _diff_kernel,
        out_shape=jax.ShapeDtypeStruct((b, t, w), BF16),
        grid=(b, t // TQ),
        in_specs=[qspec, kspec, vspec, full(lam), full(norm), full(misc)],
        out_specs=qspec,
        scratch_shapes=[pltpu.VMEM((2 * N_HEADS, LANES, TQ), BF16)] + _attn_scratch(2 * N_HEADS),
        compiler_params=_cparams(2),
        name="diff",
    )(cq, ck, cvt, lam, norm, misc)


def _mla_prep_kernel(cq_ref, ckv_ref, kr_ref, qn_ref, kvn_ref, wq_ref, wqr_ref, wk_ref, wvt_ref,
                     p_ref, ct_ref, st_ref, q_out, k_out, vt_out):
    x = cq_ref[...].astype(F32)
    xn = (x * lax.rsqrt(jnp.mean(x * x, axis=1, keepdims=True) + RMS_EPS) * qn_ref[...]).astype(BF16)
    q = (jnp.dot(xn, wq_ref[...], preferred_element_type=F32) * ct_ref[...]
         + jnp.dot(xn, wqr_ref[...], preferred_element_type=F32) * st_ref[...])
    q_out[...] = q.astype(q_out.dtype)
    c = ckv_ref[:, :KV_LORA].astype(F32)
    cn = (c * lax.rsqrt(jnp.mean(c * c, axis=1, keepdims=True) + RMS_EPS) * kvn_ref[...]).astype(BF16)
    k = (jnp.dot(cn, wk_ref[...], preferred_element_type=F32)
         + jnp.dot(kr_ref[...], p_ref[...], preferred_element_type=F32))
    k_out[...] = k.astype(k_out.dtype)
    _store_vt(vt_out, _tn_dot(wvt_ref[...], cn))


def _mla_prep(dcq, ckv, kr, qn, kvn, wq, wqr, wk, wvt, pmat, ct, st, l):
    b, t, _ = dcq.shape
    tm = 512
    hw = N_HEADS * LANES
    row = lambda w: pl.BlockSpec((None, tm, w), lambda i, bb: (bb, i, 0))
    full = lambda a: pl.BlockSpec(a.shape, lambda i, bb: (0,) * a.ndim)
    tab = pl.BlockSpec((tm, hw), lambda i, bb: (i, 0))
    return pl.pallas_call(
        _mla_prep_kernel,
        out_shape=(jax.ShapeDtypeStruct((b, t, hw), BF16), jax.ShapeDtypeStruct((b, t, hw), BF16),
                   jax.ShapeDtypeStruct((b, t // CK, N_HEADS * VROWS, CK), BF16)),
        grid=(t // tm, b),
        in_specs=[row(Q_LORA), row(MXU_N), row(MXU_N), full(qn), full(kvn), _layer_spec(wq, l), _layer_spec(wqr, l),
                  _layer_spec(wk, l), _layer_spec(wvt, l), full(pmat), tab, tab],
        out_specs=(row(hw), row(hw),
                   pl.BlockSpec((None, tm // CK, N_HEADS * VROWS, CK), lambda i, bb: (bb, i, 0, 0))),
        compiler_params=_cparams(2),
        name="mla_prep",
    )(dcq, ckv, kr, qn, kvn, wq, wqr, wk, wvt, pmat, ct, st)


def _mla_kernel(q_ref, k_ref, vt_ref, o_ref, qt_ref, s_ref, mx_ref, m_ref, acc_ref, ot_ref):
    i = pl.program_id(1)
    hs = [slice(h * LANES, (h + 1) * LANES) for h in range(N_HEADS)]
    for h in range(N_HEADS):
        qt_ref[h] = q_ref[:, hs[h]].astype(F32).T.astype(BF16)

    def qk_all(c):
        start = pl.multiple_of(c * CK, CK)
        return [jnp.dot(k_ref[pl.ds(start, CK), hs[h]], qt_ref[h], preferred_element_type=F32)
                for h in range(N_HEADS)]

    _flash_loop(2 * i, qk_all, None,
                lambda c, h: vt_ref[c, h * VROWS:(h + 1) * VROWS, :],
                (s_ref, mx_ref, m_ref, acc_ref))
    for h in range(N_HEADS):
        ot_ref[h * HEAD_DIM:(h + 1) * HEAD_DIM, :] = _softmax_out(acc_ref.at[h])
    o_ref[...] = ot_ref[...].T.astype(o_ref.dtype)


def _mla(qm, km, vmt):
    b, t, hw = qm.shape
    kspec, vspec = _kv_specs(t, hw)
    return pl.pallas_call(
        _mla_kernel,
        out_shape=jax.ShapeDtypeStruct((b, t, BRANCH_W), BF16),
        grid=(b, t // TQ),
        in_specs=[pl.BlockSpec((None, TQ, hw), lambda bb, i: (bb, i, 0)), kspec, vspec],
        out_specs=pl.BlockSpec((None, TQ, BRANCH_W), lambda bb, i: (bb, i, 0)),
        scratch_shapes=[pltpu.VMEM((N_HEADS, LANES, TQ), BF16)] + _attn_scratch(N_HEADS),
        compiler_params=_cparams(2),
        name="mla",
    )(qm, km, vmt)


def _mem_kv_kernel(x_ref, w_ref, k_ref, vt_ref):
    x = x_ref[...].astype(BF16)
    k_ref[...] = jnp.dot(x, w_ref[:, :BRANCH_W], preferred_element_type=F32).astype(k_ref.dtype)
    _store_vt(vt_ref, _tn_dot(w_ref[:, BRANCH_W:], x))


def _mem_kv(mem, w, l):
    b, m, d = mem.shape
    assert m % CK == 0
    return pl.pallas_call(
        _mem_kv_kernel,
        out_shape=(jax.ShapeDtypeStruct((b, m, BRANCH_W), BF16),
                   jax.ShapeDtypeStruct((b, m // CK, N_HEADS * VROWS, CK), BF16)),
        grid=(b,),
        in_specs=[pl.BlockSpec((None, m, d), lambda bb: (bb, 0, 0)), _layer_spec(w, l)],
        out_specs=(pl.BlockSpec((None, m, BRANCH_W), lambda bb: (bb, 0, 0)),
                   pl.BlockSpec((None, m // CK, N_HEADS * VROWS, CK), lambda bb: (bb, 0, 0, 0))),
        compiler_params=_cparams(1),
        name="mem_kv",
    )(mem, w)


def _mem_kernel(q_ref, k_ref, vt_ref, o_ref, qt_ref, ot_ref):
    _masked_qt(q_ref[...].astype(F32) * (HEAD_DIM ** -0.5 * LOG2E), 6, N_HEADS, qt_ref)
    s_all = [jnp.dot(_half(k_ref[...], h, 6), qt_ref[h], preferred_element_type=F32) for h in range(N_HEADS)]
    for h in range(N_HEADS):
        s_t = s_all[h]
        p = jnp.exp2(s_t - jnp.max(s_t, axis=0, keepdims=True)).astype(BF16)
        acc = jnp.dot(vt_ref[0, h * VROWS:(h + 1) * VROWS, :], p, preferred_element_type=F32)
        ot_ref[h * HEAD_DIM:(h + 1) * HEAD_DIM, :] = acc[:HEAD_DIM, :] / acc[HEAD_DIM:HEAD_DIM + 1, :]
    o_ref[...] = ot_ref[...].T.astype(o_ref.dtype)


def _mem_attn(eq, mk, mvt):
    b, t, w = eq.shape
    m = mk.shape[1]
    assert m == CK
    return pl.pallas_call(
        _mem_kernel,
        out_shape=jax.ShapeDtypeStruct((b, t, w), BF16),
        grid=(b, t // TQ),
        in_specs=[pl.BlockSpec((None, TQ, w), lambda bb, i: (bb, i, 0)),
                  pl.BlockSpec((None, m, w), lambda bb, i: (bb, 0, 0)),
                  pl.BlockSpec((None,) + mvt.shape[1:], lambda bb, i: (bb, 0, 0, 0))],
        out_specs=pl.BlockSpec((None, TQ, w), lambda bb, i: (bb, i, 0)),
        scratch_shapes=[pltpu.VMEM((N_HEADS, LANES, TQ), BF16), pltpu.VMEM((BRANCH_W, TQ), F32)],
        compiler_params=_cparams(2),
        name="mem_attn",
    )(eq, mk, mvt)


def _final_kernel(h_ref, hb_ref, oa_ref, ob_ref, oc_ref, od_ref, oe_ref, z_ref,
                  wg_ref, wb_ref, wo_ref, g_ref, b_ref, h_out, hb_out, acc_ref, *, alpha):
    d = h_ref.shape[1]
    half = h_ref.shape[0] // 2
    for n, o_ref in enumerate((oa_ref, ob_ref, oc_ref, od_ref, oe_ref)):
        for r in range(2):
            rows = slice(r * half, (r + 1) * half)
            z = z_ref[rows, n * BRANCH_W:(n + 1) * BRANCH_W].astype(F32)
            y = o_ref[rows, :].astype(F32) * (z / (1.0 + jnp.exp(-z)))
            u = jnp.dot(y.astype(BF16), wb_ref[n], preferred_element_type=F32)
            g = jnp.dot(hb_ref[rows, :], wg_ref[:, n * d:(n + 1) * d], preferred_element_type=F32)
            t = u / (1.0 + jnp.exp(-g))
            acc_ref[rows, :] = t if n == 0 else acc_ref[rows, :] + t
    for r in range(2):
        rows = slice(r * half, (r + 1) * half)
        out = jnp.dot(acc_ref[rows, :].astype(BF16), wo_ref[...], preferred_element_type=F32)
        x = alpha * h_ref[rows, :] + out
        mu = jnp.mean(x, axis=1, keepdims=True)
        xc = x - mu
        var = jnp.mean(xc * xc, axis=1, keepdims=True)
        y = xc * lax.rsqrt(var + LN_EPS) * g_ref[...] + b_ref[...]
        h_out[rows, :] = y
        hb_out[rows, :] = y.astype(BF16)


def _final(h, hb, os5, z, wg, wb, wo, ln_g, ln_b, alpha, l):
    n, d = h.shape
    tm = 512
    row = lambda w: pl.BlockSpec((tm, w), lambda i: (i, 0))
    full = lambda a: pl.BlockSpec(a.shape, lambda i: (0,) * a.ndim)
    return pl.pallas_call(
        functools.partial(_final_kernel, alpha=alpha),
        out_shape=(jax.ShapeDtypeStruct((n, d), F32), jax.ShapeDtypeStruct((n, d), BF16)),
        grid=(n // tm,),
        in_specs=[row(d), row(d)] + [row(BRANCH_W)] * N_BRANCH + [row(N_BRANCH * BRANCH_W),
                  _layer_spec(wg, l), _layer_spec(wb, l), _layer_spec(wo, l), full(ln_g), full(ln_b)],
        out_specs=(row(d), row(d)),
        scratch_shapes=[pltpu.VMEM((tm, d), F32)],
        compiler_params=_cparams(1),
        name="merge_out_ln",
    )(h, hb, *os5, z, wg, wb, wo, ln_g, ln_b)


ROPE_GROUPS = (("a_q", N_HEADS, HEAD_DIM, ROT_64), ("a_k", N_HEADS, HEAD_DIM, ROT_64),
               ("i_q", IDX_HEADS, IDX_DIM, ROT_32), ("i_k", 1, MXU_N, ROT_32),
               ("b_q", N_HEADS, HEAD_DIM, ROT_64), ("b_k", N_HEADS, HEAD_DIM, ROT_64),
               ("c_q", 2 * N_HEADS, DIFF_DIM, ROT_32), ("c_k", 2 * N_HEADS, DIFF_DIM, ROT_32),
               ("d_kr", 1, MXU_N, MLA_ROPE))
PLAIN_COLS = ("d_cq", "d_ckv", "e_q") + tuple(("z", j) for j in range(N_BRANCH))
VALUE_COLS = ("a_v", "b_v", "c_v")
GATE_COLS = tuple(("g", j) for j in range(OFF["g"][1] // MXU_N))


def _window_start(col):
    name, j = col if isinstance(col, tuple) else (col, 0)
    return OFF[name][0] + j * MXU_N


def _weight_prep_kernel(offs_ref, wt_ref, o_ref):
    o_ref[...] = wt_ref[...].T.astype(o_ref.dtype)


def _weight_windows(wt, cols, name):
    depth, n, d = wt.shape
    starts = [_window_start(c) for c in cols]
    assert all(st % SUBLANES == 0 and st + MXU_N <= n for st in starts)
    grid_spec = pltpu.PrefetchScalarGridSpec(
        num_scalar_prefetch=1,
        grid=(depth, len(cols)),
        in_specs=[pl.BlockSpec((None, pl.Element(MXU_N), pl.Element(d)),
                               lambda l, j, offs: (l, pl.multiple_of(offs[j], SUBLANES), 0))],
        out_specs=pl.BlockSpec((None, d, MXU_N), lambda l, j, offs: (l, 0, j)),
    )
    return pl.pallas_call(
        _weight_prep_kernel,
        out_shape=jax.ShapeDtypeStruct((depth, d, MXU_N * len(cols)), BF16),
        grid_spec=grid_spec,
        compiler_params=_cparams(2),
        name=name,
    )(jnp.asarray(np.asarray(starts, np.int32)), wt)


def _weight_prep(w_in):
    wt = jnp.swapaxes(w_in, 1, 2)
    return (_weight_windows(wt, PLAIN_COLS, "wprep_plain"), _weight_windows(wt, VALUE_COLS, "wprep_value"),
            _weight_windows(wt, [name for name, *_ in ROPE_GROUPS], "wprep_rope"),
            _weight_windows(wt, GATE_COLS, "wprep_gate"))


def _rope_tables(seq, rot_dim):
    pos = jnp.arange(seq, dtype=F32)
    inv = ROPE_THETA ** (-jnp.arange(0, rot_dim, 2, dtype=F32) / rot_dim)
    ang = pos[:, None] * inv[None, :]
    return jnp.cos(ang), jnp.sin(ang)


def _rope_cs(t, nh, hd, r):
    cos, sin = _rope_tables(t, r)
    c = jnp.concatenate([cos, cos, jnp.ones((t, hd - r), F32)], axis=1)
    s = jnp.concatenate([-sin, sin, jnp.zeros((t, hd - r), F32)], axis=1)
    return jnp.tile(c, (1, nh)), jnp.tile(s, (1, nh))


def kernel(x, mem, ln0_g, ln0_b, w_in, mla_q_norm, w_uq, mla_kv_norm, w_ukv, diff_lam, diff_norm,
           w_mem_kv, w_branch, w_out, ln_g, ln_b):
    b, t, d = x.shape
    depth = w_in.shape[0]
    alpha = (2 * depth) ** 0.25
    assert t % 512 == 0 and d == 1024

    w_plain, w_vt, w_rope, wg = _weight_prep(w_in)
    plain_widths = (BRANCH_W,) * 3 + (N_BRANCH * BRANCH_W,)
    rope_heads = tuple((hd, r // 2) for _, _, hd, r in ROPE_GROUPS)
    patterns = sorted(set((nh, hd, r) for _, nh, hd, r in ROPE_GROUPS))
    rope_tables = tuple(patterns.index((nh, hd, r)) for _, nh, hd, r in ROPE_GROUPS)
    cs = [_rope_cs(t, nh, hd, r) for nh, hd, r in patterns]
    ctab = jnp.stack([c for c, _ in cs])
    stab = jnp.stack([s for _, s in cs])

    uq = w_uq.reshape(depth, Q_LORA, N_HEADS, MLA_NOPE + MLA_ROPE)
    qn_w, qr_w = uq[..., :MLA_NOPE], uq[..., MLA_NOPE:]
    pad32 = jnp.zeros((depth, Q_LORA, N_HEADS, LANES - MLA_NOPE - MLA_ROPE), w_uq.dtype)
    hw = N_HEADS * LANES
    wq = jnp.concatenate([qn_w, qr_w, pad32], axis=-1).reshape(depth, Q_LORA, hw).astype(BF16)
    half = MLA_ROPE // 2
    wq_rot = jnp.concatenate([jnp.zeros_like(qn_w), -qr_w[..., half:], qr_w[..., :half], pad32],
                             axis=-1).reshape(depth, Q_LORA, hw).astype(BF16)
    cos_m, sin_m = _rope_tables(t, MLA_ROPE)
    one = lambda n: jnp.ones((t, n), F32)
    zer = lambda n: jnp.zeros((t, n), F32)
    qs = (MLA_NOPE + MLA_ROPE) ** -0.5 * LOG2E
    ct_q = qs * jnp.tile(jnp.concatenate([one(MLA_NOPE), cos_m, cos_m, one(LANES - MLA_NOPE - MLA_ROPE)], axis=1), (1, N_HEADS))
    st_q = qs * jnp.tile(jnp.concatenate([zer(MLA_NOPE), sin_m, sin_m, zer(LANES - MLA_NOPE - MLA_ROPE)], axis=1), (1, N_HEADS))
    ukv = w_ukv.reshape(depth, KV_LORA, N_HEADS, MLA_NOPE + MLA_V)
    wk = jnp.concatenate([ukv[..., :MLA_NOPE], jnp.zeros((depth, KV_LORA, N_HEADS, LANES - MLA_NOPE), w_ukv.dtype)],
                         axis=-1).reshape(depth, KV_LORA, hw).astype(BF16)
    wvt = ukv[..., MLA_NOPE:].reshape(depth, KV_LORA, N_HEADS * MLA_V).astype(BF16)
    place = np.zeros((MXU_N, hw), np.float32)
    for hh in range(N_HEADS):
        for j in range(MLA_ROPE):
            place[j, hh * LANES + MLA_NOPE + j] = 1.0
    place = jnp.asarray(place, BF16)

    wb = w_branch.astype(BF16)
    wo = w_out.astype(BF16)
    wmem = w_mem_kv.astype(BF16)
    norm_t = jnp.broadcast_to(diff_norm.astype(F32)[:, :, None], (depth, HEAD_DIM, TQ))

    h, hb = _layer_norm0(x.reshape(b * t, d), ln0_g, ln0_b)
    for l in range(depth):
        hb3 = hb.reshape(b, t, d)
        avt, bvt, cvt, dcq, ckv_iw, eq, z = _proj_plain(hb3, w_plain, w_vt, plain_widths, l)
        aq, ak, iq, ik, bq, bk, cq, ck, kr = _proj_rope(hb3, w_rope, ctab, stab, rope_heads, rope_tables, l)

        o_a = _dsa(aq, ak, avt, iq, ik, ik)
        o_b = _moba(bq, bk, bvt, _kbar(bk))
        lam_init = 0.8 - 0.6 * math.exp(-0.3 * l)
        misc = jnp.full((SUBLANES, LANES), lam_init, F32)
        o_c = _diff(cq, ck, cvt, diff_lam[l].astype(F32), norm_t[l], misc)
        qm, km, vmt = _mla_prep(dcq, ckv_iw, kr, mla_q_norm[l].reshape(1, Q_LORA), mla_kv_norm[l].reshape(1, KV_LORA),
                                wq, wq_rot, wk, wvt, place, ct_q, st_q, l)
        o_d = _mla(qm, km, vmt)
        o_e = _mem_attn(eq, *_mem_kv(mem, wmem, l))

        os5 = [o.reshape(b * t, BRANCH_W) for o in (o_a, o_b, o_c, o_d, o_e)]
        h, hb = _final(h, hb, os5, z.reshape(b * t, N_BRANCH * BRANCH_W), wg, wb, wo,
                       ln_g[l].reshape(1, d), ln_b[l].reshape(1, d), alpha, l)
    return h.reshape(b, t, d)
```

```python
import functools
import math

import numpy as np
import jax
import jax.numpy as jnp
from jax import lax
from jax.experimental import pallas as pl
from jax.experimental.pallas import tpu as pltpu

F32 = jnp.float32
BF16 = jnp.bfloat16
I32 = jnp.int32
I16 = jnp.int16

N_HEADS = 4
HEAD_DIM = 64
BRANCH_W = N_HEADS * HEAD_DIM
N_BRANCH = 5
ROPE_THETA = 500000.0
ROT_64 = 16
ROT_32 = 8
IDX_HEADS = 8
IDX_DIM = 32
TOPK_MAX = 256
MOBA_BLOCK = 256
MOBA_TOPK = 3
DIFF_DIM = 32
Q_LORA = 256
KV_LORA = 128
MLA_NOPE = 64
MLA_ROPE = 32
MLA_V = 64
LN_EPS = 1e-5
RMS_EPS = 1e-6

IN_LAYOUT = (
    ("a_q", BRANCH_W), ("a_k", BRANCH_W), ("a_v", BRANCH_W),
    ("i_q", IDX_HEADS * IDX_DIM), ("i_k", IDX_DIM), ("i_w", IDX_HEADS),
    ("b_q", BRANCH_W), ("b_k", BRANCH_W), ("b_v", BRANCH_W),
    ("c_q", BRANCH_W), ("c_k", BRANCH_W), ("c_v", BRANCH_W),
    ("d_cq", Q_LORA), ("d_ckv", KV_LORA), ("d_kr", MLA_ROPE),
    ("e_q", BRANCH_W),
    ("z", N_BRANCH * BRANCH_W),
    ("g", N_BRANCH * 1024),
)

SUBLANES = 8
LANES = 128
MXU_N = 256
TQ = 512
CK = 256
VROWS = HEAD_DIM + 16
FLASH_UNROLL = 4
NEG = -1e30
LOG2E = math.log2(math.e)
INT_MIN = np.int32(-2 ** 31)
HALF16 = 1 << 15
VMEM_LIMIT = 56 * 1024 * 1024


def _offsets():
    off, out = 0, {}
    for name, size in IN_LAYOUT:
        out[name] = (off, size)
        off += size
    return out


OFF = _offsets()


def _nt_dot(a, b):
    return lax.dot_general(a, b, (((1,), (1,)), ((), ())), preferred_element_type=F32)


def _tn_dot(w, x):
    return lax.dot_general(w, x, (((0,), (1,)), ((), ())), preferred_element_type=F32)


def _fold_rows(w, rows=SUBLANES):
    xs = [w[r:r + rows, :] for r in range(0, w.shape[0], rows)]
    while len(xs) > 1:
        xs = [xs[j] + xs[j + 1] for j in range(0, len(xs) - 1, 2)] + ([xs[-1]] if len(xs) % 2 else [])
    return xs[0]


def _masked_qt(q, shift, n, qt_ref):
    qt = q.T
    dim = lax.broadcasted_iota(I32, (LANES, qt.shape[1]), 0)
    for j in range(n):
        half = (j << shift) // LANES
        rows = qt[half * LANES:(half + 1) * LANES, :]
        qt_ref[j] = jnp.where(((dim + half * LANES) >> shift) == j, rows, 0.0).astype(BF16)


def _half(kc, j, shift):
    half = (j << shift) // LANES
    return kc[:, half * LANES:(half + 1) * LANES]


def _cparams(n_axes):
    return pltpu.CompilerParams(dimension_semantics=("arbitrary",) * n_axes,
                                vmem_limit_bytes=VMEM_LIMIT)


def _layer_spec(a, l):
    return pl.BlockSpec((None,) + a.shape[1:], lambda *_: (l,) + (0,) * (a.ndim - 1))


def _softmax_step(s_t, m_tile, vt_h, m_ref, acc_ref):
    m_old = m_ref[...]
    m_new = jnp.maximum(m_old, m_tile)
    alpha = jnp.exp2(m_old - m_new)
    p = jnp.exp2(s_t - m_new)
    acc_ref[...] = alpha * acc_ref[...] + jnp.dot(vt_h, p.astype(BF16), preferred_element_type=F32)
    m_ref[...] = m_new


def _softmax_init(m_ref, acc_ref):
    m_ref[...] = jnp.full(m_ref.shape, NEG, F32)
    acc_ref[...] = jnp.zeros(acc_ref.shape, F32)


def _softmax_out(acc_ref):
    return acc_ref[:HEAD_DIM, :] / acc_ref[HEAD_DIM:HEAD_DIM + 1, :]


def _store_vt(o_ref, vt):
    ones = jnp.ones((VROWS - HEAD_DIM, CK), o_ref.dtype)
    for j in range(o_ref.shape[0]):
        for h in range(N_HEADS):
            o_ref[j, h * VROWS:h * VROWS + HEAD_DIM, :] = (
                vt[h * HEAD_DIM:(h + 1) * HEAD_DIM, j * CK:(j + 1) * CK].astype(o_ref.dtype))
            o_ref[j, h * VROWS + HEAD_DIM:(h + 1) * VROWS, :] = ones


def _flash_loop(n_full, qk_all, mask, vt_rows, state, prep=None, causal_tail=True):
    s_ref, mx_ref, m_ref, acc_ref = state
    n_state = m_ref.shape[0]
    for j in range(n_state):
        _softmax_init(m_ref.at[j], acc_ref.at[j])

    def park(c, slot, d=None):
        ctx = c if prep is None else prep(c)
        for j, s in enumerate(qk_all(c)):
            if mask is not None:
                s = mask(ctx, j, s)
            if d is not None and causal_tail:
                s = jnp.where(_causal(d), s, NEG)
            s_ref[slot, j] = s
            mx_ref[slot, j] = jnp.max(s, axis=0, keepdims=True)

    def consume(c, slot):
        for j in range(n_state):
            _softmax_step(s_ref[slot, j], mx_ref[slot, j], vt_rows(c, j), m_ref.at[j], acc_ref.at[j])

    def pair(c):
        park(c + 1, 1)
        consume(c, 0)
        park(c + 2, 0)
        consume(c + 1, 1)

    def body(g, carry):
        for u in range(0, FLASH_UNROLL, 2):
            pair(FLASH_UNROLL * g + u)
        return carry

    @pl.when(n_full == 0)
    def _():
        park(0, 0, d=0)
        park(1, 1, d=1)
        consume(0, 0)
        consume(1, 1)

    @pl.when(n_full > 0)
    def _():
        park(0, 0)
        n_loop = n_full - 2
        n_group = lax.shift_right_logical(n_loop, FLASH_UNROLL.bit_length() - 1)
        lax.fori_loop(0, n_group, body, 0)
        c0 = FLASH_UNROLL * n_group
        for u in range(FLASH_UNROLL // 2 - 1):
            @pl.when(n_loop - c0 >= 2 * (u + 1))
            def _(u=u):
                pair(c0 + 2 * u)
        c = n_loop
        park(c + 1, 1)
        consume(c, 0)
        park(c + 2, 0, d=0)
        consume(c + 1, 1)
        park(c + 3, 1, d=1)
        consume(c + 2, 0)
        consume(c + 3, 1)


def _causal(d):
    kpos = lax.broadcasted_iota(I32, (CK, TQ), 0) + d * CK
    return kpos <= lax.broadcasted_iota(I32, (CK, TQ), 1)


def _attn_scratch(n_state):
    return [pltpu.VMEM((2, n_state, CK, TQ), F32), pltpu.VMEM((2, n_state, 1, TQ), F32),
            pltpu.VMEM((n_state, 1, TQ), F32), pltpu.VMEM((n_state, VROWS, TQ), F32),
            pltpu.VMEM((BRANCH_W, TQ), F32)]


def _kv_specs(t, w):
    kspec = pl.BlockSpec((None, t, w), lambda bb, i: (bb, 0, 0))
    vspec = pl.BlockSpec((None, t // CK, N_HEADS * VROWS, CK), lambda bb, i: (bb, 0, 0, 0))
    return kspec, vspec


def _ln_kernel(x_ref, g_ref, b_ref, h_ref, hb_ref):
    x = x_ref[...]
    mu = jnp.mean(x, axis=1, keepdims=True)
    xc = x - mu
    var = jnp.mean(xc * xc, axis=1, keepdims=True)
    y = xc * lax.rsqrt(var + LN_EPS) * g_ref[...] + b_ref[...]
    h_ref[...] = y
    hb_ref[...] = y.astype(BF16)


def _layer_norm0(x2, g, b):
    n, d = x2.shape
    tm = 512
    row = pl.BlockSpec((tm, d), lambda i: (i, 0))
    vec = pl.BlockSpec((1, d), lambda i: (0, 0))
    return pl.pallas_call(
        _ln_kernel,
        out_shape=(jax.ShapeDtypeStruct((n, d), F32), jax.ShapeDtypeStruct((n, d), BF16)),
        grid=(n // tm,),
        in_specs=[row, vec, vec],
        out_specs=(row, row),
        compiler_params=_cparams(1),
        name="ln0",
    )(x2, g.reshape(1, d), b.reshape(1, d))


def _proj_plain_kernel(x_ref, w_ref, wt_ref, *out_refs, n_t):
    for g, o_ref in enumerate(out_refs[:n_t]):
        _store_vt(o_ref, _tn_dot(wt_ref[:, g * BRANCH_W:(g + 1) * BRANCH_W], x_ref[...]))
    off = 0
    for o_ref in out_refs[n_t:]:
        wd = o_ref.shape[-1]
        for j in range(0, wd, MXU_N):
            acc = jnp.dot(x_ref[...], w_ref[:, off + j:off + j + MXU_N], preferred_element_type=F32)
            o_ref[:, j:j + MXU_N] = acc.astype(o_ref.dtype)
        off += wd


def _proj_plain(hb3, w, wt, widths, l):
    b, t, d = hb3.shape
    tm = 512
    n_t = wt.shape[-1] // BRANCH_W
    shapes = [jax.ShapeDtypeStruct((b, t // CK, N_HEADS * VROWS, CK), BF16)] * n_t
    specs = [pl.BlockSpec((None, tm // CK, N_HEADS * VROWS, CK), lambda i, bb: (bb, i, 0, 0))] * n_t
    shapes += [jax.ShapeDtypeStruct((b, t, wd), BF16) for wd in widths]
    specs += [pl.BlockSpec((None, tm, wd), lambda i, bb: (bb, i, 0)) for wd in widths]
    return pl.pallas_call(
        functools.partial(_proj_plain_kernel, n_t=n_t),
        out_shape=tuple(shapes),
        grid=(t // tm, b),
        in_specs=[pl.BlockSpec((None, tm, d), lambda i, bb: (bb, i, 0)),
                  _layer_spec(w, l), _layer_spec(wt, l)],
        out_specs=tuple(specs),
        compiler_params=_cparams(2),
        name="proj_plain",
    )(hb3, w, wt)


def _proj_rope_kernel(x_ref, w_ref, c_ref, s_ref, *out_refs, heads, tables):
    lane = lax.broadcasted_iota(I32, (x_ref.shape[0], MXU_N), 1)
    for g, o_ref in enumerate(out_refs):
        hd, half = heads[g]
        sl = slice(g * MXU_N, (g + 1) * MXU_N)
        acc = jnp.dot(x_ref[...], w_ref[:, sl], preferred_element_type=F32)
        partner = jnp.where((lane & (hd - 1)) < half,
                            pltpu.roll(acc, MXU_N - half, 1), pltpu.roll(acc, half, 1))
        o_ref[...] = (acc * c_ref[tables[g]] + partner * s_ref[tables[g]]).astype(o_ref.dtype)


def _proj_rope(hb3, w, ctab, stab, heads, tables, l):
    b, t, d = hb3.shape
    tm = 512
    assert w.shape[-1] == MXU_N * len(heads)
    tspec = pl.BlockSpec((ctab.shape[0], tm, MXU_N), lambda i, bb: (0, i, 0))
    ospec = pl.BlockSpec((None, tm, MXU_N), lambda i, bb: (bb, i, 0))
    return pl.pallas_call(
        functools.partial(_proj_rope_kernel, heads=heads, tables=tables),
        out_shape=(jax.ShapeDtypeStruct((b, t, MXU_N), BF16),) * len(heads),
        grid=(t // tm, b),
        in_specs=[pl.BlockSpec((None, tm, d), lambda i, bb: (bb, i, 0)),
                  _layer_spec(w, l), tspec, tspec],
        out_specs=(ospec,) * len(heads),
        compiler_params=_cparams(2),
        name="proj_rope",
    )(hb3, w, ctab, stab)


def _dsa_kernel(aq_ref, ak_ref, avt_ref, iq_ref, ik_ref, iw_ref, pick_ref, tri_ref, o_ref,
                keys_ref, hi_ref, lo_ref, bk_ref, iqt_ref, aqt_ref, wt_ref, thr_ref, s_ref, mx_ref, m_ref, acc_ref, ot_ref,
                *, topk, idx_scale):
    i = pl.program_id(1)
    n_full = 2 * i
    n_pair = i + 1

    iqt = iq_ref[...].astype(F32).T
    for hh in range(IDX_HEADS):
        iqt_ref[hh] = iqt[hh * IDX_DIM:(hh + 1) * IDX_DIM, :].astype(BF16)
    _masked_qt(aq_ref[...].astype(F32) * (HEAD_DIM ** -0.5 * LOG2E), 6, N_HEADS, aqt_ref)
    wt_ref[...] = _nt_dot(pick_ref[...], iw_ref[...]) * idx_scale

    def logits(c):
        kc = ik_ref[pl.ds(pl.multiple_of(c * CK, CK), CK), :]
        return [jnp.dot(kc[:, :IDX_DIM], iqt_ref[hh], preferred_element_type=F32) for hh in range(IDX_HEADS)]

    def put_keys(c, key):
        keys_ref[c] = key
        hi_ref[c] = (key >> 16).astype(I16)
        lo_ref[c] = ((key & 0xFFFF) - HALF16).astype(I16)

    def score_chunk(c, lg, d):
        sc = jnp.zeros((CK, TQ), F32)
        for hh in range(IDX_HEADS):
            sc = sc + jnp.maximum(lg[hh], 0.0) * wt_ref[hh:hh + 1, :]
        bits = pltpu.bitcast(sc, I32)
        key = jnp.where(bits < 0, INT_MIN - bits, bits)
        put_keys(c, key if d is None else jnp.where(_causal(d), key, INT_MIN))

    def score_pair(c, d0, d1):
        lg0, lg1 = logits(c), logits(c + 1)
        score_chunk(c, lg0, d0)
        score_chunk(c + 1, lg1, d1)

    def score_body(p, carry):
        score_pair(2 * p, None, None)
        return carry

    lax.fori_loop(0, i, score_body, 0)
    score_pair(n_full, 0, 1)

    def pair_loop(body, init):
        def pair(p, carry):
            return body(2 * p + 1, body(2 * p, carry))
        return lax.fori_loop(0, n_pair, pair, init)

    def count16(pred, also=None):
        def body(c, part):
            hit = jnp.where(pred(c), jnp.int16(1), jnp.int16(0))
            if also is not None:
                hit = jnp.where(also(c), hit, jnp.int16(0))
            return part + _fold_rows(hit, 2 * SUBLANES)
        part = pair_loop(body, jnp.zeros((2 * SUBLANES, TQ), I16))
        return jnp.sum(part.astype(F32), axis=0, keepdims=True)

    def search16(ref, need):
        def bit_body(bi, t_u):
            c_u = t_u | jnp.left_shift(jnp.int32(1), 15 - bi)
            ck = (c_u - HALF16).astype(I16)
            cnt = count16(lambda c: ref[c] >= ck)
            return jnp.where(cnt >= need, c_u, t_u)
        return lax.fori_loop(0, 16, bit_body, jnp.zeros((1, TQ), I32))

    hi_u = search16(hi_ref, float(topk))
    thr_hi = (hi_u - HALF16).astype(I16)
    n_above = count16(lambda c: hi_ref[c] > thr_hi)

    def bucket_body(c, carry):
        bk_ref[c] = jnp.where(hi_ref[c] == thr_hi, lo_ref[c], jnp.int16(-HALF16))
        return carry

    pair_loop(bucket_body, 0)
    lo_u = search16(bk_ref, float(topk) - n_above)
    thr_lo = (lo_u - HALF16).astype(I16)
    thr = ((hi_u - HALF16) << 16) | lo_u

    n_gt = n_above + count16(lambda c: bk_ref[c] > thr_lo)
    n_eq = count16(lambda c: lo_ref[c] == thr_lo, also=lambda c: hi_ref[c] == thr_hi)
    need = float(topk) - n_gt
    amb = jnp.logical_and(n_eq > need, thr > INT_MIN)
    any_amb = jnp.max(jnp.where(amb, 1.0, 0.0)) > 0.5

    @pl.when(any_amb)
    def _():
        def drop_body(c, seen):
            k = keys_ref[c]
            eq = k == thr
            eqf = jnp.where(eq, 1.0, 0.0)
            rank = jnp.dot(tri_ref[...], eqf.astype(BF16), preferred_element_type=F32) + seen
            drop = jnp.logical_and(jnp.logical_and(eq, rank > need), amb)
            keys_ref[c] = jnp.where(drop, INT_MIN, k)
            return seen + jnp.sum(eqf, axis=0, keepdims=True)

        pair_loop(drop_body, jnp.zeros((1, TQ), F32))

    thr_ref[...] = jnp.maximum(thr, INT_MIN + 1)

    def qk_all(c):
        kc = ak_ref[pl.ds(pl.multiple_of(c * CK, CK), CK), :]
        return [jnp.dot(_half(kc, h, 6), aqt_ref[h], preferred_element_type=F32) for h in range(N_HEADS)]

    _flash_loop(n_full, qk_all,
                lambda keep, h, s: jnp.where(keep, s, NEG),
                lambda c, h: avt_ref[c, h * VROWS:(h + 1) * VROWS, :],
                (s_ref, mx_ref, m_ref, acc_ref),
                prep=lambda c: keys_ref[c] >= thr_ref[...], causal_tail=False)
    for h in range(N_HEADS):
        ot_ref[h * HEAD_DIM:(h + 1) * HEAD_DIM, :] = _softmax_out(acc_ref.at[h])
    o_ref[...] = ot_ref[...].T.astype(o_ref.dtype)


def _dsa(aq, ak, avt, iq, ik, iw):
    b, t, _ = aq.shape
    topk = min(TOPK_MAX, t // 4)
    qspec = pl.BlockSpec((None, TQ, BRANCH_W), lambda bb, i: (bb, i, 0))
    kspec, vspec = _kv_specs(t, BRANCH_W)
    pick = np.zeros((2 * SUBLANES, MXU_N), np.float32)
    for hh in range(IDX_HEADS):
        pick[hh, IDX_DIM + hh] = 1.0
    pick = jnp.asarray(pick, BF16)
    tri = jnp.asarray(np.tril(np.ones((CK, CK), np.float32)), BF16)
    kern = functools.partial(_dsa_kernel, topk=topk, idx_scale=(IDX_HEADS * IDX_DIM) ** -0.5)
    return pl.pallas_call(
        kern,
        out_shape=jax.ShapeDtypeStruct((b, t, BRANCH_W), BF16),
        grid=(b, t // TQ),
        in_specs=[qspec, kspec, vspec, qspec, kspec, qspec,
                  pl.BlockSpec(pick.shape, lambda bb, i: (0, 0)), pl.BlockSpec(tri.shape, lambda bb, i: (0, 0))],
        out_specs=qspec,
        scratch_shapes=[
            pltpu.VMEM((t // CK, CK, TQ), I32),
            pltpu.VMEM((t // CK, CK, TQ), I16),
            pltpu.VMEM((t // CK, CK, TQ), I16),
            pltpu.VMEM((t // CK, CK, TQ), I16),
            pltpu.VMEM((IDX_HEADS, IDX_DIM, TQ), BF16),
            pltpu.VMEM((N_HEADS, LANES, TQ), BF16),
            pltpu.VMEM((2 * SUBLANES, TQ), F32),
            pltpu.VMEM((1, TQ), I32),
        ] + _attn_scratch(N_HEADS),
        compiler_params=_cparams(2),
        name="dsa",
    )(aq, ak, avt, iq, ik, iw, pick, tri)


def _kbar_kernel(k_ref, o_ref):
    o_ref[...] = jnp.zeros(o_ref.shape, o_ref.dtype)
    nb = k_ref.shape[0] // MOBA_BLOCK
    for n in range(nb):
        blk = k_ref[n * MOBA_BLOCK:(n + 1) * MOBA_BLOCK, :].astype(F32)
        o_ref[n:n + 1, :] = jnp.mean(blk, axis=0, keepdims=True).astype(o_ref.dtype)


def _kbar(bk):
    b, t, w = bk.shape
    nbp = max(2 * SUBLANES, t // MOBA_BLOCK)
    return pl.pallas_call(
        _kbar_kernel,
        out_shape=jax.ShapeDtypeStruct((b, nbp, w), BF16),
        grid=(b,),
        in_specs=[pl.BlockSpec((None, t, w), lambda bb: (bb, 0, 0))],
        out_specs=pl.BlockSpec((None, nbp, w), lambda bb: (bb, 0, 0)),
        compiler_params=_cparams(1),
        name="moba_kbar",
    )(bk)


def _moba_kernel(q_ref, k_ref, vt_ref, kbar_ref, o_ref, qt_ref, bias_ref, s_ref, mx_ref, m_ref, acc_ref, ot_ref):
    i = pl.program_id(1)
    nbp = kbar_ref.shape[0]
    blk = lax.broadcasted_iota(I32, (nbp, TQ), 0)
    blk_f = blk.astype(F32)
    own = 2 * i + (lax.broadcasted_iota(I32, (nbp, TQ), 1) >> (MOBA_BLOCK.bit_length() - 1))
    _masked_qt(q_ref[...].astype(F32) * (HEAD_DIM ** -0.5 * LOG2E), 6, N_HEADS, qt_ref)

    for h in range(N_HEADS):
        g = jnp.where(blk < own, jnp.dot(_half(kbar_ref[...], h, 6), qt_ref[h], preferred_element_type=F32), NEG)
        bias = jnp.full((nbp, TQ), NEG, F32)
        for _ in range(MOBA_TOPK):
            mx = jnp.max(g, axis=0, keepdims=True)
            first = jnp.min(jnp.where(g == mx, blk_f, 1e9), axis=0, keepdims=True)
            pick = jnp.logical_and(blk_f == first, mx > 0.5 * NEG)
            bias = jnp.where(pick, 0.0, bias)
            g = jnp.where(pick, NEG, g)
        bias_ref[h] = jnp.where(blk == own, 0.0, bias)

    def qk_all(c):
        kc = k_ref[pl.ds(pl.multiple_of(c * CK, CK), CK), :]
        return [jnp.dot(_half(kc, h, 6), qt_ref[h], preferred_element_type=F32) for h in range(N_HEADS)]

    _flash_loop(2 * i, qk_all, lambda c, h, s: s + bias_ref[h, pl.ds(c, 1), :],
                lambda c, h: vt_ref[c, h * VROWS:(h + 1) * VROWS, :], (s_ref, mx_ref, m_ref, acc_ref))
    for h in range(N_HEADS):
        ot_ref[h * HEAD_DIM:(h + 1) * HEAD_DIM, :] = _softmax_out(acc_ref.at[h])
    o_ref[...] = ot_ref[...].T.astype(o_ref.dtype)


def _moba(bq, bk, bvt, kbar):
    b, t, w = bq.shape
    assert TQ == 2 * MOBA_BLOCK and CK == MOBA_BLOCK and t % TQ == 0
    nbp = kbar.shape[1]
    qspec = pl.BlockSpec((None, TQ, w), lambda bb, i: (bb, i, 0))
    kspec, vspec = _kv_specs(t, w)
    return pl.pallas_call(
        _moba_kernel,
        out_shape=jax.ShapeDtypeStruct((b, t, w), BF16),
        grid=(b, t // TQ),
        in_specs=[qspec, kspec, vspec, pl.BlockSpec((None, nbp, w), lambda bb, i: (bb, 0, 0))],
        out_specs=qspec,
        scratch_shapes=[pltpu.VMEM((N_HEADS, LANES, TQ), BF16), pltpu.VMEM((N_HEADS, nbp, TQ), F32)]
        + _attn_scratch(N_HEADS),
        compiler_params=_cparams(2),
        name="moba",
    )(bq, bk, bvt, kbar)


def _diff_kernel(q_ref, k_ref, vt_ref, lam_ref, norm_ref, misc_ref, o_ref,
                 qt_ref, s_ref, mx_ref, m_ref, acc_ref, ot_ref):
    i = pl.program_id(1)
    _masked_qt(q_ref[...].astype(F32) * (DIFF_DIM ** -0.5 * LOG2E), 5, 2 * N_HEADS, qt_ref)

    dl = lam_ref[...]
    lam_init = misc_ref[0:1, 0:1]
    lam = (jnp.exp(jnp.sum(dl[0:1, :] * dl[1:2, :], axis=1, keepdims=True))
           - jnp.exp(jnp.sum(dl[2:3, :] * dl[3:4, :], axis=1, keepdims=True)) + lam_init)

    def qk_all(c):
        kc = k_ref[pl.ds(pl.multiple_of(c * CK, CK), CK), :]
        return [jnp.dot(_half(kc, j, 5), qt_ref[j], preferred_element_type=F32) for j in range(2 * N_HEADS)]

    _flash_loop(2 * i, qk_all, None,
                lambda c, j: vt_ref[c, (j // 2) * VROWS:(j // 2 + 1) * VROWS, :],
                (s_ref, mx_ref, m_ref, acc_ref))

    post = norm_ref[...] * (1.0 - lam_init)
    for h in range(N_HEADS):
        o_h = _softmax_out(acc_ref.at[2 * h]) - lam * _softmax_out(acc_ref.at[2 * h + 1])
        ms = jnp.mean(o_h * o_h, axis=0, keepdims=True)
        ot_ref[h * HEAD_DIM:(h + 1) * HEAD_DIM, :] = o_h * lax.rsqrt(ms + RMS_EPS) * post
    o_ref[...] = ot_ref[...].T.astype(o_ref.dtype)


def _diff(cq, ck, cvt, lam, norm, misc):
    b, t, w = cq.shape
    qspec = pl.BlockSpec((None, TQ, w), lambda bb, i: (bb, i, 0))
    kspec, vspec = _kv_specs(t, w)
    full = lambda a: pl.BlockSpec(a.shape, lambda bb, i: (0,) * a.ndim)
    return pl.pallas_call(
        _diff_kernel,
        out_shape=jax.ShapeDtypeStruct((b, t, w), BF16),
        grid=(b, t // TQ),
        in_specs=[qspec, kspec, vspec, full(lam), full(norm), full(misc)],
        out_specs=qspec,
        scratch_shapes=[pltpu.VMEM((2 * N_HEADS, LANES, TQ), BF16)] + _attn_scratch(2 * N_HEADS),
        compiler_params=_cparams(2),
        name="diff",
    )(cq, ck, cvt, lam, norm, misc)


def _mla_prep_kernel(cq_ref, ckv_ref, kr_ref, qn_ref, kvn_ref, wq_ref, wqr_ref, wk_ref, wvt_ref,
                     p_ref, ct_ref, st_ref, q_out, k_out, vt_out):
    x = cq_ref[...].astype(F32)
    xn = (x * lax.rsqrt(jnp.mean(x * x, axis=1, keepdims=True) + RMS_EPS) * qn_ref[...]).astype(BF16)
    q = (jnp.dot(xn, wq_ref[...], preferred_element_type=F32) * ct_ref[...]
         + jnp.dot(xn, wqr_ref[...], preferred_element_type=F32) * st_ref[...])
    q_out[...] = q.astype(q_out.dtype)
    c = ckv_ref[:, :KV_LORA].astype(F32)
    cn = (c * lax.rsqrt(jnp.mean(c * c, axis=1, keepdims=True) + RMS_EPS) * kvn_ref[...]).astype(BF16)
    k = (jnp.dot(cn, wk_ref[...], preferred_element_type=F32)
         + jnp.dot(kr_ref[...], p_ref[...], preferred_element_type=F32))
    k_out[...] = k.astype(k_out.dtype)
    _store_vt(vt_out, _tn_dot(wvt_ref[...], cn))


def _mla_prep(dcq, ckv, kr, qn, kvn, wq, wqr, wk, wvt, pmat, ct, st, l):
    b, t, _ = dcq.shape
    tm = 512
    hw = N_HEADS * LANES
    row = lambda w: pl.BlockSpec((None, tm, w), lambda i, bb: (bb, i, 0))
    full = lambda a: pl.BlockSpec(a.shape, lambda i, bb: (0,) * a.ndim)
    tab = pl.BlockSpec((tm, hw), lambda i, bb: (i, 0))
    return pl.pallas_call(
        _mla_prep_kernel,
        out_shape=(jax.ShapeDtypeStruct((b, t, hw), BF16), jax.ShapeDtypeStruct((b, t, hw), BF16),
                   jax.ShapeDtypeStruct((b, t // CK, N_HEADS * VROWS, CK), BF16)),
        grid=(t // tm, b),
        in_specs=[row(Q_LORA), row(MXU_N), row(MXU_N), full(qn), full(kvn), _layer_spec(wq, l), _layer_spec(wqr, l),
                  _layer_spec(wk, l), _layer_spec(wvt, l), full(pmat), tab, tab],
        out_specs=(row(hw), row(hw),
                   pl.BlockSpec((None, tm // CK, N_HEADS * VROWS, CK), lambda i, bb: (bb, i, 0, 0))),
        compiler_params=_cparams(2),
        name="mla_prep",
    )(dcq, ckv, kr, qn, kvn, wq, wqr, wk, wvt, pmat, ct, st)


def _mla_kernel(q_ref, k_ref, vt_ref, o_ref, qt_ref, s_ref, mx_ref, m_ref, acc_ref, ot_ref):
    i = pl.program_id(1)
    hs = [slice(h * LANES, (h + 1) * LANES) for h in range(N_HEADS)]
    for h in range(N_HEADS):
        qt_ref[h] = q_ref[:, hs[h]].astype(F32).T.astype(BF16)

    def qk_all(c):
        start = pl.multiple_of(c * CK, CK)
        return [jnp.dot(k_ref[pl.ds(start, CK), hs[h]], qt_ref[h], preferred_element_type=F32)
                for h in range(N_HEADS)]

    _flash_loop(2 * i, qk_all, None,
                lambda c, h: vt_ref[c, h * VROWS:(h + 1) * VROWS, :],
                (s_ref, mx_ref, m_ref, acc_ref))
    for h in range(N_HEADS):
        ot_ref[h * HEAD_DIM:(h + 1) * HEAD_DIM, :] = _softmax_out(acc_ref.at[h])
    o_ref[...] = ot_ref[...].T.astype(o_ref.dtype)


def _mla(qm, km, vmt):
    b, t, hw = qm.shape
    kspec, vspec = _kv_specs(t, hw)
    return pl.pallas_call(
        _mla_kernel,
        out_shape=jax.ShapeDtypeStruct((b, t, BRANCH_W), BF16),
        grid=(b, t // TQ),
        in_specs=[pl.BlockSpec((None, TQ, hw), lambda bb, i: (bb, i, 0)), kspec, vspec],
        out_specs=pl.BlockSpec((None, TQ, BRANCH_W), lambda bb, i: (bb, i, 0)),
        scratch_shapes=[pltpu.VMEM((N_HEADS, LANES, TQ), BF16)] + _attn_scratch(N_HEADS),
        compiler_params=_cparams(2),
        name="mla",
    )(qm, km, vmt)


def _mem_kv_kernel(x_ref, w_ref, k_ref, vt_ref):
    x = x_ref[...].astype(BF16)
    k_ref[...] = jnp.dot(x, w_ref[:, :BRANCH_W], preferred_element_type=F32).astype(k_ref.dtype)
    _store_vt(vt_ref, _tn_dot(w_ref[:, BRANCH_W:], x))


def _mem_kv(mem, w, l):
    b, m, d = mem.shape
    assert m % CK == 0
    return pl.pallas_call(
        _mem_kv_kernel,
        out_shape=(jax.ShapeDtypeStruct((b, m, BRANCH_W), BF16),
                   jax.ShapeDtypeStruct((b, m // CK, N_HEADS * VROWS, CK), BF16)),
        grid=(b,),
        in_specs=[pl.BlockSpec((None, m, d), lambda bb: (bb, 0, 0)), _layer_spec(w, l)],
        out_specs=(pl.BlockSpec((None, m, BRANCH_W), lambda bb: (bb, 0, 0)),
                   pl.BlockSpec((None, m // CK, N_HEADS * VROWS, CK), lambda bb: (bb, 0, 0, 0))),
        compiler_params=_cparams(1),
        name="mem_kv",
    )(mem, w)


def _mem_kernel(q_ref, k_ref, vt_ref, o_ref, qt_ref, ot_ref):
    _masked_qt(q_ref[...].astype(F32) * (HEAD_DIM ** -0.5 * LOG2E), 6, N_HEADS, qt_ref)
    s_all = [jnp.dot(_half(k_ref[...], h, 6), qt_ref[h], preferred_element_type=F32) for h in range(N_HEADS)]
    for h in range(N_HEADS):
        s_t = s_all[h]
        p = jnp.exp2(s_t - jnp.max(s_t, axis=0, keepdims=True)).astype(BF16)
        acc = jnp.dot(vt_ref[0, h * VROWS:(h + 1) * VROWS, :], p, preferred_element_type=F32)
        ot_ref[h * HEAD_DIM:(h + 1) * HEAD_DIM, :] = acc[:HEAD_DIM, :] / acc[HEAD_DIM:HEAD_DIM + 1, :]
    o_ref[...] = ot_ref[...].T.astype(o_ref.dtype)


def _mem_attn(eq, mk, mvt):
    b, t, w = eq.shape
    m = mk.shape[1]
    assert m == CK
    return pl.pallas_call(
        _mem_kernel,
        out_shape=jax.ShapeDtypeStruct((b, t, w), BF16),
        grid=(b, t // TQ),
        in_specs=[pl.BlockSpec((None, TQ, w), lambda bb, i: (bb, i, 0)),
                  pl.BlockSpec((None, m, w), lambda bb, i: (bb, 0, 0)),
                  pl.BlockSpec((None,) + mvt.shape[1:], lambda bb, i: (bb, 0, 0, 0))],
        out_specs=pl.BlockSpec((None, TQ, w), lambda bb, i: (bb, i, 0)),
        scratch_shapes=[pltpu.VMEM((N_HEADS, LANES, TQ), BF16), pltpu.VMEM((BRANCH_W, TQ), F32)],
        compiler_params=_cparams(2),
        name="mem_attn",
    )(eq, mk, mvt)


def _final_kernel(h_ref, hb_ref, oa_ref, ob_ref, oc_ref, od_ref, oe_ref, z_ref,
                  wg_ref, wb_ref, wo_ref, g_ref, b_ref, h_out, hb_out, acc_ref, *, alpha):
    d = h_ref.shape[1]
    half = h_ref.shape[0] // 2
    for n, o_ref in enumerate((oa_ref, ob_ref, oc_ref, od_ref, oe_ref)):
        for r in range(2):
            rows = slice(r * half, (r + 1) * half)
            z = z_ref[rows, n * BRANCH_W:(n + 1) * BRANCH_W].astype(F32)
            y = o_ref[rows, :].astype(F32) * (z / (1.0 + jnp.exp(-z)))
            u = jnp.dot(y.astype(BF16), wb_ref[n], preferred_element_type=F32)
            g = jnp.dot(hb_ref[rows, :], wg_ref[:, n * d:(n + 1) * d], preferred_element_type=F32)
            t = u / (1.0 + jnp.exp(-g))
            acc_ref[rows, :] = t if n == 0 else acc_ref[rows, :] + t
    for r in range(2):
        rows = slice(r * half, (r + 1) * half)
        out = jnp.dot(acc_ref[rows, :].astype(BF16), wo_ref[...], preferred_element_type=F32)
        x = alpha * h_ref[rows, :] + out
        mu = jnp.mean(x, axis=1, keepdims=True)
        xc = x - mu
        var = jnp.mean(xc * xc, axis=1, keepdims=True)
        y = xc * lax.rsqrt(var + LN_EPS) * g_ref[...] + b_ref[...]
        h_out[rows, :] = y
        hb_out[rows, :] = y.astype(BF16)


def _final(h, hb, os5, z, wg, wb, wo, ln_g, ln_b, alpha, l):
    n, d = h.shape
    tm = 512
    row = lambda w: pl.BlockSpec((tm, w), lambda i: (i, 0))
    full = lambda a: pl.BlockSpec(a.shape, lambda i: (0,) * a.ndim)
    return pl.pallas_call(
        functools.partial(_final_kernel, alpha=alpha),
        out_shape=(jax.ShapeDtypeStruct((n, d), F32), jax.ShapeDtypeStruct((n, d), BF16)),
        grid=(n // tm,),
        in_specs=[row(d), row(d)] + [row(BRANCH_W)] * N_BRANCH + [row(N_BRANCH * BRANCH_W),
                  _layer_spec(wg, l), _layer_spec(wb, l), _layer_spec(wo, l), full(ln_g), full(ln_b)],
        out_specs=(row(d), row(d)),
        scratch_shapes=[pltpu.VMEM((tm, d), F32)],
        compiler_params=_cparams(1),
        name="merge_out_ln",
    )(h, hb, *os5, z, wg, wb, wo, ln_g, ln_b)


ROPE_GROUPS = (("a_q", N_HEADS, HEAD_DIM, ROT_64), ("a_k", N_HEADS, HEAD_DIM, ROT_64),
               ("i_q", IDX_HEADS, IDX_DIM, ROT_32), ("i_k", 1, MXU_N, ROT_32),
               ("b_q", N_HEADS, HEAD_DIM, ROT_64), ("b_k", N_HEADS, HEAD_DIM, ROT_64),
               ("c_q", 2 * N_HEADS, DIFF_DIM, ROT_32), ("c_k", 2 * N_HEADS, DIFF_DIM, ROT_32),
               ("d_kr", 1, MXU_N, MLA_ROPE))
PLAIN_COLS = ("d_cq", "d_ckv", "e_q") + tuple(("z", j) for j in range(N_BRANCH))
VALUE_COLS = ("a_v", "b_v", "c_v")
GATE_COLS = tuple(("g", j) for j in range(OFF["g"][1] // MXU_N))


def _window_start(col):
    name, j = col if isinstance(col, tuple) else (col, 0)
    return OFF[name][0] + j * MXU_N


def _weight_prep_kernel(offs_ref, wt_ref, o_ref):
    o_ref[...] = wt_ref[...].T.astype(o_ref.dtype)


def _weight_windows(wt, cols, name):
    depth, n, d = wt.shape
    starts = [_window_start(c) for c in cols]
    assert all(st % SUBLANES == 0 and st + MXU_N <= n for st in starts)
    grid_spec = pltpu.PrefetchScalarGridSpec(
        num_scalar_prefetch=1,
        grid=(depth, len(cols)),
        in_specs=[pl.BlockSpec((None, pl.Element(MXU_N), pl.Element(d)),
                               lambda l, j, offs: (l, pl.multiple_of(offs[j], SUBLANES), 0))],
        out_specs=pl.BlockSpec((None, d, MXU_N), lambda l, j, offs: (l, 0, j)),
    )
    return pl.pallas_call(
        _weight_prep_kernel,
        out_shape=jax.ShapeDtypeStruct((depth, d, MXU_N * len(cols)), BF16),
        grid_spec=grid_spec,
        compiler_params=_cparams(2),
        name=name,
    )(jnp.asarray(np.asarray(starts, np.int32)), wt)


def _weight_prep(w_in):
    wt = jnp.swapaxes(w_in, 1, 2)
    return (_weight_windows(wt, PLAIN_COLS, "wprep_plain"), _weight_windows(wt, VALUE_COLS, "wprep_value"),
            _weight_windows(wt, [name for name, *_ in ROPE_GROUPS], "wprep_rope"),
            _weight_windows(wt, GATE_COLS, "wprep_gate"))


def _rope_tables(seq, rot_dim):
    pos = jnp.arange(seq, dtype=F32)
    inv = ROPE_THETA ** (-jnp.arange(0, rot_dim, 2, dtype=F32) / rot_dim)
    ang = pos[:, None] * inv[None, :]
    return jnp.cos(ang), jnp.sin(ang)


def _rope_cs(t, nh, hd, r):
    cos, sin = _rope_tables(t, r)
    c = jnp.concatenate([cos, cos, jnp.ones((t, hd - r), F32)], axis=1)
    s = jnp.concatenate([-sin, sin, jnp.zeros((t, hd - r), F32)], axis=1)
    return jnp.tile(c, (1, nh)), jnp.tile(s, (1, nh))


def kernel(x, mem, ln0_g, ln0_b, w_in, mla_q_norm, w_uq, mla_kv_norm, w_ukv, diff_lam, diff_norm,
           w_mem_kv, w_branch, w_out, ln_g, ln_b):
    b, t, d = x.shape
    depth = w_in.shape[0]
    alpha = (2 * depth) ** 0.25
    assert t % 512 == 0 and d == 1024

    w_plain, w_vt, w_rope, wg = _weight_prep(w_in)
    plain_widths = (BRANCH_W,) * 3 + (N_BRANCH * BRANCH_W,)
    rope_heads = tuple((hd, r // 2) for _, _, hd, r in ROPE_GROUPS)
    patterns = sorted(set((nh, hd, r) for _, nh, hd, r in ROPE_GROUPS))
    rope_tables = tuple(patterns.index((nh, hd, r)) for _, nh, hd, r in ROPE_GROUPS)
    cs = [_rope_cs(t, nh, hd, r) for nh, hd, r in patterns]
    ctab = jnp.stack([c for c, _ in cs])
    stab = jnp.stack([s for _, s in cs])

    uq = w_uq.reshape(depth, Q_LORA, N_HEADS, MLA_NOPE + MLA_ROPE)
    qn_w, qr_w = uq[..., :MLA_NOPE], uq[..., MLA_NOPE:]
    pad32 = jnp.zeros((depth, Q_LORA, N_HEADS, LANES - MLA_NOPE - MLA_ROPE), w_uq.dtype)
    hw = N_HEADS * LANES
    wq = jnp.concatenate([qn_w, qr_w, pad32], axis=-1).reshape(depth, Q_LORA, hw).astype(BF16)
    half = MLA_ROPE // 2
    wq_rot = jnp.concatenate([jnp.zeros_like(qn_w), -qr_w[..., half:], qr_w[..., :half], pad32],
                             axis=-1).reshape(depth, Q_LORA, hw).astype(BF16)
    cos_m, sin_m = _rope_tables(t, MLA_ROPE)
    one = lambda n: jnp.ones((t, n), F32)
    zer = lambda n: jnp.zeros((t, n), F32)
    qs = (MLA_NOPE + MLA_ROPE) ** -0.5 * LOG2E
    ct_q = qs * jnp.tile(jnp.concatenate([one(MLA_NOPE), cos_m, cos_m, one(LANES - MLA_NOPE - MLA_ROPE)], axis=1), (1, N_HEADS))
    st_q = qs * jnp.tile(jnp.concatenate([zer(MLA_NOPE), sin_m, sin_m, zer(LANES - MLA_NOPE - MLA_ROPE)], axis=1), (1, N_HEADS))
    ukv = w_ukv.reshape(depth, KV_LORA, N_HEADS, MLA_NOPE + MLA_V)
    wk = jnp.concatenate([ukv[..., :MLA_NOPE], jnp.zeros((depth, KV_LORA, N_HEADS, LANES - MLA_NOPE), w_ukv.dtype)],
                         axis=-1).reshape(depth, KV_LORA, hw).astype(BF16)
    wvt = ukv[..., MLA_NOPE:].reshape(depth, KV_LORA, N_HEADS * MLA_V).astype(BF16)
    place = np.zeros((MXU_N, hw), np.float32)
    for hh in range(N_HEADS):
        for j in range(MLA_ROPE):
            place[j, hh * LANES + MLA_NOPE + j] = 1.0
    place = jnp.asarray(place, BF16)

    wb = w_branch.astype(BF16)
    wo = w_out.astype(BF16)
    wmem = w_mem_kv.astype(BF16)
    norm_t = jnp.broadcast_to(diff_norm.astype(F32)[:, :, None], (depth, HEAD_DIM, TQ))

    h, hb = _layer_norm0(x.reshape(b * t, d), ln0_g, ln0_b)
    for l in range(depth):
        hb3 = hb.reshape(b, t, d)
        avt, bvt, cvt, dcq, ckv_iw, eq, z = _proj_plain(hb3, w_plain, w_vt, plain_widths, l)
        aq, ak, iq, ik, bq, bk, cq, ck, kr = _proj_rope(hb3, w_rope, ctab, stab, rope_heads, rope_tables, l)

        o_a = _dsa(aq, ak, avt, iq, ik, ik)
        o_b = _moba(bq, bk, bvt, _kbar(bk))
        lam_init = 0.8 - 0.6 * math.exp(-0.3 * l)
        misc = jnp.full((SUBLANES, LANES), lam_init, F32)
        o_c = _diff(cq, ck, cvt, diff_lam[l].astype(F32), norm_t[l], misc)
        qm, km, vmt = _mla_prep(dcq, ckv_iw, kr, mla_q_norm[l].reshape(1, Q_LORA), mla_kv_norm[l].reshape(1, KV_LORA),
                                wq, wq_rot, wk, wvt, place, ct_q, st_q, l)
        o_d = _mla(qm, km, vmt)
        o_e = _mem_attn(eq, *_mem_kv(mem, wmem, l))

        os5 = [o.reshape(b * t, BRANCH_W) for o in (o_a, o_b, o_c, o_d, o_e)]
        h, hb = _final(h, hb, os5, z.reshape(b * t, N_BRANCH * BRANCH_W), wg, wb, wo,
                       ln_g[l].reshape(1, d), ln_b[l].reshape(1, d), alpha, l)
    return h.reshape(b, t, d)
```

```python
import functools
import math

import numpy as np
import jax
import jax.numpy as jnp
from jax import lax
from jax.experimental import pallas as pl
from jax.experimental.pallas import tpu as pltpu

F32 = jnp.float32
BF16 = jnp.bfloat16
I32 = jnp.int32
I16 = jnp.int16

N_HEADS = 4
HEAD_DIM = 64
BRANCH_W = N_HEADS * HEAD_DIM
N_BRANCH = 5
ROPE_THETA = 500000.0
ROT_64 = 16
ROT_32 = 8
IDX_HEADS = 8
IDX_DIM = 32
TOPK_MAX = 256
MOBA_BLOCK = 256
MOBA_TOPK = 3
DIFF_DIM = 32
Q_LORA = 256
KV_LORA = 128
MLA_NOPE = 64
MLA_ROPE = 32
MLA_V = 64
LN_EPS = 1e-5
RMS_EPS = 1e-6

IN_LAYOUT = (
    ("a_q", BRANCH_W), ("a_k", BRANCH_W), ("a_v", BRANCH_W),
    ("i_q", IDX_HEADS * IDX_DIM), ("i_k", IDX_DIM), ("i_w", IDX_HEADS),
    ("b_q", BRANCH_W), ("b_k", BRANCH_W), ("b_v", BRANCH_W),
    ("c_q", BRANCH_W), ("c_k", BRANCH_W), ("c_v", BRANCH_W),
    ("d_cq", Q_LORA), ("d_ckv", KV_LORA), ("d_kr", MLA_ROPE),
    ("e_q", BRANCH_W),
    ("z", N_BRANCH * BRANCH_W),
    ("g", N_BRANCH * 1024),
)

SUBLANES = 8
LANES = 128
MXU_N = 256
TQ = 512
CK = 256
VROWS = HEAD_DIM + 16
FLASH_UNROLL = 4
NEG = -1e30
LOG2E = math.log2(math.e)
INT_MIN = np.int32(-2 ** 31)
HALF16 = 1 << 15
VMEM_LIMIT = 56 * 1024 * 1024


def _offsets():
    off, out = 0, {}
    for name, size in IN_LAYOUT:
        out[name] = (off, size)
        off += size
    return out


OFF = _offsets()


def _nt_dot(a, b):
    return lax.dot_general(a, b, (((1,), (1,)), ((), ())), preferred_element_type=F32)


def _tn_dot(w, x):
    return lax.dot_general(w, x, (((0,), (1,)), ((), ())), preferred_element_type=F32)


def _fold_rows(w, rows=SUBLANES):
    xs = [w[r:r + rows, :] for r in range(0, w.shape[0], rows)]
    while len(xs) > 1:
        xs = [xs[j] + xs[j + 1] for j in range(0, len(xs) - 1, 2)] + ([xs[-1]] if len(xs) % 2 else [])
    return xs[0]


def _masked_qt(q, shift, n, qt_ref):
    qt = q.T
    dim = lax.broadcasted_iota(I32, (LANES, qt.shape[1]), 0)
    for j in range(n):
        half = (j << shift) // LANES
        rows = qt[half * LANES:(half + 1) * LANES, :]
        qt_ref[j] = jnp.where(((dim + half * LANES) >> shift) == j, rows, 0.0).astype(BF16)


def _half(kc, j, shift):
    half = (j << shift) // LANES
    return kc[:, half * LANES:(half + 1) * LANES]


def _cparams(n_axes):
    return pltpu.CompilerParams(dimension_semantics=("arbitrary",) * n_axes,
                                vmem_limit_bytes=VMEM_LIMIT)


def _layer_spec(a, l):
    return pl.BlockSpec((None,) + a.shape[1:], lambda *_: (l,) + (0,) * (a.ndim - 1))


def _softmax_step(s_t, m_tile, vt_h, m_ref, acc_ref):
    m_old = m_ref[...]
    m_new = jnp.maximum(m_old, m_tile)
    alpha = jnp.exp2(m_old - m_new)
    p = jnp.exp2(s_t - m_new)
    acc_ref[...] = alpha * acc_ref[...] + jnp.dot(vt_h, p.astype(BF16), preferred_element_type=F32)
    m_ref[...] = m_new


def _softmax_init(m_ref, acc_ref):
    m_ref[...] = jnp.full(m_ref.shape, NEG, F32)
    acc_ref[...] = jnp.zeros(acc_ref.shape, F32)


def _softmax_out(acc_ref):
    return acc_ref[:HEAD_DIM, :] / acc_ref[HEAD_DIM:HEAD_DIM + 1, :]


def _store_vt(o_ref, vt):
    ones = jnp.ones((VROWS - HEAD_DIM, CK), o_ref.dtype)
    for j in range(o_ref.shape[0]):
        for h in range(N_HEADS):
            o_ref[j, h * VROWS:h * VROWS + HEAD_DIM, :] = (
                vt[h * HEAD_DIM:(h + 1) * HEAD_DIM, j * CK:(j + 1) * CK].astype(o_ref.dtype))
            o_ref[j, h * VROWS + HEAD_DIM:(h + 1) * VROWS, :] = ones


def _flash_loop(n_full, qk_all, mask, vt_rows, state, prep=None, causal_tail=True):
    s_ref, mx_ref, m_ref, acc_ref = state
    n_state = m_ref.shape[0]
    for j in range(n_state):
        _softmax_init(m_ref.at[j], acc_ref.at[j])

    def lanes_of(d):
        return slice(CK, TQ) if d == 1 else slice(None)

    def park(c, slot, d=None):
        lanes = lanes_of(d)
        ctx = c if prep is None else prep(c, lanes)
        for j, s in enumerate(qk_all(c, lanes)):
            if mask is not None:
                s = mask(ctx, j, s, lanes)
            if d is not None and causal_tail:
                s = jnp.where(_causal(d), s, NEG)
            s_ref[slot, j, :, lanes] = s
            mx_ref[slot, j, :, lanes] = jnp.max(s, axis=0, keepdims=True)

    def consume(c, slot, d=None):
        lanes = lanes_of(d)
        for j in range(n_state):
            _softmax_step(s_ref[slot, j, :, lanes], mx_ref[slot, j, :, lanes], vt_rows(c, j),
                          m_ref.at[j, :, lanes], acc_ref.at[j, :, lanes])

    def pair(c):
        park(c + 1, 1)
        consume(c, 0)
        park(c + 2, 0)
        consume(c + 1, 1)

    def body(g, carry):
        for u in range(0, FLASH_UNROLL, 2):
            pair(FLASH_UNROLL * g + u)
        return carry

    @pl.when(n_full == 0)
    def _():
        park(0, 0, d=0)
        park(1, 1, d=1)
        consume(0, 0)
        consume(1, 1, d=1)

    @pl.when(n_full > 0)
    def _():
        park(0, 0)
        n_loop = n_full - 2
        n_group = lax.shift_right_logical(n_loop, FLASH_UNROLL.bit_length() - 1)
        lax.fori_loop(0, n_group, body, 0)
        c0 = FLASH_UNROLL * n_group
        for u in range(FLASH_UNROLL // 2 - 1):
            @pl.when(n_loop - c0 >= 2 * (u + 1))
            def _(u=u):
                pair(c0 + 2 * u)
        c = n_loop
        park(c + 1, 1)
        consume(c, 0)
        park(c + 2, 0, d=0)
        consume(c + 1, 1)
        park(c + 3, 1, d=1)
        consume(c + 2, 0)
        consume(c + 3, 1, d=1)


def _causal(d):
    shape = (CK, TQ - d * CK)
    return lax.broadcasted_iota(I32, shape, 0) <= lax.broadcasted_iota(I32, shape, 1)


def _attn_scratch(n_state):
    return [pltpu.VMEM((2, n_state, CK, TQ), F32), pltpu.VMEM((2, n_state, 1, TQ), F32),
            pltpu.VMEM((n_state, 1, TQ), F32), pltpu.VMEM((n_state, VROWS, TQ), F32),
            pltpu.VMEM((BRANCH_W, TQ), F32)]


def _kv_specs(t, w):
    kspec = pl.BlockSpec((None, t, w), lambda bb, i: (bb, 0, 0))
    vspec = pl.BlockSpec((None, t // CK, N_HEADS * VROWS, CK), lambda bb, i: (bb, 0, 0, 0))
    return kspec, vspec


def _ln_kernel(x_ref, g_ref, b_ref, h_ref, hb_ref):
    x = x_ref[...]
    mu = jnp.mean(x, axis=1, keepdims=True)
    xc = x - mu
    var = jnp.mean(xc * xc, axis=1, keepdims=True)
    y = xc * lax.rsqrt(var + LN_EPS) * g_ref[...] + b_ref[...]
    h_ref[...] = y
    hb_ref[...] = y.astype(BF16)


def _layer_norm0(x2, g, b):
    n, d = x2.shape
    tm = 512
    row = pl.BlockSpec((tm, d), lambda i: (i, 0))
    vec = pl.BlockSpec((1, d), lambda i: (0, 0))
    return pl.pallas_call(
        _ln_kernel,
        out_shape=(jax.ShapeDtypeStruct((n, d), F32), jax.ShapeDtypeStruct((n, d), BF16)),
        grid=(n // tm,),
        in_specs=[row, vec, vec],
        out_specs=(row, row),
        compiler_params=_cparams(1),
        name="ln0",
    )(x2, g.reshape(1, d), b.reshape(1, d))


def _proj_plain_kernel(x_ref, w_ref, wt_ref, *out_refs, n_t):
    for g, o_ref in enumerate(out_refs[:n_t]):
        _store_vt(o_ref, _tn_dot(wt_ref[:, g * BRANCH_W:(g + 1) * BRANCH_W], x_ref[...]))
    off = 0
    for o_ref in out_refs[n_t:]:
        wd = o_ref.shape[-1]
        for j in range(0, wd, MXU_N):
            acc = jnp.dot(x_ref[...], w_ref[:, off + j:off + j + MXU_N], preferred_element_type=F32)
            o_ref[:, j:j + MXU_N] = acc.astype(o_ref.dtype)
        off += wd


def _proj_plain(hb3, w, wt, widths, l):
    b, t, d = hb3.shape
    tm = 512
    n_t = wt.shape[-1] // BRANCH_W
    shapes = [jax.ShapeDtypeStruct((b, t // CK, N_HEADS * VROWS, CK), BF16)] * n_t
    specs = [pl.BlockSpec((None, tm // CK, N_HEADS * VROWS, CK), lambda i, bb: (bb, i, 0, 0))] * n_t
    shapes += [jax.ShapeDtypeStruct((b, t, wd), BF16) for wd in widths]
    specs += [pl.BlockSpec((None, tm, wd), lambda i, bb: (bb, i, 0)) for wd in widths]
    return pl.pallas_call(
        functools.partial(_proj_plain_kernel, n_t=n_t),
        out_shape=tuple(shapes),
        grid=(t // tm, b),
        in_specs=[pl.BlockSpec((None, tm, d), lambda i, bb: (bb, i, 0)),
                  _layer_spec(w, l), _layer_spec(wt, l)],
        out_specs=tuple(specs),
        compiler_params=_cparams(2),
        name="proj_plain",
    )(hb3, w, wt)


def _proj_rope_kernel(x_ref, w_ref, c_ref, s_ref, *out_refs, heads, tables):
    lane = lax.broadcasted_iota(I32, (x_ref.shape[0], MXU_N), 1)
    for g, o_ref in enumerate(out_refs):
        hd, half = heads[g]
        sl = slice(g * MXU_N, (g + 1) * MXU_N)
        acc = jnp.dot(x_ref[...], w_ref[:, sl], preferred_element_type=F32)
        partner = jnp.where((lane & (hd - 1)) < half,
                            pltpu.roll(acc, MXU_N - half, 1), pltpu.roll(acc, half, 1))
        o_ref[...] = (acc * c_ref[tables[g]] + partner * s_ref[tables[g]]).astype(o_ref.dtype)


def _proj_rope(hb3, w, ctab, stab, heads, tables, l):
    b, t, d = hb3.shape
    tm = 512
    assert w.shape[-1] == MXU_N * len(heads)
    tspec = pl.BlockSpec((ctab.shape[0], tm, MXU_N), lambda i, bb: (0, i, 0))
    ospec = pl.BlockSpec((None, tm, MXU_N), lambda i, bb: (bb, i, 0))
    return pl.pallas_call(
        functools.partial(_proj_rope_kernel, heads=heads, tables=tables),
        out_shape=(jax.ShapeDtypeStruct((b, t, MXU_N), BF16),) * len(heads),
        grid=(t // tm, b),
        in_specs=[pl.BlockSpec((None, tm, d), lambda i, bb: (bb, i, 0)),
                  _layer_spec(w, l), tspec, tspec],
        out_specs=(ospec,) * len(heads),
        compiler_params=_cparams(2),
        name="proj_rope",
    )(hb3, w, ctab, stab)


def _dsa_kernel(aq_ref, ak_ref, avt_ref, iq_ref, ik_ref, iw_ref, pick_ref, tri_ref, o_ref,
                keys_ref, hi_ref, lo_ref, bk_ref, iqt_ref, aqt_ref, wt_ref, thr_ref, s_ref, mx_ref, m_ref, acc_ref, ot_ref,
                *, topk, idx_scale):
    i = pl.program_id(1)
    n_full = 2 * i

    iqt = iq_ref[...].astype(F32).T
    for hh in range(IDX_HEADS):
        iqt_ref[hh] = iqt[hh * IDX_DIM:(hh + 1) * IDX_DIM, :].astype(BF16)
    _masked_qt(aq_ref[...].astype(F32) * (HEAD_DIM ** -0.5 * LOG2E), 6, N_HEADS, aqt_ref)
    wt_ref[...] = _nt_dot(pick_ref[...], iw_ref[...]) * idx_scale

    def lanes_of(d):
        return slice(CK, TQ) if d == 1 else slice(None)

    def logits(c, d):
        kc = ik_ref[pl.ds(pl.multiple_of(c * CK, CK), CK), :]
        return [jnp.dot(kc[:, :IDX_DIM], iqt_ref[hh, :, lanes_of(d)], preferred_element_type=F32)
                for hh in range(IDX_HEADS)]

    def put_keys(c, key, lanes):
        keys_ref[c, :, lanes] = key
        hi_ref[c, :, lanes] = (key >> 16).astype(I16)
        lo_ref[c, :, lanes] = ((key & 0xFFFF) - HALF16).astype(I16)

    def score_chunk(c, lg, d):
        lanes = lanes_of(d)
        sc = jnp.zeros(lg[0].shape, F32)
        for hh in range(IDX_HEADS):
            sc = sc + jnp.maximum(lg[hh], 0.0) * wt_ref[hh:hh + 1, lanes]
        bits = pltpu.bitcast(sc, I32)
        key = jnp.where(bits < 0, INT_MIN - bits, bits)
        put_keys(c, key if d is None else jnp.where(_causal(d), key, INT_MIN), lanes)
        if d == 1:
            put_keys(c, jnp.full((CK, CK), INT_MIN, I32), slice(0, CK))

    def score_pair(c, d0, d1):
        lg0, lg1 = logits(c, d0), logits(c + 1, d1)
        score_chunk(c, lg0, d0)
        score_chunk(c + 1, lg1, d1)

    def score_body(p, carry):
        score_pair(2 * p, None, None)
        return carry

    lax.fori_loop(0, i, score_body, 0)
    score_pair(n_full, 0, 1)

    def pair_loop(body, init, last=None):
        def pair(p, carry):
            return body(2 * p + 1, body(2 * p, carry))
        carry = body(n_full, lax.fori_loop(0, i, pair, init))
        return (last or body)(n_full + 1, carry)

    def count16(pred, also=None):
        def hits(c, lanes):
            hit = jnp.where(pred(c, lanes), jnp.int16(1), jnp.int16(0))
            if also is not None:
                hit = jnp.where(also(c, lanes), hit, jnp.int16(0))
            return _fold_rows(hit, 2 * SUBLANES)

        def body(c, part):
            return part + hits(c, slice(None))

        def last(c, part):
            return jnp.concatenate([part[:, :CK], part[:, CK:] + hits(c, slice(CK, TQ))], axis=1)

        part = pair_loop(body, jnp.zeros((2 * SUBLANES, TQ), I16), last)
        return jnp.sum(part.astype(F32), axis=0, keepdims=True)

    def search16(ref, need):
        def bit_body(bi, t_u):
            c_u = t_u | jnp.left_shift(jnp.int32(1), 15 - bi)
            ck = (c_u - HALF16).astype(I16)
            cnt = count16(lambda c, lanes: ref[c, :, lanes] >= ck[:, lanes])
            return jnp.where(cnt >= need, c_u, t_u)
        return lax.fori_loop(0, 16, bit_body, jnp.zeros((1, TQ), I32))

    hi_u = search16(hi_ref, float(topk))
    thr_hi = (hi_u - HALF16).astype(I16)
    n_above = count16(lambda c, lanes: hi_ref[c, :, lanes] > thr_hi[:, lanes])

    def bucket_body(c, carry):
        bk_ref[c] = jnp.where(hi_ref[c] == thr_hi, lo_ref[c], jnp.int16(-HALF16))
        return carry

    pair_loop(bucket_body, 0)
    lo_u = search16(bk_ref, float(topk) - n_above)
    thr_lo = (lo_u - HALF16).astype(I16)
    thr = ((hi_u - HALF16) << 16) | lo_u

    n_gt = n_above + count16(lambda c, lanes: bk_ref[c, :, lanes] > thr_lo[:, lanes])
    n_eq = count16(lambda c, lanes: lo_ref[c, :, lanes] == thr_lo[:, lanes],
                   also=lambda c, lanes: hi_ref[c, :, lanes] == thr_hi[:, lanes])
    need = float(topk) - n_gt
    amb = jnp.logical_and(n_eq > need, thr > INT_MIN)
    any_amb = jnp.max(jnp.where(amb, 1.0, 0.0)) > 0.5

    @pl.when(any_amb)
    def _():
        def drop_body(c, seen):
            k = keys_ref[c]
            eq = k == thr
            eqf = jnp.where(eq, 1.0, 0.0)
            rank = jnp.dot(tri_ref[...], eqf.astype(BF16), preferred_element_type=F32) + seen
            drop = jnp.logical_and(jnp.logical_and(eq, rank > need), amb)
            keys_ref[c] = jnp.where(drop, INT_MIN, k)
            return seen + jnp.sum(eqf, axis=0, keepdims=True)

        pair_loop(drop_body, jnp.zeros((1, TQ), F32))

    thr_ref[...] = jnp.maximum(thr, INT_MIN + 1)

    def qk_all(c, lanes):
        kc = ak_ref[pl.ds(pl.multiple_of(c * CK, CK), CK), :]
        return [jnp.dot(_half(kc, h, 6), aqt_ref[h, :, lanes], preferred_element_type=F32) for h in range(N_HEADS)]

    _flash_loop(n_full, qk_all,
                lambda keep, h, s, lanes: jnp.where(keep, s, NEG),
                lambda c, h: avt_ref[c, h * VROWS:(h + 1) * VROWS, :],
                (s_ref, mx_ref, m_ref, acc_ref),
                prep=lambda c, lanes: keys_ref[c, :, lanes] >= thr_ref[:, lanes], causal_tail=False)
    for h in range(N_HEADS):
        ot_ref[h * HEAD_DIM:(h + 1) * HEAD_DIM, :] = _softmax_out(acc_ref.at[h])
    o_ref[...] = ot_ref[...].T.astype(o_ref.dtype)


def _dsa(aq, ak, avt, iq, ik, iw):
    b, t, _ = aq.shape
    topk = min(TOPK_MAX, t // 4)
    qspec = pl.BlockSpec((None, TQ, BRANCH_W), lambda bb, i: (bb, i, 0))
    kspec, vspec = _kv_specs(t, BRANCH_W)
    pick = np.zeros((2 * SUBLANES, MXU_N), np.float32)
    for hh in range(IDX_HEADS):
        pick[hh, IDX_DIM + hh] = 1.0
    pick = jnp.asarray(pick, BF16)
    tri = jnp.asarray(np.tril(np.ones((CK, CK), np.float32)), BF16)
    kern = functools.partial(_dsa_kernel, topk=topk, idx_scale=(IDX_HEADS * IDX_DIM) ** -0.5)
    return pl.pallas_call(
        kern,
        out_shape=jax.ShapeDtypeStruct((b, t, BRANCH_W), BF16),
        grid=(b, t // TQ),
        in_specs=[qspec, kspec, vspec, qspec, kspec, qspec,
                  pl.BlockSpec(pick.shape, lambda bb, i: (0, 0)), pl.BlockSpec(tri.shape, lambda bb, i: (0, 0))],
        out_specs=qspec,
        scratch_shapes=[
            pltpu.VMEM((t // CK, CK, TQ), I32),
            pltpu.VMEM((t // CK, CK, TQ), I16),
            pltpu.VMEM((t // CK, CK, TQ), I16),
            pltpu.VMEM((t // CK, CK, TQ), I16),
            pltpu.VMEM((IDX_HEADS, IDX_DIM, TQ), BF16),
            pltpu.VMEM((N_HEADS, LANES, TQ), BF16),
            pltpu.VMEM((2 * SUBLANES, TQ), F32),
            pltpu.VMEM((1, TQ), I32),
        ] + _attn_scratch(N_HEADS),
        compiler_params=_cparams(2),
        name="dsa",
    )(aq, ak, avt, iq, ik, iw, pick, tri)


def _kbar_kernel(k_ref, o_ref):
    o_ref[...] = jnp.zeros(o_ref.shape, o_ref.dtype)
    nb = k_ref.shape[0] // MOBA_BLOCK
    for n in range(nb):
        blk = k_ref[n * MOBA_BLOCK:(n + 1) * MOBA_BLOCK, :].astype(F32)
        o_ref[n:n + 1, :] = jnp.mean(blk, axis=0, keepdims=True).astype(o_ref.dtype)


def _kbar(bk):
    b, t, w = bk.shape
    nbp = max(2 * SUBLANES, t // MOBA_BLOCK)
    return pl.pallas_call(
        _kbar_kernel,
        out_shape=jax.ShapeDtypeStruct((b, nbp, w), BF16),
        grid=(b,),
        in_specs=[pl.BlockSpec((None, t, w), lambda bb: (bb, 0, 0))],
        out_specs=pl.BlockSpec((None, nbp, w), lambda bb: (bb, 0, 0)),
        compiler_params=_cparams(1),
        name="moba_kbar",
    )(bk)


def _moba_kernel(q_ref, k_ref, vt_ref, kbar_ref, o_ref, qt_ref, bias_ref, s_ref, mx_ref, m_ref, acc_ref, ot_ref):
    i = pl.program_id(1)
    nbp = kbar_ref.shape[0]
    blk = lax.broadcasted_iota(I32, (nbp, TQ), 0)
    blk_f = blk.astype(F32)
    own = 2 * i + (lax.broadcasted_iota(I32, (nbp, TQ), 1) >> (MOBA_BLOCK.bit_length() - 1))
    _masked_qt(q_ref[...].astype(F32) * (HEAD_DIM ** -0.5 * LOG2E), 6, N_HEADS, qt_ref)

    for h in range(N_HEADS):
        g = jnp.where(blk < own, jnp.dot(_half(kbar_ref[...], h, 6), qt_ref[h], preferred_element_type=F32), NEG)
        bias = jnp.full((nbp, TQ), NEG, F32)
        for _ in range(MOBA_TOPK):
            mx = jnp.max(g, axis=0, keepdims=True)
            first = jnp.min(jnp.where(g == mx, blk_f, 1e9), axis=0, keepdims=True)
            pick = jnp.logical_and(blk_f == first, mx > 0.5 * NEG)
            bias = jnp.where(pick, 0.0, bias)
            g = jnp.where(pick, NEG, g)
        bias_ref[h] = jnp.where(blk == own, 0.0, bias)

    def qk_all(c, lanes):
        kc = k_ref[pl.ds(pl.multiple_of(c * CK, CK), CK), :]
        return [jnp.dot(_half(kc, h, 6), qt_ref[h, :, lanes], preferred_element_type=F32) for h in range(N_HEADS)]

    _flash_loop(2 * i, qk_all, lambda c, h, s, lanes: s + bias_ref[h, pl.ds(c, 1), lanes],
                lambda c, h: vt_ref[c, h * VROWS:(h + 1) * VROWS, :], (s_ref, mx_ref, m_ref, acc_ref))
    for h in range(N_HEADS):
        ot_ref[h * HEAD_DIM:(h + 1) * HEAD_DIM, :] = _softmax_out(acc_ref.at[h])
    o_ref[...] = ot_ref[...].T.astype(o_ref.dtype)


def _moba(bq, bk, bvt, kbar):
    b, t, w = bq.shape
    assert TQ == 2 * MOBA_BLOCK and CK == MOBA_BLOCK and t % TQ == 0
    nbp = kbar.shape[1]
    qspec = pl.BlockSpec((None, TQ, w), lambda bb, i: (bb, i, 0))
    kspec, vspec = _kv_specs(t, w)
    return pl.pallas_call(
        _moba_kernel,
        out_shape=jax.ShapeDtypeStruct((b, t, w), BF16),
        grid=(b, t // TQ),
        in_specs=[qspec, kspec, vspec, pl.BlockSpec((None, nbp, w), lambda bb, i: (bb, 0, 0))],
        out_specs=qspec,
        scratch_shapes=[pltpu.VMEM((N_HEADS, LANES, TQ), BF16), pltpu.VMEM((N_HEADS, nbp, TQ), F32)]
        + _attn_scratch(N_HEADS),
        compiler_params=_cparams(2),
        name="moba",
    )(bq, bk, bvt, kbar)


def _diff_kernel(q_ref, k_ref, vt_ref, lam_ref, norm_ref, misc_ref, o_ref,
                 qt_ref, s_ref, mx_ref, m_ref, acc_ref, ot_ref):
    i = pl.program_id(1)
    _masked_qt(q_ref[...].astype(F32) * (DIFF_DIM ** -0.5 * LOG2E), 5, 2 * N_HEADS, qt_ref)

    dl = lam_ref[...]
    lam_init = misc_ref[0:1, 0:1]
    lam = (jnp.exp(jnp.sum(dl[0:1, :] * dl[1:2, :], axis=1, keepdims=True))
           - jnp.exp(jnp.sum(dl[2:3, :] * dl[3:4, :], axis=1, keepdims=True)) + lam_init)

    def qk_all(c, lanes):
        kc = k_ref[pl.ds(pl.multiple_of(c * CK, CK), CK), :]
        return [jnp.dot(_half(kc, j, 5), qt_ref[j, :, lanes], preferred_element_type=F32) for j in range(2 * N_HEADS)]

    _flash_loop(2 * i, qk_all, None,
                lambda c, j: vt_ref[c, (j // 2) * VROWS:(j // 2 + 1) * VROWS, :],
                (s_ref, mx_ref, m_ref, acc_ref))

    post = norm_ref[...] * (1.0 - lam_init)
    for h in range(N_HEADS):
        o_h = _softmax_out(acc_ref.at[2 * h]) - lam * _softmax_out(acc_ref.at[2 * h + 1])
        ms = jnp.mean(o_h * o_h, axis=0, keepdims=True)
        ot_ref[h * HEAD_DIM:(h + 1) * HEAD_DIM, :] = o_h * lax.rsqrt(ms + RMS_EPS) * post
    o_ref[...] = ot_ref[...].T.astype(o_ref.dtype)


def _diff(cq, ck, cvt, lam, norm, misc):
    b, t, w = cq.shape
    qspec = pl.BlockSpec((None, TQ, w), lambda bb, i: (bb, i, 0))
    kspec, vspec = _kv_specs(t, w)
    full = lambda a: pl.BlockSpec(a.shape, lambda bb, i: (0,) * a.ndim)
    return pl.pallas_call(
        _diff_kernel,
        out_shape=jax.ShapeDtypeStruct((b, t, w), BF16),
        grid=(b, t // TQ),
        in_specs=[qspec, kspec, vspec, full(lam), full(norm), full(misc)],
        out_specs=qspec,
        scratch_shapes=[pltpu.VMEM((2 * N_HEADS, LANES, TQ), BF16)] + _attn_scratch(2 * N_HEADS),
        compiler_params=_cparams(2),
        name="diff",
    )(cq, ck, cvt, lam, norm, misc)


def _mla_prep_kernel(cq_ref, ckv_ref, kr_ref, qn_ref, kvn_ref, wq_ref, wqr_ref, wk_ref, wvt_ref,
                     p_ref, ct_ref, st_ref, q_out, k_out, vt_out):
    x = cq_ref[...].astype(F32)
    xn = (x * lax.rsqrt(jnp.mean(x * x, axis=1, keepdims=True) + RMS_EPS) * qn_ref[...]).astype(BF16)
    q = (jnp.dot(xn, wq_ref[...], preferred_element_type=F32) * ct_ref[...]
         + jnp.dot(xn, wqr_ref[...], preferred_element_type=F32) * st_ref[...])
    q_out[...] = q.astype(q_out.dtype)
    c = ckv_ref[:, :KV_LORA].astype(F32)
    cn = (c * lax.rsqrt(jnp.mean(c * c, axis=1, keepdims=True) + RMS_EPS) * kvn_ref[...]).astype(BF16)
    k = (jnp.dot(cn, wk_ref[...], preferred_element_type=F32)
         + jnp.dot(kr_ref[...], p_ref[...], preferred_element_type=F32))
    k_out[...] = k.astype(k_out.dtype)
    _store_vt(vt_out, _tn_dot(wvt_ref[...], cn))


def _mla_prep(dcq, ckv, kr, qn, kvn, wq, wqr, wk, wvt, pmat, ct, st, l):
    b, t, _ = dcq.shape
    tm = 512
    hw = N_HEADS * LANES
    row = lambda w: pl.BlockSpec((None, tm, w), lambda i, bb: (bb, i, 0))
    full = lambda a: pl.BlockSpec(a.shape, lambda i, bb: (0,) * a.ndim)
    tab = pl.BlockSpec((tm, hw), lambda i, bb: (i, 0))
    return pl.pallas_call(
        _mla_prep_kernel,
        out_shape=(jax.ShapeDtypeStruct((b, t, hw), BF16), jax.ShapeDtypeStruct((b, t, hw), BF16),
                   jax.ShapeDtypeStruct((b, t // CK, N_HEADS * VROWS, CK), BF16)),
        grid=(t // tm, b),
        in_specs=[row(Q_LORA), row(MXU_N), row(MXU_N), full(qn), full(kvn), _layer_spec(wq, l), _layer_spec(wqr, l),
                  _layer_spec(wk, l), _layer_spec(wvt, l), full(pmat), tab, tab],
        out_specs=(row(hw), row(hw),
                   pl.BlockSpec((None, tm // CK, N_HEADS * VROWS, CK), lambda i, bb: (bb, i, 0, 0))),
        compiler_params=_cparams(2),
        name="mla_prep",
    )(dcq, ckv, kr, qn, kvn, wq, wqr, wk, wvt, pmat, ct, st)


def _mla_kernel(q_ref, k_ref, vt_ref, o_ref, qt_ref, s_ref, mx_ref, m_ref, acc_ref, ot_ref):
    i = pl.program_id(1)
    hs = [slice(h * LANES, (h + 1) * LANES) for h in range(N_HEADS)]
    for h in range(N_HEADS):
        qt_ref[h] = q_ref[:, hs[h]].astype(F32).T.astype(BF16)

    def qk_all(c, lanes):
        start = pl.multiple_of(c * CK, CK)
        return [jnp.dot(k_ref[pl.ds(start, CK), hs[h]], qt_ref[h, :, lanes], preferred_element_type=F32)
                for h in range(N_HEADS)]

    _flash_loop(2 * i, qk_all, None,
                lambda c, h: vt_ref[c, h * VROWS:(h + 1) * VROWS, :],
                (s_ref, mx_ref, m_ref, acc_ref))
    for h in range(N_HEADS):
        ot_ref[h * HEAD_DIM:(h + 1) * HEAD_DIM, :] = _softmax_out(acc_ref.at[h])
    o_ref[...] = ot_ref[...].T.astype(o_ref.dtype)


def _mla(qm, km, vmt):
    b, t, hw = qm.shape
    kspec, vspec = _kv_specs(t, hw)
    return pl.pallas_call(
        _mla_kernel,
        out_shape=jax.ShapeDtypeStruct((b, t, BRANCH_W), BF16),
        grid=(b, t // TQ),
        in_specs=[pl.BlockSpec((None, TQ, hw), lambda bb, i: (bb, i, 0)), kspec, vspec],
        out_specs=pl.BlockSpec((None, TQ, BRANCH_W), lambda bb, i: (bb, i, 0)),
        scratch_shapes=[pltpu.VMEM((N_HEADS, LANES, TQ), BF16)] + _attn_scratch(N_HEADS),
        compiler_params=_cparams(2),
        name="mla",
    )(qm, km, vmt)


def _mem_kv_kernel(x_ref, w_ref, k_ref, vt_ref):
    x = x_ref[...].astype(BF16)
    k_ref[...] = jnp.dot(x, w_ref[:, :BRANCH_W], preferred_element_type=F32).astype(k_ref.dtype)
    _store_vt(vt_ref, _tn_dot(w_ref[:, BRANCH_W:], x))


def _mem_kv(mem, w, l):
    b, m, d = mem.shape
    assert m % CK == 0
    return pl.pallas_call(
        _mem_kv_kernel,
        out_shape=(jax.ShapeDtypeStruct((b, m, BRANCH_W), BF16),
                   jax.ShapeDtypeStruct((b, m // CK, N_HEADS * VROWS, CK), BF16)),
        grid=(b,),
        in_specs=[pl.BlockSpec((None, m, d), lambda bb: (bb, 0, 0)), _layer_spec(w, l)],
        out_specs=(pl.BlockSpec((None, m, BRANCH_W), lambda bb: (bb, 0, 0)),
                   pl.BlockSpec((None, m // CK, N_HEADS * VROWS, CK), lambda bb: (bb, 0, 0, 0))),
        compiler_params=_cparams(1),
        name="mem_kv",
    )(mem, w)


def _mem_kernel(q_ref, k_ref, vt_ref, o_ref, qt_ref, ot_ref):
    _masked_qt(q_ref[...].astype(F32) * (HEAD_DIM ** -0.5 * LOG2E), 6, N_HEADS, qt_ref)
    s_all = [jnp.dot(_half(k_ref[...], h, 6), qt_ref[h], preferred_element_type=F32) for h in range(N_HEADS)]
    for h in range(N_HEADS):
        s_t = s_all[h]
        p = jnp.exp2(s_t - jnp.max(s_t, axis=0, keepdims=True)).astype(BF16)
        acc = jnp.dot(vt_ref[0, h * VROWS:(h + 1) * VROWS, :], p, preferred_element_type=F32)
        ot_ref[h * HEAD_DIM:(h + 1) * HEAD_DIM, :] = acc[:HEAD_DIM, :] / acc[HEAD_DIM:HEAD_DIM + 1, :]
    o_ref[...] = ot_ref[...].T.astype(o_ref.dtype)


def _mem_attn(eq, mk, mvt):
    b, t, w = eq.shape
    m = mk.shape[1]
    assert m == CK
    return pl.pallas_call(
        _mem_kernel,
        out_shape=jax.ShapeDtypeStruct((b, t, w), BF16),
        grid=(b, t // TQ),
        in_specs=[pl.BlockSpec((None, TQ, w), lambda bb, i: (bb, i, 0)),
                  pl.BlockSpec((None, m, w), lambda bb, i: (bb, 0, 0)),
                  pl.BlockSpec((None,) + mvt.shape[1:], lambda bb, i: (bb, 0, 0, 0))],
        out_specs=pl.BlockSpec((None, TQ, w), lambda bb, i: (bb, i, 0)),
        scratch_shapes=[pltpu.VMEM((N_HEADS, LANES, TQ), BF16), pltpu.VMEM((BRANCH_W, TQ), F32)],
        compiler_params=_cparams(2),
        name="mem_attn",
    )(eq, mk, mvt)


def _final_kernel(h_ref, hb_ref, oa_ref, ob_ref, oc_ref, od_ref, oe_ref, z_ref,
                  wg_ref, wb_ref, wo_ref, g_ref, b_ref, h_out, hb_out, acc_ref, *, alpha):
    d = h_ref.shape[1]
    half = h_ref.shape[0] // 2
    for n, o_ref in enumerate((oa_ref, ob_ref, oc_ref, od_ref, oe_ref)):
        for r in range(2):
            rows = slice(r * half, (r + 1) * half)
            z = z_ref[rows, n * BRANCH_W:(n + 1) * BRANCH_W].astype(F32)
            y = o_ref[rows, :].astype(F32) * (z / (1.0 + jnp.exp(-z)))
            u = jnp.dot(y.astype(BF16), wb_ref[n], preferred_element_type=F32)
            g = jnp.dot(hb_ref[rows, :], wg_ref[:, n * d:(n + 1) * d], preferred_element_type=F32)
            t = u / (1.0 + jnp.exp(-g))
            acc_ref[rows, :] = t if n == 0 else acc_ref[rows, :] + t
    for r in range(2):
        rows = slice(r * half, (r + 1) * half)
        out = jnp.dot(acc_ref[rows, :].astype(BF16), wo_ref[...], preferred_element_type=F32)
        x = alpha * h_ref[rows, :] + out
        mu = jnp.mean(x, axis=1, keepdims=True)
        xc = x - mu
        var = jnp.mean(xc * xc, axis=1, keepdims=True)
        y = xc * lax.rsqrt(var + LN_EPS) * g_ref[...] + b_ref[...]
        h_out[rows, :] = y
        hb_out[rows, :] = y.astype(BF16)


def _final(h, hb, os5, z, wg, wb, wo, ln_g, ln_b, alpha, l):
    n, d = h.shape
    tm = 512
    row = lambda w: pl.BlockSpec((tm, w), lambda i: (i, 0))
    full = lambda a: pl.BlockSpec(a.shape, lambda i: (0,) * a.ndim)
    return pl.pallas_call(
        functools.partial(_final_kernel, alpha=alpha),
        out_shape=(jax.ShapeDtypeStruct((n, d), F32), jax.ShapeDtypeStruct((n, d), BF16)),
        grid=(n // tm,),
        in_specs=[row(d), row(d)] + [row(BRANCH_W)] * N_BRANCH + [row(N_BRANCH * BRANCH_W),
                  _layer_spec(wg, l), _layer_spec(wb, l), _layer_spec(wo, l), full(ln_g), full(ln_b)],
        out_specs=(row(d), row(d)),
        scratch_shapes=[pltpu.VMEM((tm, d), F32)],
        compiler_params=_cparams(1),
        name="merge_out_ln",
    )(h, hb, *os5, z, wg, wb, wo, ln_g, ln_b)


ROPE_GROUPS = (("a_q", N_HEADS, HEAD_DIM, ROT_64), ("a_k", N_HEADS, HEAD_DIM, ROT_64),
               ("i_q", IDX_HEADS, IDX_DIM, ROT_32), ("i_k", 1, MXU_N, ROT_32),
               ("b_q", N_HEADS, HEAD_DIM, ROT_64), ("b_k", N_HEADS, HEAD_DIM, ROT_64),
               ("c_q", 2 * N_HEADS, DIFF_DIM, ROT_32), ("c_k", 2 * N_HEADS, DIFF_DIM, ROT_32),
               ("d_kr", 1, MXU_N, MLA_ROPE))
PLAIN_COLS = ("d_cq", "d_ckv", "e_q") + tuple(("z", j) for j in range(N_BRANCH))
VALUE_COLS = ("a_v", "b_v", "c_v")
GATE_COLS = tuple(("g", j) for j in range(OFF["g"][1] // MXU_N))


def _window_start(col):
    name, j = col if isinstance(col, tuple) else (col, 0)
    return OFF[name][0] + j * MXU_N


def _weight_prep_kernel(offs_ref, wt_ref, o_ref):
    o_ref[...] = wt_ref[...].T.astype(o_ref.dtype)


def _weight_windows(wt, cols, name):
    depth, n, d = wt.shape
    starts = [_window_start(c) for c in cols]
    assert all(st % SUBLANES == 0 and st + MXU_N <= n for st in starts)
    grid_spec = pltpu.PrefetchScalarGridSpec(
        num_scalar_prefetch=1,
        grid=(depth, len(cols)),
        in_specs=[pl.BlockSpec((None, pl.Element(MXU_N), pl.Element(d)),
                               lambda l, j, offs: (l, pl.multiple_of(offs[j], SUBLANES), 0))],
        out_specs=pl.BlockSpec((None, d, MXU_N), lambda l, j, offs: (l, 0, j)),
    )
    return pl.pallas_call(
        _weight_prep_kernel,
        out_shape=jax.ShapeDtypeStruct((depth, d, MXU_N * len(cols)), BF16),
        grid_spec=grid_spec,
        compiler_params=_cparams(2),
        name=name,
    )(jnp.asarray(np.asarray(starts, np.int32)), wt)


def _weight_prep(w_in):
    wt = jnp.swapaxes(w_in, 1, 2)
    return (_weight_windows(wt, PLAIN_COLS, "wprep_plain"), _weight_windows(wt, VALUE_COLS, "wprep_value"),
            _weight_windows(wt, [name for name, *_ in ROPE_GROUPS], "wprep_rope"),
            _weight_windows(wt, GATE_COLS, "wprep_gate"))


def _rope_tables(seq, rot_dim):
    pos = jnp.arange(seq, dtype=F32)
    inv = ROPE_THETA ** (-jnp.arange(0, rot_dim, 2, dtype=F32) / rot_dim)
    ang = pos[:, None] * inv[None, :]
    return jnp.cos(ang), jnp.sin(ang)


def _rope_cs(t, nh, hd, r):
    cos, sin = _rope_tables(t, r)
    c = jnp.concatenate([cos, cos, jnp.ones((t, hd - r), F32)], axis=1)
    s = jnp.concatenate([-sin, sin, jnp.zeros((t, hd - r), F32)], axis=1)
    return jnp.tile(c, (1, nh)), jnp.tile(s, (1, nh))


def kernel(x, mem, ln0_g, ln0_b, w_in, mla_q_norm, w_uq, mla_kv_norm, w_ukv, diff_lam, diff_norm,
           w_mem_kv, w_branch, w_out, ln_g, ln_b):
    b, t, d = x.shape
    depth = w_in.shape[0]
    alpha = (2 * depth) ** 0.25
    assert t % 512 == 0 and d == 1024

    w_plain, w_vt, w_rope, wg = _weight_prep(w_in)
    plain_widths = (BRANCH_W,) * 3 + (N_BRANCH * BRANCH_W,)
    rope_heads = tuple((hd, r // 2) for _, _, hd, r in ROPE_GROUPS)
    patterns = sorted(set((nh, hd, r) for _, nh, hd, r in ROPE_GROUPS))
    rope_tables = tuple(patterns.index((nh, hd, r)) for _, nh, hd, r in ROPE_GROUPS)
    cs = [_rope_cs(t, nh, hd, r) for nh, hd, r in patterns]
    ctab = jnp.stack([c for c, _ in cs])
    stab = jnp.stack([s for _, s in cs])

    uq = w_uq.reshape(depth, Q_LORA, N_HEADS, MLA_NOPE + MLA_ROPE)
    qn_w, qr_w = uq[..., :MLA_NOPE], uq[..., MLA_NOPE:]
    pad32 = jnp.zeros((depth, Q_LORA, N_HEADS, LANES - MLA_NOPE - MLA_ROPE), w_uq.dtype)
    hw = N_HEADS * LANES
    wq = jnp.concatenate([qn_w, qr_w, pad32], axis=-1).reshape(depth, Q_LORA, hw).astype(BF16)
    half = MLA_ROPE // 2
    wq_rot = jnp.concatenate([jnp.zeros_like(qn_w), -qr_w[..., half:], qr_w[..., :half], pad32],
                             axis=-1).reshape(depth, Q_LORA, hw).astype(BF16)
    cos_m, sin_m = _rope_tables(t, MLA_ROPE)
    one = lambda n: jnp.ones((t, n), F32)
    zer = lambda n: jnp.zeros((t, n), F32)
    qs = (MLA_NOPE + MLA_ROPE) ** -0.5 * LOG2E
    ct_q = qs * jnp.tile(jnp.concatenate([one(MLA_NOPE), cos_m, cos_m, one(LANES - MLA_NOPE - MLA_ROPE)], axis=1), (1, N_HEADS))
    st_q = qs * jnp.tile(jnp.concatenate([zer(MLA_NOPE), sin_m, sin_m, zer(LANES - MLA_NOPE - MLA_ROPE)], axis=1), (1, N_HEADS))
    ukv = w_ukv.reshape(depth, KV_LORA, N_HEADS, MLA_NOPE + MLA_V)
    wk = jnp.concatenate([ukv[..., :MLA_NOPE], jnp.zeros((depth, KV_LORA, N_HEADS, LANES - MLA_NOPE), w_ukv.dtype)],
                         axis=-1).reshape(depth, KV_LORA, hw).astype(BF16)
    wvt = ukv[..., MLA_NOPE:].reshape(depth, KV_LORA, N_HEADS * MLA_V).astype(BF16)
    place = np.zeros((MXU_N, hw), np.float32)
    for hh in range(N_HEADS):
        for j in range(MLA_ROPE):
            place[j, hh * LANES + MLA_NOPE + j] = 1.0
    place = jnp.asarray(place, BF16)

    wb = w_branch.astype(BF16)
    wo = w_out.astype(BF16)
    wmem = w_mem_kv.astype(BF16)
    norm_t = jnp.broadcast_to(diff_norm.astype(F32)[:, :, None], (depth, HEAD_DIM, TQ))

    h, hb = _layer_norm0(x.reshape(b * t, d), ln0_g, ln0_b)
    for l in range(depth):
        hb3 = hb.reshape(b, t, d)
        avt, bvt, cvt, dcq, ckv_iw, eq, z = _proj_plain(hb3, w_plain, w_vt, plain_widths, l)
        aq, ak, iq, ik, bq, bk, cq, ck, kr = _proj_rope(hb3, w_rope, ctab, stab, rope_heads, rope_tables, l)

        o_a = _dsa(aq, ak, avt, iq, ik, ik)
        o_b = _moba(bq, bk, bvt, _kbar(bk))
        lam_init = 0.8 - 0.6 * math.exp(-0.3 * l)
        misc = jnp.full((SUBLANES, LANES), lam_init, F32)
        o_c = _diff(cq, ck, cvt, diff_lam[l].astype(F32), norm_t[l], misc)
        qm, km, vmt = _mla_prep(dcq, ckv_iw, kr, mla_q_norm[l].reshape(1, Q_LORA), mla_kv_norm[l].reshape(1, KV_LORA),
                                wq, wq_rot, wk, wvt, place, ct_q, st_q, l)
        o_d = _mla(qm, km, vmt)
        o_e = _mem_attn(eq, *_mem_kv(mem, wmem, l))

        os5 = [o.reshape(b * t, BRANCH_W) for o in (o_a, o_b, o_c, o_d, o_e)]
        h, hb = _final(h, hb, os5, z.reshape(b * t, N_BRANCH * BRANCH_W), wg, wb, wo,
                       ln_g[l].reshape(1, d), ln_b[l].reshape(1, d), alpha, l)
    return h.reshape(b, t, d)
```

```python
import functools
import math

import numpy as np
import jax
import jax.numpy as jnp
from jax import lax
from jax.experimental import pallas as pl
from jax.experimental.pallas import tpu as pltpu

F32 = jnp.float32
BF16 = jnp.bfloat16
I32 = jnp.int32
I16 = jnp.int16

N_HEADS = 4
HEAD_DIM = 64
BRANCH_W = N_HEADS * HEAD_DIM
N_BRANCH = 5
ROPE_THETA = 500000.0
ROT_64 = 16
ROT_32 = 8
IDX_HEADS = 8
IDX_DIM = 32
TOPK_MAX = 256
MOBA_BLOCK = 256
MOBA_TOPK = 3
DIFF_DIM = 32
Q_LORA = 256
KV_LORA = 128
MLA_NOPE = 64
MLA_ROPE = 32
MLA_V = 64
LN_EPS = 1e-5
RMS_EPS = 1e-6

IN_LAYOUT = (
    ("a_q", BRANCH_W), ("a_k", BRANCH_W), ("a_v", BRANCH_W),
    ("i_q", IDX_HEADS * IDX_DIM), ("i_k", IDX_DIM), ("i_w", IDX_HEADS),
    ("b_q", BRANCH_W), ("b_k", BRANCH_W), ("b_v", BRANCH_W),
    ("c_q", BRANCH_W), ("c_k", BRANCH_W), ("c_v", BRANCH_W),
    ("d_cq", Q_LORA), ("d_ckv", KV_LORA), ("d_kr", MLA_ROPE),
    ("e_q", BRANCH_W),
    ("z", N_BRANCH * BRANCH_W),
    ("g", N_BRANCH * 1024),
)

SUBLANES = 8
LANES = 128
MXU_N = 256
TQ = 512
CK = 256
VROWS = HEAD_DIM + 16
FLASH_UNROLL = 4
NEG = -1e30
LOG2E = math.log2(math.e)
INT_MIN = np.int32(-2 ** 31)
HALF16 = 1 << 15
VMEM_LIMIT = 56 * 1024 * 1024


def _offsets():
    off, out = 0, {}
    for name, size in IN_LAYOUT:
        out[name] = (off, size)
        off += size
    return out


OFF = _offsets()


def _nt_dot(a, b):
    return lax.dot_general(a, b, (((1,), (1,)), ((), ())), preferred_element_type=F32)


def _tn_dot(w, x):
    return lax.dot_general(w, x, (((0,), (1,)), ((), ())), preferred_element_type=F32)


def _fold_rows(w, rows=SUBLANES):
    xs = [w[r:r + rows, :] for r in range(0, w.shape[0], rows)]
    while len(xs) > 1:
        xs = [xs[j] + xs[j + 1] for j in range(0, len(xs) - 1, 2)] + ([xs[-1]] if len(xs) % 2 else [])
    return xs[0]


def _masked_qt(q, shift, n, qt_ref):
    qt = q.T
    dim = lax.broadcasted_iota(I32, (LANES, qt.shape[1]), 0)
    for j in range(n):
        half = (j << shift) // LANES
        rows = qt[half * LANES:(half + 1) * LANES, :]
        qt_ref[j] = jnp.where(((dim + half * LANES) >> shift) == j, rows, 0.0).astype(BF16)


def _half(kc, j, shift):
    half = (j << shift) // LANES
    return kc[:, half * LANES:(half + 1) * LANES]


def _cparams(n_axes):
    return pltpu.CompilerParams(dimension_semantics=("arbitrary",) * n_axes,
                                vmem_limit_bytes=VMEM_LIMIT)


def _layer_spec(a, l):
    return pl.BlockSpec((None,) + a.shape[1:], lambda *_: (l,) + (0,) * (a.ndim - 1))


def _softmax_step(s_t, m_tile, vt_h, m_ref, acc_ref):
    m_old = m_ref[...]
    m_new = jnp.maximum(m_old, m_tile)
    alpha = jnp.exp2(m_old - m_new)
    p = jnp.exp2(s_t - m_new)
    acc_ref[...] = alpha * acc_ref[...] + jnp.dot(vt_h, p.astype(BF16), preferred_element_type=F32)
    m_ref[...] = m_new


def _softmax_init(m_ref, acc_ref):
    m_ref[...] = jnp.full(m_ref.shape, NEG, F32)
    acc_ref[...] = jnp.zeros(acc_ref.shape, F32)


def _softmax_out(acc_ref):
    return acc_ref[:HEAD_DIM, :] / acc_ref[HEAD_DIM:HEAD_DIM + 1, :]


def _store_vt(o_ref, vt):
    ones = jnp.ones((VROWS - HEAD_DIM, CK), o_ref.dtype)
    for j in range(o_ref.shape[0]):
        for h in range(N_HEADS):
            o_ref[j, h * VROWS:h * VROWS + HEAD_DIM, :] = (
                vt[h * HEAD_DIM:(h + 1) * HEAD_DIM, j * CK:(j + 1) * CK].astype(o_ref.dtype))
            o_ref[j, h * VROWS + HEAD_DIM:(h + 1) * VROWS, :] = ones


def _flash_loop(n_full, qk_all, mask, vt_rows, state, prep=None, causal_tail=True):
    s_ref, mx_ref, m_ref, acc_ref = state
    n_state = m_ref.shape[0]
    for j in range(n_state):
        _softmax_init(m_ref.at[j], acc_ref.at[j])

    def lanes_of(d):
        return slice(CK, TQ) if d == 1 else slice(None)

    def park(c, slot, d=None):
        lanes = lanes_of(d)
        ctx = c if prep is None else prep(c, lanes)
        for j, s in enumerate(qk_all(c, lanes)):
            if mask is not None:
                s = mask(ctx, j, s, lanes)
            if d is not None and causal_tail:
                s = jnp.where(_causal(d), s, NEG)
            s_ref[slot, j, :, lanes] = s
            mx_ref[slot, j, :, lanes] = jnp.max(s, axis=0, keepdims=True)

    def consume(c, slot, d=None):
        lanes = lanes_of(d)
        for j in range(n_state):
            _softmax_step(s_ref[slot, j, :, lanes], mx_ref[slot, j, :, lanes], vt_rows(c, j),
                          m_ref.at[j, :, lanes], acc_ref.at[j, :, lanes])

    def pair(c):
        park(c + 1, 1)
        consume(c, 0)
        park(c + 2, 0)
        consume(c + 1, 1)

    def body(g, carry):
        for u in range(0, FLASH_UNROLL, 2):
            pair(FLASH_UNROLL * g + u)
        return carry

    @pl.when(n_full == 0)
    def _():
        park(0, 0, d=0)
        park(1, 1, d=1)
        consume(0, 0)
        consume(1, 1, d=1)

    @pl.when(n_full > 0)
    def _():
        park(0, 0)
        n_loop = n_full - 2
        n_group = lax.shift_right_logical(n_loop, FLASH_UNROLL.bit_length() - 1)
        lax.fori_loop(0, n_group, body, 0)
        c0 = FLASH_UNROLL * n_group
        for u in range(FLASH_UNROLL // 2 - 1):
            @pl.when(n_loop - c0 >= 2 * (u + 1))
            def _(u=u):
                pair(c0 + 2 * u)
        c = n_loop
        park(c + 1, 1)
        consume(c, 0)
        park(c + 2, 0, d=0)
        consume(c + 1, 1)
        park(c + 3, 1, d=1)
        consume(c + 2, 0)
        consume(c + 3, 1, d=1)


def _causal(d):
    shape = (CK, TQ - d * CK)
    return lax.broadcasted_iota(I32, shape, 0) <= lax.broadcasted_iota(I32, shape, 1)


def _attn_scratch(n_state):
    return [pltpu.VMEM((2, n_state, CK, TQ), F32), pltpu.VMEM((2, n_state, 1, TQ), F32),
            pltpu.VMEM((n_state, 1, TQ), F32), pltpu.VMEM((n_state, VROWS, TQ), F32),
            pltpu.VMEM((BRANCH_W, TQ), F32)]


def _kv_specs(t, w):
    kspec = pl.BlockSpec((None, t, w), lambda bb, i: (bb, 0, 0))
    vspec = pl.BlockSpec((None, t // CK, N_HEADS * VROWS, CK), lambda bb, i: (bb, 0, 0, 0))
    return kspec, vspec


def _ln_kernel(x_ref, g_ref, b_ref, h_ref, hb_ref):
    x = x_ref[...]
    mu = jnp.mean(x, axis=1, keepdims=True)
    xc = x - mu
    var = jnp.mean(xc * xc, axis=1, keepdims=True)
    y = xc * lax.rsqrt(var + LN_EPS) * g_ref[...] + b_ref[...]
    h_ref[...] = y
    hb_ref[...] = y.astype(BF16)


def _layer_norm0(x2, g, b):
    n, d = x2.shape
    tm = 512
    row = pl.BlockSpec((tm, d), lambda i: (i, 0))
    vec = pl.BlockSpec((1, d), lambda i: (0, 0))
    return pl.pallas_call(
        _ln_kernel,
        out_shape=(jax.ShapeDtypeStruct((n, d), F32), jax.ShapeDtypeStruct((n, d), BF16)),
        grid=(n // tm,),
        in_specs=[row, vec, vec],
        out_specs=(row, row),
        compiler_params=_cparams(1),
        name="ln0",
    )(x2, g.reshape(1, d), b.reshape(1, d))


def _proj_plain_kernel(x_ref, w_ref, wt_ref, *out_refs, n_t):
    for g, o_ref in enumerate(out_refs[:n_t]):
        _store_vt(o_ref, _tn_dot(wt_ref[:, g * BRANCH_W:(g + 1) * BRANCH_W], x_ref[...]))
    off = 0
    for o_ref in out_refs[n_t:]:
        wd = o_ref.shape[-1]
        for j in range(0, wd, MXU_N):
            acc = jnp.dot(x_ref[...], w_ref[:, off + j:off + j + MXU_N], preferred_element_type=F32)
            o_ref[:, j:j + MXU_N] = acc.astype(o_ref.dtype)
        off += wd


def _proj_plain(hb3, w, wt, widths, l):
    b, t, d = hb3.shape
    tm = 512
    n_t = wt.shape[-1] // BRANCH_W
    shapes = [jax.ShapeDtypeStruct((b, t // CK, N_HEADS * VROWS, CK), BF16)] * n_t
    specs = [pl.BlockSpec((None, tm // CK, N_HEADS * VROWS, CK), lambda i, bb: (bb, i, 0, 0))] * n_t
    shapes += [jax.ShapeDtypeStruct((b, t, wd), BF16) for wd in widths]
    specs += [pl.BlockSpec((None, tm, wd), lambda i, bb: (bb, i, 0)) for wd in widths]
    return pl.pallas_call(
        functools.partial(_proj_plain_kernel, n_t=n_t),
        out_shape=tuple(shapes),
        grid=(t // tm, b),
        in_specs=[pl.BlockSpec((None, tm, d), lambda i, bb: (bb, i, 0)),
                  _layer_spec(w, l), _layer_spec(wt, l)],
        out_specs=tuple(specs),
        compiler_params=_cparams(2),
        name="proj_plain",
    )(hb3, w, wt)


def _proj_rope_kernel(x_ref, w_ref, c_ref, s_ref, *out_refs, heads, tables):
    lane = lax.broadcasted_iota(I32, (x_ref.shape[0], MXU_N), 1)
    for g, o_ref in enumerate(out_refs):
        hd, half = heads[g]
        sl = slice(g * MXU_N, (g + 1) * MXU_N)
        acc = jnp.dot(x_ref[...], w_ref[:, sl], preferred_element_type=F32)
        partner = jnp.where((lane & (hd - 1)) < half,
                            pltpu.roll(acc, MXU_N - half, 1), pltpu.roll(acc, half, 1))
        o_ref[...] = (acc * c_ref[tables[g]] + partner * s_ref[tables[g]]).astype(o_ref.dtype)


def _proj_rope(hb3, w, ctab, stab, heads, tables, l):
    b, t, d = hb3.shape
    tm = 512
    assert w.shape[-1] == MXU_N * len(heads)
    tspec = pl.BlockSpec((ctab.shape[0], tm, MXU_N), lambda i, bb: (0, i, 0))
    ospec = pl.BlockSpec((None, tm, MXU_N), lambda i, bb: (bb, i, 0))
    return pl.pallas_call(
        functools.partial(_proj_rope_kernel, heads=heads, tables=tables),
        out_shape=(jax.ShapeDtypeStruct((b, t, MXU_N), BF16),) * len(heads),
        grid=(t // tm, b),
        in_specs=[pl.BlockSpec((None, tm, d), lambda i, bb: (bb, i, 0)),
                  _layer_spec(w, l), tspec, tspec],
        out_specs=(ospec,) * len(heads),
        compiler_params=_cparams(2),
        name="proj_rope",
    )(hb3, w, ctab, stab)


def _dsa_kernel(aq_ref, ak_ref, avt_ref, iq_ref, ik_ref, iw_ref, pick_ref, tri_ref, o_ref,
                keys_ref, hi_ref, lo_ref, bk_ref, iqt_ref, aqt_ref, wt_ref, thr_ref, s_ref, mx_ref, m_ref, acc_ref, ot_ref,
                *, topk, idx_scale):
    i = pl.program_id(1)
    n_full = 2 * i

    iqt = iq_ref[...].astype(F32).T
    for hh in range(IDX_HEADS):
        iqt_ref[hh] = iqt[hh * IDX_DIM:(hh + 1) * IDX_DIM, :].astype(BF16)
    _masked_qt(aq_ref[...].astype(F32) * (HEAD_DIM ** -0.5 * LOG2E), 6, N_HEADS, aqt_ref)
    wt_ref[...] = _nt_dot(pick_ref[...], iw_ref[...]) * idx_scale

    def lanes_of(d):
        return slice(CK, TQ) if d == 1 else slice(None)

    def logits(c, d):
        kc = ik_ref[pl.ds(pl.multiple_of(c * CK, CK), CK), :]
        return [jnp.dot(kc[:, :IDX_DIM], iqt_ref[hh, :, lanes_of(d)], preferred_element_type=F32)
                for hh in range(IDX_HEADS)]

    def put_keys(c, key, lanes):
        keys_ref[c, :, lanes] = key
        hi_ref[c, :, lanes] = (key >> 16).astype(I16)
        lo_ref[c, :, lanes] = ((key & 0xFFFF) - HALF16).astype(I16)

    def score_chunk(c, lg, d):
        lanes = lanes_of(d)
        sc = jnp.zeros(lg[0].shape, F32)
        for hh in range(IDX_HEADS):
            sc = sc + jnp.maximum(lg[hh], 0.0) * wt_ref[hh:hh + 1, lanes]
        bits = pltpu.bitcast(sc, I32)
        key = jnp.where(bits < 0, INT_MIN - bits, bits)
        put_keys(c, key if d is None else jnp.where(_causal(d), key, INT_MIN), lanes)
        if d == 1:
            put_keys(c, jnp.full((CK, CK), INT_MIN, I32), slice(0, CK))

    def score_pair(c, d0, d1):
        lg0, lg1 = logits(c, d0), logits(c + 1, d1)
        score_chunk(c, lg0, d0)
        score_chunk(c + 1, lg1, d1)

    def score_body(p, carry):
        score_pair(2 * p, None, None)
        return carry

    lax.fori_loop(0, i, score_body, 0)
    score_pair(n_full, 0, 1)

    def pair_loop(body, init, last=None):
        def pair(p, carry):
            return body(2 * p + 1, body(2 * p, carry))
        carry = body(n_full, lax.fori_loop(0, i, pair, init))
        return (last or body)(n_full + 1, carry)

    def count16(pred, also=None):
        def hits(c, lanes):
            hit = jnp.where(pred(c, lanes), jnp.int16(1), jnp.int16(0))
            if also is not None:
                hit = jnp.where(also(c, lanes), hit, jnp.int16(0))
            return _fold_rows(hit, 2 * SUBLANES)

        def body(c, part):
            return part + hits(c, slice(None))

        def last(c, part):
            return jnp.concatenate([part[:, :CK], part[:, CK:] + hits(c, slice(CK, TQ))], axis=1)

        part = pair_loop(body, jnp.zeros((2 * SUBLANES, TQ), I16), last)
        return jnp.sum(part.astype(F32), axis=0, keepdims=True)

    def search16(ref, need):
        def bit_body(bi, t_u):
            c_u = t_u | jnp.left_shift(jnp.int32(1), 15 - bi)
            ck = (c_u - HALF16).astype(I16)
            cnt = count16(lambda c, lanes: ref[c, :, lanes] >= ck[:, lanes])
            return jnp.where(cnt >= need, c_u, t_u)
        return lax.fori_loop(0, 16, bit_body, jnp.zeros((1, TQ), I32))

    hi_u = search16(hi_ref, float(topk))
    thr_hi = (hi_u - HALF16).astype(I16)
    n_above = count16(lambda c, lanes: hi_ref[c, :, lanes] > thr_hi[:, lanes])

    def bucket_body(c, carry):
        bk_ref[c] = jnp.where(hi_ref[c] == thr_hi, lo_ref[c], jnp.int16(-HALF16))
        return carry

    pair_loop(bucket_body, 0)
    lo_u = search16(bk_ref, float(topk) - n_above)
    thr_lo = (lo_u - HALF16).astype(I16)
    thr = ((hi_u - HALF16) << 16) | lo_u

    n_gt = n_above + count16(lambda c, lanes: bk_ref[c, :, lanes] > thr_lo[:, lanes])
    n_eq = count16(lambda c, lanes: lo_ref[c, :, lanes] == thr_lo[:, lanes],
                   also=lambda c, lanes: hi_ref[c, :, lanes] == thr_hi[:, lanes])
    need = float(topk) - n_gt
    amb = jnp.logical_and(n_eq > need, thr > INT_MIN)

    def drop_ties(lanes):
        thr_b, need_b, amb_b = thr[:, lanes], need[:, lanes], amb[:, lanes]

        @pl.when(jnp.max(jnp.where(amb_b, 1.0, 0.0)) > 0.5)
        def _():
            def drop_body(c, seen):
                k = keys_ref[c, :, lanes]
                eq = k == thr_b
                eqf = jnp.where(eq, 1.0, 0.0)
                rank = jnp.dot(tri_ref[...], eqf.astype(BF16), preferred_element_type=F32) + seen
                drop = jnp.logical_and(jnp.logical_and(eq, rank > need_b), amb_b)
                keys_ref[c, :, lanes] = jnp.where(drop, INT_MIN, k)
                return seen + jnp.sum(eqf, axis=0, keepdims=True)

            pair_loop(drop_body, jnp.zeros((1, LANES), F32))

    for blk in range(TQ // LANES):
        drop_ties(slice(blk * LANES, (blk + 1) * LANES))

    thr_ref[...] = jnp.maximum(thr, INT_MIN + 1)

    def qk_all(c, lanes):
        kc = ak_ref[pl.ds(pl.multiple_of(c * CK, CK), CK), :]
        return [jnp.dot(_half(kc, h, 6), aqt_ref[h, :, lanes], preferred_element_type=F32) for h in range(N_HEADS)]

    _flash_loop(n_full, qk_all,
                lambda keep, h, s, lanes: jnp.where(keep, s, NEG),
                lambda c, h: avt_ref[c, h * VROWS:(h + 1) * VROWS, :],
                (s_ref, mx_ref, m_ref, acc_ref),
                prep=lambda c, lanes: keys_ref[c, :, lanes] >= thr_ref[:, lanes], causal_tail=False)
    for h in range(N_HEADS):
        ot_ref[h * HEAD_DIM:(h + 1) * HEAD_DIM, :] = _softmax_out(acc_ref.at[h])
    o_ref[...] = ot_ref[...].T.astype(o_ref.dtype)


def _dsa(aq, ak, avt, iq, ik, iw):
    b, t, _ = aq.shape
    topk = min(TOPK_MAX, t // 4)
    qspec = pl.BlockSpec((None, TQ, BRANCH_W), lambda bb, i: (bb, i, 0))
    kspec, vspec = _kv_specs(t, BRANCH_W)
    pick = np.zeros((2 * SUBLANES, MXU_N), np.float32)
    for hh in range(IDX_HEADS):
        pick[hh, IDX_DIM + hh] = 1.0
    pick = jnp.asarray(pick, BF16)
    tri = jnp.asarray(np.tril(np.ones((CK, CK), np.float32)), BF16)
    kern = functools.partial(_dsa_kernel, topk=topk, idx_scale=(IDX_HEADS * IDX_DIM) ** -0.5)
    return pl.pallas_call(
        kern,
        out_shape=jax.ShapeDtypeStruct((b, t, BRANCH_W), BF16),
        grid=(b, t // TQ),
        in_specs=[qspec, kspec, vspec, qspec, kspec, qspec,
                  pl.BlockSpec(pick.shape, lambda bb, i: (0, 0)), pl.BlockSpec(tri.shape, lambda bb, i: (0, 0))],
        out_specs=qspec,
        scratch_shapes=[
            pltpu.VMEM((t // CK, CK, TQ), I32),
            pltpu.VMEM((t // CK, CK, TQ), I16),
            pltpu.VMEM((t // CK, CK, TQ), I16),
            pltpu.VMEM((t // CK, CK, TQ), I16),
            pltpu.VMEM((IDX_HEADS, IDX_DIM, TQ), BF16),
            pltpu.VMEM((N_HEADS, LANES, TQ), BF16),
            pltpu.VMEM((2 * SUBLANES, TQ), F32),
            pltpu.VMEM((1, TQ), I32),
        ] + _attn_scratch(N_HEADS),
        compiler_params=_cparams(2),
        name="dsa",
    )(aq, ak, avt, iq, ik, iw, pick, tri)


def _kbar_kernel(k_ref, o_ref):
    o_ref[...] = jnp.zeros(o_ref.shape, o_ref.dtype)
    nb = k_ref.shape[0] // MOBA_BLOCK
    for n in range(nb):
        blk = k_ref[n * MOBA_BLOCK:(n + 1) * MOBA_BLOCK, :].astype(F32)
        o_ref[n:n + 1, :] = jnp.mean(blk, axis=0, keepdims=True).astype(o_ref.dtype)


def _kbar(bk):
    b, t, w = bk.shape
    nbp = max(2 * SUBLANES, t // MOBA_BLOCK)
    return pl.pallas_call(
        _kbar_kernel,
        out_shape=jax.ShapeDtypeStruct((b, nbp, w), BF16),
        grid=(b,),
        in_specs=[pl.BlockSpec((None, t, w), lambda bb: (bb, 0, 0))],
        out_specs=pl.BlockSpec((None, nbp, w), lambda bb: (bb, 0, 0)),
        compiler_params=_cparams(1),
        name="moba_kbar",
    )(bk)


def _moba_kernel(q_ref, k_ref, vt_ref, kbar_ref, o_ref, qt_ref, bias_ref, s_ref, mx_ref, m_ref, acc_ref, ot_ref):
    i = pl.program_id(1)
    nbp = kbar_ref.shape[0]
    blk = lax.broadcasted_iota(I32, (nbp, TQ), 0)
    blk_f = blk.astype(F32)
    own = 2 * i + (lax.broadcasted_iota(I32, (nbp, TQ), 1) >> (MOBA_BLOCK.bit_length() - 1))
    _masked_qt(q_ref[...].astype(F32) * (HEAD_DIM ** -0.5 * LOG2E), 6, N_HEADS, qt_ref)

    for h in range(N_HEADS):
        g = jnp.where(blk < own, jnp.dot(_half(kbar_ref[...], h, 6), qt_ref[h], preferred_element_type=F32), NEG)
        bias = jnp.full((nbp, TQ), NEG, F32)
        for _ in range(MOBA_TOPK):
            mx = jnp.max(g, axis=0, keepdims=True)
            first = jnp.min(jnp.where(g == mx, blk_f, 1e9), axis=0, keepdims=True)
            pick = jnp.logical_and(blk_f == first, mx > 0.5 * NEG)
            bias = jnp.where(pick, 0.0, bias)
            g = jnp.where(pick, NEG, g)
        bias_ref[h] = jnp.where(blk == own, 0.0, bias)

    def qk_all(c, lanes):
        kc = k_ref[pl.ds(pl.multiple_of(c * CK, CK), CK), :]
        return [jnp.dot(_half(kc, h, 6), qt_ref[h, :, lanes], preferred_element_type=F32) for h in range(N_HEADS)]

    _flash_loop(2 * i, qk_all, lambda c, h, s, lanes: s + bias_ref[h, pl.ds(c, 1), lanes],
                lambda c, h: vt_ref[c, h * VROWS:(h + 1) * VROWS, :], (s_ref, mx_ref, m_ref, acc_ref))
    for h in range(N_HEADS):
        ot_ref[h * HEAD_DIM:(h + 1) * HEAD_DIM, :] = _softmax_out(acc_ref.at[h])
    o_ref[...] = ot_ref[...].T.astype(o_ref.dtype)


def _moba(bq, bk, bvt, kbar):
    b, t, w = bq.shape
    assert TQ == 2 * MOBA_BLOCK and CK == MOBA_BLOCK and t % TQ == 0
    nbp = kbar.shape[1]
    qspec = pl.BlockSpec((None, TQ, w), lambda bb, i: (bb, i, 0))
    kspec, vspec = _kv_specs(t, w)
    return pl.pallas_call(
        _moba_kernel,
        out_shape=jax.ShapeDtypeStruct((b, t, w), BF16),
        grid=(b, t // TQ),
        in_specs=[qspec, kspec, vspec, pl.BlockSpec((None, nbp, w), lambda bb, i: (bb, 0, 0))],
        out_specs=qspec,
        scratch_shapes=[pltpu.VMEM((N_HEADS, LANES, TQ), BF16), pltpu.VMEM((N_HEADS, nbp, TQ), F32)]
        + _attn_scratch(N_HEADS),
        compiler_params=_cparams(2),
        name="moba",
    )(bq, bk, bvt, kbar)


def _diff_kernel(q_ref, k_ref, vt_ref, lam_ref, norm_ref, misc_ref, o_ref,
                 qt_ref, s_ref, mx_ref, m_ref, acc_ref, ot_ref):
    i = pl.program_id(1)
    _masked_qt(q_ref[...].astype(F32) * (DIFF_DIM ** -0.5 * LOG2E), 5, 2 * N_HEADS, qt_ref)

    dl = lam_ref[...]
    lam_init = misc_ref[0:1, 0:1]
    lam = (jnp.exp(jnp.sum(dl[0:1, :] * dl[1:2, :], axis=1, keepdims=True))
           - jnp.exp(jnp.sum(dl[2:3, :] * dl[3:4, :], axis=1, keepdims=True)) + lam_init)

    def qk_all(c, lanes):
        kc = k_ref[pl.ds(pl.multiple_of(c * CK, CK), CK), :]
        return [jnp.dot(_half(kc, j, 5), qt_ref[j, :, lanes], preferred_element_type=F32) for j in range(2 * N_HEADS)]

    _flash_loop(2 * i, qk_all, None,
                lambda c, j: vt_ref[c, (j // 2) * VROWS:(j // 2 + 1) * VROWS, :],
                (s_ref, mx_ref, m_ref, acc_ref))

    post = norm_ref[...] * (1.0 - lam_init)
    for h in range(N_HEADS):
        o_h = _softmax_out(acc_ref.at[2 * h]) - lam * _softmax_out(acc_ref.at[2 * h + 1])
        ms = jnp.mean(o_h * o_h, axis=0, keepdims=True)
        ot_ref[h * HEAD_DIM:(h + 1) * HEAD_DIM, :] = o_h * lax.rsqrt(ms + RMS_EPS) * post
    o_ref[...] = ot_ref[...].T.astype(o_ref.dtype)


def _diff(cq, ck, cvt, lam, norm, misc):
    b, t, w = cq.shape
    qspec = pl.BlockSpec((None, TQ, w), lambda bb, i: (bb, i, 0))
    kspec, vspec = _kv_specs(t, w)
    full = lambda a: pl.BlockSpec(a.shape, lambda bb, i: (0,) * a.ndim)
    return pl.pallas_call(
        _diff_kernel,
        out_shape=jax.ShapeDtypeStruct((b, t, w), BF16),
        grid=(b, t // TQ),
        in_specs=[qspec, kspec, vspec, full(lam), full(norm), full(misc)],
        out_specs=qspec,
        scratch_shapes=[pltpu.VMEM((2 * N_HEADS, LANES, TQ), BF16)] + _attn_scratch(2 * N_HEADS),
        compiler_params=_cparams(2),
        name="diff",
    )(cq, ck, cvt, lam, norm, misc)


def _mla_prep_kernel(cq_ref, ckv_ref, kr_ref, qn_ref, kvn_ref, wq_ref, wqr_ref, wk_ref, wvt_ref,
                     p_ref, ct_ref, st_ref, q_out, k_out, vt_out):
    x = cq_ref[...].astype(F32)
    xn = (x * lax.rsqrt(jnp.mean(x * x, axis=1, keepdims=True) + RMS_EPS) * qn_ref[...]).astype(BF16)
    q = (jnp.dot(xn, wq_ref[...], preferred_element_type=F32) * ct_ref[...]
         + jnp.dot(xn, wqr_ref[...], preferred_element_type=F32) * st_ref[...])
    q_out[...] = q.astype(q_out.dtype)
    c = ckv_ref[:, :KV_LORA].astype(F32)
    cn = (c * lax.rsqrt(jnp.mean(c * c, axis=1, keepdims=True) + RMS_EPS) * kvn_ref[...]).astype(BF16)
    k = (jnp.dot(cn, wk_ref[...], preferred_element_type=F32)
         + jnp.dot(kr_ref[...], p_ref[...], preferred_element_type=F32))
    k_out[...] = k.astype(k_out.dtype)
    _store_vt(vt_out, _tn_dot(wvt_ref[...], cn))


def _mla_prep(dcq, ckv, kr, qn, kvn, wq, wqr, wk, wvt, pmat, ct, st, l):
    b, t, _ = dcq.shape
    tm = 512
    hw = N_HEADS * LANES
    row = lambda w: pl.BlockSpec((None, tm, w), lambda i, bb: (bb, i, 0))
    full = lambda a: pl.BlockSpec(a.shape, lambda i, bb: (0,) * a.ndim)
    tab = pl.BlockSpec((tm, hw), lambda i, bb: (i, 0))
    return pl.pallas_call(
        _mla_prep_kernel,
        out_shape=(jax.ShapeDtypeStruct((b, t, hw), BF16), jax.ShapeDtypeStruct((b, t, hw), BF16),
                   jax.ShapeDtypeStruct((b, t // CK, N_HEADS * VROWS, CK), BF16)),
        grid=(t // tm, b),
        in_specs=[row(Q_LORA), row(MXU_N), row(MXU_N), full(qn), full(kvn), _layer_spec(wq, l), _layer_spec(wqr, l),
                  _layer_spec(wk, l), _layer_spec(wvt, l), full(pmat), tab, tab],
        out_specs=(row(hw), row(hw),
                   pl.BlockSpec((None, tm // CK, N_HEADS * VROWS, CK), lambda i, bb: (bb, i, 0, 0))),
        compiler_params=_cparams(2),
        name="mla_prep",
    )(dcq, ckv, kr, qn, kvn, wq, wqr, wk, wvt, pmat, ct, st)


def _mla_kernel(q_ref, k_ref, vt_ref, o_ref, qt_ref, s_ref, mx_ref, m_ref, acc_ref, ot_ref):
    i = pl.program_id(1)
    hs = [slice(h * LANES, (h + 1) * LANES) for h in range(N_HEADS)]
    for h in range(N_HEADS):
        qt_ref[h] = q_ref[:, hs[h]].astype(F32).T.astype(BF16)

    def qk_all(c, lanes):
        start = pl.multiple_of(c * CK, CK)
        return [jnp.dot(k_ref[pl.ds(start, CK), hs[h]], qt_ref[h, :, lanes], preferred_element_type=F32)
                for h in range(N_HEADS)]

    _flash_loop(2 * i, qk_all, None,
                lambda c, h: vt_ref[c, h * VROWS:(h + 1) * VROWS, :],
                (s_ref, mx_ref, m_ref, acc_ref))
    for h in range(N_HEADS):
        ot_ref[h * HEAD_DIM:(h + 1) * HEAD_DIM, :] = _softmax_out(acc_ref.at[h])
    o_ref[...] = ot_ref[...].T.astype(o_ref.dtype)


def _mla(qm, km, vmt):
    b, t, hw = qm.shape
    kspec, vspec = _kv_specs(t, hw)
    return pl.pallas_call(
        _mla_kernel,
        out_shape=jax.ShapeDtypeStruct((b, t, BRANCH_W), BF16),
        grid=(b, t // TQ),
        in_specs=[pl.BlockSpec((None, TQ, hw), lambda bb, i: (bb, i, 0)), kspec, vspec],
        out_specs=pl.BlockSpec((None, TQ, BRANCH_W), lambda bb, i: (bb, i, 0)),
        scratch_shapes=[pltpu.VMEM((N_HEADS, LANES, TQ), BF16)] + _attn_scratch(N_HEADS),
        compiler_params=_cparams(2),
        name="mla",
    )(qm, km, vmt)


def _mem_kv_kernel(x_ref, w_ref, k_ref, vt_ref):
    x = x_ref[...].astype(BF16)
    k_ref[...] = jnp.dot(x, w_ref[:, :BRANCH_W], preferred_element_type=F32).astype(k_ref.dtype)
    _store_vt(vt_ref, _tn_dot(w_ref[:, BRANCH_W:], x))


def _mem_kv(mem, w, l):
    b, m, d = mem.shape
    assert m % CK == 0
    return pl.pallas_call(
        _mem_kv_kernel,
        out_shape=(jax.ShapeDtypeStruct((b, m, BRANCH_W), BF16),
                   jax.ShapeDtypeStruct((b, m // CK, N_HEADS * VROWS, CK), BF16)),
        grid=(b,),
        in_specs=[pl.BlockSpec((None, m, d), lambda bb: (bb, 0, 0)), _layer_spec(w, l)],
        out_specs=(pl.BlockSpec((None, m, BRANCH_W), lambda bb: (bb, 0, 0)),
                   pl.BlockSpec((None, m // CK, N_HEADS * VROWS, CK), lambda bb: (bb, 0, 0, 0))),
        compiler_params=_cparams(1),
        name="mem_kv",
    )(mem, w)


def _mem_kernel(q_ref, k_ref, vt_ref, o_ref, qt_ref, ot_ref):
    _masked_qt(q_ref[...].astype(F32) * (HEAD_DIM ** -0.5 * LOG2E), 6, N_HEADS, qt_ref)
    s_all = [jnp.dot(_half(k_ref[...], h, 6), qt_ref[h], preferred_element_type=F32) for h in range(N_HEADS)]
    for h in range(N_HEADS):
        s_t = s_all[h]
        p = jnp.exp2(s_t - jnp.max(s_t, axis=0, keepdims=True)).astype(BF16)
        acc = jnp.dot(vt_ref[0, h * VROWS:(h + 1) * VROWS, :], p, preferred_element_type=F32)
        ot_ref[h * HEAD_DIM:(h + 1) * HEAD_DIM, :] = acc[:HEAD_DIM, :] / acc[HEAD_DIM:HEAD_DIM + 1, :]
    o_ref[...] = ot_ref[...].T.astype(o_ref.dtype)


def _mem_attn(eq, mk, mvt):
    b, t, w = eq.shape
    m = mk.shape[1]
    assert m == CK
    return pl.pallas_call(
        _mem_kernel,
        out_shape=jax.ShapeDtypeStruct((b, t, w), BF16),
        grid=(b, t // TQ),
        in_specs=[pl.BlockSpec((None, TQ, w), lambda bb, i: (bb, i, 0)),
                  pl.BlockSpec((None, m, w), lambda bb, i: (bb, 0, 0)),
                  pl.BlockSpec((None,) + mvt.shape[1:], lambda bb, i: (bb, 0, 0, 0))],
        out_specs=pl.BlockSpec((None, TQ, w), lambda bb, i: (bb, i, 0)),
        scratch_shapes=[pltpu.VMEM((N_HEADS, LANES, TQ), BF16), pltpu.VMEM((BRANCH_W, TQ), F32)],
        compiler_params=_cparams(2),
        name="mem_attn",
    )(eq, mk, mvt)


def _final_kernel(h_ref, hb_ref, oa_ref, ob_ref, oc_ref, od_ref, oe_ref, z_ref,
                  wg_ref, wb_ref, wo_ref, g_ref, b_ref, h_out, hb_out, acc_ref, *, alpha):
    d = h_ref.shape[1]
    half = h_ref.shape[0] // 2
    for n, o_ref in enumerate((oa_ref, ob_ref, oc_ref, od_ref, oe_ref)):
        for r in range(2):
            rows = slice(r * half, (r + 1) * half)
            z = z_ref[rows, n * BRANCH_W:(n + 1) * BRANCH_W].astype(F32)
            y = o_ref[rows, :].astype(F32) * (z / (1.0 + jnp.exp(-z)))
            u = jnp.dot(y.astype(BF16), wb_ref[n], preferred_element_type=F32)
            g = jnp.dot(hb_ref[rows, :], wg_ref[:, n * d:(n + 1) * d], preferred_element_type=F32)
            t = u / (1.0 + jnp.exp(-g))
            acc_ref[rows, :] = t if n == 0 else acc_ref[rows, :] + t
    for r in range(2):
        rows = slice(r * half, (r + 1) * half)
        out = jnp.dot(acc_ref[rows, :].astype(BF16), wo_ref[...], preferred_element_type=F32)
        x = alpha * h_ref[rows, :] + out
        mu = jnp.mean(x, axis=1, keepdims=True)
        xc = x - mu
        var = jnp.mean(xc * xc, axis=1, keepdims=True)
        y = xc * lax.rsqrt(var + LN_EPS) * g_ref[...] + b_ref[...]
        h_out[rows, :] = y
        hb_out[rows, :] = y.astype(BF16)


def _final(h, hb, os5, z, wg, wb, wo, ln_g, ln_b, alpha, l):
    n, d = h.shape
    tm = 512
    row = lambda w: pl.BlockSpec((tm, w), lambda i: (i, 0))
    full = lambda a: pl.BlockSpec(a.shape, lambda i: (0,) * a.ndim)
    return pl.pallas_call(
        functools.partial(_final_kernel, alpha=alpha),
        out_shape=(jax.ShapeDtypeStruct((n, d), F32), jax.ShapeDtypeStruct((n, d), BF16)),
        grid=(n // tm,),
        in_specs=[row(d), row(d)] + [row(BRANCH_W)] * N_BRANCH + [row(N_BRANCH * BRANCH_W),
                  _layer_spec(wg, l), _layer_spec(wb, l), _layer_spec(wo, l), full(ln_g), full(ln_b)],
        out_specs=(row(d), row(d)),
        scratch_shapes=[pltpu.VMEM((tm, d), F32)],
        compiler_params=_cparams(1),
        name="merge_out_ln",
    )(h, hb, *os5, z, wg, wb, wo, ln_g, ln_b)


ROPE_GROUPS = (("a_q", N_HEADS, HEAD_DIM, ROT_64), ("a_k", N_HEADS, HEAD_DIM, ROT_64),
               ("i_q", IDX_HEADS, IDX_DIM, ROT_32), ("i_k", 1, MXU_N, ROT_32),
               ("b_q", N_HEADS, HEAD_DIM, ROT_64), ("b_k", N_HEADS, HEAD_DIM, ROT_64),
               ("c_q", 2 * N_HEADS, DIFF_DIM, ROT_32), ("c_k", 2 * N_HEADS, DIFF_DIM, ROT_32),
               ("d_kr", 1, MXU_N, MLA_ROPE))
PLAIN_COLS = ("d_cq", "d_ckv", "e_q") + tuple(("z", j) for j in range(N_BRANCH))
VALUE_COLS = ("a_v", "b_v", "c_v")
GATE_COLS = tuple(("g", j) for j in range(OFF["g"][1] // MXU_N))


def _window_start(col):
    name, j = col if isinstance(col, tuple) else (col, 0)
    return OFF[name][0] + j * MXU_N


def _weight_prep_kernel(offs_ref, wt_ref, o_ref):
    o_ref[...] = wt_ref[...].T.astype(o_ref.dtype)


def _weight_windows(wt, cols, name):
    depth, n, d = wt.shape
    starts = [_window_start(c) for c in cols]
    assert all(st % SUBLANES == 0 and st + MXU_N <= n for st in starts)
    grid_spec = pltpu.PrefetchScalarGridSpec(
        num_scalar_prefetch=1,
        grid=(depth, len(cols)),
        in_specs=[pl.BlockSpec((None, pl.Element(MXU_N), pl.Element(d)),
                               lambda l, j, offs: (l, pl.multiple_of(offs[j], SUBLANES), 0))],
        out_specs=pl.BlockSpec((None, d, MXU_N), lambda l, j, offs: (l, 0, j)),
    )
    return pl.pallas_call(
        _weight_prep_kernel,
        out_shape=jax.ShapeDtypeStruct((depth, d, MXU_N * len(cols)), BF16),
        grid_spec=grid_spec,
        compiler_params=_cparams(2),
        name=name,
    )(jnp.asarray(np.asarray(starts, np.int32)), wt)


def _weight_prep(w_in):
    wt = jnp.swapaxes(w_in, 1, 2)
    return (_weight_windows(wt, PLAIN_COLS, "wprep_plain"), _weight_windows(wt, VALUE_COLS, "wprep_value"),
            _weight_windows(wt, [name for name, *_ in ROPE_GROUPS], "wprep_rope"),
            _weight_windows(wt, GATE_COLS, "wprep_gate"))


def _rope_tables(seq, rot_dim):
    pos = jnp.arange(seq, dtype=F32)
    inv = ROPE_THETA ** (-jnp.arange(0, rot_dim, 2, dtype=F32) / rot_dim)
    ang = pos[:, None] * inv[None, :]
    return jnp.cos(ang), jnp.sin(ang)


def _rope_cs(t, nh, hd, r):
    cos, sin = _rope_tables(t, r)
    c = jnp.concatenate([cos, cos, jnp.ones((t, hd - r), F32)], axis=1)
    s = jnp.concatenate([-sin, sin, jnp.zeros((t, hd - r), F32)], axis=1)
    return jnp.tile(c, (1, nh)), jnp.tile(s, (1, nh))


def kernel(x, mem, ln0_g, ln0_b, w_in, mla_q_norm, w_uq, mla_kv_norm, w_ukv, diff_lam, diff_norm,
           w_mem_kv, w_branch, w_out, ln_g, ln_b):
    b, t, d = x.shape
    depth = w_in.shape[0]
    alpha = (2 * depth) ** 0.25
    assert t % 512 == 0 and d == 1024

    w_plain, w_vt, w_rope, wg = _weight_prep(w_in)
    plain_widths = (BRANCH_W,) * 3 + (N_BRANCH * BRANCH_W,)
    rope_heads = tuple((hd, r // 2) for _, _, hd, r in ROPE_GROUPS)
    patterns = sorted(set((nh, hd, r) for _, nh, hd, r in ROPE_GROUPS))
    rope_tables = tuple(patterns.index((nh, hd, r)) for _, nh, hd, r in ROPE_GROUPS)
    cs = [_rope_cs(t, nh, hd, r) for nh, hd, r in patterns]
    ctab = jnp.stack([c for c, _ in cs])
    stab = jnp.stack([s for _, s in cs])

    uq = w_uq.reshape(depth, Q_LORA, N_HEADS, MLA_NOPE + MLA_ROPE)
    qn_w, qr_w = uq[..., :MLA_NOPE], uq[..., MLA_NOPE:]
    pad32 = jnp.zeros((depth, Q_LORA, N_HEADS, LANES - MLA_NOPE - MLA_ROPE), w_uq.dtype)
    hw = N_HEADS * LANES
    wq = jnp.concatenate([qn_w, qr_w, pad32], axis=-1).reshape(depth, Q_LORA, hw).astype(BF16)
    half = MLA_ROPE // 2
    wq_rot = jnp.concatenate([jnp.zeros_like(qn_w), -qr_w[..., half:], qr_w[..., :half], pad32],
                             axis=-1).reshape(depth, Q_LORA, hw).astype(BF16)
    cos_m, sin_m = _rope_tables(t, MLA_ROPE)
    one = lambda n: jnp.ones((t, n), F32)
    zer = lambda n: jnp.zeros((t, n), F32)
    qs = (MLA_NOPE + MLA_ROPE) ** -0.5 * LOG2E
    ct_q = qs * jnp.tile(jnp.concatenate([one(MLA_NOPE), cos_m, cos_m, one(LANES - MLA_NOPE - MLA_ROPE)], axis=1), (1, N_HEADS))
    st_q = qs * jnp.tile(jnp.concatenate([zer(MLA_NOPE), sin_m, sin_m, zer(LANES - MLA_NOPE - MLA_ROPE)], axis=1), (1, N_HEADS))
    ukv = w_ukv.reshape(depth, KV_LORA, N_HEADS, MLA_NOPE + MLA_V)
    wk = jnp.concatenate([ukv[..., :MLA_NOPE], jnp.zeros((depth, KV_LORA, N_HEADS, LANES - MLA_NOPE), w_ukv.dtype)],
                         axis=-1).reshape(depth, KV_LORA, hw).astype(BF16)
    wvt = ukv[..., MLA_NOPE:].reshape(depth, KV_LORA, N_HEADS * MLA_V).astype(BF16)
    place = np.zeros((MXU_N, hw), np.float32)
    for hh in range(N_HEADS):
        for j in range(MLA_ROPE):
            place[j, hh * LANES + MLA_NOPE + j] = 1.0
    place = jnp.asarray(place, BF16)

    wb = w_branch.astype(BF16)
    wo = w_out.astype(BF16)
    wmem = w_mem_kv.astype(BF16)
    norm_t = jnp.broadcast_to(diff_norm.astype(F32)[:, :, None], (depth, HEAD_DIM, TQ))

    h, hb = _layer_norm0(x.reshape(b * t, d), ln0_g, ln0_b)
    for l in range(depth):
        hb3 = hb.reshape(b, t, d)
        avt, bvt, cvt, dcq, ckv_iw, eq, z = _proj_plain(hb3, w_plain, w_vt, plain_widths, l)
        aq, ak, iq, ik, bq, bk, cq, ck, kr = _proj_rope(hb3, w_rope, ctab, stab, rope_heads, rope_tables, l)

        o_a = _dsa(aq, ak, avt, iq, ik, ik)
        o_b = _moba(bq, bk, bvt, _kbar(bk))
        lam_init = 0.8 - 0.6 * math.exp(-0.3 * l)
        misc = jnp.full((SUBLANES, LANES), lam_init, F32)
        o_c = _diff(cq, ck, cvt, diff_lam[l].astype(F32), norm_t[l], misc)
        qm, km, vmt = _mla_prep(dcq, ckv_iw, kr, mla_q_norm[l].reshape(1, Q_LORA), mla_kv_norm[l].reshape(1, KV_LORA),
                                wq, wq_rot, wk, wvt, place, ct_q, st_q, l)
        o_d = _mla(qm, km, vmt)
        o_e = _mem_attn(eq, *_mem_kv(mem, wmem, l))

        os5 = [o.reshape(b * t, BRANCH_W) for o in (o_a, o_b, o_c, o_d, o_e)]
        h, hb = _final(h, hb, os5, z.reshape(b * t, N_BRANCH * BRANCH_W), wg, wb, wo,
                       ln_g[l].reshape(1, d), ln_b[l].reshape(1, d), alpha, l)
    return h.reshape(b, t, d)
```

```python
import functools
import math

import numpy as np
import jax
import jax.numpy as jnp
from jax import lax
from jax.experimental import pallas as pl
from jax.experimental.pallas import tpu as pltpu

F32 = jnp.float32
BF16 = jnp.bfloat16
I32 = jnp.int32
I16 = jnp.int16

N_HEADS = 4
HEAD_DIM = 64
BRANCH_W = N_HEADS * HEAD_DIM
N_BRANCH = 5
ROPE_THETA = 500000.0
ROT_64 = 16
ROT_32 = 8
IDX_HEADS = 8
IDX_DIM = 32
TOPK_MAX = 256
MOBA_BLOCK = 256
MOBA_TOPK = 3
DIFF_DIM = 32
Q_LORA = 256
KV_LORA = 128
MLA_NOPE = 64
MLA_ROPE = 32
MLA_V = 64
LN_EPS = 1e-5
RMS_EPS = 1e-6

IN_LAYOUT = (
    ("a_q", BRANCH_W), ("a_k", BRANCH_W), ("a_v", BRANCH_W),
    ("i_q", IDX_HEADS * IDX_DIM), ("i_k", IDX_DIM), ("i_w", IDX_HEADS),
    ("b_q", BRANCH_W), ("b_k", BRANCH_W), ("b_v", BRANCH_W),
    ("c_q", BRANCH_W), ("c_k", BRANCH_W), ("c_v", BRANCH_W),
    ("d_cq", Q_LORA), ("d_ckv", KV_LORA), ("d_kr", MLA_ROPE),
    ("e_q", BRANCH_W),
    ("z", N_BRANCH * BRANCH_W),
    ("g", N_BRANCH * 1024),
)

SUBLANES = 8
LANES = 128
MXU_N = 256
TQ = 512
CK = 256
VROWS = HEAD_DIM + 16
FLASH_UNROLL = 4
NEG = -1e30
LOG2E = math.log2(math.e)
INT_MIN = np.int32(-2 ** 31)
HALF16 = 1 << 15
VMEM_LIMIT = 56 * 1024 * 1024


def _offsets():
    off, out = 0, {}
    for name, size in IN_LAYOUT:
        out[name] = (off, size)
        off += size
    return out


OFF = _offsets()


def _nt_dot(a, b):
    return lax.dot_general(a, b, (((1,), (1,)), ((), ())), preferred_element_type=F32)


def _tn_dot(w, x):
    return lax.dot_general(w, x, (((0,), (1,)), ((), ())), preferred_element_type=F32)


def _fold_rows(w, rows=SUBLANES):
    xs = [w[r:r + rows, :] for r in range(0, w.shape[0], rows)]
    while len(xs) > 1:
        xs = [xs[j] + xs[j + 1] for j in range(0, len(xs) - 1, 2)] + ([xs[-1]] if len(xs) % 2 else [])
    return xs[0]


def _masked_qt(q, shift, n, qt_ref):
    qt = q.T
    dim = lax.broadcasted_iota(I32, (LANES, qt.shape[1]), 0)
    for j in range(n):
        half = (j << shift) // LANES
        rows = qt[half * LANES:(half + 1) * LANES, :]
        qt_ref[j] = jnp.where(((dim + half * LANES) >> shift) == j, rows, 0.0).astype(BF16)


def _half(kc, j, shift):
    half = (j << shift) // LANES
    return kc[:, half * LANES:(half + 1) * LANES]


def _cparams(n_axes):
    return pltpu.CompilerParams(dimension_semantics=("arbitrary",) * n_axes,
                                vmem_limit_bytes=VMEM_LIMIT)


def _layer_spec(a, l):
    return pl.BlockSpec((None,) + a.shape[1:], lambda *_: (l,) + (0,) * (a.ndim - 1))


def _softmax_step(s_t, m_tile, vt_h, m_ref, acc_ref):
    m_old = m_ref[...]
    m_new = jnp.maximum(m_old, m_tile)
    alpha = jnp.exp2(m_old - m_new)
    p = jnp.exp2(s_t - m_new)
    acc_ref[...] = alpha * acc_ref[...] + jnp.dot(vt_h, p.astype(BF16), preferred_element_type=F32)
    m_ref[...] = m_new


def _softmax_init(m_ref, acc_ref):
    m_ref[...] = jnp.full(m_ref.shape, NEG, F32)
    acc_ref[...] = jnp.zeros(acc_ref.shape, F32)


def _softmax_out(acc_ref):
    return acc_ref[:HEAD_DIM, :] / acc_ref[HEAD_DIM:HEAD_DIM + 1, :]


def _store_vt(o_ref, vt):
    ones = jnp.ones((VROWS - HEAD_DIM, CK), o_ref.dtype)
    for j in range(o_ref.shape[0]):
        for h in range(N_HEADS):
            o_ref[j, h * VROWS:h * VROWS + HEAD_DIM, :] = (
                vt[h * HEAD_DIM:(h + 1) * HEAD_DIM, j * CK:(j + 1) * CK].astype(o_ref.dtype))
            o_ref[j, h * VROWS + HEAD_DIM:(h + 1) * VROWS, :] = ones


def _flash_loop(n_full, qk_all, mask, vt_rows, state, prep=None, causal_tail=True):
    s_ref, mx_ref, m_ref, acc_ref = state
    n_state = m_ref.shape[0]
    for j in range(n_state):
        _softmax_init(m_ref.at[j], acc_ref.at[j])

    def lanes_of(d):
        return slice(CK, TQ) if d == 1 else slice(None)

    def park(c, slot, d=None):
        lanes = lanes_of(d)
        ctx = c if prep is None else prep(c, lanes)
        for j, s in enumerate(qk_all(c, lanes)):
            if mask is not None:
                s = mask(ctx, j, s, lanes)
            if d is not None and causal_tail:
                s = jnp.where(_causal(d), s, NEG)
            s_ref[slot, j, :, lanes] = s
            mx_ref[slot, j, :, lanes] = jnp.max(s, axis=0, keepdims=True)

    def consume(c, slot, d=None):
        lanes = lanes_of(d)
        for j in range(n_state):
            _softmax_step(s_ref[slot, j, :, lanes], mx_ref[slot, j, :, lanes], vt_rows(c, j),
                          m_ref.at[j, :, lanes], acc_ref.at[j, :, lanes])

    def pair(c):
        park(c + 1, 1)
        consume(c, 0)
        park(c + 2, 0)
        consume(c + 1, 1)

    def body(g, carry):
        for u in range(0, FLASH_UNROLL, 2):
            pair(FLASH_UNROLL * g + u)
        return carry

    @pl.when(n_full == 0)
    def _():
        park(0, 0, d=0)
        park(1, 1, d=1)
        consume(0, 0)
        consume(1, 1, d=1)

    @pl.when(n_full > 0)
    def _():
        park(0, 0)
        n_loop = n_full - 2
        n_group = lax.shift_right_logical(n_loop, FLASH_UNROLL.bit_length() - 1)
        lax.fori_loop(0, n_group, body, 0)
        c0 = FLASH_UNROLL * n_group
        for u in range(FLASH_UNROLL // 2 - 1):
            @pl.when(n_loop - c0 >= 2 * (u + 1))
            def _(u=u):
                pair(c0 + 2 * u)
        c = n_loop
        park(c + 1, 1)
        consume(c, 0)
        park(c + 2, 0, d=0)
        consume(c + 1, 1)
        park(c + 3, 1, d=1)
        consume(c + 2, 0)
        consume(c + 3, 1, d=1)


def _causal(d):
    shape = (CK, TQ - d * CK)
    return lax.broadcasted_iota(I32, shape, 0) <= lax.broadcasted_iota(I32, shape, 1)


def _attn_scratch(n_state):
    return [pltpu.VMEM((2, n_state, CK, TQ), F32), pltpu.VMEM((2, n_state, 1, TQ), F32),
            pltpu.VMEM((n_state, 1, TQ), F32), pltpu.VMEM((n_state, VROWS, TQ), F32),
            pltpu.VMEM((BRANCH_W, TQ), F32)]


def _kv_specs(t, w):
    kspec = pl.BlockSpec((None, t, w), lambda bb, i: (bb, 0, 0))
    vspec = pl.BlockSpec((None, t // CK, N_HEADS * VROWS, CK), lambda bb, i: (bb, 0, 0, 0))
    return kspec, vspec


def _ln_kernel(x_ref, g_ref, b_ref, h_ref, hb_ref):
    x = x_ref[...]
    mu = jnp.mean(x, axis=1, keepdims=True)
    xc = x - mu
    var = jnp.mean(xc * xc, axis=1, keepdims=True)
    y = xc * lax.rsqrt(var + LN_EPS) * g_ref[...] + b_ref[...]
    h_ref[...] = y
    hb_ref[...] = y.astype(BF16)


def _layer_norm0(x2, g, b):
    n, d = x2.shape
    tm = 512
    row = pl.BlockSpec((tm, d), lambda i: (i, 0))
    vec = pl.BlockSpec((1, d), lambda i: (0, 0))
    return pl.pallas_call(
        _ln_kernel,
        out_shape=(jax.ShapeDtypeStruct((n, d), F32), jax.ShapeDtypeStruct((n, d), BF16)),
        grid=(n // tm,),
        in_specs=[row, vec, vec],
        out_specs=(row, row),
        compiler_params=_cparams(1),
        name="ln0",
    )(x2, g.reshape(1, d), b.reshape(1, d))


def _proj_plain_kernel(x_ref, w_ref, wt_ref, *out_refs, n_t):
    for g, o_ref in enumerate(out_refs[:n_t]):
        _store_vt(o_ref, _tn_dot(wt_ref[:, g * BRANCH_W:(g + 1) * BRANCH_W], x_ref[...]))
    off = 0
    for o_ref in out_refs[n_t:]:
        wd = o_ref.shape[-1]
        for j in range(0, wd, MXU_N):
            acc = jnp.dot(x_ref[...], w_ref[:, off + j:off + j + MXU_N], preferred_element_type=F32)
            o_ref[:, j:j + MXU_N] = acc.astype(o_ref.dtype)
        off += wd


def _proj_plain(hb3, w, wt, widths, l):
    b, t, d = hb3.shape
    tm = 512
    n_t = wt.shape[-1] // BRANCH_W
    shapes = [jax.ShapeDtypeStruct((b, t // CK, N_HEADS * VROWS, CK), BF16)] * n_t
    specs = [pl.BlockSpec((None, tm // CK, N_HEADS * VROWS, CK), lambda i, bb: (bb, i, 0, 0))] * n_t
    shapes += [jax.ShapeDtypeStruct((b, t, wd), BF16) for wd in widths]
    specs += [pl.BlockSpec((None, tm, wd), lambda i, bb: (bb, i, 0)) for wd in widths]
    return pl.pallas_call(
        functools.partial(_proj_plain_kernel, n_t=n_t),
        out_shape=tuple(shapes),
        grid=(t // tm, b),
        in_specs=[pl.BlockSpec((None, tm, d), lambda i, bb: (bb, i, 0)),
                  _layer_spec(w, l), _layer_spec(wt, l)],
        out_specs=tuple(specs),
        compiler_params=_cparams(2),
        name="proj_plain",
    )(hb3, w, wt)


def _proj_rope_kernel(x_ref, w_ref, c_ref, s_ref, *out_refs, heads, tables):
    lane = lax.broadcasted_iota(I32, (x_ref.shape[0], MXU_N), 1)
    for g, o_ref in enumerate(out_refs):
        hd, half = heads[g]
        sl = slice(g * MXU_N, (g + 1) * MXU_N)
        acc = jnp.dot(x_ref[...], w_ref[:, sl], preferred_element_type=F32)
        partner = jnp.where((lane & (hd - 1)) < half,
                            pltpu.roll(acc, MXU_N - half, 1), pltpu.roll(acc, half, 1))
        o_ref[...] = (acc * c_ref[tables[g]] + partner * s_ref[tables[g]]).astype(o_ref.dtype)


def _proj_rope(hb3, w, ctab, stab, heads, tables, l):
    b, t, d = hb3.shape
    tm = 512
    assert w.shape[-1] == MXU_N * len(heads)
    tspec = pl.BlockSpec((ctab.shape[0], tm, MXU_N), lambda i, bb: (0, i, 0))
    ospec = pl.BlockSpec((None, tm, MXU_N), lambda i, bb: (bb, i, 0))
    return pl.pallas_call(
        functools.partial(_proj_rope_kernel, heads=heads, tables=tables),
        out_shape=(jax.ShapeDtypeStruct((b, t, MXU_N), BF16),) * len(heads),
        grid=(t // tm, b),
        in_specs=[pl.BlockSpec((None, tm, d), lambda i, bb: (bb, i, 0)),
                  _layer_spec(w, l), tspec, tspec],
        out_specs=(ospec,) * len(heads),
        compiler_params=_cparams(2),
        name="proj_rope",
    )(hb3, w, ctab, stab)


def _dsa_kernel(aq_ref, ak_ref, avt_ref, iq_ref, ik_ref, iw_ref, pick_ref, tri_ref, o_ref,
                keys_ref, hi_ref, lo_ref, bk_ref, iqt_ref, aqt_ref, wt_ref, thr_ref, s_ref, mx_ref, m_ref, acc_ref, ot_ref,
                *, topk, idx_scale):
    i = pl.program_id(1)
    n_full = 2 * i

    iqt = iq_ref[...].astype(F32).T
    for hh in range(IDX_HEADS):
        iqt_ref[hh] = iqt[hh * IDX_DIM:(hh + 1) * IDX_DIM, :].astype(BF16)
    _masked_qt(aq_ref[...].astype(F32) * (HEAD_DIM ** -0.5 * LOG2E), 6, N_HEADS, aqt_ref)
    wt_ref[...] = _nt_dot(pick_ref[...], iw_ref[...]) * idx_scale

    def lanes_of(d):
        return slice(CK, TQ) if d == 1 else slice(None)

    def logits(c, d):
        kc = ik_ref[pl.ds(pl.multiple_of(c * CK, CK), CK), :]
        return [jnp.dot(kc[:, :IDX_DIM], iqt_ref[hh, :, lanes_of(d)], preferred_element_type=F32)
                for hh in range(IDX_HEADS)]

    def put_keys(c, key, lanes):
        keys_ref[c, :, lanes] = key
        hi_ref[c, :, lanes] = (key >> 16).astype(I16)
        lo_ref[c, :, lanes] = ((key & 0xFFFF) - HALF16).astype(I16)

    def score_chunk(c, lg, d):
        lanes = lanes_of(d)
        sc = jnp.zeros(lg[0].shape, F32)
        for hh in range(IDX_HEADS):
            sc = sc + jnp.maximum(lg[hh], 0.0) * wt_ref[hh:hh + 1, lanes]
        bits = pltpu.bitcast(sc, I32)
        key = jnp.where(bits < 0, INT_MIN - bits, bits)
        put_keys(c, key if d is None else jnp.where(_causal(d), key, INT_MIN), lanes)
        if d == 1:
            put_keys(c, jnp.full((CK, CK), INT_MIN, I32), slice(0, CK))

    def score_pair(c, d0, d1):
        lg0, lg1 = logits(c, d0), logits(c + 1, d1)
        score_chunk(c, lg0, d0)
        score_chunk(c + 1, lg1, d1)

    def score_body(p, carry):
        score_pair(2 * p, None, None)
        return carry

    lax.fori_loop(0, i, score_body, 0)
    score_pair(n_full, 0, 1)

    def pair_loop(body, init, last=None):
        def pair(p, carry):
            return body(2 * p + 1, body(2 * p, carry))
        carry = body(n_full, lax.fori_loop(0, i, pair, init))
        return (last or body)(n_full + 1, carry)

    def count16(pred, also=None):
        def hits(c, lanes):
            hit = jnp.where(pred(c, lanes), jnp.int16(1), jnp.int16(0))
            if also is not None:
                hit = jnp.where(also(c, lanes), hit, jnp.int16(0))
            return _fold_rows(hit, 2 * SUBLANES)

        def body(c, part):
            return part + hits(c, slice(None))

        def last(c, part):
            return jnp.concatenate([part[:, :CK], part[:, CK:] + hits(c, slice(CK, TQ))], axis=1)

        part = pair_loop(body, jnp.zeros((2 * SUBLANES, TQ), I16), last)
        return jnp.sum(part.astype(F32), axis=0, keepdims=True)

    def search16(ref, need):
        def bit_body(bi, t_u):
            c_u = t_u | jnp.left_shift(jnp.int32(1), 15 - bi)
            ck = (c_u - HALF16).astype(I16)
            cnt = count16(lambda c, lanes: ref[c, :, lanes] >= ck[:, lanes])
            return jnp.where(cnt >= need, c_u, t_u)
        return lax.fori_loop(0, 16, bit_body, jnp.zeros((1, TQ), I32))

    hi_u = search16(hi_ref, float(topk))
    thr_hi = (hi_u - HALF16).astype(I16)
    n_above = count16(lambda c, lanes: hi_ref[c, :, lanes] > thr_hi[:, lanes])

    def bucket_body(c, carry):
        bk_ref[c] = jnp.where(hi_ref[c] == thr_hi, lo_ref[c], jnp.int16(-HALF16))
        return carry

    pair_loop(bucket_body, 0)
    lo_u = search16(bk_ref, float(topk) - n_above)
    thr_lo = (lo_u - HALF16).astype(I16)
    thr = ((hi_u - HALF16) << 16) | lo_u

    n_gt = n_above + count16(lambda c, lanes: bk_ref[c, :, lanes] > thr_lo[:, lanes])
    n_eq = count16(lambda c, lanes: lo_ref[c, :, lanes] == thr_lo[:, lanes],
                   also=lambda c, lanes: hi_ref[c, :, lanes] == thr_hi[:, lanes])
    need = float(topk) - n_gt
    amb = jnp.logical_and(n_eq > need, thr > INT_MIN)
    any_amb = jnp.max(jnp.where(amb, 1.0, 0.0)) > 0.5

    @pl.when(any_amb)
    def _():
        def drop_body(c, seen):
            k = keys_ref[c]
            eq = k == thr
            eqf = jnp.where(eq, 1.0, 0.0)
            rank = jnp.dot(tri_ref[...], eqf.astype(BF16), preferred_element_type=F32) + seen
            drop = jnp.logical_and(jnp.logical_and(eq, rank > need), amb)
            keys_ref[c] = jnp.where(drop, INT_MIN, k)
            return seen + jnp.sum(eqf, axis=0, keepdims=True)

        pair_loop(drop_body, jnp.zeros((1, TQ), F32))

    thr_ref[...] = jnp.maximum(thr, INT_MIN + 1)

    def qk_all(c, lanes):
        kc = ak_ref[pl.ds(pl.multiple_of(c * CK, CK), CK), :]
        return [jnp.dot(_half(kc, h, 6), aqt_ref[h, :, lanes], preferred_element_type=F32) for h in range(N_HEADS)]

    _flash_loop(n_full, qk_all,
                lambda bias, h, s, lanes: s + bias,
                lambda c, h: avt_ref[c, h * VROWS:(h + 1) * VROWS, :],
                (s_ref, mx_ref, m_ref, acc_ref),
                prep=lambda c, lanes: jnp.where(keys_ref[c, :, lanes] >= thr_ref[:, lanes], 0.0, NEG),
                causal_tail=False)
    for h in range(N_HEADS):
        ot_ref[h * HEAD_DIM:(h + 1) * HEAD_DIM, :] = _softmax_out(acc_ref.at[h])
    o_ref[...] = ot_ref[...].T.astype(o_ref.dtype)


def _dsa(aq, ak, avt, iq, ik, iw):
    b, t, _ = aq.shape
    topk = min(TOPK_MAX, t // 4)
    qspec = pl.BlockSpec((None, TQ, BRANCH_W), lambda bb, i: (bb, i, 0))
    kspec, vspec = _kv_specs(t, BRANCH_W)
    pick = np.zeros((2 * SUBLANES, MXU_N), np.float32)
    for hh in range(IDX_HEADS):
        pick[hh, IDX_DIM + hh] = 1.0
    pick = jnp.asarray(pick, BF16)
    tri = jnp.asarray(np.tril(np.ones((CK, CK), np.float32)), BF16)
    kern = functools.partial(_dsa_kernel, topk=topk, idx_scale=(IDX_HEADS * IDX_DIM) ** -0.5)
    return pl.pallas_call(
        kern,
        out_shape=jax.ShapeDtypeStruct((b, t, BRANCH_W), BF16),
        grid=(b, t // TQ),
        in_specs=[qspec, kspec, vspec, qspec, kspec, qspec,
                  pl.BlockSpec(pick.shape, lambda bb, i: (0, 0)), pl.BlockSpec(tri.shape, lambda bb, i: (0, 0))],
        out_specs=qspec,
        scratch_shapes=[
            pltpu.VMEM((t // CK, CK, TQ), I32),
            pltpu.VMEM((t // CK, CK, TQ), I16),
            pltpu.VMEM((t // CK, CK, TQ), I16),
            pltpu.VMEM((t // CK, CK, TQ), I16),
            pltpu.VMEM((IDX_HEADS, IDX_DIM, TQ), BF16),
            pltpu.VMEM((N_HEADS, LANES, TQ), BF16),
            pltpu.VMEM((2 * SUBLANES, TQ), F32),
            pltpu.VMEM((1, TQ), I32),
        ] + _attn_scratch(N_HEADS),
        compiler_params=_cparams(2),
        name="dsa",
    )(aq, ak, avt, iq, ik, iw, pick, tri)


def _kbar_kernel(k_ref, o_ref):
    o_ref[...] = jnp.zeros(o_ref.shape, o_ref.dtype)
    nb = k_ref.shape[0] // MOBA_BLOCK
    for n in range(nb):
        blk = k_ref[n * MOBA_BLOCK:(n + 1) * MOBA_BLOCK, :].astype(F32)
        o_ref[n:n + 1, :] = jnp.mean(blk, axis=0, keepdims=True).astype(o_ref.dtype)


def _kbar(bk):
    b, t, w = bk.shape
    nbp = max(2 * SUBLANES, t // MOBA_BLOCK)
    return pl.pallas_call(
        _kbar_kernel,
        out_shape=jax.ShapeDtypeStruct((b, nbp, w), BF16),
        grid=(b,),
        in_specs=[pl.BlockSpec((None, t, w), lambda bb: (bb, 0, 0))],
        out_specs=pl.BlockSpec((None, nbp, w), lambda bb: (bb, 0, 0)),
        compiler_params=_cparams(1),
        name="moba_kbar",
    )(bk)


def _moba_kernel(q_ref, k_ref, vt_ref, kbar_ref, o_ref, qt_ref, bias_ref, s_ref, mx_ref, m_ref, acc_ref, ot_ref):
    i = pl.program_id(1)
    nbp = kbar_ref.shape[0]
    blk = lax.broadcasted_iota(I32, (nbp, TQ), 0)
    blk_f = blk.astype(F32)
    own = 2 * i + (lax.broadcasted_iota(I32, (nbp, TQ), 1) >> (MOBA_BLOCK.bit_length() - 1))
    _masked_qt(q_ref[...].astype(F32) * (HEAD_DIM ** -0.5 * LOG2E), 6, N_HEADS, qt_ref)

    for h in range(N_HEADS):
        g = jnp.where(blk < own, jnp.dot(_half(kbar_ref[...], h, 6), qt_ref[h], preferred_element_type=F32), NEG)
        bias = jnp.full((nbp, TQ), NEG, F32)
        for _ in range(MOBA_TOPK):
            mx = jnp.max(g, axis=0, keepdims=True)
            first = jnp.min(jnp.where(g == mx, blk_f, 1e9), axis=0, keepdims=True)
            pick = jnp.logical_and(blk_f == first, mx > 0.5 * NEG)
            bias = jnp.where(pick, 0.0, bias)
            g = jnp.where(pick, NEG, g)
        bias_ref[h] = jnp.where(blk == own, 0.0, bias)

    def qk_all(c, lanes):
        kc = k_ref[pl.ds(pl.multiple_of(c * CK, CK), CK), :]
        return [jnp.dot(_half(kc, h, 6), qt_ref[h, :, lanes], preferred_element_type=F32) for h in range(N_HEADS)]

    _flash_loop(2 * i, qk_all, lambda c, h, s, lanes: s + bias_ref[h, pl.ds(c, 1), lanes],
                lambda c, h: vt_ref[c, h * VROWS:(h + 1) * VROWS, :], (s_ref, mx_ref, m_ref, acc_ref))
    for h in range(N_HEADS):
        ot_ref[h * HEAD_DIM:(h + 1) * HEAD_DIM, :] = _softmax_out(acc_ref.at[h])
    o_ref[...] = ot_ref[...].T.astype(o_ref.dtype)


def _moba(bq, bk, bvt, kbar):
    b, t, w = bq.shape
    assert TQ == 2 * MOBA_BLOCK and CK == MOBA_BLOCK and t % TQ == 0
    nbp = kbar.shape[1]
    qspec = pl.BlockSpec((None, TQ, w), lambda bb, i: (bb, i, 0))
    kspec, vspec = _kv_specs(t, w)
    return pl.pallas_call(
        _moba_kernel,
        out_shape=jax.ShapeDtypeStruct((b, t, w), BF16),
        grid=(b, t // TQ),
        in_specs=[qspec, kspec, vspec, pl.BlockSpec((None, nbp, w), lambda bb, i: (bb, 0, 0))],
        out_specs=qspec,
        scratch_shapes=[pltpu.VMEM((N_HEADS, LANES, TQ), BF16), pltpu.VMEM((N_HEADS, nbp, TQ), F32)]
        + _attn_scratch(N_HEADS),
        compiler_params=_cparams(2),
        name="moba",
    )(bq, bk, bvt, kbar)


def _diff_kernel(q_ref, k_ref, vt_ref, lam_ref, norm_ref, misc_ref, o_ref,
                 qt_ref, s_ref, mx_ref, m_ref, acc_ref, ot_ref):
    i = pl.program_id(1)
    _masked_qt(q_ref[...].astype(F32) * (DIFF_DIM ** -0.5 * LOG2E), 5, 2 * N_HEADS, qt_ref)

    dl = lam_ref[...]
    lam_init = misc_ref[0:1, 0:1]
    lam = (jnp.exp(jnp.sum(dl[0:1, :] * dl[1:2, :], axis=1, keepdims=True))
           - jnp.exp(jnp.sum(dl[2:3, :] * dl[3:4, :], axis=1, keepdims=True)) + lam_init)

    def qk_all(c, lanes):
        kc = k_ref[pl.ds(pl.multiple_of(c * CK, CK), CK), :]
        return [jnp.dot(_half(kc, j, 5), qt_ref[j, :, lanes], preferred_element_type=F32) for j in range(2 * N_HEADS)]

    _flash_loop(2 * i, qk_all, None,
                lambda c, j: vt_ref[c, (j // 2) * VROWS:(j // 2 + 1) * VROWS, :],
                (s_ref, mx_ref, m_ref, acc_ref))

    post = norm_ref[...] * (1.0 - lam_init)
    for h in range(N_HEADS):
        o_h = _softmax_out(acc_ref.at[2 * h]) - lam * _softmax_out(acc_ref.at[2 * h + 1])
        ms = jnp.mean(o_h * o_h, axis=0, keepdims=True)
        ot_ref[h * HEAD_DIM:(h + 1) * HEAD_DIM, :] = o_h * lax.rsqrt(ms + RMS_EPS) * post
    o_ref[...] = ot_ref[...].T.astype(o_ref.dtype)


def _diff(cq, ck, cvt, lam, norm, misc):
    b, t, w = cq.shape
    qspec = pl.BlockSpec((None, TQ, w), lambda bb, i: (bb, i, 0))
    kspec, vspec = _kv_specs(t, w)
    full = lambda a: pl.BlockSpec(a.shape, lambda bb, i: (0,) * a.ndim)
    return pl.pallas_call(
        _diff_kernel,
        out_shape=jax.ShapeDtypeStruct((b, t, w), BF16),
        grid=(b, t // TQ),
        in_specs=[qspec, kspec, vspec, full(lam), full(norm), full(misc)],
        out_specs=qspec,
        scratch_shapes=[pltpu.VMEM((2 * N_HEADS, LANES, TQ), BF16)] + _attn_scratch(2 * N_HEADS),
        compiler_params=_cparams(2),
        name="diff",
    )(cq, ck, cvt, lam, norm, misc)


def _mla_prep_kernel(cq_ref, ckv_ref, kr_ref, qn_ref, kvn_ref, wq_ref, wqr_ref, wk_ref, wvt_ref,
                     p_ref, ct_ref, st_ref, q_out, k_out, vt_out):
    x = cq_ref[...].astype(F32)
    xn = (x * lax.rsqrt(jnp.mean(x * x, axis=1, keepdims=True) + RMS_EPS) * qn_ref[...]).astype(BF16)
    q = (jnp.dot(xn, wq_ref[...], preferred_element_type=F32) * ct_ref[...]
         + jnp.dot(xn, wqr_ref[...], preferred_element_type=F32) * st_ref[...])
    q_out[...] = q.astype(q_out.dtype)
    c = ckv_ref[:, :KV_LORA].astype(F32)
    cn = (c * lax.rsqrt(jnp.mean(c * c, axis=1, keepdims=True) + RMS_EPS) * kvn_ref[...]).astype(BF16)
    k = (jnp.dot(cn, wk_ref[...], preferred_element_type=F32)
         + jnp.dot(kr_ref[...], p_ref[...], preferred_element_type=F32))
    k_out[...] = k.astype(k_out.dtype)
    _store_vt(vt_out, _tn_dot(wvt_ref[...], cn))


def _mla_prep(dcq, ckv, kr, qn, kvn, wq, wqr, wk, wvt, pmat, ct, st, l):
    b, t, _ = dcq.shape
    tm = 512
    hw = N_HEADS * LANES
    row = lambda w: pl.BlockSpec((None, tm, w), lambda i, bb: (bb, i, 0))
    full = lambda a: pl.BlockSpec(a.shape, lambda i, bb: (0,) * a.ndim)
    tab = pl.BlockSpec((tm, hw), lambda i, bb: (i, 0))
    return pl.pallas_call(
        _mla_prep_kernel,
        out_shape=(jax.ShapeDtypeStruct((b, t, hw), BF16), jax.ShapeDtypeStruct((b, t, hw), BF16),
                   jax.ShapeDtypeStruct((b, t // CK, N_HEADS * VROWS, CK), BF16)),
        grid=(t // tm, b),
        in_specs=[row(Q_LORA), row(MXU_N), row(MXU_N), full(qn), full(kvn), _layer_spec(wq, l), _layer_spec(wqr, l),
                  _layer_spec(wk, l), _layer_spec(wvt, l), full(pmat), tab, tab],
        out_specs=(row(hw), row(hw),
                   pl.BlockSpec((None, tm // CK, N_HEADS * VROWS, CK), lambda i, bb: (bb, i, 0, 0))),
        compiler_params=_cparams(2),
        name="mla_prep",
    )(dcq, ckv, kr, qn, kvn, wq, wqr, wk, wvt, pmat, ct, st)


def _mla_kernel(q_ref, k_ref, vt_ref, o_ref, qt_ref, s_ref, mx_ref, m_ref, acc_ref, ot_ref):
    i = pl.program_id(1)
    hs = [slice(h * LANES, (h + 1) * LANES) for h in range(N_HEADS)]
    for h in range(N_HEADS):
        qt_ref[h] = q_ref[:, hs[h]].astype(F32).T.astype(BF16)

    def qk_all(c, lanes):
        start = pl.multiple_of(c * CK, CK)
        return [jnp.dot(k_ref[pl.ds(start, CK), hs[h]], qt_ref[h, :, lanes], preferred_element_type=F32)
                for h in range(N_HEADS)]

    _flash_loop(2 * i, qk_all, None,
                lambda c, h: vt_ref[c, h * VROWS:(h + 1) * VROWS, :],
                (s_ref, mx_ref, m_ref, acc_ref))
    for h in range(N_HEADS):
        ot_ref[h * HEAD_DIM:(h + 1) * HEAD_DIM, :] = _softmax_out(acc_ref.at[h])
    o_ref[...] = ot_ref[...].T.astype(o_ref.dtype)


def _mla(qm, km, vmt):
    b, t, hw = qm.shape
    kspec, vspec = _kv_specs(t, hw)
    return pl.pallas_call(
        _mla_kernel,
        out_shape=jax.ShapeDtypeStruct((b, t, BRANCH_W), BF16),
        grid=(b, t // TQ),
        in_specs=[pl.BlockSpec((None, TQ, hw), lambda bb, i: (bb, i, 0)), kspec, vspec],
        out_specs=pl.BlockSpec((None, TQ, BRANCH_W), lambda bb, i: (bb, i, 0)),
        scratch_shapes=[pltpu.VMEM((N_HEADS, LANES, TQ), BF16)] + _attn_scratch(N_HEADS),
        compiler_params=_cparams(2),
        name="mla",
    )(qm, km, vmt)


def _mem_kv_kernel(x_ref, w_ref, k_ref, vt_ref):
    x = x_ref[...].astype(BF16)
    k_ref[...] = jnp.dot(x, w_ref[:, :BRANCH_W], preferred_element_type=F32).astype(k_ref.dtype)
    _store_vt(vt_ref, _tn_dot(w_ref[:, BRANCH_W:], x))


def _mem_kv(mem, w, l):
    b, m, d = mem.shape
    assert m % CK == 0
    return pl.pallas_call(
        _mem_kv_kernel,
        out_shape=(jax.ShapeDtypeStruct((b, m, BRANCH_W), BF16),
                   jax.ShapeDtypeStruct((b, m // CK, N_HEADS * VROWS, CK), BF16)),
        grid=(b,),
        in_specs=[pl.BlockSpec((None, m, d), lambda bb: (bb, 0, 0)), _layer_spec(w, l)],
        out_specs=(pl.BlockSpec((None, m, BRANCH_W), lambda bb: (bb, 0, 0)),
                   pl.BlockSpec((None, m // CK, N_HEADS * VROWS, CK), lambda bb: (bb, 0, 0, 0))),
        compiler_params=_cparams(1),
        name="mem_kv",
    )(mem, w)


def _mem_kernel(q_ref, k_ref, vt_ref, o_ref, qt_ref, ot_ref):
    _masked_qt(q_ref[...].astype(F32) * (HEAD_DIM ** -0.5 * LOG2E), 6, N_HEADS, qt_ref)
    s_all = [jnp.dot(_half(k_ref[...], h, 6), qt_ref[h], preferred_element_type=F32) for h in range(N_HEADS)]
    for h in range(N_HEADS):
        s_t = s_all[h]
        p = jnp.exp2(s_t - jnp.max(s_t, axis=0, keepdims=True)).astype(BF16)
        acc = jnp.dot(vt_ref[0, h * VROWS:(h + 1) * VROWS, :], p, preferred_element_type=F32)
        ot_ref[h * HEAD_DIM:(h + 1) * HEAD_DIM, :] = acc[:HEAD_DIM, :] / acc[HEAD_DIM:HEAD_DIM + 1, :]
    o_ref[...] = ot_ref[...].T.astype(o_ref.dtype)


def _mem_attn(eq, mk, mvt):
    b, t, w = eq.shape
    m = mk.shape[1]
    assert m == CK
    return pl.pallas_call(
        _mem_kernel,
        out_shape=jax.ShapeDtypeStruct((b, t, w), BF16),
        grid=(b, t // TQ),
        in_specs=[pl.BlockSpec((None, TQ, w), lambda bb, i: (bb, i, 0)),
                  pl.BlockSpec((None, m, w), lambda bb, i: (bb, 0, 0)),
                  pl.BlockSpec((None,) + mvt.shape[1:], lambda bb, i: (bb, 0, 0, 0))],
        out_specs=pl.BlockSpec((None, TQ, w), lambda bb, i: (bb, i, 0)),
        scratch_shapes=[pltpu.VMEM((N_HEADS, LANES, TQ), BF16), pltpu.VMEM((BRANCH_W, TQ), F32)],
        compiler_params=_cparams(2),
        name="mem_attn",
    )(eq, mk, mvt)


def _final_kernel(h_ref, hb_ref, oa_ref, ob_ref, oc_ref, od_ref, oe_ref, z_ref,
                  wg_ref, wb_ref, wo_ref, g_ref, b_ref, h_out, hb_out, acc_ref, *, alpha):
    d = h_ref.shape[1]
    half = h_ref.shape[0] // 2
    for n, o_ref in enumerate((oa_ref, ob_ref, oc_ref, od_ref, oe_ref)):
        for r in range(2):
            rows = slice(r * half, (r + 1) * half)
            z = z_ref[rows, n * BRANCH_W:(n + 1) * BRANCH_W].astype(F32)
            y = o_ref[rows, :].astype(F32) * (z / (1.0 + jnp.exp(-z)))
            u = jnp.dot(y.astype(BF16), wb_ref[n], preferred_element_type=F32)
            g = jnp.dot(hb_ref[rows, :], wg_ref[:, n * d:(n + 1) * d], preferred_element_type=F32)
            t = u / (1.0 + jnp.exp(-g))
            acc_ref[rows, :] = t if n == 0 else acc_ref[rows, :] + t
    for r in range(2):
        rows = slice(r * half, (r + 1) * half)
        out = jnp.dot(acc_ref[rows, :].astype(BF16), wo_ref[...], preferred_element_type=F32)
        x = alpha * h_ref[rows, :] + out
        mu = jnp.mean(x, axis=1, keepdims=True)
        xc = x - mu
        var = jnp.mean(xc * xc, axis=1, keepdims=True)
        y = xc * lax.rsqrt(var + LN_EPS) * g_ref[...] + b_ref[...]
        h_out[rows, :] = y
        hb_out[rows, :] = y.astype(BF16)


def _final(h, hb, os5, z, wg, wb, wo, ln_g, ln_b, alpha, l):
    n, d = h.shape
    tm = 512
    row = lambda w: pl.BlockSpec((tm, w), lambda i: (i, 0))
    full = lambda a: pl.BlockSpec(a.shape, lambda i: (0,) * a.ndim)
    return pl.pallas_call(
        functools.partial(_final_kernel, alpha=alpha),
        out_shape=(jax.ShapeDtypeStruct((n, d), F32), jax.ShapeDtypeStruct((n, d), BF16)),
        grid=(n // tm,),
        in_specs=[row(d), row(d)] + [row(BRANCH_W)] * N_BRANCH + [row(N_BRANCH * BRANCH_W),
                  _layer_spec(wg, l), _layer_spec(wb, l), _layer_spec(wo, l), full(ln_g), full(ln_b)],
        out_specs=(row(d), row(d)),
        scratch_shapes=[pltpu.VMEM((tm, d), F32)],
        compiler_params=_cparams(1),
        name="merge_out_ln",
    )(h, hb, *os5, z, wg, wb, wo, ln_g, ln_b)


ROPE_GROUPS = (("a_q", N_HEADS, HEAD_DIM, ROT_64), ("a_k", N_HEADS, HEAD_DIM, ROT_64),
               ("i_q", IDX_HEADS, IDX_DIM, ROT_32), ("i_k", 1, MXU_N, ROT_32),
               ("b_q", N_HEADS, HEAD_DIM, ROT_64), ("b_k", N_HEADS, HEAD_DIM, ROT_64),
               ("c_q", 2 * N_HEADS, DIFF_DIM, ROT_32), ("c_k", 2 * N_HEADS, DIFF_DIM, ROT_32),
               ("d_kr", 1, MXU_N, MLA_ROPE))
PLAIN_COLS = ("d_cq", "d_ckv", "e_q") + tuple(("z", j) for j in range(N_BRANCH))
VALUE_COLS = ("a_v", "b_v", "c_v")
GATE_COLS = tuple(("g", j) for j in range(OFF["g"][1] // MXU_N))


def _window_start(col):
    name, j = col if isinstance(col, tuple) else (col, 0)
    return OFF[name][0] + j * MXU_N


def _weight_prep_kernel(offs_ref, wt_ref, o_ref):
    o_ref[...] = wt_ref[...].T.astype(o_ref.dtype)


def _weight_windows(wt, cols, name):
    depth, n, d = wt.shape
    starts = [_window_start(c) for c in cols]
    assert all(st % SUBLANES == 0 and st + MXU_N <= n for st in starts)
    grid_spec = pltpu.PrefetchScalarGridSpec(
        num_scalar_prefetch=1,
        grid=(depth, len(cols)),
        in_specs=[pl.BlockSpec((None, pl.Element(MXU_N), pl.Element(d)),
                               lambda l, j, offs: (l, pl.multiple_of(offs[j], SUBLANES), 0))],
        out_specs=pl.BlockSpec((None, d, MXU_N), lambda l, j, offs: (l, 0, j)),
    )
    return pl.pallas_call(
        _weight_prep_kernel,
        out_shape=jax.ShapeDtypeStruct((depth, d, MXU_N * len(cols)), BF16),
        grid_spec=grid_spec,
        compiler_params=_cparams(2),
        name=name,
    )(jnp.asarray(np.asarray(starts, np.int32)), wt)


def _weight_prep(w_in):
    wt = jnp.swapaxes(w_in, 1, 2)
    return (_weight_windows(wt, PLAIN_COLS, "wprep_plain"), _weight_windows(wt, VALUE_COLS, "wprep_value"),
            _weight_windows(wt, [name for name, *_ in ROPE_GROUPS], "wprep_rope"),
            _weight_windows(wt, GATE_COLS, "wprep_gate"))


def _rope_tables(seq, rot_dim):
    pos = jnp.arange(seq, dtype=F32)
    inv = ROPE_THETA ** (-jnp.arange(0, rot_dim, 2, dtype=F32) / rot_dim)
    ang = pos[:, None] * inv[None, :]
    return jnp.cos(ang), jnp.sin(ang)


def _rope_cs(t, nh, hd, r):
    cos, sin = _rope_tables(t, r)
    c = jnp.concatenate([cos, cos, jnp.ones((t, hd - r), F32)], axis=1)
    s = jnp.concatenate([-sin, sin, jnp.zeros((t, hd - r), F32)], axis=1)
    return jnp.tile(c, (1, nh)), jnp.tile(s, (1, nh))


def kernel(x, mem, ln0_g, ln0_b, w_in, mla_q_norm, w_uq, mla_kv_norm, w_ukv, diff_lam, diff_norm,
           w_mem_kv, w_branch, w_out, ln_g, ln_b):
    b, t, d = x.shape
    depth = w_in.shape[0]
    alpha = (2 * depth) ** 0.25
    assert t % 512 == 0 and d == 1024

    w_plain, w_vt, w_rope, wg = _weight_prep(w_in)
    plain_widths = (BRANCH_W,) * 3 + (N_BRANCH * BRANCH_W,)
    rope_heads = tuple((hd, r // 2) for _, _, hd, r in ROPE_GROUPS)
    patterns = sorted(set((nh, hd, r) for _, nh, hd, r in ROPE_GROUPS))
    rope_tables = tuple(patterns.index((nh, hd, r)) for _, nh, hd, r in ROPE_GROUPS)
    cs = [_rope_cs(t, nh, hd, r) for nh, hd, r in patterns]
    ctab = jnp.stack([c for c, _ in cs])
    stab = jnp.stack([s for _, s in cs])

    uq = w_uq.reshape(depth, Q_LORA, N_HEADS, MLA_NOPE + MLA_ROPE)
    qn_w, qr_w = uq[..., :MLA_NOPE], uq[..., MLA_NOPE:]
    pad32 = jnp.zeros((depth, Q_LORA, N_HEADS, LANES - MLA_NOPE - MLA_ROPE), w_uq.dtype)
    hw = N_HEADS * LANES
    wq = jnp.concatenate([qn_w, qr_w, pad32], axis=-1).reshape(depth, Q_LORA, hw).astype(BF16)
    half = MLA_ROPE // 2
    wq_rot = jnp.concatenate([jnp.zeros_like(qn_w), -qr_w[..., half:], qr_w[..., :half], pad32],
                             axis=-1).reshape(depth, Q_LORA, hw).astype(BF16)
    cos_m, sin_m = _rope_tables(t, MLA_ROPE)
    one = lambda n: jnp.ones((t, n), F32)
    zer = lambda n: jnp.zeros((t, n), F32)
    qs = (MLA_NOPE + MLA_ROPE) ** -0.5 * LOG2E
    ct_q = qs * jnp.tile(jnp.concatenate([one(MLA_NOPE), cos_m, cos_m, one(LANES - MLA_NOPE - MLA_ROPE)], axis=1), (1, N_HEADS))
    st_q = qs * jnp.tile(jnp.concatenate([zer(MLA_NOPE), sin_m, sin_m, zer(LANES - MLA_NOPE - MLA_ROPE)], axis=1), (1, N_HEADS))
    ukv = w_ukv.reshape(depth, KV_LORA, N_HEADS, MLA_NOPE + MLA_V)
    wk = jnp.concatenate([ukv[..., :MLA_NOPE], jnp.zeros((depth, KV_LORA, N_HEADS, LANES - MLA_NOPE), w_ukv.dtype)],
                         axis=-1).reshape(depth, KV_LORA, hw).astype(BF16)
    wvt = ukv[..., MLA_NOPE:].reshape(depth, KV_LORA, N_HEADS * MLA_V).astype(BF16)
    place = np.zeros((MXU_N, hw), np.float32)
    for hh in range(N_HEADS):
        for j in range(MLA_ROPE):
            place[j, hh * LANES + MLA_NOPE + j] = 1.0
    place = jnp.asarray(place, BF16)

    wb = w_branch.astype(BF16)
    wo = w_out.astype(BF16)
    wmem = w_mem_kv.astype(BF16)
    norm_t = jnp.broadcast_to(diff_norm.astype(F32)[:, :, None], (depth, HEAD_DIM, TQ))

    h, hb = _layer_norm0(x.reshape(b * t, d), ln0_g, ln0_b)
    for l in range(depth):
        hb3 = hb.reshape(b, t, d)
        avt, bvt, cvt, dcq, ckv_iw, eq, z = _proj_plain(hb3, w_plain, w_vt, plain_widths, l)
        aq, ak, iq, ik, bq, bk, cq, ck, kr = _proj_rope(hb3, w_rope, ctab, stab, rope_heads, rope_tables, l)

        o_a = _dsa(aq, ak, avt, iq, ik, ik)
        o_b = _moba(bq, bk, bvt, _kbar(bk))
        lam_init = 0.8 - 0.6 * math.exp(-0.3 * l)
        misc = jnp.full((SUBLANES, LANES), lam_init, F32)
        o_c = _diff(cq, ck, cvt, diff_lam[l].astype(F32), norm_t[l], misc)
        qm, km, vmt = _mla_prep(dcq, ckv_iw, kr, mla_q_norm[l].reshape(1, Q_LORA), mla_kv_norm[l].reshape(1, KV_LORA),
                                wq, wq_rot, wk, wvt, place, ct_q, st_q, l)
        o_d = _mla(qm, km, vmt)
        o_e = _mem_attn(eq, *_mem_kv(mem, wmem, l))

        os5 = [o.reshape(b * t, BRANCH_W) for o in (o_a, o_b, o_c, o_d, o_e)]
        h, hb = _final(h, hb, os5, z.reshape(b * t, N_BRANCH * BRANCH_W), wg, wb, wo,
                       ln_g[l].reshape(1, d), ln_b[l].reshape(1, d), alpha, l)
    return h.reshape(b, t, d)
```

```python
import functools
import math

import numpy as np
import jax
import jax.numpy as jnp
from jax import lax
from jax.experimental import pallas as pl
from jax.experimental.pallas import tpu as pltpu

F32 = jnp.float32
BF16 = jnp.bfloat16
I32 = jnp.int32
I16 = jnp.int16

N_HEADS = 4
HEAD_DIM = 64
BRANCH_W = N_HEADS * HEAD_DIM
N_BRANCH = 5
ROPE_THETA = 500000.0
ROT_64 = 16
ROT_32 = 8
IDX_HEADS = 8
IDX_DIM = 32
TOPK_MAX = 256
MOBA_BLOCK = 256
MOBA_TOPK = 3
DIFF_DIM = 32
Q_LORA = 256
KV_LORA = 128
MLA_NOPE = 64
MLA_ROPE = 32
MLA_V = 64
LN_EPS = 1e-5
RMS_EPS = 1e-6

IN_LAYOUT = (
    ("a_q", BRANCH_W), ("a_k", BRANCH_W), ("a_v", BRANCH_W),
    ("i_q", IDX_HEADS * IDX_DIM), ("i_k", IDX_DIM), ("i_w", IDX_HEADS),
    ("b_q", BRANCH_W), ("b_k", BRANCH_W), ("b_v", BRANCH_W),
    ("c_q", BRANCH_W), ("c_k", BRANCH_W), ("c_v", BRANCH_W),
    ("d_cq", Q_LORA), ("d_ckv", KV_LORA), ("d_kr", MLA_ROPE),
    ("e_q", BRANCH_W),
    ("z", N_BRANCH * BRANCH_W),
    ("g", N_BRANCH * 1024),
)

SUBLANES = 8
LANES = 128
MXU_N = 256
TQ = 512
CK = 256
VROWS = HEAD_DIM + 16
FLASH_UNROLL = 4
NEG = -1e30
LOG2E = math.log2(math.e)
INT_MIN = np.int32(-2 ** 31)
HALF16 = 1 << 15
VMEM_LIMIT = 56 * 1024 * 1024


def _offsets():
    off, out = 0, {}
    for name, size in IN_LAYOUT:
        out[name] = (off, size)
        off += size
    return out


OFF = _offsets()


def _nt_dot(a, b):
    return lax.dot_general(a, b, (((1,), (1,)), ((), ())), preferred_element_type=F32)


def _tn_dot(w, x):
    return lax.dot_general(w, x, (((0,), (1,)), ((), ())), preferred_element_type=F32)


def _fold_rows(w, rows=SUBLANES):
    xs = [w[r:r + rows, :] for r in range(0, w.shape[0], rows)]
    while len(xs) > 1:
        xs = [xs[j] + xs[j + 1] for j in range(0, len(xs) - 1, 2)] + ([xs[-1]] if len(xs) % 2 else [])
    return xs[0]


def _masked_qt(q, shift, n, qt_ref):
    qt = q.T
    dim = lax.broadcasted_iota(I32, (LANES, qt.shape[1]), 0)
    for j in range(n):
        half = (j << shift) // LANES
        rows = qt[half * LANES:(half + 1) * LANES, :]
        qt_ref[j] = jnp.where(((dim + half * LANES) >> shift) == j, rows, 0.0).astype(BF16)


def _half(kc, j, shift):
    half = (j << shift) // LANES
    return kc[:, half * LANES:(half + 1) * LANES]


def _cparams(n_axes):
    return pltpu.CompilerParams(dimension_semantics=("arbitrary",) * n_axes,
                                vmem_limit_bytes=VMEM_LIMIT)


def _layer_spec(a, l):
    return pl.BlockSpec((None,) + a.shape[1:], lambda *_: (l,) + (0,) * (a.ndim - 1))


def _softmax_step(s_t, m_tile, vt_h, m_ref, acc_ref):
    m_old = m_ref[...]
    m_new = jnp.maximum(m_old, m_tile)
    alpha = jnp.exp2(m_old - m_new)
    p = jnp.exp2(s_t - m_new)
    acc_ref[...] = alpha * acc_ref[...] + jnp.dot(vt_h, p.astype(BF16), preferred_element_type=F32)
    m_ref[...] = m_new


def _softmax_init(m_ref, acc_ref):
    m_ref[...] = jnp.full(m_ref.shape, NEG, F32)
    acc_ref[...] = jnp.zeros(acc_ref.shape, F32)


def _softmax_out(acc_ref):
    return acc_ref[:HEAD_DIM, :] / acc_ref[HEAD_DIM:HEAD_DIM + 1, :]


def _store_vt(o_ref, vt):
    ones = jnp.ones((VROWS - HEAD_DIM, CK), o_ref.dtype)
    for j in range(o_ref.shape[0]):
        for h in range(N_HEADS):
            o_ref[j, h * VROWS:h * VROWS + HEAD_DIM, :] = (
                vt[h * HEAD_DIM:(h + 1) * HEAD_DIM, j * CK:(j + 1) * CK].astype(o_ref.dtype))
            o_ref[j, h * VROWS + HEAD_DIM:(h + 1) * VROWS, :] = ones


def _flash_loop(n_full, qk_all, mask, vt_rows, state, prep=None, causal_tail=True):
    s_ref, mx_ref, m_ref, acc_ref = state
    n_state = m_ref.shape[0]
    for j in range(n_state):
        _softmax_init(m_ref.at[j], acc_ref.at[j])

    def lanes_of(d):
        return slice(CK, TQ) if d == 1 else slice(None)

    def park(c, slot, d=None):
        lanes = lanes_of(d)
        ctx = c if prep is None else prep(c, lanes)
        for j, s in enumerate(qk_all(c, lanes)):
            if mask is not None:
                s = mask(ctx, j, s, lanes)
            if d is not None and causal_tail:
                s = jnp.where(_causal(d), s, NEG)
            s_ref[slot, j, :, lanes] = s
            mx_ref[slot, j, :, lanes] = jnp.max(s, axis=0, keepdims=True)

    def consume(c, slot, d=None):
        lanes = lanes_of(d)
        for j in range(n_state):
            _softmax_step(s_ref[slot, j, :, lanes], mx_ref[slot, j, :, lanes], vt_rows(c, j),
                          m_ref.at[j, :, lanes], acc_ref.at[j, :, lanes])

    def pair(c):
        park(c + 1, 1)
        consume(c, 0)
        park(c + 2, 0)
        consume(c + 1, 1)

    def body(g, carry):
        for u in range(0, FLASH_UNROLL, 2):
            pair(FLASH_UNROLL * g + u)
        return carry

    @pl.when(n_full == 0)
    def _():
        park(0, 0, d=0)
        park(1, 1, d=1)
        consume(0, 0)
        consume(1, 1, d=1)

    @pl.when(n_full > 0)
    def _():
        park(0, 0)
        n_loop = n_full - 2
        n_group = lax.shift_right_logical(n_loop, FLASH_UNROLL.bit_length() - 1)
        lax.fori_loop(0, n_group, body, 0)
        c0 = FLASH_UNROLL * n_group
        for u in range(FLASH_UNROLL // 2 - 1):
            @pl.when(n_loop - c0 >= 2 * (u + 1))
            def _(u=u):
                pair(c0 + 2 * u)
        c = n_loop
        park(c + 1, 1)
        consume(c, 0)
        park(c + 2, 0, d=0)
        consume(c + 1, 1)
        park(c + 3, 1, d=1)
        consume(c + 2, 0)
        consume(c + 3, 1, d=1)


def _causal(d):
    shape = (CK, TQ - d * CK)
    return lax.broadcasted_iota(I32, shape, 0) <= lax.broadcasted_iota(I32, shape, 1)


def _attn_scratch(n_state):
    return [pltpu.VMEM((2, n_state, CK, TQ), F32), pltpu.VMEM((2, n_state, 1, TQ), F32),
            pltpu.VMEM((n_state, 1, TQ), F32), pltpu.VMEM((n_state, VROWS, TQ), F32),
            pltpu.VMEM((BRANCH_W, TQ), F32)]


def _kv_specs(t, w):
    kspec = pl.BlockSpec((None, t, w), lambda bb, i: (bb, 0, 0))
    vspec = pl.BlockSpec((None, t // CK, N_HEADS * VROWS, CK), lambda bb, i: (bb, 0, 0, 0))
    return kspec, vspec


def _ln_kernel(x_ref, g_ref, b_ref, h_ref, hb_ref):
    x = x_ref[...]
    mu = jnp.mean(x, axis=1, keepdims=True)
    xc = x - mu
    var = jnp.mean(xc * xc, axis=1, keepdims=True)
    y = xc * lax.rsqrt(var + LN_EPS) * g_ref[...] + b_ref[...]
    h_ref[...] = y
    hb_ref[...] = y.astype(BF16)


def _layer_norm0(x2, g, b):
    n, d = x2.shape
    tm = 512
    row = pl.BlockSpec((tm, d), lambda i: (i, 0))
    vec = pl.BlockSpec((1, d), lambda i: (0, 0))
    return pl.pallas_call(
        _ln_kernel,
        out_shape=(jax.ShapeDtypeStruct((n, d), F32), jax.ShapeDtypeStruct((n, d), BF16)),
        grid=(n // tm,),
        in_specs=[row, vec, vec],
        out_specs=(row, row),
        compiler_params=_cparams(1),
        name="ln0",
    )(x2, g.reshape(1, d), b.reshape(1, d))


def _proj_plain_kernel(x_ref, w_ref, wt_ref, *out_refs, n_t):
    for g, o_ref in enumerate(out_refs[:n_t]):
        _store_vt(o_ref, _tn_dot(wt_ref[:, g * BRANCH_W:(g + 1) * BRANCH_W], x_ref[...]))
    off = 0
    for o_ref in out_refs[n_t:]:
        wd = o_ref.shape[-1]
        for j in range(0, wd, MXU_N):
            acc = jnp.dot(x_ref[...], w_ref[:, off + j:off + j + MXU_N], preferred_element_type=F32)
            o_ref[:, j:j + MXU_N] = acc.astype(o_ref.dtype)
        off += wd


def _proj_plain(hb3, w, wt, widths, l):
    b, t, d = hb3.shape
    tm = 512
    n_t = wt.shape[-1] // BRANCH_W
    shapes = [jax.ShapeDtypeStruct((b, t // CK, N_HEADS * VROWS, CK), BF16)] * n_t
    specs = [pl.BlockSpec((None, tm // CK, N_HEADS * VROWS, CK), lambda i, bb: (bb, i, 0, 0))] * n_t
    shapes += [jax.ShapeDtypeStruct((b, t, wd), BF16) for wd in widths]
    specs += [pl.BlockSpec((None, tm, wd), lambda i, bb: (bb, i, 0)) for wd in widths]
    return pl.pallas_call(
        functools.partial(_proj_plain_kernel, n_t=n_t),
        out_shape=tuple(shapes),
        grid=(t // tm, b),
        in_specs=[pl.BlockSpec((None, tm, d), lambda i, bb: (bb, i, 0)),
                  _layer_spec(w, l), _layer_spec(wt, l)],
        out_specs=tuple(specs),
        compiler_params=_cparams(2),
        name="proj_plain",
    )(hb3, w, wt)


def _proj_rope_kernel(x_ref, w_ref, c_ref, s_ref, *out_refs, heads, tables):
    lane = lax.broadcasted_iota(I32, (x_ref.shape[0], MXU_N), 1)
    for g, o_ref in enumerate(out_refs):
        hd, half = heads[g]
        sl = slice(g * MXU_N, (g + 1) * MXU_N)
        acc = jnp.dot(x_ref[...], w_ref[:, sl], preferred_element_type=F32)
        partner = jnp.where((lane & (hd - 1)) < half,
                            pltpu.roll(acc, MXU_N - half, 1), pltpu.roll(acc, half, 1))
        o_ref[...] = (acc * c_ref[tables[g]] + partner * s_ref[tables[g]]).astype(o_ref.dtype)


def _proj_rope(hb3, w, ctab, stab, heads, tables, l):
    b, t, d = hb3.shape
    tm = 512
    assert w.shape[-1] == MXU_N * len(heads)
    tspec = pl.BlockSpec((ctab.shape[0], tm, MXU_N), lambda i, bb: (0, i, 0))
    ospec = pl.BlockSpec((None, tm, MXU_N), lambda i, bb: (bb, i, 0))
    return pl.pallas_call(
        functools.partial(_proj_rope_kernel, heads=heads, tables=tables),
        out_shape=(jax.ShapeDtypeStruct((b, t, MXU_N), BF16),) * len(heads),
        grid=(t // tm, b),
        in_specs=[pl.BlockSpec((None, tm, d), lambda i, bb: (bb, i, 0)),
                  _layer_spec(w, l), tspec, tspec],
        out_specs=(ospec,) * len(heads),
        compiler_params=_cparams(2),
        name="proj_rope",
    )(hb3, w, ctab, stab)


def _dsa_kernel(aq_ref, ak_ref, avt_ref, iq_ref, ik_ref, iw_ref, pick_ref, tri_ref, o_ref,
                keys_ref, hi_ref, lo_ref, bk_ref, top_ref, sel_ref, iqt_ref, aqt_ref, wt_ref, thr_ref, s_ref, mx_ref, m_ref, acc_ref, ot_ref,
                *, topk, idx_scale):
    i = pl.program_id(1)
    n_full = 2 * i

    iqt = iq_ref[...].astype(F32).T
    for hh in range(IDX_HEADS):
        iqt_ref[hh] = iqt[hh * IDX_DIM:(hh + 1) * IDX_DIM, :].astype(BF16)
    _masked_qt(aq_ref[...].astype(F32) * (HEAD_DIM ** -0.5 * LOG2E), 6, N_HEADS, aqt_ref)
    wt_ref[...] = _nt_dot(pick_ref[...], iw_ref[...]) * idx_scale

    def lanes_of(d):
        return slice(CK, TQ) if d == 1 else slice(None)

    def logits(c, d):
        kc = ik_ref[pl.ds(pl.multiple_of(c * CK, CK), CK), :]
        return [jnp.dot(kc[:, :IDX_DIM], iqt_ref[hh, :, lanes_of(d)], preferred_element_type=F32)
                for hh in range(IDX_HEADS)]

    def put_keys(c, key, lanes):
        keys_ref[c, :, lanes] = key
        hi_ref[c, :, lanes] = (key >> 16).astype(I16)
        lo_ref[c, :, lanes] = ((key & 0xFFFF) - HALF16).astype(I16)

    def score_chunk(c, lg, d):
        lanes = lanes_of(d)
        sc = jnp.zeros(lg[0].shape, F32)
        for hh in range(IDX_HEADS):
            sc = sc + jnp.maximum(lg[hh], 0.0) * wt_ref[hh:hh + 1, lanes]
        bits = pltpu.bitcast(sc, I32)
        key = jnp.where(bits < 0, INT_MIN - bits, bits)
        put_keys(c, key if d is None else jnp.where(_causal(d), key, INT_MIN), lanes)
        if d == 1:
            put_keys(c, jnp.full((CK, CK), INT_MIN, I32), slice(0, CK))

    def score_pair(c, d0, d1):
        lg0, lg1 = logits(c, d0), logits(c + 1, d1)
        score_chunk(c, lg0, d0)
        score_chunk(c + 1, lg1, d1)

    def score_body(p, carry):
        score_pair(2 * p, None, None)
        return carry

    lax.fori_loop(0, i, score_body, 0)
    score_pair(n_full, 0, 1)

    def pair_loop(body, init, last=None):
        def pair(p, carry):
            return body(2 * p + 1, body(2 * p, carry))
        carry = body(n_full, lax.fori_loop(0, i, pair, init))
        return (last or body)(n_full + 1, carry)

    def count16(pred, also=None):
        def hits(c, lanes):
            hit = jnp.where(pred(c, lanes), jnp.int16(1), jnp.int16(0))
            if also is not None:
                hit = jnp.where(also(c, lanes), hit, jnp.int16(0))
            return _fold_rows(hit, 2 * SUBLANES)

        def body(c, part):
            return part + hits(c, slice(None))

        def last(c, part):
            return jnp.concatenate([part[:, :CK], part[:, CK:] + hits(c, slice(CK, TQ))], axis=1)

        part = pair_loop(body, jnp.zeros((2 * SUBLANES, TQ), I16), last)
        return jnp.sum(part.astype(F32), axis=0, keepdims=True)

    def search16(count_ge, need):
        def bit_body(bi, t_u):
            c_u = t_u | jnp.left_shift(jnp.int32(1), 15 - bi)
            cnt = count_ge((c_u - HALF16).astype(I16))
            return jnp.where(cnt >= need, c_u, t_u)
        return lax.fori_loop(0, 16, bit_body, jnp.zeros((1, TQ), I32))

    hi_u = search16(lambda ck: count16(lambda c, lanes: hi_ref[c, :, lanes] >= ck[:, lanes]),
                    float(topk))
    thr_hi = (hi_u - HALF16).astype(I16)
    n_above = count16(lambda c, lanes: hi_ref[c, :, lanes] > thr_hi[:, lanes])
    need_lo = float(topk) - n_above

    fill = jnp.int16(-HALF16)
    top_ref[...] = jnp.full(top_ref.shape, fill, I16)

    def bucket(c):
        return jnp.where(hi_ref[c] == thr_hi, lo_ref[c], fill)

    def top_body(c, carry):
        x = bucket(c)
        for r in range(4):
            m = top_ref[r]
            swap = x > m
            top_ref[r] = jnp.where(swap, x, m)
            x = jnp.where(swap, m, x)
        return carry

    many = i > 1

    @pl.when(many)
    def _():
        pair_loop(top_body, 0)

    fourth = jnp.max(_fold_rows(jnp.where(top_ref[3] > fill, jnp.int16(1), jnp.int16(0)),
                                2 * SUBLANES).astype(F32)) > 0.5
    use_top = jnp.logical_and(many, jnp.logical_not(fourth))

    def count_top(pred):
        part = jnp.zeros((2 * SUBLANES, TQ), I16)
        for r in range(3):
            part = part + _fold_rows(jnp.where(pred(top_ref[r]), jnp.int16(1), jnp.int16(0)), 2 * SUBLANES)
        return jnp.sum(part.astype(F32), axis=0, keepdims=True)

    def lo_select(count_ge, count_gt):
        lo_u = search16(count_ge, need_lo)
        sel_ref[0:1, :] = lo_u.astype(F32)
        sel_ref[1:2, :] = count_gt((lo_u - HALF16).astype(I16))

    @pl.when(use_top)
    def _():
        lo_select(lambda ck: count_top(lambda x: x >= ck), lambda t: count_top(lambda x: x > t))

    @pl.when(jnp.logical_not(use_top))
    def _():
        def bucket_body(c, carry):
            bk_ref[c] = bucket(c)
            return carry

        pair_loop(bucket_body, 0)
        lo_select(lambda ck: count16(lambda c, lanes: bk_ref[c, :, lanes] >= ck[:, lanes]),
                  lambda t: count16(lambda c, lanes: bk_ref[c, :, lanes] > t[:, lanes]))

    lo_u = sel_ref[0:1, :].astype(I32)
    thr_lo = (lo_u - HALF16).astype(I16)
    thr = ((hi_u - HALF16) << 16) | lo_u

    n_gt = n_above + sel_ref[1:2, :]
    n_eq = count16(lambda c, lanes: lo_ref[c, :, lanes] == thr_lo[:, lanes],
                   also=lambda c, lanes: hi_ref[c, :, lanes] == thr_hi[:, lanes])
    need = float(topk) - n_gt
    amb = jnp.logical_and(n_eq > need, thr > INT_MIN)
    any_amb = jnp.max(jnp.where(amb, 1.0, 0.0)) > 0.5

    @pl.when(any_amb)
    def _():
        def drop_body(c, seen):
            k = keys_ref[c]
            eq = k == thr
            eqf = jnp.where(eq, 1.0, 0.0)
            rank = jnp.dot(tri_ref[...], eqf.astype(BF16), preferred_element_type=F32) + seen
            drop = jnp.logical_and(jnp.logical_and(eq, rank > need), amb)
            keys_ref[c] = jnp.where(drop, INT_MIN, k)
            return seen + jnp.sum(eqf, axis=0, keepdims=True)

        pair_loop(drop_body, jnp.zeros((1, TQ), F32))

    thr_ref[...] = jnp.maximum(thr, INT_MIN + 1)

    def qk_all(c, lanes):
        kc = ak_ref[pl.ds(pl.multiple_of(c * CK, CK), CK), :]
        return [jnp.dot(_half(kc, h, 6), aqt_ref[h, :, lanes], preferred_element_type=F32) for h in range(N_HEADS)]

    _flash_loop(n_full, qk_all,
                lambda bias, h, s, lanes: s + bias,
                lambda c, h: avt_ref[c, h * VROWS:(h + 1) * VROWS, :],
                (s_ref, mx_ref, m_ref, acc_ref),
                prep=lambda c, lanes: jnp.where(keys_ref[c, :, lanes] >= thr_ref[:, lanes], 0.0, NEG),
                causal_tail=False)
    for h in range(N_HEADS):
        ot_ref[h * HEAD_DIM:(h + 1) * HEAD_DIM, :] = _softmax_out(acc_ref.at[h])
    o_ref[...] = ot_ref[...].T.astype(o_ref.dtype)


def _dsa(aq, ak, avt, iq, ik, iw):
    b, t, _ = aq.shape
    topk = min(TOPK_MAX, t // 4)
    qspec = pl.BlockSpec((None, TQ, BRANCH_W), lambda bb, i: (bb, i, 0))
    kspec, vspec = _kv_specs(t, BRANCH_W)
    pick = np.zeros((2 * SUBLANES, MXU_N), np.float32)
    for hh in range(IDX_HEADS):
        pick[hh, IDX_DIM + hh] = 1.0
    pick = jnp.asarray(pick, BF16)
    tri = jnp.asarray(np.tril(np.ones((CK, CK), np.float32)), BF16)
    kern = functools.partial(_dsa_kernel, topk=topk, idx_scale=(IDX_HEADS * IDX_DIM) ** -0.5)
    return pl.pallas_call(
        kern,
        out_shape=jax.ShapeDtypeStruct((b, t, BRANCH_W), BF16),
        grid=(b, t // TQ),
        in_specs=[qspec, kspec, vspec, qspec, kspec, qspec,
                  pl.BlockSpec(pick.shape, lambda bb, i: (0, 0)), pl.BlockSpec(tri.shape, lambda bb, i: (0, 0))],
        out_specs=qspec,
        scratch_shapes=[
            pltpu.VMEM((t // CK, CK, TQ), I32),
            pltpu.VMEM((t // CK, CK, TQ), I16),
            pltpu.VMEM((t // CK, CK, TQ), I16),
            pltpu.VMEM((t // CK, CK, TQ), I16),
            pltpu.VMEM((4, CK, TQ), I16),
            pltpu.VMEM((SUBLANES, TQ), F32),
            pltpu.VMEM((IDX_HEADS, IDX_DIM, TQ), BF16),
            pltpu.VMEM((N_HEADS, LANES, TQ), BF16),
            pltpu.VMEM((2 * SUBLANES, TQ), F32),
            pltpu.VMEM((1, TQ), I32),
        ] + _attn_scratch(N_HEADS),
        compiler_params=_cparams(2),
        name="dsa",
    )(aq, ak, avt, iq, ik, iw, pick, tri)


def _kbar_kernel(k_ref, o_ref):
    o_ref[...] = jnp.zeros(o_ref.shape, o_ref.dtype)
    nb = k_ref.shape[0] // MOBA_BLOCK
    for n in range(nb):
        blk = k_ref[n * MOBA_BLOCK:(n + 1) * MOBA_BLOCK, :].astype(F32)
        o_ref[n:n + 1, :] = jnp.mean(blk, axis=0, keepdims=True).astype(o_ref.dtype)


def _kbar(bk):
    b, t, w = bk.shape
    nbp = max(2 * SUBLANES, t // MOBA_BLOCK)
    return pl.pallas_call(
        _kbar_kernel,
        out_shape=jax.ShapeDtypeStruct((b, nbp, w), BF16),
        grid=(b,),
        in_specs=[pl.BlockSpec((None, t, w), lambda bb: (bb, 0, 0))],
        out_specs=pl.BlockSpec((None, nbp, w), lambda bb: (bb, 0, 0)),
        compiler_params=_cparams(1),
        name="moba_kbar",
    )(bk)


def _moba_kernel(q_ref, k_ref, vt_ref, kbar_ref, o_ref, qt_ref, bias_ref, s_ref, mx_ref, m_ref, acc_ref, ot_ref):
    i = pl.program_id(1)
    nbp = kbar_ref.shape[0]
    blk = lax.broadcasted_iota(I32, (nbp, TQ), 0)
    blk_f = blk.astype(F32)
    own = 2 * i + (lax.broadcasted_iota(I32, (nbp, TQ), 1) >> (MOBA_BLOCK.bit_length() - 1))
    _masked_qt(q_ref[...].astype(F32) * (HEAD_DIM ** -0.5 * LOG2E), 6, N_HEADS, qt_ref)

    for h in range(N_HEADS):
        g = jnp.where(blk < own, jnp.dot(_half(kbar_ref[...], h, 6), qt_ref[h], preferred_element_type=F32), NEG)
        bias = jnp.full((nbp, TQ), NEG, F32)
        for _ in range(MOBA_TOPK):
            mx = jnp.max(g, axis=0, keepdims=True)
            first = jnp.min(jnp.where(g == mx, blk_f, 1e9), axis=0, keepdims=True)
            pick = jnp.logical_and(blk_f == first, mx > 0.5 * NEG)
            bias = jnp.where(pick, 0.0, bias)
            g = jnp.where(pick, NEG, g)
        bias_ref[h] = jnp.where(blk == own, 0.0, bias)

    def qk_all(c, lanes):
        kc = k_ref[pl.ds(pl.multiple_of(c * CK, CK), CK), :]
        return [jnp.dot(_half(kc, h, 6), qt_ref[h, :, lanes], preferred_element_type=F32) for h in range(N_HEADS)]

    _flash_loop(2 * i, qk_all, lambda c, h, s, lanes: s + bias_ref[h, pl.ds(c, 1), lanes],
                lambda c, h: vt_ref[c, h * VROWS:(h + 1) * VROWS, :], (s_ref, mx_ref, m_ref, acc_ref))
    for h in range(N_HEADS):
        ot_ref[h * HEAD_DIM:(h + 1) * HEAD_DIM, :] = _softmax_out(acc_ref.at[h])
    o_ref[...] = ot_ref[...].T.astype(o_ref.dtype)


def _moba(bq, bk, bvt, kbar):
    b, t, w = bq.shape
    assert TQ == 2 * MOBA_BLOCK and CK == MOBA_BLOCK and t % TQ == 0
    nbp = kbar.shape[1]
    qspec = pl.BlockSpec((None, TQ, w), lambda bb, i: (bb, i, 0))
    kspec, vspec = _kv_specs(t, w)
    return pl.pallas_call(
        _moba_kernel,
        out_shape=jax.ShapeDtypeStruct((b, t, w), BF16),
        grid=(b, t // TQ),
        in_specs=[qspec, kspec, vspec, pl.BlockSpec((None, nbp, w), lambda bb, i: (bb, 0, 0))],
        out_specs=qspec,
        scratch_shapes=[pltpu.VMEM((N_HEADS, LANES, TQ), BF16), pltpu.VMEM((N_HEADS, nbp, TQ), F32)]
        + _attn_scratch(N_HEADS),
        compiler_params=_cparams(2),
        name="moba",
    )(bq, bk, bvt, kbar)


def _diff_kernel(q_ref, k_ref, vt_ref, lam_ref, norm_ref, misc_ref, o_ref,
                 qt_ref, s_ref, mx_ref, m_ref, acc_ref, ot_ref):
    i = pl.program_id(1)
    _masked_qt(q_ref[...].astype(F32) * (DIFF_DIM ** -0.5 * LOG2E), 5, 2 * N_HEADS, qt_ref)

    dl = lam_ref[...]
    lam_init = misc_ref[0:1, 0:1]
    lam = (jnp.exp(jnp.sum(dl[0:1, :] * dl[1:2, :], axis=1, keepdims=True))
           - jnp.exp(jnp.sum(dl[2:3, :] * dl[3:4, :], axis=1, keepdims=True)) + lam_init)

    def qk_all(c, lanes):
        kc = k_ref[pl.ds(pl.multiple_of(c * CK, CK), CK), :]
        return [jnp.dot(_half(kc, j, 5), qt_ref[j, :, lanes], preferred_element_type=F32) for j in range(2 * N_HEADS)]

    _flash_loop(2 * i, qk_all, None,
                lambda c, j: vt_ref[c, (j // 2) * VROWS:(j // 2 + 1) * VROWS, :],
                (s_ref, mx_ref, m_ref, acc_ref))

    post = norm_ref[...] * (1.0 - lam_init)
    for h in range(N_HEADS):
        o_h = _softmax_out(acc_ref.at[2 * h]) - lam * _softmax_out(acc_ref.at[2 * h + 1])
        ms = jnp.mean(o_h * o_h, axis=0, keepdims=True)
        ot_ref[h * HEAD_DIM:(h + 1) * HEAD_DIM, :] = o_h * lax.rsqrt(ms + RMS_EPS) * post
    o_ref[...] = ot_ref[...].T.astype(o_ref.dtype)


def _diff(cq, ck, cvt, lam, norm, misc):
    b, t, w = cq.shape
    qspec = pl.BlockSpec((None, TQ, w), lambda bb, i: (bb, i, 0))
    kspec, vspec = _kv_specs(t, w)
    full = lambda a: pl.BlockSpec(a.shape, lambda bb, i: (0,) * a.ndim)
    return pl.pallas_call(
        _diff_kernel,
        out_shape=jax.ShapeDtypeStruct((b, t, w), BF16),
        grid=(b, t // TQ),
        in_specs=[qspec, kspec, vspec, full(lam), full(norm), full(misc)],
        out_specs=qspec,
        scratch_shapes=[pltpu.VMEM((2 * N_HEADS, LANES, TQ), BF16)] + _attn_scratch(2 * N_HEADS),
        compiler_params=_cparams(2),
        name="diff",
    )(cq, ck, cvt, lam, norm, misc)


def _mla_prep_kernel(cq_ref, ckv_ref, kr_ref, qn_ref, kvn_ref, wq_ref, wqr_ref, wk_ref, wvt_ref,
                     p_ref, ct_ref, st_ref, q_out, k_out, vt_out):
    x = cq_ref[...].astype(F32)
    xn = (x * lax.rsqrt(jnp.mean(x * x, axis=1, keepdims=True) + RMS_EPS) * qn_ref[...]).astype(BF16)
    q = (jnp.dot(xn, wq_ref[...], preferred_element_type=F32) * ct_ref[...]
         + jnp.dot(xn, wqr_ref[...], preferred_element_type=F32) * st_ref[...])
    q_out[...] = q.astype(q_out.dtype)
    c = ckv_ref[:, :KV_LORA].astype(F32)
    cn = (c * lax.rsqrt(jnp.mean(c * c, axis=1, keepdims=True) + RMS_EPS) * kvn_ref[...]).astype(BF16)
    k = (jnp.dot(cn, wk_ref[...], preferred_element_type=F32)
         + jnp.dot(kr_ref[...], p_ref[...], preferred_element_type=F32))
    k_out[...] = k.astype(k_out.dtype)
    _store_vt(vt_out, _tn_dot(wvt_ref[...], cn))


def _mla_prep(dcq, ckv, kr, qn, kvn, wq, wqr, wk, wvt, pmat, ct, st, l):
    b, t, _ = dcq.shape
    tm = 512
    hw = N_HEADS * LANES
    row = lambda w: pl.BlockSpec((None, tm, w), lambda i, bb: (bb, i, 0))
    full = lambda a: pl.BlockSpec(a.shape, lambda i, bb: (0,) * a.ndim)
    tab = pl.BlockSpec((tm, hw), lambda i, bb: (i, 0))
    return pl.pallas_call(
        _mla_prep_kernel,
        out_shape=(jax.ShapeDtypeStruct((b, t, hw), BF16), jax.ShapeDtypeStruct((b, t, hw), BF16),
                   jax.ShapeDtypeStruct((b, t // CK, N_HEADS * VROWS, CK), BF16)),
        grid=(t // tm, b),
        in_specs=[row(Q_LORA), row(MXU_N), row(MXU_N), full(qn), full(kvn), _layer_spec(wq, l), _layer_spec(wqr, l),
                  _layer_spec(wk, l), _layer_spec(wvt, l), full(pmat), tab, tab],
        out_specs=(row(hw), row(hw),
                   pl.BlockSpec((None, tm // CK, N_HEADS * VROWS, CK), lambda i, bb: (bb, i, 0, 0))),
        compiler_params=_cparams(2),
        name="mla_prep",
    )(dcq, ckv, kr, qn, kvn, wq, wqr, wk, wvt, pmat, ct, st)


def _mla_kernel(q_ref, k_ref, vt_ref, o_ref, qt_ref, s_ref, mx_ref, m_ref, acc_ref, ot_ref):
    i = pl.program_id(1)
    hs = [slice(h * LANES, (h + 1) * LANES) for h in range(N_HEADS)]
    for h in range(N_HEADS):
        qt_ref[h] = q_ref[:, hs[h]].astype(F32).T.astype(BF16)

    def qk_all(c, lanes):
        start = pl.multiple_of(c * CK, CK)
        return [jnp.dot(k_ref[pl.ds(start, CK), hs[h]], qt_ref[h, :, lanes], preferred_element_type=F32)
                for h in range(N_HEADS)]

    _flash_loop(2 * i, qk_all, None,
                lambda c, h: vt_ref[c, h * VROWS:(h + 1) * VROWS, :],
                (s_ref, mx_ref, m_ref, acc_ref))
    for h in range(N_HEADS):
        ot_ref[h * HEAD_DIM:(h + 1) * HEAD_DIM, :] = _softmax_out(acc_ref.at[h])
    o_ref[...] = ot_ref[...].T.astype(o_ref.dtype)


def _mla(qm, km, vmt):
    b, t, hw = qm.shape
    kspec, vspec = _kv_specs(t, hw)
    return pl.pallas_call(
        _mla_kernel,
        out_shape=jax.ShapeDtypeStruct((b, t, BRANCH_W), BF16),
        grid=(b, t // TQ),
        in_specs=[pl.BlockSpec((None, TQ, hw), lambda bb, i: (bb, i, 0)), kspec, vspec],
        out_specs=pl.BlockSpec((None, TQ, BRANCH_W), lambda bb, i: (bb, i, 0)),
        scratch_shapes=[pltpu.VMEM((N_HEADS, LANES, TQ), BF16)] + _attn_scratch(N_HEADS),
        compiler_params=_cparams(2),
        name="mla",
    )(qm, km, vmt)


def _mem_kv_kernel(x_ref, w_ref, k_ref, vt_ref):
    x = x_ref[...].astype(BF16)
    k_ref[...] = jnp.dot(x, w_ref[:, :BRANCH_W], preferred_element_type=F32).astype(k_ref.dtype)
    _store_vt(vt_ref, _tn_dot(w_ref[:, BRANCH_W:], x))


def _mem_kv(mem, w, l):
    b, m, d = mem.shape
    assert m % CK == 0
    return pl.pallas_call(
        _mem_kv_kernel,
        out_shape=(jax.ShapeDtypeStruct((b, m, BRANCH_W), BF16),
                   jax.ShapeDtypeStruct((b, m // CK, N_HEADS * VROWS, CK), BF16)),
        grid=(b,),
        in_specs=[pl.BlockSpec((None, m, d), lambda bb: (bb, 0, 0)), _layer_spec(w, l)],
        out_specs=(pl.BlockSpec((None, m, BRANCH_W), lambda bb: (bb, 0, 0)),
                   pl.BlockSpec((None, m // CK, N_HEADS * VROWS, CK), lambda bb: (bb, 0, 0, 0))),
        compiler_params=_cparams(1),
        name="mem_kv",
    )(mem, w)


def _mem_kernel(q_ref, k_ref, vt_ref, o_ref, qt_ref, ot_ref):
    _masked_qt(q_ref[...].astype(F32) * (HEAD_DIM ** -0.5 * LOG2E), 6, N_HEADS, qt_ref)
    s_all = [jnp.dot(_half(k_ref[...], h, 6), qt_ref[h], preferred_element_type=F32) for h in range(N_HEADS)]
    for h in range(N_HEADS):
        s_t = s_all[h]
        p = jnp.exp2(s_t - jnp.max(s_t, axis=0, keepdims=True)).astype(BF16)
        acc = jnp.dot(vt_ref[0, h * VROWS:(h + 1) * VROWS, :], p, preferred_element_type=F32)
        ot_ref[h * HEAD_DIM:(h + 1) * HEAD_DIM, :] = acc[:HEAD_DIM, :] / acc[HEAD_DIM:HEAD_DIM + 1, :]
    o_ref[...] = ot_ref[...].T.astype(o_ref.dtype)


def _mem_attn(eq, mk, mvt):
    b, t, w = eq.shape
    m = mk.shape[1]
    assert m == CK
    return pl.pallas_call(
        _mem_kernel,
        out_shape=jax.ShapeDtypeStruct((b, t, w), BF16),
        grid=(b, t // TQ),
        in_specs=[pl.BlockSpec((None, TQ, w), lambda bb, i: (bb, i, 0)),
                  pl.BlockSpec((None, m, w), lambda bb, i: (bb, 0, 0)),
                  pl.BlockSpec((None,) + mvt.shape[1:], lambda bb, i: (bb, 0, 0, 0))],
        out_specs=pl.BlockSpec((None, TQ, w), lambda bb, i: (bb, i, 0)),
        scratch_shapes=[pltpu.VMEM((N_HEADS, LANES, TQ), BF16), pltpu.VMEM((BRANCH_W, TQ), F32)],
        compiler_params=_cparams(2),
        name="mem_attn",
    )(eq, mk, mvt)


def _final_kernel(h_ref, hb_ref, oa_ref, ob_ref, oc_ref, od_ref, oe_ref, z_ref,
                  wg_ref, wb_ref, wo_ref, g_ref, b_ref, h_out, hb_out, acc_ref, *, alpha):
    d = h_ref.shape[1]
    half = h_ref.shape[0] // 2
    for n, o_ref in enumerate((oa_ref, ob_ref, oc_ref, od_ref, oe_ref)):
        for r in range(2):
            rows = slice(r * half, (r + 1) * half)
            z = z_ref[rows, n * BRANCH_W:(n + 1) * BRANCH_W].astype(F32)
            y = o_ref[rows, :].astype(F32) * (z / (1.0 + jnp.exp(-z)))
            u = jnp.dot(y.astype(BF16), wb_ref[n], preferred_element_type=F32)
            g = jnp.dot(hb_ref[rows, :], wg_ref[:, n * d:(n + 1) * d], preferred_element_type=F32)
            t = u / (1.0 + jnp.exp(-g))
            acc_ref[rows, :] = t if n == 0 else acc_ref[rows, :] + t
    for r in range(2):
        rows = slice(r * half, (r + 1) * half)
        out = jnp.dot(acc_ref[rows, :].astype(BF16), wo_ref[...], preferred_element_type=F32)
        x = alpha * h_ref[rows, :] + out
        mu = jnp.mean(x, axis=1, keepdims=True)
        xc = x - mu
        var = jnp.mean(xc * xc, axis=1, keepdims=True)
        y = xc * lax.rsqrt(var + LN_EPS) * g_ref[...] + b_ref[...]
        h_out[rows, :] = y
        hb_out[rows, :] = y.astype(BF16)


def _final(h, hb, os5, z, wg, wb, wo, ln_g, ln_b, alpha, l):
    n, d = h.shape
    tm = 512
    row = lambda w: pl.BlockSpec((tm, w), lambda i: (i, 0))
    full = lambda a: pl.BlockSpec(a.shape, lambda i: (0,) * a.ndim)
    return pl.pallas_call(
        functools.partial(_final_kernel, alpha=alpha),
        out_shape=(jax.ShapeDtypeStruct((n, d), F32), jax.ShapeDtypeStruct((n, d), BF16)),
        grid=(n // tm,),
        in_specs=[row(d), row(d)] + [row(BRANCH_W)] * N_BRANCH + [row(N_BRANCH * BRANCH_W),
                  _layer_spec(wg, l), _layer_spec(wb, l), _layer_spec(wo, l), full(ln_g), full(ln_b)],
        out_specs=(row(d), row(d)),
        scratch_shapes=[pltpu.VMEM((tm, d), F32)],
        compiler_params=_cparams(1),
        name="merge_out_ln",
    )(h, hb, *os5, z, wg, wb, wo, ln_g, ln_b)


ROPE_GROUPS = (("a_q", N_HEADS, HEAD_DIM, ROT_64), ("a_k", N_HEADS, HEAD_DIM, ROT_64),
               ("i_q", IDX_HEADS, IDX_DIM, ROT_32), ("i_k", 1, MXU_N, ROT_32),
               ("b_q", N_HEADS, HEAD_DIM, ROT_64), ("b_k", N_HEADS, HEAD_DIM, ROT_64),
               ("c_q", 2 * N_HEADS, DIFF_DIM, ROT_32), ("c_k", 2 * N_HEADS, DIFF_DIM, ROT_32),
               ("d_kr", 1, MXU_N, MLA_ROPE))
PLAIN_COLS = ("d_cq", "d_ckv", "e_q") + tuple(("z", j) for j in range(N_BRANCH))
VALUE_COLS = ("a_v", "b_v", "c_v")
GATE_COLS = tuple(("g", j) for j in range(OFF["g"][1] // MXU_N))


def _window_start(col):
    name, j = col if isinstance(col, tuple) else (col, 0)
    return OFF[name][0] + j * MXU_N


def _weight_prep_kernel(offs_ref, wt_ref, o_ref):
    o_ref[...] = wt_ref[...].T.astype(o_ref.dtype)


def _weight_windows(wt, cols, name):
    depth, n, d = wt.shape
    starts = [_window_start(c) for c in cols]
    assert all(st % SUBLANES == 0 and st + MXU_N <= n for st in starts)
    grid_spec = pltpu.PrefetchScalarGridSpec(
        num_scalar_prefetch=1,
        grid=(depth, len(cols)),
        in_specs=[pl.BlockSpec((None, pl.Element(MXU_N), pl.Element(d)),
                               lambda l, j, offs: (l, pl.multiple_of(offs[j], SUBLANES), 0))],
        out_specs=pl.BlockSpec((None, d, MXU_N), lambda l, j, offs: (l, 0, j)),
    )
    return pl.pallas_call(
        _weight_prep_kernel,
        out_shape=jax.ShapeDtypeStruct((depth, d, MXU_N * len(cols)), BF16),
        grid_spec=grid_spec,
        compiler_params=_cparams(2),
        name=name,
    )(jnp.asarray(np.asarray(starts, np.int32)), wt)


def _weight_prep(w_in):
    wt = jnp.swapaxes(w_in, 1, 2)
    return (_weight_windows(wt, PLAIN_COLS, "wprep_plain"), _weight_windows(wt, VALUE_COLS, "wprep_value"),
            _weight_windows(wt, [name for name, *_ in ROPE_GROUPS], "wprep_rope"),
            _weight_windows(wt, GATE_COLS, "wprep_gate"))


def _rope_tables(seq, rot_dim):
    pos = jnp.arange(seq, dtype=F32)
    inv = ROPE_THETA ** (-jnp.arange(0, rot_dim, 2, dtype=F32) / rot_dim)
    ang = pos[:, None] * inv[None, :]
    return jnp.cos(ang), jnp.sin(ang)


def _rope_cs(t, nh, hd, r):
    cos, sin = _rope_tables(t, r)
    c = jnp.concatenate([cos, cos, jnp.ones((t, hd - r), F32)], axis=1)
    s = jnp.concatenate([-sin, sin, jnp.zeros((t, hd - r), F32)], axis=1)
    return jnp.tile(c, (1, nh)), jnp.tile(s, (1, nh))


def kernel(x, mem, ln0_g, ln0_b, w_in, mla_q_norm, w_uq, mla_kv_norm, w_ukv, diff_lam, diff_norm,
           w_mem_kv, w_branch, w_out, ln_g, ln_b):
    b, t, d = x.shape
    depth = w_in.shape[0]
    alpha = (2 * depth) ** 0.25
    assert t % 512 == 0 and d == 1024

    w_plain, w_vt, w_rope, wg = _weight_prep(w_in)
    plain_widths = (BRANCH_W,) * 3 + (N_BRANCH * BRANCH_W,)
    rope_heads = tuple((hd, r // 2) for _, _, hd, r in ROPE_GROUPS)
    patterns = sorted(set((nh, hd, r) for _, nh, hd, r in ROPE_GROUPS))
    rope_tables = tuple(patterns.index((nh, hd, r)) for _, nh, hd, r in ROPE_GROUPS)
    cs = [_rope_cs(t, nh, hd, r) for nh, hd, r in patterns]
    ctab = jnp.stack([c for c, _ in cs])
    stab = jnp.stack([s for _, s in cs])

    uq = w_uq.reshape(depth, Q_LORA, N_HEADS, MLA_NOPE + MLA_ROPE)
    qn_w, qr_w = uq[..., :MLA_NOPE], uq[..., MLA_NOPE:]
    pad32 = jnp.zeros((depth, Q_LORA, N_HEADS, LANES - MLA_NOPE - MLA_ROPE), w_uq.dtype)
    hw = N_HEADS * LANES
    wq = jnp.concatenate([qn_w, qr_w, pad32], axis=-1).reshape(depth, Q_LORA, hw).astype(BF16)
    half = MLA_ROPE // 2
    wq_rot = jnp.concatenate([jnp.zeros_like(qn_w), -qr_w[..., half:], qr_w[..., :half], pad32],
                             axis=-1).reshape(depth, Q_LORA, hw).astype(BF16)
    cos_m, sin_m = _rope_tables(t, MLA_ROPE)
    one = lambda n: jnp.ones((t, n), F32)
    zer = lambda n: jnp.zeros((t, n), F32)
    qs = (MLA_NOPE + MLA_ROPE) ** -0.5 * LOG2E
    ct_q = qs * jnp.tile(jnp.concatenate([one(MLA_NOPE), cos_m, cos_m, one(LANES - MLA_NOPE - MLA_ROPE)], axis=1), (1, N_HEADS))
    st_q = qs * jnp.tile(jnp.concatenate([zer(MLA_NOPE), sin_m, sin_m, zer(LANES - MLA_NOPE - MLA_ROPE)], axis=1), (1, N_HEADS))
    ukv = w_ukv.reshape(depth, KV_LORA, N_HEADS, MLA_NOPE + MLA_V)
    wk = jnp.concatenate([ukv[..., :MLA_NOPE], jnp.zeros((depth, KV_LORA, N_HEADS, LANES - MLA_NOPE), w_ukv.dtype)],
                         axis=-1).reshape(depth, KV_LORA, hw).astype(BF16)
    wvt = ukv[..., MLA_NOPE:].reshape(depth, KV_LORA, N_HEADS * MLA_V).astype(BF16)
    place = np.zeros((MXU_N, hw), np.float32)
    for hh in range(N_HEADS):
        for j in range(MLA_ROPE):
            place[j, hh * LANES + MLA_NOPE + j] = 1.0
    place = jnp.asarray(place, BF16)

    wb = w_branch.astype(BF16)
    wo = w_out.astype(BF16)
    wmem = w_mem_kv.astype(BF16)
    norm_t = jnp.broadcast_to(diff_norm.astype(F32)[:, :, None], (depth, HEAD_DIM, TQ))

    h, hb = _layer_norm0(x.reshape(b * t, d), ln0_g, ln0_b)
    for l in range(depth):
        hb3 = hb.reshape(b, t, d)
        avt, bvt, cvt, dcq, ckv_iw, eq, z = _proj_plain(hb3, w_plain, w_vt, plain_widths, l)
        aq, ak, iq, ik, bq, bk, cq, ck, kr = _proj_rope(hb3, w_rope, ctab, stab, rope_heads, rope_tables, l)

        o_a = _dsa(aq, ak, avt, iq, ik, ik)
        o_b = _moba(bq, bk, bvt, _kbar(bk))
        lam_init = 0.8 - 0.6 * math.exp(-0.3 * l)
        misc = jnp.full((SUBLANES, LANES), lam_init, F32)
        o_c = _diff(cq, ck, cvt, diff_lam[l].astype(F32), norm_t[l], misc)
        qm, km, vmt = _mla_prep(dcq, ckv_iw, kr, mla_q_norm[l].reshape(1, Q_LORA), mla_kv_norm[l].reshape(1, KV_LORA),
                                wq, wq_rot, wk, wvt, place, ct_q, st_q, l)
        o_d = _mla(qm, km, vmt)
        o_e = _mem_attn(eq, *_mem_kv(mem, wmem, l))

        os5 = [o.reshape(b * t, BRANCH_W) for o in (o_a, o_b, o_c, o_d, o_e)]
        h, hb = _final(h, hb, os5, z.reshape(b * t, N_BRANCH * BRANCH_W), wg, wb, wo,
                       ln_g[l].reshape(1, d), ln_b[l].reshape(1, d), alpha, l)
    return h.reshape(b, t, d)
```

```python
import functools
import math

import numpy as np
import jax
import jax.numpy as jnp
from jax import lax
from jax.experimental import pallas as pl
from jax.experimental.pallas import tpu as pltpu

F32 = jnp.float32
BF16 = jnp.bfloat16
I32 = jnp.int32
I16 = jnp.int16

N_HEADS = 4
HEAD_DIM = 64
BRANCH_W = N_HEADS * HEAD_DIM
N_BRANCH = 5
ROPE_THETA = 500000.0
ROT_64 = 16
ROT_32 = 8
IDX_HEADS = 8
IDX_DIM = 32
TOPK_MAX = 256
MOBA_BLOCK = 256
MOBA_TOPK = 3
DIFF_DIM = 32
Q_LORA = 256
KV_LORA = 128
MLA_NOPE = 64
MLA_ROPE = 32
MLA_V = 64
LN_EPS = 1e-5
RMS_EPS = 1e-6

IN_LAYOUT = (
    ("a_q", BRANCH_W), ("a_k", BRANCH_W), ("a_v", BRANCH_W),
    ("i_q", IDX_HEADS * IDX_DIM), ("i_k", IDX_DIM), ("i_w", IDX_HEADS),
    ("b_q", BRANCH_W), ("b_k", BRANCH_W), ("b_v", BRANCH_W),
    ("c_q", BRANCH_W), ("c_k", BRANCH_W), ("c_v", BRANCH_W),
    ("d_cq", Q_LORA), ("d_ckv", KV_LORA), ("d_kr", MLA_ROPE),
    ("e_q", BRANCH_W),
    ("z", N_BRANCH * BRANCH_W),
    ("g", N_BRANCH * 1024),
)

SUBLANES = 8
LANES = 128
MXU_N = 256
TQ = 512
CK = 256
VROWS = HEAD_DIM + 16
FLASH_UNROLL = 4
NEG = -1e30
LOG2E = math.log2(math.e)
INT_MIN = np.int32(-2 ** 31)
HALF16 = 1 << 15
VMEM_LIMIT = 56 * 1024 * 1024


def _offsets():
    off, out = 0, {}
    for name, size in IN_LAYOUT:
        out[name] = (off, size)
        off += size
    return out


OFF = _offsets()


def _nt_dot(a, b):
    return lax.dot_general(a, b, (((1,), (1,)), ((), ())), preferred_element_type=F32)


def _tn_dot(w, x):
    return lax.dot_general(w, x, (((0,), (1,)), ((), ())), preferred_element_type=F32)


def _fold_rows(w, rows=SUBLANES):
    xs = [w[r:r + rows, :] for r in range(0, w.shape[0], rows)]
    while len(xs) > 1:
        xs = [xs[j] + xs[j + 1] for j in range(0, len(xs) - 1, 2)] + ([xs[-1]] if len(xs) % 2 else [])
    return xs[0]


def _masked_qt(q, shift, n, qt_ref):
    qt = q.T
    dim = lax.broadcasted_iota(I32, (LANES, qt.shape[1]), 0)
    for j in range(n):
        half = (j << shift) // LANES
        rows = qt[half * LANES:(half + 1) * LANES, :]
        qt_ref[j] = jnp.where(((dim + half * LANES) >> shift) == j, rows, 0.0).astype(BF16)


def _half(kc, j, shift):
    half = (j << shift) // LANES
    return kc[:, half * LANES:(half + 1) * LANES]


def _cparams(n_axes):
    return pltpu.CompilerParams(dimension_semantics=("arbitrary",) * n_axes,
                                vmem_limit_bytes=VMEM_LIMIT)


def _layer_spec(a, l):
    return pl.BlockSpec((None,) + a.shape[1:], lambda *_: (l,) + (0,) * (a.ndim - 1))


def _softmax_step(s_t, m_tile, vt_h, m_ref, acc_ref):
    m_old = m_ref[...]
    m_new = jnp.maximum(m_old, m_tile)
    alpha = jnp.exp2(m_old - m_new)
    p = jnp.exp2(s_t - m_new)
    acc_ref[...] = alpha * acc_ref[...] + jnp.dot(vt_h, p.astype(BF16), preferred_element_type=F32)
    m_ref[...] = m_new


def _softmax_init(m_ref, acc_ref):
    m_ref[...] = jnp.full(m_ref.shape, NEG, F32)
    acc_ref[...] = jnp.zeros(acc_ref.shape, F32)


def _softmax_out(acc_ref):
    return acc_ref[:HEAD_DIM, :] / acc_ref[HEAD_DIM:HEAD_DIM + 1, :]


def _store_vt(o_ref, vt):
    ones = jnp.ones((VROWS - HEAD_DIM, CK), o_ref.dtype)
    for j in range(o_ref.shape[0]):
        for h in range(N_HEADS):
            o_ref[j, h * VROWS:h * VROWS + HEAD_DIM, :] = (
                vt[h * HEAD_DIM:(h + 1) * HEAD_DIM, j * CK:(j + 1) * CK].astype(o_ref.dtype))
            o_ref[j, h * VROWS + HEAD_DIM:(h + 1) * VROWS, :] = ones


def _flash_loop(n_full, qk_all, mask, vt_rows, state, prep=None, causal_tail=True):
    s_ref, mx_ref, m_ref, acc_ref = state
    n_state = m_ref.shape[0]
    for j in range(n_state):
        _softmax_init(m_ref.at[j], acc_ref.at[j])

    def lanes_of(d):
        return slice(CK, TQ) if d == 1 else slice(None)

    def park(c, slot, d=None):
        lanes = lanes_of(d)
        ctx = c if prep is None else prep(c, lanes)
        for j, s in enumerate(qk_all(c, lanes)):
            if mask is not None:
                s = mask(ctx, j, s, lanes)
            if d is not None and causal_tail:
                s = jnp.where(_causal(d), s, NEG)
            s_ref[slot, j, :, lanes] = s
            mx_ref[slot, j, :, lanes] = jnp.max(s, axis=0, keepdims=True)

    def consume(c, slot, d=None):
        lanes = lanes_of(d)
        for j in range(n_state):
            _softmax_step(s_ref[slot, j, :, lanes], mx_ref[slot, j, :, lanes], vt_rows(c, j),
                          m_ref.at[j, :, lanes], acc_ref.at[j, :, lanes])

    def pair(c):
        park(c + 1, 1)
        consume(c, 0)
        park(c + 2, 0)
        consume(c + 1, 1)

    def body(g, carry):
        for u in range(0, FLASH_UNROLL, 2):
            pair(FLASH_UNROLL * g + u)
        return carry

    @pl.when(n_full == 0)
    def _():
        park(0, 0, d=0)
        park(1, 1, d=1)
        consume(0, 0)
        consume(1, 1, d=1)

    @pl.when(n_full > 0)
    def _():
        park(0, 0)
        n_loop = n_full - 2
        n_group = lax.shift_right_logical(n_loop, FLASH_UNROLL.bit_length() - 1)
        lax.fori_loop(0, n_group, body, 0)
        c0 = FLASH_UNROLL * n_group
        for u in range(FLASH_UNROLL // 2 - 1):
            @pl.when(n_loop - c0 >= 2 * (u + 1))
            def _(u=u):
                pair(c0 + 2 * u)
        c = n_loop
        park(c + 1, 1)
        consume(c, 0)
        park(c + 2, 0, d=0)
        consume(c + 1, 1)
        park(c + 3, 1, d=1)
        consume(c + 2, 0)
        consume(c + 3, 1, d=1)


def _causal(d):
    shape = (CK, TQ - d * CK)
    return lax.broadcasted_iota(I32, shape, 0) <= lax.broadcasted_iota(I32, shape, 1)


def _attn_scratch(n_state):
    return [pltpu.VMEM((2, n_state, CK, TQ), F32), pltpu.VMEM((2, n_state, 1, TQ), F32),
            pltpu.VMEM((n_state, 1, TQ), F32), pltpu.VMEM((n_state, VROWS, TQ), F32),
            pltpu.VMEM((BRANCH_W, TQ), F32)]


def _kv_specs(t, w):
    kspec = pl.BlockSpec((None, t, w), lambda bb, i: (bb, 0, 0))
    vspec = pl.BlockSpec((None, t // CK, N_HEADS * VROWS, CK), lambda bb, i: (bb, 0, 0, 0))
    return kspec, vspec


def _ln_kernel(x_ref, g_ref, b_ref, h_ref, hb_ref):
    x = x_ref[...]
    mu = jnp.mean(x, axis=1, keepdims=True)
    xc = x - mu
    var = jnp.mean(xc * xc, axis=1, keepdims=True)
    y = xc * lax.rsqrt(var + LN_EPS) * g_ref[...] + b_ref[...]
    h_ref[...] = y
    hb_ref[...] = y.astype(BF16)


def _layer_norm0(x2, g, b):
    n, d = x2.shape
    tm = 512
    row = pl.BlockSpec((tm, d), lambda i: (i, 0))
    vec = pl.BlockSpec((1, d), lambda i: (0, 0))
    return pl.pallas_call(
        _ln_kernel,
        out_shape=(jax.ShapeDtypeStruct((n, d), F32), jax.ShapeDtypeStruct((n, d), BF16)),
        grid=(n // tm,),
        in_specs=[row, vec, vec],
        out_specs=(row, row),
        compiler_params=_cparams(1),
        name="ln0",
    )(x2, g.reshape(1, d), b.reshape(1, d))


def _proj_plain_kernel(x_ref, w_ref, wt_ref, *out_refs, n_t):
    for g, o_ref in enumerate(out_refs[:n_t]):
        _store_vt(o_ref, _tn_dot(wt_ref[:, g * BRANCH_W:(g + 1) * BRANCH_W], x_ref[...]))
    off = 0
    for o_ref in out_refs[n_t:]:
        wd = o_ref.shape[-1]
        for j in range(0, wd, MXU_N):
            acc = jnp.dot(x_ref[...], w_ref[:, off + j:off + j + MXU_N], preferred_element_type=F32)
            o_ref[:, j:j + MXU_N] = acc.astype(o_ref.dtype)
        off += wd


def _proj_plain(hb3, w, wt, widths, l):
    b, t, d = hb3.shape
    tm = 512
    n_t = wt.shape[-1] // BRANCH_W
    shapes = [jax.ShapeDtypeStruct((b, t // CK, N_HEADS * VROWS, CK), BF16)] * n_t
    specs = [pl.BlockSpec((None, tm // CK, N_HEADS * VROWS, CK), lambda i, bb: (bb, i, 0, 0))] * n_t
    shapes += [jax.ShapeDtypeStruct((b, t, wd), BF16) for wd in widths]
    specs += [pl.BlockSpec((None, tm, wd), lambda i, bb: (bb, i, 0)) for wd in widths]
    return pl.pallas_call(
        functools.partial(_proj_plain_kernel, n_t=n_t),
        out_shape=tuple(shapes),
        grid=(t // tm, b),
        in_specs=[pl.BlockSpec((None, tm, d), lambda i, bb: (bb, i, 0)),
                  _layer_spec(w, l), _layer_spec(wt, l)],
        out_specs=tuple(specs),
        compiler_params=_cparams(2),
        name="proj_plain",
    )(hb3, w, wt)


def _proj_rope_kernel(x_ref, w_ref, c_ref, s_ref, *out_refs, heads, tables):
    lane = lax.broadcasted_iota(I32, (x_ref.shape[0], MXU_N), 1)
    for g, o_ref in enumerate(out_refs):
        hd, half = heads[g]
        sl = slice(g * MXU_N, (g + 1) * MXU_N)
        acc = jnp.dot(x_ref[...], w_ref[:, sl], preferred_element_type=F32)
        partner = jnp.where((lane & (hd - 1)) < half,
                            pltpu.roll(acc, MXU_N - half, 1), pltpu.roll(acc, half, 1))
        o_ref[...] = (acc * c_ref[tables[g]] + partner * s_ref[tables[g]]).astype(o_ref.dtype)


def _proj_rope(hb3, w, ctab, stab, heads, tables, l):
    b, t, d = hb3.shape
    tm = 512
    assert w.shape[-1] == MXU_N * len(heads)
    tspec = pl.BlockSpec((ctab.shape[0], tm, MXU_N), lambda i, bb: (0, i, 0))
    ospec = pl.BlockSpec((None, tm, MXU_N), lambda i, bb: (bb, i, 0))
    return pl.pallas_call(
        functools.partial(_proj_rope_kernel, heads=heads, tables=tables),
        out_shape=(jax.ShapeDtypeStruct((b, t, MXU_N), BF16),) * len(heads),
        grid=(t // tm, b),
        in_specs=[pl.BlockSpec((None, tm, d), lambda i, bb: (bb, i, 0)),
                  _layer_spec(w, l), tspec, tspec],
        out_specs=(ospec,) * len(heads),
        compiler_params=_cparams(2),
        name="proj_rope",
    )(hb3, w, ctab, stab)


def _dsa_kernel(aq_ref, ak_ref, avt_ref, iq_ref, ik_ref, iw_ref, pick_ref, tri_ref, o_ref,
                keys_ref, hi_ref, lo_ref, bk_ref, top_ref, sel_ref, iqt_ref, aqt_ref, wt_ref, thr_ref, s_ref, mx_ref, m_ref, acc_ref, ot_ref,
                *, topk, idx_scale):
    i = pl.program_id(1)
    n_full = 2 * i

    iqt = iq_ref[...].astype(F32).T
    for hh in range(IDX_HEADS):
        iqt_ref[hh] = iqt[hh * IDX_DIM:(hh + 1) * IDX_DIM, :].astype(BF16)
    _masked_qt(aq_ref[...].astype(F32) * (HEAD_DIM ** -0.5 * LOG2E), 6, N_HEADS, aqt_ref)
    wt_ref[...] = _nt_dot(pick_ref[...], iw_ref[...]) * idx_scale

    def lanes_of(d):
        return slice(CK, TQ) if d == 1 else slice(None)

    def logits(c, d):
        kc = ik_ref[pl.ds(pl.multiple_of(c * CK, CK), CK), :]
        return [jnp.dot(kc[:, :IDX_DIM], iqt_ref[hh, :, lanes_of(d)], preferred_element_type=F32)
                for hh in range(IDX_HEADS)]

    def put_keys(c, key, lanes):
        keys_ref[c, :, lanes] = key
        hi_ref[c, :, lanes] = (key >> 16).astype(I16)
        lo_ref[c, :, lanes] = ((key & 0xFFFF) - HALF16).astype(I16)

    def score_chunk(c, lg, d):
        lanes = lanes_of(d)
        sc = jnp.zeros(lg[0].shape, F32)
        for hh in range(IDX_HEADS):
            sc = sc + jnp.maximum(lg[hh], 0.0) * wt_ref[hh:hh + 1, lanes]
        bits = pltpu.bitcast(sc, I32)
        key = jnp.where(bits < 0, INT_MIN - bits, bits)
        put_keys(c, key if d is None else jnp.where(_causal(d), key, INT_MIN), lanes)
        if d == 1:
            put_keys(c, jnp.full((CK, CK), INT_MIN, I32), slice(0, CK))

    def score_pair(c, d0, d1):
        lg0, lg1 = logits(c, d0), logits(c + 1, d1)
        score_chunk(c, lg0, d0)
        score_chunk(c + 1, lg1, d1)

    def score_body(p, carry):
        score_pair(2 * p, None, None)
        return carry

    lax.fori_loop(0, i, score_body, 0)
    score_pair(n_full, 0, 1)

    def pair_loop(body, init, last=None):
        def pair(p, carry):
            return body(2 * p + 1, body(2 * p, carry))
        carry = body(n_full, lax.fori_loop(0, i, pair, init))
        return (last or body)(n_full + 1, carry)

    def count16(pred, also=None):
        def hits(c, lanes):
            hit = jnp.where(pred(c, lanes), jnp.int16(1), jnp.int16(0))
            if also is not None:
                hit = jnp.where(also(c, lanes), hit, jnp.int16(0))
            return _fold_rows(hit, 2 * SUBLANES)

        def body(c, part):
            return part + hits(c, slice(None))

        def last(c, part):
            return jnp.concatenate([part[:, :CK], part[:, CK:] + hits(c, slice(CK, TQ))], axis=1)

        part = pair_loop(body, jnp.zeros((2 * SUBLANES, TQ), I16), last)
        return jnp.sum(part.astype(F32), axis=0, keepdims=True)

    def search16(count_ge, need):
        def bit_body(bi, t_u):
            c_u = t_u | jnp.left_shift(jnp.int32(1), 15 - bi)
            cnt = count_ge((c_u - HALF16).astype(I16))
            return jnp.where(cnt >= need, c_u, t_u)
        return lax.fori_loop(0, 16, bit_body, jnp.zeros((1, TQ), I32))

    hi_u = search16(lambda ck: count16(lambda c, lanes: hi_ref[c, :, lanes] >= ck[:, lanes]),
                    float(topk))
    thr_hi = (hi_u - HALF16).astype(I16)
    n_above = count16(lambda c, lanes: hi_ref[c, :, lanes] > thr_hi[:, lanes])
    need_lo = float(topk) - n_above

    fill = jnp.int16(-HALF16)
    top_ref[...] = jnp.full(top_ref.shape, fill, I16)

    def bucket(c):
        return jnp.where(hi_ref[c] == thr_hi, lo_ref[c], fill)

    def top_body(c, carry):
        x = bucket(c)
        for r in range(4):
            m = top_ref[r]
            swap = x > m
            top_ref[r] = jnp.where(swap, x, m)
            x = jnp.where(swap, m, x)
        return carry

    many = i > 1

    @pl.when(many)
    def _():
        pair_loop(top_body, 0)

    fourth = jnp.max(_fold_rows(jnp.where(top_ref[3] > fill, jnp.int16(1), jnp.int16(0)),
                                2 * SUBLANES).astype(F32)) > 0.5
    use_top = jnp.logical_and(many, jnp.logical_not(fourth))

    def count_top(pred):
        part = jnp.zeros((2 * SUBLANES, TQ), I16)
        for r in range(3):
            part = part + _fold_rows(jnp.where(pred(top_ref[r]), jnp.int16(1), jnp.int16(0)), 2 * SUBLANES)
        return jnp.sum(part.astype(F32), axis=0, keepdims=True)

    def lo_select(count_ge, count_gt):
        lo_u = search16(count_ge, need_lo)
        sel_ref[0:1, :] = lo_u.astype(F32)
        sel_ref[1:2, :] = count_gt((lo_u - HALF16).astype(I16))

    @pl.when(use_top)
    def _():
        lo_select(lambda ck: count_top(lambda x: x >= ck), lambda t: count_top(lambda x: x > t))

    @pl.when(jnp.logical_not(use_top))
    def _():
        def bucket_body(c, carry):
            bk_ref[c] = bucket(c)
            return carry

        pair_loop(bucket_body, 0)
        lo_select(lambda ck: count16(lambda c, lanes: bk_ref[c, :, lanes] >= ck[:, lanes]),
                  lambda t: count16(lambda c, lanes: bk_ref[c, :, lanes] > t[:, lanes]))

    lo_u = sel_ref[0:1, :].astype(I32)
    thr_lo = (lo_u - HALF16).astype(I16)
    thr = ((hi_u - HALF16) << 16) | lo_u

    n_gt = n_above + sel_ref[1:2, :]
    n_eq = count16(lambda c, lanes: lo_ref[c, :, lanes] == thr_lo[:, lanes],
                   also=lambda c, lanes: hi_ref[c, :, lanes] == thr_hi[:, lanes])
    need = float(topk) - n_gt
    amb = jnp.logical_and(n_eq > need, thr > INT_MIN)
    any_amb = jnp.max(jnp.where(amb, 1.0, 0.0)) > 0.5

    @pl.when(any_amb)
    def _():
        def drop_body(c, seen):
            k = keys_ref[c]
            eq = k == thr
            eqf = jnp.where(eq, 1.0, 0.0)
            rank = jnp.dot(tri_ref[...], eqf.astype(BF16), preferred_element_type=F32) + seen
            drop = jnp.logical_and(jnp.logical_and(eq, rank > need), amb)
            keys_ref[c] = jnp.where(drop, INT_MIN, k)
            return seen + jnp.sum(eqf, axis=0, keepdims=True)

        pair_loop(drop_body, jnp.zeros((1, TQ), F32))

    thr_ref[...] = jnp.maximum(thr, INT_MIN + 1)

    def qk_all(c, lanes):
        kc = ak_ref[pl.ds(pl.multiple_of(c * CK, CK), CK), :]
        return [jnp.dot(_half(kc, h, 6), aqt_ref[h, :, lanes], preferred_element_type=F32) for h in range(N_HEADS)]

    _flash_loop(n_full, qk_all,
                lambda bias, h, s, lanes: s + bias,
                lambda c, h: avt_ref[c, h * VROWS:(h + 1) * VROWS, :],
                (s_ref, mx_ref, m_ref, acc_ref),
                prep=lambda c, lanes: jnp.where(keys_ref[c, :, lanes] >= thr_ref[:, lanes], 0.0, NEG),
                causal_tail=False)
    for h in range(N_HEADS):
        ot_ref[h * HEAD_DIM:(h + 1) * HEAD_DIM, :] = _softmax_out(acc_ref.at[h])
    o_ref[...] = ot_ref[...].T.astype(o_ref.dtype)


def _dsa(aq, ak, avt, iq, ik, iw):
    b, t, _ = aq.shape
    topk = min(TOPK_MAX, t // 4)
    qspec = pl.BlockSpec((None, TQ, BRANCH_W), lambda bb, i: (bb, i, 0))
    kspec, vspec = _kv_specs(t, BRANCH_W)
    pick = np.zeros((2 * SUBLANES, MXU_N), np.float32)
    for hh in range(IDX_HEADS):
        pick[hh, IDX_DIM + hh] = 1.0
    pick = jnp.asarray(pick, BF16)
    tri = jnp.asarray(np.tril(np.ones((CK, CK), np.float32)), BF16)
    kern = functools.partial(_dsa_kernel, topk=topk, idx_scale=(IDX_HEADS * IDX_DIM) ** -0.5)
    return pl.pallas_call(
        kern,
        out_shape=jax.ShapeDtypeStruct((b, t, BRANCH_W), BF16),
        grid=(b, t // TQ),
        in_specs=[qspec, kspec, vspec, qspec, kspec, qspec,
                  pl.BlockSpec(pick.shape, lambda bb, i: (0, 0)), pl.BlockSpec(tri.shape, lambda bb, i: (0, 0))],
        out_specs=qspec,
        scratch_shapes=[
            pltpu.VMEM((t // CK, CK, TQ), I32),
            pltpu.VMEM((t // CK, CK, TQ), I16),
            pltpu.VMEM((t // CK, CK, TQ), I16),
            pltpu.VMEM((t // CK, CK, TQ), I16),
            pltpu.VMEM((4, CK, TQ), I16),
            pltpu.VMEM((SUBLANES, TQ), F32),
            pltpu.VMEM((IDX_HEADS, IDX_DIM, TQ), BF16),
            pltpu.VMEM((N_HEADS, LANES, TQ), BF16),
            pltpu.VMEM((2 * SUBLANES, TQ), F32),
            pltpu.VMEM((1, TQ), I32),
        ] + _attn_scratch(N_HEADS),
        compiler_params=_cparams(2),
        name="dsa",
    )(aq, ak, avt, iq, ik, iw, pick, tri)


def _kbar_kernel(k_ref, o_ref):
    o_ref[...] = jnp.zeros(o_ref.shape, o_ref.dtype)
    nb = k_ref.shape[0] // MOBA_BLOCK
    for n in range(nb):
        blk = k_ref[n * MOBA_BLOCK:(n + 1) * MOBA_BLOCK, :].astype(F32)
        o_ref[n:n + 1, :] = jnp.mean(blk, axis=0, keepdims=True).astype(o_ref.dtype)


def _kbar(bk):
    b, t, w = bk.shape
    nbp = max(2 * SUBLANES, t // MOBA_BLOCK)
    return pl.pallas_call(
        _kbar_kernel,
        out_shape=jax.ShapeDtypeStruct((b, nbp, w), BF16),
        grid=(b,),
        in_specs=[pl.BlockSpec((None, t, w), lambda bb: (bb, 0, 0))],
        out_specs=pl.BlockSpec((None, nbp, w), lambda bb: (bb, 0, 0)),
        compiler_params=_cparams(1),
        name="moba_kbar",
    )(bk)


def _moba_kernel(q_ref, k_ref, vt_ref, kbar_ref, o_ref, qt_ref, bias_ref, s_ref, mx_ref, m_ref, acc_ref, ot_ref):
    i = pl.program_id(1)
    nbp = kbar_ref.shape[0]
    blk = lax.broadcasted_iota(I32, (nbp, TQ), 0)
    blk_f = blk.astype(F32)
    own = 2 * i + (lax.broadcasted_iota(I32, (nbp, TQ), 1) >> (MOBA_BLOCK.bit_length() - 1))
    _masked_qt(q_ref[...].astype(F32) * (HEAD_DIM ** -0.5 * LOG2E), 6, N_HEADS, qt_ref)

    for h in range(N_HEADS):
        g = jnp.where(blk < own, jnp.dot(_half(kbar_ref[...], h, 6), qt_ref[h], preferred_element_type=F32), NEG)
        bias = jnp.full((nbp, TQ), NEG, F32)
        for _ in range(MOBA_TOPK):
            mx = jnp.max(g, axis=0, keepdims=True)
            first = jnp.min(jnp.where(g == mx, blk_f, 1e9), axis=0, keepdims=True)
            pick = jnp.logical_and(blk_f == first, mx > 0.5 * NEG)
            bias = jnp.where(pick, 0.0, bias)
            g = jnp.where(pick, NEG, g)
        bias_ref[h] = jnp.where(blk == own, 0.0, bias)

    def qk_all(c, lanes):
        kc = k_ref[pl.ds(pl.multiple_of(c * CK, CK), CK), :]
        return [jnp.dot(_half(kc, h, 6), qt_ref[h, :, lanes], preferred_element_type=F32) for h in range(N_HEADS)]

    _flash_loop(2 * i, qk_all, lambda c, h, s, lanes: s + bias_ref[h, pl.ds(c, 1), lanes],
                lambda c, h: vt_ref[c, h * VROWS:(h + 1) * VROWS, :], (s_ref, mx_ref, m_ref, acc_ref))
    for h in range(N_HEADS):
        ot_ref[h * HEAD_DIM:(h + 1) * HEAD_DIM, :] = _softmax_out(acc_ref.at[h])
    o_ref[...] = ot_ref[...].T.astype(o_ref.dtype)


def _moba(bq, bk, bvt, kbar):
    b, t, w = bq.shape
    assert TQ == 2 * MOBA_BLOCK and CK == MOBA_BLOCK and t % TQ == 0
    nbp = kbar.shape[1]
    qspec = pl.BlockSpec((None, TQ, w), lambda bb, i: (bb, i, 0))
    kspec, vspec = _kv_specs(t, w)
    return pl.pallas_call(
        _moba_kernel,
        out_shape=jax.ShapeDtypeStruct((b, t, w), BF16),
        grid=(b, t // TQ),
        in_specs=[qspec, kspec, vspec, pl.BlockSpec((None, nbp, w), lambda bb, i: (bb, 0, 0))],
        out_specs=qspec,
        scratch_shapes=[pltpu.VMEM((N_HEADS, LANES, TQ), BF16), pltpu.VMEM((N_HEADS, nbp, TQ), F32)]
        + _attn_scratch(N_HEADS),
        compiler_params=_cparams(2),
        name="moba",
    )(bq, bk, bvt, kbar)


def _diff_kernel(q_ref, k_ref, vt_ref, lam_ref, norm_ref, misc_ref, o_ref,
                 qt_ref, s_ref, mx_ref, m_ref, acc_ref, ot_ref):
    i = pl.program_id(1)
    _masked_qt(q_ref[...].astype(F32) * (DIFF_DIM ** -0.5 * LOG2E), 5, 2 * N_HEADS, qt_ref)

    dl = lam_ref[...]
    lam_init = misc_ref[0:1, 0:1]
    lam = (jnp.exp(jnp.sum(dl[0:1, :] * dl[1:2, :], axis=1, keepdims=True))
           - jnp.exp(jnp.sum(dl[2:3, :] * dl[3:4, :], axis=1, keepdims=True)) + lam_init)

    def qk_all(c, lanes):
        kc = k_ref[pl.ds(pl.multiple_of(c * CK, CK), CK), :]
        return [jnp.dot(_half(kc, j, 5), qt_ref[j, :, lanes], preferred_element_type=F32) for j in range(2 * N_HEADS)]

    _flash_loop(2 * i, qk_all, None,
                lambda c, j: vt_ref[c, (j // 2) * VROWS:(j // 2 + 1) * VROWS, :],
                (s_ref, mx_ref, m_ref, acc_ref))

    post = norm_ref[...] * (1.0 - lam_init)
    for h in range(N_HEADS):
        o_h = _softmax_out(acc_ref.at[2 * h]) - lam * _softmax_out(acc_ref.at[2 * h + 1])
        ms = jnp.mean(o_h * o_h, axis=0, keepdims=True)
        ot_ref[h * HEAD_DIM:(h + 1) * HEAD_DIM, :] = o_h * lax.rsqrt(ms + RMS_EPS) * post
    o_ref[...] = ot_ref[...].T.astype(o_ref.dtype)


def _diff(cq, ck, cvt, lam, norm, misc):
    b, t, w = cq.shape
    qspec = pl.BlockSpec((None, TQ, w), lambda bb, i: (bb, i, 0))
    kspec, vspec = _kv_specs(t, w)
    full = lambda a: pl.BlockSpec(a.shape, lambda bb, i: (0,) * a.ndim)
    return pl.pallas_call(
        _diff_kernel,
        out_shape=jax.ShapeDtypeStruct((b, t, w), BF16),
        grid=(b, t // TQ),
        in_specs=[qspec, kspec, vspec, full(lam), full(norm), full(misc)],
        out_specs=qspec,
        scratch_shapes=[pltpu.VMEM((2 * N_HEADS, LANES, TQ), BF16)] + _attn_scratch(2 * N_HEADS),
        compiler_params=_cparams(2),
        name="diff",
    )(cq, ck, cvt, lam, norm, misc)


def _mla_prep_kernel(cq_ref, ckv_ref, kr_ref, qn_ref, kvn_ref, wq_ref, wqr_ref, wk_ref, wvt_ref,
                     p_ref, ct_ref, st_ref, q_out, k_out, vt_out):
    x = cq_ref[...].astype(F32)
    xn = (x * lax.rsqrt(jnp.mean(x * x, axis=1, keepdims=True) + RMS_EPS) * qn_ref[...]).astype(BF16)
    q = (jnp.dot(xn, wq_ref[...], preferred_element_type=F32) * ct_ref[...]
         + jnp.dot(xn, wqr_ref[...], preferred_element_type=F32) * st_ref[...])
    q_out[...] = q.astype(q_out.dtype)
    c = ckv_ref[:, :KV_LORA].astype(F32)
    cn = (c * lax.rsqrt(jnp.mean(c * c, axis=1, keepdims=True) + RMS_EPS) * kvn_ref[...]).astype(BF16)
    k = (jnp.dot(cn, wk_ref[...], preferred_element_type=F32)
         + jnp.dot(kr_ref[...], p_ref[...], preferred_element_type=F32))
    k_out[...] = k.astype(k_out.dtype)
    _store_vt(vt_out, _tn_dot(wvt_ref[...], cn))


def _mla_prep(dcq, ckv, kr, qn, kvn, wq, wqr, wk, wvt, pmat, ct, st, l):
    b, t, _ = dcq.shape
    tm = 1024
    hw = N_HEADS * LANES
    row = lambda w: pl.BlockSpec((None, tm, w), lambda i, bb: (bb, i, 0))
    full = lambda a: pl.BlockSpec(a.shape, lambda i, bb: (0,) * a.ndim)
    tab = pl.BlockSpec((tm, hw), lambda i, bb: (i, 0))
    return pl.pallas_call(
        _mla_prep_kernel,
        out_shape=(jax.ShapeDtypeStruct((b, t, hw), BF16), jax.ShapeDtypeStruct((b, t, hw), BF16),
                   jax.ShapeDtypeStruct((b, t // CK, N_HEADS * VROWS, CK), BF16)),
        grid=(t // tm, b),
        in_specs=[row(Q_LORA), row(MXU_N), row(MXU_N), full(qn), full(kvn), _layer_spec(wq, l), _layer_spec(wqr, l),
                  _layer_spec(wk, l), _layer_spec(wvt, l), full(pmat), tab, tab],
        out_specs=(row(hw), row(hw),
                   pl.BlockSpec((None, tm // CK, N_HEADS * VROWS, CK), lambda i, bb: (bb, i, 0, 0))),
        compiler_params=_cparams(2),
        name="mla_prep",
    )(dcq, ckv, kr, qn, kvn, wq, wqr, wk, wvt, pmat, ct, st)


def _mla_kernel(q_ref, k_ref, vt_ref, o_ref, qt_ref, s_ref, mx_ref, m_ref, acc_ref, ot_ref):
    i = pl.program_id(1)
    hs = [slice(h * LANES, (h + 1) * LANES) for h in range(N_HEADS)]
    for h in range(N_HEADS):
        qt_ref[h] = q_ref[:, hs[h]].astype(F32).T.astype(BF16)

    def qk_all(c, lanes):
        start = pl.multiple_of(c * CK, CK)
        return [jnp.dot(k_ref[pl.ds(start, CK), hs[h]], qt_ref[h, :, lanes], preferred_element_type=F32)
                for h in range(N_HEADS)]

    _flash_loop(2 * i, qk_all, None,
                lambda c, h: vt_ref[c, h * VROWS:(h + 1) * VROWS, :],
                (s_ref, mx_ref, m_ref, acc_ref))
    for h in range(N_HEADS):
        ot_ref[h * HEAD_DIM:(h + 1) * HEAD_DIM, :] = _softmax_out(acc_ref.at[h])
    o_ref[...] = ot_ref[...].T.astype(o_ref.dtype)


def _mla(qm, km, vmt):
    b, t, hw = qm.shape
    kspec, vspec = _kv_specs(t, hw)
    return pl.pallas_call(
        _mla_kernel,
        out_shape=jax.ShapeDtypeStruct((b, t, BRANCH_W), BF16),
        grid=(b, t // TQ),
        in_specs=[pl.BlockSpec((None, TQ, hw), lambda bb, i: (bb, i, 0)), kspec, vspec],
        out_specs=pl.BlockSpec((None, TQ, BRANCH_W), lambda bb, i: (bb, i, 0)),
        scratch_shapes=[pltpu.VMEM((N_HEADS, LANES, TQ), BF16)] + _attn_scratch(N_HEADS),
        compiler_params=_cparams(2),
        name="mla",
    )(qm, km, vmt)


def _mem_kv_kernel(x_ref, w_ref, k_ref, vt_ref):
    x = x_ref[...].astype(BF16)
    k_ref[...] = jnp.dot(x, w_ref[:, :BRANCH_W], preferred_element_type=F32).astype(k_ref.dtype)
    _store_vt(vt_ref, _tn_dot(w_ref[:, BRANCH_W:], x))


def _mem_kv(mem, w, l):
    b, m, d = mem.shape
    assert m % CK == 0
    return pl.pallas_call(
        _mem_kv_kernel,
        out_shape=(jax.ShapeDtypeStruct((b, m, BRANCH_W), BF16),
                   jax.ShapeDtypeStruct((b, m // CK, N_HEADS * VROWS, CK), BF16)),
        grid=(b,),
        in_specs=[pl.BlockSpec((None, m, d), lambda bb: (bb, 0, 0)), _layer_spec(w, l)],
        out_specs=(pl.BlockSpec((None, m, BRANCH_W), lambda bb: (bb, 0, 0)),
                   pl.BlockSpec((None, m // CK, N_HEADS * VROWS, CK), lambda bb: (bb, 0, 0, 0))),
        compiler_params=_cparams(1),
        name="mem_kv",
    )(mem, w)


def _mem_kernel(q_ref, k_ref, vt_ref, o_ref, qt_ref, ot_ref):
    _masked_qt(q_ref[...].astype(F32) * (HEAD_DIM ** -0.5 * LOG2E), 6, N_HEADS, qt_ref)
    s_all = [jnp.dot(_half(k_ref[...], h, 6), qt_ref[h], preferred_element_type=F32) for h in range(N_HEADS)]
    for h in range(N_HEADS):
        s_t = s_all[h]
        p = jnp.exp2(s_t - jnp.max(s_t, axis=0, keepdims=True)).astype(BF16)
        acc = jnp.dot(vt_ref[0, h * VROWS:(h + 1) * VROWS, :], p, preferred_element_type=F32)
        ot_ref[h * HEAD_DIM:(h + 1) * HEAD_DIM, :] = acc[:HEAD_DIM, :] / acc[HEAD_DIM:HEAD_DIM + 1, :]
    o_ref[...] = ot_ref[...].T.astype(o_ref.dtype)


def _mem_attn(eq, mk, mvt):
    b, t, w = eq.shape
    m = mk.shape[1]
    assert m == CK
    return pl.pallas_call(
        _mem_kernel,
        out_shape=jax.ShapeDtypeStruct((b, t, w), BF16),
        grid=(b, t // TQ),
        in_specs=[pl.BlockSpec((None, TQ, w), lambda bb, i: (bb, i, 0)),
                  pl.BlockSpec((None, m, w), lambda bb, i: (bb, 0, 0)),
                  pl.BlockSpec((None,) + mvt.shape[1:], lambda bb, i: (bb, 0, 0, 0))],
        out_specs=pl.BlockSpec((None, TQ, w), lambda bb, i: (bb, i, 0)),
        scratch_shapes=[pltpu.VMEM((N_HEADS, LANES, TQ), BF16), pltpu.VMEM((BRANCH_W, TQ), F32)],
        compiler_params=_cparams(2),
        name="mem_attn",
    )(eq, mk, mvt)


def _final_kernel(h_ref, hb_ref, oa_ref, ob_ref, oc_ref, od_ref, oe_ref, z_ref,
                  wg_ref, wb_ref, wo_ref, g_ref, b_ref, h_out, hb_out, acc_ref, *, alpha):
    d = h_ref.shape[1]
    half = h_ref.shape[0] // 2
    for n, o_ref in enumerate((oa_ref, ob_ref, oc_ref, od_ref, oe_ref)):
        for r in range(2):
            rows = slice(r * half, (r + 1) * half)
            z = z_ref[rows, n * BRANCH_W:(n + 1) * BRANCH_W].astype(F32)
            y = o_ref[rows, :].astype(F32) * (z / (1.0 + jnp.exp(-z)))
            u = jnp.dot(y.astype(BF16), wb_ref[n], preferred_element_type=F32)
            g = jnp.dot(hb_ref[rows, :], wg_ref[:, n * d:(n + 1) * d], preferred_element_type=F32)
            t = u / (1.0 + jnp.exp(-g))
            acc_ref[rows, :] = t if n == 0 else acc_ref[rows, :] + t
    for r in range(2):
        rows = slice(r * half, (r + 1) * half)
        out = jnp.dot(acc_ref[rows, :].astype(BF16), wo_ref[...], preferred_element_type=F32)
        x = alpha * h_ref[rows, :] + out
        mu = jnp.mean(x, axis=1, keepdims=True)
        xc = x - mu
        var = jnp.mean(xc * xc, axis=1, keepdims=True)
        y = xc * lax.rsqrt(var + LN_EPS) * g_ref[...] + b_ref[...]
        h_out[rows, :] = y
        hb_out[rows, :] = y.astype(BF16)


def _final(h, hb, os5, z, wg, wb, wo, ln_g, ln_b, alpha, l):
    n, d = h.shape
    tm = 512
    row = lambda w: pl.BlockSpec((tm, w), lambda i: (i, 0))
    full = lambda a: pl.BlockSpec(a.shape, lambda i: (0,) * a.ndim)
    return pl.pallas_call(
        functools.partial(_final_kernel, alpha=alpha),
        out_shape=(jax.ShapeDtypeStruct((n, d), F32), jax.ShapeDtypeStruct((n, d), BF16)),
        grid=(n // tm,),
        in_specs=[row(d), row(d)] + [row(BRANCH_W)] * N_BRANCH + [row(N_BRANCH * BRANCH_W),
                  _layer_spec(wg, l), _layer_spec(wb, l), _layer_spec(wo, l), full(ln_g), full(ln_b)],
        out_specs=(row(d), row(d)),
        scratch_shapes=[pltpu.VMEM((tm, d), F32)],
        compiler_params=_cparams(1),
        name="merge_out_ln",
    )(h, hb, *os5, z, wg, wb, wo, ln_g, ln_b)


ROPE_GROUPS = (("a_q", N_HEADS, HEAD_DIM, ROT_64), ("a_k", N_HEADS, HEAD_DIM, ROT_64),
               ("i_q", IDX_HEADS, IDX_DIM, ROT_32), ("i_k", 1, MXU_N, ROT_32),
               ("b_q", N_HEADS, HEAD_DIM, ROT_64), ("b_k", N_HEADS, HEAD_DIM, ROT_64),
               ("c_q", 2 * N_HEADS, DIFF_DIM, ROT_32), ("c_k", 2 * N_HEADS, DIFF_DIM, ROT_32),
               ("d_kr", 1, MXU_N, MLA_ROPE))
PLAIN_COLS = ("d_cq", "d_ckv", "e_q") + tuple(("z", j) for j in range(N_BRANCH))
VALUE_COLS = ("a_v", "b_v", "c_v")
GATE_COLS = tuple(("g", j) for j in range(OFF["g"][1] // MXU_N))


def _window_start(col):
    name, j = col if isinstance(col, tuple) else (col, 0)
    return OFF[name][0] + j * MXU_N


def _weight_prep_kernel(offs_ref, wt_ref, o_ref):
    o_ref[...] = wt_ref[...].T.astype(o_ref.dtype)


def _weight_windows(wt, cols, name):
    depth, n, d = wt.shape
    starts = [_window_start(c) for c in cols]
    assert all(st % SUBLANES == 0 and st + MXU_N <= n for st in starts)
    grid_spec = pltpu.PrefetchScalarGridSpec(
        num_scalar_prefetch=1,
        grid=(depth, len(cols)),
        in_specs=[pl.BlockSpec((None, pl.Element(MXU_N), pl.Element(d)),
                               lambda l, j, offs: (l, pl.multiple_of(offs[j], SUBLANES), 0))],
        out_specs=pl.BlockSpec((None, d, MXU_N), lambda l, j, offs: (l, 0, j)),
    )
    return pl.pallas_call(
        _weight_prep_kernel,
        out_shape=jax.ShapeDtypeStruct((depth, d, MXU_N * len(cols)), BF16),
        grid_spec=grid_spec,
        compiler_params=_cparams(2),
        name=name,
    )(jnp.asarray(np.asarray(starts, np.int32)), wt)


def _weight_prep(w_in):
    wt = jnp.swapaxes(w_in, 1, 2)
    return (_weight_windows(wt, PLAIN_COLS, "wprep_plain"), _weight_windows(wt, VALUE_COLS, "wprep_value"),
            _weight_windows(wt, [name for name, *_ in ROPE_GROUPS], "wprep_rope"),
            _weight_windows(wt, GATE_COLS, "wprep_gate"))


def _rope_tables(seq, rot_dim):
    pos = jnp.arange(seq, dtype=F32)
    inv = ROPE_THETA ** (-jnp.arange(0, rot_dim, 2, dtype=F32) / rot_dim)
    ang = pos[:, None] * inv[None, :]
    return jnp.cos(ang), jnp.sin(ang)


def _rope_cs(t, nh, hd, r):
    cos, sin = _rope_tables(t, r)
    c = jnp.concatenate([cos, cos, jnp.ones((t, hd - r), F32)], axis=1)
    s = jnp.concatenate([-sin, sin, jnp.zeros((t, hd - r), F32)], axis=1)
    return jnp.tile(c, (1, nh)), jnp.tile(s, (1, nh))


def kernel(x, mem, ln0_g, ln0_b, w_in, mla_q_norm, w_uq, mla_kv_norm, w_ukv, diff_lam, diff_norm,
           w_mem_kv, w_branch, w_out, ln_g, ln_b):
    b, t, d = x.shape
    depth = w_in.shape[0]
    alpha = (2 * depth) ** 0.25
    assert t % 512 == 0 and d == 1024

    w_plain, w_vt, w_rope, wg = _weight_prep(w_in)
    plain_widths = (BRANCH_W,) * 3 + (N_BRANCH * BRANCH_W,)
    rope_heads = tuple((hd, r // 2) for _, _, hd, r in ROPE_GROUPS)
    patterns = sorted(set((nh, hd, r) for _, nh, hd, r in ROPE_GROUPS))
    rope_tables = tuple(patterns.index((nh, hd, r)) for _, nh, hd, r in ROPE_GROUPS)
    cs = [_rope_cs(t, nh, hd, r) for nh, hd, r in patterns]
    ctab = jnp.stack([c for c, _ in cs])
    stab = jnp.stack([s for _, s in cs])

    uq = w_uq.reshape(depth, Q_LORA, N_HEADS, MLA_NOPE + MLA_ROPE)
    qn_w, qr_w = uq[..., :MLA_NOPE], uq[..., MLA_NOPE:]
    pad32 = jnp.zeros((depth, Q_LORA, N_HEADS, LANES - MLA_NOPE - MLA_ROPE), w_uq.dtype)
    hw = N_HEADS * LANES
    wq = jnp.concatenate([qn_w, qr_w, pad32], axis=-1).reshape(depth, Q_LORA, hw).astype(BF16)
    half = MLA_ROPE // 2
    wq_rot = jnp.concatenate([jnp.zeros_like(qn_w), -qr_w[..., half:], qr_w[..., :half], pad32],
                             axis=-1).reshape(depth, Q_LORA, hw).astype(BF16)
    cos_m, sin_m = _rope_tables(t, MLA_ROPE)
    one = lambda n: jnp.ones((t, n), F32)
    zer = lambda n: jnp.zeros((t, n), F32)
    qs = (MLA_NOPE + MLA_ROPE) ** -0.5 * LOG2E
    ct_q = qs * jnp.tile(jnp.concatenate([one(MLA_NOPE), cos_m, cos_m, one(LANES - MLA_NOPE - MLA_ROPE)], axis=1), (1, N_HEADS))
    st_q = qs * jnp.tile(jnp.concatenate([zer(MLA_NOPE), sin_m, sin_m, zer(LANES - MLA_NOPE - MLA_ROPE)], axis=1), (1, N_HEADS))
    ukv = w_ukv.reshape(depth, KV_LORA, N_HEADS, MLA_NOPE + MLA_V)
    wk = jnp.concatenate([ukv[..., :MLA_NOPE], jnp.zeros((depth, KV_LORA, N_HEADS, LANES - MLA_NOPE), w_ukv.dtype)],
                         axis=-1).reshape(depth, KV_LORA, hw).astype(BF16)
    wvt = ukv[..., MLA_NOPE:].reshape(depth, KV_LORA, N_HEADS * MLA_V).astype(BF16)
    place = np.zeros((MXU_N, hw), np.float32)
    for hh in range(N_HEADS):
        for j in range(MLA_ROPE):
            place[j, hh * LANES + MLA_NOPE + j] = 1.0
    place = jnp.asarray(place, BF16)

    wb = w_branch.astype(BF16)
    wo = w_out.astype(BF16)
    wmem = w_mem_kv.astype(BF16)
    norm_t = jnp.broadcast_to(diff_norm.astype(F32)[:, :, None], (depth, HEAD_DIM, TQ))

    h, hb = _layer_norm0(x.reshape(b * t, d), ln0_g, ln0_b)
    for l in range(depth):
        hb3 = hb.reshape(b, t, d)
        avt, bvt, cvt, dcq, ckv_iw, eq, z = _proj_plain(hb3, w_plain, w_vt, plain_widths, l)
        aq, ak, iq, ik, bq, bk, cq, ck, kr = _proj_rope(hb3, w_rope, ctab, stab, rope_heads, rope_tables, l)

        o_a = _dsa(aq, ak, avt, iq, ik, ik)
        o_b = _moba(bq, bk, bvt, _kbar(bk))
        lam_init = 0.8 - 0.6 * math.exp(-0.3 * l)
        misc = jnp.full((SUBLANES, LANES), lam_init, F32)
        o_c = _diff(cq, ck, cvt, diff_lam[l].astype(F32), norm_t[l], misc)
        qm, km, vmt = _mla_prep(dcq, ckv_iw, kr, mla_q_norm[l].reshape(1, Q_LORA), mla_kv_norm[l].reshape(1, KV_LORA),
                                wq, wq_rot, wk, wvt, place, ct_q, st_q, l)
        o_d = _mla(qm, km, vmt)
        o_e = _mem_attn(eq, *_mem_kv(mem, wmem, l))

        os5 = [o.reshape(b * t, BRANCH_W) for o in (o_a, o_b, o_c, o_d, o_e)]
        h, hb = _final(h, hb, os5, z.reshape(b * t, N_BRANCH * BRANCH_W), wg, wb, wo,
                       ln_g[l].reshape(1, d), ln_b[l].reshape(1, d), alpha, l)
    return h.reshape(b, t, d)
```

```python
import functools
import math

import numpy as np
import jax
import jax.numpy as jnp
from jax import lax
from jax.experimental import pallas as pl
from jax.experimental.pallas import tpu as pltpu

F32 = jnp.float32
BF16 = jnp.bfloat16
I32 = jnp.int32
I16 = jnp.int16

N_HEADS = 4
HEAD_DIM = 64
BRANCH_W = N_HEADS * HEAD_DIM
N_BRANCH = 5
ROPE_THETA = 500000.0
ROT_64 = 16
ROT_32 = 8
IDX_HEADS = 8
IDX_DIM = 32
TOPK_MAX = 256
MOBA_BLOCK = 256
MOBA_TOPK = 3
DIFF_DIM = 32
Q_LORA = 256
KV_LORA = 128
MLA_NOPE = 64
MLA_ROPE = 32
MLA_V = 64
LN_EPS = 1e-5
RMS_EPS = 1e-6

IN_LAYOUT = (
    ("a_q", BRANCH_W), ("a_k", BRANCH_W), ("a_v", BRANCH_W),
    ("i_q", IDX_HEADS * IDX_DIM), ("i_k", IDX_DIM), ("i_w", IDX_HEADS),
    ("b_q", BRANCH_W), ("b_k", BRANCH_W), ("b_v", BRANCH_W),
    ("c_q", BRANCH_W), ("c_k", BRANCH_W), ("c_v", BRANCH_W),
    ("d_cq", Q_LORA), ("d_ckv", KV_LORA), ("d_kr", MLA_ROPE),
    ("e_q", BRANCH_W),
    ("z", N_BRANCH * BRANCH_W),
    ("g", N_BRANCH * 1024),
)

SUBLANES = 8
LANES = 128
MXU_N = 256
TQ = 512
CK = 256
VROWS = HEAD_DIM + 16
FLASH_UNROLL = 4
NEG = -1e30
LOG2E = math.log2(math.e)
INT_MIN = np.int32(-2 ** 31)
HALF16 = 1 << 15
VMEM_LIMIT = 56 * 1024 * 1024


def _offsets():
    off, out = 0, {}
    for name, size in IN_LAYOUT:
        out[name] = (off, size)
        off += size
    return out


OFF = _offsets()


def _nt_dot(a, b):
    return lax.dot_general(a, b, (((1,), (1,)), ((), ())), preferred_element_type=F32)


def _tn_dot(w, x):
    return lax.dot_general(w, x, (((0,), (1,)), ((), ())), preferred_element_type=F32)


def _fold_rows(w, rows=SUBLANES):
    xs = [w[r:r + rows, :] for r in range(0, w.shape[0], rows)]
    while len(xs) > 1:
        xs = [xs[j] + xs[j + 1] for j in range(0, len(xs) - 1, 2)] + ([xs[-1]] if len(xs) % 2 else [])
    return xs[0]


def _masked_qt(q, shift, n, qt_ref):
    qt = q.T
    dim = lax.broadcasted_iota(I32, (LANES, qt.shape[1]), 0)
    for j in range(n):
        half = (j << shift) // LANES
        rows = qt[half * LANES:(half + 1) * LANES, :]
        qt_ref[j] = jnp.where(((dim + half * LANES) >> shift) == j, rows, 0.0).astype(BF16)


def _half(kc, j, shift):
    half = (j << shift) // LANES
    return kc[:, half * LANES:(half + 1) * LANES]


def _cparams(n_axes):
    return pltpu.CompilerParams(dimension_semantics=("arbitrary",) * n_axes,
                                vmem_limit_bytes=VMEM_LIMIT)


def _layer_spec(a, l):
    return pl.BlockSpec((None,) + a.shape[1:], lambda *_: (l,) + (0,) * (a.ndim - 1))


def _softmax_step(s_t, m_tile, vt_h, m_ref, acc_ref):
    m_old = m_ref[...]
    m_new = jnp.maximum(m_old, m_tile)
    alpha = jnp.exp2(m_old - m_new)
    p = jnp.exp2(s_t - m_new)
    acc_ref[...] = alpha * acc_ref[...] + jnp.dot(vt_h, p.astype(BF16), preferred_element_type=F32)
    m_ref[...] = m_new


def _softmax_init(m_ref, acc_ref):
    m_ref[...] = jnp.full(m_ref.shape, NEG, F32)
    acc_ref[...] = jnp.zeros(acc_ref.shape, F32)


def _softmax_out(acc_ref):
    return acc_ref[:HEAD_DIM, :] / acc_ref[HEAD_DIM:HEAD_DIM + 1, :]


def _store_vt(o_ref, vt):
    ones = jnp.ones((VROWS - HEAD_DIM, CK), o_ref.dtype)
    for j in range(o_ref.shape[0]):
        for h in range(N_HEADS):
            o_ref[j, h * VROWS:h * VROWS + HEAD_DIM, :] = (
                vt[h * HEAD_DIM:(h + 1) * HEAD_DIM, j * CK:(j + 1) * CK].astype(o_ref.dtype))
            o_ref[j, h * VROWS + HEAD_DIM:(h + 1) * VROWS, :] = ones


def _flash_loop(n_full, qk_all, mask, vt_rows, state, prep=None, causal_tail=True):
    s_ref, mx_ref, m_ref, acc_ref = state
    n_state = m_ref.shape[0]
    for j in range(n_state):
        _softmax_init(m_ref.at[j], acc_ref.at[j])

    def lanes_of(d):
        return slice(CK, TQ) if d == 1 else slice(None)

    def park(c, slot, d=None):
        lanes = lanes_of(d)
        ctx = c if prep is None else prep(c, lanes)
        for j, s in enumerate(qk_all(c, lanes)):
            if mask is not None:
                s = mask(ctx, j, s, lanes)
            if d is not None and causal_tail:
                s = jnp.where(_causal(d), s, NEG)
            s_ref[slot, j, :, lanes] = s
            mx_ref[slot, j, :, lanes] = jnp.max(s, axis=0, keepdims=True)

    def consume(c, slot, d=None):
        lanes = lanes_of(d)
        for j in range(n_state):
            _softmax_step(s_ref[slot, j, :, lanes], mx_ref[slot, j, :, lanes], vt_rows(c, j),
                          m_ref.at[j, :, lanes], acc_ref.at[j, :, lanes])

    def pair(c):
        park(c + 1, 1)
        consume(c, 0)
        park(c + 2, 0)
        consume(c + 1, 1)

    def body(g, carry):
        for u in range(0, FLASH_UNROLL, 2):
            pair(FLASH_UNROLL * g + u)
        return carry

    @pl.when(n_full == 0)
    def _():
        park(0, 0, d=0)
        park(1, 1, d=1)
        consume(0, 0)
        consume(1, 1, d=1)

    @pl.when(n_full > 0)
    def _():
        park(0, 0)
        n_loop = n_full - 2
        n_group = lax.shift_right_logical(n_loop, FLASH_UNROLL.bit_length() - 1)
        lax.fori_loop(0, n_group, body, 0)
        c0 = FLASH_UNROLL * n_group
        for u in range(FLASH_UNROLL // 2 - 1):
            @pl.when(n_loop - c0 >= 2 * (u + 1))
            def _(u=u):
                pair(c0 + 2 * u)
        c = n_loop
        park(c + 1, 1)
        consume(c, 0)
        park(c + 2, 0, d=0)
        consume(c + 1, 1)
        park(c + 3, 1, d=1)
        consume(c + 2, 0)
        consume(c + 3, 1, d=1)


def _causal(d):
    shape = (CK, TQ - d * CK)
    return lax.broadcasted_iota(I32, shape, 0) <= lax.broadcasted_iota(I32, shape, 1)


def _attn_scratch(n_state):
    return [pltpu.VMEM((2, n_state, CK, TQ), F32), pltpu.VMEM((2, n_state, 1, TQ), F32),
            pltpu.VMEM((n_state, 1, TQ), F32), pltpu.VMEM((n_state, VROWS, TQ), F32),
            pltpu.VMEM((BRANCH_W, TQ), F32)]


def _kv_specs(t, w):
    kspec = pl.BlockSpec((None, t, w), lambda bb, i: (bb, 0, 0))
    vspec = pl.BlockSpec((None, t // CK, N_HEADS * VROWS, CK), lambda bb, i: (bb, 0, 0, 0))
    return kspec, vspec


def _ln_kernel(x_ref, g_ref, b_ref, h_ref, hb_ref):
    x = x_ref[...]
    mu = jnp.mean(x, axis=1, keepdims=True)
    xc = x - mu
    var = jnp.mean(xc * xc, axis=1, keepdims=True)
    y = xc * lax.rsqrt(var + LN_EPS) * g_ref[...] + b_ref[...]
    h_ref[...] = y
    hb_ref[...] = y.astype(BF16)


def _layer_norm0(x2, g, b):
    n, d = x2.shape
    tm = 512
    row = pl.BlockSpec((tm, d), lambda i: (i, 0))
    vec = pl.BlockSpec((1, d), lambda i: (0, 0))
    return pl.pallas_call(
        _ln_kernel,
        out_shape=(jax.ShapeDtypeStruct((n, d), F32), jax.ShapeDtypeStruct((n, d), BF16)),
        grid=(n // tm,),
        in_specs=[row, vec, vec],
        out_specs=(row, row),
        compiler_params=_cparams(1),
        name="ln0",
    )(x2, g.reshape(1, d), b.reshape(1, d))


def _proj_plain_kernel(x_ref, w_ref, wt_ref, *out_refs, n_t):
    for g, o_ref in enumerate(out_refs[:n_t]):
        _store_vt(o_ref, _tn_dot(wt_ref[:, g * BRANCH_W:(g + 1) * BRANCH_W], x_ref[...]))
    off = 0
    for o_ref in out_refs[n_t:]:
        wd = o_ref.shape[-1]
        for j in range(0, wd, MXU_N):
            acc = jnp.dot(x_ref[...], w_ref[:, off + j:off + j + MXU_N], preferred_element_type=F32)
            o_ref[:, j:j + MXU_N] = acc.astype(o_ref.dtype)
        off += wd


def _proj_plain(hb3, w, wt, widths, l):
    b, t, d = hb3.shape
    tm = 512
    n_t = wt.shape[-1] // BRANCH_W
    shapes = [jax.ShapeDtypeStruct((b, t // CK, N_HEADS * VROWS, CK), BF16)] * n_t
    specs = [pl.BlockSpec((None, tm // CK, N_HEADS * VROWS, CK), lambda i, bb: (bb, i, 0, 0))] * n_t
    shapes += [jax.ShapeDtypeStruct((b, t, wd), BF16) for wd in widths]
    specs += [pl.BlockSpec((None, tm, wd), lambda i, bb: (bb, i, 0)) for wd in widths]
    return pl.pallas_call(
        functools.partial(_proj_plain_kernel, n_t=n_t),
        out_shape=tuple(shapes),
        grid=(t // tm, b),
        in_specs=[pl.BlockSpec((None, tm, d), lambda i, bb: (bb, i, 0)),
                  _layer_spec(w, l), _layer_spec(wt, l)],
        out_specs=tuple(specs),
        compiler_params=_cparams(2),
        name="proj_plain",
    )(hb3, w, wt)


def _proj_rope_kernel(x_ref, w_ref, c_ref, s_ref, *out_refs, heads, tables):
    lane = lax.broadcasted_iota(I32, (x_ref.shape[0], MXU_N), 1)
    for g, o_ref in enumerate(out_refs):
        hd, half = heads[g]
        sl = slice(g * MXU_N, (g + 1) * MXU_N)
        acc = jnp.dot(x_ref[...], w_ref[:, sl], preferred_element_type=F32)
        partner = jnp.where((lane & (hd - 1)) < half,
                            pltpu.roll(acc, MXU_N - half, 1), pltpu.roll(acc, half, 1))
        o_ref[...] = (acc * c_ref[tables[g]] + partner * s_ref[tables[g]]).astype(o_ref.dtype)


def _proj_rope(hb3, w, ctab, stab, heads, tables, l):
    b, t, d = hb3.shape
    tm = 512
    assert w.shape[-1] == MXU_N * len(heads)
    tspec = pl.BlockSpec((ctab.shape[0], tm, MXU_N), lambda i, bb: (0, i, 0))
    ospec = pl.BlockSpec((None, tm, MXU_N), lambda i, bb: (bb, i, 0))
    return pl.pallas_call(
        functools.partial(_proj_rope_kernel, heads=heads, tables=tables),
        out_shape=(jax.ShapeDtypeStruct((b, t, MXU_N), BF16),) * len(heads),
        grid=(t // tm, b),
        in_specs=[pl.BlockSpec((None, tm, d), lambda i, bb: (bb, i, 0)),
                  _layer_spec(w, l), tspec, tspec],
        out_specs=(ospec,) * len(heads),
        compiler_params=_cparams(2),
        name="proj_rope",
    )(hb3, w, ctab, stab)


def _dsa_kernel(aq_ref, ak_ref, avt_ref, iq_ref, ik_ref, iw_ref, pick_ref, tri_ref, o_ref,
                keys_ref, hi_ref, lo_ref, bk_ref, top_ref, sel_ref, iqt_ref, aqt_ref, wt_ref, thr_ref, s_ref, mx_ref, m_ref, acc_ref, ot_ref,
                *, topk, idx_scale):
    i = pl.program_id(1)
    n_full = 2 * i

    iqt = iq_ref[...].astype(F32).T
    for hh in range(IDX_HEADS):
        iqt_ref[hh] = iqt[hh * IDX_DIM:(hh + 1) * IDX_DIM, :].astype(BF16)
    _masked_qt(aq_ref[...].astype(F32) * (HEAD_DIM ** -0.5 * LOG2E), 6, N_HEADS, aqt_ref)
    wt_ref[...] = _nt_dot(pick_ref[...], iw_ref[...]) * idx_scale

    def lanes_of(d):
        return slice(CK, TQ) if d == 1 else slice(None)

    def logits(c, d):
        kc = ik_ref[pl.ds(pl.multiple_of(c * CK, CK), CK), :]
        return [jnp.dot(kc[:, :IDX_DIM], iqt_ref[hh, :, lanes_of(d)], preferred_element_type=F32)
                for hh in range(IDX_HEADS)]

    def put_keys(c, key, lanes):
        keys_ref[c, :, lanes] = key
        hi_ref[c, :, lanes] = (key >> 16).astype(I16)
        lo_ref[c, :, lanes] = ((key & 0xFFFF) - HALF16).astype(I16)

    def score_chunk(c, lg, d):
        lanes = lanes_of(d)
        sc = jnp.zeros(lg[0].shape, F32)
        for hh in range(IDX_HEADS):
            sc = sc + jnp.maximum(lg[hh], 0.0) * wt_ref[hh:hh + 1, lanes]
        bits = pltpu.bitcast(sc, I32)
        key = jnp.where(bits < 0, INT_MIN - bits, bits)
        put_keys(c, key if d is None else jnp.where(_causal(d), key, INT_MIN), lanes)
        if d == 1:
            put_keys(c, jnp.full((CK, CK), INT_MIN, I32), slice(0, CK))

    def score_pair(c, d0, d1):
        lg0, lg1 = logits(c, d0), logits(c + 1, d1)
        score_chunk(c, lg0, d0)
        score_chunk(c + 1, lg1, d1)

    def score_body(p, carry):
        score_pair(2 * p, None, None)
        return carry

    lax.fori_loop(0, i, score_body, 0)
    score_pair(n_full, 0, 1)

    def pair_loop(body, init, last=None):
        def pair(p, carry):
            return body(2 * p + 1, body(2 * p, carry))
        carry = body(n_full, lax.fori_loop(0, i, pair, init))
        return (last or body)(n_full + 1, carry)

    def count16(pred, also=None):
        def hits(c, lanes):
            hit = jnp.where(pred(c, lanes), jnp.int16(1), jnp.int16(0))
            if also is not None:
                hit = jnp.where(also(c, lanes), hit, jnp.int16(0))
            return _fold_rows(hit, 2 * SUBLANES)

        def body(c, part):
            return part + hits(c, slice(None))

        def last(c, part):
            return jnp.concatenate([part[:, :CK], part[:, CK:] + hits(c, slice(CK, TQ))], axis=1)

        part = pair_loop(body, jnp.zeros((2 * SUBLANES, TQ), I16), last)
        return jnp.sum(part.astype(F32), axis=0, keepdims=True)

    def search16(count_ge, need):
        def bit_body(bi, t_u):
            c_u = t_u | jnp.left_shift(jnp.int32(1), 15 - bi)
            cnt = count_ge((c_u - HALF16).astype(I16))
            return jnp.where(cnt >= need, c_u, t_u)
        return lax.fori_loop(0, 16, bit_body, jnp.zeros((1, TQ), I32))

    hi_u = search16(lambda ck: count16(lambda c, lanes: hi_ref[c, :, lanes] >= ck[:, lanes]),
                    float(topk))
    thr_hi = (hi_u - HALF16).astype(I16)
    n_above = count16(lambda c, lanes: hi_ref[c, :, lanes] > thr_hi[:, lanes])
    need_lo = float(topk) - n_above

    fill = jnp.int16(-HALF16)
    top_ref[...] = jnp.full(top_ref.shape, fill, I16)

    def bucket(c):
        return jnp.where(hi_ref[c] == thr_hi, lo_ref[c], fill)

    def top_body(c, carry):
        x = bucket(c)
        for r in range(4):
            m = top_ref[r]
            swap = x > m
            top_ref[r] = jnp.where(swap, x, m)
            x = jnp.where(swap, m, x)
        return carry

    many = i > 1

    @pl.when(many)
    def _():
        pair_loop(top_body, 0)

    fourth = jnp.max(_fold_rows(jnp.where(top_ref[3] > fill, jnp.int16(1), jnp.int16(0)),
                                2 * SUBLANES).astype(F32)) > 0.5
    use_top = jnp.logical_and(many, jnp.logical_not(fourth))

    def count_top(pred):
        part = jnp.zeros((2 * SUBLANES, TQ), I16)
        for r in range(3):
            part = part + _fold_rows(jnp.where(pred(top_ref[r]), jnp.int16(1), jnp.int16(0)), 2 * SUBLANES)
        return jnp.sum(part.astype(F32), axis=0, keepdims=True)

    def lo_select(count_ge, count_gt):
        lo_u = search16(count_ge, need_lo)
        sel_ref[0:1, :] = lo_u.astype(F32)
        sel_ref[1:2, :] = count_gt((lo_u - HALF16).astype(I16))

    @pl.when(use_top)
    def _():
        lo_select(lambda ck: count_top(lambda x: x >= ck), lambda t: count_top(lambda x: x > t))

    @pl.when(jnp.logical_not(use_top))
    def _():
        def bucket_body(c, carry):
            bk_ref[c] = bucket(c)
            return carry

        pair_loop(bucket_body, 0)
        lo_select(lambda ck: count16(lambda c, lanes: bk_ref[c, :, lanes] >= ck[:, lanes]),
                  lambda t: count16(lambda c, lanes: bk_ref[c, :, lanes] > t[:, lanes]))

    lo_u = sel_ref[0:1, :].astype(I32)
    thr_lo = (lo_u - HALF16).astype(I16)
    thr = ((hi_u - HALF16) << 16) | lo_u

    n_gt = n_above + sel_ref[1:2, :]
    n_eq = count16(lambda c, lanes: lo_ref[c, :, lanes] == thr_lo[:, lanes],
                   also=lambda c, lanes: hi_ref[c, :, lanes] == thr_hi[:, lanes])
    need = float(topk) - n_gt
    amb = jnp.logical_and(n_eq > need, thr > INT_MIN)
    any_amb = jnp.max(jnp.where(amb, 1.0, 0.0)) > 0.5

    @pl.when(any_amb)
    def _():
        def drop_body(c, seen):
            k = keys_ref[c]
            eq = k == thr
            eqf = jnp.where(eq, 1.0, 0.0)
            rank = jnp.dot(tri_ref[...], eqf.astype(BF16), preferred_element_type=F32) + seen
            drop = jnp.logical_and(jnp.logical_and(eq, rank > need), amb)
            keys_ref[c] = jnp.where(drop, INT_MIN, k)
            return seen + jnp.sum(eqf, axis=0, keepdims=True)

        pair_loop(drop_body, jnp.zeros((1, TQ), F32))

    thr_ref[...] = jnp.maximum(thr, INT_MIN + 1)

    def qk_all(c, lanes):
        kc = ak_ref[pl.ds(pl.multiple_of(c * CK, CK), CK), :]
        return [jnp.dot(_half(kc, h, 6), aqt_ref[h, :, lanes], preferred_element_type=F32) for h in range(N_HEADS)]

    _flash_loop(n_full, qk_all,
                lambda bias, h, s, lanes: s + bias,
                lambda c, h: avt_ref[c, h * VROWS:(h + 1) * VROWS, :],
                (s_ref, mx_ref, m_ref, acc_ref),
                prep=lambda c, lanes: jnp.where(keys_ref[c, :, lanes] >= thr_ref[:, lanes], 0.0, NEG),
                causal_tail=False)
    for h in range(N_HEADS):
        ot_ref[h * HEAD_DIM:(h + 1) * HEAD_DIM, :] = _softmax_out(acc_ref.at[h])
    o_ref[...] = ot_ref[...].T.astype(o_ref.dtype)


def _dsa(aq, ak, avt, iq, ik, iw):
    b, t, _ = aq.shape
    topk = min(TOPK_MAX, t // 4)
    qspec = pl.BlockSpec((None, TQ, BRANCH_W), lambda bb, i: (bb, i, 0))
    kspec, vspec = _kv_specs(t, BRANCH_W)
    pick = np.zeros((2 * SUBLANES, MXU_N), np.float32)
    for hh in range(IDX_HEADS):
        pick[hh, IDX_DIM + hh] = 1.0
    pick = jnp.asarray(pick, BF16)
    tri = jnp.asarray(np.tril(np.ones((CK, CK), np.float32)), BF16)
    kern = functools.partial(_dsa_kernel, topk=topk, idx_scale=(IDX_HEADS * IDX_DIM) ** -0.5)
    return pl.pallas_call(
        kern,
        out_shape=jax.ShapeDtypeStruct((b, t, BRANCH_W), BF16),
        grid=(b, t // TQ),
        in_specs=[qspec, kspec, vspec, qspec, kspec, qspec,
                  pl.BlockSpec(pick.shape, lambda bb, i: (0, 0)), pl.BlockSpec(tri.shape, lambda bb, i: (0, 0))],
        out_specs=qspec,
        scratch_shapes=[
            pltpu.VMEM((t // CK, CK, TQ), I32),
            pltpu.VMEM((t // CK, CK, TQ), I16),
            pltpu.VMEM((t // CK, CK, TQ), I16),
            pltpu.VMEM((t // CK, CK, TQ), I16),
            pltpu.VMEM((4, CK, TQ), I16),
            pltpu.VMEM((SUBLANES, TQ), F32),
            pltpu.VMEM((IDX_HEADS, IDX_DIM, TQ), BF16),
            pltpu.VMEM((N_HEADS, LANES, TQ), BF16),
            pltpu.VMEM((2 * SUBLANES, TQ), F32),
            pltpu.VMEM((1, TQ), I32),
        ] + _attn_scratch(N_HEADS),
        compiler_params=_cparams(2),
        name="dsa",
    )(aq, ak, avt, iq, ik, iw, pick, tri)


def _kbar_kernel(k_ref, o_ref):
    o_ref[...] = jnp.zeros(o_ref.shape, o_ref.dtype)
    nb = k_ref.shape[0] // MOBA_BLOCK
    for n in range(nb):
        blk = k_ref[n * MOBA_BLOCK:(n + 1) * MOBA_BLOCK, :].astype(F32)
        o_ref[n:n + 1, :] = jnp.mean(blk, axis=0, keepdims=True).astype(o_ref.dtype)


def _kbar(bk):
    b, t, w = bk.shape
    nbp = max(2 * SUBLANES, t // MOBA_BLOCK)
    return pl.pallas_call(
        _kbar_kernel,
        out_shape=jax.ShapeDtypeStruct((b, nbp, w), BF16),
        grid=(b,),
        in_specs=[pl.BlockSpec((None, t, w), lambda bb: (bb, 0, 0))],
        out_specs=pl.BlockSpec((None, nbp, w), lambda bb: (bb, 0, 0)),
        compiler_params=_cparams(1),
        name="moba_kbar",
    )(bk)


def _moba_kernel(q_ref, k_ref, vt_ref, kbar_ref, o_ref, qt_ref, bias_ref, s_ref, mx_ref, m_ref, acc_ref, ot_ref):
    i = pl.program_id(1)
    nbp = kbar_ref.shape[0]
    blk = lax.broadcasted_iota(I32, (nbp, TQ), 0)
    blk_f = blk.astype(F32)
    own = 2 * i + (lax.broadcasted_iota(I32, (nbp, TQ), 1) >> (MOBA_BLOCK.bit_length() - 1))
    _masked_qt(q_ref[...].astype(F32) * (HEAD_DIM ** -0.5 * LOG2E), 6, N_HEADS, qt_ref)

    for h in range(N_HEADS):
        g = jnp.where(blk < own, jnp.dot(_half(kbar_ref[...], h, 6), qt_ref[h], preferred_element_type=F32), NEG)
        bias = jnp.full((nbp, TQ), NEG, F32)
        for _ in range(MOBA_TOPK):
            mx = jnp.max(g, axis=0, keepdims=True)
            first = jnp.min(jnp.where(g == mx, blk_f, 1e9), axis=0, keepdims=True)
            pick = jnp.logical_and(blk_f == first, mx > 0.5 * NEG)
            bias = jnp.where(pick, 0.0, bias)
            g = jnp.where(pick, NEG, g)
        bias_ref[h] = jnp.where(blk == own, 0.0, bias)

    def qk_all(c, lanes):
        kc = k_ref[pl.ds(pl.multiple_of(c * CK, CK), CK), :]
        return [jnp.dot(_half(kc, h, 6), qt_ref[h, :, lanes], preferred_element_type=F32) for h in range(N_HEADS)]

    _flash_loop(2 * i, qk_all, lambda c, h, s, lanes: s + bias_ref[h, pl.ds(c, 1), lanes],
                lambda c, h: vt_ref[c, h * VROWS:(h + 1) * VROWS, :], (s_ref, mx_ref, m_ref, acc_ref))
    for h in range(N_HEADS):
        ot_ref[h * HEAD_DIM:(h + 1) * HEAD_DIM, :] = _softmax_out(acc_ref.at[h])
    o_ref[...] = ot_ref[...].T.astype(o_ref.dtype)


def _moba(bq, bk, bvt, kbar):
    b, t, w = bq.shape
    assert TQ == 2 * MOBA_BLOCK and CK == MOBA_BLOCK and t % TQ == 0
    nbp = kbar.shape[1]
    qspec = pl.BlockSpec((None, TQ, w), lambda bb, i: (bb, i, 0))
    kspec, vspec = _kv_specs(t, w)
    return pl.pallas_call(
        _moba_kernel,
        out_shape=jax.ShapeDtypeStruct((b, t, w), BF16),
        grid=(b, t // TQ),
        in_specs=[qspec, kspec, vspec, pl.BlockSpec((None, nbp, w), lambda bb, i: (bb, 0, 0))],
        out_specs=qspec,
        scratch_shapes=[pltpu.VMEM((N_HEADS, LANES, TQ), BF16), pltpu.VMEM((N_HEADS, nbp, TQ), F32)]
        + _attn_scratch(N_HEADS),
        compiler_params=_cparams(2),
        name="moba",
    )(bq, bk, bvt, kbar)


def _diff_kernel(q_ref, k_ref, vt_ref, lam_ref, norm_ref, misc_ref, o_ref,
                 qt_ref, s_ref, mx_ref, m_ref, acc_ref, ot_ref):
    i = pl.program_id(1)
    _masked_qt(q_ref[...].astype(F32) * (DIFF_DIM ** -0.5 * LOG2E), 5, 2 * N_HEADS, qt_ref)

    dl = lam_ref[...]
    lam_init = misc_ref[0:1, 0:1]
    lam = (jnp.exp(jnp.sum(dl[0:1, :] * dl[1:2, :], axis=1, keepdims=True))
           - jnp.exp(jnp.sum(dl[2:3, :] * dl[3:4, :], axis=1, keepdims=True)) + lam_init)

    def qk_all(c, lanes):
        kc = k_ref[pl.ds(pl.multiple_of(c * CK, CK), CK), :]
        return [jnp.dot(_half(kc, j, 5), qt_ref[j, :, lanes], preferred_element_type=F32) for j in range(2 * N_HEADS)]

    _flash_loop(2 * i, qk_all, None,
                lambda c, j: vt_ref[c, (j // 2) * VROWS:(j // 2 + 1) * VROWS, :],
                (s_ref, mx_ref, m_ref, acc_ref))

    post = norm_ref[...] * (1.0 - lam_init)
    for h in range(N_HEADS):
        o_h = _softmax_out(acc_ref.at[2 * h]) - lam * _softmax_out(acc_ref.at[2 * h + 1])
        ms = jnp.mean(o_h * o_h, axis=0, keepdims=True)
        ot_ref[h * HEAD_DIM:(h + 1) * HEAD_DIM, :] = o_h * lax.rsqrt(ms + RMS_EPS) * post
    o_ref[...] = ot_ref[...].T.astype(o_ref.dtype)


def _diff(cq, ck, cvt, lam, norm, misc):
    b, t, w = cq.shape
    qspec = pl.BlockSpec((None, TQ, w), lambda bb, i: (bb, i, 0))
    kspec, vspec = _kv_specs(t, w)
    full = lambda a: pl.BlockSpec(a.shape, lambda bb, i: (0,) * a.ndim)
    return pl.pallas_call(
        _diff_kernel,
        out_shape=jax.ShapeDtypeStruct((b, t, w), BF16),
        grid=(b, t // TQ),
        in_specs=[qspec, kspec, vspec, full(lam), full(norm), full(misc)],
        out_specs=qspec,
        scratch_shapes=[pltpu.VMEM((2 * N_HEADS, LANES, TQ), BF16)] + _attn_scratch(2 * N_HEADS),
        compiler_params=_cparams(2),
        name="diff",
    )(cq, ck, cvt, lam, norm, misc)


def _mla_prep_kernel(cq_ref, ckv_ref, kr_ref, qn_ref, kvn_ref, wq_ref, wqr_ref, wk_ref, wvt_ref,
                     p_ref, ct_ref, st_ref, q_out, k_out, vt_out):
    x = cq_ref[...].astype(F32)
    xn = (x * lax.rsqrt(jnp.mean(x * x, axis=1, keepdims=True) + RMS_EPS) * qn_ref[...]).astype(BF16)
    q = (jnp.dot(xn, wq_ref[...], preferred_element_type=F32) * ct_ref[...]
         + jnp.dot(xn, wqr_ref[...], preferred_element_type=F32) * st_ref[...])
    q_out[...] = q.astype(q_out.dtype)
    c = ckv_ref[:, :KV_LORA].astype(F32)
    cn = (c * lax.rsqrt(jnp.mean(c * c, axis=1, keepdims=True) + RMS_EPS) * kvn_ref[...]).astype(BF16)
    k = (jnp.dot(cn, wk_ref[...], preferred_element_type=F32)
         + jnp.dot(kr_ref[...], p_ref[...], preferred_element_type=F32))
    k_out[...] = k.astype(k_out.dtype)
    _store_vt(vt_out, _tn_dot(wvt_ref[...], cn))


def _mla_prep(dcq, ckv, kr, qn, kvn, wq, wqr, wk, wvt, pmat, ct, st, l):
    b, t, _ = dcq.shape
    tm = 2048
    hw = N_HEADS * LANES
    row = lambda w: pl.BlockSpec((None, tm, w), lambda i, bb: (bb, i, 0))
    full = lambda a: pl.BlockSpec(a.shape, lambda i, bb: (0,) * a.ndim)
    tab = pl.BlockSpec((tm, hw), lambda i, bb: (i, 0))
    return pl.pallas_call(
        _mla_prep_kernel,
        out_shape=(jax.ShapeDtypeStruct((b, t, hw), BF16), jax.ShapeDtypeStruct((b, t, hw), BF16),
                   jax.ShapeDtypeStruct((b, t // CK, N_HEADS * VROWS, CK), BF16)),
        grid=(t // tm, b),
        in_specs=[row(Q_LORA), row(MXU_N), row(MXU_N), full(qn), full(kvn), _layer_spec(wq, l), _layer_spec(wqr, l),
                  _layer_spec(wk, l), _layer_spec(wvt, l), full(pmat), tab, tab],
        out_specs=(row(hw), row(hw),
                   pl.BlockSpec((None, tm // CK, N_HEADS * VROWS, CK), lambda i, bb: (bb, i, 0, 0))),
        compiler_params=_cparams(2),
        name="mla_prep",
    )(dcq, ckv, kr, qn, kvn, wq, wqr, wk, wvt, pmat, ct, st)


def _mla_kernel(q_ref, k_ref, vt_ref, o_ref, qt_ref, s_ref, mx_ref, m_ref, acc_ref, ot_ref):
    i = pl.program_id(1)
    hs = [slice(h * LANES, (h + 1) * LANES) for h in range(N_HEADS)]
    for h in range(N_HEADS):
        qt_ref[h] = q_ref[:, hs[h]].astype(F32).T.astype(BF16)

    def qk_all(c, lanes):
        start = pl.multiple_of(c * CK, CK)
        return [jnp.dot(k_ref[pl.ds(start, CK), hs[h]], qt_ref[h, :, lanes], preferred_element_type=F32)
                for h in range(N_HEADS)]

    _flash_loop(2 * i, qk_all, None,
                lambda c, h: vt_ref[c, h * VROWS:(h + 1) * VROWS, :],
                (s_ref, mx_ref, m_ref, acc_ref))
    for h in range(N_HEADS):
        ot_ref[h * HEAD_DIM:(h + 1) * HEAD_DIM, :] = _softmax_out(acc_ref.at[h])
    o_ref[...] = ot_ref[...].T.astype(o_ref.dtype)


def _mla(qm, km, vmt):
    b, t, hw = qm.shape
    kspec, vspec = _kv_specs(t, hw)
    return pl.pallas_call(
        _mla_kernel,
        out_shape=jax.ShapeDtypeStruct((b, t, BRANCH_W), BF16),
        grid=(b, t // TQ),
        in_specs=[pl.BlockSpec((None, TQ, hw), lambda bb, i: (bb, i, 0)), kspec, vspec],
        out_specs=pl.BlockSpec((None, TQ, BRANCH_W), lambda bb, i: (bb, i, 0)),
        scratch_shapes=[pltpu.VMEM((N_HEADS, LANES, TQ), BF16)] + _attn_scratch(N_HEADS),
        compiler_params=_cparams(2),
        name="mla",
    )(qm, km, vmt)


def _mem_kv_kernel(x_ref, w_ref, k_ref, vt_ref):
    x = x_ref[...].astype(BF16)
    k_ref[...] = jnp.dot(x, w_ref[:, :BRANCH_W], preferred_element_type=F32).astype(k_ref.dtype)
    _store_vt(vt_ref, _tn_dot(w_ref[:, BRANCH_W:], x))


def _mem_kv(mem, w, l):
    b, m, d = mem.shape
    assert m % CK == 0
    return pl.pallas_call(
        _mem_kv_kernel,
        out_shape=(jax.ShapeDtypeStruct((b, m, BRANCH_W), BF16),
                   jax.ShapeDtypeStruct((b, m // CK, N_HEADS * VROWS, CK), BF16)),
        grid=(b,),
        in_specs=[pl.BlockSpec((None, m, d), lambda bb: (bb, 0, 0)), _layer_spec(w, l)],
        out_specs=(pl.BlockSpec((None, m, BRANCH_W), lambda bb: (bb, 0, 0)),
                   pl.BlockSpec((None, m // CK, N_HEADS * VROWS, CK), lambda bb: (bb, 0, 0, 0))),
        compiler_params=_cparams(1),
        name="mem_kv",
    )(mem, w)


def _mem_kernel(q_ref, k_ref, vt_ref, o_ref, qt_ref, ot_ref):
    _masked_qt(q_ref[...].astype(F32) * (HEAD_DIM ** -0.5 * LOG2E), 6, N_HEADS, qt_ref)
    s_all = [jnp.dot(_half(k_ref[...], h, 6), qt_ref[h], preferred_element_type=F32) for h in range(N_HEADS)]
    for h in range(N_HEADS):
        s_t = s_all[h]
        p = jnp.exp2(s_t - jnp.max(s_t, axis=0, keepdims=True)).astype(BF16)
        acc = jnp.dot(vt_ref[0, h * VROWS:(h + 1) * VROWS, :], p, preferred_element_type=F32)
        ot_ref[h * HEAD_DIM:(h + 1) * HEAD_DIM, :] = acc[:HEAD_DIM, :] / acc[HEAD_DIM:HEAD_DIM + 1, :]
    o_ref[...] = ot_ref[...].T.astype(o_ref.dtype)


def _mem_attn(eq, mk, mvt):
    b, t, w = eq.shape
    m = mk.shape[1]
    assert m == CK
    tm = 1024
    return pl.pallas_call(
        _mem_kernel,
        out_shape=jax.ShapeDtypeStruct((b, t, w), BF16),
        grid=(b, t // tm),
        in_specs=[pl.BlockSpec((None, tm, w), lambda bb, i: (bb, i, 0)),
                  pl.BlockSpec((None, m, w), lambda bb, i: (bb, 0, 0)),
                  pl.BlockSpec((None,) + mvt.shape[1:], lambda bb, i: (bb, 0, 0, 0))],
        out_specs=pl.BlockSpec((None, tm, w), lambda bb, i: (bb, i, 0)),
        scratch_shapes=[pltpu.VMEM((N_HEADS, LANES, tm), BF16), pltpu.VMEM((BRANCH_W, tm), F32)],
        compiler_params=_cparams(2),
        name="mem_attn",
    )(eq, mk, mvt)


def _final_kernel(h_ref, hb_ref, oa_ref, ob_ref, oc_ref, od_ref, oe_ref, z_ref,
                  wg_ref, wb_ref, wo_ref, g_ref, b_ref, h_out, hb_out, acc_ref, *, alpha):
    d = h_ref.shape[1]
    half = h_ref.shape[0] // 2
    for n, o_ref in enumerate((oa_ref, ob_ref, oc_ref, od_ref, oe_ref)):
        for r in range(2):
            rows = slice(r * half, (r + 1) * half)
            z = z_ref[rows, n * BRANCH_W:(n + 1) * BRANCH_W].astype(F32)
            y = o_ref[rows, :].astype(F32) * (z / (1.0 + jnp.exp(-z)))
            u = jnp.dot(y.astype(BF16), wb_ref[n], preferred_element_type=F32)
            g = jnp.dot(hb_ref[rows, :], wg_ref[:, n * d:(n + 1) * d], preferred_element_type=F32)
            t = u / (1.0 + jnp.exp(-g))
            acc_ref[rows, :] = t if n == 0 else acc_ref[rows, :] + t
    for r in range(2):
        rows = slice(r * half, (r + 1) * half)
        out = jnp.dot(acc_ref[rows, :].astype(BF16), wo_ref[...], preferred_element_type=F32)
        x = alpha * h_ref[rows, :] + out
        mu = jnp.mean(x, axis=1, keepdims=True)
        xc = x - mu
        var = jnp.mean(xc * xc, axis=1, keepdims=True)
        y = xc * lax.rsqrt(var + LN_EPS) * g_ref[...] + b_ref[...]
        h_out[rows, :] = y
        hb_out[rows, :] = y.astype(BF16)


def _final(h, hb, os5, z, wg, wb, wo, ln_g, ln_b, alpha, l):
    n, d = h.shape
    tm = 512
    row = lambda w: pl.BlockSpec((tm, w), lambda i: (i, 0))
    full = lambda a: pl.BlockSpec(a.shape, lambda i: (0,) * a.ndim)
    return pl.pallas_call(
        functools.partial(_final_kernel, alpha=alpha),
        out_shape=(jax.ShapeDtypeStruct((n, d), F32), jax.ShapeDtypeStruct((n, d), BF16)),
        grid=(n // tm,),
        in_specs=[row(d), row(d)] + [row(BRANCH_W)] * N_BRANCH + [row(N_BRANCH * BRANCH_W),
                  _layer_spec(wg, l), _layer_spec(wb, l), _layer_spec(wo, l), full(ln_g), full(ln_b)],
        out_specs=(row(d), row(d)),
        scratch_shapes=[pltpu.VMEM((tm, d), F32)],
        compiler_params=_cparams(1),
        name="merge_out_ln",
    )(h, hb, *os5, z, wg, wb, wo, ln_g, ln_b)


ROPE_GROUPS = (("a_q", N_HEADS, HEAD_DIM, ROT_64), ("a_k", N_HEADS, HEAD_DIM, ROT_64),
               ("i_q", IDX_HEADS, IDX_DIM, ROT_32), ("i_k", 1, MXU_N, ROT_32),
               ("b_q", N_HEADS, HEAD_DIM, ROT_64), ("b_k", N_HEADS, HEAD_DIM, ROT_64),
               ("c_q", 2 * N_HEADS, DIFF_DIM, ROT_32), ("c_k", 2 * N_HEADS, DIFF_DIM, ROT_32),
               ("d_kr", 1, MXU_N, MLA_ROPE))
PLAIN_COLS = ("d_cq", "d_ckv", "e_q") + tuple(("z", j) for j in range(N_BRANCH))
VALUE_COLS = ("a_v", "b_v", "c_v")
GATE_COLS = tuple(("g", j) for j in range(OFF["g"][1] // MXU_N))


def _window_start(col):
    name, j = col if isinstance(col, tuple) else (col, 0)
    return OFF[name][0] + j * MXU_N


def _weight_prep_kernel(offs_ref, wt_ref, o_ref):
    o_ref[...] = wt_ref[...].T.astype(o_ref.dtype)


def _weight_windows(wt, cols, name):
    depth, n, d = wt.shape
    starts = [_window_start(c) for c in cols]
    assert all(st % SUBLANES == 0 and st + MXU_N <= n for st in starts)
    grid_spec = pltpu.PrefetchScalarGridSpec(
        num_scalar_prefetch=1,
        grid=(depth, len(cols)),
        in_specs=[pl.BlockSpec((None, pl.Element(MXU_N), pl.Element(d)),
                               lambda l, j, offs: (l, pl.multiple_of(offs[j], SUBLANES), 0))],
        out_specs=pl.BlockSpec((None, d, MXU_N), lambda l, j, offs: (l, 0, j)),
    )
    return pl.pallas_call(
        _weight_prep_kernel,
        out_shape=jax.ShapeDtypeStruct((depth, d, MXU_N * len(cols)), BF16),
        grid_spec=grid_spec,
        compiler_params=_cparams(2),
        name=name,
    )(jnp.asarray(np.asarray(starts, np.int32)), wt)


def _weight_prep(w_in):
    wt = jnp.swapaxes(w_in, 1, 2)
    return (_weight_windows(wt, PLAIN_COLS, "wprep_plain"), _weight_windows(wt, VALUE_COLS, "wprep_value"),
            _weight_windows(wt, [name for name, *_ in ROPE_GROUPS], "wprep_rope"),
            _weight_windows(wt, GATE_COLS, "wprep_gate"))


def _rope_tables(seq, rot_dim):
    pos = jnp.arange(seq, dtype=F32)
    inv = ROPE_THETA ** (-jnp.arange(0, rot_dim, 2, dtype=F32) / rot_dim)
    ang = pos[:, None] * inv[None, :]
    return jnp.cos(ang), jnp.sin(ang)


def _rope_cs(t, nh, hd, r):
    cos, sin = _rope_tables(t, r)
    c = jnp.concatenate([cos, cos, jnp.ones((t, hd - r), F32)], axis=1)
    s = jnp.concatenate([-sin, sin, jnp.zeros((t, hd - r), F32)], axis=1)
    return jnp.tile(c, (1, nh)), jnp.tile(s, (1, nh))


def kernel(x, mem, ln0_g, ln0_b, w_in, mla_q_norm, w_uq, mla_kv_norm, w_ukv, diff_lam, diff_norm,
           w_mem_kv, w_branch, w_out, ln_g, ln_b):
    b, t, d = x.shape
    depth = w_in.shape[0]
    alpha = (2 * depth) ** 0.25
    assert t % 512 == 0 and d == 1024

    w_plain, w_vt, w_rope, wg = _weight_prep(w_in)
    plain_widths = (BRANCH_W,) * 3 + (N_BRANCH * BRANCH_W,)
    rope_heads = tuple((hd, r // 2) for _, _, hd, r in ROPE_GROUPS)
    patterns = sorted(set((nh, hd, r) for _, nh, hd, r in ROPE_GROUPS))
    rope_tables = tuple(patterns.index((nh, hd, r)) for _, nh, hd, r in ROPE_GROUPS)
    cs = [_rope_cs(t, nh, hd, r) for nh, hd, r in patterns]
    ctab = jnp.stack([c for c, _ in cs])
    stab = jnp.stack([s for _, s in cs])

    uq = w_uq.reshape(depth, Q_LORA, N_HEADS, MLA_NOPE + MLA_ROPE)
    qn_w, qr_w = uq[..., :MLA_NOPE], uq[..., MLA_NOPE:]
    pad32 = jnp.zeros((depth, Q_LORA, N_HEADS, LANES - MLA_NOPE - MLA_ROPE), w_uq.dtype)
    hw = N_HEADS * LANES
    wq = jnp.concatenate([qn_w, qr_w, pad32], axis=-1).reshape(depth, Q_LORA, hw).astype(BF16)
    half = MLA_ROPE // 2
    wq_rot = jnp.concatenate([jnp.zeros_like(qn_w), -qr_w[..., half:], qr_w[..., :half], pad32],
                             axis=-1).reshape(depth, Q_LORA, hw).astype(BF16)
    cos_m, sin_m = _rope_tables(t, MLA_ROPE)
    one = lambda n: jnp.ones((t, n), F32)
    zer = lambda n: jnp.zeros((t, n), F32)
    qs = (MLA_NOPE + MLA_ROPE) ** -0.5 * LOG2E
    ct_q = qs * jnp.tile(jnp.concatenate([one(MLA_NOPE), cos_m, cos_m, one(LANES - MLA_NOPE - MLA_ROPE)], axis=1), (1, N_HEADS))
    st_q = qs * jnp.tile(jnp.concatenate([zer(MLA_NOPE), sin_m, sin_m, zer(LANES - MLA_NOPE - MLA_ROPE)], axis=1), (1, N_HEADS))
    ukv = w_ukv.reshape(depth, KV_LORA, N_HEADS, MLA_NOPE + MLA_V)
    wk = jnp.concatenate([ukv[..., :MLA_NOPE], jnp.zeros((depth, KV_LORA, N_HEADS, LANES - MLA_NOPE), w_ukv.dtype)],
                         axis=-1).reshape(depth, KV_LORA, hw).astype(BF16)
    wvt = ukv[..., MLA_NOPE:].reshape(depth, KV_LORA, N_HEADS * MLA_V).astype(BF16)
    place = np.zeros((MXU_N, hw), np.float32)
    for hh in range(N_HEADS):
        for j in range(MLA_ROPE):
            place[j, hh * LANES + MLA_NOPE + j] = 1.0
    place = jnp.asarray(place, BF16)

    wb = w_branch.astype(BF16)
    wo = w_out.astype(BF16)
    wmem = w_mem_kv.astype(BF16)
    norm_t = jnp.broadcast_to(diff_norm.astype(F32)[:, :, None], (depth, HEAD_DIM, TQ))

    h, hb = _layer_norm0(x.reshape(b * t, d), ln0_g, ln0_b)
    for l in range(depth):
        hb3 = hb.reshape(b, t, d)
        avt, bvt, cvt, dcq, ckv_iw, eq, z = _proj_plain(hb3, w_plain, w_vt, plain_widths, l)
        aq, ak, iq, ik, bq, bk, cq, ck, kr = _proj_rope(hb3, w_rope, ctab, stab, rope_heads, rope_tables, l)

        o_a = _dsa(aq, ak, avt, iq, ik, ik)
        o_b = _moba(bq, bk, bvt, _kbar(bk))
        lam_init = 0.8 - 0.6 * math.exp(-0.3 * l)
        misc = jnp.full((SUBLANES, LANES), lam_init, F32)
        o_c = _diff(cq, ck, cvt, diff_lam[l].astype(F32), norm_t[l], misc)
        qm, km, vmt = _mla_prep(dcq, ckv_iw, kr, mla_q_norm[l].reshape(1, Q_LORA), mla_kv_norm[l].reshape(1, KV_LORA),
                                wq, wq_rot, wk, wvt, place, ct_q, st_q, l)
        o_d = _mla(qm, km, vmt)
        o_e = _mem_attn(eq, *_mem_kv(mem, wmem, l))

        os5 = [o.reshape(b * t, BRANCH_W) for o in (o_a, o_b, o_c, o_d, o_e)]
        h, hb = _final(h, hb, os5, z.reshape(b * t, N_BRANCH * BRANCH_W), wg, wb, wo,
                       ln_g[l].reshape(1, d), ln_b[l].reshape(1, d), alpha, l)
    return h.reshape(b, t, d)
```

```python
import functools
import math

import numpy as np
import jax
import jax.numpy as jnp
from jax import lax
from jax.experimental import pallas as pl
from jax.experimental.pallas import tpu as pltpu

F32 = jnp.float32
BF16 = jnp.bfloat16
I32 = jnp.int32
I16 = jnp.int16

N_HEADS = 4
HEAD_DIM = 64
BRANCH_W = N_HEADS * HEAD_DIM
N_BRANCH = 5
ROPE_THETA = 500000.0
ROT_64 = 16
ROT_32 = 8
IDX_HEADS = 8
IDX_DIM = 32
TOPK_MAX = 256
MOBA_BLOCK = 256
MOBA_TOPK = 3
DIFF_DIM = 32
Q_LORA = 256
KV_LORA = 128
MLA_NOPE = 64
MLA_ROPE = 32
MLA_V = 64
LN_EPS = 1e-5
RMS_EPS = 1e-6

IN_LAYOUT = (
    ("a_q", BRANCH_W), ("a_k", BRANCH_W), ("a_v", BRANCH_W),
    ("i_q", IDX_HEADS * IDX_DIM), ("i_k", IDX_DIM), ("i_w", IDX_HEADS),
    ("b_q", BRANCH_W), ("b_k", BRANCH_W), ("b_v", BRANCH_W),
    ("c_q", BRANCH_W), ("c_k", BRANCH_W), ("c_v", BRANCH_W),
    ("d_cq", Q_LORA), ("d_ckv", KV_LORA), ("d_kr", MLA_ROPE),
    ("e_q", BRANCH_W),
    ("z", N_BRANCH * BRANCH_W),
    ("g", N_BRANCH * 1024),
)

SUBLANES = 8
LANES = 128
MXU_N = 256
TQ = 512
CK = 256
VROWS = HEAD_DIM + 16
FLASH_UNROLL = 4
NEG = -1e30
LOG2E = math.log2(math.e)
INT_MIN = np.int32(-2 ** 31)
HALF16 = 1 << 15
VMEM_LIMIT = 56 * 1024 * 1024


def _offsets():
    off, out = 0, {}
    for name, size in IN_LAYOUT:
        out[name] = (off, size)
        off += size
    return out


OFF = _offsets()


def _nt_dot(a, b):
    return lax.dot_general(a, b, (((1,), (1,)), ((), ())), preferred_element_type=F32)


def _tn_dot(w, x):
    return lax.dot_general(w, x, (((0,), (1,)), ((), ())), preferred_element_type=F32)


def _fold_rows(w, rows=SUBLANES):
    xs = [w[r:r + rows, :] for r in range(0, w.shape[0], rows)]
    while len(xs) > 1:
        xs = [xs[j] + xs[j + 1] for j in range(0, len(xs) - 1, 2)] + ([xs[-1]] if len(xs) % 2 else [])
    return xs[0]


def _masked_qt(q, shift, n, qt_ref):
    qt = q.T
    dim = lax.broadcasted_iota(I32, (LANES, qt.shape[1]), 0)
    for j in range(n):
        half = (j << shift) // LANES
        rows = qt[half * LANES:(half + 1) * LANES, :]
        qt_ref[j] = jnp.where(((dim + half * LANES) >> shift) == j, rows, 0.0).astype(BF16)


def _half(kc, j, shift):
    half = (j << shift) // LANES
    return kc[:, half * LANES:(half + 1) * LANES]


def _cparams(n_axes):
    return pltpu.CompilerParams(dimension_semantics=("arbitrary",) * n_axes,
                                vmem_limit_bytes=VMEM_LIMIT)


def _layer_spec(a, l):
    return pl.BlockSpec((None,) + a.shape[1:], lambda *_: (l,) + (0,) * (a.ndim - 1))


def _softmax_step(s_t, m_tile, vt_h, m_ref, acc_ref):
    m_old = m_ref[...]
    m_new = jnp.maximum(m_old, m_tile)
    alpha = jnp.exp2(m_old - m_new)
    p = jnp.exp2(s_t - m_new)
    acc_ref[...] = alpha * acc_ref[...] + jnp.dot(vt_h, p.astype(BF16), preferred_element_type=F32)
    m_ref[...] = m_new


def _softmax_init(m_ref, acc_ref):
    m_ref[...] = jnp.full(m_ref.shape, NEG, F32)
    acc_ref[...] = jnp.zeros(acc_ref.shape, F32)


def _softmax_out(acc_ref):
    return acc_ref[:HEAD_DIM, :] / acc_ref[HEAD_DIM:HEAD_DIM + 1, :]


def _store_vt(o_ref, vt):
    ones = jnp.ones((VROWS - HEAD_DIM, CK), o_ref.dtype)
    for j in range(o_ref.shape[0]):
        for h in range(N_HEADS):
            o_ref[j, h * VROWS:h * VROWS + HEAD_DIM, :] = (
                vt[h * HEAD_DIM:(h + 1) * HEAD_DIM, j * CK:(j + 1) * CK].astype(o_ref.dtype))
            o_ref[j, h * VROWS + HEAD_DIM:(h + 1) * VROWS, :] = ones


def _flash_loop(n_full, qk_all, mask, vt_rows, state, prep=None, causal_tail=True):
    s_ref, mx_ref, m_ref, acc_ref = state
    n_state = m_ref.shape[0]
    for j in range(n_state):
        _softmax_init(m_ref.at[j], acc_ref.at[j])

    def lanes_of(d):
        return slice(CK, TQ) if d == 1 else slice(None)

    def park(c, slot, d=None):
        lanes = lanes_of(d)
        ctx = c if prep is None else prep(c, lanes)
        for j, s in enumerate(qk_all(c, lanes)):
            if mask is not None:
                s = mask(ctx, j, s, lanes)
            if d is not None and causal_tail:
                s = jnp.where(_causal(d), s, NEG)
            s_ref[slot, j, :, lanes] = s
            mx_ref[slot, j, :, lanes] = jnp.max(s, axis=0, keepdims=True)

    def consume(c, slot, d=None):
        lanes = lanes_of(d)
        for j in range(n_state):
            _softmax_step(s_ref[slot, j, :, lanes], mx_ref[slot, j, :, lanes], vt_rows(c, j),
                          m_ref.at[j, :, lanes], acc_ref.at[j, :, lanes])

    def pair(c):
        park(c + 1, 1)
        consume(c, 0)
        park(c + 2, 0)
        consume(c + 1, 1)

    def body(g, carry):
        for u in range(0, FLASH_UNROLL, 2):
            pair(FLASH_UNROLL * g + u)
        return carry

    @pl.when(n_full == 0)
    def _():
        park(0, 0, d=0)
        park(1, 1, d=1)
        consume(0, 0)
        consume(1, 1, d=1)

    @pl.when(n_full > 0)
    def _():
        park(0, 0)
        n_loop = n_full - 2
        n_group = lax.shift_right_logical(n_loop, FLASH_UNROLL.bit_length() - 1)
        lax.fori_loop(0, n_group, body, 0)
        c0 = FLASH_UNROLL * n_group
        for u in range(FLASH_UNROLL // 2 - 1):
            @pl.when(n_loop - c0 >= 2 * (u + 1))
            def _(u=u):
                pair(c0 + 2 * u)
        c = n_loop
        park(c + 1, 1)
        consume(c, 0)
        park(c + 2, 0, d=0)
        consume(c + 1, 1)
        park(c + 3, 1, d=1)
        consume(c + 2, 0)
        consume(c + 3, 1, d=1)


def _causal(d):
    shape = (CK, TQ - d * CK)
    return lax.broadcasted_iota(I32, shape, 0) <= lax.broadcasted_iota(I32, shape, 1)


def _attn_scratch(n_state):
    return [pltpu.VMEM((2, n_state, CK, TQ), F32), pltpu.VMEM((2, n_state, 1, TQ), F32),
            pltpu.VMEM((n_state, 1, TQ), F32), pltpu.VMEM((n_state, VROWS, TQ), F32),
            pltpu.VMEM((BRANCH_W, TQ), F32)]


def _kv_specs(t, w):
    kspec = pl.BlockSpec((None, t, w), lambda bb, i: (bb, 0, 0))
    vspec = pl.BlockSpec((None, t // CK, N_HEADS * VROWS, CK), lambda bb, i: (bb, 0, 0, 0))
    return kspec, vspec


def _ln_kernel(x_ref, g_ref, b_ref, h_ref, hb_ref):
    x = x_ref[...]
    mu = jnp.mean(x, axis=1, keepdims=True)
    xc = x - mu
    var = jnp.mean(xc * xc, axis=1, keepdims=True)
    y = xc * lax.rsqrt(var + LN_EPS) * g_ref[...] + b_ref[...]
    h_ref[...] = y
    hb_ref[...] = y.astype(BF16)


def _layer_norm0(x2, g, b):
    n, d = x2.shape
    tm = 1024
    row = pl.BlockSpec((tm, d), lambda i: (i, 0))
    vec = pl.BlockSpec((1, d), lambda i: (0, 0))
    return pl.pallas_call(
        _ln_kernel,
        out_shape=(jax.ShapeDtypeStruct((n, d), F32), jax.ShapeDtypeStruct((n, d), BF16)),
        grid=(n // tm,),
        in_specs=[row, vec, vec],
        out_specs=(row, row),
        compiler_params=_cparams(1),
        name="ln0",
    )(x2, g.reshape(1, d), b.reshape(1, d))


def _proj_plain_kernel(x_ref, w_ref, wt_ref, *out_refs, n_t):
    for g, o_ref in enumerate(out_refs[:n_t]):
        _store_vt(o_ref, _tn_dot(wt_ref[:, g * BRANCH_W:(g + 1) * BRANCH_W], x_ref[...]))
    off = 0
    for o_ref in out_refs[n_t:]:
        wd = o_ref.shape[-1]
        for j in range(0, wd, MXU_N):
            acc = jnp.dot(x_ref[...], w_ref[:, off + j:off + j + MXU_N], preferred_element_type=F32)
            o_ref[:, j:j + MXU_N] = acc.astype(o_ref.dtype)
        off += wd


def _proj_plain(hb3, w, wt, widths, l):
    b, t, d = hb3.shape
    tm = 1024
    n_t = wt.shape[-1] // BRANCH_W
    shapes = [jax.ShapeDtypeStruct((b, t // CK, N_HEADS * VROWS, CK), BF16)] * n_t
    specs = [pl.BlockSpec((None, tm // CK, N_HEADS * VROWS, CK), lambda i, bb: (bb, i, 0, 0))] * n_t
    shapes += [jax.ShapeDtypeStruct((b, t, wd), BF16) for wd in widths]
    specs += [pl.BlockSpec((None, tm, wd), lambda i, bb: (bb, i, 0)) for wd in widths]
    return pl.pallas_call(
        functools.partial(_proj_plain_kernel, n_t=n_t),
        out_shape=tuple(shapes),
        grid=(t // tm, b),
        in_specs=[pl.BlockSpec((None, tm, d), lambda i, bb: (bb, i, 0)),
                  _layer_spec(w, l), _layer_spec(wt, l)],
        out_specs=tuple(specs),
        compiler_params=_cparams(2),
        name="proj_plain",
    )(hb3, w, wt)


def _proj_rope_kernel(x_ref, w_ref, c_ref, s_ref, *out_refs, heads, tables):
    lane = lax.broadcasted_iota(I32, (x_ref.shape[0], MXU_N), 1)
    for g, o_ref in enumerate(out_refs):
        hd, half = heads[g]
        sl = slice(g * MXU_N, (g + 1) * MXU_N)
        acc = jnp.dot(x_ref[...], w_ref[:, sl], preferred_element_type=F32)
        partner = jnp.where((lane & (hd - 1)) < half,
                            pltpu.roll(acc, MXU_N - half, 1), pltpu.roll(acc, half, 1))
        o_ref[...] = (acc * c_ref[tables[g]] + partner * s_ref[tables[g]]).astype(o_ref.dtype)


def _proj_rope(hb3, w, ctab, stab, heads, tables, l):
    b, t, d = hb3.shape
    tm = 1024
    assert w.shape[-1] == MXU_N * len(heads)
    tspec = pl.BlockSpec((ctab.shape[0], tm, MXU_N), lambda i, bb: (0, i, 0))
    ospec = pl.BlockSpec((None, tm, MXU_N), lambda i, bb: (bb, i, 0))
    return pl.pallas_call(
        functools.partial(_proj_rope_kernel, heads=heads, tables=tables),
        out_shape=(jax.ShapeDtypeStruct((b, t, MXU_N), BF16),) * len(heads),
        grid=(t // tm, b),
        in_specs=[pl.BlockSpec((None, tm, d), lambda i, bb: (bb, i, 0)),
                  _layer_spec(w, l), tspec, tspec],
        out_specs=(ospec,) * len(heads),
        compiler_params=_cparams(2),
        name="proj_rope",
    )(hb3, w, ctab, stab)


def _dsa_kernel(aq_ref, ak_ref, avt_ref, iq_ref, ik_ref, iw_ref, pick_ref, tri_ref, o_ref,
                keys_ref, hi_ref, lo_ref, bk_ref, top_ref, sel_ref, iqt_ref, aqt_ref, wt_ref, thr_ref, s_ref, mx_ref, m_ref, acc_ref, ot_ref,
                *, topk, idx_scale):
    i = pl.program_id(1)
    n_full = 2 * i

    iqt = iq_ref[...].astype(F32).T
    for hh in range(IDX_HEADS):
        iqt_ref[hh] = iqt[hh * IDX_DIM:(hh + 1) * IDX_DIM, :].astype(BF16)
    _masked_qt(aq_ref[...].astype(F32) * (HEAD_DIM ** -0.5 * LOG2E), 6, N_HEADS, aqt_ref)
    wt_ref[...] = _nt_dot(pick_ref[...], iw_ref[...]) * idx_scale

    def lanes_of(d):
        return slice(CK, TQ) if d == 1 else slice(None)

    def logits(c, d):
        kc = ik_ref[pl.ds(pl.multiple_of(c * CK, CK), CK), :]
        return [jnp.dot(kc[:, :IDX_DIM], iqt_ref[hh, :, lanes_of(d)], preferred_element_type=F32)
                for hh in range(IDX_HEADS)]

    def put_keys(c, key, lanes):
        keys_ref[c, :, lanes] = key
        hi_ref[c, :, lanes] = (key >> 16).astype(I16)
        lo_ref[c, :, lanes] = ((key & 0xFFFF) - HALF16).astype(I16)

    def score_chunk(c, lg, d):
        lanes = lanes_of(d)
        sc = jnp.zeros(lg[0].shape, F32)
        for hh in range(IDX_HEADS):
            sc = sc + jnp.maximum(lg[hh], 0.0) * wt_ref[hh:hh + 1, lanes]
        bits = pltpu.bitcast(sc, I32)
        key = jnp.where(bits < 0, INT_MIN - bits, bits)
        put_keys(c, key if d is None else jnp.where(_causal(d), key, INT_MIN), lanes)
        if d == 1:
            put_keys(c, jnp.full((CK, CK), INT_MIN, I32), slice(0, CK))

    def score_pair(c, d0, d1):
        lg0, lg1 = logits(c, d0), logits(c + 1, d1)
        score_chunk(c, lg0, d0)
        score_chunk(c + 1, lg1, d1)

    def score_body(p, carry):
        score_pair(2 * p, None, None)
        return carry

    lax.fori_loop(0, i, score_body, 0)
    score_pair(n_full, 0, 1)

    def pair_loop(body, init, last=None):
        def pair(p, carry):
            return body(2 * p + 1, body(2 * p, carry))
        carry = body(n_full, lax.fori_loop(0, i, pair, init))
        return (last or body)(n_full + 1, carry)

    def count16(pred, also=None):
        def hits(c, lanes):
            hit = jnp.where(pred(c, lanes), jnp.int16(1), jnp.int16(0))
            if also is not None:
                hit = jnp.where(also(c, lanes), hit, jnp.int16(0))
            return _fold_rows(hit, 2 * SUBLANES)

        def body(c, part):
            return part + hits(c, slice(None))

        def last(c, part):
            return jnp.concatenate([part[:, :CK], part[:, CK:] + hits(c, slice(CK, TQ))], axis=1)

        part = pair_loop(body, jnp.zeros((2 * SUBLANES, TQ), I16), last)
        return jnp.sum(part.astype(F32), axis=0, keepdims=True)

    def search16(count_ge, need):
        def bit_body(bi, t_u):
            c_u = t_u | jnp.left_shift(jnp.int32(1), 15 - bi)
            cnt = count_ge((c_u - HALF16).astype(I16))
            return jnp.where(cnt >= need, c_u, t_u)
        return lax.fori_loop(0, 16, bit_body, jnp.zeros((1, TQ), I32))

    hi_u = search16(lambda ck: count16(lambda c, lanes: hi_ref[c, :, lanes] >= ck[:, lanes]),
                    float(topk))
    thr_hi = (hi_u - HALF16).astype(I16)
    n_above = count16(lambda c, lanes: hi_ref[c, :, lanes] > thr_hi[:, lanes])
    need_lo = float(topk) - n_above

    fill = jnp.int16(-HALF16)
    top_ref[...] = jnp.full(top_ref.shape, fill, I16)

    def bucket(c):
        return jnp.where(hi_ref[c] == thr_hi, lo_ref[c], fill)

    def top_body(c, carry):
        x = bucket(c)
        for r in range(4):
            m = top_ref[r]
            swap = x > m
            top_ref[r] = jnp.where(swap, x, m)
            x = jnp.where(swap, m, x)
        return carry

    many = i > 1

    @pl.when(many)
    def _():
        pair_loop(top_body, 0)

    fourth = jnp.max(_fold_rows(jnp.where(top_ref[3] > fill, jnp.int16(1), jnp.int16(0)),
                                2 * SUBLANES).astype(F32)) > 0.5
    use_top = jnp.logical_and(many, jnp.logical_not(fourth))

    def count_top(pred):
        part = jnp.zeros((2 * SUBLANES, TQ), I16)
        for r in range(3):
            part = part + _fold_rows(jnp.where(pred(top_ref[r]), jnp.int16(1), jnp.int16(0)), 2 * SUBLANES)
        return jnp.sum(part.astype(F32), axis=0, keepdims=True)

    def lo_select(count_ge, count_gt):
        lo_u = search16(count_ge, need_lo)
        sel_ref[0:1, :] = lo_u.astype(F32)
        sel_ref[1:2, :] = count_gt((lo_u - HALF16).astype(I16))

    @pl.when(use_top)
    def _():
        lo_select(lambda ck: count_top(lambda x: x >= ck), lambda t: count_top(lambda x: x > t))

    @pl.when(jnp.logical_not(use_top))
    def _():
        def bucket_body(c, carry):
            bk_ref[c] = bucket(c)
            return carry

        pair_loop(bucket_body, 0)
        lo_select(lambda ck: count16(lambda c, lanes: bk_ref[c, :, lanes] >= ck[:, lanes]),
                  lambda t: count16(lambda c, lanes: bk_ref[c, :, lanes] > t[:, lanes]))

    lo_u = sel_ref[0:1, :].astype(I32)
    thr_lo = (lo_u - HALF16).astype(I16)
    thr = ((hi_u - HALF16) << 16) | lo_u

    n_gt = n_above + sel_ref[1:2, :]
    n_eq = count16(lambda c, lanes: lo_ref[c, :, lanes] == thr_lo[:, lanes],
                   also=lambda c, lanes: hi_ref[c, :, lanes] == thr_hi[:, lanes])
    need = float(topk) - n_gt
    amb = jnp.logical_and(n_eq > need, thr > INT_MIN)
    any_amb = jnp.max(jnp.where(amb, 1.0, 0.0)) > 0.5

    @pl.when(any_amb)
    def _():
        def drop_body(c, seen):
            k = keys_ref[c]
            eq = k == thr
            eqf = jnp.where(eq, 1.0, 0.0)
            rank = jnp.dot(tri_ref[...], eqf.astype(BF16), preferred_element_type=F32) + seen
            drop = jnp.logical_and(jnp.logical_and(eq, rank > need), amb)
            keys_ref[c] = jnp.where(drop, INT_MIN, k)
            return seen + jnp.sum(eqf, axis=0, keepdims=True)

        pair_loop(drop_body, jnp.zeros((1, TQ), F32))

    thr_ref[...] = jnp.maximum(thr, INT_MIN + 1)

    def qk_all(c, lanes):
        kc = ak_ref[pl.ds(pl.multiple_of(c * CK, CK), CK), :]
        return [jnp.dot(_half(kc, h, 6), aqt_ref[h, :, lanes], preferred_element_type=F32) for h in range(N_HEADS)]

    _flash_loop(n_full, qk_all,
                lambda bias, h, s, lanes: s + bias,
                lambda c, h: avt_ref[c, h * VROWS:(h + 1) * VROWS, :],
                (s_ref, mx_ref, m_ref, acc_ref),
                prep=lambda c, lanes: jnp.where(keys_ref[c, :, lanes] >= thr_ref[:, lanes], 0.0, NEG),
                causal_tail=False)
    for h in range(N_HEADS):
        ot_ref[h * HEAD_DIM:(h + 1) * HEAD_DIM, :] = _softmax_out(acc_ref.at[h])
    o_ref[...] = ot_ref[...].T.astype(o_ref.dtype)


def _dsa(aq, ak, avt, iq, ik, iw):
    b, t, _ = aq.shape
    topk = min(TOPK_MAX, t // 4)
    qspec = pl.BlockSpec((None, TQ, BRANCH_W), lambda bb, i: (bb, i, 0))
    kspec, vspec = _kv_specs(t, BRANCH_W)
    pick = np.zeros((2 * SUBLANES, MXU_N), np.float32)
    for hh in range(IDX_HEADS):
        pick[hh, IDX_DIM + hh] = 1.0
    pick = jnp.asarray(pick, BF16)
    tri = jnp.asarray(np.tril(np.ones((CK, CK), np.float32)), BF16)
    kern = functools.partial(_dsa_kernel, topk=topk, idx_scale=(IDX_HEADS * IDX_DIM) ** -0.5)
    return pl.pallas_call(
        kern,
        out_shape=jax.ShapeDtypeStruct((b, t, BRANCH_W), BF16),
        grid=(b, t // TQ),
        in_specs=[qspec, kspec, vspec, qspec, kspec, qspec,
                  pl.BlockSpec(pick.shape, lambda bb, i: (0, 0)), pl.BlockSpec(tri.shape, lambda bb, i: (0, 0))],
        out_specs=qspec,
        scratch_shapes=[
            pltpu.VMEM((t // CK, CK, TQ), I32),
            pltpu.VMEM((t // CK, CK, TQ), I16),
            pltpu.VMEM((t // CK, CK, TQ), I16),
            pltpu.VMEM((t // CK, CK, TQ), I16),
            pltpu.VMEM((4, CK, TQ), I16),
            pltpu.VMEM((SUBLANES, TQ), F32),
            pltpu.VMEM((IDX_HEADS, IDX_DIM, TQ), BF16),
            pltpu.VMEM((N_HEADS, LANES, TQ), BF16),
            pltpu.VMEM((2 * SUBLANES, TQ), F32),
            pltpu.VMEM((1, TQ), I32),
        ] + _attn_scratch(N_HEADS),
        compiler_params=_cparams(2),
        name="dsa",
    )(aq, ak, avt, iq, ik, iw, pick, tri)


def _kbar_kernel(k_ref, o_ref):
    o_ref[...] = jnp.zeros(o_ref.shape, o_ref.dtype)
    nb = k_ref.shape[0] // MOBA_BLOCK
    for n in range(nb):
        blk = k_ref[n * MOBA_BLOCK:(n + 1) * MOBA_BLOCK, :].astype(F32)
        o_ref[n:n + 1, :] = jnp.mean(blk, axis=0, keepdims=True).astype(o_ref.dtype)


def _kbar(bk):
    b, t, w = bk.shape
    nbp = max(2 * SUBLANES, t // MOBA_BLOCK)
    return pl.pallas_call(
        _kbar_kernel,
        out_shape=jax.ShapeDtypeStruct((b, nbp, w), BF16),
        grid=(b,),
        in_specs=[pl.BlockSpec((None, t, w), lambda bb: (bb, 0, 0))],
        out_specs=pl.BlockSpec((None, nbp, w), lambda bb: (bb, 0, 0)),
        compiler_params=_cparams(1),
        name="moba_kbar",
    )(bk)


def _moba_kernel(q_ref, k_ref, vt_ref, kbar_ref, o_ref, qt_ref, bias_ref, s_ref, mx_ref, m_ref, acc_ref, ot_ref):
    i = pl.program_id(1)
    nbp = kbar_ref.shape[0]
    blk = lax.broadcasted_iota(I32, (nbp, TQ), 0)
    blk_f = blk.astype(F32)
    own = 2 * i + (lax.broadcasted_iota(I32, (nbp, TQ), 1) >> (MOBA_BLOCK.bit_length() - 1))
    _masked_qt(q_ref[...].astype(F32) * (HEAD_DIM ** -0.5 * LOG2E), 6, N_HEADS, qt_ref)

    for h in range(N_HEADS):
        g = jnp.where(blk < own, jnp.dot(_half(kbar_ref[...], h, 6), qt_ref[h], preferred_element_type=F32), NEG)
        bias = jnp.full((nbp, TQ), NEG, F32)
        for _ in range(MOBA_TOPK):
            mx = jnp.max(g, axis=0, keepdims=True)
            first = jnp.min(jnp.where(g == mx, blk_f, 1e9), axis=0, keepdims=True)
            pick = jnp.logical_and(blk_f == first, mx > 0.5 * NEG)
            bias = jnp.where(pick, 0.0, bias)
            g = jnp.where(pick, NEG, g)
        bias_ref[h] = jnp.where(blk == own, 0.0, bias)

    def qk_all(c, lanes):
        kc = k_ref[pl.ds(pl.multiple_of(c * CK, CK), CK), :]
        return [jnp.dot(_half(kc, h, 6), qt_ref[h, :, lanes], preferred_element_type=F32) for h in range(N_HEADS)]

    _flash_loop(2 * i, qk_all, lambda c, h, s, lanes: s + bias_ref[h, pl.ds(c, 1), lanes],
                lambda c, h: vt_ref[c, h * VROWS:(h + 1) * VROWS, :], (s_ref, mx_ref, m_ref, acc_ref))
    for h in range(N_HEADS):
        ot_ref[h * HEAD_DIM:(h + 1) * HEAD_DIM, :] = _softmax_out(acc_ref.at[h])
    o_ref[...] = ot_ref[...].T.astype(o_ref.dtype)


def _moba(bq, bk, bvt, kbar):
    b, t, w = bq.shape
    assert TQ == 2 * MOBA_BLOCK and CK == MOBA_BLOCK and t % TQ == 0
    nbp = kbar.shape[1]
    qspec = pl.BlockSpec((None, TQ, w), lambda bb, i: (bb, i, 0))
    kspec, vspec = _kv_specs(t, w)
    return pl.pallas_call(
        _moba_kernel,
        out_shape=jax.ShapeDtypeStruct((b, t, w), BF16),
        grid=(b, t // TQ),
        in_specs=[qspec, kspec, vspec, pl.BlockSpec((None, nbp, w), lambda bb, i: (bb, 0, 0))],
        out_specs=qspec,
        scratch_shapes=[pltpu.VMEM((N_HEADS, LANES, TQ), BF16), pltpu.VMEM((N_HEADS, nbp, TQ), F32)]
        + _attn_scratch(N_HEADS),
        compiler_params=_cparams(2),
        name="moba",
    )(bq, bk, bvt, kbar)


def _diff_kernel(q_ref, k_ref, vt_ref, lam_ref, norm_ref, misc_ref, o_ref,
                 qt_ref, s_ref, mx_ref, m_ref, acc_ref, ot_ref):
    i = pl.program_id(1)
    _masked_qt(q_ref[...].astype(F32) * (DIFF_DIM ** -0.5 * LOG2E), 5, 2 * N_HEADS, qt_ref)

    dl = lam_ref[...]
    lam_init = misc_ref[0:1, 0:1]
    lam = (jnp.exp(jnp.sum(dl[0:1, :] * dl[1:2, :], axis=1, keepdims=True))
           - jnp.exp(jnp.sum(dl[2:3, :] * dl[3:4, :], axis=1, keepdims=True)) + lam_init)

    def qk_all(c, lanes):
        kc = k_ref[pl.ds(pl.multiple_of(c * CK, CK), CK), :]
        return [jnp.dot(_half(kc, j, 5), qt_ref[j, :, lanes], preferred_element_type=F32) for j in range(2 * N_HEADS)]

    _flash_loop(2 * i, qk_all, None,
                lambda c, j: vt_ref[c, (j // 2) * VROWS:(j // 2 + 1) * VROWS, :],
                (s_ref, mx_ref, m_ref, acc_ref))

    post = norm_ref[...] * (1.0 - lam_init)
    for h in range(N_HEADS):
        o_h = _softmax_out(acc_ref.at[2 * h]) - lam * _softmax_out(acc_ref.at[2 * h + 1])
        ms = jnp.mean(o_h * o_h, axis=0, keepdims=True)
        ot_ref[h * HEAD_DIM:(h + 1) * HEAD_DIM, :] = o_h * lax.rsqrt(ms + RMS_EPS) * post
    o_ref[...] = ot_ref[...].T.astype(o_ref.dtype)


def _diff(cq, ck, cvt, lam, norm, misc):
    b, t, w = cq.shape
    qspec = pl.BlockSpec((None, TQ, w), lambda bb, i: (bb, i, 0))
    kspec, vspec = _kv_specs(t, w)
    full = lambda a: pl.BlockSpec(a.shape, lambda bb, i: (0,) * a.ndim)
    return pl.pallas_call(
        _diff_kernel,
        out_shape=jax.ShapeDtypeStruct((b, t, w), BF16),
        grid=(b, t // TQ),
        in_specs=[qspec, kspec, vspec, full(lam), full(norm), full(misc)],
        out_specs=qspec,
        scratch_shapes=[pltpu.VMEM((2 * N_HEADS, LANES, TQ), BF16)] + _attn_scratch(2 * N_HEADS),
        compiler_params=_cparams(2),
        name="diff",
    )(cq, ck, cvt, lam, norm, misc)


def _mla_prep_kernel(cq_ref, ckv_ref, kr_ref, qn_ref, kvn_ref, wq_ref, wqr_ref, wk_ref, wvt_ref,
                     p_ref, ct_ref, st_ref, q_out, k_out, vt_out):
    x = cq_ref[...].astype(F32)
    xn = (x * lax.rsqrt(jnp.mean(x * x, axis=1, keepdims=True) + RMS_EPS) * qn_ref[...]).astype(BF16)
    q = (jnp.dot(xn, wq_ref[...], preferred_element_type=F32) * ct_ref[...]
         + jnp.dot(xn, wqr_ref[...], preferred_element_type=F32) * st_ref[...])
    q_out[...] = q.astype(q_out.dtype)
    c = ckv_ref[:, :KV_LORA].astype(F32)
    cn = (c * lax.rsqrt(jnp.mean(c * c, axis=1, keepdims=True) + RMS_EPS) * kvn_ref[...]).astype(BF16)
    k = (jnp.dot(cn, wk_ref[...], preferred_element_type=F32)
         + jnp.dot(kr_ref[...], p_ref[...], preferred_element_type=F32))
    k_out[...] = k.astype(k_out.dtype)
    _store_vt(vt_out, _tn_dot(wvt_ref[...], cn))


def _mla_prep(dcq, ckv, kr, qn, kvn, wq, wqr, wk, wvt, pmat, ct, st, l):
    b, t, _ = dcq.shape
    tm = 2048
    hw = N_HEADS * LANES
    row = lambda w: pl.BlockSpec((None, tm, w), lambda i, bb: (bb, i, 0))
    full = lambda a: pl.BlockSpec(a.shape, lambda i, bb: (0,) * a.ndim)
    tab = pl.BlockSpec((tm, hw), lambda i, bb: (i, 0))
    return pl.pallas_call(
        _mla_prep_kernel,
        out_shape=(jax.ShapeDtypeStruct((b, t, hw), BF16), jax.ShapeDtypeStruct((b, t, hw), BF16),
                   jax.ShapeDtypeStruct((b, t // CK, N_HEADS * VROWS, CK), BF16)),
        grid=(t // tm, b),
        in_specs=[row(Q_LORA), row(MXU_N), row(MXU_N), full(qn), full(kvn), _layer_spec(wq, l), _layer_spec(wqr, l),
                  _layer_spec(wk, l), _layer_spec(wvt, l), full(pmat), tab, tab],
        out_specs=(row(hw), row(hw),
                   pl.BlockSpec((None, tm // CK, N_HEADS * VROWS, CK), lambda i, bb: (bb, i, 0, 0))),
        compiler_params=_cparams(2),
        name="mla_prep",
    )(dcq, ckv, kr, qn, kvn, wq, wqr, wk, wvt, pmat, ct, st)


def _mla_kernel(q_ref, k_ref, vt_ref, o_ref, qt_ref, s_ref, mx_ref, m_ref, acc_ref, ot_ref):
    i = pl.program_id(1)
    hs = [slice(h * LANES, (h + 1) * LANES) for h in range(N_HEADS)]
    for h in range(N_HEADS):
        qt_ref[h] = q_ref[:, hs[h]].astype(F32).T.astype(BF16)

    def qk_all(c, lanes):
        start = pl.multiple_of(c * CK, CK)
        return [jnp.dot(k_ref[pl.ds(start, CK), hs[h]], qt_ref[h, :, lanes], preferred_element_type=F32)
                for h in range(N_HEADS)]

    _flash_loop(2 * i, qk_all, None,
                lambda c, h: vt_ref[c, h * VROWS:(h + 1) * VROWS, :],
                (s_ref, mx_ref, m_ref, acc_ref))
    for h in range(N_HEADS):
        ot_ref[h * HEAD_DIM:(h + 1) * HEAD_DIM, :] = _softmax_out(acc_ref.at[h])
    o_ref[...] = ot_ref[...].T.astype(o_ref.dtype)


def _mla(qm, km, vmt):
    b, t, hw = qm.shape
    kspec, vspec = _kv_specs(t, hw)
    return pl.pallas_call(
        _mla_kernel,
        out_shape=jax.ShapeDtypeStruct((b, t, BRANCH_W), BF16),
        grid=(b, t // TQ),
        in_specs=[pl.BlockSpec((None, TQ, hw), lambda bb, i: (bb, i, 0)), kspec, vspec],
        out_specs=pl.BlockSpec((None, TQ, BRANCH_W), lambda bb, i: (bb, i, 0)),
        scratch_shapes=[pltpu.VMEM((N_HEADS, LANES, TQ), BF16)] + _attn_scratch(N_HEADS),
        compiler_params=_cparams(2),
        name="mla",
    )(qm, km, vmt)


def _mem_kv_kernel(x_ref, w_ref, k_ref, vt_ref):
    x = x_ref[...].astype(BF16)
    k_ref[...] = jnp.dot(x, w_ref[:, :BRANCH_W], preferred_element_type=F32).astype(k_ref.dtype)
    _store_vt(vt_ref, _tn_dot(w_ref[:, BRANCH_W:], x))


def _mem_kv(mem, w, l):
    b, m, d = mem.shape
    assert m % CK == 0
    return pl.pallas_call(
        _mem_kv_kernel,
        out_shape=(jax.ShapeDtypeStruct((b, m, BRANCH_W), BF16),
                   jax.ShapeDtypeStruct((b, m // CK, N_HEADS * VROWS, CK), BF16)),
        grid=(b,),
        in_specs=[pl.BlockSpec((None, m, d), lambda bb: (bb, 0, 0)), _layer_spec(w, l)],
        out_specs=(pl.BlockSpec((None, m, BRANCH_W), lambda bb: (bb, 0, 0)),
                   pl.BlockSpec((None, m // CK, N_HEADS * VROWS, CK), lambda bb: (bb, 0, 0, 0))),
        compiler_params=_cparams(1),
        name="mem_kv",
    )(mem, w)


def _mem_kernel(q_ref, k_ref, vt_ref, o_ref, qt_ref, ot_ref):
    _masked_qt(q_ref[...].astype(F32) * (HEAD_DIM ** -0.5 * LOG2E), 6, N_HEADS, qt_ref)
    s_all = [jnp.dot(_half(k_ref[...], h, 6), qt_ref[h], preferred_element_type=F32) for h in range(N_HEADS)]
    for h in range(N_HEADS):
        s_t = s_all[h]
        p = jnp.exp2(s_t - jnp.max(s_t, axis=0, keepdims=True)).astype(BF16)
        acc = jnp.dot(vt_ref[0, h * VROWS:(h + 1) * VROWS, :], p, preferred_element_type=F32)
        ot_ref[h * HEAD_DIM:(h + 1) * HEAD_DIM, :] = acc[:HEAD_DIM, :] / acc[HEAD_DIM:HEAD_DIM + 1, :]
    o_ref[...] = ot_ref[...].T.astype(o_ref.dtype)


def _mem_attn(eq, mk, mvt):
    b, t, w = eq.shape
    m = mk.shape[1]
    assert m == CK
    tm = 2048
    return pl.pallas_call(
        _mem_kernel,
        out_shape=jax.ShapeDtypeStruct((b, t, w), BF16),
        grid=(b, t // tm),
        in_specs=[pl.BlockSpec((None, tm, w), lambda bb, i: (bb, i, 0)),
                  pl.BlockSpec((None, m, w), lambda bb, i: (bb, 0, 0)),
                  pl.BlockSpec((None,) + mvt.shape[1:], lambda bb, i: (bb, 0, 0, 0))],
        out_specs=pl.BlockSpec((None, tm, w), lambda bb, i: (bb, i, 0)),
        scratch_shapes=[pltpu.VMEM((N_HEADS, LANES, tm), BF16), pltpu.VMEM((BRANCH_W, tm), F32)],
        compiler_params=_cparams(2),
        name="mem_attn",
    )(eq, mk, mvt)


def _final_kernel(h_ref, hb_ref, oa_ref, ob_ref, oc_ref, od_ref, oe_ref, z_ref,
                  wg_ref, wb_ref, wo_ref, g_ref, b_ref, h_out, hb_out, acc_ref, *, alpha):
    d = h_ref.shape[1]
    half = h_ref.shape[0] // 2
    for n, o_ref in enumerate((oa_ref, ob_ref, oc_ref, od_ref, oe_ref)):
        for r in range(2):
            rows = slice(r * half, (r + 1) * half)
            z = z_ref[rows, n * BRANCH_W:(n + 1) * BRANCH_W].astype(F32)
            y = o_ref[rows, :].astype(F32) * (z / (1.0 + jnp.exp(-z)))
            u = jnp.dot(y.astype(BF16), wb_ref[n], preferred_element_type=F32)
            g = jnp.dot(hb_ref[rows, :], wg_ref[:, n * d:(n + 1) * d], preferred_element_type=F32)
            t = u / (1.0 + jnp.exp(-g))
            acc_ref[rows, :] = t if n == 0 else acc_ref[rows, :] + t
    for r in range(2):
        rows = slice(r * half, (r + 1) * half)
        out = jnp.dot(acc_ref[rows, :].astype(BF16), wo_ref[...], preferred_element_type=F32)
        x = alpha * h_ref[rows, :] + out
        mu = jnp.mean(x, axis=1, keepdims=True)
        xc = x - mu
        var = jnp.mean(xc * xc, axis=1, keepdims=True)
        y = xc * lax.rsqrt(var + LN_EPS) * g_ref[...] + b_ref[...]
        h_out[rows, :] = y
        hb_out[rows, :] = y.astype(BF16)


def _final(h, hb, os5, z, wg, wb, wo, ln_g, ln_b, alpha, l):
    n, d = h.shape
    tm = 512
    row = lambda w: pl.BlockSpec((tm, w), lambda i: (i, 0))
    full = lambda a: pl.BlockSpec(a.shape, lambda i: (0,) * a.ndim)
    return pl.pallas_call(
        functools.partial(_final_kernel, alpha=alpha),
        out_shape=(jax.ShapeDtypeStruct((n, d), F32), jax.ShapeDtypeStruct((n, d), BF16)),
        grid=(n // tm,),
        in_specs=[row(d), row(d)] + [row(BRANCH_W)] * N_BRANCH + [row(N_BRANCH * BRANCH_W),
                  _layer_spec(wg, l), _layer_spec(wb, l), _layer_spec(wo, l), full(ln_g), full(ln_b)],
        out_specs=(row(d), row(d)),
        scratch_shapes=[pltpu.VMEM((tm, d), F32)],
        compiler_params=_cparams(1),
        name="merge_out_ln",
    )(h, hb, *os5, z, wg, wb, wo, ln_g, ln_b)


ROPE_GROUPS = (("a_q", N_HEADS, HEAD_DIM, ROT_64), ("a_k", N_HEADS, HEAD_DIM, ROT_64),
               ("i_q", IDX_HEADS, IDX_DIM, ROT_32), ("i_k", 1, MXU_N, ROT_32),
               ("b_q", N_HEADS, HEAD_DIM, ROT_64), ("b_k", N_HEADS, HEAD_DIM, ROT_64),
               ("c_q", 2 * N_HEADS, DIFF_DIM, ROT_32), ("c_k", 2 * N_HEADS, DIFF_DIM, ROT_32),
               ("d_kr", 1, MXU_N, MLA_ROPE))
PLAIN_COLS = ("d_cq", "d_ckv", "e_q") + tuple(("z", j) for j in range(N_BRANCH))
VALUE_COLS = ("a_v", "b_v", "c_v")
GATE_COLS = tuple(("g", j) for j in range(OFF["g"][1] // MXU_N))


def _window_start(col):
    name, j = col if isinstance(col, tuple) else (col, 0)
    return OFF[name][0] + j * MXU_N


def _weight_prep_kernel(offs_ref, wt_ref, o_ref):
    o_ref[...] = wt_ref[...].T.astype(o_ref.dtype)


def _weight_windows(wt, cols, name):
    depth, n, d = wt.shape
    starts = [_window_start(c) for c in cols]
    assert all(st % SUBLANES == 0 and st + MXU_N <= n for st in starts)
    grid_spec = pltpu.PrefetchScalarGridSpec(
        num_scalar_prefetch=1,
        grid=(depth, len(cols)),
        in_specs=[pl.BlockSpec((None, pl.Element(MXU_N), pl.Element(d)),
                               lambda l, j, offs: (l, pl.multiple_of(offs[j], SUBLANES), 0))],
        out_specs=pl.BlockSpec((None, d, MXU_N), lambda l, j, offs: (l, 0, j)),
    )
    return pl.pallas_call(
        _weight_prep_kernel,
        out_shape=jax.ShapeDtypeStruct((depth, d, MXU_N * len(cols)), BF16),
        grid_spec=grid_spec,
        compiler_params=_cparams(2),
        name=name,
    )(jnp.asarray(np.asarray(starts, np.int32)), wt)


def _weight_prep(w_in):
    wt = jnp.swapaxes(w_in, 1, 2)
    return (_weight_windows(wt, PLAIN_COLS, "wprep_plain"), _weight_windows(wt, VALUE_COLS, "wprep_value"),
            _weight_windows(wt, [name for name, *_ in ROPE_GROUPS], "wprep_rope"),
            _weight_windows(wt, GATE_COLS, "wprep_gate"))


def _rope_tables(seq, rot_dim):
    pos = jnp.arange(seq, dtype=F32)
    inv = ROPE_THETA ** (-jnp.arange(0, rot_dim, 2, dtype=F32) / rot_dim)
    ang = pos[:, None] * inv[None, :]
    return jnp.cos(ang), jnp.sin(ang)


def _rope_cs(t, nh, hd, r):
    cos, sin = _rope_tables(t, r)
    c = jnp.concatenate([cos, cos, jnp.ones((t, hd - r), F32)], axis=1)
    s = jnp.concatenate([-sin, sin, jnp.zeros((t, hd - r), F32)], axis=1)
    return jnp.tile(c, (1, nh)), jnp.tile(s, (1, nh))


def kernel(x, mem, ln0_g, ln0_b, w_in, mla_q_norm, w_uq, mla_kv_norm, w_ukv, diff_lam, diff_norm,
           w_mem_kv, w_branch, w_out, ln_g, ln_b):
    b, t, d = x.shape
    depth = w_in.shape[0]
    alpha = (2 * depth) ** 0.25
    assert t % 512 == 0 and d == 1024

    w_plain, w_vt, w_rope, wg = _weight_prep(w_in)
    plain_widths = (BRANCH_W,) * 3 + (N_BRANCH * BRANCH_W,)
    rope_heads = tuple((hd, r // 2) for _, _, hd, r in ROPE_GROUPS)
    patterns = sorted(set((nh, hd, r) for _, nh, hd, r in ROPE_GROUPS))
    rope_tables = tuple(patterns.index((nh, hd, r)) for _, nh, hd, r in ROPE_GROUPS)
    cs = [_rope_cs(t, nh, hd, r) for nh, hd, r in patterns]
    ctab = jnp.stack([c for c, _ in cs])
    stab = jnp.stack([s for _, s in cs])

    uq = w_uq.reshape(depth, Q_LORA, N_HEADS, MLA_NOPE + MLA_ROPE)
    qn_w, qr_w = uq[..., :MLA_NOPE], uq[..., MLA_NOPE:]
    pad32 = jnp.zeros((depth, Q_LORA, N_HEADS, LANES - MLA_NOPE - MLA_ROPE), w_uq.dtype)
    hw = N_HEADS * LANES
    wq = jnp.concatenate([qn_w, qr_w, pad32], axis=-1).reshape(depth, Q_LORA, hw).astype(BF16)
    half = MLA_ROPE // 2
    wq_rot = jnp.concatenate([jnp.zeros_like(qn_w), -qr_w[..., half:], qr_w[..., :half], pad32],
                             axis=-1).reshape(depth, Q_LORA, hw).astype(BF16)
    cos_m, sin_m = _rope_tables(t, MLA_ROPE)
    one = lambda n: jnp.ones((t, n), F32)
    zer = lambda n: jnp.zeros((t, n), F32)
    qs = (MLA_NOPE + MLA_ROPE) ** -0.5 * LOG2E
    ct_q = qs * jnp.tile(jnp.concatenate([one(MLA_NOPE), cos_m, cos_m, one(LANES - MLA_NOPE - MLA_ROPE)], axis=1), (1, N_HEADS))
    st_q = qs * jnp.tile(jnp.concatenate([zer(MLA_NOPE), sin_m, sin_m, zer(LANES - MLA_NOPE - MLA_ROPE)], axis=1), (1, N_HEADS))
    ukv = w_ukv.reshape(depth, KV_LORA, N_HEADS, MLA_NOPE + MLA_V)
    wk = jnp.concatenate([ukv[..., :MLA_NOPE], jnp.zeros((depth, KV_LORA, N_HEADS, LANES - MLA_NOPE), w_ukv.dtype)],
                         axis=-1).reshape(depth, KV_LORA, hw).astype(BF16)
    wvt = ukv[..., MLA_NOPE:].reshape(depth, KV_LORA, N_HEADS * MLA_V).astype(BF16)
    place = np.zeros((MXU_N, hw), np.float32)
    for hh in range(N_HEADS):
        for j in range(MLA_ROPE):
            place[j, hh * LANES + MLA_NOPE + j] = 1.0
    place = jnp.asarray(place, BF16)

    wb = w_branch.astype(BF16)
    wo = w_out.astype(BF16)
    wmem = w_mem_kv.astype(BF16)
    norm_t = jnp.broadcast_to(diff_norm.astype(F32)[:, :, None], (depth, HEAD_DIM, TQ))

    h, hb = _layer_norm0(x.reshape(b * t, d), ln0_g, ln0_b)
    for l in range(depth):
        hb3 = hb.reshape(b, t, d)
        avt, bvt, cvt, dcq, ckv_iw, eq, z = _proj_plain(hb3, w_plain, w_vt, plain_widths, l)
        aq, ak, iq, ik, bq, bk, cq, ck, kr = _proj_rope(hb3, w_rope, ctab, stab, rope_heads, rope_tables, l)

        o_a = _dsa(aq, ak, avt, iq, ik, ik)
        o_b = _moba(bq, bk, bvt, _kbar(bk))
        lam_init = 0.8 - 0.6 * math.exp(-0.3 * l)
        misc = jnp.full((SUBLANES, LANES), lam_init, F32)
        o_c = _diff(cq, ck, cvt, diff_lam[l].astype(F32), norm_t[l], misc)
        qm, km, vmt = _mla_prep(dcq, ckv_iw, kr, mla_q_norm[l].reshape(1, Q_LORA), mla_kv_norm[l].reshape(1, KV_LORA),
                                wq, wq_rot, wk, wvt, place, ct_q, st_q, l)
        o_d = _mla(qm, km, vmt)
        o_e = _mem_attn(eq, *_mem_kv(mem, wmem, l))

        os5 = [o.reshape(b * t, BRANCH_W) for o in (o_a, o_b, o_c, o_d, o_e)]
        h, hb = _final(h, hb, os5, z.reshape(b * t, N_BRANCH * BRANCH_W), wg, wb, wo,
                       ln_g[l].reshape(1, d), ln_b[l].reshape(1, d), alpha, l)
    return h.reshape(b, t, d)
```

```python
import functools
import math

import numpy as np
import jax
import jax.numpy as jnp
from jax import lax
from jax.experimental import pallas as pl
from jax.experimental.pallas import tpu as pltpu

F32 = jnp.float32
BF16 = jnp.bfloat16
I32 = jnp.int32
I16 = jnp.int16

N_HEADS = 4
HEAD_DIM = 64
BRANCH_W = N_HEADS * HEAD_DIM
N_BRANCH = 5
ROPE_THETA = 500000.0
ROT_64 = 16
ROT_32 = 8
IDX_HEADS = 8
IDX_DIM = 32
TOPK_MAX = 256
MOBA_BLOCK = 256
MOBA_TOPK = 3
DIFF_DIM = 32
Q_LORA = 256
KV_LORA = 128
MLA_NOPE = 64
MLA_ROPE = 32
MLA_V = 64
LN_EPS = 1e-5
RMS_EPS = 1e-6

IN_LAYOUT = (
    ("a_q", BRANCH_W), ("a_k", BRANCH_W), ("a_v", BRANCH_W),
    ("i_q", IDX_HEADS * IDX_DIM), ("i_k", IDX_DIM), ("i_w", IDX_HEADS),
    ("b_q", BRANCH_W), ("b_k", BRANCH_W), ("b_v", BRANCH_W),
    ("c_q", BRANCH_W), ("c_k", BRANCH_W), ("c_v", BRANCH_W),
    ("d_cq", Q_LORA), ("d_ckv", KV_LORA), ("d_kr", MLA_ROPE),
    ("e_q", BRANCH_W),
    ("z", N_BRANCH * BRANCH_W),
    ("g", N_BRANCH * 1024),
)

SUBLANES = 8
LANES = 128
MXU_N = 256
TQ = 512
CK = 256
VROWS = HEAD_DIM + 16
FLASH_UNROLL = 4
NEG = -1e30
LOG2E = math.log2(math.e)
INT_MIN = np.int32(-2 ** 31)
HALF16 = 1 << 15
VMEM_LIMIT = 56 * 1024 * 1024


def _offsets():
    off, out = 0, {}
    for name, size in IN_LAYOUT:
        out[name] = (off, size)
        off += size
    return out


OFF = _offsets()


def _nt_dot(a, b):
    return lax.dot_general(a, b, (((1,), (1,)), ((), ())), preferred_element_type=F32)


def _tn_dot(w, x):
    return lax.dot_general(w, x, (((0,), (1,)), ((), ())), preferred_element_type=F32)


def _fold_rows(w, rows=SUBLANES):
    xs = [w[r:r + rows, :] for r in range(0, w.shape[0], rows)]
    while len(xs) > 1:
        xs = [xs[j] + xs[j + 1] for j in range(0, len(xs) - 1, 2)] + ([xs[-1]] if len(xs) % 2 else [])
    return xs[0]


def _masked_qt(q, shift, n, qt_ref):
    qt = q.T
    dim = lax.broadcasted_iota(I32, (LANES, qt.shape[1]), 0)
    for j in range(n):
        half = (j << shift) // LANES
        rows = qt[half * LANES:(half + 1) * LANES, :]
        qt_ref[j] = jnp.where(((dim + half * LANES) >> shift) == j, rows, 0.0).astype(BF16)


def _half(kc, j, shift):
    half = (j << shift) // LANES
    return kc[:, half * LANES:(half + 1) * LANES]


def _cparams(n_axes):
    return pltpu.CompilerParams(dimension_semantics=("arbitrary",) * n_axes,
                                vmem_limit_bytes=VMEM_LIMIT)


def _layer_spec(a, l):
    return pl.BlockSpec((None,) + a.shape[1:], lambda *_: (l,) + (0,) * (a.ndim - 1))


def _softmax_step(s_t, m_tile, vt_h, m_ref, acc_ref):
    m_old = m_ref[...]
    m_new = jnp.maximum(m_old, m_tile)
    alpha = jnp.exp2(m_old - m_new)
    p = jnp.exp2(s_t - m_new)
    acc_ref[...] = alpha * acc_ref[...] + jnp.dot(vt_h, p.astype(BF16), preferred_element_type=F32)
    m_ref[...] = m_new


def _softmax_init(m_ref, acc_ref):
    m_ref[...] = jnp.full(m_ref.shape, NEG, F32)
    acc_ref[...] = jnp.zeros(acc_ref.shape, F32)


def _softmax_out(acc_ref):
    return acc_ref[:HEAD_DIM, :] / acc_ref[HEAD_DIM:HEAD_DIM + 1, :]


def _store_vt(o_ref, vt):
    ones = jnp.ones((VROWS - HEAD_DIM, CK), o_ref.dtype)
    for j in range(o_ref.shape[0]):
        for h in range(N_HEADS):
            o_ref[j, h * VROWS:h * VROWS + HEAD_DIM, :] = (
                vt[h * HEAD_DIM:(h + 1) * HEAD_DIM, j * CK:(j + 1) * CK].astype(o_ref.dtype))
            o_ref[j, h * VROWS + HEAD_DIM:(h + 1) * VROWS, :] = ones


def _flash_loop(n_full, qk_all, mask, vt_rows, state, prep=None, causal_tail=True):
    s_ref, mx_ref, m_ref, acc_ref = state
    n_state = m_ref.shape[0]
    for j in range(n_state):
        _softmax_init(m_ref.at[j], acc_ref.at[j])

    def lanes_of(d):
        return slice(CK, TQ) if d == 1 else slice(None)

    def park(c, slot, d=None):
        lanes = lanes_of(d)
        ctx = c if prep is None else prep(c, lanes)
        for j, s in enumerate(qk_all(c, lanes)):
            if mask is not None:
                s = mask(ctx, j, s, lanes)
            if d is not None and causal_tail:
                s = jnp.where(_causal(d), s, NEG)
            s_ref[slot, j, :, lanes] = s
            mx_ref[slot, j, :, lanes] = jnp.max(s, axis=0, keepdims=True)

    def consume(c, slot, d=None):
        lanes = lanes_of(d)
        for j in range(n_state):
            _softmax_step(s_ref[slot, j, :, lanes], mx_ref[slot, j, :, lanes], vt_rows(c, j),
                          m_ref.at[j, :, lanes], acc_ref.at[j, :, lanes])

    def pair(c):
        park(c + 1, 1)
        consume(c, 0)
        park(c + 2, 0)
        consume(c + 1, 1)

    def body(g, carry):
        for u in range(0, FLASH_UNROLL, 2):
            pair(FLASH_UNROLL * g + u)
        return carry

    @pl.when(n_full == 0)
    def _():
        park(0, 0, d=0)
        park(1, 1, d=1)
        consume(0, 0)
        consume(1, 1, d=1)

    @pl.when(n_full > 0)
    def _():
        park(0, 0)
        n_loop = n_full - 2
        n_group = lax.shift_right_logical(n_loop, FLASH_UNROLL.bit_length() - 1)
        lax.fori_loop(0, n_group, body, 0)
        c0 = FLASH_UNROLL * n_group
        for u in range(FLASH_UNROLL // 2 - 1):
            @pl.when(n_loop - c0 >= 2 * (u + 1))
            def _(u=u):
                pair(c0 + 2 * u)
        c = n_loop
        park(c + 1, 1)
        consume(c, 0)
        park(c + 2, 0, d=0)
        consume(c + 1, 1)
        park(c + 3, 1, d=1)
        consume(c + 2, 0)
        consume(c + 3, 1, d=1)


def _causal(d):
    shape = (CK, TQ - d * CK)
    return lax.broadcasted_iota(I32, shape, 0) <= lax.broadcasted_iota(I32, shape, 1)


def _attn_scratch(n_state):
    return [pltpu.VMEM((2, n_state, CK, TQ), F32), pltpu.VMEM((2, n_state, 1, TQ), F32),
            pltpu.VMEM((n_state, 1, TQ), F32), pltpu.VMEM((n_state, VROWS, TQ), F32),
            pltpu.VMEM((BRANCH_W, TQ), F32)]


def _kv_specs(t, w):
    kspec = pl.BlockSpec((None, t, w), lambda bb, i: (bb, 0, 0))
    vspec = pl.BlockSpec((None, t // CK, N_HEADS * VROWS, CK), lambda bb, i: (bb, 0, 0, 0))
    return kspec, vspec


def _ln_kernel(x_ref, g_ref, b_ref, h_ref, hb_ref):
    x = x_ref[...]
    mu = jnp.mean(x, axis=1, keepdims=True)
    xc = x - mu
    var = jnp.mean(xc * xc, axis=1, keepdims=True)
    y = xc * lax.rsqrt(var + LN_EPS) * g_ref[...] + b_ref[...]
    h_ref[...] = y
    hb_ref[...] = y.astype(BF16)


def _layer_norm0(x2, g, b):
    n, d = x2.shape
    tm = 2048
    row = pl.BlockSpec((tm, d), lambda i: (i, 0))
    vec = pl.BlockSpec((1, d), lambda i: (0, 0))
    return pl.pallas_call(
        _ln_kernel,
        out_shape=(jax.ShapeDtypeStruct((n, d), F32), jax.ShapeDtypeStruct((n, d), BF16)),
        grid=(n // tm,),
        in_specs=[row, vec, vec],
        out_specs=(row, row),
        compiler_params=_cparams(1),
        name="ln0",
    )(x2, g.reshape(1, d), b.reshape(1, d))


def _proj_plain_kernel(x_ref, w_ref, wt_ref, *out_refs, n_t):
    for g, o_ref in enumerate(out_refs[:n_t]):
        _store_vt(o_ref, _tn_dot(wt_ref[:, g * BRANCH_W:(g + 1) * BRANCH_W], x_ref[...]))
    off = 0
    for o_ref in out_refs[n_t:]:
        wd = o_ref.shape[-1]
        for j in range(0, wd, MXU_N):
            acc = jnp.dot(x_ref[...], w_ref[:, off + j:off + j + MXU_N], preferred_element_type=F32)
            o_ref[:, j:j + MXU_N] = acc.astype(o_ref.dtype)
        off += wd


def _proj_plain(hb3, w, wt, widths, l):
    b, t, d = hb3.shape
    tm = 2048
    n_t = wt.shape[-1] // BRANCH_W
    shapes = [jax.ShapeDtypeStruct((b, t // CK, N_HEADS * VROWS, CK), BF16)] * n_t
    specs = [pl.BlockSpec((None, tm // CK, N_HEADS * VROWS, CK), lambda i, bb: (bb, i, 0, 0))] * n_t
    shapes += [jax.ShapeDtypeStruct((b, t, wd), BF16) for wd in widths]
    specs += [pl.BlockSpec((None, tm, wd), lambda i, bb: (bb, i, 0)) for wd in widths]
    return pl.pallas_call(
        functools.partial(_proj_plain_kernel, n_t=n_t),
        out_shape=tuple(shapes),
        grid=(t // tm, b),
        in_specs=[pl.BlockSpec((None, tm, d), lambda i, bb: (bb, i, 0)),
                  _layer_spec(w, l), _layer_spec(wt, l)],
        out_specs=tuple(specs),
        compiler_params=_cparams(2),
        name="proj_plain",
    )(hb3, w, wt)


def _proj_rope_kernel(x_ref, w_ref, c_ref, s_ref, *out_refs, heads, tables):
    lane = lax.broadcasted_iota(I32, (x_ref.shape[0], MXU_N), 1)
    for g, o_ref in enumerate(out_refs):
        hd, half = heads[g]
        sl = slice(g * MXU_N, (g + 1) * MXU_N)
        acc = jnp.dot(x_ref[...], w_ref[:, sl], preferred_element_type=F32)
        partner = jnp.where((lane & (hd - 1)) < half,
                            pltpu.roll(acc, MXU_N - half, 1), pltpu.roll(acc, half, 1))
        o_ref[...] = (acc * c_ref[tables[g]] + partner * s_ref[tables[g]]).astype(o_ref.dtype)


def _proj_rope(hb3, w, ctab, stab, heads, tables, l):
    b, t, d = hb3.shape
    tm = 1024
    assert w.shape[-1] == MXU_N * len(heads)
    tspec = pl.BlockSpec((ctab.shape[0], tm, MXU_N), lambda i, bb: (0, i, 0))
    ospec = pl.BlockSpec((None, tm, MXU_N), lambda i, bb: (bb, i, 0))
    return pl.pallas_call(
        functools.partial(_proj_rope_kernel, heads=heads, tables=tables),
        out_shape=(jax.ShapeDtypeStruct((b, t, MXU_N), BF16),) * len(heads),
        grid=(t // tm, b),
        in_specs=[pl.BlockSpec((None, tm, d), lambda i, bb: (bb, i, 0)),
                  _layer_spec(w, l), tspec, tspec],
        out_specs=(ospec,) * len(heads),
        compiler_params=_cparams(2),
        name="proj_rope",
    )(hb3, w, ctab, stab)


def _dsa_kernel(aq_ref, ak_ref, avt_ref, iq_ref, ik_ref, iw_ref, pick_ref, tri_ref, o_ref,
                keys_ref, hi_ref, lo_ref, bk_ref, top_ref, sel_ref, iqt_ref, aqt_ref, wt_ref, thr_ref, s_ref, mx_ref, m_ref, acc_ref, ot_ref,
                *, topk, idx_scale):
    i = pl.program_id(1)
    n_full = 2 * i

    iqt = iq_ref[...].astype(F32).T
    for hh in range(IDX_HEADS):
        iqt_ref[hh] = iqt[hh * IDX_DIM:(hh + 1) * IDX_DIM, :].astype(BF16)
    _masked_qt(aq_ref[...].astype(F32) * (HEAD_DIM ** -0.5 * LOG2E), 6, N_HEADS, aqt_ref)
    wt_ref[...] = _nt_dot(pick_ref[...], iw_ref[...]) * idx_scale

    def lanes_of(d):
        return slice(CK, TQ) if d == 1 else slice(None)

    def logits(c, d):
        kc = ik_ref[pl.ds(pl.multiple_of(c * CK, CK), CK), :]
        return [jnp.dot(kc[:, :IDX_DIM], iqt_ref[hh, :, lanes_of(d)], preferred_element_type=F32)
                for hh in range(IDX_HEADS)]

    def put_keys(c, key, lanes):
        keys_ref[c, :, lanes] = key
        hi_ref[c, :, lanes] = (key >> 16).astype(I16)
        lo_ref[c, :, lanes] = ((key & 0xFFFF) - HALF16).astype(I16)

    def score_chunk(c, lg, d):
        lanes = lanes_of(d)
        sc = jnp.zeros(lg[0].shape, F32)
        for hh in range(IDX_HEADS):
            sc = sc + jnp.maximum(lg[hh], 0.0) * wt_ref[hh:hh + 1, lanes]
        bits = pltpu.bitcast(sc, I32)
        key = jnp.where(bits < 0, INT_MIN - bits, bits)
        put_keys(c, key if d is None else jnp.where(_causal(d), key, INT_MIN), lanes)
        if d == 1:
            put_keys(c, jnp.full((CK, CK), INT_MIN, I32), slice(0, CK))

    def score_pair(c, d0, d1):
        lg0, lg1 = logits(c, d0), logits(c + 1, d1)
        score_chunk(c, lg0, d0)
        score_chunk(c + 1, lg1, d1)

    def score_body(p, carry):
        score_pair(2 * p, None, None)
        return carry

    lax.fori_loop(0, i, score_body, 0)
    score_pair(n_full, 0, 1)

    def pair_loop(body, init, last=None):
        def pair(p, carry):
            return body(2 * p + 1, body(2 * p, carry))
        carry = body(n_full, lax.fori_loop(0, i, pair, init))
        return (last or body)(n_full + 1, carry)

    def count16(pred, also=None):
        def hits(c, lanes):
            hit = jnp.where(pred(c, lanes), jnp.int16(1), jnp.int16(0))
            if also is not None:
                hit = jnp.where(also(c, lanes), hit, jnp.int16(0))
            return _fold_rows(hit, 2 * SUBLANES)

        def body(c, part):
            return part + hits(c, slice(None))

        def last(c, part):
            return jnp.concatenate([part[:, :CK], part[:, CK:] + hits(c, slice(CK, TQ))], axis=1)

        part = pair_loop(body, jnp.zeros((2 * SUBLANES, TQ), I16), last)
        return jnp.sum(part.astype(F32), axis=0, keepdims=True)

    def search16(count_ge, need):
        def bit_body(bi, t_u):
            c_u = t_u | jnp.left_shift(jnp.int32(1), 15 - bi)
            cnt = count_ge((c_u - HALF16).astype(I16))
            return jnp.where(cnt >= need, c_u, t_u)
        return lax.fori_loop(0, 16, bit_body, jnp.zeros((1, TQ), I32))

    hi_u = search16(lambda ck: count16(lambda c, lanes: hi_ref[c, :, lanes] >= ck[:, lanes]),
                    float(topk))
    thr_hi = (hi_u - HALF16).astype(I16)
    n_above = count16(lambda c, lanes: hi_ref[c, :, lanes] > thr_hi[:, lanes])
    need_lo = float(topk) - n_above

    fill = jnp.int16(-HALF16)
    top_ref[...] = jnp.full(top_ref.shape, fill, I16)

    def bucket(c):
        return jnp.where(hi_ref[c] == thr_hi, lo_ref[c], fill)

    def top_body(c, carry):
        x = bucket(c)
        for r in range(4):
            m = top_ref[r]
            swap = x > m
            top_ref[r] = jnp.where(swap, x, m)
            x = jnp.where(swap, m, x)
        return carry

    many = i > 1

    @pl.when(many)
    def _():
        pair_loop(top_body, 0)

    fourth = jnp.max(_fold_rows(jnp.where(top_ref[3] > fill, jnp.int16(1), jnp.int16(0)),
                                2 * SUBLANES).astype(F32)) > 0.5
    use_top = jnp.logical_and(many, jnp.logical_not(fourth))

    def count_top(pred):
        part = jnp.zeros((2 * SUBLANES, TQ), I16)
        for r in range(3):
            part = part + _fold_rows(jnp.where(pred(top_ref[r]), jnp.int16(1), jnp.int16(0)), 2 * SUBLANES)
        return jnp.sum(part.astype(F32), axis=0, keepdims=True)

    def lo_select(count_ge, count_gt):
        lo_u = search16(count_ge, need_lo)
        sel_ref[0:1, :] = lo_u.astype(F32)
        sel_ref[1:2, :] = count_gt((lo_u - HALF16).astype(I16))

    @pl.when(use_top)
    def _():
        lo_select(lambda ck: count_top(lambda x: x >= ck), lambda t: count_top(lambda x: x > t))

    @pl.when(jnp.logical_not(use_top))
    def _():
        def bucket_body(c, carry):
            bk_ref[c] = bucket(c)
            return carry

        pair_loop(bucket_body, 0)
        lo_select(lambda ck: count16(lambda c, lanes: bk_ref[c, :, lanes] >= ck[:, lanes]),
                  lambda t: count16(lambda c, lanes: bk_ref[c, :, lanes] > t[:, lanes]))

    lo_u = sel_ref[0:1, :].astype(I32)
    thr_lo = (lo_u - HALF16).astype(I16)
    thr = ((hi_u - HALF16) << 16) | lo_u

    n_gt = n_above + sel_ref[1:2, :]
    n_eq = count16(lambda c, lanes: lo_ref[c, :, lanes] == thr_lo[:, lanes],
                   also=lambda c, lanes: hi_ref[c, :, lanes] == thr_hi[:, lanes])
    need = float(topk) - n_gt
    amb = jnp.logical_and(n_eq > need, thr > INT_MIN)
    any_amb = jnp.max(jnp.where(amb, 1.0, 0.0)) > 0.5

    @pl.when(any_amb)
    def _():
        def drop_body(c, seen):
            k = keys_ref[c]
            eq = k == thr
            eqf = jnp.where(eq, 1.0, 0.0)
            rank = jnp.dot(tri_ref[...], eqf.astype(BF16), preferred_element_type=F32) + seen
            drop = jnp.logical_and(jnp.logical_and(eq, rank > need), amb)
            keys_ref[c] = jnp.where(drop, INT_MIN, k)
            return seen + jnp.sum(eqf, axis=0, keepdims=True)

        pair_loop(drop_body, jnp.zeros((1, TQ), F32))

    thr_ref[...] = jnp.maximum(thr, INT_MIN + 1)

    def qk_all(c, lanes):
        kc = ak_ref[pl.ds(pl.multiple_of(c * CK, CK), CK), :]
        return [jnp.dot(_half(kc, h, 6), aqt_ref[h, :, lanes], preferred_element_type=F32) for h in range(N_HEADS)]

    _flash_loop(n_full, qk_all,
                lambda bias, h, s, lanes: s + bias,
                lambda c, h: avt_ref[c, h * VROWS:(h + 1) * VROWS, :],
                (s_ref, mx_ref, m_ref, acc_ref),
                prep=lambda c, lanes: jnp.where(keys_ref[c, :, lanes] >= thr_ref[:, lanes], 0.0, NEG),
                causal_tail=False)
    for h in range(N_HEADS):
        ot_ref[h * HEAD_DIM:(h + 1) * HEAD_DIM, :] = _softmax_out(acc_ref.at[h])
    o_ref[...] = ot_ref[...].T.astype(o_ref.dtype)


def _dsa(aq, ak, avt, iq, ik, iw):
    b, t, _ = aq.shape
    topk = min(TOPK_MAX, t // 4)
    qspec = pl.BlockSpec((None, TQ, BRANCH_W), lambda bb, i: (bb, i, 0))
    kspec, vspec = _kv_specs(t, BRANCH_W)
    pick = np.zeros((2 * SUBLANES, MXU_N), np.float32)
    for hh in range(IDX_HEADS):
        pick[hh, IDX_DIM + hh] = 1.0
    pick = jnp.asarray(pick, BF16)
    tri = jnp.asarray(np.tril(np.ones((CK, CK), np.float32)), BF16)
    kern = functools.partial(_dsa_kernel, topk=topk, idx_scale=(IDX_HEADS * IDX_DIM) ** -0.5)
    return pl.pallas_call(
        kern,
        out_shape=jax.ShapeDtypeStruct((b, t, BRANCH_W), BF16),
        grid=(b, t // TQ),
        in_specs=[qspec, kspec, vspec, qspec, kspec, qspec,
                  pl.BlockSpec(pick.shape, lambda bb, i: (0, 0)), pl.BlockSpec(tri.shape, lambda bb, i: (0, 0))],
        out_specs=qspec,
        scratch_shapes=[
            pltpu.VMEM((t // CK, CK, TQ), I32),
            pltpu.VMEM((t // CK, CK, TQ), I16),
            pltpu.VMEM((t // CK, CK, TQ), I16),
            pltpu.VMEM((t // CK, CK, TQ), I16),
            pltpu.VMEM((4, CK, TQ), I16),
            pltpu.VMEM((SUBLANES, TQ), F32),
            pltpu.VMEM((IDX_HEADS, IDX_DIM, TQ), BF16),
            pltpu.VMEM((N_HEADS, LANES, TQ), BF16),
            pltpu.VMEM((2 * SUBLANES, TQ), F32),
            pltpu.VMEM((1, TQ), I32),
        ] + _attn_scratch(N_HEADS),
        compiler_params=_cparams(2),
        name="dsa",
    )(aq, ak, avt, iq, ik, iw, pick, tri)


def _kbar_kernel(k_ref, o_ref):
    o_ref[...] = jnp.zeros(o_ref.shape, o_ref.dtype)
    nb = k_ref.shape[0] // MOBA_BLOCK
    for n in range(nb):
        blk = k_ref[n * MOBA_BLOCK:(n + 1) * MOBA_BLOCK, :].astype(F32)
        o_ref[n:n + 1, :] = jnp.mean(blk, axis=0, keepdims=True).astype(o_ref.dtype)


def _kbar(bk):
    b, t, w = bk.shape
    nbp = max(2 * SUBLANES, t // MOBA_BLOCK)
    return pl.pallas_call(
        _kbar_kernel,
        out_shape=jax.ShapeDtypeStruct((b, nbp, w), BF16),
        grid=(b,),
        in_specs=[pl.BlockSpec((None, t, w), lambda bb: (bb, 0, 0))],
        out_specs=pl.BlockSpec((None, nbp, w), lambda bb: (bb, 0, 0)),
        compiler_params=_cparams(1),
        name="moba_kbar",
    )(bk)


def _moba_kernel(q_ref, k_ref, vt_ref, kbar_ref, o_ref, qt_ref, bias_ref, s_ref, mx_ref, m_ref, acc_ref, ot_ref):
    i = pl.program_id(1)
    nbp = kbar_ref.shape[0]
    blk = lax.broadcasted_iota(I32, (nbp, TQ), 0)
    blk_f = blk.astype(F32)
    own = 2 * i + (lax.broadcasted_iota(I32, (nbp, TQ), 1) >> (MOBA_BLOCK.bit_length() - 1))
    _masked_qt(q_ref[...].astype(F32) * (HEAD_DIM ** -0.5 * LOG2E), 6, N_HEADS, qt_ref)

    for h in range(N_HEADS):
        g = jnp.where(blk < own, jnp.dot(_half(kbar_ref[...], h, 6), qt_ref[h], preferred_element_type=F32), NEG)
        bias = jnp.full((nbp, TQ), NEG, F32)
        for _ in range(MOBA_TOPK):
            mx = jnp.max(g, axis=0, keepdims=True)
            first = jnp.min(jnp.where(g == mx, blk_f, 1e9), axis=0, keepdims=True)
            pick = jnp.logical_and(blk_f == first, mx > 0.5 * NEG)
            bias = jnp.where(pick, 0.0, bias)
            g = jnp.where(pick, NEG, g)
        bias_ref[h] = jnp.where(blk == own, 0.0, bias)

    def qk_all(c, lanes):
        kc = k_ref[pl.ds(pl.multiple_of(c * CK, CK), CK), :]
        return [jnp.dot(_half(kc, h, 6), qt_ref[h, :, lanes], preferred_element_type=F32) for h in range(N_HEADS)]

    _flash_loop(2 * i, qk_all, lambda c, h, s, lanes: s + bias_ref[h, pl.ds(c, 1), lanes],
                lambda c, h: vt_ref[c, h * VROWS:(h + 1) * VROWS, :], (s_ref, mx_ref, m_ref, acc_ref))
    for h in range(N_HEADS):
        ot_ref[h * HEAD_DIM:(h + 1) * HEAD_DIM, :] = _softmax_out(acc_ref.at[h])
    o_ref[...] = ot_ref[...].T.astype(o_ref.dtype)


def _moba(bq, bk, bvt, kbar):
    b, t, w = bq.shape
    assert TQ == 2 * MOBA_BLOCK and CK == MOBA_BLOCK and t % TQ == 0
    nbp = kbar.shape[1]
    qspec = pl.BlockSpec((None, TQ, w), lambda bb, i: (bb, i, 0))
    kspec, vspec = _kv_specs(t, w)
    return pl.pallas_call(
        _moba_kernel,
        out_shape=jax.ShapeDtypeStruct((b, t, w), BF16),
        grid=(b, t // TQ),
        in_specs=[qspec, kspec, vspec, pl.BlockSpec((None, nbp, w), lambda bb, i: (bb, 0, 0))],
        out_specs=qspec,
        scratch_shapes=[pltpu.VMEM((N_HEADS, LANES, TQ), BF16), pltpu.VMEM((N_HEADS, nbp, TQ), F32)]
        + _attn_scratch(N_HEADS),
        compiler_params=_cparams(2),
        name="moba",
    )(bq, bk, bvt, kbar)


def _diff_kernel(q_ref, k_ref, vt_ref, lam_ref, norm_ref, misc_ref, o_ref,
                 qt_ref, s_ref, mx_ref, m_ref, acc_ref, ot_ref):
    i = pl.program_id(1)
    _masked_qt(q_ref[...].astype(F32) * (DIFF_DIM ** -0.5 * LOG2E), 5, 2 * N_HEADS, qt_ref)

    dl = lam_ref[...]
    lam_init = misc_ref[0:1, 0:1]
    lam = (jnp.exp(jnp.sum(dl[0:1, :] * dl[1:2, :], axis=1, keepdims=True))
           - jnp.exp(jnp.sum(dl[2:3, :] * dl[3:4, :], axis=1, keepdims=True)) + lam_init)

    def qk_all(c, lanes):
        kc = k_ref[pl.ds(pl.multiple_of(c * CK, CK), CK), :]
        return [jnp.dot(_half(kc, j, 5), qt_ref[j, :, lanes], preferred_element_type=F32) for j in range(2 * N_HEADS)]

    _flash_loop(2 * i, qk_all, None,
                lambda c, j: vt_ref[c, (j // 2) * VROWS:(j // 2 + 1) * VROWS, :],
                (s_ref, mx_ref, m_ref, acc_ref))

    post = norm_ref[...] * (1.0 - lam_init)
    for h in range(N_HEADS):
        o_h = _softmax_out(acc_ref.at[2 * h]) - lam * _softmax_out(acc_ref.at[2 * h + 1])
        ms = jnp.mean(o_h * o_h, axis=0, keepdims=True)
        ot_ref[h * HEAD_DIM:(h + 1) * HEAD_DIM, :] = o_h * lax.rsqrt(ms + RMS_EPS) * post
    o_ref[...] = ot_ref[...].T.astype(o_ref.dtype)


def _diff(cq, ck, cvt, lam, norm, misc):
    b, t, w = cq.shape
    qspec = pl.BlockSpec((None, TQ, w), lambda bb, i: (bb, i, 0))
    kspec, vspec = _kv_specs(t, w)
    full = lambda a: pl.BlockSpec(a.shape, lambda bb, i: (0,) * a.ndim)
    return pl.pallas_call(
        _diff_kernel,
        out_shape=jax.ShapeDtypeStruct((b, t, w), BF16),
        grid=(b, t // TQ),
        in_specs=[qspec, kspec, vspec, full(lam), full(norm), full(misc)],
        out_specs=qspec,
        scratch_shapes=[pltpu.VMEM((2 * N_HEADS, LANES, TQ), BF16)] + _attn_scratch(2 * N_HEADS),
        compiler_params=_cparams(2),
        name="diff",
    )(cq, ck, cvt, lam, norm, misc)


def _mla_prep_kernel(cq_ref, ckv_ref, kr_ref, qn_ref, kvn_ref, wq_ref, wqr_ref, wk_ref, wvt_ref,
                     p_ref, ct_ref, st_ref, q_out, k_out, vt_out):
    x = cq_ref[...].astype(F32)
    xn = (x * lax.rsqrt(jnp.mean(x * x, axis=1, keepdims=True) + RMS_EPS) * qn_ref[...]).astype(BF16)
    q = (jnp.dot(xn, wq_ref[...], preferred_element_type=F32) * ct_ref[...]
         + jnp.dot(xn, wqr_ref[...], preferred_element_type=F32) * st_ref[...])
    q_out[...] = q.astype(q_out.dtype)
    c = ckv_ref[:, :KV_LORA].astype(F32)
    cn = (c * lax.rsqrt(jnp.mean(c * c, axis=1, keepdims=True) + RMS_EPS) * kvn_ref[...]).astype(BF16)
    k = (jnp.dot(cn, wk_ref[...], preferred_element_type=F32)
         + jnp.dot(kr_ref[...], p_ref[...], preferred_element_type=F32))
    k_out[...] = k.astype(k_out.dtype)
    _store_vt(vt_out, _tn_dot(wvt_ref[...], cn))


def _mla_prep(dcq, ckv, kr, qn, kvn, wq, wqr, wk, wvt, pmat, ct, st, l):
    b, t, _ = dcq.shape
    tm = 2048
    hw = N_HEADS * LANES
    row = lambda w: pl.BlockSpec((None, tm, w), lambda i, bb: (bb, i, 0))
    full = lambda a: pl.BlockSpec(a.shape, lambda i, bb: (0,) * a.ndim)
    tab = pl.BlockSpec((tm, hw), lambda i, bb: (i, 0))
    return pl.pallas_call(
        _mla_prep_kernel,
        out_shape=(jax.ShapeDtypeStruct((b, t, hw), BF16), jax.ShapeDtypeStruct((b, t, hw), BF16),
                   jax.ShapeDtypeStruct((b, t // CK, N_HEADS * VROWS, CK), BF16)),
        grid=(t // tm, b),
        in_specs=[row(Q_LORA), row(MXU_N), row(MXU_N), full(qn), full(kvn), _layer_spec(wq, l), _layer_spec(wqr, l),
                  _layer_spec(wk, l), _layer_spec(wvt, l), full(pmat), tab, tab],
        out_specs=(row(hw), row(hw),
                   pl.BlockSpec((None, tm // CK, N_HEADS * VROWS, CK), lambda i, bb: (bb, i, 0, 0))),
        compiler_params=_cparams(2),
        name="mla_prep",
    )(dcq, ckv, kr, qn, kvn, wq, wqr, wk, wvt, pmat, ct, st)


def _mla_kernel(q_ref, k_ref, vt_ref, o_ref, qt_ref, s_ref, mx_ref, m_ref, acc_ref, ot_ref):
    i = pl.program_id(1)
    hs = [slice(h * LANES, (h + 1) * LANES) for h in range(N_HEADS)]
    for h in range(N_HEADS):
        qt_ref[h] = q_ref[:, hs[h]].astype(F32).T.astype(BF16)

    def qk_all(c, lanes):
        start = pl.multiple_of(c * CK, CK)
        return [jnp.dot(k_ref[pl.ds(start, CK), hs[h]], qt_ref[h, :, lanes], preferred_element_type=F32)
                for h in range(N_HEADS)]

    _flash_loop(2 * i, qk_all, None,
                lambda c, h: vt_ref[c, h * VROWS:(h + 1) * VROWS, :],
                (s_ref, mx_ref, m_ref, acc_ref))
    for h in range(N_HEADS):
        ot_ref[h * HEAD_DIM:(h + 1) * HEAD_DIM, :] = _softmax_out(acc_ref.at[h])
    o_ref[...] = ot_ref[...].T.astype(o_ref.dtype)


def _mla(qm, km, vmt):
    b, t, hw = qm.shape
    kspec, vspec = _kv_specs(t, hw)
    return pl.pallas_call(
        _mla_kernel,
        out_shape=jax.ShapeDtypeStruct((b, t, BRANCH_W), BF16),
        grid=(b, t // TQ),
        in_specs=[pl.BlockSpec((None, TQ, hw), lambda bb, i: (bb, i, 0)), kspec, vspec],
        out_specs=pl.BlockSpec((None, TQ, BRANCH_W), lambda bb, i: (bb, i, 0)),
        scratch_shapes=[pltpu.VMEM((N_HEADS, LANES, TQ), BF16)] + _attn_scratch(N_HEADS),
        compiler_params=_cparams(2),
        name="mla",
    )(qm, km, vmt)


def _mem_kv_kernel(x_ref, w_ref, k_ref, vt_ref):
    x = x_ref[...].astype(BF16)
    k_ref[...] = jnp.dot(x, w_ref[:, :BRANCH_W], preferred_element_type=F32).astype(k_ref.dtype)
    _store_vt(vt_ref, _tn_dot(w_ref[:, BRANCH_W:], x))


def _mem_kv(mem, w, l):
    b, m, d = mem.shape
    assert m % CK == 0
    return pl.pallas_call(
        _mem_kv_kernel,
        out_shape=(jax.ShapeDtypeStruct((b, m, BRANCH_W), BF16),
                   jax.ShapeDtypeStruct((b, m // CK, N_HEADS * VROWS, CK), BF16)),
        grid=(b,),
        in_specs=[pl.BlockSpec((None, m, d), lambda bb: (bb, 0, 0)), _layer_spec(w, l)],
        out_specs=(pl.BlockSpec((None, m, BRANCH_W), lambda bb: (bb, 0, 0)),
                   pl.BlockSpec((None, m // CK, N_HEADS * VROWS, CK), lambda bb: (bb, 0, 0, 0))),
        compiler_params=_cparams(1),
        name="mem_kv",
    )(mem, w)


def _mem_kernel(q_ref, k_ref, vt_ref, o_ref, qt_ref, ot_ref):
    _masked_qt(q_ref[...].astype(F32) * (HEAD_DIM ** -0.5 * LOG2E), 6, N_HEADS, qt_ref)
    s_all = [jnp.dot(_half(k_ref[...], h, 6), qt_ref[h], preferred_element_type=F32) for h in range(N_HEADS)]
    for h in range(N_HEADS):
        s_t = s_all[h]
        p = jnp.exp2(s_t - jnp.max(s_t, axis=0, keepdims=True)).astype(BF16)
        acc = jnp.dot(vt_ref[0, h * VROWS:(h + 1) * VROWS, :], p, preferred_element_type=F32)
        ot_ref[h * HEAD_DIM:(h + 1) * HEAD_DIM, :] = acc[:HEAD_DIM, :] / acc[HEAD_DIM:HEAD_DIM + 1, :]
    o_ref[...] = ot_ref[...].T.astype(o_ref.dtype)


def _mem_attn(eq, mk, mvt):
    b, t, w = eq.shape
    m = mk.shape[1]
    assert m == CK
    tm = 2048
    return pl.pallas_call(
        _mem_kernel,
        out_shape=jax.ShapeDtypeStruct((b, t, w), BF16),
        grid=(b, t // tm),
        in_specs=[pl.BlockSpec((None, tm, w), lambda bb, i: (bb, i, 0)),
                  pl.BlockSpec((None, m, w), lambda bb, i: (bb, 0, 0)),
                  pl.BlockSpec((None,) + mvt.shape[1:], lambda bb, i: (bb, 0, 0, 0))],
        out_specs=pl.BlockSpec((None, tm, w), lambda bb, i: (bb, i, 0)),
        scratch_shapes=[pltpu.VMEM((N_HEADS, LANES, tm), BF16), pltpu.VMEM((BRANCH_W, tm), F32)],
        compiler_params=_cparams(2),
        name="mem_attn",
    )(eq, mk, mvt)


def _final_kernel(h_ref, hb_ref, oa_ref, ob_ref, oc_ref, od_ref, oe_ref, z_ref,
                  wg_ref, wb_ref, wo_ref, g_ref, b_ref, h_out, hb_out, acc_ref, *, alpha):
    d = h_ref.shape[1]
    n_blk = h_ref.shape[0] // MXU_N
    for n, o_ref in enumerate((oa_ref, ob_ref, oc_ref, od_ref, oe_ref)):
        for r in range(n_blk):
            rows = slice(r * MXU_N, (r + 1) * MXU_N)
            z = z_ref[rows, n * BRANCH_W:(n + 1) * BRANCH_W].astype(F32)
            y = o_ref[rows, :].astype(F32) * (z / (1.0 + jnp.exp(-z)))
            u = jnp.dot(y.astype(BF16), wb_ref[n], preferred_element_type=F32)
            g = jnp.dot(hb_ref[rows, :], wg_ref[:, n * d:(n + 1) * d], preferred_element_type=F32)
            t = u / (1.0 + jnp.exp(-g))
            acc_ref[rows, :] = t if n == 0 else acc_ref[rows, :] + t
    for r in range(n_blk):
        rows = slice(r * MXU_N, (r + 1) * MXU_N)
        out = jnp.dot(acc_ref[rows, :].astype(BF16), wo_ref[...], preferred_element_type=F32)
        x = alpha * h_ref[rows, :] + out
        mu = jnp.mean(x, axis=1, keepdims=True)
        xc = x - mu
        var = jnp.mean(xc * xc, axis=1, keepdims=True)
        y = xc * lax.rsqrt(var + LN_EPS) * g_ref[...] + b_ref[...]
        h_out[rows, :] = y
        hb_out[rows, :] = y.astype(BF16)


def _final(h, hb, os5, z, wg, wb, wo, ln_g, ln_b, alpha, l):
    n, d = h.shape
    tm = 1024
    row = lambda w: pl.BlockSpec((tm, w), lambda i: (i, 0))
    full = lambda a: pl.BlockSpec(a.shape, lambda i: (0,) * a.ndim)
    return pl.pallas_call(
        functools.partial(_final_kernel, alpha=alpha),
        out_shape=(jax.ShapeDtypeStruct((n, d), F32), jax.ShapeDtypeStruct((n, d), BF16)),
        grid=(n // tm,),
        in_specs=[row(d), row(d)] + [row(BRANCH_W)] * N_BRANCH + [row(N_BRANCH * BRANCH_W),
                  _layer_spec(wg, l), _layer_spec(wb, l), _layer_spec(wo, l), full(ln_g), full(ln_b)],
        out_specs=(row(d), row(d)),
        scratch_shapes=[pltpu.VMEM((tm, d), F32)],
        compiler_params=_cparams(1),
        name="merge_out_ln",
    )(h, hb, *os5, z, wg, wb, wo, ln_g, ln_b)


ROPE_GROUPS = (("a_q", N_HEADS, HEAD_DIM, ROT_64), ("a_k", N_HEADS, HEAD_DIM, ROT_64),
               ("i_q", IDX_HEADS, IDX_DIM, ROT_32), ("i_k", 1, MXU_N, ROT_32),
               ("b_q", N_HEADS, HEAD_DIM, ROT_64), ("b_k", N_HEADS, HEAD_DIM, ROT_64),
               ("c_q", 2 * N_HEADS, DIFF_DIM, ROT_32), ("c_k", 2 * N_HEADS, DIFF_DIM, ROT_32),
               ("d_kr", 1, MXU_N, MLA_ROPE))
PLAIN_COLS = ("d_cq", "d_ckv", "e_q") + tuple(("z", j) for j in range(N_BRANCH))
VALUE_COLS = ("a_v", "b_v", "c_v")
GATE_COLS = tuple(("g", j) for j in range(OFF["g"][1] // MXU_N))


def _window_start(col):
    name, j = col if isinstance(col, tuple) else (col, 0)
    return OFF[name][0] + j * MXU_N


def _weight_prep_kernel(offs_ref, wt_ref, o_ref):
    o_ref[...] = wt_ref[...].T.astype(o_ref.dtype)


def _weight_windows(wt, cols, name):
    depth, n, d = wt.shape
    starts = [_window_start(c) for c in cols]
    assert all(st % SUBLANES == 0 and st + MXU_N <= n for st in starts)
    grid_spec = pltpu.PrefetchScalarGridSpec(
        num_scalar_prefetch=1,
        grid=(depth, len(cols)),
        in_specs=[pl.BlockSpec((None, pl.Element(MXU_N), pl.Element(d)),
                               lambda l, j, offs: (l, pl.multiple_of(offs[j], SUBLANES), 0))],
        out_specs=pl.BlockSpec((None, d, MXU_N), lambda l, j, offs: (l, 0, j)),
    )
    return pl.pallas_call(
        _weight_prep_kernel,
        out_shape=jax.ShapeDtypeStruct((depth, d, MXU_N * len(cols)), BF16),
        grid_spec=grid_spec,
        compiler_params=_cparams(2),
        name=name,
    )(jnp.asarray(np.asarray(starts, np.int32)), wt)


def _weight_prep(w_in):
    wt = jnp.swapaxes(w_in, 1, 2)
    return (_weight_windows(wt, PLAIN_COLS, "wprep_plain"), _weight_windows(wt, VALUE_COLS, "wprep_value"),
            _weight_windows(wt, [name for name, *_ in ROPE_GROUPS], "wprep_rope"),
            _weight_windows(wt, GATE_COLS, "wprep_gate"))


def _rope_tables(seq, rot_dim):
    pos = jnp.arange(seq, dtype=F32)
    inv = ROPE_THETA ** (-jnp.arange(0, rot_dim, 2, dtype=F32) / rot_dim)
    ang = pos[:, None] * inv[None, :]
    return jnp.cos(ang), jnp.sin(ang)


def _rope_cs(t, nh, hd, r):
    cos, sin = _rope_tables(t, r)
    c = jnp.concatenate([cos, cos, jnp.ones((t, hd - r), F32)], axis=1)
    s = jnp.concatenate([-sin, sin, jnp.zeros((t, hd - r), F32)], axis=1)
    return jnp.tile(c, (1, nh)), jnp.tile(s, (1, nh))


def kernel(x, mem, ln0_g, ln0_b, w_in, mla_q_norm, w_uq, mla_kv_norm, w_ukv, diff_lam, diff_norm,
           w_mem_kv, w_branch, w_out, ln_g, ln_b):
    b, t, d = x.shape
    depth = w_in.shape[0]
    alpha = (2 * depth) ** 0.25
    assert t % 512 == 0 and d == 1024

    w_plain, w_vt, w_rope, wg = _weight_prep(w_in)
    plain_widths = (BRANCH_W,) * 3 + (N_BRANCH * BRANCH_W,)
    rope_heads = tuple((hd, r // 2) for _, _, hd, r in ROPE_GROUPS)
    patterns = sorted(set((nh, hd, r) for _, nh, hd, r in ROPE_GROUPS))
    rope_tables = tuple(patterns.index((nh, hd, r)) for _, nh, hd, r in ROPE_GROUPS)
    cs = [_rope_cs(t, nh, hd, r) for nh, hd, r in patterns]
    ctab = jnp.stack([c for c, _ in cs])
    stab = jnp.stack([s for _, s in cs])

    uq = w_uq.reshape(depth, Q_LORA, N_HEADS, MLA_NOPE + MLA_ROPE)
    qn_w, qr_w = uq[..., :MLA_NOPE], uq[..., MLA_NOPE:]
    pad32 = jnp.zeros((depth, Q_LORA, N_HEADS, LANES - MLA_NOPE - MLA_ROPE), w_uq.dtype)
    hw = N_HEADS * LANES
    wq = jnp.concatenate([qn_w, qr_w, pad32], axis=-1).reshape(depth, Q_LORA, hw).astype(BF16)
    half = MLA_ROPE // 2
    wq_rot = jnp.concatenate([jnp.zeros_like(qn_w), -qr_w[..., half:], qr_w[..., :half], pad32],
                             axis=-1).reshape(depth, Q_LORA, hw).astype(BF16)
    cos_m, sin_m = _rope_tables(t, MLA_ROPE)
    one = lambda n: jnp.ones((t, n), F32)
    zer = lambda n: jnp.zeros((t, n), F32)
    qs = (MLA_NOPE + MLA_ROPE) ** -0.5 * LOG2E
    ct_q = qs * jnp.tile(jnp.concatenate([one(MLA_NOPE), cos_m, cos_m, one(LANES - MLA_NOPE - MLA_ROPE)], axis=1), (1, N_HEADS))
    st_q = qs * jnp.tile(jnp.concatenate([zer(MLA_NOPE), sin_m, sin_m, zer(LANES - MLA_NOPE - MLA_ROPE)], axis=1), (1, N_HEADS))
    ukv = w_ukv.reshape(depth, KV_LORA, N_HEADS, MLA_NOPE + MLA_V)
    wk = jnp.concatenate([ukv[..., :MLA_NOPE], jnp.zeros((depth, KV_LORA, N_HEADS, LANES - MLA_NOPE), w_ukv.dtype)],
                         axis=-1).reshape(depth, KV_LORA, hw).astype(BF16)
    wvt = ukv[..., MLA_NOPE:].reshape(depth, KV_LORA, N_HEADS * MLA_V).astype(BF16)
    place = np.zeros((MXU_N, hw), np.float32)
    for hh in range(N_HEADS):
        for j in range(MLA_ROPE):
            place[j, hh * LANES + MLA_NOPE + j] = 1.0
    place = jnp.asarray(place, BF16)

    wb = w_branch.astype(BF16)
    wo = w_out.astype(BF16)
    wmem = w_mem_kv.astype(BF16)
    norm_t = jnp.broadcast_to(diff_norm.astype(F32)[:, :, None], (depth, HEAD_DIM, TQ))

    h, hb = _layer_norm0(x.reshape(b * t, d), ln0_g, ln0_b)
    for l in range(depth):
        hb3 = hb.reshape(b, t, d)
        avt, bvt, cvt, dcq, ckv_iw, eq, z = _proj_plain(hb3, w_plain, w_vt, plain_widths, l)
        aq, ak, iq, ik, bq, bk, cq, ck, kr = _proj_rope(hb3, w_rope, ctab, stab, rope_heads, rope_tables, l)

        o_a = _dsa(aq, ak, avt, iq, ik, ik)
        o_b = _moba(bq, bk, bvt, _kbar(bk))
        lam_init = 0.8 - 0.6 * math.exp(-0.3 * l)
        misc = jnp.full((SUBLANES, LANES), lam_init, F32)
        o_c = _diff(cq, ck, cvt, diff_lam[l].astype(F32), norm_t[l], misc)
        qm, km, vmt = _mla_prep(dcq, ckv_iw, kr, mla_q_norm[l].reshape(1, Q_LORA), mla_kv_norm[l].reshape(1, KV_LORA),
                                wq, wq_rot, wk, wvt, place, ct_q, st_q, l)
        o_d = _mla(qm, km, vmt)
        o_e = _mem_attn(eq, *_mem_kv(mem, wmem, l))

        os5 = [o.reshape(b * t, BRANCH_W) for o in (o_a, o_b, o_c, o_d, o_e)]
        h, hb = _final(h, hb, os5, z.reshape(b * t, N_BRANCH * BRANCH_W), wg, wb, wo,
                       ln_g[l].reshape(1, d), ln_b[l].reshape(1, d), alpha, l)
    return h.reshape(b, t, d)
```

```python
import functools
import math

import numpy as np
import jax
import jax.numpy as jnp
from jax import lax
from jax.experimental import pallas as pl
from jax.experimental.pallas import tpu as pltpu

F32 = jnp.float32
BF16 = jnp.bfloat16
I32 = jnp.int32
I16 = jnp.int16

N_HEADS = 4
HEAD_DIM = 64
BRANCH_W = N_HEADS * HEAD_DIM
N_BRANCH = 5
ROPE_THETA = 500000.0
ROT_64 = 16
ROT_32 = 8
IDX_HEADS = 8
IDX_DIM = 32
TOPK_MAX = 256
MOBA_BLOCK = 256
MOBA_TOPK = 3
DIFF_DIM = 32
Q_LORA = 256
KV_LORA = 128
MLA_NOPE = 64
MLA_ROPE = 32
MLA_V = 64
LN_EPS = 1e-5
RMS_EPS = 1e-6

IN_LAYOUT = (
    ("a_q", BRANCH_W), ("a_k", BRANCH_W), ("a_v", BRANCH_W),
    ("i_q", IDX_HEADS * IDX_DIM), ("i_k", IDX_DIM), ("i_w", IDX_HEADS),
    ("b_q", BRANCH_W), ("b_k", BRANCH_W), ("b_v", BRANCH_W),
    ("c_q", BRANCH_W), ("c_k", BRANCH_W), ("c_v", BRANCH_W),
    ("d_cq", Q_LORA), ("d_ckv", KV_LORA), ("d_kr", MLA_ROPE),
    ("e_q", BRANCH_W),
    ("z", N_BRANCH * BRANCH_W),
    ("g", N_BRANCH * 1024),
)

SUBLANES = 8
LANES = 128
MXU_N = 256
TQ = 512
CK = 256
VROWS = HEAD_DIM + 16
FLASH_UNROLL = 4
NEG = -1e30
LOG2E = math.log2(math.e)
INT_MIN = np.int32(-2 ** 31)
HALF16 = 1 << 15
VMEM_LIMIT = 56 * 1024 * 1024


def _offsets():
    off, out = 0, {}
    for name, size in IN_LAYOUT:
        out[name] = (off, size)
        off += size
    return out


OFF = _offsets()


def _nt_dot(a, b):
    return lax.dot_general(a, b, (((1,), (1,)), ((), ())), preferred_element_type=F32)


def _tn_dot(w, x):
    return lax.dot_general(w, x, (((0,), (1,)), ((), ())), preferred_element_type=F32)


def _fold_rows(w, rows=SUBLANES):
    xs = [w[r:r + rows, :] for r in range(0, w.shape[0], rows)]
    while len(xs) > 1:
        xs = [xs[j] + xs[j + 1] for j in range(0, len(xs) - 1, 2)] + ([xs[-1]] if len(xs) % 2 else [])
    return xs[0]


def _masked_qt(q, shift, n, qt_ref):
    qt = q.T
    dim = lax.broadcasted_iota(I32, (LANES, qt.shape[1]), 0)
    for j in range(n):
        half = (j << shift) // LANES
        rows = qt[half * LANES:(half + 1) * LANES, :]
        qt_ref[j] = jnp.where(((dim + half * LANES) >> shift) == j, rows, 0.0).astype(BF16)


def _half(kc, j, shift):
    half = (j << shift) // LANES
    return kc[:, half * LANES:(half + 1) * LANES]


def _cparams(n_axes):
    return pltpu.CompilerParams(dimension_semantics=("arbitrary",) * n_axes,
                                vmem_limit_bytes=VMEM_LIMIT)


def _layer_spec(a, l):
    return pl.BlockSpec((None,) + a.shape[1:], lambda *_: (l,) + (0,) * (a.ndim - 1))


def _softmax_step(s_t, m_tile, vt_h, m_ref, acc_ref):
    m_old = m_ref[...]
    m_new = jnp.maximum(m_old, m_tile)
    alpha = jnp.exp2(m_old - m_new)
    p = jnp.exp2(s_t - m_new)
    acc_ref[...] = alpha * acc_ref[...] + jnp.dot(vt_h, p.astype(BF16), preferred_element_type=F32)
    m_ref[...] = m_new


def _softmax_init(m_ref, acc_ref):
    m_ref[...] = jnp.full(m_ref.shape, NEG, F32)
    acc_ref[...] = jnp.zeros(acc_ref.shape, F32)


def _softmax_out(acc_ref):
    return acc_ref[:HEAD_DIM, :] / acc_ref[HEAD_DIM:HEAD_DIM + 1, :]


def _store_vt(o_ref, vt):
    ones = jnp.ones((VROWS - HEAD_DIM, CK), o_ref.dtype)
    for j in range(o_ref.shape[0]):
        for h in range(N_HEADS):
            o_ref[j, h * VROWS:h * VROWS + HEAD_DIM, :] = (
                vt[h * HEAD_DIM:(h + 1) * HEAD_DIM, j * CK:(j + 1) * CK].astype(o_ref.dtype))
            o_ref[j, h * VROWS + HEAD_DIM:(h + 1) * VROWS, :] = ones


def _flash_loop(n_full, qk_all, mask, vt_rows, state, prep=None, causal_tail=True):
    s_ref, mx_ref, m_ref, acc_ref = state
    n_state = m_ref.shape[0]
    for j in range(n_state):
        _softmax_init(m_ref.at[j], acc_ref.at[j])

    def lanes_of(d):
        return slice(CK, TQ) if d == 1 else slice(None)

    def park(c, slot, d=None):
        lanes = lanes_of(d)
        ctx = c if prep is None else prep(c, lanes)
        for j, s in enumerate(qk_all(c, lanes)):
            if mask is not None:
                s = mask(ctx, j, s, lanes)
            if d is not None and causal_tail:
                s = jnp.where(_causal(d), s, NEG)
            s_ref[slot, j, :, lanes] = s
            mx_ref[slot, j, :, lanes] = jnp.max(s, axis=0, keepdims=True)

    def consume(c, slot, d=None):
        lanes = lanes_of(d)
        for j in range(n_state):
            _softmax_step(s_ref[slot, j, :, lanes], mx_ref[slot, j, :, lanes], vt_rows(c, j),
                          m_ref.at[j, :, lanes], acc_ref.at[j, :, lanes])

    def pair(c):
        park(c + 1, 1)
        consume(c, 0)
        park(c + 2, 0)
        consume(c + 1, 1)

    def body(g, carry):
        for u in range(0, FLASH_UNROLL, 2):
            pair(FLASH_UNROLL * g + u)
        return carry

    @pl.when(n_full == 0)
    def _():
        park(0, 0, d=0)
        park(1, 1, d=1)
        consume(0, 0)
        consume(1, 1, d=1)

    @pl.when(n_full > 0)
    def _():
        park(0, 0)
        n_loop = n_full - 2
        n_group = lax.shift_right_logical(n_loop, FLASH_UNROLL.bit_length() - 1)
        lax.fori_loop(0, n_group, body, 0)
        c0 = FLASH_UNROLL * n_group
        for u in range(FLASH_UNROLL // 2 - 1):
            @pl.when(n_loop - c0 >= 2 * (u + 1))
            def _(u=u):
                pair(c0 + 2 * u)
        c = n_loop
        park(c + 1, 1)
        consume(c, 0)
        park(c + 2, 0, d=0)
        consume(c + 1, 1)
        park(c + 3, 1, d=1)
        consume(c + 2, 0)
        consume(c + 3, 1, d=1)


def _causal(d):
    shape = (CK, TQ - d * CK)
    return lax.broadcasted_iota(I32, shape, 0) <= lax.broadcasted_iota(I32, shape, 1)


def _attn_scratch(n_state):
    return [pltpu.VMEM((2, n_state, CK, TQ), F32), pltpu.VMEM((2, n_state, 1, TQ), F32),
            pltpu.VMEM((n_state, 1, TQ), F32), pltpu.VMEM((n_state, VROWS, TQ), F32),
            pltpu.VMEM((BRANCH_W, TQ), F32)]


def _kv_specs(t, w):
    kspec = pl.BlockSpec((None, t, w), lambda bb, i: (bb, 0, 0))
    vspec = pl.BlockSpec((None, t // CK, N_HEADS * VROWS, CK), lambda bb, i: (bb, 0, 0, 0))
    return kspec, vspec


def _ln_kernel(x_ref, g_ref, b_ref, h_ref, hb_ref):
    x = x_ref[...]
    mu = jnp.mean(x, axis=1, keepdims=True)
    xc = x - mu
    var = jnp.mean(xc * xc, axis=1, keepdims=True)
    y = xc * lax.rsqrt(var + LN_EPS) * g_ref[...] + b_ref[...]
    h_ref[...] = y
    hb_ref[...] = y.astype(BF16)


def _layer_norm0(x2, g, b):
    n, d = x2.shape
    tm = 2048
    row = pl.BlockSpec((tm, d), lambda i: (i, 0))
    vec = pl.BlockSpec((1, d), lambda i: (0, 0))
    return pl.pallas_call(
        _ln_kernel,
        out_shape=(jax.ShapeDtypeStruct((n, d), F32), jax.ShapeDtypeStruct((n, d), BF16)),
        grid=(n // tm,),
        in_specs=[row, vec, vec],
        out_specs=(row, row),
        compiler_params=_cparams(1),
        name="ln0",
    )(x2, g.reshape(1, d), b.reshape(1, d))


def _proj_plain_kernel(x_ref, w_ref, wt_ref, *out_refs, n_t):
    for g, o_ref in enumerate(out_refs[:n_t]):
        _store_vt(o_ref, _tn_dot(wt_ref[:, g * BRANCH_W:(g + 1) * BRANCH_W], x_ref[...]))
    off = 0
    for o_ref in out_refs[n_t:]:
        wd = o_ref.shape[-1]
        for j in range(0, wd, MXU_N):
            acc = jnp.dot(x_ref[...], w_ref[:, off + j:off + j + MXU_N], preferred_element_type=F32)
            o_ref[:, j:j + MXU_N] = acc.astype(o_ref.dtype)
        off += wd


def _proj_plain(hb3, w, wt, widths, l):
    b, t, d = hb3.shape
    tm = 2048
    n_t = wt.shape[-1] // BRANCH_W
    shapes = [jax.ShapeDtypeStruct((b, t // CK, N_HEADS * VROWS, CK), BF16)] * n_t
    specs = [pl.BlockSpec((None, tm // CK, N_HEADS * VROWS, CK), lambda i, bb: (bb, i, 0, 0))] * n_t
    shapes += [jax.ShapeDtypeStruct((b, t, wd), BF16) for wd in widths]
    specs += [pl.BlockSpec((None, tm, wd), lambda i, bb: (bb, i, 0)) for wd in widths]
    return pl.pallas_call(
        functools.partial(_proj_plain_kernel, n_t=n_t),
        out_shape=tuple(shapes),
        grid=(t // tm, b),
        in_specs=[pl.BlockSpec((None, tm, d), lambda i, bb: (bb, i, 0)),
                  _layer_spec(w, l), _layer_spec(wt, l)],
        out_specs=tuple(specs),
        compiler_params=_cparams(2),
        name="proj_plain",
    )(hb3, w, wt)


def _proj_rope_kernel(x_ref, w_ref, c_ref, s_ref, *out_refs, heads, tables):
    lane = lax.broadcasted_iota(I32, (x_ref.shape[0], MXU_N), 1)
    for g, o_ref in enumerate(out_refs):
        hd, half = heads[g]
        sl = slice(g * MXU_N, (g + 1) * MXU_N)
        acc = jnp.dot(x_ref[...], w_ref[:, sl], preferred_element_type=F32)
        partner = jnp.where((lane & (hd - 1)) < half,
                            pltpu.roll(acc, MXU_N - half, 1), pltpu.roll(acc, half, 1))
        o_ref[...] = (acc * c_ref[tables[g]] + partner * s_ref[tables[g]]).astype(o_ref.dtype)


def _proj_rope(hb3, w, ctab, stab, heads, tables, l):
    b, t, d = hb3.shape
    tm = 1024
    assert w.shape[-1] == MXU_N * len(heads)
    tspec = pl.BlockSpec((ctab.shape[0], tm, MXU_N), lambda i, bb: (0, i, 0))
    ospec = pl.BlockSpec((None, tm, MXU_N), lambda i, bb: (bb, i, 0))
    return pl.pallas_call(
        functools.partial(_proj_rope_kernel, heads=heads, tables=tables),
        out_shape=(jax.ShapeDtypeStruct((b, t, MXU_N), BF16),) * len(heads),
        grid=(t // tm, b),
        in_specs=[pl.BlockSpec((None, tm, d), lambda i, bb: (bb, i, 0)),
                  _layer_spec(w, l), tspec, tspec],
        out_specs=(ospec,) * len(heads),
        compiler_params=_cparams(2),
        name="proj_rope",
    )(hb3, w, ctab, stab)


def _dsa_kernel(aq_ref, ak_ref, avt_ref, iq_ref, ik_ref, iw_ref, pick_ref, tri_ref, o_ref,
                keys_ref, hi_ref, lo_ref, bk_ref, top_ref, sel_ref, iqt_ref, aqt_ref, wt_ref, thr_ref, s_ref, mx_ref, m_ref, acc_ref, ot_ref,
                *, topk, idx_scale):
    i = pl.program_id(1)
    n_full = 2 * i

    iqt = iq_ref[...].astype(F32).T
    for hh in range(IDX_HEADS):
        iqt_ref[hh] = iqt[hh * IDX_DIM:(hh + 1) * IDX_DIM, :].astype(BF16)
    _masked_qt(aq_ref[...].astype(F32) * (HEAD_DIM ** -0.5 * LOG2E), 6, N_HEADS, aqt_ref)
    wt_ref[...] = _nt_dot(pick_ref[...], iw_ref[...]) * idx_scale

    def lanes_of(d):
        return slice(CK, TQ) if d == 1 else slice(None)

    def logits(c, d):
        kc = ik_ref[pl.ds(pl.multiple_of(c * CK, CK), CK), :]
        return [jnp.dot(kc[:, :IDX_DIM], iqt_ref[hh, :, lanes_of(d)], preferred_element_type=F32)
                for hh in range(IDX_HEADS)]

    def put_keys(c, key, lanes):
        keys_ref[c, :, lanes] = key
        hi_ref[c, :, lanes] = (key >> 16).astype(I16)
        lo_ref[c, :, lanes] = ((key & 0xFFFF) - HALF16).astype(I16)

    def score_chunk(c, lg, d):
        lanes = lanes_of(d)
        sc = jnp.zeros(lg[0].shape, F32)
        for hh in range(IDX_HEADS):
            sc = sc + jnp.maximum(lg[hh], 0.0) * wt_ref[hh:hh + 1, lanes]
        bits = pltpu.bitcast(sc, I32)
        key = jnp.where(bits < 0, INT_MIN - bits, bits)
        put_keys(c, key if d is None else jnp.where(_causal(d), key, INT_MIN), lanes)
        if d == 1:
            put_keys(c, jnp.full((CK, CK), INT_MIN, I32), slice(0, CK))

    def score_pair(c, d0, d1):
        lg0, lg1 = logits(c, d0), logits(c + 1, d1)
        score_chunk(c, lg0, d0)
        score_chunk(c + 1, lg1, d1)

    def score_body(p, carry):
        score_pair(2 * p, None, None)
        return carry

    lax.fori_loop(0, i, score_body, 0)
    score_pair(n_full, 0, 1)

    def pair_loop(body, init, last=None):
        def pair(p, carry):
            return body(2 * p + 1, body(2 * p, carry))
        carry = body(n_full, lax.fori_loop(0, i, pair, init))
        return (last or body)(n_full + 1, carry)

    def count16(pred, also=None):
        def hits(c, lanes):
            hit = jnp.where(pred(c, lanes), jnp.int16(1), jnp.int16(0))
            if also is not None:
                hit = jnp.where(also(c, lanes), hit, jnp.int16(0))
            return _fold_rows(hit, 2 * SUBLANES)

        def body(c, part):
            return part + hits(c, slice(None))

        def last(c, part):
            return jnp.concatenate([part[:, :CK], part[:, CK:] + hits(c, slice(CK, TQ))], axis=1)

        part = pair_loop(body, jnp.zeros((2 * SUBLANES, TQ), I16), last)
        return jnp.sum(part.astype(F32), axis=0, keepdims=True)

    def search16(count_ge, need):
        def bit_body(bi, t_u):
            c_u = t_u | jnp.left_shift(jnp.int32(1), 15 - bi)
            cnt = count_ge((c_u - HALF16).astype(I16))
            return jnp.where(cnt >= need, c_u, t_u)
        return lax.fori_loop(0, 16, bit_body, jnp.zeros((1, TQ), I32))

    hi_u = search16(lambda ck: count16(lambda c, lanes: hi_ref[c, :, lanes] >= ck[:, lanes]),
                    float(topk))
    thr_hi = (hi_u - HALF16).astype(I16)
    n_above = count16(lambda c, lanes: hi_ref[c, :, lanes] > thr_hi[:, lanes])
    need_lo = float(topk) - n_above

    fill = jnp.int16(-HALF16)
    top_ref[...] = jnp.full(top_ref.shape, fill, I16)

    def bucket(c):
        return jnp.where(hi_ref[c] == thr_hi, lo_ref[c], fill)

    def top_body(c, carry):
        x = bucket(c)
        for r in range(4):
            m = top_ref[r]
            swap = x > m
            top_ref[r] = jnp.where(swap, x, m)
            x = jnp.where(swap, m, x)
        return carry

    many = i > 1

    @pl.when(many)
    def _():
        pair_loop(top_body, 0)

    fourth = jnp.max(_fold_rows(jnp.where(top_ref[3] > fill, jnp.int16(1), jnp.int16(0)),
                                2 * SUBLANES).astype(F32)) > 0.5
    use_top = jnp.logical_and(many, jnp.logical_not(fourth))

    def count_top(pred):
        part = jnp.zeros((2 * SUBLANES, TQ), I16)
        for r in range(3):
            part = part + _fold_rows(jnp.where(pred(top_ref[r]), jnp.int16(1), jnp.int16(0)), 2 * SUBLANES)
        return jnp.sum(part.astype(F32), axis=0, keepdims=True)

    def lo_select(count_ge, count_gt):
        lo_u = search16(count_ge, need_lo)
        sel_ref[0:1, :] = lo_u.astype(F32)
        sel_ref[1:2, :] = count_gt((lo_u - HALF16).astype(I16))

    @pl.when(use_top)
    def _():
        lo_select(lambda ck: count_top(lambda x: x >= ck), lambda t: count_top(lambda x: x > t))

    @pl.when(jnp.logical_not(use_top))
    def _():
        def bucket_body(c, carry):
            bk_ref[c] = bucket(c)
            return carry

        pair_loop(bucket_body, 0)
        lo_select(lambda ck: count16(lambda c, lanes: bk_ref[c, :, lanes] >= ck[:, lanes]),
                  lambda t: count16(lambda c, lanes: bk_ref[c, :, lanes] > t[:, lanes]))

    lo_u = sel_ref[0:1, :].astype(I32)
    thr_lo = (lo_u - HALF16).astype(I16)
    thr = ((hi_u - HALF16) << 16) | lo_u

    n_gt = n_above + sel_ref[1:2, :]
    n_eq = count16(lambda c, lanes: lo_ref[c, :, lanes] == thr_lo[:, lanes],
                   also=lambda c, lanes: hi_ref[c, :, lanes] == thr_hi[:, lanes])
    need = float(topk) - n_gt
    amb = jnp.logical_and(n_eq > need, thr > INT_MIN)
    any_amb = jnp.max(jnp.where(amb, 1.0, 0.0)) > 0.5

    @pl.when(any_amb)
    def _():
        def drop_body(c, seen):
            k = keys_ref[c]
            eq = k == thr
            eqf = jnp.where(eq, 1.0, 0.0)
            rank = jnp.dot(tri_ref[...], eqf.astype(BF16), preferred_element_type=F32) + seen
            drop = jnp.logical_and(jnp.logical_and(eq, rank > need), amb)
            keys_ref[c] = jnp.where(drop, INT_MIN, k)
            return seen + jnp.sum(eqf, axis=0, keepdims=True)

        pair_loop(drop_body, jnp.zeros((1, TQ), F32))

    thr_ref[...] = jnp.maximum(thr, INT_MIN + 1)

    def qk_all(c, lanes):
        kc = ak_ref[pl.ds(pl.multiple_of(c * CK, CK), CK), :]
        return [jnp.dot(_half(kc, h, 6), aqt_ref[h, :, lanes], preferred_element_type=F32) for h in range(N_HEADS)]

    _flash_loop(n_full, qk_all,
                lambda bias, h, s, lanes: s + bias,
                lambda c, h: avt_ref[c, h * VROWS:(h + 1) * VROWS, :],
                (s_ref, mx_ref, m_ref, acc_ref),
                prep=lambda c, lanes: jnp.where(keys_ref[c, :, lanes] >= thr_ref[:, lanes], 0.0, NEG),
                causal_tail=False)
    for h in range(N_HEADS):
        ot_ref[h * HEAD_DIM:(h + 1) * HEAD_DIM, :] = _softmax_out(acc_ref.at[h])
    o_ref[...] = ot_ref[...].T.astype(o_ref.dtype)


def _dsa(aq, ak, avt, iq, ik, iw):
    b, t, _ = aq.shape
    topk = min(TOPK_MAX, t // 4)
    qspec = pl.BlockSpec((None, TQ, BRANCH_W), lambda bb, i: (bb, i, 0))
    kspec, vspec = _kv_specs(t, BRANCH_W)
    pick = np.zeros((2 * SUBLANES, MXU_N), np.float32)
    for hh in range(IDX_HEADS):
        pick[hh, IDX_DIM + hh] = 1.0
    pick = jnp.asarray(pick, BF16)
    tri = jnp.asarray(np.tril(np.ones((CK, CK), np.float32)), BF16)
    kern = functools.partial(_dsa_kernel, topk=topk, idx_scale=(IDX_HEADS * IDX_DIM) ** -0.5)
    return pl.pallas_call(
        kern,
        out_shape=jax.ShapeDtypeStruct((b, t, BRANCH_W), BF16),
        grid=(b, t // TQ),
        in_specs=[qspec, kspec, vspec, qspec, kspec, qspec,
                  pl.BlockSpec(pick.shape, lambda bb, i: (0, 0)), pl.BlockSpec(tri.shape, lambda bb, i: (0, 0))],
        out_specs=qspec,
        scratch_shapes=[
            pltpu.VMEM((t // CK, CK, TQ), I32),
            pltpu.VMEM((t // CK, CK, TQ), I16),
            pltpu.VMEM((t // CK, CK, TQ), I16),
            pltpu.VMEM((t // CK, CK, TQ), I16),
            pltpu.VMEM((4, CK, TQ), I16),
            pltpu.VMEM((SUBLANES, TQ), F32),
            pltpu.VMEM((IDX_HEADS, IDX_DIM, TQ), BF16),
            pltpu.VMEM((N_HEADS, LANES, TQ), BF16),
            pltpu.VMEM((2 * SUBLANES, TQ), F32),
            pltpu.VMEM((1, TQ), I32),
        ] + _attn_scratch(N_HEADS),
        compiler_params=_cparams(2),
        name="dsa",
    )(aq, ak, avt, iq, ik, iw, pick, tri)


def _kbar_kernel(k_ref, o_ref):
    o_ref[...] = jnp.zeros(o_ref.shape, o_ref.dtype)
    nb = k_ref.shape[0] // MOBA_BLOCK
    for n in range(nb):
        blk = k_ref[n * MOBA_BLOCK:(n + 1) * MOBA_BLOCK, :].astype(F32)
        o_ref[n:n + 1, :] = jnp.mean(blk, axis=0, keepdims=True).astype(o_ref.dtype)


def _kbar(bk):
    b, t, w = bk.shape
    nbp = max(2 * SUBLANES, t // MOBA_BLOCK)
    return pl.pallas_call(
        _kbar_kernel,
        out_shape=jax.ShapeDtypeStruct((b, nbp, w), BF16),
        grid=(b,),
        in_specs=[pl.BlockSpec((None, t, w), lambda bb: (bb, 0, 0))],
        out_specs=pl.BlockSpec((None, nbp, w), lambda bb: (bb, 0, 0)),
        compiler_params=_cparams(1),
        name="moba_kbar",
    )(bk)


def _moba_kernel(q_ref, k_ref, vt_ref, kbar_ref, o_ref, qt_ref, bias_ref, s_ref, mx_ref, m_ref, acc_ref, ot_ref):
    i = pl.program_id(1)
    nbp = kbar_ref.shape[0]
    blk = lax.broadcasted_iota(I32, (nbp, TQ), 0)
    blk_f = blk.astype(F32)
    own = 2 * i + (lax.broadcasted_iota(I32, (nbp, TQ), 1) >> (MOBA_BLOCK.bit_length() - 1))
    _masked_qt(q_ref[...].astype(F32) * (HEAD_DIM ** -0.5 * LOG2E), 6, N_HEADS, qt_ref)

    for h in range(N_HEADS):
        g = jnp.where(blk < own, jnp.dot(_half(kbar_ref[...], h, 6), qt_ref[h], preferred_element_type=F32), NEG)
        bias = jnp.full((nbp, TQ), NEG, F32)
        for _ in range(MOBA_TOPK):
            mx = jnp.max(g, axis=0, keepdims=True)
            first = jnp.min(jnp.where(g == mx, blk_f, 1e9), axis=0, keepdims=True)
            pick = jnp.logical_and(blk_f == first, mx > 0.5 * NEG)
            bias = jnp.where(pick, 0.0, bias)
            g = jnp.where(pick, NEG, g)
        bias_ref[h] = jnp.where(blk == own, 0.0, bias)

    def qk_all(c, lanes):
        kc = k_ref[pl.ds(pl.multiple_of(c * CK, CK), CK), :]
        return [jnp.dot(_half(kc, h, 6), qt_ref[h, :, lanes], preferred_element_type=F32) for h in range(N_HEADS)]

    _flash_loop(2 * i, qk_all, lambda c, h, s, lanes: s + bias_ref[h, pl.ds(c, 1), lanes],
                lambda c, h: vt_ref[c, h * VROWS:(h + 1) * VROWS, :], (s_ref, mx_ref, m_ref, acc_ref))
    for h in range(N_HEADS):
        ot_ref[h * HEAD_DIM:(h + 1) * HEAD_DIM, :] = _softmax_out(acc_ref.at[h])
    o_ref[...] = ot_ref[...].T.astype(o_ref.dtype)


def _moba(bq, bk, bvt, kbar):
    b, t, w = bq.shape
    assert TQ == 2 * MOBA_BLOCK and CK == MOBA_BLOCK and t % TQ == 0
    nbp = kbar.shape[1]
    qspec = pl.BlockSpec((None, TQ, w), lambda bb, i: (bb, i, 0))
    kspec, vspec = _kv_specs(t, w)
    return pl.pallas_call(
        _moba_kernel,
        out_shape=jax.ShapeDtypeStruct((b, t, w), BF16),
        grid=(b, t // TQ),
        in_specs=[qspec, kspec, vspec, pl.BlockSpec((None, nbp, w), lambda bb, i: (bb, 0, 0))],
        out_specs=qspec,
        scratch_shapes=[pltpu.VMEM((N_HEADS, LANES, TQ), BF16), pltpu.VMEM((N_HEADS, nbp, TQ), F32)]
        + _attn_scratch(N_HEADS),
        compiler_params=_cparams(2),
        name="moba",
    )(bq, bk, bvt, kbar)


def _diff_kernel(q_ref, k_ref, vt_ref, lam_ref, norm_ref, misc_ref, o_ref,
                 qt_ref, s_ref, mx_ref, m_ref, acc_ref, ot_ref):
    i = pl.program_id(1)
    _masked_qt(q_ref[...].astype(F32) * (DIFF_DIM ** -0.5 * LOG2E), 5, 2 * N_HEADS, qt_ref)

    dl = lam_ref[...]
    lam_init = misc_ref[0:1, 0:1]
    lam = (jnp.exp(jnp.sum(dl[0:1, :] * dl[1:2, :], axis=1, keepdims=True))
           - jnp.exp(jnp.sum(dl[2:3, :] * dl[3:4, :], axis=1, keepdims=True)) + lam_init)

    def qk_all(c, lanes):
        kc = k_ref[pl.ds(pl.multiple_of(c * CK, CK), CK), :]
        return [jnp.dot(_half(kc, j, 5), qt_ref[j, :, lanes], preferred_element_type=F32) for j in range(2 * N_HEADS)]

    _flash_loop(2 * i, qk_all, None,
                lambda c, j: vt_ref[c, (j // 2) * VROWS:(j // 2 + 1) * VROWS, :],
                (s_ref, mx_ref, m_ref, acc_ref))

    post = norm_ref[...] * (1.0 - lam_init)
    for h in range(N_HEADS):
        o_h = _softmax_out(acc_ref.at[2 * h]) - lam * _softmax_out(acc_ref.at[2 * h + 1])
        ms = jnp.mean(o_h * o_h, axis=0, keepdims=True)
        ot_ref[h * HEAD_DIM:(h + 1) * HEAD_DIM, :] = o_h * lax.rsqrt(ms + RMS_EPS) * post
    o_ref[...] = ot_ref[...].T.astype(o_ref.dtype)


def _diff(cq, ck, cvt, lam, norm, misc):
    b, t, w = cq.shape
    qspec = pl.BlockSpec((None, TQ, w), lambda bb, i: (bb, i, 0))
    kspec, vspec = _kv_specs(t, w)
    full = lambda a: pl.BlockSpec(a.shape, lambda bb, i: (0,) * a.ndim)
    return pl.pallas_call(
        _diff_kernel,
        out_shape=jax.ShapeDtypeStruct((b, t, w), BF16),
        grid=(b, t // TQ),
        in_specs=[qspec, kspec, vspec, full(lam), full(norm), full(misc)],
        out_specs=qspec,
        scratch_shapes=[pltpu.VMEM((2 * N_HEADS, LANES, TQ), BF16)] + _attn_scratch(2 * N_HEADS),
        compiler_params=_cparams(2),
        name="diff",
    )(cq, ck, cvt, lam, norm, misc)


def _mla_prep_kernel(cq_ref, ckv_ref, kr_ref, qn_ref, kvn_ref, wq_ref, wqr_ref, wk_ref, wvt_ref,
                     p_ref, ct_ref, st_ref, q_out, k_out, vt_out):
    x = cq_ref[...].astype(F32)
    xn = (x * lax.rsqrt(jnp.mean(x * x, axis=1, keepdims=True) + RMS_EPS) * qn_ref[...]).astype(BF16)
    q = (jnp.dot(xn, wq_ref[...], preferred_element_type=F32) * ct_ref[...]
         + jnp.dot(xn, wqr_ref[...], preferred_element_type=F32) * st_ref[...])
    q_out[...] = q.astype(q_out.dtype)
    c = ckv_ref[:, :KV_LORA].astype(F32)
    cn = (c * lax.rsqrt(jnp.mean(c * c, axis=1, keepdims=True) + RMS_EPS) * kvn_ref[...]).astype(BF16)
    k = (jnp.dot(cn, wk_ref[...], preferred_element_type=F32)
         + jnp.dot(kr_ref[...], p_ref[...], preferred_element_type=F32))
    k_out[...] = k.astype(k_out.dtype)
    _store_vt(vt_out, _tn_dot(wvt_ref[...], cn))


def _mla_prep(dcq, ckv, kr, qn, kvn, wq, wqr, wk, wvt, pmat, ct, st, l):
    b, t, _ = dcq.shape
    tm = 2048
    hw = N_HEADS * LANES
    row = lambda w: pl.BlockSpec((None, tm, w), lambda i, bb: (bb, i, 0))
    full = lambda a: pl.BlockSpec(a.shape, lambda i, bb: (0,) * a.ndim)
    tab = pl.BlockSpec((tm, hw), lambda i, bb: (i, 0))
    return pl.pallas_call(
        _mla_prep_kernel,
        out_shape=(jax.ShapeDtypeStruct((b, t, hw), BF16), jax.ShapeDtypeStruct((b, t, hw), BF16),
                   jax.ShapeDtypeStruct((b, t // CK, N_HEADS * VROWS, CK), BF16)),
        grid=(t // tm, b),
        in_specs=[row(Q_LORA), row(MXU_N), row(MXU_N), full(qn), full(kvn), _layer_spec(wq, l), _layer_spec(wqr, l),
                  _layer_spec(wk, l), _layer_spec(wvt, l), full(pmat), tab, tab],
        out_specs=(row(hw), row(hw),
                   pl.BlockSpec((None, tm // CK, N_HEADS * VROWS, CK), lambda i, bb: (bb, i, 0, 0))),
        compiler_params=_cparams(2),
        name="mla_prep",
    )(dcq, ckv, kr, qn, kvn, wq, wqr, wk, wvt, pmat, ct, st)


def _mla_kernel(q_ref, k_ref, vt_ref, o_ref, qt_ref, s_ref, mx_ref, m_ref, acc_ref, ot_ref):
    i = pl.program_id(1)
    hs = [slice(h * LANES, (h + 1) * LANES) for h in range(N_HEADS)]
    for h in range(N_HEADS):
        qt_ref[h] = q_ref[:, hs[h]].astype(F32).T.astype(BF16)

    def qk_all(c, lanes):
        start = pl.multiple_of(c * CK, CK)
        return [jnp.dot(k_ref[pl.ds(start, CK), hs[h]], qt_ref[h, :, lanes], preferred_element_type=F32)
                for h in range(N_HEADS)]

    _flash_loop(2 * i, qk_all, None,
                lambda c, h: vt_ref[c, h * VROWS:(h + 1) * VROWS, :],
                (s_ref, mx_ref, m_ref, acc_ref))
    for h in range(N_HEADS):
        ot_ref[h * HEAD_DIM:(h + 1) * HEAD_DIM, :] = _softmax_out(acc_ref.at[h])
    o_ref[...] = ot_ref[...].T.astype(o_ref.dtype)


def _mla(qm, km, vmt):
    b, t, hw = qm.shape
    kspec, vspec = _kv_specs(t, hw)
    return pl.pallas_call(
        _mla_kernel,
        out_shape=jax.ShapeDtypeStruct((b, t, BRANCH_W), BF16),
        grid=(b, t // TQ),
        in_specs=[pl.BlockSpec((None, TQ, hw), lambda bb, i: (bb, i, 0)), kspec, vspec],
        out_specs=pl.BlockSpec((None, TQ, BRANCH_W), lambda bb, i: (bb, i, 0)),
        scratch_shapes=[pltpu.VMEM((N_HEADS, LANES, TQ), BF16)] + _attn_scratch(N_HEADS),
        compiler_params=_cparams(2),
        name="mla",
    )(qm, km, vmt)


def _mem_kv_kernel(x_ref, w_ref, k_ref, vt_ref):
    x = x_ref[...].astype(BF16)
    k_ref[...] = jnp.dot(x, w_ref[:, :BRANCH_W], preferred_element_type=F32).astype(k_ref.dtype)
    _store_vt(vt_ref, _tn_dot(w_ref[:, BRANCH_W:], x))


def _mem_kv(mem, w, l):
    b, m, d = mem.shape
    assert m % CK == 0
    return pl.pallas_call(
        _mem_kv_kernel,
        out_shape=(jax.ShapeDtypeStruct((b, m, BRANCH_W), BF16),
                   jax.ShapeDtypeStruct((b, m // CK, N_HEADS * VROWS, CK), BF16)),
        grid=(b,),
        in_specs=[pl.BlockSpec((None, m, d), lambda bb: (bb, 0, 0)), _layer_spec(w, l)],
        out_specs=(pl.BlockSpec((None, m, BRANCH_W), lambda bb: (bb, 0, 0)),
                   pl.BlockSpec((None, m // CK, N_HEADS * VROWS, CK), lambda bb: (bb, 0, 0, 0))),
        compiler_params=_cparams(1),
        name="mem_kv",
    )(mem, w)


def _mem_kernel(q_ref, k_ref, vt_ref, o_ref, qt_ref, ot_ref):
    _masked_qt(q_ref[...].astype(F32) * (HEAD_DIM ** -0.5 * LOG2E), 6, N_HEADS, qt_ref)
    s_all = [jnp.dot(_half(k_ref[...], h, 6), qt_ref[h], preferred_element_type=F32) for h in range(N_HEADS)]
    for h in range(N_HEADS):
        s_t = s_all[h]
        p = jnp.exp2(s_t - jnp.max(s_t, axis=0, keepdims=True)).astype(BF16)
        acc = jnp.dot(vt_ref[0, h * VROWS:(h + 1) * VROWS, :], p, preferred_element_type=F32)
        ot_ref[h * HEAD_DIM:(h + 1) * HEAD_DIM, :] = acc[:HEAD_DIM, :] / acc[HEAD_DIM:HEAD_DIM + 1, :]
    o_ref[...] = ot_ref[...].T.astype(o_ref.dtype)


def _mem_attn(eq, mk, mvt):
    b, t, w = eq.shape
    m = mk.shape[1]
    assert m == CK
    tm = 2048
    return pl.pallas_call(
        _mem_kernel,
        out_shape=jax.ShapeDtypeStruct((b, t, w), BF16),
        grid=(b, t // tm),
        in_specs=[pl.BlockSpec((None, tm, w), lambda bb, i: (bb, i, 0)),
                  pl.BlockSpec((None, m, w), lambda bb, i: (bb, 0, 0)),
                  pl.BlockSpec((None,) + mvt.shape[1:], lambda bb, i: (bb, 0, 0, 0))],
        out_specs=pl.BlockSpec((None, tm, w), lambda bb, i: (bb, i, 0)),
        scratch_shapes=[pltpu.VMEM((N_HEADS, LANES, tm), BF16), pltpu.VMEM((BRANCH_W, tm), F32)],
        compiler_params=_cparams(2),
        name="mem_attn",
    )(eq, mk, mvt)


def _final_kernel(h_ref, hb_ref, oa_ref, ob_ref, oc_ref, od_ref, oe_ref, z_ref,
                  wg_ref, wb_ref, wo_ref, g_ref, b_ref, h_out, hb_out, acc_ref, *, alpha):
    d = h_ref.shape[1]
    n_blk = h_ref.shape[0] // MXU_N
    for n, o_ref in enumerate((oa_ref, ob_ref, oc_ref, od_ref, oe_ref)):
        for r in range(n_blk):
            rows = slice(r * MXU_N, (r + 1) * MXU_N)
            z = z_ref[rows, n * BRANCH_W:(n + 1) * BRANCH_W].astype(F32)
            y = o_ref[rows, :].astype(F32) * (z / (1.0 + jnp.exp(-z)))
            u = jnp.dot(y.astype(BF16), wb_ref[n], preferred_element_type=F32)
            g = jnp.dot(hb_ref[rows, :], wg_ref[:, n * d:(n + 1) * d], preferred_element_type=F32)
            t = u / (1.0 + jnp.exp(-g))
            acc_ref[rows, :] = t if n == 0 else acc_ref[rows, :] + t
    for r in range(n_blk):
        rows = slice(r * MXU_N, (r + 1) * MXU_N)
        out = jnp.dot(acc_ref[rows, :].astype(BF16), wo_ref[...], preferred_element_type=F32)
        x = alpha * h_ref[rows, :] + out
        mu = jnp.mean(x, axis=1, keepdims=True)
        xc = x - mu
        var = jnp.mean(xc * xc, axis=1, keepdims=True)
        y = xc * lax.rsqrt(var + LN_EPS) * g_ref[...] + b_ref[...]
        h_out[rows, :] = y
        hb_out[rows, :] = y.astype(BF16)


def _final(h, hb, os5, z, wg, wb, wo, ln_g, ln_b, alpha, l):
    n, d = h.shape
    tm = 1024
    row = lambda w: pl.BlockSpec((tm, w), lambda i: (i, 0))
    full = lambda a: pl.BlockSpec(a.shape, lambda i: (0,) * a.ndim)
    return pl.pallas_call(
        functools.partial(_final_kernel, alpha=alpha),
        out_shape=(jax.ShapeDtypeStruct((n, d), F32), jax.ShapeDtypeStruct((n, d), BF16)),
        grid=(n // tm,),
        in_specs=[row(d), row(d)] + [row(BRANCH_W)] * N_BRANCH + [row(N_BRANCH * BRANCH_W),
                  _layer_spec(wg, l), _layer_spec(wb, l), _layer_spec(wo, l), full(ln_g), full(ln_b)],
        out_specs=(row(d), row(d)),
        scratch_shapes=[pltpu.VMEM((tm, d), F32)],
        compiler_params=_cparams(1),
        name="merge_out_ln",
    )(h, hb, *os5, z, wg, wb, wo, ln_g, ln_b)


ROPE_GROUPS = (("a_q", N_HEADS, HEAD_DIM, ROT_64), ("a_k", N_HEADS, HEAD_DIM, ROT_64),
               ("i_q", IDX_HEADS, IDX_DIM, ROT_32), ("i_k", 1, MXU_N, ROT_32),
               ("b_q", N_HEADS, HEAD_DIM, ROT_64), ("b_k", N_HEADS, HEAD_DIM, ROT_64),
               ("c_q", 2 * N_HEADS, DIFF_DIM, ROT_32), ("c_k", 2 * N_HEADS, DIFF_DIM, ROT_32),
               ("d_kr", 1, MXU_N, MLA_ROPE))
PLAIN_COLS = ("d_cq", "d_ckv", "e_q") + tuple(("z", j) for j in range(N_BRANCH))
VALUE_COLS = ("a_v", "b_v", "c_v")
GATE_WINDOW = 4 * MXU_N
GATE_COLS = tuple(("g", j) for j in range(0, OFF["g"][1] // MXU_N, GATE_WINDOW // MXU_N))


def _window_start(col):
    name, j = col if isinstance(col, tuple) else (col, 0)
    return OFF[name][0] + j * MXU_N


def _weight_prep_kernel(offs_ref, wt_ref, o_ref):
    o_ref[...] = wt_ref[...].T.astype(o_ref.dtype)


def _weight_windows(wt, cols, name, width=MXU_N):
    depth, n, d = wt.shape
    starts = [_window_start(c) for c in cols]
    assert all(st % SUBLANES == 0 and st + width <= n for st in starts)
    grid_spec = pltpu.PrefetchScalarGridSpec(
        num_scalar_prefetch=1,
        grid=(depth, len(cols)),
        in_specs=[pl.BlockSpec((None, pl.Element(width), pl.Element(d)),
                               lambda l, j, offs: (l, pl.multiple_of(offs[j], SUBLANES), 0))],
        out_specs=pl.BlockSpec((None, d, width), lambda l, j, offs: (l, 0, j)),
    )
    return pl.pallas_call(
        _weight_prep_kernel,
        out_shape=jax.ShapeDtypeStruct((depth, d, width * len(cols)), BF16),
        grid_spec=grid_spec,
        compiler_params=_cparams(2),
        name=name,
    )(jnp.asarray(np.asarray(starts, np.int32)), wt)


def _weight_prep(w_in):
    wt = jnp.swapaxes(w_in, 1, 2)
    return (_weight_windows(wt, PLAIN_COLS, "wprep_plain"), _weight_windows(wt, VALUE_COLS, "wprep_value"),
            _weight_windows(wt, [name for name, *_ in ROPE_GROUPS], "wprep_rope"),
            _weight_windows(wt, GATE_COLS, "wprep_gate", GATE_WINDOW))


def _rope_tables(seq, rot_dim):
    pos = jnp.arange(seq, dtype=F32)
    inv = ROPE_THETA ** (-jnp.arange(0, rot_dim, 2, dtype=F32) / rot_dim)
    ang = pos[:, None] * inv[None, :]
    return jnp.cos(ang), jnp.sin(ang)


def _rope_cs(t, nh, hd, r):
    cos, sin = _rope_tables(t, r)
    c = jnp.concatenate([cos, cos, jnp.ones((t, hd - r), F32)], axis=1)
    s = jnp.concatenate([-sin, sin, jnp.zeros((t, hd - r), F32)], axis=1)
    return jnp.tile(c, (1, nh)), jnp.tile(s, (1, nh))


def kernel(x, mem, ln0_g, ln0_b, w_in, mla_q_norm, w_uq, mla_kv_norm, w_ukv, diff_lam, diff_norm,
           w_mem_kv, w_branch, w_out, ln_g, ln_b):
    b, t, d = x.shape
    depth = w_in.shape[0]
    alpha = (2 * depth) ** 0.25
    assert t % 512 == 0 and d == 1024

    w_plain, w_vt, w_rope, wg = _weight_prep(w_in)
    plain_widths = (BRANCH_W,) * 3 + (N_BRANCH * BRANCH_W,)
    rope_heads = tuple((hd, r // 2) for _, _, hd, r in ROPE_GROUPS)
    patterns = sorted(set((nh, hd, r) for _, nh, hd, r in ROPE_GROUPS))
    rope_tables = tuple(patterns.index((nh, hd, r)) for _, nh, hd, r in ROPE_GROUPS)
    cs = [_rope_cs(t, nh, hd, r) for nh, hd, r in patterns]
    ctab = jnp.stack([c for c, _ in cs])
    stab = jnp.stack([s for _, s in cs])

    uq = w_uq.reshape(depth, Q_LORA, N_HEADS, MLA_NOPE + MLA_ROPE)
    qn_w, qr_w = uq[..., :MLA_NOPE], uq[..., MLA_NOPE:]
    pad32 = jnp.zeros((depth, Q_LORA, N_HEADS, LANES - MLA_NOPE - MLA_ROPE), w_uq.dtype)
    hw = N_HEADS * LANES
    wq = jnp.concatenate([qn_w, qr_w, pad32], axis=-1).reshape(depth, Q_LORA, hw).astype(BF16)
    half = MLA_ROPE // 2
    wq_rot = jnp.concatenate([jnp.zeros_like(qn_w), -qr_w[..., half:], qr_w[..., :half], pad32],
                             axis=-1).reshape(depth, Q_LORA, hw).astype(BF16)
    cos_m, sin_m = _rope_tables(t, MLA_ROPE)
    one = lambda n: jnp.ones((t, n), F32)
    zer = lambda n: jnp.zeros((t, n), F32)
    qs = (MLA_NOPE + MLA_ROPE) ** -0.5 * LOG2E
    ct_q = qs * jnp.tile(jnp.concatenate([one(MLA_NOPE), cos_m, cos_m, one(LANES - MLA_NOPE - MLA_ROPE)], axis=1), (1, N_HEADS))
    st_q = qs * jnp.tile(jnp.concatenate([zer(MLA_NOPE), sin_m, sin_m, zer(LANES - MLA_NOPE - MLA_ROPE)], axis=1), (1, N_HEADS))
    ukv = w_ukv.reshape(depth, KV_LORA, N_HEADS, MLA_NOPE + MLA_V)
    wk = jnp.concatenate([ukv[..., :MLA_NOPE], jnp.zeros((depth, KV_LORA, N_HEADS, LANES - MLA_NOPE), w_ukv.dtype)],
                         axis=-1).reshape(depth, KV_LORA, hw).astype(BF16)
    wvt = ukv[..., MLA_NOPE:].reshape(depth, KV_LORA, N_HEADS * MLA_V).astype(BF16)
    place = np.zeros((MXU_N, hw), np.float32)
    for hh in range(N_HEADS):
        for j in range(MLA_ROPE):
            place[j, hh * LANES + MLA_NOPE + j] = 1.0
    place = jnp.asarray(place, BF16)

    wb = w_branch.astype(BF16)
    wo = w_out.astype(BF16)
    wmem = w_mem_kv.astype(BF16)
    norm_t = jnp.broadcast_to(diff_norm.astype(F32)[:, :, None], (depth, HEAD_DIM, TQ))

    h, hb = _layer_norm0(x.reshape(b * t, d), ln0_g, ln0_b)
    for l in range(depth):
        hb3 = hb.reshape(b, t, d)
        avt, bvt, cvt, dcq, ckv_iw, eq, z = _proj_plain(hb3, w_plain, w_vt, plain_widths, l)
        aq, ak, iq, ik, bq, bk, cq, ck, kr = _proj_rope(hb3, w_rope, ctab, stab, rope_heads, rope_tables, l)

        o_a = _dsa(aq, ak, avt, iq, ik, ik)
        o_b = _moba(bq, bk, bvt, _kbar(bk))
        lam_init = 0.8 - 0.6 * math.exp(-0.3 * l)
        misc = jnp.full((SUBLANES, LANES), lam_init, F32)
        o_c = _diff(cq, ck, cvt, diff_lam[l].astype(F32), norm_t[l], misc)
        qm, km, vmt = _mla_prep(dcq, ckv_iw, kr, mla_q_norm[l].reshape(1, Q_LORA), mla_kv_norm[l].reshape(1, KV_LORA),
                                wq, wq_rot, wk, wvt, place, ct_q, st_q, l)
        o_d = _mla(qm, km, vmt)
        o_e = _mem_attn(eq, *_mem_kv(mem, wmem, l))

        os5 = [o.reshape(b * t, BRANCH_W) for o in (o_a, o_b, o_c, o_d, o_e)]
        h, hb = _final(h, hb, os5, z.reshape(b * t, N_BRANCH * BRANCH_W), wg, wb, wo,
                       ln_g[l].reshape(1, d), ln_b[l].reshape(1, d), alpha, l)
    return h.reshape(b, t, d)
```

```python
import functools
import math

import numpy as np
import jax
import jax.numpy as jnp
from jax import lax
from jax.experimental import pallas as pl
from jax.experimental.pallas import tpu as pltpu

F32 = jnp.float32
BF16 = jnp.bfloat16
I32 = jnp.int32
I16 = jnp.int16

N_HEADS = 4
HEAD_DIM = 64
BRANCH_W = N_HEADS * HEAD_DIM
N_BRANCH = 5
ROPE_THETA = 500000.0
ROT_64 = 16
ROT_32 = 8
IDX_HEADS = 8
IDX_DIM = 32
TOPK_MAX = 256
MOBA_BLOCK = 256
MOBA_TOPK = 3
DIFF_DIM = 32
Q_LORA = 256
KV_LORA = 128
MLA_NOPE = 64
MLA_ROPE = 32
MLA_V = 64
LN_EPS = 1e-5
RMS_EPS = 1e-6

IN_LAYOUT = (
    ("a_q", BRANCH_W), ("a_k", BRANCH_W), ("a_v", BRANCH_W),
    ("i_q", IDX_HEADS * IDX_DIM), ("i_k", IDX_DIM), ("i_w", IDX_HEADS),
    ("b_q", BRANCH_W), ("b_k", BRANCH_W), ("b_v", BRANCH_W),
    ("c_q", BRANCH_W), ("c_k", BRANCH_W), ("c_v", BRANCH_W),
    ("d_cq", Q_LORA), ("d_ckv", KV_LORA), ("d_kr", MLA_ROPE),
    ("e_q", BRANCH_W),
    ("z", N_BRANCH * BRANCH_W),
    ("g", N_BRANCH * 1024),
)

SUBLANES = 8
LANES = 128
MXU_N = 256
TQ = 512
CK = 256
VROWS = HEAD_DIM + 16
FLASH_UNROLL = 4
NEG = -1e30
LOG2E = math.log2(math.e)
INT_MIN = np.int32(-2 ** 31)
HALF16 = 1 << 15
VMEM_LIMIT = 56 * 1024 * 1024


def _offsets():
    off, out = 0, {}
    for name, size in IN_LAYOUT:
        out[name] = (off, size)
        off += size
    return out


OFF = _offsets()


def _nt_dot(a, b):
    return lax.dot_general(a, b, (((1,), (1,)), ((), ())), preferred_element_type=F32)


def _tn_dot(w, x):
    return lax.dot_general(w, x, (((0,), (1,)), ((), ())), preferred_element_type=F32)


def _fold_rows(w, rows=SUBLANES):
    xs = [w[r:r + rows, :] for r in range(0, w.shape[0], rows)]
    while len(xs) > 1:
        xs = [xs[j] + xs[j + 1] for j in range(0, len(xs) - 1, 2)] + ([xs[-1]] if len(xs) % 2 else [])
    return xs[0]


def _masked_qt(q, shift, n, qt_ref):
    qt = q.T
    dim = lax.broadcasted_iota(I32, (LANES, qt.shape[1]), 0)
    for j in range(n):
        half = (j << shift) // LANES
        rows = qt[half * LANES:(half + 1) * LANES, :]
        qt_ref[j] = jnp.where(((dim + half * LANES) >> shift) == j, rows, 0.0).astype(BF16)


def _half(kc, j, shift):
    half = (j << shift) // LANES
    return kc[:, half * LANES:(half + 1) * LANES]


def _cparams(n_axes):
    return pltpu.CompilerParams(dimension_semantics=("arbitrary",) * n_axes,
                                vmem_limit_bytes=VMEM_LIMIT)


def _layer_spec(a, l):
    return pl.BlockSpec((None,) + a.shape[1:], lambda *_: (l,) + (0,) * (a.ndim - 1))


def _softmax_step(s_t, m_tile, vt_h, m_ref, acc_ref):
    m_old = m_ref[...]
    m_new = jnp.maximum(m_old, m_tile)
    alpha = jnp.exp2(m_old - m_new)
    p = jnp.exp2(s_t - m_new)
    acc_ref[...] = alpha * acc_ref[...] + jnp.dot(vt_h, p.astype(BF16), preferred_element_type=F32)
    m_ref[...] = m_new


def _softmax_init(m_ref, acc_ref):
    m_ref[...] = jnp.full(m_ref.shape, NEG, F32)
    acc_ref[...] = jnp.zeros(acc_ref.shape, F32)


def _softmax_out(acc_ref):
    return acc_ref[:HEAD_DIM, :] / acc_ref[HEAD_DIM:HEAD_DIM + 1, :]


def _store_vt(o_ref, vt):
    ones = jnp.ones((VROWS - HEAD_DIM, CK), o_ref.dtype)
    for j in range(o_ref.shape[0]):
        for h in range(N_HEADS):
            o_ref[j, h * VROWS:h * VROWS + HEAD_DIM, :] = (
                vt[h * HEAD_DIM:(h + 1) * HEAD_DIM, j * CK:(j + 1) * CK].astype(o_ref.dtype))
            o_ref[j, h * VROWS + HEAD_DIM:(h + 1) * VROWS, :] = ones


def _flash_loop(n_full, qk_all, mask, vt_rows, state, prep=None, causal_tail=True):
    s_ref, mx_ref, m_ref, acc_ref = state
    n_state = m_ref.shape[0]
    for j in range(n_state):
        _softmax_init(m_ref.at[j], acc_ref.at[j])

    def lanes_of(d):
        return slice(CK, TQ) if d == 1 else slice(None)

    def park(c, slot, d=None):
        lanes = lanes_of(d)
        ctx = c if prep is None else prep(c, lanes)
        for j, s in enumerate(qk_all(c, lanes)):
            if mask is not None:
                s = mask(ctx, j, s, lanes)
            if d is not None and causal_tail:
                s = jnp.where(_causal(d), s, NEG)
            s_ref[slot, j, :, lanes] = s
            mx_ref[slot, j, :, lanes] = jnp.max(s, axis=0, keepdims=True)

    def consume(c, slot, d=None):
        lanes = lanes_of(d)
        for j in range(n_state):
            _softmax_step(s_ref[slot, j, :, lanes], mx_ref[slot, j, :, lanes], vt_rows(c, j),
                          m_ref.at[j, :, lanes], acc_ref.at[j, :, lanes])

    def pair(c):
        park(c + 1, 1)
        consume(c, 0)
        park(c + 2, 0)
        consume(c + 1, 1)

    def body(g, carry):
        for u in range(0, FLASH_UNROLL, 2):
            pair(FLASH_UNROLL * g + u)
        return carry

    @pl.when(n_full == 0)
    def _():
        park(0, 0, d=0)
        park(1, 1, d=1)
        consume(0, 0)
        consume(1, 1, d=1)

    @pl.when(n_full > 0)
    def _():
        park(0, 0)
        n_loop = n_full - 2
        n_group = lax.shift_right_logical(n_loop, FLASH_UNROLL.bit_length() - 1)
        lax.fori_loop(0, n_group, body, 0)
        c0 = FLASH_UNROLL * n_group
        for u in range(FLASH_UNROLL // 2 - 1):
            @pl.when(n_loop - c0 >= 2 * (u + 1))
            def _(u=u):
                pair(c0 + 2 * u)
        c = n_loop
        park(c + 1, 1)
        consume(c, 0)
        park(c + 2, 0, d=0)
        consume(c + 1, 1)
        park(c + 3, 1, d=1)
        consume(c + 2, 0)
        consume(c + 3, 1, d=1)


def _causal(d):
    shape = (CK, TQ - d * CK)
    return lax.broadcasted_iota(I32, shape, 0) <= lax.broadcasted_iota(I32, shape, 1)


def _attn_scratch(n_state):
    return [pltpu.VMEM((2, n_state, CK, TQ), F32), pltpu.VMEM((2, n_state, 1, TQ), F32),
            pltpu.VMEM((n_state, 1, TQ), F32), pltpu.VMEM((n_state, VROWS, TQ), F32),
            pltpu.VMEM((BRANCH_W, TQ), F32)]


def _kv_specs(t, w):
    kspec = pl.BlockSpec((None, t, w), lambda bb, i: (bb, 0, 0))
    vspec = pl.BlockSpec((None, t // CK, N_HEADS * VROWS, CK), lambda bb, i: (bb, 0, 0, 0))
    return kspec, vspec


def _ln_kernel(x_ref, g_ref, b_ref, h_ref, hb_ref):
    x = x_ref[...]
    mu = jnp.mean(x, axis=1, keepdims=True)
    xc = x - mu
    var = jnp.mean(xc * xc, axis=1, keepdims=True)
    y = xc * lax.rsqrt(var + LN_EPS) * g_ref[...] + b_ref[...]
    h_ref[...] = y
    hb_ref[...] = y.astype(BF16)


def _layer_norm0(x2, g, b):
    n, d = x2.shape
    tm = 2048
    row = pl.BlockSpec((tm, d), lambda i: (i, 0))
    vec = pl.BlockSpec((1, d), lambda i: (0, 0))
    return pl.pallas_call(
        _ln_kernel,
        out_shape=(jax.ShapeDtypeStruct((n, d), F32), jax.ShapeDtypeStruct((n, d), BF16)),
        grid=(n // tm,),
        in_specs=[row, vec, vec],
        out_specs=(row, row),
        compiler_params=_cparams(1),
        name="ln0",
    )(x2, g.reshape(1, d), b.reshape(1, d))


def _proj_plain_kernel(x_ref, w_ref, wt_ref, *out_refs, n_t):
    for g, o_ref in enumerate(out_refs[:n_t]):
        _store_vt(o_ref, _tn_dot(wt_ref[:, g * BRANCH_W:(g + 1) * BRANCH_W], x_ref[...]))
    off = 0
    for o_ref in out_refs[n_t:]:
        wd = o_ref.shape[-1]
        for j in range(0, wd, MXU_N):
            acc = jnp.dot(x_ref[...], w_ref[:, off + j:off + j + MXU_N], preferred_element_type=F32)
            o_ref[:, j:j + MXU_N] = acc.astype(o_ref.dtype)
        off += wd


def _proj_plain(hb3, w, wt, widths, l):
    b, t, d = hb3.shape
    tm = 2048
    n_t = wt.shape[-1] // BRANCH_W
    shapes = [jax.ShapeDtypeStruct((b, t // CK, N_HEADS * VROWS, CK), BF16)] * n_t
    specs = [pl.BlockSpec((None, tm // CK, N_HEADS * VROWS, CK), lambda i, bb: (bb, i, 0, 0))] * n_t
    shapes += [jax.ShapeDtypeStruct((b, t, wd), BF16) for wd in widths]
    specs += [pl.BlockSpec((None, tm, wd), lambda i, bb: (bb, i, 0)) for wd in widths]
    return pl.pallas_call(
        functools.partial(_proj_plain_kernel, n_t=n_t),
        out_shape=tuple(shapes),
        grid=(t // tm, b),
        in_specs=[pl.BlockSpec((None, tm, d), lambda i, bb: (bb, i, 0)),
                  _layer_spec(w, l), _layer_spec(wt, l)],
        out_specs=tuple(specs),
        compiler_params=_cparams(2),
        name="proj_plain",
    )(hb3, w, wt)


def _proj_rope_kernel(x_ref, w_ref, c_ref, s_ref, *out_refs, heads, tables):
    lane = lax.broadcasted_iota(I32, (x_ref.shape[0], MXU_N), 1)
    for g, o_ref in enumerate(out_refs):
        hd, half = heads[g]
        sl = slice(g * MXU_N, (g + 1) * MXU_N)
        acc = jnp.dot(x_ref[...], w_ref[:, sl], preferred_element_type=F32)
        partner = jnp.where((lane & (hd - 1)) < half,
                            pltpu.roll(acc, MXU_N - half, 1), pltpu.roll(acc, half, 1))
        o_ref[...] = (acc * c_ref[tables[g]] + partner * s_ref[tables[g]]).astype(o_ref.dtype)


def _proj_rope(hb3, w, ctab, stab, heads, tables, l):
    b, t, d = hb3.shape
    tm = 1024
    assert w.shape[-1] == MXU_N * len(heads)
    tspec = pl.BlockSpec((ctab.shape[0], tm, MXU_N), lambda i, bb: (0, i, 0))
    ospec = pl.BlockSpec((None, tm, MXU_N), lambda i, bb: (bb, i, 0))
    return pl.pallas_call(
        functools.partial(_proj_rope_kernel, heads=heads, tables=tables),
        out_shape=(jax.ShapeDtypeStruct((b, t, MXU_N), BF16),) * len(heads),
        grid=(t // tm, b),
        in_specs=[pl.BlockSpec((None, tm, d), lambda i, bb: (bb, i, 0)),
                  _layer_spec(w, l), tspec, tspec],
        out_specs=(ospec,) * len(heads),
        compiler_params=_cparams(2),
        name="proj_rope",
    )(hb3, w, ctab, stab)


def _dsa_kernel(aq_ref, ak_ref, avt_ref, iq_ref, ik_ref, iw_ref, pick_ref, tri_ref, o_ref,
                keys_ref, hi_ref, lo_ref, bk_ref, top_ref, sel_ref, iqt_ref, aqt_ref, wt_ref, thr_ref, s_ref, mx_ref, m_ref, acc_ref, ot_ref,
                *, topk, idx_scale):
    i = pl.program_id(1)
    n_full = 2 * i

    iqt = iq_ref[...].astype(F32).T
    for hh in range(IDX_HEADS):
        iqt_ref[hh] = iqt[hh * IDX_DIM:(hh + 1) * IDX_DIM, :].astype(BF16)
    _masked_qt(aq_ref[...].astype(F32) * (HEAD_DIM ** -0.5 * LOG2E), 6, N_HEADS, aqt_ref)
    wt_ref[...] = _nt_dot(pick_ref[...], iw_ref[...]) * idx_scale

    def lanes_of(d):
        return slice(CK, TQ) if d == 1 else slice(None)

    def logits(c, d):
        kc = ik_ref[pl.ds(pl.multiple_of(c * CK, CK), CK), :]
        return [jnp.dot(kc[:, :IDX_DIM], iqt_ref[hh, :, lanes_of(d)], preferred_element_type=F32)
                for hh in range(IDX_HEADS)]

    def put_keys(c, key, lanes):
        keys_ref[c, :, lanes] = key
        hi_ref[c, :, lanes] = (key >> 16).astype(I16)
        lo_ref[c, :, lanes] = ((key & 0xFFFF) - HALF16).astype(I16)

    def score_chunk(c, lg, d):
        lanes = lanes_of(d)
        sc = jnp.zeros(lg[0].shape, F32)
        for hh in range(IDX_HEADS):
            sc = sc + jnp.maximum(lg[hh], 0.0) * wt_ref[hh:hh + 1, lanes]
        bits = pltpu.bitcast(sc, I32)
        key = jnp.where(bits < 0, INT_MIN - bits, bits)
        put_keys(c, key if d is None else jnp.where(_causal(d), key, INT_MIN), lanes)
        if d == 1:
            put_keys(c, jnp.full((CK, CK), INT_MIN, I32), slice(0, CK))

    def score_pair(c, d0, d1):
        lg0, lg1 = logits(c, d0), logits(c + 1, d1)
        score_chunk(c, lg0, d0)
        score_chunk(c + 1, lg1, d1)

    def score_body(p, carry):
        score_pair(2 * p, None, None)
        return carry

    lax.fori_loop(0, i, score_body, 0)
    score_pair(n_full, 0, 1)

    def pair_loop(body, init, last=None):
        def pair(p, carry):
            return body(2 * p + 1, body(2 * p, carry))
        carry = body(n_full, lax.fori_loop(0, i, pair, init))
        return (last or body)(n_full + 1, carry)

    def count16(pred, also=None):
        def hits(c, lanes):
            hit = jnp.where(pred(c, lanes), jnp.int16(1), jnp.int16(0))
            if also is not None:
                hit = jnp.where(also(c, lanes), hit, jnp.int16(0))
            return _fold_rows(hit, 2 * SUBLANES)

        def body(c, part):
            return part + hits(c, slice(None))

        def last(c, part):
            return jnp.concatenate([part[:, :CK], part[:, CK:] + hits(c, slice(CK, TQ))], axis=1)

        part = pair_loop(body, jnp.zeros((2 * SUBLANES, TQ), I16), last)
        return jnp.sum(part.astype(F32), axis=0, keepdims=True)

    def search16(count_ge, need):
        def bit_body(bi, t_u):
            c_u = t_u | jnp.left_shift(jnp.int32(1), 15 - bi)
            cnt = count_ge((c_u - HALF16).astype(I16))
            return jnp.where(cnt >= need, c_u, t_u)
        return lax.fori_loop(0, 16, bit_body, jnp.zeros((1, TQ), I32))

    hi_u = search16(lambda ck: count16(lambda c, lanes: hi_ref[c, :, lanes] >= ck[:, lanes]),
                    float(topk))
    thr_hi = (hi_u - HALF16).astype(I16)
    n_above = count16(lambda c, lanes: hi_ref[c, :, lanes] > thr_hi[:, lanes])
    need_lo = float(topk) - n_above

    fill = jnp.int16(-HALF16)
    top_ref[...] = jnp.full(top_ref.shape, fill, I16)

    def bucket(c):
        return jnp.where(hi_ref[c] == thr_hi, lo_ref[c], fill)

    def top_body(c, carry):
        x = bucket(c)
        for r in range(4):
            m = top_ref[r]
            swap = x > m
            top_ref[r] = jnp.where(swap, x, m)
            x = jnp.where(swap, m, x)
        return carry

    many = i > 1

    @pl.when(many)
    def _():
        pair_loop(top_body, 0)

    fourth = jnp.max(_fold_rows(jnp.where(top_ref[3] > fill, jnp.int16(1), jnp.int16(0)),
                                2 * SUBLANES).astype(F32)) > 0.5
    use_top = jnp.logical_and(many, jnp.logical_not(fourth))

    def count_top(pred):
        part = jnp.zeros((2 * SUBLANES, TQ), I16)
        for r in range(3):
            part = part + _fold_rows(jnp.where(pred(top_ref[r]), jnp.int16(1), jnp.int16(0)), 2 * SUBLANES)
        return jnp.sum(part.astype(F32), axis=0, keepdims=True)

    def lo_select(count_ge, count_gt):
        lo_u = search16(count_ge, need_lo)
        sel_ref[0:1, :] = lo_u.astype(F32)
        sel_ref[1:2, :] = count_gt((lo_u - HALF16).astype(I16))

    @pl.when(use_top)
    def _():
        lo_select(lambda ck: count_top(lambda x: x >= ck), lambda t: count_top(lambda x: x > t))

    @pl.when(jnp.logical_not(use_top))
    def _():
        def bucket_body(c, carry):
            bk_ref[c] = bucket(c)
            return carry

        pair_loop(bucket_body, 0)
        lo_select(lambda ck: count16(lambda c, lanes: bk_ref[c, :, lanes] >= ck[:, lanes]),
                  lambda t: count16(lambda c, lanes: bk_ref[c, :, lanes] > t[:, lanes]))

    lo_u = sel_ref[0:1, :].astype(I32)
    thr_lo = (lo_u - HALF16).astype(I16)
    thr = ((hi_u - HALF16) << 16) | lo_u

    n_gt = n_above + sel_ref[1:2, :]
    n_eq = count16(lambda c, lanes: lo_ref[c, :, lanes] == thr_lo[:, lanes],
                   also=lambda c, lanes: hi_ref[c, :, lanes] == thr_hi[:, lanes])
    need = float(topk) - n_gt
    amb = jnp.logical_and(n_eq > need, thr > INT_MIN)
    any_amb = jnp.max(jnp.where(amb, 1.0, 0.0)) > 0.5

    @pl.when(any_amb)
    def _():
        def drop_body(c, seen):
            k = keys_ref[c]
            eq = k == thr
            eqf = jnp.where(eq, 1.0, 0.0)
            rank = jnp.dot(tri_ref[...], eqf.astype(BF16), preferred_element_type=F32) + seen
            drop = jnp.logical_and(jnp.logical_and(eq, rank > need), amb)
            keys_ref[c] = jnp.where(drop, INT_MIN, k)
            return seen + jnp.sum(eqf, axis=0, keepdims=True)

        pair_loop(drop_body, jnp.zeros((1, TQ), F32))

    thr_ref[...] = jnp.maximum(thr, INT_MIN + 1)

    def qk_all(c, lanes):
        kc = ak_ref[pl.ds(pl.multiple_of(c * CK, CK), CK), :]
        return [jnp.dot(_half(kc, h, 6), aqt_ref[h, :, lanes], preferred_element_type=F32) for h in range(N_HEADS)]

    _flash_loop(n_full, qk_all,
                lambda bias, h, s, lanes: s + bias,
                lambda c, h: avt_ref[c, h * VROWS:(h + 1) * VROWS, :],
                (s_ref, mx_ref, m_ref, acc_ref),
                prep=lambda c, lanes: jnp.where(keys_ref[c, :, lanes] >= thr_ref[:, lanes], 0.0, NEG),
                causal_tail=False)
    for h in range(N_HEADS):
        ot_ref[h * HEAD_DIM:(h + 1) * HEAD_DIM, :] = _softmax_out(acc_ref.at[h])
    o_ref[...] = ot_ref[...].T.astype(o_ref.dtype)


def _dsa(aq, ak, avt, iq, ik, iw):
    b, t, _ = aq.shape
    topk = min(TOPK_MAX, t // 4)
    qspec = pl.BlockSpec((None, TQ, BRANCH_W), lambda bb, i: (bb, i, 0))
    kspec, vspec = _kv_specs(t, BRANCH_W)
    pick = np.zeros((2 * SUBLANES, MXU_N), np.float32)
    for hh in range(IDX_HEADS):
        pick[hh, IDX_DIM + hh] = 1.0
    pick = jnp.asarray(pick, BF16)
    tri = jnp.asarray(np.tril(np.ones((CK, CK), np.float32)), BF16)
    kern = functools.partial(_dsa_kernel, topk=topk, idx_scale=(IDX_HEADS * IDX_DIM) ** -0.5)
    return pl.pallas_call(
        kern,
        out_shape=jax.ShapeDtypeStruct((b, t, BRANCH_W), BF16),
        grid=(b, t // TQ),
        in_specs=[qspec, kspec, vspec, qspec, kspec, qspec,
                  pl.BlockSpec(pick.shape, lambda bb, i: (0, 0)), pl.BlockSpec(tri.shape, lambda bb, i: (0, 0))],
        out_specs=qspec,
        scratch_shapes=[
            pltpu.VMEM((t // CK, CK, TQ), I32),
            pltpu.VMEM((t // CK, CK, TQ), I16),
            pltpu.VMEM((t // CK, CK, TQ), I16),
            pltpu.VMEM((t // CK, CK, TQ), I16),
            pltpu.VMEM((4, CK, TQ), I16),
            pltpu.VMEM((SUBLANES, TQ), F32),
            pltpu.VMEM((IDX_HEADS, IDX_DIM, TQ), BF16),
            pltpu.VMEM((N_HEADS, LANES, TQ), BF16),
            pltpu.VMEM((2 * SUBLANES, TQ), F32),
            pltpu.VMEM((1, TQ), I32),
        ] + _attn_scratch(N_HEADS),
        compiler_params=_cparams(2),
        name="dsa",
    )(aq, ak, avt, iq, ik, iw, pick, tri)


def _kbar_kernel(k_ref, o_ref):
    o_ref[...] = jnp.zeros(o_ref.shape, o_ref.dtype)
    nb = k_ref.shape[0] // MOBA_BLOCK
    for n in range(nb):
        blk = k_ref[n * MOBA_BLOCK:(n + 1) * MOBA_BLOCK, :].astype(F32)
        o_ref[n:n + 1, :] = jnp.mean(blk, axis=0, keepdims=True).astype(o_ref.dtype)


def _kbar(bk):
    b, t, w = bk.shape
    nbp = max(2 * SUBLANES, t // MOBA_BLOCK)
    return pl.pallas_call(
        _kbar_kernel,
        out_shape=jax.ShapeDtypeStruct((b, nbp, w), BF16),
        grid=(b,),
        in_specs=[pl.BlockSpec((None, t, w), lambda bb: (bb, 0, 0))],
        out_specs=pl.BlockSpec((None, nbp, w), lambda bb: (bb, 0, 0)),
        compiler_params=_cparams(1),
        name="moba_kbar",
    )(bk)


def _moba_kernel(q_ref, k_ref, vt_ref, kbar_ref, o_ref, qt_ref, bias_ref, s_ref, mx_ref, m_ref, acc_ref, ot_ref):
    i = pl.program_id(1)
    nbp = kbar_ref.shape[0]
    blk = lax.broadcasted_iota(I32, (nbp, TQ), 0)
    blk_f = blk.astype(F32)
    own = 2 * i + (lax.broadcasted_iota(I32, (nbp, TQ), 1) >> (MOBA_BLOCK.bit_length() - 1))
    _masked_qt(q_ref[...].astype(F32) * (HEAD_DIM ** -0.5 * LOG2E), 6, N_HEADS, qt_ref)

    for h in range(N_HEADS):
        g = jnp.where(blk < own, jnp.dot(_half(kbar_ref[...], h, 6), qt_ref[h], preferred_element_type=F32), NEG)
        bias = jnp.full((nbp, TQ), NEG, F32)
        for _ in range(MOBA_TOPK):
            mx = jnp.max(g, axis=0, keepdims=True)
            first = jnp.min(jnp.where(g == mx, blk_f, 1e9), axis=0, keepdims=True)
            pick = jnp.logical_and(blk_f == first, mx > 0.5 * NEG)
            bias = jnp.where(pick, 0.0, bias)
            g = jnp.where(pick, NEG, g)
        bias_ref[h] = jnp.where(blk == own, 0.0, bias)

    def qk_all(c, lanes):
        kc = k_ref[pl.ds(pl.multiple_of(c * CK, CK), CK), :]
        return [jnp.dot(_half(kc, h, 6), qt_ref[h, :, lanes], preferred_element_type=F32) for h in range(N_HEADS)]

    _flash_loop(2 * i, qk_all, lambda c, h, s, lanes: s + bias_ref[h, pl.ds(c, 1), lanes],
                lambda c, h: vt_ref[c, h * VROWS:(h + 1) * VROWS, :], (s_ref, mx_ref, m_ref, acc_ref))
    for h in range(N_HEADS):
        ot_ref[h * HEAD_DIM:(h + 1) * HEAD_DIM, :] = _softmax_out(acc_ref.at[h])
    o_ref[...] = ot_ref[...].T.astype(o_ref.dtype)


def _moba(bq, bk, bvt, kbar):
    b, t, w = bq.shape
    assert TQ == 2 * MOBA_BLOCK and CK == MOBA_BLOCK and t % TQ == 0
    nbp = kbar.shape[1]
    qspec = pl.BlockSpec((None, TQ, w), lambda bb, i: (bb, i, 0))
    kspec, vspec = _kv_specs(t, w)
    return pl.pallas_call(
        _moba_kernel,
        out_shape=jax.ShapeDtypeStruct((b, t, w), BF16),
        grid=(b, t // TQ),
        in_specs=[qspec, kspec, vspec, pl.BlockSpec((None, nbp, w), lambda bb, i: (bb, 0, 0))],
        out_specs=qspec,
        scratch_shapes=[pltpu.VMEM((N_HEADS, LANES, TQ), BF16), pltpu.VMEM((N_HEADS, nbp, TQ), F32)]
        + _attn_scratch(N_HEADS),
        compiler_params=_cparams(2),
        name="moba",
    )(bq, bk, bvt, kbar)


def _diff_kernel(q_ref, k_ref, vt_ref, lam_ref, norm_ref, misc_ref, o_ref,
                 qt_ref, s_ref, mx_ref, m_ref, acc_ref, ot_ref):
    i = pl.program_id(1)
    _masked_qt(q_ref[...].astype(F32) * (DIFF_DIM ** -0.5 * LOG2E), 5, 2 * N_HEADS, qt_ref)

    dl = lam_ref[...]
    lam_init = misc_ref[0:1, 0:1]
    lam = (jnp.exp(jnp.sum(dl[0:1, :] * dl[1:2, :], axis=1, keepdims=True))
           - jnp.exp(jnp.sum(dl[2:3, :] * dl[3:4, :], axis=1, keepdims=True)) + lam_init)

    def qk_all(c, lanes):
        kc = k_ref[pl.ds(pl.multiple_of(c * CK, CK), CK), :]
        return [jnp.dot(_half(kc, j, 5), qt_ref[j, :, lanes], preferred_element_type=F32) for j in range(2 * N_HEADS)]

    _flash_loop(2 * i, qk_all, None,
                lambda c, j: vt_ref[c, (j // 2) * VROWS:(j // 2 + 1) * VROWS, :],
                (s_ref, mx_ref, m_ref, acc_ref))

    post = norm_ref[...] * (1.0 - lam_init)
    for h in range(N_HEADS):
        o_h = _softmax_out(acc_ref.at[2 * h]) - lam * _softmax_out(acc_ref.at[2 * h + 1])
        ms = jnp.mean(o_h * o_h, axis=0, keepdims=True)
        ot_ref[h * HEAD_DIM:(h + 1) * HEAD_DIM, :] = o_h * lax.rsqrt(ms + RMS_EPS) * post
    o_ref[...] = ot_ref[...].T.astype(o_ref.dtype)


def _diff(cq, ck, cvt, lam, norm, misc):
    b, t, w = cq.shape
    qspec = pl.BlockSpec((None, TQ, w), lambda bb, i: (bb, i, 0))
    kspec, vspec = _kv_specs(t, w)
    full = lambda a: pl.BlockSpec(a.shape, lambda bb, i: (0,) * a.ndim)
    return pl.pallas_call(
        _diff_kernel,
        out_shape=jax.ShapeDtypeStruct((b, t, w), BF16),
        grid=(b, t // TQ),
        in_specs=[qspec, kspec, vspec, full(lam), full(norm), full(misc)],
        out_specs=qspec,
        scratch_shapes=[pltpu.VMEM((2 * N_HEADS, LANES, TQ), BF16)] + _attn_scratch(2 * N_HEADS),
        compiler_params=_cparams(2),
        name="diff",
    )(cq, ck, cvt, lam, norm, misc)


def _mla_prep_kernel(cq_ref, ckv_ref, kr_ref, qn_ref, kvn_ref, wq_ref, wqr_ref, wk_ref, wvt_ref,
                     p_ref, ct_ref, st_ref, q_out, k_out, vt_out):
    x = cq_ref[...].astype(F32)
    xn = (x * lax.rsqrt(jnp.mean(x * x, axis=1, keepdims=True) + RMS_EPS) * qn_ref[...]).astype(BF16)
    q = (jnp.dot(xn, wq_ref[...], preferred_element_type=F32) * ct_ref[...]
         + jnp.dot(xn, wqr_ref[...], preferred_element_type=F32) * st_ref[...])
    q_out[...] = q.astype(q_out.dtype)
    c = ckv_ref[:, :KV_LORA].astype(F32)
    cn = (c * lax.rsqrt(jnp.mean(c * c, axis=1, keepdims=True) + RMS_EPS) * kvn_ref[...]).astype(BF16)
    k = (jnp.dot(cn, wk_ref[...], preferred_element_type=F32)
         + jnp.dot(kr_ref[...], p_ref[...], preferred_element_type=F32))
    k_out[...] = k.astype(k_out.dtype)
    _store_vt(vt_out, _tn_dot(wvt_ref[...], cn))


def _mla_prep(dcq, ckv, kr, qn, kvn, wq, wqr, wk, wvt, pmat, ct, st, l):
    b, t, _ = dcq.shape
    tm = 2048
    hw = N_HEADS * LANES
    row = lambda w: pl.BlockSpec((None, tm, w), lambda i, bb: (bb, i, 0))
    full = lambda a: pl.BlockSpec(a.shape, lambda i, bb: (0,) * a.ndim)
    tab = pl.BlockSpec((tm, hw), lambda i, bb: (i, 0))
    return pl.pallas_call(
        _mla_prep_kernel,
        out_shape=(jax.ShapeDtypeStruct((b, t, hw), BF16), jax.ShapeDtypeStruct((b, t, hw), BF16),
                   jax.ShapeDtypeStruct((b, t // CK, N_HEADS * VROWS, CK), BF16)),
        grid=(t // tm, b),
        in_specs=[row(Q_LORA), row(MXU_N), row(MXU_N), full(qn), full(kvn), _layer_spec(wq, l), _layer_spec(wqr, l),
                  _layer_spec(wk, l), _layer_spec(wvt, l), full(pmat), tab, tab],
        out_specs=(row(hw), row(hw),
                   pl.BlockSpec((None, tm // CK, N_HEADS * VROWS, CK), lambda i, bb: (bb, i, 0, 0))),
        compiler_params=_cparams(2),
        name="mla_prep",
    )(dcq, ckv, kr, qn, kvn, wq, wqr, wk, wvt, pmat, ct, st)


def _mla_kernel(q_ref, k_ref, vt_ref, o_ref, qt_ref, s_ref, mx_ref, m_ref, acc_ref, ot_ref):
    i = pl.program_id(1)
    hs = [slice(h * LANES, (h + 1) * LANES) for h in range(N_HEADS)]
    for h in range(N_HEADS):
        qt_ref[h] = q_ref[:, hs[h]].astype(F32).T.astype(BF16)

    def qk_all(c, lanes):
        start = pl.multiple_of(c * CK, CK)
        return [jnp.dot(k_ref[pl.ds(start, CK), hs[h]], qt_ref[h, :, lanes], preferred_element_type=F32)
                for h in range(N_HEADS)]

    _flash_loop(2 * i, qk_all, None,
                lambda c, h: vt_ref[c, h * VROWS:(h + 1) * VROWS, :],
                (s_ref, mx_ref, m_ref, acc_ref))
    for h in range(N_HEADS):
        ot_ref[h * HEAD_DIM:(h + 1) * HEAD_DIM, :] = _softmax_out(acc_ref.at[h])
    o_ref[...] = ot_ref[...].T.astype(o_ref.dtype)


def _mla(qm, km, vmt):
    b, t, hw = qm.shape
    kspec, vspec = _kv_specs(t, hw)
    return pl.pallas_call(
        _mla_kernel,
        out_shape=jax.ShapeDtypeStruct((b, t, BRANCH_W), BF16),
        grid=(b, t // TQ),
        in_specs=[pl.BlockSpec((None, TQ, hw), lambda bb, i: (bb, i, 0)), kspec, vspec],
        out_specs=pl.BlockSpec((None, TQ, BRANCH_W), lambda bb, i: (bb, i, 0)),
        scratch_shapes=[pltpu.VMEM((N_HEADS, LANES, TQ), BF16)] + _attn_scratch(N_HEADS),
        compiler_params=_cparams(2),
        name="mla",
    )(qm, km, vmt)


def _mem_kv_kernel(x_ref, w_ref, k_ref, vt_ref):
    x = x_ref[...].astype(BF16)
    k_ref[...] = jnp.dot(x, w_ref[:, :BRANCH_W], preferred_element_type=F32).astype(k_ref.dtype)
    _store_vt(vt_ref, _tn_dot(w_ref[:, BRANCH_W:], x))


def _mem_kv(mem, w, l):
    b, m, d = mem.shape
    assert m % CK == 0
    return pl.pallas_call(
        _mem_kv_kernel,
        out_shape=(jax.ShapeDtypeStruct((b, m, BRANCH_W), BF16),
                   jax.ShapeDtypeStruct((b, m // CK, N_HEADS * VROWS, CK), BF16)),
        grid=(b,),
        in_specs=[pl.BlockSpec((None, m, d), lambda bb: (bb, 0, 0)), _layer_spec(w, l)],
        out_specs=(pl.BlockSpec((None, m, BRANCH_W), lambda bb: (bb, 0, 0)),
                   pl.BlockSpec((None, m // CK, N_HEADS * VROWS, CK), lambda bb: (bb, 0, 0, 0))),
        compiler_params=_cparams(1),
        name="mem_kv",
    )(mem, w)


def _mem_kernel(q_ref, k_ref, vt_ref, o_ref, qt_ref, ot_ref):
    _masked_qt(q_ref[...].astype(F32) * (HEAD_DIM ** -0.5 * LOG2E), 6, N_HEADS, qt_ref)
    s_all = [jnp.dot(_half(k_ref[...], h, 6), qt_ref[h], preferred_element_type=F32) for h in range(N_HEADS)]
    for h in range(N_HEADS):
        s_t = s_all[h]
        p = jnp.exp2(s_t - jnp.max(s_t, axis=0, keepdims=True)).astype(BF16)
        acc = jnp.dot(vt_ref[0, h * VROWS:(h + 1) * VROWS, :], p, preferred_element_type=F32)
        ot_ref[h * HEAD_DIM:(h + 1) * HEAD_DIM, :] = acc[:HEAD_DIM, :] / acc[HEAD_DIM:HEAD_DIM + 1, :]
    o_ref[...] = ot_ref[...].T.astype(o_ref.dtype)


def _mem_attn(eq, mk, mvt):
    b, t, w = eq.shape
    m = mk.shape[1]
    assert m == CK
    tm = 2048
    return pl.pallas_call(
        _mem_kernel,
        out_shape=jax.ShapeDtypeStruct((b, t, w), BF16),
        grid=(b, t // tm),
        in_specs=[pl.BlockSpec((None, tm, w), lambda bb, i: (bb, i, 0)),
                  pl.BlockSpec((None, m, w), lambda bb, i: (bb, 0, 0)),
                  pl.BlockSpec((None,) + mvt.shape[1:], lambda bb, i: (bb, 0, 0, 0))],
        out_specs=pl.BlockSpec((None, tm, w), lambda bb, i: (bb, i, 0)),
        scratch_shapes=[pltpu.VMEM((N_HEADS, LANES, tm), BF16), pltpu.VMEM((BRANCH_W, tm), F32)],
        compiler_params=_cparams(2),
        name="mem_attn",
    )(eq, mk, mvt)


def _final_kernel(h_ref, hb_ref, oa_ref, ob_ref, oc_ref, od_ref, oe_ref, z_ref,
                  wg_ref, wb_ref, wo_ref, g_ref, b_ref, h_out, hb_out, acc_ref, *, alpha):
    d = h_ref.shape[1]
    n_blk = h_ref.shape[0] // MXU_N
    for n, o_ref in enumerate((oa_ref, ob_ref, oc_ref, od_ref, oe_ref)):
        for r in range(n_blk):
            rows = slice(r * MXU_N, (r + 1) * MXU_N)
            z = z_ref[rows, n * BRANCH_W:(n + 1) * BRANCH_W].astype(F32)
            y = o_ref[rows, :].astype(F32) * (z / (1.0 + jnp.exp(-z)))
            u = jnp.dot(y.astype(BF16), wb_ref[n], preferred_element_type=F32)
            g = jnp.dot(hb_ref[rows, :], wg_ref[:, n * d:(n + 1) * d], preferred_element_type=F32)
            t = u / (1.0 + jnp.exp(-g))
            acc_ref[rows, :] = t if n == 0 else acc_ref[rows, :] + t
    for r in range(n_blk):
        rows = slice(r * MXU_N, (r + 1) * MXU_N)
        out = jnp.dot(acc_ref[rows, :].astype(BF16), wo_ref[...], preferred_element_type=F32)
        x = alpha * h_ref[rows, :] + out
        mu = jnp.mean(x, axis=1, keepdims=True)
        xc = x - mu
        var = jnp.mean(xc * xc, axis=1, keepdims=True)
        y = xc * lax.rsqrt(var + LN_EPS) * g_ref[...] + b_ref[...]
        h_out[rows, :] = y
        hb_out[rows, :] = y.astype(BF16)


def _final(h, hb, os5, z, wg, wb, wo, ln_g, ln_b, alpha, l):
    n, d = h.shape
    tm = 1024
    row = lambda w: pl.BlockSpec((tm, w), lambda i: (i, 0))
    full = lambda a: pl.BlockSpec(a.shape, lambda i: (0,) * a.ndim)
    return pl.pallas_call(
        functools.partial(_final_kernel, alpha=alpha),
        out_shape=(jax.ShapeDtypeStruct((n, d), F32), jax.ShapeDtypeStruct((n, d), BF16)),
        grid=(n // tm,),
        in_specs=[row(d), row(d)] + [row(BRANCH_W)] * N_BRANCH + [row(N_BRANCH * BRANCH_W),
                  _layer_spec(wg, l), _layer_spec(wb, l), _layer_spec(wo, l), full(ln_g), full(ln_b)],
        out_specs=(row(d), row(d)),
        scratch_shapes=[pltpu.VMEM((tm, d), F32)],
        compiler_params=_cparams(1),
        name="merge_out_ln",
    )(h, hb, *os5, z, wg, wb, wo, ln_g, ln_b)


ROPE_GROUPS = (("a_q", N_HEADS, HEAD_DIM, ROT_64), ("a_k", N_HEADS, HEAD_DIM, ROT_64),
               ("i_q", IDX_HEADS, IDX_DIM, ROT_32), ("i_k", 1, MXU_N, ROT_32),
               ("b_q", N_HEADS, HEAD_DIM, ROT_64), ("b_k", N_HEADS, HEAD_DIM, ROT_64),
               ("c_q", 2 * N_HEADS, DIFF_DIM, ROT_32), ("c_k", 2 * N_HEADS, DIFF_DIM, ROT_32),
               ("d_kr", 1, MXU_N, MLA_ROPE))
PLAIN_COLS = ("d_cq", "d_ckv", "e_q") + tuple(("z", j) for j in range(N_BRANCH))
VALUE_COLS = ("a_v", "b_v", "c_v")
GATE_WINDOW = 4 * MXU_N
GATE_COLS = tuple(("g", j) for j in range(0, OFF["g"][1] // MXU_N, GATE_WINDOW // MXU_N))


def _window_start(col):
    name, j = col if isinstance(col, tuple) else (col, 0)
    return OFF[name][0] + j * MXU_N


def _weight_prep_kernel(wt_hbm, o_ref, buf, sem, *, starts, width):
    l = pl.program_id(0)
    copies = [pltpu.make_async_copy(wt_hbm.at[l, pl.ds(st, width), :], buf.at[j], sem.at[j])
              for j, st in enumerate(starts)]
    for cp in copies:
        cp.start()
    for j, cp in enumerate(copies):
        cp.wait()
        o_ref[:, j * width:(j + 1) * width] = buf[j].T.astype(o_ref.dtype)


def _weight_windows(wt, cols, name, width=MXU_N):
    depth, n, d = wt.shape
    starts = tuple(_window_start(c) for c in cols)
    assert all(st % SUBLANES == 0 and st + width <= n for st in starts)
    return pl.pallas_call(
        functools.partial(_weight_prep_kernel, starts=starts, width=width),
        out_shape=jax.ShapeDtypeStruct((depth, d, width * len(cols)), BF16),
        grid=(depth,),
        in_specs=[pl.BlockSpec(memory_space=pl.ANY)],
        out_specs=pl.BlockSpec((None, d, width * len(cols)), lambda l: (l, 0, 0)),
        scratch_shapes=[pltpu.VMEM((len(cols), width, d), F32), pltpu.SemaphoreType.DMA((len(cols),))],
        compiler_params=_cparams(1),
        name=name,
    )(wt)


def _weight_prep(w_in):
    wt = jnp.swapaxes(w_in, 1, 2)
    return (_weight_windows(wt, PLAIN_COLS, "wprep_plain"), _weight_windows(wt, VALUE_COLS, "wprep_value"),
            _weight_windows(wt, [name for name, *_ in ROPE_GROUPS], "wprep_rope"),
            _weight_windows(wt, GATE_COLS, "wprep_gate", GATE_WINDOW))


def _rope_tables(seq, rot_dim):
    pos = jnp.arange(seq, dtype=F32)
    inv = ROPE_THETA ** (-jnp.arange(0, rot_dim, 2, dtype=F32) / rot_dim)
    ang = pos[:, None] * inv[None, :]
    return jnp.cos(ang), jnp.sin(ang)


def _rope_cs(t, nh, hd, r):
    cos, sin = _rope_tables(t, r)
    c = jnp.concatenate([cos, cos, jnp.ones((t, hd - r), F32)], axis=1)
    s = jnp.concatenate([-sin, sin, jnp.zeros((t, hd - r), F32)], axis=1)
    return jnp.tile(c, (1, nh)), jnp.tile(s, (1, nh))


def kernel(x, mem, ln0_g, ln0_b, w_in, mla_q_norm, w_uq, mla_kv_norm, w_ukv, diff_lam, diff_norm,
           w_mem_kv, w_branch, w_out, ln_g, ln_b):
    b, t, d = x.shape
    depth = w_in.shape[0]
    alpha = (2 * depth) ** 0.25
    assert t % 512 == 0 and d == 1024

    w_plain, w_vt, w_rope, wg = _weight_prep(w_in)
    plain_widths = (BRANCH_W,) * 3 + (N_BRANCH * BRANCH_W,)
    rope_heads = tuple((hd, r // 2) for _, _, hd, r in ROPE_GROUPS)
    patterns = sorted(set((nh, hd, r) for _, nh, hd, r in ROPE_GROUPS))
    rope_tables = tuple(patterns.index((nh, hd, r)) for _, nh, hd, r in ROPE_GROUPS)
    cs = [_rope_cs(t, nh, hd, r) for nh, hd, r in patterns]
    ctab = jnp.stack([c for c, _ in cs])
    stab = jnp.stack([s for _, s in cs])

    uq = w_uq.reshape(depth, Q_LORA, N_HEADS, MLA_NOPE + MLA_ROPE)
    qn_w, qr_w = uq[..., :MLA_NOPE], uq[..., MLA_NOPE:]
    pad32 = jnp.zeros((depth, Q_LORA, N_HEADS, LANES - MLA_NOPE - MLA_ROPE), w_uq.dtype)
    hw = N_HEADS * LANES
    wq = jnp.concatenate([qn_w, qr_w, pad32], axis=-1).reshape(depth, Q_LORA, hw).astype(BF16)
    half = MLA_ROPE // 2
    wq_rot = jnp.concatenate([jnp.zeros_like(qn_w), -qr_w[..., half:], qr_w[..., :half], pad32],
                             axis=-1).reshape(depth, Q_LORA, hw).astype(BF16)
    cos_m, sin_m = _rope_tables(t, MLA_ROPE)
    one = lambda n: jnp.ones((t, n), F32)
    zer = lambda n: jnp.zeros((t, n), F32)
    qs = (MLA_NOPE + MLA_ROPE) ** -0.5 * LOG2E
    ct_q = qs * jnp.tile(jnp.concatenate([one(MLA_NOPE), cos_m, cos_m, one(LANES - MLA_NOPE - MLA_ROPE)], axis=1), (1, N_HEADS))
    st_q = qs * jnp.tile(jnp.concatenate([zer(MLA_NOPE), sin_m, sin_m, zer(LANES - MLA_NOPE - MLA_ROPE)], axis=1), (1, N_HEADS))
    ukv = w_ukv.reshape(depth, KV_LORA, N_HEADS, MLA_NOPE + MLA_V)
    wk = jnp.concatenate([ukv[..., :MLA_NOPE], jnp.zeros((depth, KV_LORA, N_HEADS, LANES - MLA_NOPE), w_ukv.dtype)],
                         axis=-1).reshape(depth, KV_LORA, hw).astype(BF16)
    wvt = ukv[..., MLA_NOPE:].reshape(depth, KV_LORA, N_HEADS * MLA_V).astype(BF16)
    place = np.zeros((MXU_N, hw), np.float32)
    for hh in range(N_HEADS):
        for j in range(MLA_ROPE):
            place[j, hh * LANES + MLA_NOPE + j] = 1.0
    place = jnp.asarray(place, BF16)

    wb = w_branch.astype(BF16)
    wo = w_out.astype(BF16)
    wmem = w_mem_kv.astype(BF16)
    norm_t = jnp.broadcast_to(diff_norm.astype(F32)[:, :, None], (depth, HEAD_DIM, TQ))

    h, hb = _layer_norm0(x.reshape(b * t, d), ln0_g, ln0_b)
    for l in range(depth):
        hb3 = hb.reshape(b, t, d)
        avt, bvt, cvt, dcq, ckv_iw, eq, z = _proj_plain(hb3, w_plain, w_vt, plain_widths, l)
        aq, ak, iq, ik, bq, bk, cq, ck, kr = _proj_rope(hb3, w_rope, ctab, stab, rope_heads, rope_tables, l)

        o_a = _dsa(aq, ak, avt, iq, ik, ik)
        o_b = _moba(bq, bk, bvt, _kbar(bk))
        lam_init = 0.8 - 0.6 * math.exp(-0.3 * l)
        misc = jnp.full((SUBLANES, LANES), lam_init, F32)
        o_c = _diff(cq, ck, cvt, diff_lam[l].astype(F32), norm_t[l], misc)
        qm, km, vmt = _mla_prep(dcq, ckv_iw, kr, mla_q_norm[l].reshape(1, Q_LORA), mla_kv_norm[l].reshape(1, KV_LORA),
                                wq, wq_rot, wk, wvt, place, ct_q, st_q, l)
        o_d = _mla(qm, km, vmt)
        o_e = _mem_attn(eq, *_mem_kv(mem, wmem, l))

        os5 = [o.reshape(b * t, BRANCH_W) for o in (o_a, o_b, o_c, o_d, o_e)]
        h, hb = _final(h, hb, os5, z.reshape(b * t, N_BRANCH * BRANCH_W), wg, wb, wo,
                       ln_g[l].reshape(1, d), ln_b[l].reshape(1, d), alpha, l)
    return h.reshape(b, t, d)
```
